```python
import math
import jax, jax.numpy as jnp
from jax import lax
import numpy as np

D_MODEL = 1024
BATCH = 8
SEQ = 8192
DEPTH = 1

CHUNK = 64
D_PLE = 256
D_FF = 2816
RET_HEADS = 8
RET_DK = 128
RET_DV = 256
RET_QK = RET_HEADS * RET_DK
RET_V = RET_HEADS * RET_DV
MLA_HEADS = 8
MLA_NOPE = 128
MLA_ROPE = 64
MLA_DV = 128
Q_LORA = 256
KV_LORA = 256
Q_BLOCK = 128
ROPE_BASE = 10000.0
EPS = 1e-5
N_LN = 4
DEEPNORM_ALPHA = (2.0 * DEPTH) ** 0.25
DEEPNORM_BETA = (8.0 * DEPTH) ** -0.25
SPLITS = (RET_QK, RET_QK, RET_V, RET_V, Q_LORA, KV_LORA, MLA_ROPE, D_MODEL, D_MODEL)
D_IN_TOTAL = sum(SPLITS)

kernel_name = "hybrid_retention_mla_macaron_deepnorm"


def layer_norm(x, g, b):
    xf = x.astype(jnp.float32)
    mu = jnp.mean(xf, axis=-1, keepdims=True)
    var = jnp.mean(jnp.square(xf - mu), axis=-1, keepdims=True)
    y = (xf - mu) * lax.rsqrt(var + EPS)
    return (y * g.astype(jnp.float32) + b.astype(jnp.float32)).astype(x.dtype)


def rms_norm(x, g):
    xf = x.astype(jnp.float32)
    y = xf * lax.rsqrt(jnp.mean(jnp.square(xf), axis=-1, keepdims=True) + EPS)
    return (y * g.astype(jnp.float32)).astype(x.dtype)


def rope(t, positions):
    half = t.shape[-1] // 2
    inv_freq = ROPE_BASE ** (-jnp.arange(half, dtype=jnp.float32) / half)
    ang = positions.astype(jnp.float32)[:, :, None] * inv_freq
    cos = jnp.cos(ang)[:, :, None, :]
    sin = jnp.sin(ang)[:, :, None, :]
    t1 = t[..., :half].astype(jnp.float32)
    t2 = t[..., half:].astype(jnp.float32)
    return jnp.concatenate([t1 * cos - t2 * sin, t2 * cos + t1 * sin], axis=-1).astype(t.dtype)


def swiglu_ffn(x, w_in, w_out):
    g, u = jnp.split(x @ w_in, 2, axis=-1)
    return (jax.nn.silu(g) * u) @ w_out


def chunk_retention(q, k, v):
    B, S, H, dk = q.shape
    dv = v.shape[-1]
    n_chunks = S // CHUNK
    log_gamma = jnp.log(1.0 - 2.0 ** (-5.0 - jnp.arange(H, dtype=jnp.float32)))
    idx = jnp.arange(CHUNK, dtype=jnp.float32)
    intra_decay = jnp.exp(log_gamma[:, None, None] * jnp.abs(idx[:, None] - idx[None, :]))
    xi = jnp.exp(log_gamma[:, None] * (idx + 1.0))
    zeta = jnp.exp(log_gamma[:, None] * (CHUNK - 1.0 - idx))
    chunk_decay = jnp.exp(log_gamma * CHUNK)

    qc = q.reshape(B, n_chunks, CHUNK, H, dk)
    kc = k.reshape(B, n_chunks, CHUNK, H, dk)
    vc = v.reshape(B, n_chunks, CHUNK, H, dv)

    scores = jnp.einsum('bnchd,bnshd->bnhcs', qc, kc) * intra_decay
    y_intra = jnp.einsum('bnhcs,bnshe->bnche', scores, vc)

    xi_ch = xi.T[None, :, :, None]

    def step(state, inp):
        q_n, k_n, v_n = inp
        cross = jnp.einsum('bchd,bhde->bche', q_n, state) * xi_ch
        state = state * chunk_decay[None, :, None, None] + jnp.einsum('bchd,bche,hc->bhde', k_n, v_n, zeta)
        return state, cross

    s0 = jnp.zeros((B, H, dk, dv), jnp.float32)
    xs = (jnp.moveaxis(qc, 1, 0), jnp.moveaxis(kc, 1, 0), jnp.moveaxis(vc, 1, 0))
    _, y_cross = lax.scan(step, s0, xs)
    y = y_intra + jnp.moveaxis(y_cross, 0, 1)
    return y.reshape(B, S, H, dv)


def mla_block_attention(q_nope, q_pe, k_nope, k_pe, v):
    B, S, H, _ = q_nope.shape
    n_blocks = S // Q_BLOCK
    scale = (MLA_NOPE + MLA_ROPE) ** -0.5
    key_chunk = jnp.arange(S) // CHUNK

    def to_blocks(t):
        return jnp.moveaxis(t.reshape(B, n_blocks, Q_BLOCK, *t.shape[2:]), 1, 0)

    def one_block(args):
        blk, qn, qp = args
        s = (jnp.einsum('bqhd,bkhd->bhqk', qn, k_nope)
             + jnp.einsum('bqhr,bkr->bhqk', qp, k_pe)).astype(jnp.float32) * scale
        q_chunk = (blk * Q_BLOCK + jnp.arange(Q_BLOCK)) // CHUNK
        mask = key_chunk[None, :] <= q_chunk[:, None]
        s = jnp.where(mask[None, None], s, -jnp.inf)
        probs = jax.nn.softmax(s, axis=-1).astype(v.dtype)
        return jnp.einsum('bhqk,bkhe->bqhe', probs, v)

    o = lax.map(one_block, (jnp.arange(n_blocks), to_blocks(q_nope), to_blocks(q_pe)))
    return jnp.moveaxis(o, 0, 1).reshape(B, S, H * v.shape[-1])


def hybrid_mixer(h, positions, w_in, ret_gn_g, w_ret_o, q_norm_g, kv_norm_g,
                 w_uq, w_ukv, w_mla_o, w_out):
    B, S, _ = h.shape
    proj = h @ w_in
    (r_q, r_k, r_v, r_g, c_q, c_kv, k_pe_raw, gate_ret, gate_mla) = jnp.split(
        proj, np.cumsum(SPLITS)[:-1], axis=-1)

    rq = rope(r_q.reshape(B, S, RET_HEADS, RET_DK), positions)
    rk = rope(r_k.reshape(B, S, RET_HEADS, RET_DK), positions) * (RET_DK ** -0.5)
    rv = r_v.reshape(B, S, RET_HEADS, RET_DV)
    y = chunk_retention(rq.astype(jnp.float32), rk.astype(jnp.float32), rv.astype(jnp.float32))
    mu = jnp.mean(y, axis=-1, keepdims=True)
    var = jnp.mean(jnp.square(y - mu), axis=-1, keepdims=True)
    y = ((y - mu) * lax.rsqrt(var + EPS)).reshape(B, S, RET_V) * ret_gn_g.astype(jnp.float32)
    y = (jax.nn.silu(r_g.astype(jnp.float32)) * y).astype(h.dtype)
    y_ret = y @ w_ret_o

    q = (rms_norm(c_q, q_norm_g) @ w_uq).reshape(B, S, MLA_HEADS, MLA_NOPE + MLA_ROPE)
    q_nope, q_pe = q[..., :MLA_NOPE], rope(q[..., MLA_NOPE:], positions)
    kv = (rms_norm(c_kv, kv_norm_g) @ w_ukv).reshape(B, S, MLA_HEADS, MLA_NOPE + MLA_DV)
    k_nope, v = kv[..., :MLA_NOPE], kv[..., MLA_NOPE:]
    k_pe = rope(k_pe_raw[:, :, None, :], positions)[:, :, 0, :]
    y_mla = mla_block_attention(q_nope, q_pe, k_nope, k_pe, v) @ w_mla_o

    mix = jax.nn.sigmoid(gate_ret) * y_ret + jax.nn.sigmoid(gate_mla) * y_mla
    return mix @ w_out


def _fwd_setup_inputs(seed: int = 0) -> dict:
    key = jax.random.key(seed)
    ks = jax.random.split(key, 24)
    f32 = jnp.float32

    def nrm(k, shape, scale):
        return jax.random.normal(k, shape, f32) * scale

    offset = jax.random.randint(ks[2], (BATCH, 1), 0, 4096, dtype=jnp.int32)
    positions = offset + jnp.arange(SEQ, dtype=jnp.int32)[None, :]
    return {
        "x": nrm(ks[0], (BATCH, SEQ, D_MODEL), 1.0),
        "p": nrm(ks[1], (DEPTH, BATCH, SEQ, D_PLE), 1.0),
        "positions": positions,
        "ln_g": 1.0 + nrm(ks[3], (DEPTH, N_LN, D_MODEL), 0.02),
        "ln_b": nrm(ks[4], (DEPTH, N_LN, D_MODEL), 0.02),
        "ffn1_w_in": nrm(ks[5], (DEPTH, D_MODEL, 2 * D_FF), D_MODEL ** -0.5),
        "ffn1_w_out": nrm(ks[6], (DEPTH, D_FF, D_MODEL), DEEPNORM_BETA * D_FF ** -0.5),
        "w_in": nrm(ks[7], (DEPTH, D_MODEL, D_IN_TOTAL), D_MODEL ** -0.5),
        "ret_gn_g": 1.0 + nrm(ks[8], (DEPTH, RET_V), 0.02),
        "w_ret_o": nrm(ks[9], (DEPTH, RET_V, D_MODEL), DEEPNORM_BETA * RET_V ** -0.5),
        "q_norm_g": 1.0 + nrm(ks[10], (DEPTH, Q_LORA), 0.02),
        "kv_norm_g": 1.0 + nrm(ks[11], (DEPTH, KV_LORA), 0.02),
        "w_uq": nrm(ks[12], (DEPTH, Q_LORA, MLA_HEADS * (MLA_NOPE + MLA_ROPE)), Q_LORA ** -0.5),
        "w_ukv": nrm(ks[13], (DEPTH, KV_LORA, MLA_HEADS * (MLA_NOPE + MLA_DV)), KV_LORA ** -0.5),
        "w_mla_o": nrm(ks[14], (DEPTH, MLA_HEADS * MLA_DV, D_MODEL), DEEPNORM_BETA * (MLA_HEADS * MLA_DV) ** -0.5),
        "w_out": nrm(ks[15], (DEPTH, D_MODEL, D_MODEL), DEEPNORM_BETA * D_MODEL ** -0.5),
        "ffn2_w_in": nrm(ks[16], (DEPTH, D_MODEL, 2 * D_FF), D_MODEL ** -0.5),
        "ffn2_w_out": nrm(ks[17], (DEPTH, D_FF, D_MODEL), DEEPNORM_BETA * D_FF ** -0.5),
        "ple_w_gate": nrm(ks[18], (DEPTH, D_MODEL, D_MODEL), D_MODEL ** -0.5),
        "ple_w_proj": nrm(ks[19], (DEPTH, D_PLE, D_MODEL), DEEPNORM_BETA * D_PLE ** -0.5),
    }


def _fwd_reference(x, p, positions, ln_g, ln_b, ffn1_w_in, ffn1_w_out, w_in, ret_gn_g,
              w_ret_o, q_norm_g, kv_norm_g, w_uq, w_ukv, w_mla_o, w_out,
              ffn2_w_in, ffn2_w_out, ple_w_gate, ple_w_proj):
    h = x
    for i in range(DEPTH):
        h = layer_norm(DEEPNORM_ALPHA * h + 0.5 * swiglu_ffn(h, ffn1_w_in[i], ffn1_w_out[i]),
                       ln_g[i, 0], ln_b[i, 0])
        mixed = hybrid_mixer(h, positions, w_in[i], ret_gn_g[i], w_ret_o[i], q_norm_g[i],
                             kv_norm_g[i], w_uq[i], w_ukv[i], w_mla_o[i], w_out[i])
        h = layer_norm(DEEPNORM_ALPHA * h + mixed, ln_g[i, 1], ln_b[i, 1])
        h = layer_norm(DEEPNORM_ALPHA * h + 0.5 * swiglu_ffn(h, ffn2_w_in[i], ffn2_w_out[i]),
                       ln_g[i, 2], ln_b[i, 2])
        ple = jax.nn.sigmoid(h @ ple_w_gate[i]) * (p[i] @ ple_w_proj[i])
        h = layer_norm(DEEPNORM_ALPHA * h + ple, ln_g[i, 3], ln_b[i, 3])
    return h


import jax as _jax
import jax.numpy as _jnp

TWIN_FORMAT = 'train_step'
FWD_PARAMS = ['x', 'p', 'positions', 'ln_g', 'ln_b', 'ffn1_w_in', 'ffn1_w_out', 'w_in', 'ret_gn_g', 'w_ret_o', 'q_norm_g', 'kv_norm_g', 'w_uq', 'w_ukv', 'w_mla_o', 'w_out', 'ffn2_w_in', 'ffn2_w_out', 'ple_w_gate', 'ple_w_proj']
TWIN_WEIGHTS = ['ln_g', 'ln_b', 'ffn1_w_in', 'ffn1_w_out', 'w_in', 'ret_gn_g', 'w_ret_o', 'q_norm_g', 'kv_norm_g', 'w_uq', 'w_ukv', 'w_mla_o', 'w_out', 'ffn2_w_in', 'ffn2_w_out', 'ple_w_gate', 'ple_w_proj']
TWIN_DIFF_INPUT = 'x'
TWIN_INPUTS = ['x', 'p', 'positions', 'ln_g', 'ln_b', 'ffn1_w_in', 'ffn1_w_out', 'w_in', 'ret_gn_g', 'w_ret_o', 'q_norm_g', 'kv_norm_g', 'w_uq', 'w_ukv', 'w_mla_o', 'w_out', 'ffn2_w_in', 'ffn2_w_out', 'ple_w_gate', 'ple_w_proj', 'loss_target', 'm_ln_g', 'm_ln_b', 'm_ffn1_w_in', 'm_ffn1_w_out', 'm_w_in', 'm_ret_gn_g', 'm_w_ret_o', 'm_q_norm_g', 'm_kv_norm_g', 'm_w_uq', 'm_w_ukv', 'm_w_mla_o', 'm_w_out', 'm_ffn2_w_in', 'm_ffn2_w_out', 'm_ple_w_gate', 'm_ple_w_proj', 'v_ln_g', 'v_ln_b', 'v_ffn1_w_in', 'v_ffn1_w_out', 'v_w_in', 'v_ret_gn_g', 'v_w_ret_o', 'v_q_norm_g', 'v_kv_norm_g', 'v_w_uq', 'v_w_ukv', 'v_w_mla_o', 'v_w_out', 'v_ffn2_w_in', 'v_ffn2_w_out', 'v_ple_w_gate', 'v_ple_w_proj']
TWIN_OUTPUTS = ['loss', 'grad_x', 'grad_ln_g', 'grad_ln_b', 'grad_ffn1_w_in', 'grad_ffn1_w_out', 'grad_w_in', 'grad_ret_gn_g', 'grad_w_ret_o', 'grad_q_norm_g', 'grad_kv_norm_g', 'grad_w_uq', 'grad_w_ukv', 'grad_w_mla_o', 'grad_w_out', 'grad_ffn2_w_in', 'grad_ffn2_w_out', 'grad_ple_w_gate', 'grad_ple_w_proj', 'delta_ln_g', 'delta_ln_b', 'delta_ffn1_w_in', 'delta_ffn1_w_out', 'delta_w_in', 'delta_ret_gn_g', 'delta_w_ret_o', 'delta_q_norm_g', 'delta_kv_norm_g', 'delta_w_uq', 'delta_w_ukv', 'delta_w_mla_o', 'delta_w_out', 'delta_ffn2_w_in', 'delta_ffn2_w_out', 'delta_ple_w_gate', 'delta_ple_w_proj', 'new_m_ln_g', 'new_m_ln_b', 'new_m_ffn1_w_in', 'new_m_ffn1_w_out', 'new_m_w_in', 'new_m_ret_gn_g', 'new_m_w_ret_o', 'new_m_q_norm_g', 'new_m_kv_norm_g', 'new_m_w_uq', 'new_m_w_ukv', 'new_m_w_mla_o', 'new_m_w_out', 'new_m_ffn2_w_in', 'new_m_ffn2_w_out', 'new_m_ple_w_gate', 'new_m_ple_w_proj', 'new_v_ln_g', 'new_v_ln_b', 'new_v_ffn1_w_in', 'new_v_ffn1_w_out', 'new_v_w_in', 'new_v_ret_gn_g', 'new_v_w_ret_o', 'new_v_q_norm_g', 'new_v_kv_norm_g', 'new_v_w_uq', 'new_v_w_ukv', 'new_v_w_mla_o', 'new_v_w_out', 'new_v_ffn2_w_in', 'new_v_ffn2_w_out', 'new_v_ple_w_gate', 'new_v_ple_w_proj']
TWIN_LEAF_KINDS = {'loss': 'loss', 'grad_x': 'grad_x', 'grad_ln_g': 'grad_w', 'grad_ln_b': 'grad_w', 'grad_ffn1_w_in': 'grad_w', 'grad_ffn1_w_out': 'grad_w', 'grad_w_in': 'grad_w', 'grad_ret_gn_g': 'grad_w', 'grad_w_ret_o': 'grad_w', 'grad_q_norm_g': 'grad_w', 'grad_kv_norm_g': 'grad_w', 'grad_w_uq': 'grad_w', 'grad_w_ukv': 'grad_w', 'grad_w_mla_o': 'grad_w', 'grad_w_out': 'grad_w', 'grad_ffn2_w_in': 'grad_w', 'grad_ffn2_w_out': 'grad_w', 'grad_ple_w_gate': 'grad_w', 'grad_ple_w_proj': 'grad_w', 'delta_ln_g': 'delta_w', 'delta_ln_b': 'delta_w', 'delta_ffn1_w_in': 'delta_w', 'delta_ffn1_w_out': 'delta_w', 'delta_w_in': 'delta_w', 'delta_ret_gn_g': 'delta_w', 'delta_w_ret_o': 'delta_w', 'delta_q_norm_g': 'delta_w', 'delta_kv_norm_g': 'delta_w', 'delta_w_uq': 'delta_w', 'delta_w_ukv': 'delta_w', 'delta_w_mla_o': 'delta_w', 'delta_w_out': 'delta_w', 'delta_ffn2_w_in': 'delta_w', 'delta_ffn2_w_out': 'delta_w', 'delta_ple_w_gate': 'delta_w', 'delta_ple_w_proj': 'delta_w', 'new_m_ln_g': 'new_m', 'new_m_ln_b': 'new_m', 'new_m_ffn1_w_in': 'new_m', 'new_m_ffn1_w_out': 'new_m', 'new_m_w_in': 'new_m', 'new_m_ret_gn_g': 'new_m', 'new_m_w_ret_o': 'new_m', 'new_m_q_norm_g': 'new_m', 'new_m_kv_norm_g': 'new_m', 'new_m_w_uq': 'new_m', 'new_m_w_ukv': 'new_m', 'new_m_w_mla_o': 'new_m', 'new_m_w_out': 'new_m', 'new_m_ffn2_w_in': 'new_m', 'new_m_ffn2_w_out': 'new_m', 'new_m_ple_w_gate': 'new_m', 'new_m_ple_w_proj': 'new_m', 'new_v_ln_g': 'new_v', 'new_v_ln_b': 'new_v', 'new_v_ffn1_w_in': 'new_v', 'new_v_ffn1_w_out': 'new_v', 'new_v_w_in': 'new_v', 'new_v_ret_gn_g': 'new_v', 'new_v_w_ret_o': 'new_v', 'new_v_q_norm_g': 'new_v', 'new_v_kv_norm_g': 'new_v', 'new_v_w_uq': 'new_v', 'new_v_w_ukv': 'new_v', 'new_v_w_mla_o': 'new_v', 'new_v_w_out': 'new_v', 'new_v_ffn2_w_in': 'new_v', 'new_v_ffn2_w_out': 'new_v', 'new_v_ple_w_gate': 'new_v', 'new_v_ple_w_proj': 'new_v'}


def _forward(args):
    return _fwd_reference(*[args[k] for k in FWD_PARAMS])


def _output_shape():
    def fwd():
        inp = _fwd_setup_inputs(0)
        return _fwd_reference(*[inp[k] for k in FWD_PARAMS])
    out = _jax.eval_shape(fwd)
    return out.shape, out.dtype

N_MICROBATCH = 1
ADAM_LR = 0.001
ADAM_B1 = 0.9
ADAM_B2 = 0.999
ADAM_EPS = 1e-08
ADAM_WD = 0.01
ADAM_STEP = 10
PER_EXAMPLE_BATCH_AXIS = {'x': 0, 'p': 1, 'positions': 0, 'loss_target': 0}
SHARED_INPUTS = []
_WEIGHT_DTYPES = {'ln_g': _jnp.float32, 'ln_b': _jnp.float32, 'ffn1_w_in': _jnp.float32, 'ffn1_w_out': _jnp.float32, 'w_in': _jnp.float32, 'ret_gn_g': _jnp.float32, 'w_ret_o': _jnp.float32, 'q_norm_g': _jnp.float32, 'kv_norm_g': _jnp.float32, 'w_uq': _jnp.float32, 'w_ukv': _jnp.float32, 'w_mla_o': _jnp.float32, 'w_out': _jnp.float32, 'ffn2_w_in': _jnp.float32, 'ffn2_w_out': _jnp.float32, 'ple_w_gate': _jnp.float32, 'ple_w_proj': _jnp.float32}
MOMENT_SCALE = {'ln_g': 3.212526e+01, 'ln_b': 1.317006e+00, 'ffn1_w_in': 2.252928e-02, 'ffn1_w_out': 6.190374e-02, 'w_in': 1.695441e-02, 'ret_gn_g': 1.802927e-02, 'w_ret_o': 3.951707e-02, 'q_norm_g': 1.135364e-02, 'kv_norm_g': 1.715326e-02, 'w_uq': 4.708899e-03, 'w_ukv': 5.573020e-03, 'w_mla_o': 1.047159e-02, 'w_out': 4.075860e-02, 'ffn2_w_in': 2.200138e-02, 'ffn2_w_out': 6.050591e-02, 'ple_w_gate': 2.554403e-02, 'ple_w_proj': 1.102624e-01}


def _to_microbatches(a, axis):
    t = _jnp.moveaxis(a, axis, 0)
    t = t.reshape((N_MICROBATCH, t.shape[0] // N_MICROBATCH) + t.shape[1:])
    return _jnp.moveaxis(t, 1, axis + 1)


def setup_inputs(seed: int = 0) -> dict:
    inp = _fwd_setup_inputs(seed)
    key = _jax.random.fold_in(_jax.random.key(seed), 7919)
    shape, _ = _output_shape()
    out = dict(inp)
    out["loss_target"] = _jax.random.normal(_jax.random.fold_in(key, 0), shape, _jnp.float32)
    for i, name in enumerate(TWIN_WEIGHTS):
        w = inp[name].astype(_jnp.float32)
        if MOMENT_SCALE is None:
            s = _jnp.sqrt(_jnp.mean(_jnp.square(w)) + 1e-30)
        else:
            s = MOMENT_SCALE[name]
        km, kv = _jax.random.split(_jax.random.fold_in(key, i + 1))
        out[name] = w
        out["m_" + name] = s * _jax.random.normal(km, w.shape, _jnp.float32)
        out["v_" + name] = (s * s) * _jax.random.uniform(kv, w.shape, _jnp.float32, 0.5, 1.5)
    if N_MICROBATCH > 1:
        for name, axis in PER_EXAMPLE_BATCH_AXIS.items():
            out[name] = _to_microbatches(out[name], axis)
    return {'x': out['x'], 'p': out['p'], 'positions': out['positions'], 'ln_g': out['ln_g'], 'ln_b': out['ln_b'], 'ffn1_w_in': out['ffn1_w_in'], 'ffn1_w_out': out['ffn1_w_out'], 'w_in': out['w_in'], 'ret_gn_g': out['ret_gn_g'], 'w_ret_o': out['w_ret_o'], 'q_norm_g': out['q_norm_g'], 'kv_norm_g': out['kv_norm_g'], 'w_uq': out['w_uq'], 'w_ukv': out['w_ukv'], 'w_mla_o': out['w_mla_o'], 'w_out': out['w_out'], 'ffn2_w_in': out['ffn2_w_in'], 'ffn2_w_out': out['ffn2_w_out'], 'ple_w_gate': out['ple_w_gate'], 'ple_w_proj': out['ple_w_proj'], 'loss_target': out['loss_target'], 'm_ln_g': out['m_ln_g'], 'm_ln_b': out['m_ln_b'], 'm_ffn1_w_in': out['m_ffn1_w_in'], 'm_ffn1_w_out': out['m_ffn1_w_out'], 'm_w_in': out['m_w_in'], 'm_ret_gn_g': out['m_ret_gn_g'], 'm_w_ret_o': out['m_w_ret_o'], 'm_q_norm_g': out['m_q_norm_g'], 'm_kv_norm_g': out['m_kv_norm_g'], 'm_w_uq': out['m_w_uq'], 'm_w_ukv': out['m_w_ukv'], 'm_w_mla_o': out['m_w_mla_o'], 'm_w_out': out['m_w_out'], 'm_ffn2_w_in': out['m_ffn2_w_in'], 'm_ffn2_w_out': out['m_ffn2_w_out'], 'm_ple_w_gate': out['m_ple_w_gate'], 'm_ple_w_proj': out['m_ple_w_proj'], 'v_ln_g': out['v_ln_g'], 'v_ln_b': out['v_ln_b'], 'v_ffn1_w_in': out['v_ffn1_w_in'], 'v_ffn1_w_out': out['v_ffn1_w_out'], 'v_w_in': out['v_w_in'], 'v_ret_gn_g': out['v_ret_gn_g'], 'v_w_ret_o': out['v_w_ret_o'], 'v_q_norm_g': out['v_q_norm_g'], 'v_kv_norm_g': out['v_kv_norm_g'], 'v_w_uq': out['v_w_uq'], 'v_w_ukv': out['v_w_ukv'], 'v_w_mla_o': out['v_w_mla_o'], 'v_w_out': out['v_w_out'], 'v_ffn2_w_in': out['v_ffn2_w_in'], 'v_ffn2_w_out': out['v_ffn2_w_out'], 'v_ple_w_gate': out['v_ple_w_gate'], 'v_ple_w_proj': out['v_ple_w_proj']}


def _loss(weights, diff, rest, loss_target):
    with _jax.named_scope("forward"):
        args = {**rest, TWIN_DIFF_INPUT: diff, **{k: w.astype(_WEIGHT_DTYPES[k]) for k, w in weights.items()}}
        y = _forward(args)
    with _jax.named_scope("loss_head"):
        err = _jnp.square(y.astype(_jnp.float32) - loss_target)
        return 0.5 * _jnp.sum(_jnp.mean(err, axis=-1)) if err.ndim else 0.5 * err


def _adamw(w, g, m, v):
    m = ADAM_B1 * m + (1.0 - ADAM_B1) * g
    v = ADAM_B2 * v + (1.0 - ADAM_B2) * _jnp.square(g)
    m_hat = m / (1.0 - ADAM_B1 ** ADAM_STEP)
    v_hat = v / (1.0 - ADAM_B2 ** ADAM_STEP)
    delta = -ADAM_LR * (m_hat / (_jnp.sqrt(v_hat) + ADAM_EPS) + ADAM_WD * w)
    return delta, m, v


def reference(x, p, positions, ln_g, ln_b, ffn1_w_in, ffn1_w_out, w_in, ret_gn_g, w_ret_o, q_norm_g, kv_norm_g, w_uq, w_ukv, w_mla_o, w_out, ffn2_w_in, ffn2_w_out, ple_w_gate, ple_w_proj, loss_target, m_ln_g, m_ln_b, m_ffn1_w_in, m_ffn1_w_out, m_w_in, m_ret_gn_g, m_w_ret_o, m_q_norm_g, m_kv_norm_g, m_w_uq, m_w_ukv, m_w_mla_o, m_w_out, m_ffn2_w_in, m_ffn2_w_out, m_ple_w_gate, m_ple_w_proj, v_ln_g, v_ln_b, v_ffn1_w_in, v_ffn1_w_out, v_w_in, v_ret_gn_g, v_w_ret_o, v_q_norm_g, v_kv_norm_g, v_w_uq, v_w_ukv, v_w_mla_o, v_w_out, v_ffn2_w_in, v_ffn2_w_out, v_ple_w_gate, v_ple_w_proj):
    given = dict(x=x, p=p, positions=positions, ln_g=ln_g, ln_b=ln_b, ffn1_w_in=ffn1_w_in, ffn1_w_out=ffn1_w_out, w_in=w_in, ret_gn_g=ret_gn_g, w_ret_o=w_ret_o, q_norm_g=q_norm_g, kv_norm_g=kv_norm_g, w_uq=w_uq, w_ukv=w_ukv, w_mla_o=w_mla_o, w_out=w_out, ffn2_w_in=ffn2_w_in, ffn2_w_out=ffn2_w_out, ple_w_gate=ple_w_gate, ple_w_proj=ple_w_proj, loss_target=loss_target, m_ln_g=m_ln_g, m_ln_b=m_ln_b, m_ffn1_w_in=m_ffn1_w_in, m_ffn1_w_out=m_ffn1_w_out, m_w_in=m_w_in, m_ret_gn_g=m_ret_gn_g, m_w_ret_o=m_w_ret_o, m_q_norm_g=m_q_norm_g, m_kv_norm_g=m_kv_norm_g, m_w_uq=m_w_uq, m_w_ukv=m_w_ukv, m_w_mla_o=m_w_mla_o, m_w_out=m_w_out, m_ffn2_w_in=m_ffn2_w_in, m_ffn2_w_out=m_ffn2_w_out, m_ple_w_gate=m_ple_w_gate, m_ple_w_proj=m_ple_w_proj, v_ln_g=v_ln_g, v_ln_b=v_ln_b, v_ffn1_w_in=v_ffn1_w_in, v_ffn1_w_out=v_ffn1_w_out, v_w_in=v_w_in, v_ret_gn_g=v_ret_gn_g, v_w_ret_o=v_w_ret_o, v_q_norm_g=v_q_norm_g, v_kv_norm_g=v_kv_norm_g, v_w_uq=v_w_uq, v_w_ukv=v_w_ukv, v_w_mla_o=v_w_mla_o, v_w_out=v_w_out, v_ffn2_w_in=v_ffn2_w_in, v_ffn2_w_out=v_ffn2_w_out, v_ple_w_gate=v_ple_w_gate, v_ple_w_proj=v_ple_w_proj)
    weights = {n: given[n] for n in TWIN_WEIGHTS}
    shared = {n: given[n] for n in SHARED_INPUTS}
    per_example = {n: given[n] for n in ['x', 'p', 'positions']}
    grad_fn = _jax.value_and_grad(_loss, argnums=(0, 1))

    def one_microbatch(ex, loss_target):
        ex = dict(ex)
        diff = ex.pop(TWIN_DIFF_INPUT)
        return grad_fn(weights, diff, {**shared, **ex}, loss_target)

    if N_MICROBATCH == 1:
        loss, (grad_w, grad_x) = one_microbatch(per_example, given["loss_target"])
    else:
        def body(carry, xs):
            loss_sum, grad_sum = carry
            l_k, (gw_k, gx_k) = one_microbatch(xs[0], xs[1])
            with _jax.named_scope("update"):
                return (loss_sum + l_k, _jax.tree.map(_jnp.add, grad_sum, gw_k)), gx_k

        init = (_jnp.zeros((), _jnp.float32), _jax.tree.map(_jnp.zeros_like, weights))
        (loss, grad_w), grad_x = _jax.lax.scan(body, init, (per_example, given["loss_target"]))
    with _jax.named_scope("update"):
        delta_w, new_m, new_v = {}, {}, {}
        for n in TWIN_WEIGHTS:
            delta_w[n], new_m[n], new_v[n] = _adamw(weights[n], grad_w[n], given["m_" + n], given["v_" + n])
    return (loss, grad_x, *[grad_w[n] for n in TWIN_WEIGHTS], *[delta_w[n] for n in TWIN_WEIGHTS],
            *[new_m[n] for n in TWIN_WEIGHTS], *[new_v[n] for n in TWIN_WEIGHTS])
```

```python
import functools

import jax
import jax.numpy as jnp
from jax import lax
from jax.experimental import pallas as pl
from jax.experimental.pallas import tpu as pltpu

D_MODEL = 1024
CHUNK = 64
D_PLE = 256
D_FF = 2816
RET_HEADS = 8
RET_DK = 128
RET_DV = 256
MLA_HEADS = 8
MLA_NOPE = 128
MLA_ROPE = 64
MLA_DV = 128
MLA_QK = 256
Q_LORA = 256
KV_LORA = 256
ROPE_BASE = 10000.0
EPS = 1e-5
N_LN = 4
ALPHA = 2.0 ** 0.25
ADAM_LR = 0.001
ADAM_B1 = 0.9
ADAM_B2 = 0.999
ADAM_EPS = 1e-08
ADAM_WD = 0.01
ADAM_STEP = 10

LANES = 128
PACK_C = 1024
PACK_ROWS = 32
VMEM_LIMIT = 60 << 20
N_CHIPS = 4

F32 = jnp.float32
BF16 = jnp.bfloat16
MESH = pl.DeviceIdType.MESH
HBM_SPEC = pl.BlockSpec(memory_space=pltpu.HBM)
VMEM_SPEC = pl.BlockSpec(memory_space=pltpu.VMEM)

BIG_WEIGHTS = ("ffn1_w_in", "ffn1_w_out", "w_in", "w_ret_o", "w_uq", "w_ukv", "w_mla_o", "w_out",
               "ffn2_w_in", "ffn2_w_out", "ple_w_gate", "ple_w_proj")
COL_SHARDED = ("ffn1_w_in", "w_in", "w_uq", "w_ukv", "ffn2_w_in", "ple_w_proj")


def _dot(a, b):
    return jnp.dot(a, b, preferred_element_type=F32)


def _dot_nt(a, b):
    return lax.dot_general(a, b, (((1,), (1,)), ((), ())), preferred_element_type=F32)


def _dot_tn(a, b):
    return lax.dot_general(a, b, (((0,), (0,)), ((), ())), preferred_element_type=F32)


def _bf(x):
    return x.astype(BF16)


def _sigmoid(x):
    return 1.0 / (1.0 + jnp.exp(-x))


def _mean(x):
    return jnp.mean(x, axis=-1, keepdims=True)


def _ln_stats(z):
    zc = z - _mean(z)
    rstd = lax.rsqrt(_mean(zc * zc) + EPS)
    return zc * rstd, rstd


def _ln_bwd(dy, xhat, rstd, g):
    dxhat = dy * g
    dz = rstd * (dxhat - _mean(dxhat) - xhat * _mean(dxhat * xhat))
    return dz, jnp.sum(dy * xhat, axis=0, keepdims=True), jnp.sum(dy, axis=0, keepdims=True)


def _roll(x, shift):
    return pltpu.roll(x, shift, 1)


def _chunk_of(idx):
    return jnp.right_shift(idx, CHUNK.bit_length() - 1)


def _tile(n, cap, mult=LANES):
    if n <= cap:
        return n
    for t in range(cap - cap % mult, 0, -mult):
        if n % t == 0:
            return t
    return n


def _zero_map(nd, *_):
    return (0,) * nd


def _params(sem):
    return pltpu.CompilerParams(dimension_semantics=sem, vmem_limit_bytes=VMEM_LIMIT)


def _rowcall(name, body, n_rows, tm, row_ins, full_ins, row_outs, acc_outs=()):
    def kern(*refs):
        body(pl.program_id(0), *refs)

    in_specs = [pl.BlockSpec((tm, a.shape[1]), lambda i: (i, 0)) for a in row_ins]
    in_specs += [pl.BlockSpec(a.shape, functools.partial(_zero_map, a.ndim), pipeline_mode=pl.Buffered(1))
                 for a in full_ins]
    out_specs = [pl.BlockSpec((tm, w), lambda i: (i, 0)) for (w, _) in row_outs]
    out_specs += [pl.BlockSpec(s, functools.partial(_zero_map, len(s))) for (s, _) in acc_outs]
    out_shape = [jax.ShapeDtypeStruct((n_rows, w), dt) for (w, dt) in row_outs]
    out_shape += [jax.ShapeDtypeStruct(s, dt) for (s, dt) in acc_outs]
    return pl.pallas_call(kern, grid=(n_rows // tm,), in_specs=in_specs, out_specs=out_specs,
                          out_shape=out_shape, name=name, compiler_params=_params(("arbitrary",)))(
                              *row_ins, *full_ins)


def _acc(step, ref, val):
    @pl.when(step == 0)
    def _():
        ref[...] = val

    @pl.when(step != 0)
    def _():
        ref[...] += val


def _ffn_fwd(name, x, w_in4, w_out, ln_g, ln_b, tm):
    T, D = x.shape
    fh = w_in4.shape[2]

    def body(i, x_ref, w4_ref, wo_ref, g_ref, b_ref, h_ref, z_ref, a_ref):
        xv = x_ref[...]
        xb = _bf(xv)
        f = jnp.zeros((tm, D), F32)
        for k in range(2):
            gk = _dot(xb, w4_ref[k])
            uk = _dot(xb, w4_ref[2 + k])
            a_ref[:, k * fh:(k + 1) * fh] = _bf(gk)
            a_ref[:, (2 + k) * fh:(3 + k) * fh] = _bf(uk)
            f += _dot(_bf(gk * _sigmoid(gk) * uk), wo_ref[k * fh:(k + 1) * fh, :])
        z = ALPHA * xv + 0.5 * f
        xhat, _ = _ln_stats(z)
        z_ref[...] = z
        h_ref[...] = xhat * g_ref[...] + b_ref[...]

    return _rowcall(name, body, T, tm, [x], [w_in4, w_out, ln_g, ln_b],
                    [(D, F32), (D, F32), (4 * fh, BF16)])


def _ffn_bwd(name, dh, z, a, w_in4, w_out, ln_g, tm):
    T, D = dh.shape
    fh = w_in4.shape[2]

    def body(i, dh_ref, z_ref, a_ref, w4_ref, wo_ref, g_ref, dx_ref, da_ref, s_ref, df_ref, dg_ref, db_ref):
        xhat, rstd = _ln_stats(z_ref[...])
        dz, dg, db = _ln_bwd(dh_ref[...], xhat, rstd, g_ref[...])
        _acc(i, dg_ref, dg)
        _acc(i, db_ref, db)
        dfb = _bf(0.5 * dz)
        df_ref[...] = dfb
        dx = ALPHA * dz
        for k in range(2):
            gk = a_ref[:, k * fh:(k + 1) * fh].astype(F32)
            uk = a_ref[:, (2 + k) * fh:(3 + k) * fh].astype(F32)
            ds = _dot_nt(dfb, wo_ref[k * fh:(k + 1) * fh, :])
            sig = _sigmoid(gk)
            silu = gk * sig
            dgk = _bf(ds * uk * sig * (1.0 + gk * (1.0 - sig)))
            duk = _bf(ds * silu)
            s_ref[:, k * fh:(k + 1) * fh] = _bf(silu * uk)
            da_ref[:, k * fh:(k + 1) * fh] = dgk
            da_ref[:, (2 + k) * fh:(3 + k) * fh] = duk
            dx += _dot_nt(dgk, w4_ref[k]) + _dot_nt(duk, w4_ref[2 + k])
        dx_ref[...] = dx

    return _rowcall(name, body, T, tm, [dh, z, a], [w_in4, w_out, ln_g],
                    [(D, F32), (4 * fh, BF16), (2 * fh, BF16), (D, BF16)],
                    [((1, D), F32), ((1, D), F32)])


def _mm_tn(name, a, b, out_dtype=BF16):
    T, M = a.shape
    N = b.shape[1]
    tk = _tile(T, 512, 8)
    tm = _tile(M, 1408)
    tn = _tile(N, 1536)
    nk = T // tk

    def kern(a_ref, b_ref, o_ref, acc_ref):
        k = pl.program_id(2)
        part = _dot_tn(_bf(a_ref[...]), _bf(b_ref[...]))

        @pl.when(k == 0)
        def _():
            acc_ref[...] = part

        @pl.when(k != 0)
        def _():
            acc_ref[...] += part

        @pl.when(k == nk - 1)
        def _():
            o_ref[...] = acc_ref[...].astype(out_dtype)

    return pl.pallas_call(
        kern, grid=(M // tm, N // tn, nk),
        in_specs=[pl.BlockSpec((tk, tm), lambda i, j, k: (k, i)), pl.BlockSpec((tk, tn), lambda i, j, k: (k, j))],
        out_specs=pl.BlockSpec((tm, tn), lambda i, j, k: (i, j)),
        out_shape=jax.ShapeDtypeStruct((M, N), out_dtype),
        scratch_shapes=[pltpu.VMEM((tm, tn), F32)], name=name,
        compiler_params=_params(("arbitrary", "arbitrary", "arbitrary")))(a, b)


def _proj_ret(h1, w_r, cos_r, sin_r, tm):
    T, D = h1.shape
    qk = RET_HEADS * RET_DK
    rv = RET_HEADS * RET_DV

    def body(i, h_ref, cos_ref, sin_ref, w_ref, q_ref, k_ref, v_ref, g_ref):
        hb = _bf(h_ref[...])
        cos, sin = cos_ref[...], sin_ref[...]
        for out_ref, off, scale in ((q_ref, 0, 1.0), (k_ref, qk, RET_DK ** -0.5)):
            pr = _dot(hb, w_ref[:, off:off + qk])
            for h in range(RET_HEADS):
                t = pr[:, h * RET_DK:(h + 1) * RET_DK]
                out_ref[:, h * RET_DK:(h + 1) * RET_DK] = _bf((t * cos + _roll(t, RET_DK // 2) * sin) * scale)
        v_ref[...] = _bf(_dot(hb, w_ref[:, 2 * qk:2 * qk + rv]))
        g_ref[...] = _dot(hb, w_ref[:, 2 * qk + rv:2 * qk + 2 * rv])

    return _rowcall("proj_ret", body, T, tm, [h1, cos_r, sin_r], [w_r],
                    [(qk, BF16), (qk, BF16), (rv, BF16), (rv, F32)])


def _rope_pe(t, c, s1, s2):
    return t * c + _roll(t, LANES - MLA_ROPE // 2) * s1 + _roll(t, MLA_ROPE // 2) * s2


def _rope_pe_bwd(dy, c, s1, s2):
    return dy * c + _roll(dy * s1, MLA_ROPE // 2) + _roll(dy * s2, LANES - MLA_ROPE // 2)


def _rms(x, g):
    r = lax.rsqrt(_mean(x * x) + EPS)
    return x * r, r


def _proj_mla(h1, tabs, w_c, w_kpe, w_g, w_uq, w_uk, w_uv, qn_g, kvn_g, tm):
    T, D = h1.shape
    H = MLA_HEADS

    def body(i, h_ref, c_ref, s1_ref, s2_ref, wc_ref, wk_ref, wg_ref, wuq_ref, wuk_ref, wuv_ref, qg_ref, kg_ref,
             lat_ref, gt_ref, q_ref, k_ref, v_ref, ln_ref):
        hb = _bf(h_ref[...])
        c, s1, s2 = c_ref[...], s1_ref[...], s2_ref[...]
        lat = _dot(hb, wc_ref[...])
        lat_ref[...] = lat
        gt_ref[...] = _dot(hb, wg_ref[...])
        cqn, _ = _rms(lat[:, :Q_LORA], None)
        ckn, _ = _rms(lat[:, Q_LORA:], None)
        cqn = _bf(cqn * qg_ref[...])
        ckn = _bf(ckn * kg_ref[...])
        ln_ref[:, :Q_LORA] = cqn
        ln_ref[:, Q_LORA:] = ckn
        q = _dot(cqn, wuq_ref[...])
        kn = _dot(ckn, wuk_ref[...])
        v_ref[...] = _bf(_dot(ckn, wuv_ref[...]))
        kpe = _bf(_rope_pe(_dot(hb, wk_ref[...]), c, s1, s2))
        for h in range(H):
            o = h * MLA_QK
            q_ref[:, o:o + MLA_NOPE] = _bf(q[:, o:o + MLA_NOPE])
            q_ref[:, o + MLA_NOPE:o + MLA_QK] = _bf(_rope_pe(q[:, o + MLA_NOPE:o + MLA_QK], c, s1, s2))
            k_ref[:, o:o + MLA_NOPE] = _bf(kn[:, h * MLA_NOPE:(h + 1) * MLA_NOPE])
            k_ref[:, o + MLA_NOPE:o + MLA_QK] = kpe

    lat_w = Q_LORA + KV_LORA
    return _rowcall("proj_mla", body, T, tm, [h1, *tabs], [w_c, w_kpe, w_g, w_uq, w_uk, w_uv, qn_g, kvn_g],
                    [(lat_w, F32), (2 * D, F32), (H * MLA_QK, BF16), (H * MLA_QK, BF16), (H * MLA_DV, BF16),
                     (lat_w, BF16)])


def _ret_block(T):
    return min(256, T)


def _ret_decay(lg, bt):
    n = lax.broadcasted_iota(jnp.int32, (bt, bt), 0)
    m = lax.broadcasted_iota(jnp.int32, (bt, bt), 1)
    dmat = jnp.where(_chunk_of(m) <= _chunk_of(n), jnp.exp(lg * jnp.abs(n - m).astype(F32)), 0.0)
    pos = lax.broadcasted_iota(jnp.int32, (bt, 1), 0).astype(F32)
    xi = jnp.exp(lg * (pos + 1.0))
    zeta = jnp.exp(lg * (bt - 1.0 - pos))
    return dmat, xi, zeta, jnp.exp(lg * bt)


def _ret_specs(bt, rev, nb):
    def blk(w):
        if rev:
            return pl.BlockSpec((bt, w), lambda h, b: (nb - 1 - b, h))
        return pl.BlockSpec((bt, w), lambda h, b: (b, h))
    return pl.BlockSpec((None, 1, LANES), lambda h, b: (h, 0, 0)), blk


def _ret_fwd(rq, rk, rv, lgam):
    T = rq.shape[0]
    bt = _ret_block(T)
    nb = T // bt
    lg_spec, blk = _ret_specs(bt, False, nb)

    def kern(lg_ref, q_ref, k_ref, v_ref, y_ref, s_ref):
        @pl.when(pl.program_id(1) == 0)
        def _():
            s_ref[...] = jnp.zeros_like(s_ref)

        dmat, xi, zeta, gb = _ret_decay(lg_ref[:, :1], bt)
        q, k, v = q_ref[...], k_ref[...], v_ref[...]
        sc = _dot_nt(q, k) * dmat
        y_ref[...] = _dot(_bf(sc), v) + _dot(q, _bf(s_ref[...])) * xi
        s_ref[...] = s_ref[...] * gb + _dot_tn(_bf(k.astype(F32) * zeta), v)

    return pl.pallas_call(
        kern, grid=(RET_HEADS, nb), in_specs=[lg_spec, blk(RET_DK), blk(RET_DK), blk(RET_DV)],
        out_specs=blk(RET_DV), out_shape=jax.ShapeDtypeStruct((T, RET_HEADS * RET_DV), F32),
        scratch_shapes=[pltpu.VMEM((RET_DK, RET_DV), F32)], name="ret_fwd",
        compiler_params=_params(("arbitrary", "arbitrary")))(lgam, rq, rk, rv)


def _ret_bwd_q(rq, rk, rv, dy, lgam):
    T = rq.shape[0]
    bt = _ret_block(T)
    nb = T // bt
    lg_spec, blk = _ret_specs(bt, False, nb)

    def kern(lg_ref, k_ref, v_ref, dy_ref, dq_ref, s_ref):
        @pl.when(pl.program_id(1) == 0)
        def _():
            s_ref[...] = jnp.zeros_like(s_ref)

        dmat, xi, zeta, gb = _ret_decay(lg_ref[:, :1], bt)
        k, v, dy = k_ref[...], v_ref[...], dy_ref[...]
        dp = _dot_nt(dy, v) * dmat
        dq_ref[...] = _dot(_bf(dp), k) + _dot_nt(dy, _bf(s_ref[...])) * xi
        s_ref[...] = s_ref[...] * gb + _dot_tn(_bf(k.astype(F32) * zeta), v)

    return pl.pallas_call(
        kern, grid=(RET_HEADS, nb), in_specs=[lg_spec, blk(RET_DK), blk(RET_DV), blk(RET_DV)],
        out_specs=blk(RET_DK), out_shape=jax.ShapeDtypeStruct((T, RET_HEADS * RET_DK), F32),
        scratch_shapes=[pltpu.VMEM((RET_DK, RET_DV), F32)], name="ret_bwd_q",
        compiler_params=_params(("arbitrary", "arbitrary")))(lgam, rk, rv, dy)


def _ret_bwd_kv(rq, rk, rv, dy, lgam):
    T = rq.shape[0]
    bt = _ret_block(T)
    nb = T // bt
    lg_spec, blk = _ret_specs(bt, True, nb)

    def kern(lg_ref, q_ref, k_ref, v_ref, dy_ref, dk_ref, dv_ref, g_ref):
        @pl.when(pl.program_id(1) == 0)
        def _():
            g_ref[...] = jnp.zeros_like(g_ref)

        dmat, xi, zeta, gb = _ret_decay(lg_ref[:, :1], bt)
        q, k, v, dy = q_ref[...], k_ref[...], v_ref[...], dy_ref[...]
        gs = _bf(g_ref[...])
        p = _dot_nt(q, k) * dmat
        dp = _dot_nt(dy, v) * dmat
        dv_ref[...] = _bf(_dot_tn(_bf(p), dy) + _dot(k, gs) * zeta)
        dk_ref[...] = _dot_tn(_bf(dp), q) + _dot_nt(v, gs) * zeta
        g_ref[...] = g_ref[...] * gb + _dot_tn(_bf(q.astype(F32) * xi), dy)

    return pl.pallas_call(
        kern, grid=(RET_HEADS, nb), in_specs=[lg_spec, blk(RET_DK), blk(RET_DK), blk(RET_DV), blk(RET_DV)],
        out_specs=[blk(RET_DK), blk(RET_DV)],
        out_shape=[jax.ShapeDtypeStruct((T, RET_HEADS * RET_DK), F32),
                   jax.ShapeDtypeStruct((T, RET_HEADS * RET_DV), BF16)],
        scratch_shapes=[pltpu.VMEM((RET_DK, RET_DV), F32)], name="ret_bwd_kv",
        compiler_params=_params(("arbitrary", "arbitrary")))(lgam, rq, rk, rv, dy)


def _attn_block(T):
    return min(512, T)


def _attn_mask(tb):
    r = lax.broadcasted_iota(jnp.int32, (tb, tb), 0)
    c = lax.broadcasted_iota(jnp.int32, (tb, tb), 1)
    return _chunk_of(c) <= _chunk_of(r)


ATTN_SCALE = (MLA_NOPE + MLA_ROPE) ** -0.5
MASKED = -1e30


def _attn_fwd(q, k, v):
    T = q.shape[0]
    tb = _attn_block(T)
    nb = T // tb

    def kern(q_ref, k_ref, v_ref, o_ref, lse_ref, m_ref, l_ref, acc_ref):
        qb = pl.program_id(1)
        qv = q_ref[...]
        m_ref[...] = jnp.full_like(m_ref, MASKED)
        l_ref[...] = jnp.zeros_like(l_ref)
        acc_ref[...] = jnp.zeros_like(acc_ref)

        def step(kb, diagonal):
            rows = pl.ds(pl.multiple_of(kb * tb, tb), tb)
            s = _dot_nt(qv, k_ref[rows, :]) * ATTN_SCALE
            if diagonal:
                s = jnp.where(_attn_mask(tb), s, MASKED)
            m_old = m_ref[...]
            m_new = jnp.maximum(m_old, jnp.max(s, axis=-1, keepdims=True))
            p = jnp.exp(s - m_new)
            corr = jnp.exp(m_old - m_new)
            l_ref[...] = l_ref[...] * corr + jnp.sum(p, axis=-1, keepdims=True)
            acc_ref[...] = acc_ref[...] * corr + _dot(_bf(p), v_ref[rows, :])
            m_ref[...] = m_new

        def loop_body(kb, carry):
            step(kb, False)
            return carry

        lax.fori_loop(0, qb, loop_body, 0)
        step(qb, True)
        o_ref[...] = acc_ref[...] / l_ref[...]
        lse_ref[...] = jnp.broadcast_to(m_ref[...] + jnp.log(l_ref[...]), (tb, LANES))

    return pl.pallas_call(
        kern, grid=(MLA_HEADS, nb),
        in_specs=[pl.BlockSpec((tb, MLA_QK), lambda h, i: (i, h)), pl.BlockSpec((T, MLA_QK), lambda h, i: (0, h)),
                  pl.BlockSpec((T, MLA_DV), lambda h, i: (0, h))],
        out_specs=[pl.BlockSpec((tb, MLA_DV), lambda h, i: (i, h)), pl.BlockSpec((tb, LANES), lambda h, i: (i, h))],
        out_shape=[jax.ShapeDtypeStruct((T, MLA_HEADS * MLA_DV), F32),
                   jax.ShapeDtypeStruct((T, MLA_HEADS * LANES), F32)],
        scratch_shapes=[pltpu.VMEM((tb, 1), F32), pltpu.VMEM((tb, 1), F32), pltpu.VMEM((tb, MLA_DV), F32)],
        name="attn_fwd", compiler_params=_params(("arbitrary", "arbitrary")))(q, k, v)


def _attn_bwd_q(q, k, v, do, o, lse):
    T = q.shape[0]
    tb = _attn_block(T)
    nb = T // tb

    def kern(q_ref, k_ref, v_ref, do_ref, o_ref, lse_ref, dq_ref, dl_ref):
        qb = pl.program_id(1)
        qv, dov = q_ref[...], do_ref[...]
        lse = lse_ref[:, :1]
        delta = jnp.sum(dov.astype(F32) * o_ref[...], axis=-1, keepdims=True)
        dl_ref[...] = jnp.broadcast_to(delta, (tb, LANES))
        dq_ref[...] = jnp.zeros_like(dq_ref)

        def step(kb, diagonal):
            rows = pl.ds(pl.multiple_of(kb * tb, tb), tb)
            kv = k_ref[rows, :]
            s = _dot_nt(qv, kv) * ATTN_SCALE
            if diagonal:
                s = jnp.where(_attn_mask(tb), s, MASKED)
            p = jnp.exp(s - lse)
            ds = p * (_dot_nt(dov, v_ref[rows, :]) - delta) * ATTN_SCALE
            dq_ref[...] += _dot(_bf(ds), kv)

        def loop_body(kb, carry):
            step(kb, False)
            return carry

        lax.fori_loop(0, qb, loop_body, 0)
        step(qb, True)

    def blk(w):
        return pl.BlockSpec((tb, w), lambda h, i: (i, h))

    def full(w):
        return pl.BlockSpec((T, w), lambda h, i: (0, h))

    return pl.pallas_call(
        kern, grid=(MLA_HEADS, nb),
        in_specs=[blk(MLA_QK), full(MLA_QK), full(MLA_DV), blk(MLA_DV), blk(MLA_DV), blk(LANES)],
        out_specs=[blk(MLA_QK), blk(LANES)],
        out_shape=[jax.ShapeDtypeStruct((T, MLA_HEADS * MLA_QK), F32),
                   jax.ShapeDtypeStruct((T, MLA_HEADS * LANES), F32)],
        name="attn_bwd_q", compiler_params=_params(("arbitrary", "arbitrary")))(q, k, v, do, o, lse)


def _attn_bwd_kv(q, k, v, do, lse, delta):
    T = q.shape[0]
    tb = _attn_block(T)
    nb = T // tb

    def kern(q_ref, k_ref, v_ref, do_ref, lse_ref, dl_ref, dk_ref, dv_ref, dv_acc):
        kb = pl.program_id(1)
        kv, vv = k_ref[...], v_ref[...]
        dk_ref[...] = jnp.zeros_like(dk_ref)
        dv_acc[...] = jnp.zeros_like(dv_acc)

        def step(qb, diagonal):
            rows = pl.ds(pl.multiple_of(qb * tb, tb), tb)
            qv, dov = q_ref[rows, :], do_ref[rows, :]
            s = _dot_nt(qv, kv) * ATTN_SCALE
            if diagonal:
                s = jnp.where(_attn_mask(tb), s, MASKED)
            p = jnp.exp(s - lse_ref[rows, :][:, :1])
            dv_acc[...] += _dot_tn(_bf(p), dov)
            ds = p * (_dot_nt(dov, vv) - dl_ref[rows, :][:, :1]) * ATTN_SCALE
            dk_ref[...] += _dot_tn(_bf(ds), qv)

        def loop_body(qb, carry):
            step(qb, False)
            return carry

        step(kb, True)
        lax.fori_loop(kb + 1, nb, loop_body, 0)
        dv_ref[...] = _bf(dv_acc[...])

    def blk(w):
        return pl.BlockSpec((tb, w), lambda h, i: (i, h))

    def full(w):
        return pl.BlockSpec((T, w), lambda h, i: (0, h))

    return pl.pallas_call(
        kern, grid=(MLA_HEADS, nb),
        in_specs=[full(MLA_QK), blk(MLA_QK), blk(MLA_DV), full(MLA_DV), full(LANES), full(LANES)],
        out_specs=[blk(MLA_QK), blk(MLA_DV)],
        out_shape=[jax.ShapeDtypeStruct((T, MLA_HEADS * MLA_QK), F32),
                   jax.ShapeDtypeStruct((T, MLA_HEADS * MLA_DV), BF16)],
        scratch_shapes=[pltpu.VMEM((tb, MLA_DV), F32)],
        name="attn_bwd_kv", compiler_params=_params(("arbitrary", "arbitrary")))(q, k, v, do, lse, delta)


def _group_norm(y):
    yc = y - _mean(y)
    rstd = lax.rsqrt(_mean(yc * yc) + EPS)
    return yc * rstd, rstd


def _mix_fwd(y, rg, o, gates, h1, gn_g, w_ret_o, w_mla_o, w_out, ln_g, ln_b, tm):
    T, D = h1.shape

    def body(i, y_ref, rg_ref, o_ref, gt_ref, h_ref, gn_ref, wr_ref, wm_ref, wo_ref, g_ref, b_ref,
             h2_ref, z_ref, yret_ref, ymla_ref, yr_ref, mix_ref):
        for h in range(RET_HEADS):
            sl = slice(h * RET_DV, (h + 1) * RET_DV)
            yn, _ = _group_norm(y_ref[:, sl])
            r = rg_ref[:, sl]
            yr_ref[:, sl] = _bf(r * _sigmoid(r) * (yn * gn_ref[:, sl]))
        yret = _dot(yr_ref[...], wr_ref[...])
        ymla = _dot(_bf(o_ref[...]), wm_ref[...])
        yret_ref[...] = yret
        ymla_ref[...] = ymla
        mix = _bf(_sigmoid(gt_ref[:, :D]) * yret + _sigmoid(gt_ref[:, D:]) * ymla)
        mix_ref[...] = mix
        z = ALPHA * h_ref[...] + _dot(mix, wo_ref[...])
        xhat, _ = _ln_stats(z)
        z_ref[...] = z
        h2_ref[...] = xhat * g_ref[...] + b_ref[...]

    return _rowcall("mix_fwd", body, T, tm, [y, rg, o, gates, h1], [gn_g, w_ret_o, w_mla_o, w_out, ln_g, ln_b],
                    [(D, F32), (D, F32), (D, F32), (D, F32), (RET_HEADS * RET_DV, BF16), (D, BF16)])


def _mix_bwd(dh2, z1, gates, yret, ymla, y, rg, gn_g, w_ret_o, w_mla_o, w_out, ln_g, tm):
    T, D = dh2.shape
    rv = RET_HEADS * RET_DV

    def body(i, dh_ref, z_ref, gt_ref, yret_ref, ymla_ref, y_ref, rg_ref, gn_ref, wr_ref, wm_ref, wo_ref, g_ref,
             dz_ref, dgt_ref, drg_ref, dy_ref, do_ref, dyret_ref, dymla_ref, dg_ref, db_ref, dgn_ref):
        xhat, rstd = _ln_stats(z_ref[...])
        dz, dg, db = _ln_bwd(dh_ref[...], xhat, rstd, g_ref[...])
        _acc(i, dg_ref, dg)
        _acc(i, db_ref, db)
        dz_ref[...] = dz
        dmix = _dot_nt(_bf(dz), wo_ref[...])
        sr = _sigmoid(gt_ref[:, :D])
        sm = _sigmoid(gt_ref[:, D:])
        dgt_ref[:, :D] = _bf(dmix * yret_ref[...] * sr * (1.0 - sr))
        dgt_ref[:, D:] = _bf(dmix * ymla_ref[...] * sm * (1.0 - sm))
        dyret = _bf(dmix * sr)
        dymla = _bf(dmix * sm)
        dyret_ref[...] = dyret
        dymla_ref[...] = dymla
        do_ref[...] = _bf(_dot_nt(dymla, wm_ref[...]))
        dyr = _dot_nt(dyret, wr_ref[...])
        dgn = []
        for h in range(RET_HEADS):
            sl = slice(h * RET_DV, (h + 1) * RET_DV)
            yn, grstd = _group_norm(y_ref[:, sl])
            r = rg_ref[:, sl]
            sig = _sigmoid(r)
            d = dyr[:, sl]
            drg_ref[:, sl] = _bf(d * (yn * gn_ref[:, sl]) * sig * (1.0 + r * (1.0 - sig)))
            dt = d * (r * sig)
            dgn.append(jnp.sum(dt * yn, axis=0, keepdims=True))
            dyn = dt * gn_ref[:, sl]
            dy_ref[:, sl] = _bf(grstd * (dyn - _mean(dyn) - yn * _mean(dyn * yn)))
        _acc(i, dgn_ref, jnp.concatenate(dgn, axis=1))

    return _rowcall("mix_bwd", body, T, tm, [dh2, z1, gates, yret, ymla, y, rg], [gn_g, w_ret_o, w_mla_o, w_out, ln_g],
                    [(D, F32), (2 * D, BF16), (rv, BF16), (rv, BF16), (MLA_HEADS * MLA_DV, BF16), (D, BF16), (D, BF16)],
                    [((1, D), F32), ((1, D), F32), ((1, rv), F32)])


def _proj_mla_bwd(dq, dk, dv, lat, tabs, w_uq, w_uk, w_uv, qn_g, kvn_g, tm):
    T = dq.shape[0]
    H = MLA_HEADS
    lat_w = Q_LORA + KV_LORA

    def body(i, dq_ref, dk_ref, dv_ref, lat_ref, c_ref, s1_ref, s2_ref, wuq_ref, wuk_ref, wuv_ref, qg_ref, kg_ref,
             dlat_ref, dkpe_ref, dqb_ref, dkn_ref, dqg_ref, dkg_ref):
        c, s1, s2 = c_ref[...], s1_ref[...], s2_ref[...]
        dkpe = jnp.zeros((tm, LANES), F32)
        for h in range(H):
            o = h * MLA_QK
            dqb_ref[:, o:o + MLA_NOPE] = _bf(dq_ref[:, o:o + MLA_NOPE])
            dqb_ref[:, o + MLA_NOPE:o + MLA_QK] = _bf(_rope_pe_bwd(dq_ref[:, o + MLA_NOPE:o + MLA_QK], c, s1, s2))
            dkn_ref[:, h * MLA_NOPE:(h + 1) * MLA_NOPE] = _bf(dk_ref[:, o:o + MLA_NOPE])
            dkpe += dk_ref[:, o + MLA_NOPE:o + MLA_QK]
        dkpe_ref[...] = _bf(_rope_pe_bwd(dkpe, c, s1, s2))
        dcqn = _dot_nt(dqb_ref[...], wuq_ref[...])
        dckn = _dot_nt(dkn_ref[...], wuk_ref[...]) + _dot_nt(dv_ref[...], wuv_ref[...])
        for dn, x, g_ref, dg_ref, sl in ((dcqn, lat_ref[:, :Q_LORA], qg_ref, dqg_ref, slice(0, Q_LORA)),
                                         (dckn, lat_ref[:, Q_LORA:], kg_ref, dkg_ref, slice(Q_LORA, lat_w))):
            xn, r = _rms(x, None)
            _acc(i, dg_ref, jnp.sum(dn * xn, axis=0, keepdims=True))
            dxn = dn * g_ref[...]
            dlat_ref[:, sl] = _bf(r * (dxn - xn * _mean(dxn * xn)))

    return _rowcall("proj_mla_bwd", body, T, tm, [dq, dk, dv, lat, *tabs], [w_uq, w_uk, w_uv, qn_g, kvn_g],
                    [(lat_w, BF16), (LANES, BF16), (H * MLA_QK, BF16), (H * MLA_NOPE, BF16)],
                    [((1, Q_LORA), F32), ((1, KV_LORA), F32)])


def _proj_bwd(drq, drk, drv, drg, dz1, dlat, dkpe, dgates, cos_r, sin_r, w_r, w_c, w_kpe, w_g, tm):
    T, D = dz1.shape
    qk = RET_HEADS * RET_DK
    rv = RET_HEADS * RET_DV

    def body(i, drq_ref, drk_ref, drv_ref, drg_ref, dz_ref, dlat_ref, dkpe_ref, dgt_ref, cos_ref, sin_ref,
             wr_ref, wc_ref, wk_ref, wg_ref, dh_ref, dpr_ref):
        cos, sin = cos_ref[...], sin_ref[...]
        for src, off, scale in ((drq_ref, 0, 1.0), (drk_ref, qk, RET_DK ** -0.5)):
            for h in range(RET_HEADS):
                d = src[:, h * RET_DK:(h + 1) * RET_DK]
                dpr_ref[:, off + h * RET_DK:off + (h + 1) * RET_DK] = _bf(
                    (d * cos + _roll(d * sin, RET_DK // 2)) * scale)
        dpr_ref[:, 2 * qk:2 * qk + rv] = drv_ref[...]
        dpr_ref[:, 2 * qk + rv:] = drg_ref[...]
        dh_ref[...] = (ALPHA * dz_ref[...] + _dot_nt(dpr_ref[...], wr_ref[...]) + _dot_nt(dlat_ref[...], wc_ref[...])
                       + _dot_nt(dkpe_ref[...], wk_ref[...]) + _dot_nt(dgt_ref[...], wg_ref[...]))

    return _rowcall("proj_bwd", body, T, tm, [drq, drk, drv, drg, dz1, dlat, dkpe, dgates, cos_r, sin_r],
                    [w_r, w_c, w_kpe, w_g], [(D, F32), (2 * qk + 2 * rv, BF16)])


def _ple_loss(h3, p, target, w_gate, w_proj, ln_g, ln_b, tm):
    T, D = h3.shape

    def body(i, h_ref, p_ref, t_ref, wg_ref, wp_ref, g_ref, b_ref, dh_ref, dgp_ref, dpp_ref, loss_ref, dg_ref, db_ref):
        hv = h_ref[...]
        sg = _sigmoid(_dot(_bf(hv), wg_ref[...]))
        pp = _dot(_bf(p_ref[...]), wp_ref[...])
        xhat, rstd = _ln_stats(ALPHA * hv + sg * pp)
        err = xhat * g_ref[...] + b_ref[...] - t_ref[...]
        row_loss = 0.5 * _mean(err * err)
        _acc(i, loss_ref, jnp.broadcast_to(jnp.sum(row_loss, axis=0, keepdims=True), (1, LANES)))
        dz, dg, db = _ln_bwd(err * (1.0 / D), xhat, rstd, g_ref[...])
        _acc(i, dg_ref, dg)
        _acc(i, db_ref, db)
        dgp = _bf(dz * pp * sg * (1.0 - sg))
        dgp_ref[...] = dgp
        dpp_ref[...] = _bf(dz * sg)
        dh_ref[...] = ALPHA * dz + _dot_nt(dgp, wg_ref[...])

    return _rowcall("ple_loss", body, T, tm, [h3, p, target], [w_gate, w_proj, ln_g, ln_b],
                    [(D, F32), (D, BF16), (D, BF16)], [((1, LANES), F32), ((1, D), F32), ((1, D), F32)])


def _ewise(name, fn, ins, n_out, out_dtype=F32):
    r, c = ins[0].shape
    tr = _tile(r, max(8, (1 << 19) // c // 8 * 8), 8)

    def kern(*refs):
        outs = fn(*[x[...] for x in refs[:len(ins)]])
        for o_ref, o in zip(refs[len(ins):], outs):
            o_ref[...] = o.astype(out_dtype)

    spec = pl.BlockSpec((tr, c), lambda i: (i, 0))
    return pl.pallas_call(kern, grid=(r // tr,), in_specs=[spec] * len(ins), out_specs=[spec] * n_out,
                          out_shape=[jax.ShapeDtypeStruct((r, c), out_dtype)] * n_out, name=name,
                          compiler_params=_params(("arbitrary",)))(*ins)


def _adamw_math(w, g, m, v):
    m = ADAM_B1 * m + (1.0 - ADAM_B1) * g
    v = ADAM_B2 * v + (1.0 - ADAM_B2) * (g * g)
    m_hat = m / (1.0 - ADAM_B1 ** ADAM_STEP)
    v_hat = v / (1.0 - ADAM_B2 ** ADAM_STEP)
    return -ADAM_LR * (m_hat / (jnp.sqrt(v_hat) + ADAM_EPS) + ADAM_WD * w), m, v


def _adamw(name, w, g, m, v):
    shape = w.shape
    c = shape[-1]
    flat = [t.reshape(-1, c) for t in (w, g, m, v)]
    return [t.reshape(shape) for t in _ewise(name, _adamw_math, flat, 3)]


def _place():
    return lax.axis_index("x"), lax.axis_index("y"), lax.axis_index("c")


def _dma_sems(n):
    return [pltpu.SemaphoreType.DMA((n,)), pltpu.SemaphoreType.DMA((n,))]


def _chips_exchange(name, src, broadcast):
    _, R, C = src.shape

    def kern(src_ref, out_ref, send_sems, recv_sems, local_sem):
        x, y, c = _place()
        me = 2 * x + y
        peers = [(1 - x, y), (x, 1 - y), (1 - x, 1 - y)]

        def piece(k):
            return src_ref.at[c] if broadcast else src_ref.at[k]

        def copy(j, chip, landing):
            px, py = chip
            return pltpu.make_async_remote_copy(
                src_ref=piece(2 * px + py), dst_ref=out_ref.at[landing], send_sem=send_sems.at[j],
                recv_sem=recv_sems.at[j], device_id=(px, py, c), device_id_type=MESH)

        own = pltpu.make_async_copy(piece(me), out_ref.at[me], local_sem.at[0])
        own.start()
        sends = [copy(j, chip, me) for j, chip in enumerate(peers)]
        for cp in sends:
            cp.start()
        for j, (px, py) in enumerate(peers):
            copy(j, (px, py), 2 * px + py).wait_recv()
        for cp in sends:
            cp.wait_send()
        own.wait()

    return pl.pallas_call(
        kern, out_shape=jax.ShapeDtypeStruct((N_CHIPS, R, C), src.dtype), in_specs=[HBM_SPEC], out_specs=HBM_SPEC,
        scratch_shapes=_dma_sems(3) + [pltpu.SemaphoreType.DMA((1,))], name=name)(src)


def _sibling_halves(name, src):
    n, _, R, C = src.shape

    def kern(src_ref, out_ref, send_sems, recv_sems):
        x, y, c = _place()

        def copy(k):
            return pltpu.make_async_remote_copy(
                src_ref=src_ref.at[k, 1 - c], dst_ref=out_ref.at[k], send_sem=send_sems.at[k],
                recv_sem=recv_sems.at[k], device_id=(x, y, 1 - c), device_id_type=MESH)

        cps = [copy(k) for k in range(n)]
        for cp in cps:
            cp.start()
        for cp in cps:
            cp.wait_recv()
        for cp in cps:
            cp.wait_send()

    return pl.pallas_call(
        kern, out_shape=jax.ShapeDtypeStruct((n, R, C), src.dtype), in_specs=[HBM_SPEC], out_specs=HBM_SPEC,
        scratch_shapes=_dma_sems(n), name=name)(src)


def _sibling_join(name, src):
    n, R, C = src.shape

    def kern(src_ref, out_ref, send_sems, recv_sems, local_sems):
        x, y, c = _place()

        def copy(k, half):
            return pltpu.make_async_remote_copy(
                src_ref=src_ref.at[k], dst_ref=out_ref.at[k, half], send_sem=send_sems.at[k],
                recv_sem=recv_sems.at[k], device_id=(x, y, 1 - c), device_id_type=MESH)

        mine = [pltpu.make_async_copy(src_ref.at[k], out_ref.at[k, c], local_sems.at[k]) for k in range(n)]
        sends = [copy(k, c) for k in range(n)]
        for cp in mine + sends:
            cp.start()
        for k in range(n):
            copy(k, 1 - c).wait_recv()
        for cp in sends:
            cp.wait_send()
        for cp in mine:
            cp.wait()

    return pl.pallas_call(
        kern, out_shape=jax.ShapeDtypeStruct((n, 2, R, C), src.dtype), in_specs=[HBM_SPEC], out_specs=HBM_SPEC,
        scratch_shapes=_dma_sems(n) + [pltpu.SemaphoreType.DMA((n,))], name=name)(src)


def _all_devices(name, src, reduce):
    r, c = src.shape
    n_dev = 2 * N_CHIPS

    def kern(src_ref, out_ref, *scratch):
        if reduce:
            gat_ref, send_sems, recv_sems = scratch
        else:
            gat_ref = out_ref
            send_sems, recv_sems = scratch
        x, y, cc = _place()
        me = 4 * x + 2 * y + cc
        gat_ref[me] = src_ref[...]
        peers = []
        for j in range(1, n_dev):
            px = 1 - x if j & 4 else x
            py = 1 - y if j & 2 else y
            pc = 1 - cc if j & 1 else cc
            peers.append((px, py, pc))

        def copy(j, peer, slot):
            return pltpu.make_async_remote_copy(
                src_ref=src_ref, dst_ref=gat_ref.at[slot], send_sem=send_sems.at[j], recv_sem=recv_sems.at[j],
                device_id=peer, device_id_type=MESH)

        sends = [copy(j, peer, me) for j, peer in enumerate(peers)]
        for cp in sends:
            cp.start()
        for j, (px, py, pc) in enumerate(peers):
            copy(j, (px, py, pc), 4 * px + 2 * py + pc).wait_recv()
        for cp in sends:
            cp.wait_send()
        if reduce:
            total = gat_ref[0]
            for d in range(1, n_dev):
                total = total + gat_ref[d]
            out_ref[...] = total

    out_shape = jax.ShapeDtypeStruct((r, c) if reduce else (n_dev, r, c), src.dtype)
    scratch = ([pltpu.VMEM((n_dev, r, c), src.dtype)] if reduce else []) + _dma_sems(n_dev - 1)
    return pl.pallas_call(kern, out_shape=out_shape, in_specs=[VMEM_SPEC], out_specs=VMEM_SPEC,
                          scratch_shapes=scratch, name=name)(src)


def _pack_rows(n_elems):
    rows = -(-n_elems // PACK_C)
    return -(-rows // PACK_ROWS) * PACK_ROWS


def _pack(parts, lead):
    lead_shape = parts[0].shape[:lead]
    flat = jnp.concatenate([t.reshape(lead_shape + (-1,)) for t in parts], axis=-1)
    rows = _pack_rows(flat.shape[-1])
    flat = jnp.pad(flat, [(0, 0)] * lead + [(0, rows * PACK_C - flat.shape[-1])])
    return flat.reshape(lead_shape + (2, rows // 2, PACK_C))


def _unpack(packed, shapes, lead):
    lead_shape = packed.shape[:lead]
    flat = packed.reshape(lead_shape + (-1,))
    out, off = [], 0
    for s in shapes:
        n = 1
        for d in s:
            n *= d
        out.append(lax.slice_in_dim(flat, off, off + n, axis=lead).reshape(lead_shape + tuple(s)))
        off += n
    return out


def _join_shards(name, shards):
    _, r, c = shards.shape
    if name in COL_SHARDED:
        return shards.transpose(1, 0, 2).reshape(r, N_CHIPS * c)
    return shards.reshape(N_CHIPS * r, c)


def _split_shards(name, full):
    r, c = full.shape
    if name in COL_SHARDED:
        return full.reshape(r, N_CHIPS, c // N_CHIPS).transpose(1, 0, 2)
    return full.reshape(N_CHIPS, r // N_CHIPS, c)


def _rope_tables(positions):
    pos = positions.reshape(-1).astype(F32)[:, None]
    half = RET_DK // 2
    ang = pos * (ROPE_BASE ** (-jnp.arange(half, dtype=F32) / half))
    cos_r = jnp.concatenate([jnp.cos(ang)] * 2, axis=1)
    sin_r = jnp.concatenate([-jnp.sin(ang), jnp.sin(ang)], axis=1)
    half = MLA_ROPE // 2
    ang = pos * (ROPE_BASE ** (-jnp.arange(half, dtype=F32) / half))
    zeros = jnp.zeros_like(ang)
    rest = LANES - MLA_ROPE
    c = jnp.concatenate([jnp.cos(ang)] * 2 + [jnp.ones((ang.shape[0], rest), F32)], axis=1)
    s1 = jnp.concatenate([-jnp.sin(ang), zeros, jnp.zeros((ang.shape[0], rest), F32)], axis=1)
    s2 = jnp.concatenate([zeros, jnp.sin(ang), jnp.zeros((ang.shape[0], rest), F32)], axis=1)
    return cos_r, sin_r, (c, s1, s2)


def _local_step(x, p, positions, target, w, ln_g, ln_b, gn_g, qn_g, kvn_g):
    T, D = x.shape
    tm = min(256, T)
    H = MLA_HEADS
    qk, rv = RET_HEADS * RET_DK, RET_HEADS * RET_DV
    cos_r, sin_r, tabs = _rope_tables(positions)
    lgam = jnp.broadcast_to(jnp.log(1.0 - 2.0 ** (-5.0 - jnp.arange(RET_HEADS, dtype=F32)))[:, None, None],
                            (RET_HEADS, 1, LANES))
    lng = [ln_g[k:k + 1] for k in range(N_LN)]
    lnb = [ln_b[k:k + 1] for k in range(N_LN)]

    w_in = w["w_in"]
    o_lat, o_kpe, o_gate = 2 * qk + 2 * rv, 2 * qk + 2 * rv + Q_LORA + KV_LORA, 2 * qk + 2 * rv + Q_LORA + KV_LORA + MLA_ROPE
    w_r, w_c = w_in[:, :o_lat], w_in[:, o_lat:o_kpe]
    w_kpe = jnp.pad(w_in[:, o_kpe:o_gate], ((0, 0), (0, LANES - MLA_ROPE)))
    w_g = w_in[:, o_gate:]
    w_uq = jnp.pad(w["w_uq"].reshape(Q_LORA, H, MLA_NOPE + MLA_ROPE),
                   ((0, 0), (0, 0), (0, MLA_QK - MLA_NOPE - MLA_ROPE))).reshape(Q_LORA, H * MLA_QK)
    w_ukv = w["w_ukv"].reshape(KV_LORA, H, MLA_NOPE + MLA_DV)
    w_uk = w_ukv[:, :, :MLA_NOPE].reshape(KV_LORA, H * MLA_NOPE)
    w_uv = w_ukv[:, :, MLA_NOPE:].reshape(KV_LORA, H * MLA_DV)

    h1, z0, a1 = _ffn_fwd("ffn1_fwd", x, w["ffn1_w_in"], w["ffn1_w_out"], lng[0], lnb[0], tm)
    rq, rk, rvv, rg = _proj_ret(h1, w_r, cos_r, sin_r, tm)
    lat, gates, q, k, v, latn = _proj_mla(h1, tabs, w_c, w_kpe, w_g, w_uq, w_uk, w_uv, qn_g, kvn_g, tm)
    y = _ret_fwd(rq, rk, rvv, lgam)
    o, lse = _attn_fwd(q, k, v)
    h2, z1, yret, ymla, yr, mix = _mix_fwd(y, rg, o, gates, h1, gn_g, w["w_ret_o"], w["w_mla_o"], w["w_out"],
                                           lng[1], lnb[1], tm)
    h3, z2, a2 = _ffn_fwd("ffn2_fwd", h2, w["ffn2_w_in"], w["ffn2_w_out"], lng[2], lnb[2], tm)

    dh3, dgp, dpp, loss, dg3, db3 = _ple_loss(h3, p, target, w["ple_w_gate"], w["ple_w_proj"], lng[3], lnb[3], tm)
    dh2, da2, s2, df2, dg2, db2 = _ffn_bwd("ffn2_bwd", dh3, z2, a2, w["ffn2_w_in"], w["ffn2_w_out"], lng[2], tm)
    (dz1, dgates, drg, dy, do, dyret, dymla, dg1, db1, dgn) = _mix_bwd(
        dh2, z1, gates, yret, ymla, y, rg, gn_g, w["w_ret_o"], w["w_mla_o"], w["w_out"], lng[1], tm)
    drq = _ret_bwd_q(rq, rk, rvv, dy, lgam)
    drk, drv = _ret_bwd_kv(rq, rk, rvv, dy, lgam)
    dq, delta = _attn_bwd_q(q, k, v, do, o, lse)
    dk, dv = _attn_bwd_kv(q, k, v, do, lse, delta)
    dlat, dkpe, dqb, dkn, dqg, dkg = _proj_mla_bwd(dq, dk, dv, lat, tabs, w_uq, w_uk, w_uv, qn_g, kvn_g, tm)
    dh1, dpr = _proj_bwd(drq, drk, drv, drg, dz1, dlat, dkpe, dgates, cos_r, sin_r, w_r, w_c, w_kpe, w_g, tm)
    dx, da1, s1, df1, dg0, db0 = _ffn_bwd("ffn1_bwd", dh1, z0, a1, w["ffn1_w_in"], w["ffn1_w_out"], lng[0], tm)

    g_uq = _mm_tn("wg_uq", latn[:, :Q_LORA], dqb).reshape(Q_LORA, H, MLA_QK)[:, :, :MLA_NOPE + MLA_ROPE]
    g_uk = _mm_tn("wg_uk", latn[:, Q_LORA:], dkn).reshape(KV_LORA, H, MLA_NOPE)
    g_uv = _mm_tn("wg_uv", latn[:, Q_LORA:], dv).reshape(KV_LORA, H, MLA_DV)
    grads = {
        "ffn1_w_in": _mm_tn("wg_ffn1_in", x, da1),
        "ffn1_w_out": _mm_tn("wg_ffn1_out", s1, df1),
        "w_in": jnp.concatenate([_mm_tn("wg_in_r", h1, dpr), _mm_tn("wg_in_c", h1, dlat),
                                 _mm_tn("wg_in_kpe", h1, dkpe)[:, :MLA_ROPE], _mm_tn("wg_in_g", h1, dgates)], axis=1),
        "w_ret_o": _mm_tn("wg_ret_o", yr, dyret),
        "w_uq": g_uq.reshape(Q_LORA, H * (MLA_NOPE + MLA_ROPE)),
        "w_ukv": jnp.concatenate([g_uk, g_uv], axis=2).reshape(KV_LORA, H * (MLA_NOPE + MLA_DV)),
        "w_mla_o": _mm_tn("wg_mla_o", o, dymla),
        "w_out": _mm_tn("wg_out", mix, dz1),
        "ffn2_w_in": _mm_tn("wg_ffn2_in", h2, da2),
        "ffn2_w_out": _mm_tn("wg_ffn2_out", s2, df2),
        "ple_w_gate": _mm_tn("wg_ple_gate", h3, dgp),
        "ple_w_proj": _mm_tn("wg_ple_proj", p, dpp),
    }
    small = {"ln_g": jnp.concatenate([dg0, dg1, dg2, dg3], axis=0), "ln_b": jnp.concatenate([db0, db1, db2, db3], axis=0),
             "ret_gn_g": dgn, "q_norm_g": dqg, "kv_norm_g": dkg}
    return loss[0, 0], dx, grads, small


def _gather_weights(shards):
    packed = _pack([_bf(shards[n]) for n in BIG_WEIGHTS], 0)
    halves = _chips_exchange("gather_chips", packed, True)
    both = _sibling_join("gather_cores", halves)
    parts = _unpack(both, [shards[n].shape for n in BIG_WEIGHTS], 1)
    w = {}
    for n, t in zip(BIG_WEIGHTS, parts):
        w[n] = t if n in ("ffn1_w_in", "ffn2_w_in") else _join_shards(n, t)
    return w


def _reduce_grads(grads, shapes):
    packed = _pack([_split_shards(n, grads[n]) for n in BIG_WEIGHTS], 1)
    c = lax.axis_index("c")
    theirs = _sibling_halves("reduce_cores", packed)
    mine = lax.dynamic_index_in_dim(packed, c, axis=1, keepdims=False)
    n, r, cc = mine.shape
    chip_sum = _ewise("reduce_cores_add", lambda a, b: (a.astype(F32) + b.astype(F32),),
                      [mine.reshape(n * r, cc), theirs.reshape(n * r, cc)], 1, BF16)[0].reshape(n, r, cc)
    parts = _chips_exchange("reduce_chips", chip_sum, False)
    total = _ewise("reduce_chips_add",
                   lambda a, b, c_, d: (((a.astype(F32) + b.astype(F32)) + c_.astype(F32)) + d.astype(F32),),
                   [parts[k] for k in range(N_CHIPS)], 1, F32)[0]
    both = _sibling_join("reduce_join", total[None])
    return dict(zip(BIG_WEIGHTS, _unpack(both[0], [shapes[n] for n in BIG_WEIGHTS], 0)))


def kernel(x, p, positions, ln_g, ln_b, ffn1_w_in, ffn1_w_out, w_in, ret_gn_g, w_ret_o, q_norm_g, kv_norm_g, w_uq, w_ukv, w_mla_o, w_out, ffn2_w_in, ffn2_w_out, ple_w_gate, ple_w_proj, loss_target, m_ln_g, m_ln_b, m_ffn1_w_in, m_ffn1_w_out, m_w_in, m_ret_gn_g, m_w_ret_o, m_q_norm_g, m_kv_norm_g, m_w_uq, m_w_ukv, m_w_mla_o, m_w_out, m_ffn2_w_in, m_ffn2_w_out, m_ple_w_gate, m_ple_w_proj, v_ln_g, v_ln_b, v_ffn1_w_in, v_ffn1_w_out, v_w_in, v_ret_gn_g, v_w_ret_o, v_q_norm_g, v_kv_norm_g, v_w_uq, v_w_ukv, v_w_mla_o, v_w_out, v_ffn2_w_in, v_ffn2_w_out, v_ple_w_gate, v_ple_w_proj):
    names = ("ln_g", "ln_b", "ffn1_w_in", "ffn1_w_out", "w_in", "ret_gn_g", "w_ret_o", "q_norm_g", "kv_norm_g", "w_uq",
             "w_ukv", "w_mla_o", "w_out", "ffn2_w_in", "ffn2_w_out", "ple_w_gate", "ple_w_proj")
    weights = dict(zip(names, (ln_g, ln_b, ffn1_w_in, ffn1_w_out, w_in, ret_gn_g, w_ret_o, q_norm_g, kv_norm_g, w_uq,
                               w_ukv, w_mla_o, w_out, ffn2_w_in, ffn2_w_out, ple_w_gate, ple_w_proj)))
    m_in = dict(zip(names, (m_ln_g, m_ln_b, m_ffn1_w_in, m_ffn1_w_out, m_w_in, m_ret_gn_g, m_w_ret_o, m_q_norm_g,
                            m_kv_norm_g, m_w_uq, m_w_ukv, m_w_mla_o, m_w_out, m_ffn2_w_in, m_ffn2_w_out, m_ple_w_gate,
                            m_ple_w_proj)))
    v_in = dict(zip(names, (v_ln_g, v_ln_b, v_ffn1_w_in, v_ffn1_w_out, v_w_in, v_ret_gn_g, v_w_ret_o, v_q_norm_g,
                            v_kv_norm_g, v_w_uq, v_w_ukv, v_w_mla_o, v_w_out, v_ffn2_w_in, v_ffn2_w_out, v_ple_w_gate,
                            v_ple_w_proj)))
    chip = 2 * lax.axis_index("x") + lax.axis_index("y")
    D = x.shape[-1]
    dq = D // N_CHIPS

    shards = {n: weights[n][0] for n in BIG_WEIGHTS}
    w = _gather_weights(shards)
    ln_all = _all_devices("gather_ln", jnp.concatenate([ln_g[0], ln_b[0]], axis=0), False)
    ln_full = ln_all[::2].transpose(1, 0, 2).reshape(2 * N_LN, D)
    loss, dx, grads, small = _local_step(x[0], p[0, 0], positions, loss_target[0], w, ln_full[:N_LN], ln_full[N_LN:],
                                         ret_gn_g, q_norm_g, kv_norm_g)

    loss = lax.psum(loss, ("x", "y", "c"))
    big = _reduce_grads(grads, {n: shards[n].shape for n in BIG_WEIGHTS})
    small_names = ("ln_g", "ln_b", "ret_gn_g", "q_norm_g", "kv_norm_g")
    flat = jnp.concatenate([small[n].reshape(-1) for n in small_names])
    rows = -(-flat.shape[0] // LANES // 8) * 8
    flat = jnp.pad(flat, (0, rows * LANES - flat.shape[0])).reshape(rows, LANES)
    flat = _all_devices("reduce_small", flat, True).reshape(-1)
    off = 0
    for n in small_names:
        size = small[n].size
        small[n] = flat[off:off + size].reshape(small[n].shape)
        off += size
    g_out = dict(big)
    for n in ("ln_g", "ln_b"):
        g_out[n] = lax.dynamic_slice_in_dim(small[n], chip * dq, dq, axis=1)
    for n in ("ret_gn_g", "q_norm_g", "kv_norm_g"):
        g_out[n] = small[n]

    deltas, new_m, new_v = {}, {}, {}
    for n in names:
        g = g_out[n].reshape(weights[n].shape)
        g_out[n] = g
        deltas[n], new_m[n], new_v[n] = _adamw("adamw_" + n, weights[n], g, m_in[n], v_in[n])
    return (loss, dx[None], *[g_out[n] for n in names], *[deltas[n] for n in names], *[new_m[n] for n in names],
            *[new_v[n] for n in names])
```

```python
import functools

import jax
import jax.numpy as jnp
from jax import lax
from jax.experimental import pallas as pl
from jax.experimental.pallas import tpu as pltpu

D_MODEL = 1024
CHUNK = 64
D_PLE = 256
D_FF = 2816
RET_HEADS = 8
RET_DK = 128
RET_DV = 256
MLA_HEADS = 8
MLA_NOPE = 128
MLA_ROPE = 64
MLA_DV = 128
MLA_QK = 256
Q_LORA = 256
KV_LORA = 256
ROPE_BASE = 10000.0
EPS = 1e-5
N_LN = 4
ALPHA = 2.0 ** 0.25
ADAM_LR = 0.001
ADAM_B1 = 0.9
ADAM_B2 = 0.999
ADAM_EPS = 1e-08
ADAM_WD = 0.01
ADAM_STEP = 10

LANES = 128
VMEM_LIMIT = 60 << 20
N_CHIPS = 4

F32 = jnp.float32
BF16 = jnp.bfloat16
MESH = pl.DeviceIdType.MESH
HBM_SPEC = pl.BlockSpec(memory_space=pltpu.HBM)
VMEM_SPEC = pl.BlockSpec(memory_space=pltpu.VMEM)

BIG_WEIGHTS = ("ffn1_w_in", "ffn1_w_out", "w_in", "w_ret_o", "w_uq", "w_ukv", "w_mla_o", "w_out",
               "ffn2_w_in", "ffn2_w_out", "ple_w_gate", "ple_w_proj")
COL_SHARDED = ("ffn1_w_in", "w_in", "w_uq", "w_ukv", "ffn2_w_in", "ple_w_proj")


def _dot(a, b):
    return jnp.dot(a, b, preferred_element_type=F32)


def _dot_nt(a, b):
    return lax.dot_general(a, b, (((1,), (1,)), ((), ())), preferred_element_type=F32)


def _dot_tn(a, b):
    return lax.dot_general(a, b, (((0,), (0,)), ((), ())), preferred_element_type=F32)


def _bf(x):
    return x.astype(BF16)


def _sigmoid(x):
    return 1.0 / (1.0 + jnp.exp(-x))


def _mean(x):
    return jnp.mean(x, axis=-1, keepdims=True)


def _ln_stats(z):
    zc = z - _mean(z)
    rstd = lax.rsqrt(_mean(zc * zc) + EPS)
    return zc * rstd, rstd


def _ln_bwd(dy, xhat, rstd, g):
    dxhat = dy * g
    dz = rstd * (dxhat - _mean(dxhat) - xhat * _mean(dxhat * xhat))
    return dz, jnp.sum(dy * xhat, axis=0, keepdims=True), jnp.sum(dy, axis=0, keepdims=True)


def _roll(x, shift):
    return pltpu.roll(x, shift, 1)


def _chunk_of(idx):
    return jnp.right_shift(idx, CHUNK.bit_length() - 1)


def _tile(n, cap, mult=LANES):
    if n <= cap:
        return n
    for t in range(cap - cap % mult, 0, -mult):
        if n % t == 0:
            return t
    return n


def _zero_map(nd, *_):
    return (0,) * nd


def _params(sem):
    return pltpu.CompilerParams(dimension_semantics=sem, vmem_limit_bytes=VMEM_LIMIT)


def _rowcall(name, body, n_rows, tm, row_ins, full_ins, row_outs, acc_outs=()):
    def kern(*refs):
        body(pl.program_id(0), *refs)

    in_specs = [pl.BlockSpec((tm, a.shape[1]), lambda i: (i, 0)) for a in row_ins]
    in_specs += [pl.BlockSpec(a.shape, functools.partial(_zero_map, a.ndim), pipeline_mode=pl.Buffered(1))
                 for a in full_ins]
    out_specs = [pl.BlockSpec((tm, w), lambda i: (i, 0)) for (w, _) in row_outs]
    out_specs += [pl.BlockSpec(s, functools.partial(_zero_map, len(s))) for (s, _) in acc_outs]
    out_shape = [jax.ShapeDtypeStruct((n_rows, w), dt) for (w, dt) in row_outs]
    out_shape += [jax.ShapeDtypeStruct(s, dt) for (s, dt) in acc_outs]
    return pl.pallas_call(kern, grid=(n_rows // tm,), in_specs=in_specs, out_specs=out_specs,
                          out_shape=out_shape, name=name, compiler_params=_params(("arbitrary",)))(
                              *row_ins, *full_ins)


def _acc(step, ref, val):
    @pl.when(step == 0)
    def _():
        ref[...] = val

    @pl.when(step != 0)
    def _():
        ref[...] += val


def _ffn_fwd(name, x, w_in4, w_out, ln_g, ln_b, tm):
    T, D = x.shape
    fh = w_in4.shape[2]

    def body(i, x_ref, w4_ref, wo_ref, g_ref, b_ref, h_ref, z_ref, a_ref):
        xv = x_ref[...]
        xb = _bf(xv)
        f = jnp.zeros((tm, D), F32)
        for k in range(2):
            gk = _dot(xb, w4_ref[k])
            uk = _dot(xb, w4_ref[2 + k])
            a_ref[:, k * fh:(k + 1) * fh] = _bf(gk)
            a_ref[:, (2 + k) * fh:(3 + k) * fh] = _bf(uk)
            f += _dot(_bf(gk * _sigmoid(gk) * uk), wo_ref[k * fh:(k + 1) * fh, :])
        z = ALPHA * xv + 0.5 * f
        xhat, _ = _ln_stats(z)
        z_ref[...] = z
        h_ref[...] = xhat * g_ref[...] + b_ref[...]

    return _rowcall(name, body, T, tm, [x], [w_in4, w_out, ln_g, ln_b],
                    [(D, F32), (D, F32), (4 * fh, BF16)])


def _ffn_bwd(name, dh, z, a, w_in4, w_out, ln_g, tm):
    T, D = dh.shape
    fh = w_in4.shape[2]

    def body(i, dh_ref, z_ref, a_ref, w4_ref, wo_ref, g_ref, dx_ref, da_ref, s_ref, df_ref, dg_ref, db_ref):
        xhat, rstd = _ln_stats(z_ref[...])
        dz, dg, db = _ln_bwd(dh_ref[...], xhat, rstd, g_ref[...])
        _acc(i, dg_ref, dg)
        _acc(i, db_ref, db)
        dfb = _bf(0.5 * dz)
        df_ref[...] = dfb
        dx = ALPHA * dz
        for k in range(2):
            gk = a_ref[:, k * fh:(k + 1) * fh].astype(F32)
            uk = a_ref[:, (2 + k) * fh:(3 + k) * fh].astype(F32)
            ds = _dot_nt(dfb, wo_ref[k * fh:(k + 1) * fh, :])
            sig = _sigmoid(gk)
            silu = gk * sig
            dgk = _bf(ds * uk * sig * (1.0 + gk * (1.0 - sig)))
            duk = _bf(ds * silu)
            s_ref[:, k * fh:(k + 1) * fh] = _bf(silu * uk)
            da_ref[:, k * fh:(k + 1) * fh] = dgk
            da_ref[:, (2 + k) * fh:(3 + k) * fh] = duk
            dx += _dot_nt(dgk, w4_ref[k]) + _dot_nt(duk, w4_ref[2 + k])
        dx_ref[...] = dx

    return _rowcall(name, body, T, tm, [dh, z, a], [w_in4, w_out, ln_g],
                    [(D, F32), (4 * fh, BF16), (2 * fh, BF16), (D, BF16)],
                    [((1, D), F32), ((1, D), F32)])


def _mm_tn(name, a, b, out_dtype=BF16, n_split=1):
    T, M = a.shape
    N = b.shape[1]
    tk = _tile(T, 512, 8)
    tm = _tile(M, 1408)
    tn = _tile(N // n_split, 1536)
    per = N // n_split // tn
    nk = T // tk
    if n_split > 1:
        out_spec = pl.BlockSpec((None, tm, tn), lambda i, j, k: (j // per, i, j % per))
        out_shape = jax.ShapeDtypeStruct((n_split, M, N // n_split), out_dtype)
    else:
        out_spec = pl.BlockSpec((tm, tn), lambda i, j, k: (i, j))
        out_shape = jax.ShapeDtypeStruct((M, N), out_dtype)

    def kern(a_ref, b_ref, o_ref, acc_ref):
        k = pl.program_id(2)
        part = _dot_tn(_bf(a_ref[...]), _bf(b_ref[...]))

        @pl.when(k == 0)
        def _():
            acc_ref[...] = part

        @pl.when(k != 0)
        def _():
            acc_ref[...] += part

        @pl.when(k == nk - 1)
        def _():
            o_ref[...] = acc_ref[...].astype(out_dtype)

    return pl.pallas_call(
        kern, grid=(M // tm, N // tn, nk),
        in_specs=[pl.BlockSpec((tk, tm), lambda i, j, k: (k, i)), pl.BlockSpec((tk, tn), lambda i, j, k: (k, j))],
        out_specs=out_spec, out_shape=out_shape,
        scratch_shapes=[pltpu.VMEM((tm, tn), F32)], name=name,
        compiler_params=_params(("arbitrary", "arbitrary", "arbitrary")))(a, b)


def _proj_ret(h1, w_r, cos_r, sin_r, tm):
    T, D = h1.shape
    qk = RET_HEADS * RET_DK
    rv = RET_HEADS * RET_DV

    def body(i, h_ref, cos_ref, sin_ref, w_ref, q_ref, k_ref, v_ref, g_ref):
        hb = _bf(h_ref[...])
        cos, sin = cos_ref[...], sin_ref[...]
        for out_ref, off, scale in ((q_ref, 0, 1.0), (k_ref, qk, RET_DK ** -0.5)):
            pr = _dot(hb, w_ref[:, off:off + qk])
            for h in range(RET_HEADS):
                t = pr[:, h * RET_DK:(h + 1) * RET_DK]
                out_ref[:, h * RET_DK:(h + 1) * RET_DK] = _bf((t * cos + _roll(t, RET_DK // 2) * sin) * scale)
        v_ref[...] = _bf(_dot(hb, w_ref[:, 2 * qk:2 * qk + rv]))
        g_ref[...] = _dot(hb, w_ref[:, 2 * qk + rv:2 * qk + 2 * rv])

    return _rowcall("proj_ret", body, T, tm, [h1, cos_r, sin_r], [w_r],
                    [(qk, BF16), (qk, BF16), (rv, BF16), (rv, F32)])


def _rope_pe(t, c, s1, s2):
    return t * c + _roll(t, LANES - MLA_ROPE // 2) * s1 + _roll(t, MLA_ROPE // 2) * s2


def _rope_pe_bwd(dy, c, s1, s2):
    return dy * c + _roll(dy * s1, MLA_ROPE // 2) + _roll(dy * s2, LANES - MLA_ROPE // 2)


def _rms(x, g):
    r = lax.rsqrt(_mean(x * x) + EPS)
    return x * r, r


def _proj_mla(h1, tabs, w_c, w_kpe, w_g, w_uq, w_uk, w_uv, qn_g, kvn_g, tm):
    T, D = h1.shape
    H = MLA_HEADS

    def body(i, h_ref, c_ref, s1_ref, s2_ref, wc_ref, wk_ref, wg_ref, wuq_ref, wuk_ref, wuv_ref, qg_ref, kg_ref,
             lat_ref, gt_ref, q_ref, k_ref, v_ref, ln_ref):
        hb = _bf(h_ref[...])
        c, s1, s2 = c_ref[...], s1_ref[...], s2_ref[...]
        lat = _dot(hb, wc_ref[...])
        lat_ref[...] = lat
        gt_ref[...] = _dot(hb, wg_ref[...])
        cqn, _ = _rms(lat[:, :Q_LORA], None)
        ckn, _ = _rms(lat[:, Q_LORA:], None)
        cqn = _bf(cqn * qg_ref[...])
        ckn = _bf(ckn * kg_ref[...])
        ln_ref[:, :Q_LORA] = cqn
        ln_ref[:, Q_LORA:] = ckn
        q = _dot(cqn, wuq_ref[...])
        kn = _dot(ckn, wuk_ref[...])
        v_ref[...] = _bf(_dot(ckn, wuv_ref[...]))
        kpe = _bf(_rope_pe(_dot(hb, wk_ref[...]), c, s1, s2))
        for h in range(H):
            o = h * MLA_QK
            q_ref[:, o:o + MLA_NOPE] = _bf(q[:, o:o + MLA_NOPE])
            q_ref[:, o + MLA_NOPE:o + MLA_QK] = _bf(_rope_pe(q[:, o + MLA_NOPE:o + MLA_QK], c, s1, s2))
            k_ref[:, o:o + MLA_NOPE] = _bf(kn[:, h * MLA_NOPE:(h + 1) * MLA_NOPE])
            k_ref[:, o + MLA_NOPE:o + MLA_QK] = kpe

    lat_w = Q_LORA + KV_LORA
    return _rowcall("proj_mla", body, T, tm, [h1, *tabs], [w_c, w_kpe, w_g, w_uq, w_uk, w_uv, qn_g, kvn_g],
                    [(lat_w, F32), (2 * D, F32), (H * MLA_QK, BF16), (H * MLA_QK, BF16), (H * MLA_DV, BF16),
                     (lat_w, BF16)])


def _ret_block(T):
    return min(256, T)


def _ret_decay(lg, bt):
    n = lax.broadcasted_iota(jnp.int32, (bt, bt), 0)
    m = lax.broadcasted_iota(jnp.int32, (bt, bt), 1)
    dmat = jnp.where(_chunk_of(m) <= _chunk_of(n), jnp.exp(lg * jnp.abs(n - m).astype(F32)), 0.0)
    pos = lax.broadcasted_iota(jnp.int32, (bt, 1), 0).astype(F32)
    xi = jnp.exp(lg * (pos + 1.0))
    zeta = jnp.exp(lg * (bt - 1.0 - pos))
    return dmat, xi, zeta, jnp.exp(lg * bt)


def _ret_specs(bt, rev, nb):
    def blk(w):
        if rev:
            return pl.BlockSpec((bt, w), lambda h, b: (nb - 1 - b, h))
        return pl.BlockSpec((bt, w), lambda h, b: (b, h))
    return pl.BlockSpec((None, 1, LANES), lambda h, b: (h, 0, 0)), blk


def _ret_fwd(rq, rk, rv, lgam):
    T = rq.shape[0]
    bt = _ret_block(T)
    nb = T // bt
    lg_spec, blk = _ret_specs(bt, False, nb)

    def kern(lg_ref, q_ref, k_ref, v_ref, y_ref, s_ref):
        @pl.when(pl.program_id(1) == 0)
        def _():
            s_ref[...] = jnp.zeros_like(s_ref)

        dmat, xi, zeta, gb = _ret_decay(lg_ref[:, :1], bt)
        q, k, v = q_ref[...], k_ref[...], v_ref[...]
        sc = _dot_nt(q, k) * dmat
        y_ref[...] = _dot(_bf(sc), v) + _dot(q, _bf(s_ref[...])) * xi
        s_ref[...] = s_ref[...] * gb + _dot_tn(_bf(k.astype(F32) * zeta), v)

    return pl.pallas_call(
        kern, grid=(RET_HEADS, nb), in_specs=[lg_spec, blk(RET_DK), blk(RET_DK), blk(RET_DV)],
        out_specs=blk(RET_DV), out_shape=jax.ShapeDtypeStruct((T, RET_HEADS * RET_DV), F32),
        scratch_shapes=[pltpu.VMEM((RET_DK, RET_DV), F32)], name="ret_fwd",
        compiler_params=_params(("arbitrary", "arbitrary")))(lgam, rq, rk, rv)


def _ret_bwd_q(rq, rk, rv, dy, lgam):
    T = rq.shape[0]
    bt = _ret_block(T)
    nb = T // bt
    lg_spec, blk = _ret_specs(bt, False, nb)

    def kern(lg_ref, k_ref, v_ref, dy_ref, dq_ref, s_ref):
        @pl.when(pl.program_id(1) == 0)
        def _():
            s_ref[...] = jnp.zeros_like(s_ref)

        dmat, xi, zeta, gb = _ret_decay(lg_ref[:, :1], bt)
        k, v, dy = k_ref[...], v_ref[...], dy_ref[...]
        dp = _dot_nt(dy, v) * dmat
        dq_ref[...] = _dot(_bf(dp), k) + _dot_nt(dy, _bf(s_ref[...])) * xi
        s_ref[...] = s_ref[...] * gb + _dot_tn(_bf(k.astype(F32) * zeta), v)

    return pl.pallas_call(
        kern, grid=(RET_HEADS, nb), in_specs=[lg_spec, blk(RET_DK), blk(RET_DV), blk(RET_DV)],
        out_specs=blk(RET_DK), out_shape=jax.ShapeDtypeStruct((T, RET_HEADS * RET_DK), F32),
        scratch_shapes=[pltpu.VMEM((RET_DK, RET_DV), F32)], name="ret_bwd_q",
        compiler_params=_params(("arbitrary", "arbitrary")))(lgam, rk, rv, dy)


def _ret_bwd_kv(rq, rk, rv, dy, lgam):
    T = rq.shape[0]
    bt = _ret_block(T)
    nb = T // bt
    lg_spec, blk = _ret_specs(bt, True, nb)

    def kern(lg_ref, q_ref, k_ref, v_ref, dy_ref, dk_ref, dv_ref, g_ref):
        @pl.when(pl.program_id(1) == 0)
        def _():
            g_ref[...] = jnp.zeros_like(g_ref)

        dmat, xi, zeta, gb = _ret_decay(lg_ref[:, :1], bt)
        q, k, v, dy = q_ref[...], k_ref[...], v_ref[...], dy_ref[...]
        gs = _bf(g_ref[...])
        p = _dot_nt(q, k) * dmat
        dp = _dot_nt(dy, v) * dmat
        dv_ref[...] = _bf(_dot_tn(_bf(p), dy) + _dot(k, gs) * zeta)
        dk_ref[...] = _dot_tn(_bf(dp), q) + _dot_nt(v, gs) * zeta
        g_ref[...] = g_ref[...] * gb + _dot_tn(_bf(q.astype(F32) * xi), dy)

    return pl.pallas_call(
        kern, grid=(RET_HEADS, nb), in_specs=[lg_spec, blk(RET_DK), blk(RET_DK), blk(RET_DV), blk(RET_DV)],
        out_specs=[blk(RET_DK), blk(RET_DV)],
        out_shape=[jax.ShapeDtypeStruct((T, RET_HEADS * RET_DK), F32),
                   jax.ShapeDtypeStruct((T, RET_HEADS * RET_DV), BF16)],
        scratch_shapes=[pltpu.VMEM((RET_DK, RET_DV), F32)], name="ret_bwd_kv",
        compiler_params=_params(("arbitrary", "arbitrary")))(lgam, rq, rk, rv, dy)


def _attn_block(T):
    return min(512, T)


def _attn_mask(tb):
    r = lax.broadcasted_iota(jnp.int32, (tb, tb), 0)
    c = lax.broadcasted_iota(jnp.int32, (tb, tb), 1)
    return _chunk_of(c) <= _chunk_of(r)


ATTN_SCALE = (MLA_NOPE + MLA_ROPE) ** -0.5
MASKED = -1e30


def _attn_fwd(q, k, v):
    T = q.shape[0]
    tb = _attn_block(T)
    nb = T // tb

    def kern(q_ref, k_ref, v_ref, o_ref, lse_ref, m_ref, l_ref, acc_ref):
        qb = pl.program_id(1)
        qv = q_ref[...]
        m_ref[...] = jnp.full_like(m_ref, MASKED)
        l_ref[...] = jnp.zeros_like(l_ref)
        acc_ref[...] = jnp.zeros_like(acc_ref)

        def step(kb, diagonal):
            rows = pl.ds(pl.multiple_of(kb * tb, tb), tb)
            s = _dot_nt(qv, k_ref[rows, :]) * ATTN_SCALE
            if diagonal:
                s = jnp.where(_attn_mask(tb), s, MASKED)
            m_old = m_ref[...]
            m_new = jnp.maximum(m_old, jnp.max(s, axis=-1, keepdims=True))
            p = jnp.exp(s - m_new)
            corr = jnp.exp(m_old - m_new)
            l_ref[...] = l_ref[...] * corr + jnp.sum(p, axis=-1, keepdims=True)
            acc_ref[...] = acc_ref[...] * corr + _dot(_bf(p), v_ref[rows, :])
            m_ref[...] = m_new

        def loop_body(kb, carry):
            step(kb, False)
            return carry

        lax.fori_loop(0, qb, loop_body, 0)
        step(qb, True)
        o_ref[...] = acc_ref[...] / l_ref[...]
        lse_ref[...] = jnp.broadcast_to(m_ref[...] + jnp.log(l_ref[...]), (tb, LANES))

    return pl.pallas_call(
        kern, grid=(MLA_HEADS, nb),
        in_specs=[pl.BlockSpec((tb, MLA_QK), lambda h, i: (i, h)), pl.BlockSpec((T, MLA_QK), lambda h, i: (0, h)),
                  pl.BlockSpec((T, MLA_DV), lambda h, i: (0, h))],
        out_specs=[pl.BlockSpec((tb, MLA_DV), lambda h, i: (i, h)), pl.BlockSpec((tb, LANES), lambda h, i: (i, h))],
        out_shape=[jax.ShapeDtypeStruct((T, MLA_HEADS * MLA_DV), F32),
                   jax.ShapeDtypeStruct((T, MLA_HEADS * LANES), F32)],
        scratch_shapes=[pltpu.VMEM((tb, 1), F32), pltpu.VMEM((tb, 1), F32), pltpu.VMEM((tb, MLA_DV), F32)],
        name="attn_fwd", compiler_params=_params(("arbitrary", "arbitrary")))(q, k, v)


def _attn_bwd_q(q, k, v, do, o, lse):
    T = q.shape[0]
    tb = _attn_block(T)
    nb = T // tb

    def kern(q_ref, k_ref, v_ref, do_ref, o_ref, lse_ref, dq_ref, dl_ref):
        qb = pl.program_id(1)
        qv, dov = q_ref[...], do_ref[...]
        lse = lse_ref[:, :1]
        delta = jnp.sum(dov.astype(F32) * o_ref[...], axis=-1, keepdims=True)
        dl_ref[...] = jnp.broadcast_to(delta, (tb, LANES))
        dq_ref[...] = jnp.zeros_like(dq_ref)

        def step(kb, diagonal):
            rows = pl.ds(pl.multiple_of(kb * tb, tb), tb)
            kv = k_ref[rows, :]
            s = _dot_nt(qv, kv) * ATTN_SCALE
            if diagonal:
                s = jnp.where(_attn_mask(tb), s, MASKED)
            p = jnp.exp(s - lse)
            ds = p * (_dot_nt(dov, v_ref[rows, :]) - delta) * ATTN_SCALE
            dq_ref[...] += _dot(_bf(ds), kv)

        def loop_body(kb, carry):
            step(kb, False)
            return carry

        lax.fori_loop(0, qb, loop_body, 0)
        step(qb, True)

    def blk(w):
        return pl.BlockSpec((tb, w), lambda h, i: (i, h))

    def full(w):
        return pl.BlockSpec((T, w), lambda h, i: (0, h))

    return pl.pallas_call(
        kern, grid=(MLA_HEADS, nb),
        in_specs=[blk(MLA_QK), full(MLA_QK), full(MLA_DV), blk(MLA_DV), blk(MLA_DV), blk(LANES)],
        out_specs=[blk(MLA_QK), blk(LANES)],
        out_shape=[jax.ShapeDtypeStruct((T, MLA_HEADS * MLA_QK), F32),
                   jax.ShapeDtypeStruct((T, MLA_HEADS * LANES), F32)],
        name="attn_bwd_q", compiler_params=_params(("arbitrary", "arbitrary")))(q, k, v, do, o, lse)


def _attn_bwd_kv(q, k, v, do, lse, delta):
    T = q.shape[0]
    tb = _attn_block(T)
    nb = T // tb

    def kern(q_ref, k_ref, v_ref, do_ref, lse_ref, dl_ref, dk_ref, dv_ref, dv_acc):
        kb = pl.program_id(1)
        kv, vv = k_ref[...], v_ref[...]
        dk_ref[...] = jnp.zeros_like(dk_ref)
        dv_acc[...] = jnp.zeros_like(dv_acc)

        def step(qb, diagonal):
            rows = pl.ds(pl.multiple_of(qb * tb, tb), tb)
            qv, dov = q_ref[rows, :], do_ref[rows, :]
            s = _dot_nt(qv, kv) * ATTN_SCALE
            if diagonal:
                s = jnp.where(_attn_mask(tb), s, MASKED)
            p = jnp.exp(s - lse_ref[rows, :][:, :1])
            dv_acc[...] += _dot_tn(_bf(p), dov)
            ds = p * (_dot_nt(dov, vv) - dl_ref[rows, :][:, :1]) * ATTN_SCALE
            dk_ref[...] += _dot_tn(_bf(ds), qv)

        def loop_body(qb, carry):
            step(qb, False)
            return carry

        step(kb, True)
        lax.fori_loop(kb + 1, nb, loop_body, 0)
        dv_ref[...] = _bf(dv_acc[...])

    def blk(w):
        return pl.BlockSpec((tb, w), lambda h, i: (i, h))

    def full(w):
        return pl.BlockSpec((T, w), lambda h, i: (0, h))

    return pl.pallas_call(
        kern, grid=(MLA_HEADS, nb),
        in_specs=[full(MLA_QK), blk(MLA_QK), blk(MLA_DV), full(MLA_DV), full(LANES), full(LANES)],
        out_specs=[blk(MLA_QK), blk(MLA_DV)],
        out_shape=[jax.ShapeDtypeStruct((T, MLA_HEADS * MLA_QK), F32),
                   jax.ShapeDtypeStruct((T, MLA_HEADS * MLA_DV), BF16)],
        scratch_shapes=[pltpu.VMEM((tb, MLA_DV), F32)],
        name="attn_bwd_kv", compiler_params=_params(("arbitrary", "arbitrary")))(q, k, v, do, lse, delta)


def _group_norm(y):
    yc = y - _mean(y)
    rstd = lax.rsqrt(_mean(yc * yc) + EPS)
    return yc * rstd, rstd


def _mix_fwd(y, rg, o, gates, h1, gn_g, w_ret_o, w_mla_o, w_out, ln_g, ln_b, tm):
    T, D = h1.shape

    def body(i, y_ref, rg_ref, o_ref, gt_ref, h_ref, gn_ref, wr_ref, wm_ref, wo_ref, g_ref, b_ref,
             h2_ref, z_ref, yret_ref, ymla_ref, yr_ref, mix_ref):
        for h in range(RET_HEADS):
            sl = slice(h * RET_DV, (h + 1) * RET_DV)
            yn, _ = _group_norm(y_ref[:, sl])
            r = rg_ref[:, sl]
            yr_ref[:, sl] = _bf(r * _sigmoid(r) * (yn * gn_ref[:, sl]))
        yret = _dot(yr_ref[...], wr_ref[...])
        ymla = _dot(_bf(o_ref[...]), wm_ref[...])
        yret_ref[...] = yret
        ymla_ref[...] = ymla
        mix = _bf(_sigmoid(gt_ref[:, :D]) * yret + _sigmoid(gt_ref[:, D:]) * ymla)
        mix_ref[...] = mix
        z = ALPHA * h_ref[...] + _dot(mix, wo_ref[...])
        xhat, _ = _ln_stats(z)
        z_ref[...] = z
        h2_ref[...] = xhat * g_ref[...] + b_ref[...]

    return _rowcall("mix_fwd", body, T, tm, [y, rg, o, gates, h1], [gn_g, w_ret_o, w_mla_o, w_out, ln_g, ln_b],
                    [(D, F32), (D, F32), (D, F32), (D, F32), (RET_HEADS * RET_DV, BF16), (D, BF16)])


def _mix_bwd(dh2, z1, gates, yret, ymla, y, rg, gn_g, w_ret_o, w_mla_o, w_out, ln_g, tm):
    T, D = dh2.shape
    rv = RET_HEADS * RET_DV

    def body(i, dh_ref, z_ref, gt_ref, yret_ref, ymla_ref, y_ref, rg_ref, gn_ref, wr_ref, wm_ref, wo_ref, g_ref,
             dz_ref, dgt_ref, drg_ref, dy_ref, do_ref, dyret_ref, dymla_ref, dg_ref, db_ref, dgn_ref):
        xhat, rstd = _ln_stats(z_ref[...])
        dz, dg, db = _ln_bwd(dh_ref[...], xhat, rstd, g_ref[...])
        _acc(i, dg_ref, dg)
        _acc(i, db_ref, db)
        dz_ref[...] = dz
        dmix = _dot_nt(_bf(dz), wo_ref[...])
        sr = _sigmoid(gt_ref[:, :D])
        sm = _sigmoid(gt_ref[:, D:])
        dgt_ref[:, :D] = _bf(dmix * yret_ref[...] * sr * (1.0 - sr))
        dgt_ref[:, D:] = _bf(dmix * ymla_ref[...] * sm * (1.0 - sm))
        dyret = _bf(dmix * sr)
        dymla = _bf(dmix * sm)
        dyret_ref[...] = dyret
        dymla_ref[...] = dymla
        do_ref[...] = _bf(_dot_nt(dymla, wm_ref[...]))
        dyr = _dot_nt(dyret, wr_ref[...])
        dgn = []
        for h in range(RET_HEADS):
            sl = slice(h * RET_DV, (h + 1) * RET_DV)
            yn, grstd = _group_norm(y_ref[:, sl])
            r = rg_ref[:, sl]
            sig = _sigmoid(r)
            d = dyr[:, sl]
            drg_ref[:, sl] = _bf(d * (yn * gn_ref[:, sl]) * sig * (1.0 + r * (1.0 - sig)))
            dt = d * (r * sig)
            dgn.append(jnp.sum(dt * yn, axis=0, keepdims=True))
            dyn = dt * gn_ref[:, sl]
            dy_ref[:, sl] = _bf(grstd * (dyn - _mean(dyn) - yn * _mean(dyn * yn)))
        _acc(i, dgn_ref, jnp.concatenate(dgn, axis=1))

    return _rowcall("mix_bwd", body, T, tm, [dh2, z1, gates, yret, ymla, y, rg], [gn_g, w_ret_o, w_mla_o, w_out, ln_g],
                    [(D, F32), (2 * D, BF16), (rv, BF16), (rv, BF16), (MLA_HEADS * MLA_DV, BF16), (D, BF16), (D, BF16)],
                    [((1, D), F32), ((1, D), F32), ((1, rv), F32)])


def _proj_mla_bwd(dq, dk, dv, lat, tabs, w_uq, w_uk, w_uv, qn_g, kvn_g, tm):
    T = dq.shape[0]
    H = MLA_HEADS
    lat_w = Q_LORA + KV_LORA

    def body(i, dq_ref, dk_ref, dv_ref, lat_ref, c_ref, s1_ref, s2_ref, wuq_ref, wuk_ref, wuv_ref, qg_ref, kg_ref,
             dlat_ref, dkpe_ref, dqb_ref, dkn_ref, dqg_ref, dkg_ref):
        c, s1, s2 = c_ref[...], s1_ref[...], s2_ref[...]
        dkpe = jnp.zeros((tm, LANES), F32)
        for h in range(H):
            o = h * MLA_QK
            dqb_ref[:, o:o + MLA_NOPE] = _bf(dq_ref[:, o:o + MLA_NOPE])
            dqb_ref[:, o + MLA_NOPE:o + MLA_QK] = _bf(_rope_pe_bwd(dq_ref[:, o + MLA_NOPE:o + MLA_QK], c, s1, s2))
            dkn_ref[:, h * MLA_NOPE:(h + 1) * MLA_NOPE] = _bf(dk_ref[:, o:o + MLA_NOPE])
            dkpe += dk_ref[:, o + MLA_NOPE:o + MLA_QK]
        dkpe_ref[...] = _bf(_rope_pe_bwd(dkpe, c, s1, s2))
        dcqn = _dot_nt(dqb_ref[...], wuq_ref[...])
        dckn = _dot_nt(dkn_ref[...], wuk_ref[...]) + _dot_nt(dv_ref[...], wuv_ref[...])
        for dn, x, g_ref, dg_ref, sl in ((dcqn, lat_ref[:, :Q_LORA], qg_ref, dqg_ref, slice(0, Q_LORA)),
                                         (dckn, lat_ref[:, Q_LORA:], kg_ref, dkg_ref, slice(Q_LORA, lat_w))):
            xn, r = _rms(x, None)
            _acc(i, dg_ref, jnp.sum(dn * xn, axis=0, keepdims=True))
            dxn = dn * g_ref[...]
            dlat_ref[:, sl] = _bf(r * (dxn - xn * _mean(dxn * xn)))

    return _rowcall("proj_mla_bwd", body, T, tm, [dq, dk, dv, lat, *tabs], [w_uq, w_uk, w_uv, qn_g, kvn_g],
                    [(lat_w, BF16), (LANES, BF16), (H * MLA_QK, BF16), (H * MLA_NOPE, BF16)],
                    [((1, Q_LORA), F32), ((1, KV_LORA), F32)])


def _proj_bwd(drq, drk, drv, drg, dz1, dlat, dkpe, dgates, cos_r, sin_r, w_r, w_c, w_kpe, w_g, tm):
    T, D = dz1.shape
    qk = RET_HEADS * RET_DK
    rv = RET_HEADS * RET_DV

    def body(i, drq_ref, drk_ref, drv_ref, drg_ref, dz_ref, dlat_ref, dkpe_ref, dgt_ref, cos_ref, sin_ref,
             wr_ref, wc_ref, wk_ref, wg_ref, dh_ref, dpr_ref):
        cos, sin = cos_ref[...], sin_ref[...]
        for src, off, scale in ((drq_ref, 0, 1.0), (drk_ref, qk, RET_DK ** -0.5)):
            for h in range(RET_HEADS):
                d = src[:, h * RET_DK:(h + 1) * RET_DK]
                dpr_ref[:, off + h * RET_DK:off + (h + 1) * RET_DK] = _bf(
                    (d * cos + _roll(d * sin, RET_DK // 2)) * scale)
        dpr_ref[:, 2 * qk:2 * qk + rv] = drv_ref[...]
        dpr_ref[:, 2 * qk + rv:] = drg_ref[...]
        dh_ref[...] = (ALPHA * dz_ref[...] + _dot_nt(dpr_ref[...], wr_ref[...]) + _dot_nt(dlat_ref[...], wc_ref[...])
                       + _dot_nt(dkpe_ref[...], wk_ref[...]) + _dot_nt(dgt_ref[...], wg_ref[...]))

    return _rowcall("proj_bwd", body, T, tm, [drq, drk, drv, drg, dz1, dlat, dkpe, dgates, cos_r, sin_r],
                    [w_r, w_c, w_kpe, w_g], [(D, F32), (2 * qk + 2 * rv, BF16)])


def _ple_loss(h3, p, target, w_gate, w_proj, ln_g, ln_b, tm):
    T, D = h3.shape

    def body(i, h_ref, p_ref, t_ref, wg_ref, wp_ref, g_ref, b_ref, dh_ref, dgp_ref, dpp_ref, loss_ref, dg_ref, db_ref):
        hv = h_ref[...]
        sg = _sigmoid(_dot(_bf(hv), wg_ref[...]))
        pp = _dot(_bf(p_ref[...]), wp_ref[...])
        xhat, rstd = _ln_stats(ALPHA * hv + sg * pp)
        err = xhat * g_ref[...] + b_ref[...] - t_ref[...]
        row_loss = 0.5 * _mean(err * err)
        _acc(i, loss_ref, jnp.broadcast_to(jnp.sum(row_loss, axis=0, keepdims=True), (1, LANES)))
        dz, dg, db = _ln_bwd(err * (1.0 / D), xhat, rstd, g_ref[...])
        _acc(i, dg_ref, dg)
        _acc(i, db_ref, db)
        dgp = _bf(dz * pp * sg * (1.0 - sg))
        dgp_ref[...] = dgp
        dpp_ref[...] = _bf(dz * sg)
        dh_ref[...] = ALPHA * dz + _dot_nt(dgp, wg_ref[...])

    return _rowcall("ple_loss", body, T, tm, [h3, p, target], [w_gate, w_proj, ln_g, ln_b],
                    [(D, F32), (D, BF16), (D, BF16)], [((1, LANES), F32), ((1, D), F32), ((1, D), F32)])


def _ewise(name, fn, ins, n_out, out_dtype=F32):
    r, c = ins[0].shape
    tr = _tile(r, max(8, (1 << 19) // c // 8 * 8), 8)

    def kern(*refs):
        outs = fn(*[x[...] for x in refs[:len(ins)]])
        for o_ref, o in zip(refs[len(ins):], outs):
            o_ref[...] = o.astype(out_dtype)

    spec = pl.BlockSpec((tr, c), lambda i: (i, 0))
    return pl.pallas_call(kern, grid=(r // tr,), in_specs=[spec] * len(ins), out_specs=[spec] * n_out,
                          out_shape=[jax.ShapeDtypeStruct((r, c), out_dtype)] * n_out, name=name,
                          compiler_params=_params(("arbitrary",)))(*ins)


def _adamw_math(w, g, m, v):
    m = ADAM_B1 * m + (1.0 - ADAM_B1) * g
    v = ADAM_B2 * v + (1.0 - ADAM_B2) * (g * g)
    m_hat = m / (1.0 - ADAM_B1 ** ADAM_STEP)
    v_hat = v / (1.0 - ADAM_B2 ** ADAM_STEP)
    return -ADAM_LR * (m_hat / (jnp.sqrt(v_hat) + ADAM_EPS) + ADAM_WD * w), m, v


def _adamw(name, w, g, m, v):
    shape = w.shape
    c = shape[-1]
    flat = [t.reshape(-1, c) for t in (w, g, m, v)]
    return [t.reshape(shape) for t in _ewise(name, _adamw_math, flat, 3)]


def _place():
    return lax.axis_index("x"), lax.axis_index("y"), lax.axis_index("c")


def _dma_sems(n):
    return [pltpu.SemaphoreType.DMA((n,)), pltpu.SemaphoreType.DMA((n,))]


N_PEER_CHIPS = N_CHIPS - 1


def _chips_exchange(name, srcs, broadcast):
    n = len(srcs)

    def kern(*refs):
        src_refs, out_refs = refs[:n], refs[n:2 * n]
        send_sems, recv_sems = refs[2 * n:]
        x, y, c = _place()
        peers = [(1 - x, y), (x, 1 - y), (1 - x, 1 - y)]

        def copy(a, j):
            px, py = peers[j]
            piece = src_refs[a].at[c] if broadcast else src_refs[a].at[2 * px + py]
            return pltpu.make_async_remote_copy(
                src_ref=piece, dst_ref=out_refs[a].at[j], send_sem=send_sems.at[a * N_PEER_CHIPS + j],
                recv_sem=recv_sems.at[a * N_PEER_CHIPS + j], device_id=(px, py, c), device_id_type=MESH)

        cps = [copy(a, j) for j in range(N_PEER_CHIPS) for a in range(n)]
        for cp in cps:
            cp.start()
        for cp in cps:
            cp.wait_recv()
        for cp in cps:
            cp.wait_send()

    return pl.pallas_call(
        kern, out_shape=[jax.ShapeDtypeStruct((N_PEER_CHIPS,) + s.shape[1:], s.dtype) for s in srcs],
        in_specs=[HBM_SPEC] * n, out_specs=[HBM_SPEC] * n, scratch_shapes=_dma_sems(n * N_PEER_CHIPS),
        name=name)(*srcs)


def _sibling_swap(name, srcs, halves):
    n = len(srcs)

    def kern(*refs):
        src_refs, out_refs = refs[:n], refs[n:2 * n]
        send_sems, recv_sems = refs[2 * n:]
        x, y, c = _place()

        def copy(a):
            piece = src_refs[a].at[:, 1 - c] if halves else src_refs[a]
            return pltpu.make_async_remote_copy(
                src_ref=piece, dst_ref=out_refs[a], send_sem=send_sems.at[a], recv_sem=recv_sems.at[a],
                device_id=(x, y, 1 - c), device_id_type=MESH)

        cps = [copy(a) for a in range(n)]
        for cp in cps:
            cp.start()
        for cp in cps:
            cp.wait_recv()
        for cp in cps:
            cp.wait_send()

    def out_shape(s):
        return jax.ShapeDtypeStruct((s.shape[0],) + s.shape[2:] if halves else s.shape, s.dtype)

    return pl.pallas_call(
        kern, out_shape=[out_shape(s) for s in srcs], in_specs=[HBM_SPEC] * n, out_specs=[HBM_SPEC] * n,
        scratch_shapes=_dma_sems(n), name=name)(*srcs)


def _all_devices(name, src, reduce):
    r, c = src.shape
    n_dev = 2 * N_CHIPS

    def kern(src_ref, out_ref, *scratch):
        if reduce:
            gat_ref, send_sems, recv_sems = scratch
        else:
            gat_ref = out_ref
            send_sems, recv_sems = scratch
        x, y, cc = _place()
        me = 4 * x + 2 * y + cc
        gat_ref[me] = src_ref[...]
        peers = []
        for j in range(1, n_dev):
            px = 1 - x if j & 4 else x
            py = 1 - y if j & 2 else y
            pc = 1 - cc if j & 1 else cc
            peers.append((px, py, pc))

        def copy(j, peer, slot):
            return pltpu.make_async_remote_copy(
                src_ref=src_ref, dst_ref=gat_ref.at[slot], send_sem=send_sems.at[j], recv_sem=recv_sems.at[j],
                device_id=peer, device_id_type=MESH)

        sends = [copy(j, peer, me) for j, peer in enumerate(peers)]
        for cp in sends:
            cp.start()
        for j, (px, py, pc) in enumerate(peers):
            copy(j, (px, py, pc), 4 * px + 2 * py + pc).wait_recv()
        for cp in sends:
            cp.wait_send()
        if reduce:
            total = gat_ref[0]
            for d in range(1, n_dev):
                total = total + gat_ref[d]
            out_ref[...] = total

    out_shape = jax.ShapeDtypeStruct((r, c) if reduce else (n_dev, r, c), src.dtype)
    scratch = ([pltpu.VMEM((n_dev, r, c), src.dtype)] if reduce else []) + _dma_sems(n_dev - 1)
    return pl.pallas_call(kern, out_shape=out_shape, in_specs=[VMEM_SPEC], out_specs=VMEM_SPEC,
                          scratch_shapes=scratch, name=name)(src)


ROW_GROUP = ("ffn1_w_out", "w_ret_o", "w_mla_o", "w_out", "ffn2_w_out", "ple_w_gate")
COL_GROUP = ("w_uq", "w_ukv", "ple_w_proj")


def _to_groups(t, axis):
    return [jnp.concatenate([t[n] for n in ROW_GROUP], axis=axis), t["ffn1_w_in"], t["ffn2_w_in"], t["w_in"],
            jnp.concatenate([t[n] for n in COL_GROUP], axis=axis + 1)]


def _from_groups(groups, shapes, axis):
    rows, ffn1, ffn2, w_in, cols = groups
    out = {"ffn1_w_in": ffn1, "ffn2_w_in": ffn2, "w_in": w_in}
    off = 0
    for n in ROW_GROUP:
        out[n] = lax.slice_in_dim(rows, off, off + shapes[n][0], axis=axis)
        off += shapes[n][0]
    off = 0
    for n in COL_GROUP:
        out[n] = lax.slice_in_dim(cols, off, off + shapes[n][1], axis=axis + 1)
        off += shapes[n][1]
    return out


def _halves(t, axis):
    return t.reshape(t.shape[:axis] + (2, t.shape[axis] // 2) + t.shape[axis + 1:])


def _by_core(mine, theirs, axis):
    c = lax.axis_index("c")
    both = jnp.where(c == 0, jnp.stack([mine, theirs], axis), jnp.stack([theirs, mine], axis))
    return both.reshape(both.shape[:axis] + (2 * both.shape[axis + 1],) + both.shape[axis + 2:])


def _chip_order(own, others):
    me = 2 * lax.axis_index("x") + lax.axis_index("y")
    cands = jnp.concatenate([own[None], others], axis=0)
    slot_of_flip = (0, 2, 1, 3)
    pick = jnp.asarray(slot_of_flip, jnp.int32)[jnp.arange(N_CHIPS, dtype=jnp.int32) ^ me]
    return jnp.stack([lax.dynamic_index_in_dim(cands, pick[k], 0, keepdims=False) for k in range(N_CHIPS)])


def _join_shards(name, shards):
    _, r, c = shards.shape
    if name in COL_SHARDED:
        return shards.transpose(1, 0, 2).reshape(r, N_CHIPS * c)
    return shards.reshape(N_CHIPS * r, c)


def _split_shards(name, full):
    if full.ndim == 3:
        return full
    r, c = full.shape
    if name in COL_SHARDED:
        return jnp.stack([full[:, k * (c // N_CHIPS):(k + 1) * (c // N_CHIPS)] for k in range(N_CHIPS)])
    return full.reshape(N_CHIPS, r // N_CHIPS, c)


def _rope_tables(positions):
    pos = positions.reshape(-1).astype(F32)[:, None]
    half = RET_DK // 2
    ang = pos * (ROPE_BASE ** (-jnp.arange(half, dtype=F32) / half))
    cos_r = jnp.concatenate([jnp.cos(ang)] * 2, axis=1)
    sin_r = jnp.concatenate([-jnp.sin(ang), jnp.sin(ang)], axis=1)
    half = MLA_ROPE // 2
    ang = pos * (ROPE_BASE ** (-jnp.arange(half, dtype=F32) / half))
    zeros = jnp.zeros_like(ang)
    rest = LANES - MLA_ROPE
    c = jnp.concatenate([jnp.cos(ang)] * 2 + [jnp.ones((ang.shape[0], rest), F32)], axis=1)
    s1 = jnp.concatenate([-jnp.sin(ang), zeros, jnp.zeros((ang.shape[0], rest), F32)], axis=1)
    s2 = jnp.concatenate([zeros, jnp.sin(ang), jnp.zeros((ang.shape[0], rest), F32)], axis=1)
    return cos_r, sin_r, (c, s1, s2)


def _local_step(x, p, positions, target, w, ln_g, ln_b, gn_g, qn_g, kvn_g):
    T, D = x.shape
    tm = min(256, T)
    H = MLA_HEADS
    qk, rv = RET_HEADS * RET_DK, RET_HEADS * RET_DV
    cos_r, sin_r, tabs = _rope_tables(positions)
    lgam = jnp.broadcast_to(jnp.log(1.0 - 2.0 ** (-5.0 - jnp.arange(RET_HEADS, dtype=F32)))[:, None, None],
                            (RET_HEADS, 1, LANES))
    lng = [ln_g[k:k + 1] for k in range(N_LN)]
    lnb = [ln_b[k:k + 1] for k in range(N_LN)]

    w_in = w["w_in"]
    o_lat, o_kpe, o_gate = 2 * qk + 2 * rv, 2 * qk + 2 * rv + Q_LORA + KV_LORA, 2 * qk + 2 * rv + Q_LORA + KV_LORA + MLA_ROPE
    w_r, w_c = w_in[:, :o_lat], w_in[:, o_lat:o_kpe]
    w_kpe = jnp.pad(w_in[:, o_kpe:o_gate], ((0, 0), (0, LANES - MLA_ROPE)))
    w_g = w_in[:, o_gate:]
    w_uq = jnp.pad(w["w_uq"].reshape(Q_LORA, H, MLA_NOPE + MLA_ROPE),
                   ((0, 0), (0, 0), (0, MLA_QK - MLA_NOPE - MLA_ROPE))).reshape(Q_LORA, H * MLA_QK)
    w_ukv = w["w_ukv"].reshape(KV_LORA, H, MLA_NOPE + MLA_DV)
    w_uk = w_ukv[:, :, :MLA_NOPE].reshape(KV_LORA, H * MLA_NOPE)
    w_uv = w_ukv[:, :, MLA_NOPE:].reshape(KV_LORA, H * MLA_DV)

    h1, z0, a1 = _ffn_fwd("ffn1_fwd", x, w["ffn1_w_in"], w["ffn1_w_out"], lng[0], lnb[0], tm)
    rq, rk, rvv, rg = _proj_ret(h1, w_r, cos_r, sin_r, tm)
    lat, gates, q, k, v, latn = _proj_mla(h1, tabs, w_c, w_kpe, w_g, w_uq, w_uk, w_uv, qn_g, kvn_g, tm)
    y = _ret_fwd(rq, rk, rvv, lgam)
    o, lse = _attn_fwd(q, k, v)
    h2, z1, yret, ymla, yr, mix = _mix_fwd(y, rg, o, gates, h1, gn_g, w["w_ret_o"], w["w_mla_o"], w["w_out"],
                                           lng[1], lnb[1], tm)
    h3, z2, a2 = _ffn_fwd("ffn2_fwd", h2, w["ffn2_w_in"], w["ffn2_w_out"], lng[2], lnb[2], tm)

    dh3, dgp, dpp, loss, dg3, db3 = _ple_loss(h3, p, target, w["ple_w_gate"], w["ple_w_proj"], lng[3], lnb[3], tm)
    dh2, da2, s2, df2, dg2, db2 = _ffn_bwd("ffn2_bwd", dh3, z2, a2, w["ffn2_w_in"], w["ffn2_w_out"], lng[2], tm)
    (dz1, dgates, drg, dy, do, dyret, dymla, dg1, db1, dgn) = _mix_bwd(
        dh2, z1, gates, yret, ymla, y, rg, gn_g, w["w_ret_o"], w["w_mla_o"], w["w_out"], lng[1], tm)
    drq = _ret_bwd_q(rq, rk, rvv, dy, lgam)
    drk, drv = _ret_bwd_kv(rq, rk, rvv, dy, lgam)
    dq, delta = _attn_bwd_q(q, k, v, do, o, lse)
    dk, dv = _attn_bwd_kv(q, k, v, do, lse, delta)
    dlat, dkpe, dqb, dkn, dqg, dkg = _proj_mla_bwd(dq, dk, dv, lat, tabs, w_uq, w_uk, w_uv, qn_g, kvn_g, tm)
    dh1, dpr = _proj_bwd(drq, drk, drv, drg, dz1, dlat, dkpe, dgates, cos_r, sin_r, w_r, w_c, w_kpe, w_g, tm)
    dx, da1, s1, df1, dg0, db0 = _ffn_bwd("ffn1_bwd", dh1, z0, a1, w["ffn1_w_in"], w["ffn1_w_out"], lng[0], tm)

    g_uq = _mm_tn("wg_uq", latn[:, :Q_LORA], dqb).reshape(Q_LORA, H, MLA_QK)[:, :, :MLA_NOPE + MLA_ROPE]
    g_uk = _mm_tn("wg_uk", latn[:, Q_LORA:], dkn).reshape(KV_LORA, H, MLA_NOPE)
    g_uv = _mm_tn("wg_uv", latn[:, Q_LORA:], dv).reshape(KV_LORA, H, MLA_DV)
    grads = {
        "ffn1_w_in": _mm_tn("wg_ffn1_in", x, da1, n_split=N_CHIPS),
        "ffn1_w_out": _mm_tn("wg_ffn1_out", s1, df1),
        "w_in": jnp.concatenate([_mm_tn("wg_in_r", h1, dpr), _mm_tn("wg_in_c", h1, dlat),
                                 _mm_tn("wg_in_kpe", h1, dkpe)[:, :MLA_ROPE], _mm_tn("wg_in_g", h1, dgates)], axis=1),
        "w_ret_o": _mm_tn("wg_ret_o", yr, dyret),
        "w_uq": g_uq.reshape(Q_LORA, H * (MLA_NOPE + MLA_ROPE)),
        "w_ukv": jnp.concatenate([g_uk, g_uv], axis=2).reshape(KV_LORA, H * (MLA_NOPE + MLA_DV)),
        "w_mla_o": _mm_tn("wg_mla_o", o, dymla),
        "w_out": _mm_tn("wg_out", mix, dz1),
        "ffn2_w_in": _mm_tn("wg_ffn2_in", h2, da2, n_split=N_CHIPS),
        "ffn2_w_out": _mm_tn("wg_ffn2_out", s2, df2),
        "ple_w_gate": _mm_tn("wg_ple_gate", h3, dgp),
        "ple_w_proj": _mm_tn("wg_ple_proj", p, dpp),
    }
    small = {"ln_g": jnp.concatenate([dg0, dg1, dg2, dg3], axis=0), "ln_b": jnp.concatenate([db0, db1, db2, db3], axis=0),
             "ret_gn_g": dgn, "q_norm_g": dqg, "kv_norm_g": dkg}
    return loss[0, 0], dx, grads, small


def _gather_weights(shards):
    own = _to_groups({n: _bf(shards[n]) for n in BIG_WEIGHTS}, 0)
    mine = _chips_exchange("gather_chips", [_halves(g, 0) for g in own], True)
    theirs = _sibling_swap("gather_cores", mine, False)
    groups = [_chip_order(g, _by_core(m, t, 1)) for g, m, t in zip(own, mine, theirs)]
    parts = _from_groups(groups, {n: shards[n].shape for n in BIG_WEIGHTS}, 1)
    return {n: t if n in ("ffn1_w_in", "ffn2_w_in") else _join_shards(n, t) for n, t in parts.items()}


def _reduce_grads(grads, shapes):
    c = lax.axis_index("c")
    me = 2 * lax.axis_index("x") + lax.axis_index("y")
    groups = [_halves(g, 1) for g in _to_groups({n: _split_shards(n, grads[n]) for n in BIG_WEIGHTS}, 1)]
    theirs = _sibling_swap("reduce_cores", groups, True)
    chip_sums = []
    for i, (g, t) in enumerate(zip(groups, theirs)):
        mine = lax.dynamic_index_in_dim(g, c, axis=1, keepdims=False)
        k, r, cc = mine.shape
        chip_sums.append(_ewise("reduce_cores_add%d" % i, lambda a, b: (a.astype(F32) + b.astype(F32),),
                                [mine.reshape(k * r, cc), t.reshape(k * r, cc)], 1, BF16)[0].reshape(k, r, cc))
    parts = _chips_exchange("reduce_chips", chip_sums, False)
    totals = []
    for i, (s, pt) in enumerate(zip(chip_sums, parts)):
        own = lax.dynamic_index_in_dim(s, me, axis=0, keepdims=False)
        totals.append(_ewise("reduce_chips_add%d" % i,
                             lambda a, b, c_, d: (((a.astype(F32) + b.astype(F32)) + c_.astype(F32)) + d.astype(F32),),
                             [own, pt[0], pt[1], pt[2]], 1, F32)[0])
    others = _sibling_swap("reduce_join", totals, False)
    return _from_groups([_by_core(t, o, 0) for t, o in zip(totals, others)], shapes, 0)


def kernel(x, p, positions, ln_g, ln_b, ffn1_w_in, ffn1_w_out, w_in, ret_gn_g, w_ret_o, q_norm_g, kv_norm_g, w_uq, w_ukv, w_mla_o, w_out, ffn2_w_in, ffn2_w_out, ple_w_gate, ple_w_proj, loss_target, m_ln_g, m_ln_b, m_ffn1_w_in, m_ffn1_w_out, m_w_in, m_ret_gn_g, m_w_ret_o, m_q_norm_g, m_kv_norm_g, m_w_uq, m_w_ukv, m_w_mla_o, m_w_out, m_ffn2_w_in, m_ffn2_w_out, m_ple_w_gate, m_ple_w_proj, v_ln_g, v_ln_b, v_ffn1_w_in, v_ffn1_w_out, v_w_in, v_ret_gn_g, v_w_ret_o, v_q_norm_g, v_kv_norm_g, v_w_uq, v_w_ukv, v_w_mla_o, v_w_out, v_ffn2_w_in, v_ffn2_w_out, v_ple_w_gate, v_ple_w_proj):
    names = ("ln_g", "ln_b", "ffn1_w_in", "ffn1_w_out", "w_in", "ret_gn_g", "w_ret_o", "q_norm_g", "kv_norm_g", "w_uq",
             "w_ukv", "w_mla_o", "w_out", "ffn2_w_in", "ffn2_w_out", "ple_w_gate", "ple_w_proj")
    weights = dict(zip(names, (ln_g, ln_b, ffn1_w_in, ffn1_w_out, w_in, ret_gn_g, w_ret_o, q_norm_g, kv_norm_g, w_uq,
                               w_ukv, w_mla_o, w_out, ffn2_w_in, ffn2_w_out, ple_w_gate, ple_w_proj)))
    m_in = dict(zip(names, (m_ln_g, m_ln_b, m_ffn1_w_in, m_ffn1_w_out, m_w_in, m_ret_gn_g, m_w_ret_o, m_q_norm_g,
                            m_kv_norm_g, m_w_uq, m_w_ukv, m_w_mla_o, m_w_out, m_ffn2_w_in, m_ffn2_w_out, m_ple_w_gate,
                            m_ple_w_proj)))
    v_in = dict(zip(names, (v_ln_g, v_ln_b, v_ffn1_w_in, v_ffn1_w_out, v_w_in, v_ret_gn_g, v_w_ret_o, v_q_norm_g,
                            v_kv_norm_g, v_w_uq, v_w_ukv, v_w_mla_o, v_w_out, v_ffn2_w_in, v_ffn2_w_out, v_ple_w_gate,
                            v_ple_w_proj)))
    chip = 2 * lax.axis_index("x") + lax.axis_index("y")
    D = x.shape[-1]
    dq = D // N_CHIPS

    shards = {n: weights[n][0] for n in BIG_WEIGHTS}
    w = _gather_weights(shards)
    ln_all = _all_devices("gather_ln", jnp.concatenate([ln_g[0], ln_b[0]], axis=0), False)
    ln_full = ln_all[::2].transpose(1, 0, 2).reshape(2 * N_LN, D)
    loss, dx, grads, small = _local_step(x[0], p[0, 0], positions, loss_target[0], w, ln_full[:N_LN], ln_full[N_LN:],
                                         ret_gn_g, q_norm_g, kv_norm_g)

    loss = lax.psum(loss, ("x", "y", "c"))
    big = _reduce_grads(grads, {n: shards[n].shape for n in BIG_WEIGHTS})
    small_names = ("ln_g", "ln_b", "ret_gn_g", "q_norm_g", "kv_norm_g")
    flat = jnp.concatenate([small[n].reshape(-1) for n in small_names])
    rows = -(-flat.shape[0] // LANES // 8) * 8
    flat = jnp.pad(flat, (0, rows * LANES - flat.shape[0])).reshape(rows, LANES)
    flat = _all_devices("reduce_small", flat, True).reshape(-1)
    off = 0
    for n in small_names:
        size = small[n].size
        small[n] = flat[off:off + size].reshape(small[n].shape)
        off += size
    g_out = dict(big)
    for n in ("ln_g", "ln_b"):
        g_out[n] = lax.dynamic_slice_in_dim(small[n], chip * dq, dq, axis=1)
    for n in ("ret_gn_g", "q_norm_g", "kv_norm_g"):
        g_out[n] = small[n]

    deltas, new_m, new_v = {}, {}, {}
    for n in names:
        g = g_out[n].reshape(weights[n].shape)
        g_out[n] = g
        deltas[n], new_m[n], new_v[n] = _adamw("adamw_" + n, weights[n], g, m_in[n], v_in[n])
    return (loss, dx[None], *[g_out[n] for n in names], *[deltas[n] for n in names], *[new_m[n] for n in names],
            *[new_v[n] for n in names])
```

```python
import functools

import jax
import jax.numpy as jnp
from jax import lax
from jax.experimental import pallas as pl
from jax.experimental.pallas import tpu as pltpu

D_MODEL = 1024
CHUNK = 64
D_PLE = 256
D_FF = 2816
RET_HEADS = 8
RET_DK = 128
RET_DV = 256
MLA_HEADS = 8
MLA_NOPE = 128
MLA_ROPE = 64
MLA_DV = 128
MLA_QK = 256
Q_LORA = 256
KV_LORA = 256
ROPE_BASE = 10000.0
EPS = 1e-5
N_LN = 4
ALPHA = 2.0 ** 0.25
ADAM_LR = 0.001
ADAM_B1 = 0.9
ADAM_B2 = 0.999
ADAM_EPS = 1e-08
ADAM_WD = 0.01
ADAM_STEP = 10

LANES = 128
VMEM_LIMIT = 60 << 20
N_CHIPS = 4

F32 = jnp.float32
BF16 = jnp.bfloat16
MESH = pl.DeviceIdType.MESH
HBM_SPEC = pl.BlockSpec(memory_space=pltpu.HBM)
VMEM_SPEC = pl.BlockSpec(memory_space=pltpu.VMEM)

BIG_WEIGHTS = ("ffn1_w_in", "ffn1_w_out", "w_in", "w_ret_o", "w_uq", "w_ukv", "w_mla_o", "w_out",
               "ffn2_w_in", "ffn2_w_out", "ple_w_gate", "ple_w_proj")
COL_SHARDED = ("ffn1_w_in", "w_in", "w_uq", "w_ukv", "ffn2_w_in", "ple_w_proj")


def _dot(a, b):
    return jnp.dot(a, b, preferred_element_type=F32)


def _dot_nt(a, b):
    return lax.dot_general(a, b, (((1,), (1,)), ((), ())), preferred_element_type=F32)


def _dot_tn(a, b):
    return lax.dot_general(a, b, (((0,), (0,)), ((), ())), preferred_element_type=F32)


def _bf(x):
    return x.astype(BF16)


def _sigmoid(x):
    return 1.0 / (1.0 + jnp.exp(-x))


def _mean(x):
    return jnp.mean(x, axis=-1, keepdims=True)


def _ln_stats(z):
    zc = z - _mean(z)
    rstd = lax.rsqrt(_mean(zc * zc) + EPS)
    return zc * rstd, rstd


def _ln_bwd(dy, xhat, rstd, g):
    dxhat = dy * g
    dz = rstd * (dxhat - _mean(dxhat) - xhat * _mean(dxhat * xhat))
    return dz, jnp.sum(dy * xhat, axis=0, keepdims=True), jnp.sum(dy, axis=0, keepdims=True)


def _roll(x, shift):
    return pltpu.roll(x, shift, 1)


def _chunk_of(idx):
    return jnp.right_shift(idx, CHUNK.bit_length() - 1)


def _tile(n, cap, mult=LANES):
    if n <= cap:
        return n
    for t in range(cap - cap % mult, 0, -mult):
        if n % t == 0:
            return t
    return n


def _zero_map(nd, *_):
    return (0,) * nd


def _params(sem):
    return pltpu.CompilerParams(dimension_semantics=sem, vmem_limit_bytes=VMEM_LIMIT)


def _rowcall(name, body, n_rows, tm, row_ins, full_ins, row_outs, acc_outs=(), tiled_outs=()):
    def kern(*refs):
        body(pl.program_id(0), *refs)

    in_specs = [pl.BlockSpec((tm, a.shape[1]), lambda i: (i, 0)) for a in row_ins]
    in_specs += [pl.BlockSpec(a.shape, functools.partial(_zero_map, a.ndim), pipeline_mode=pl.Buffered(1))
                 for a in full_ins]
    out_specs = [pl.BlockSpec((tm, w), lambda i: (i, 0)) for (w, _) in row_outs]
    out_specs += [pl.BlockSpec(s, functools.partial(_zero_map, len(s))) for (s, _) in acc_outs]
    out_specs += [spec for (_, spec) in tiled_outs]
    out_shape = [jax.ShapeDtypeStruct((n_rows, w), dt) for (w, dt) in row_outs]
    out_shape += [jax.ShapeDtypeStruct(s, dt) for (s, dt) in acc_outs]
    out_shape += [shape for (shape, _) in tiled_outs]
    return pl.pallas_call(kern, grid=(n_rows // tm,), in_specs=in_specs, out_specs=out_specs,
                          out_shape=out_shape, name=name, compiler_params=_params(("arbitrary",)))(
                              *row_ins, *full_ins)


def _acc(step, ref, val):
    @pl.when(step == 0)
    def _():
        ref[...] = val

    @pl.when(step != 0)
    def _():
        ref[...] += val


def _ffn_fwd(name, x, w_in4, w_out, ln_g, ln_b, tm):
    T, D = x.shape
    fh = w_in4.shape[2]

    def body(i, x_ref, w4_ref, wo_ref, g_ref, b_ref, h_ref, z_ref, a_ref):
        xv = x_ref[...]
        xb = _bf(xv)
        f = jnp.zeros((tm, D), F32)
        for k in range(2):
            gk = _dot(xb, w4_ref[k])
            uk = _dot(xb, w4_ref[2 + k])
            a_ref[:, k * fh:(k + 1) * fh] = _bf(gk)
            a_ref[:, (2 + k) * fh:(3 + k) * fh] = _bf(uk)
            f += _dot(_bf(gk * _sigmoid(gk) * uk), wo_ref[k * fh:(k + 1) * fh, :])
        z = ALPHA * xv + 0.5 * f
        xhat, _ = _ln_stats(z)
        z_ref[...] = z
        h_ref[...] = xhat * g_ref[...] + b_ref[...]

    return _rowcall(name, body, T, tm, [x], [w_in4, w_out, ln_g, ln_b],
                    [(D, F32), (D, F32), (4 * fh, BF16)])


def _ffn_bwd(name, dh, z, a, w_in4, w_out, ln_g, tm):
    T, D = dh.shape
    fh = w_in4.shape[2]

    def body(i, dh_ref, z_ref, a_ref, w4_ref, wo_ref, g_ref, dx_ref, da_ref, s_ref, df_ref, dg_ref, db_ref):
        xhat, rstd = _ln_stats(z_ref[...])
        dz, dg, db = _ln_bwd(dh_ref[...], xhat, rstd, g_ref[...])
        _acc(i, dg_ref, dg)
        _acc(i, db_ref, db)
        dfb = _bf(0.5 * dz)
        df_ref[...] = dfb
        dx = ALPHA * dz
        for k in range(2):
            gk = a_ref[:, k * fh:(k + 1) * fh].astype(F32)
            uk = a_ref[:, (2 + k) * fh:(3 + k) * fh].astype(F32)
            ds = _dot_nt(dfb, wo_ref[k * fh:(k + 1) * fh, :])
            sig = _sigmoid(gk)
            silu = gk * sig
            dgk = _bf(ds * uk * sig * (1.0 + gk * (1.0 - sig)))
            duk = _bf(ds * silu)
            s_ref[:, k * fh:(k + 1) * fh] = _bf(silu * uk)
            da_ref[:, k * fh:(k + 1) * fh] = dgk
            da_ref[:, (2 + k) * fh:(3 + k) * fh] = duk
            dx += _dot_nt(dgk, w4_ref[k]) + _dot_nt(duk, w4_ref[2 + k])
        dx_ref[...] = dx

    return _rowcall(name, body, T, tm, [dh, z, a], [w_in4, w_out, ln_g],
                    [(D, F32), (4 * fh, BF16), (2 * fh, BF16), (D, BF16)],
                    [((1, D), F32), ((1, D), F32)])


def _mm_tn(name, a, b, out_dtype=BF16, n_split=1):
    T, M = a.shape
    N = b.shape[1]
    tk = _tile(T, 512, 8)
    tm = _tile(M, 1408)
    tn = _tile(N // n_split, 1536)
    per = N // n_split // tn
    nk = T // tk
    if n_split > 1:
        out_spec = pl.BlockSpec((None, tm, tn), lambda i, j, k: (j // per, i, j % per))
        out_shape = jax.ShapeDtypeStruct((n_split, M, N // n_split), out_dtype)
    else:
        out_spec = pl.BlockSpec((tm, tn), lambda i, j, k: (i, j))
        out_shape = jax.ShapeDtypeStruct((M, N), out_dtype)

    def kern(a_ref, b_ref, o_ref, acc_ref):
        k = pl.program_id(2)
        part = _dot_tn(_bf(a_ref[...]), _bf(b_ref[...]))

        @pl.when(k == 0)
        def _():
            acc_ref[...] = part

        @pl.when(k != 0)
        def _():
            acc_ref[...] += part

        @pl.when(k == nk - 1)
        def _():
            o_ref[...] = acc_ref[...].astype(out_dtype)

    return pl.pallas_call(
        kern, grid=(M // tm, N // tn, nk),
        in_specs=[pl.BlockSpec((tk, tm), lambda i, j, k: (k, i)), pl.BlockSpec((tk, tn), lambda i, j, k: (k, j))],
        out_specs=out_spec, out_shape=out_shape,
        scratch_shapes=[pltpu.VMEM((tm, tn), F32)], name=name,
        compiler_params=_params(("arbitrary", "arbitrary", "arbitrary")))(a, b)


def _proj_ret(h1, w_r, cos_r, sin_r, tm):
    T, D = h1.shape
    qk = RET_HEADS * RET_DK
    rv = RET_HEADS * RET_DV

    def body(i, h_ref, cos_ref, sin_ref, w_ref, q_ref, k_ref, v_ref, g_ref):
        hb = _bf(h_ref[...])
        cos, sin = cos_ref[...], sin_ref[...]
        for out_ref, off, scale in ((q_ref, 0, 1.0), (k_ref, qk, RET_DK ** -0.5)):
            pr = _dot(hb, w_ref[:, off:off + qk])
            for h in range(RET_HEADS):
                t = pr[:, h * RET_DK:(h + 1) * RET_DK]
                out_ref[:, h * RET_DK:(h + 1) * RET_DK] = _bf((t * cos + _roll(t, RET_DK // 2) * sin) * scale)
        v_ref[...] = _bf(_dot(hb, w_ref[:, 2 * qk:2 * qk + rv]))
        g_ref[...] = _dot(hb, w_ref[:, 2 * qk + rv:2 * qk + 2 * rv])

    return _rowcall("proj_ret", body, T, tm, [h1, cos_r, sin_r], [w_r],
                    [(qk, BF16), (qk, BF16), (rv, BF16), (rv, F32)])


def _rope_pe(t, c, s1, s2):
    return t * c + _roll(t, LANES - MLA_ROPE // 2) * s1 + _roll(t, MLA_ROPE // 2) * s2


def _rope_pe_bwd(dy, c, s1, s2):
    return dy * c + _roll(dy * s1, MLA_ROPE // 2) + _roll(dy * s2, LANES - MLA_ROPE // 2)


def _rms(x, g):
    r = lax.rsqrt(_mean(x * x) + EPS)
    return x * r, r


def _attn_block(T):
    return min(512, T)


def _transposed_blocks(T, tm, w, dtype):
    tb = _attn_block(T)
    per = tb // tm
    return (jax.ShapeDtypeStruct((T // tb, MLA_HEADS, w, tb), dtype),
            pl.BlockSpec((None, MLA_HEADS, w, tm), lambda i: (i // per, 0, 0, i % per)))


def _proj_mla(h1, tabs, w_c, w_kpe, w_g, w_uq, w_uk, w_uv, qn_g, kvn_g, tm):
    T, D = h1.shape
    H = MLA_HEADS

    def body(i, h_ref, c_ref, s1_ref, s2_ref, wc_ref, wk_ref, wg_ref, wuq_ref, wuk_ref, wuv_ref, qg_ref, kg_ref,
             lat_ref, gt_ref, q_ref, k_ref, v_ref, ln_ref, qt_ref, vt_ref):
        hb = _bf(h_ref[...])
        c, s1, s2 = c_ref[...], s1_ref[...], s2_ref[...]
        lat = _dot(hb, wc_ref[...])
        lat_ref[...] = lat
        gt_ref[...] = _dot(hb, wg_ref[...])
        cqn, _ = _rms(lat[:, :Q_LORA], None)
        ckn, _ = _rms(lat[:, Q_LORA:], None)
        cqn = _bf(cqn * qg_ref[...])
        ckn = _bf(ckn * kg_ref[...])
        ln_ref[:, :Q_LORA] = cqn
        ln_ref[:, Q_LORA:] = ckn
        q = _dot(cqn, wuq_ref[...])
        kn = _dot(ckn, wuk_ref[...])
        vv = _dot(ckn, wuv_ref[...])
        v_ref[...] = _bf(vv)
        kpe = _bf(_rope_pe(_dot(hb, wk_ref[...]), c, s1, s2))
        for h in range(H):
            o = h * MLA_QK
            qh = jnp.concatenate([q[:, o:o + MLA_NOPE], _rope_pe(q[:, o + MLA_NOPE:o + MLA_QK], c, s1, s2)], axis=1)
            q_ref[:, o:o + MLA_QK] = _bf(qh)
            qt_ref[h] = _bf(qh.T)
            vt_ref[h] = _bf(vv[:, h * MLA_DV:(h + 1) * MLA_DV].T)
            k_ref[:, o:o + MLA_NOPE] = _bf(kn[:, h * MLA_NOPE:(h + 1) * MLA_NOPE])
            k_ref[:, o + MLA_NOPE:o + MLA_QK] = kpe

    lat_w = Q_LORA + KV_LORA
    return _rowcall("proj_mla", body, T, tm, [h1, *tabs], [w_c, w_kpe, w_g, w_uq, w_uk, w_uv, qn_g, kvn_g],
                    [(lat_w, F32), (2 * D, F32), (H * MLA_QK, BF16), (H * MLA_QK, BF16), (H * MLA_DV, BF16),
                     (lat_w, BF16)],
                    tiled_outs=[_transposed_blocks(T, tm, MLA_QK, BF16), _transposed_blocks(T, tm, MLA_DV, BF16)])


def _ret_block(T):
    return min(256, T)


def _ret_decay(lg, bt):
    n = lax.broadcasted_iota(jnp.int32, (bt, bt), 0)
    m = lax.broadcasted_iota(jnp.int32, (bt, bt), 1)
    dmat = jnp.where(_chunk_of(m) <= _chunk_of(n), jnp.exp(lg * jnp.abs(n - m).astype(F32)), 0.0)
    pos = lax.broadcasted_iota(jnp.int32, (bt, 1), 0).astype(F32)
    xi = jnp.exp(lg * (pos + 1.0))
    zeta = jnp.exp(lg * (bt - 1.0 - pos))
    return dmat, xi, zeta, jnp.exp(lg * bt)


def _ret_specs(bt, rev, nb):
    def blk(w):
        if rev:
            return pl.BlockSpec((bt, w), lambda h, b: (nb - 1 - b, h))
        return pl.BlockSpec((bt, w), lambda h, b: (b, h))
    return pl.BlockSpec((None, 1, LANES), lambda h, b: (h, 0, 0)), blk


def _ret_fwd(rq, rk, rv, lgam):
    T = rq.shape[0]
    bt = _ret_block(T)
    nb = T // bt
    lg_spec, blk = _ret_specs(bt, False, nb)

    def kern(lg_ref, q_ref, k_ref, v_ref, y_ref, s_ref):
        @pl.when(pl.program_id(1) == 0)
        def _():
            s_ref[...] = jnp.zeros_like(s_ref)

        dmat, xi, zeta, gb = _ret_decay(lg_ref[:, :1], bt)
        q, k, v = q_ref[...], k_ref[...], v_ref[...]
        sc = _dot_nt(q, k) * dmat
        y_ref[...] = _dot(_bf(sc), v) + _dot(q, _bf(s_ref[...])) * xi
        s_ref[...] = s_ref[...] * gb + _dot_tn(_bf(k.astype(F32) * zeta), v)

    return pl.pallas_call(
        kern, grid=(RET_HEADS, nb), in_specs=[lg_spec, blk(RET_DK), blk(RET_DK), blk(RET_DV)],
        out_specs=blk(RET_DV), out_shape=jax.ShapeDtypeStruct((T, RET_HEADS * RET_DV), F32),
        scratch_shapes=[pltpu.VMEM((RET_DK, RET_DV), F32)], name="ret_fwd",
        compiler_params=_params(("arbitrary", "arbitrary")))(lgam, rq, rk, rv)


def _ret_bwd_q(rq, rk, rv, dy, lgam):
    T = rq.shape[0]
    bt = _ret_block(T)
    nb = T // bt
    lg_spec, blk = _ret_specs(bt, False, nb)

    def kern(lg_ref, k_ref, v_ref, dy_ref, dq_ref, s_ref):
        @pl.when(pl.program_id(1) == 0)
        def _():
            s_ref[...] = jnp.zeros_like(s_ref)

        dmat, xi, zeta, gb = _ret_decay(lg_ref[:, :1], bt)
        k, v, dy = k_ref[...], v_ref[...], dy_ref[...]
        dp = _dot_nt(dy, v) * dmat
        dq_ref[...] = _dot(_bf(dp), k) + _dot_nt(dy, _bf(s_ref[...])) * xi
        s_ref[...] = s_ref[...] * gb + _dot_tn(_bf(k.astype(F32) * zeta), v)

    return pl.pallas_call(
        kern, grid=(RET_HEADS, nb), in_specs=[lg_spec, blk(RET_DK), blk(RET_DV), blk(RET_DV)],
        out_specs=blk(RET_DK), out_shape=jax.ShapeDtypeStruct((T, RET_HEADS * RET_DK), F32),
        scratch_shapes=[pltpu.VMEM((RET_DK, RET_DV), F32)], name="ret_bwd_q",
        compiler_params=_params(("arbitrary", "arbitrary")))(lgam, rk, rv, dy)


def _ret_bwd_kv(rq, rk, rv, dy, lgam):
    T = rq.shape[0]
    bt = _ret_block(T)
    nb = T // bt
    lg_spec, blk = _ret_specs(bt, True, nb)

    def kern(lg_ref, q_ref, k_ref, v_ref, dy_ref, dk_ref, dv_ref, g_ref):
        @pl.when(pl.program_id(1) == 0)
        def _():
            g_ref[...] = jnp.zeros_like(g_ref)

        dmat, xi, zeta, gb = _ret_decay(lg_ref[:, :1], bt)
        q, k, v, dy = q_ref[...], k_ref[...], v_ref[...], dy_ref[...]
        gs = _bf(g_ref[...])
        p = _dot_nt(q, k) * dmat
        dp = _dot_nt(dy, v) * dmat
        dv_ref[...] = _bf(_dot_tn(_bf(p), dy) + _dot(k, gs) * zeta)
        dk_ref[...] = _dot_tn(_bf(dp), q) + _dot_nt(v, gs) * zeta
        g_ref[...] = g_ref[...] * gb + _dot_tn(_bf(q.astype(F32) * xi), dy)

    return pl.pallas_call(
        kern, grid=(RET_HEADS, nb), in_specs=[lg_spec, blk(RET_DK), blk(RET_DK), blk(RET_DV), blk(RET_DV)],
        out_specs=[blk(RET_DK), blk(RET_DV)],
        out_shape=[jax.ShapeDtypeStruct((T, RET_HEADS * RET_DK), F32),
                   jax.ShapeDtypeStruct((T, RET_HEADS * RET_DV), BF16)],
        scratch_shapes=[pltpu.VMEM((RET_DK, RET_DV), F32)], name="ret_bwd_kv",
        compiler_params=_params(("arbitrary", "arbitrary")))(lgam, rq, rk, rv, dy)


def _attn_mask(tb):
    r = lax.broadcasted_iota(jnp.int32, (tb, tb), 0)
    c = lax.broadcasted_iota(jnp.int32, (tb, tb), 1)
    return _chunk_of(c) <= _chunk_of(r)


def _attn_mask_t(tb):
    key = lax.broadcasted_iota(jnp.int32, (tb, tb), 0)
    qry = lax.broadcasted_iota(jnp.int32, (tb, tb), 1)
    return _chunk_of(key) <= _chunk_of(qry)


ATTN_SCALE = (MLA_NOPE + MLA_ROPE) ** -0.5
MASKED = -1e30
SUBLANES = 8


def _head_blocks(nb, w, tb):
    return pl.BlockSpec((nb, None, w, tb), lambda h, i: (0, h, 0, 0))


def _one_block(w, tb):
    return pl.BlockSpec((None, None, w, tb), lambda h, i: (i, h, 0, 0))


def _attn_fwd(k, qt, vt):
    T = k.shape[0]
    tb = _attn_block(T)
    nb = T // tb

    def kern(qt_ref, k_ref, vt_ref, o_ref, lse_ref, lser_ref, m_ref, l_ref, acc_ref):
        qb = pl.program_id(1)
        qt = qt_ref[...]
        m_ref[...] = jnp.full_like(m_ref, MASKED)
        l_ref[...] = jnp.zeros_like(l_ref)
        acc_ref[...] = jnp.zeros_like(acc_ref)

        def step(kb, diagonal):
            rows = pl.ds(pl.multiple_of(kb * tb, tb), tb)
            s = _dot(k_ref[rows, :], qt) * ATTN_SCALE
            if diagonal:
                s = jnp.where(_attn_mask_t(tb), s, MASKED)
            m_old = m_ref[...]
            m_new = jnp.maximum(m_old, jnp.max(s, axis=0, keepdims=True))
            p = jnp.exp(s - m_new)
            corr = jnp.exp(m_old - m_new)
            l_ref[...] = l_ref[...] * corr + jnp.sum(p, axis=0, keepdims=True)
            acc_ref[...] = acc_ref[...] * corr + _dot(vt_ref[kb], _bf(p))
            m_ref[...] = m_new

        def loop_body(kb, carry):
            step(kb, False)
            return carry

        lax.fori_loop(0, qb, loop_body, 0)
        step(qb, True)
        o_ref[...] = (acc_ref[...] / l_ref[...]).T
        lse = m_ref[...] + jnp.log(l_ref[...])
        lser_ref[...] = jnp.broadcast_to(lse, (SUBLANES, tb))
        lse_ref[...] = jnp.broadcast_to(lse, (LANES, tb)).T

    return pl.pallas_call(
        kern, grid=(MLA_HEADS, nb),
        in_specs=[_one_block(MLA_QK, tb), pl.BlockSpec((T, MLA_QK), lambda h, i: (0, h)),
                  _head_blocks(nb, MLA_DV, tb)],
        out_specs=[pl.BlockSpec((tb, MLA_DV), lambda h, i: (i, h)), pl.BlockSpec((tb, LANES), lambda h, i: (i, h)),
                   _one_block(SUBLANES, tb)],
        out_shape=[jax.ShapeDtypeStruct((T, MLA_HEADS * MLA_DV), F32),
                   jax.ShapeDtypeStruct((T, MLA_HEADS * LANES), F32),
                   jax.ShapeDtypeStruct((nb, MLA_HEADS, SUBLANES, tb), F32)],
        scratch_shapes=[pltpu.VMEM((1, tb), F32), pltpu.VMEM((1, tb), F32), pltpu.VMEM((MLA_DV, tb), F32)],
        name="attn_fwd", compiler_params=_params(("arbitrary", "arbitrary")))(qt, k, vt)


def _attn_bwd_q(q, k, v, do, o, lse):
    T = q.shape[0]
    tb = _attn_block(T)
    nb = T // tb

    def kern(q_ref, k_ref, v_ref, do_ref, o_ref, lse_ref, dq_ref, dlr_ref):
        qb = pl.program_id(1)
        qv, dov = q_ref[...], do_ref[...]
        lse = lse_ref[:, :1]
        delta = jnp.sum(dov.astype(F32) * o_ref[...], axis=-1, keepdims=True)
        dlr_ref[...] = jnp.broadcast_to(delta, (tb, LANES)).T[:SUBLANES, :]
        dq_ref[...] = jnp.zeros_like(dq_ref)

        def step(kb, diagonal):
            rows = pl.ds(pl.multiple_of(kb * tb, tb), tb)
            kv = k_ref[rows, :]
            s = _dot_nt(qv, kv) * ATTN_SCALE
            if diagonal:
                s = jnp.where(_attn_mask(tb), s, MASKED)
            p = jnp.exp(s - lse)
            ds = p * (_dot_nt(dov, v_ref[rows, :]) - delta) * ATTN_SCALE
            dq_ref[...] += _dot(_bf(ds), kv)

        def loop_body(kb, carry):
            step(kb, False)
            return carry

        lax.fori_loop(0, qb, loop_body, 0)
        step(qb, True)

    def blk(w):
        return pl.BlockSpec((tb, w), lambda h, i: (i, h))

    def full(w):
        return pl.BlockSpec((T, w), lambda h, i: (0, h))

    return pl.pallas_call(
        kern, grid=(MLA_HEADS, nb),
        in_specs=[blk(MLA_QK), full(MLA_QK), full(MLA_DV), blk(MLA_DV), blk(MLA_DV), blk(LANES)],
        out_specs=[blk(MLA_QK), _one_block(SUBLANES, tb)],
        out_shape=[jax.ShapeDtypeStruct((T, MLA_HEADS * MLA_QK), F32),
                   jax.ShapeDtypeStruct((nb, MLA_HEADS, SUBLANES, tb), F32)],
        name="attn_bwd_q", compiler_params=_params(("arbitrary", "arbitrary")))(q, k, v, do, o, lse)


def _attn_bwd_kv(q, k, v, do, qt, dot_, lse_rows, delta_rows):
    T = q.shape[0]
    tb = _attn_block(T)
    nb = T // tb

    def kern(q_ref, k_ref, v_ref, do_ref, qt_ref, dot_ref, lse_ref, dl_ref, dk_ref, dv_ref, dv_acc):
        kb = pl.program_id(1)
        kv, vv = k_ref[...], v_ref[...]
        dk_ref[...] = jnp.zeros_like(dk_ref)
        dv_acc[...] = jnp.zeros_like(dv_acc)

        def step(qb, diagonal):
            rows = pl.ds(pl.multiple_of(qb * tb, tb), tb)
            s = _dot(kv, qt_ref[qb]) * ATTN_SCALE
            if diagonal:
                s = jnp.where(_attn_mask_t(tb), s, MASKED)
            p = jnp.exp(s - lse_ref[qb][:1, :])
            dv_acc[...] += _dot(_bf(p), do_ref[rows, :])
            ds = p * (_dot(vv, dot_ref[qb]) - dl_ref[qb][:1, :]) * ATTN_SCALE
            dk_ref[...] += _dot(_bf(ds), q_ref[rows, :])

        def loop_body(qb, carry):
            step(qb, False)
            return carry

        step(kb, True)
        lax.fori_loop(kb + 1, nb, loop_body, 0)
        dv_ref[...] = _bf(dv_acc[...])

    def blk(w):
        return pl.BlockSpec((tb, w), lambda h, i: (i, h))

    def full(w):
        return pl.BlockSpec((T, w), lambda h, i: (0, h))

    return pl.pallas_call(
        kern, grid=(MLA_HEADS, nb),
        in_specs=[full(MLA_QK), blk(MLA_QK), blk(MLA_DV), full(MLA_DV), _head_blocks(nb, MLA_QK, tb),
                  _head_blocks(nb, MLA_DV, tb), _head_blocks(nb, SUBLANES, tb), _head_blocks(nb, SUBLANES, tb)],
        out_specs=[blk(MLA_QK), blk(MLA_DV)],
        out_shape=[jax.ShapeDtypeStruct((T, MLA_HEADS * MLA_QK), F32),
                   jax.ShapeDtypeStruct((T, MLA_HEADS * MLA_DV), BF16)],
        scratch_shapes=[pltpu.VMEM((tb, MLA_DV), F32)],
        name="attn_bwd_kv", compiler_params=_params(("arbitrary", "arbitrary")))(
            q, k, v, do, qt, dot_, lse_rows, delta_rows)


def _group_norm(y):
    yc = y - _mean(y)
    rstd = lax.rsqrt(_mean(yc * yc) + EPS)
    return yc * rstd, rstd


def _mix_fwd(y, rg, o, gates, h1, gn_g, w_ret_o, w_mla_o, w_out, ln_g, ln_b, tm):
    T, D = h1.shape

    def body(i, y_ref, rg_ref, o_ref, gt_ref, h_ref, gn_ref, wr_ref, wm_ref, wo_ref, g_ref, b_ref,
             h2_ref, z_ref, yret_ref, ymla_ref, yr_ref, mix_ref):
        for h in range(RET_HEADS):
            sl = slice(h * RET_DV, (h + 1) * RET_DV)
            yn, _ = _group_norm(y_ref[:, sl])
            r = rg_ref[:, sl]
            yr_ref[:, sl] = _bf(r * _sigmoid(r) * (yn * gn_ref[:, sl]))
        yret = _dot(yr_ref[...], wr_ref[...])
        ymla = _dot(_bf(o_ref[...]), wm_ref[...])
        yret_ref[...] = yret
        ymla_ref[...] = ymla
        mix = _bf(_sigmoid(gt_ref[:, :D]) * yret + _sigmoid(gt_ref[:, D:]) * ymla)
        mix_ref[...] = mix
        z = ALPHA * h_ref[...] + _dot(mix, wo_ref[...])
        xhat, _ = _ln_stats(z)
        z_ref[...] = z
        h2_ref[...] = xhat * g_ref[...] + b_ref[...]

    return _rowcall("mix_fwd", body, T, tm, [y, rg, o, gates, h1], [gn_g, w_ret_o, w_mla_o, w_out, ln_g, ln_b],
                    [(D, F32), (D, F32), (D, F32), (D, F32), (RET_HEADS * RET_DV, BF16), (D, BF16)])


def _mix_bwd(dh2, z1, gates, yret, ymla, y, rg, gn_g, w_ret_o, w_mla_o, w_out, ln_g, tm):
    T, D = dh2.shape
    rv = RET_HEADS * RET_DV

    def body(i, dh_ref, z_ref, gt_ref, yret_ref, ymla_ref, y_ref, rg_ref, gn_ref, wr_ref, wm_ref, wo_ref, g_ref,
             dz_ref, dgt_ref, drg_ref, dy_ref, do_ref, dyret_ref, dymla_ref, dg_ref, db_ref, dgn_ref, dot_ref):
        xhat, rstd = _ln_stats(z_ref[...])
        dz, dg, db = _ln_bwd(dh_ref[...], xhat, rstd, g_ref[...])
        _acc(i, dg_ref, dg)
        _acc(i, db_ref, db)
        dz_ref[...] = dz
        dmix = _dot_nt(_bf(dz), wo_ref[...])
        sr = _sigmoid(gt_ref[:, :D])
        sm = _sigmoid(gt_ref[:, D:])
        dgt_ref[:, :D] = _bf(dmix * yret_ref[...] * sr * (1.0 - sr))
        dgt_ref[:, D:] = _bf(dmix * ymla_ref[...] * sm * (1.0 - sm))
        dyret = _bf(dmix * sr)
        dymla = _bf(dmix * sm)
        dyret_ref[...] = dyret
        dymla_ref[...] = dymla
        dov = _dot_nt(dymla, wm_ref[...])
        do_ref[...] = _bf(dov)
        for h in range(MLA_HEADS):
            dot_ref[h] = _bf(dov[:, h * MLA_DV:(h + 1) * MLA_DV].T)
        dyr = _dot_nt(dyret, wr_ref[...])
        dgn = []
        for h in range(RET_HEADS):
            sl = slice(h * RET_DV, (h + 1) * RET_DV)
            yn, grstd = _group_norm(y_ref[:, sl])
            r = rg_ref[:, sl]
            sig = _sigmoid(r)
            d = dyr[:, sl]
            drg_ref[:, sl] = _bf(d * (yn * gn_ref[:, sl]) * sig * (1.0 + r * (1.0 - sig)))
            dt = d * (r * sig)
            dgn.append(jnp.sum(dt * yn, axis=0, keepdims=True))
            dyn = dt * gn_ref[:, sl]
            dy_ref[:, sl] = _bf(grstd * (dyn - _mean(dyn) - yn * _mean(dyn * yn)))
        _acc(i, dgn_ref, jnp.concatenate(dgn, axis=1))

    return _rowcall("mix_bwd", body, T, tm, [dh2, z1, gates, yret, ymla, y, rg], [gn_g, w_ret_o, w_mla_o, w_out, ln_g],
                    [(D, F32), (2 * D, BF16), (rv, BF16), (rv, BF16), (MLA_HEADS * MLA_DV, BF16), (D, BF16), (D, BF16)],
                    [((1, D), F32), ((1, D), F32), ((1, rv), F32)],
                    tiled_outs=[_transposed_blocks(T, tm, MLA_DV, BF16)])


def _proj_mla_bwd(dq, dk, dv, lat, tabs, w_uq, w_uk, w_uv, qn_g, kvn_g, tm):
    T = dq.shape[0]
    H = MLA_HEADS
    lat_w = Q_LORA + KV_LORA

    def body(i, dq_ref, dk_ref, dv_ref, lat_ref, c_ref, s1_ref, s2_ref, wuq_ref, wuk_ref, wuv_ref, qg_ref, kg_ref,
             dlat_ref, dkpe_ref, dqb_ref, dkn_ref, dqg_ref, dkg_ref):
        c, s1, s2 = c_ref[...], s1_ref[...], s2_ref[...]
        dkpe = jnp.zeros((tm, LANES), F32)
        for h in range(H):
            o = h * MLA_QK
            dqb_ref[:, o:o + MLA_NOPE] = _bf(dq_ref[:, o:o + MLA_NOPE])
            dqb_ref[:, o + MLA_NOPE:o + MLA_QK] = _bf(_rope_pe_bwd(dq_ref[:, o + MLA_NOPE:o + MLA_QK], c, s1, s2))
            dkn_ref[:, h * MLA_NOPE:(h + 1) * MLA_NOPE] = _bf(dk_ref[:, o:o + MLA_NOPE])
            dkpe += dk_ref[:, o + MLA_NOPE:o + MLA_QK]
        dkpe_ref[...] = _bf(_rope_pe_bwd(dkpe, c, s1, s2))
        dcqn = _dot_nt(dqb_ref[...], wuq_ref[...])
        dckn = _dot_nt(dkn_ref[...], wuk_ref[...]) + _dot_nt(dv_ref[...], wuv_ref[...])
        for dn, x, g_ref, dg_ref, sl in ((dcqn, lat_ref[:, :Q_LORA], qg_ref, dqg_ref, slice(0, Q_LORA)),
                                         (dckn, lat_ref[:, Q_LORA:], kg_ref, dkg_ref, slice(Q_LORA, lat_w))):
            xn, r = _rms(x, None)
            _acc(i, dg_ref, jnp.sum(dn * xn, axis=0, keepdims=True))
            dxn = dn * g_ref[...]
            dlat_ref[:, sl] = _bf(r * (dxn - xn * _mean(dxn * xn)))

    return _rowcall("proj_mla_bwd", body, T, tm, [dq, dk, dv, lat, *tabs], [w_uq, w_uk, w_uv, qn_g, kvn_g],
                    [(lat_w, BF16), (LANES, BF16), (H * MLA_QK, BF16), (H * MLA_NOPE, BF16)],
                    [((1, Q_LORA), F32), ((1, KV_LORA), F32)])


def _proj_bwd(drq, drk, drv, drg, dz1, dlat, dkpe, dgates, cos_r, sin_r, w_r, w_c, w_kpe, w_g, tm):
    T, D = dz1.shape
    qk = RET_HEADS * RET_DK
    rv = RET_HEADS * RET_DV

    def body(i, drq_ref, drk_ref, drv_ref, drg_ref, dz_ref, dlat_ref, dkpe_ref, dgt_ref, cos_ref, sin_ref,
             wr_ref, wc_ref, wk_ref, wg_ref, dh_ref, dpr_ref):
        cos, sin = cos_ref[...], sin_ref[...]
        for src, off, scale in ((drq_ref, 0, 1.0), (drk_ref, qk, RET_DK ** -0.5)):
            for h in range(RET_HEADS):
                d = src[:, h * RET_DK:(h + 1) * RET_DK]
                dpr_ref[:, off + h * RET_DK:off + (h + 1) * RET_DK] = _bf(
                    (d * cos + _roll(d * sin, RET_DK // 2)) * scale)
        dpr_ref[:, 2 * qk:2 * qk + rv] = drv_ref[...]
        dpr_ref[:, 2 * qk + rv:] = drg_ref[...]
        dh_ref[...] = (ALPHA * dz_ref[...] + _dot_nt(dpr_ref[...], wr_ref[...]) + _dot_nt(dlat_ref[...], wc_ref[...])
                       + _dot_nt(dkpe_ref[...], wk_ref[...]) + _dot_nt(dgt_ref[...], wg_ref[...]))

    return _rowcall("proj_bwd", body, T, tm, [drq, drk, drv, drg, dz1, dlat, dkpe, dgates, cos_r, sin_r],
                    [w_r, w_c, w_kpe, w_g], [(D, F32), (2 * qk + 2 * rv, BF16)])


def _ple_loss(h3, p, target, w_gate, w_proj, ln_g, ln_b, tm):
    T, D = h3.shape

    def body(i, h_ref, p_ref, t_ref, wg_ref, wp_ref, g_ref, b_ref, dh_ref, dgp_ref, dpp_ref, loss_ref, dg_ref, db_ref):
        hv = h_ref[...]
        sg = _sigmoid(_dot(_bf(hv), wg_ref[...]))
        pp = _dot(_bf(p_ref[...]), wp_ref[...])
        xhat, rstd = _ln_stats(ALPHA * hv + sg * pp)
        err = xhat * g_ref[...] + b_ref[...] - t_ref[...]
        row_loss = 0.5 * _mean(err * err)
        _acc(i, loss_ref, jnp.broadcast_to(jnp.sum(row_loss, axis=0, keepdims=True), (1, LANES)))
        dz, dg, db = _ln_bwd(err * (1.0 / D), xhat, rstd, g_ref[...])
        _acc(i, dg_ref, dg)
        _acc(i, db_ref, db)
        dgp = _bf(dz * pp * sg * (1.0 - sg))
        dgp_ref[...] = dgp
        dpp_ref[...] = _bf(dz * sg)
        dh_ref[...] = ALPHA * dz + _dot_nt(dgp, wg_ref[...])

    return _rowcall("ple_loss", body, T, tm, [h3, p, target], [w_gate, w_proj, ln_g, ln_b],
                    [(D, F32), (D, BF16), (D, BF16)], [((1, LANES), F32), ((1, D), F32), ((1, D), F32)])


def _ewise(name, fn, ins, n_out, out_dtype=F32):
    r, c = ins[0].shape
    tr = _tile(r, max(8, (1 << 19) // c // 8 * 8), 8)

    def kern(*refs):
        outs = fn(*[x[...] for x in refs[:len(ins)]])
        for o_ref, o in zip(refs[len(ins):], outs):
            o_ref[...] = o.astype(out_dtype)

    spec = pl.BlockSpec((tr, c), lambda i: (i, 0))
    return pl.pallas_call(kern, grid=(r // tr,), in_specs=[spec] * len(ins), out_specs=[spec] * n_out,
                          out_shape=[jax.ShapeDtypeStruct((r, c), out_dtype)] * n_out, name=name,
                          compiler_params=_params(("arbitrary",)))(*ins)


def _adamw_math(w, g, m, v):
    m = ADAM_B1 * m + (1.0 - ADAM_B1) * g
    v = ADAM_B2 * v + (1.0 - ADAM_B2) * (g * g)
    m_hat = m / (1.0 - ADAM_B1 ** ADAM_STEP)
    v_hat = v / (1.0 - ADAM_B2 ** ADAM_STEP)
    return -ADAM_LR * (m_hat / (jnp.sqrt(v_hat) + ADAM_EPS) + ADAM_WD * w), m, v


def _adamw(name, w, g, m, v):
    shape = w.shape
    c = shape[-1]
    flat = [t.reshape(-1, c) for t in (w, g, m, v)]
    return [t.reshape(shape) for t in _ewise(name, _adamw_math, flat, 3)]


def _place():
    return lax.axis_index("x"), lax.axis_index("y"), lax.axis_index("c")


def _dma_sems(n):
    return [pltpu.SemaphoreType.DMA((n,)), pltpu.SemaphoreType.DMA((n,))]


N_PEER_CHIPS = N_CHIPS - 1


def _chips_exchange(name, srcs, broadcast):
    n = len(srcs)

    def kern(*refs):
        src_refs, out_refs = refs[:n], refs[n:2 * n]
        send_sems, recv_sems = refs[2 * n:]
        x, y, c = _place()
        peers = [(1 - x, y), (x, 1 - y), (1 - x, 1 - y)]

        def copy(a, j):
            px, py = peers[j]
            piece = src_refs[a].at[c] if broadcast else src_refs[a].at[2 * px + py]
            return pltpu.make_async_remote_copy(
                src_ref=piece, dst_ref=out_refs[a].at[j], send_sem=send_sems.at[a * N_PEER_CHIPS + j],
                recv_sem=recv_sems.at[a * N_PEER_CHIPS + j], device_id=(px, py, c), device_id_type=MESH)

        cps = [copy(a, j) for j in range(N_PEER_CHIPS) for a in range(n)]
        for cp in cps:
            cp.start()
        for cp in cps:
            cp.wait_recv()
        for cp in cps:
            cp.wait_send()

    return pl.pallas_call(
        kern, out_shape=[jax.ShapeDtypeStruct((N_PEER_CHIPS,) + s.shape[1:], s.dtype) for s in srcs],
        in_specs=[HBM_SPEC] * n, out_specs=[HBM_SPEC] * n, scratch_shapes=_dma_sems(n * N_PEER_CHIPS),
        name=name)(*srcs)


def _sibling_swap(name, srcs, halves):
    n = len(srcs)

    def kern(*refs):
        src_refs, out_refs = refs[:n], refs[n:2 * n]
        send_sems, recv_sems = refs[2 * n:]
        x, y, c = _place()

        def copy(a):
            piece = src_refs[a].at[:, 1 - c] if halves else src_refs[a]
            return pltpu.make_async_remote_copy(
                src_ref=piece, dst_ref=out_refs[a], send_sem=send_sems.at[a], recv_sem=recv_sems.at[a],
                device_id=(x, y, 1 - c), device_id_type=MESH)

        cps = [copy(a) for a in range(n)]
        for cp in cps:
            cp.start()
        for cp in cps:
            cp.wait_recv()
        for cp in cps:
            cp.wait_send()

    def out_shape(s):
        return jax.ShapeDtypeStruct((s.shape[0],) + s.shape[2:] if halves else s.shape, s.dtype)

    return pl.pallas_call(
        kern, out_shape=[out_shape(s) for s in srcs], in_specs=[HBM_SPEC] * n, out_specs=[HBM_SPEC] * n,
        scratch_shapes=_dma_sems(n), name=name)(*srcs)


def _all_devices(name, src, reduce):
    r, c = src.shape
    n_dev = 2 * N_CHIPS

    def kern(src_ref, out_ref, *scratch):
        if reduce:
            gat_ref, send_sems, recv_sems = scratch
        else:
            gat_ref = out_ref
            send_sems, recv_sems = scratch
        x, y, cc = _place()
        me = 4 * x + 2 * y + cc
        gat_ref[me] = src_ref[...]
        peers = []
        for j in range(1, n_dev):
            px = 1 - x if j & 4 else x
            py = 1 - y if j & 2 else y
            pc = 1 - cc if j & 1 else cc
            peers.append((px, py, pc))

        def copy(j, peer, slot):
            return pltpu.make_async_remote_copy(
                src_ref=src_ref, dst_ref=gat_ref.at[slot], send_sem=send_sems.at[j], recv_sem=recv_sems.at[j],
                device_id=peer, device_id_type=MESH)

        sends = [copy(j, peer, me) for j, peer in enumerate(peers)]
        for cp in sends:
            cp.start()
        for j, (px, py, pc) in enumerate(peers):
            copy(j, (px, py, pc), 4 * px + 2 * py + pc).wait_recv()
        for cp in sends:
            cp.wait_send()
        if reduce:
            total = gat_ref[0]
            for d in range(1, n_dev):
                total = total + gat_ref[d]
            out_ref[...] = total

    out_shape = jax.ShapeDtypeStruct((r, c) if reduce else (n_dev, r, c), src.dtype)
    scratch = ([pltpu.VMEM((n_dev, r, c), src.dtype)] if reduce else []) + _dma_sems(n_dev - 1)
    return pl.pallas_call(kern, out_shape=out_shape, in_specs=[VMEM_SPEC], out_specs=VMEM_SPEC,
                          scratch_shapes=scratch, name=name)(src)


ROW_GROUP = ("ffn1_w_out", "w_ret_o", "w_mla_o", "w_out", "ffn2_w_out", "ple_w_gate")
COL_GROUP = ("w_uq", "w_ukv", "ple_w_proj")


def _to_groups(t, axis):
    return [jnp.concatenate([t[n] for n in ROW_GROUP], axis=axis), t["ffn1_w_in"], t["ffn2_w_in"], t["w_in"],
            jnp.concatenate([t[n] for n in COL_GROUP], axis=axis + 1)]


def _from_groups(groups, shapes, axis):
    rows, ffn1, ffn2, w_in, cols = groups
    out = {"ffn1_w_in": ffn1, "ffn2_w_in": ffn2, "w_in": w_in}
    off = 0
    for n in ROW_GROUP:
        out[n] = lax.slice_in_dim(rows, off, off + shapes[n][0], axis=axis)
        off += shapes[n][0]
    off = 0
    for n in COL_GROUP:
        out[n] = lax.slice_in_dim(cols, off, off + shapes[n][1], axis=axis + 1)
        off += shapes[n][1]
    return out


def _halves(t, axis):
    return t.reshape(t.shape[:axis] + (2, t.shape[axis] // 2) + t.shape[axis + 1:])


def _by_core(mine, theirs, axis):
    c = lax.axis_index("c")
    both = jnp.where(c == 0, jnp.stack([mine, theirs], axis), jnp.stack([theirs, mine], axis))
    return both.reshape(both.shape[:axis] + (2 * both.shape[axis + 1],) + both.shape[axis + 2:])


def _chip_order(own, others):
    me = 2 * lax.axis_index("x") + lax.axis_index("y")
    cands = jnp.concatenate([own[None], others], axis=0)
    slot_of_flip = (0, 2, 1, 3)
    pick = jnp.asarray(slot_of_flip, jnp.int32)[jnp.arange(N_CHIPS, dtype=jnp.int32) ^ me]
    return jnp.stack([lax.dynamic_index_in_dim(cands, pick[k], 0, keepdims=False) for k in range(N_CHIPS)])


def _join_shards(name, shards):
    _, r, c = shards.shape
    if name in COL_SHARDED:
        return shards.transpose(1, 0, 2).reshape(r, N_CHIPS * c)
    return shards.reshape(N_CHIPS * r, c)


def _split_shards(name, full):
    if full.ndim == 3:
        return full
    r, c = full.shape
    if name in COL_SHARDED:
        return jnp.stack([full[:, k * (c // N_CHIPS):(k + 1) * (c // N_CHIPS)] for k in range(N_CHIPS)])
    return full.reshape(N_CHIPS, r // N_CHIPS, c)


def _rope_tables(positions):
    pos = positions.reshape(-1).astype(F32)[:, None]
    half = RET_DK // 2
    ang = pos * (ROPE_BASE ** (-jnp.arange(half, dtype=F32) / half))
    cos_r = jnp.concatenate([jnp.cos(ang)] * 2, axis=1)
    sin_r = jnp.concatenate([-jnp.sin(ang), jnp.sin(ang)], axis=1)
    half = MLA_ROPE // 2
    ang = pos * (ROPE_BASE ** (-jnp.arange(half, dtype=F32) / half))
    zeros = jnp.zeros_like(ang)
    rest = LANES - MLA_ROPE
    c = jnp.concatenate([jnp.cos(ang)] * 2 + [jnp.ones((ang.shape[0], rest), F32)], axis=1)
    s1 = jnp.concatenate([-jnp.sin(ang), zeros, jnp.zeros((ang.shape[0], rest), F32)], axis=1)
    s2 = jnp.concatenate([zeros, jnp.sin(ang), jnp.zeros((ang.shape[0], rest), F32)], axis=1)
    return cos_r, sin_r, (c, s1, s2)


def _local_step(x, p, positions, target, w, ln_g, ln_b, gn_g, qn_g, kvn_g):
    T, D = x.shape
    tm = min(256, T)
    H = MLA_HEADS
    qk, rv = RET_HEADS * RET_DK, RET_HEADS * RET_DV
    cos_r, sin_r, tabs = _rope_tables(positions)
    lgam = jnp.broadcast_to(jnp.log(1.0 - 2.0 ** (-5.0 - jnp.arange(RET_HEADS, dtype=F32)))[:, None, None],
                            (RET_HEADS, 1, LANES))
    lng = [ln_g[k:k + 1] for k in range(N_LN)]
    lnb = [ln_b[k:k + 1] for k in range(N_LN)]

    w_in = w["w_in"]
    o_lat, o_kpe, o_gate = 2 * qk + 2 * rv, 2 * qk + 2 * rv + Q_LORA + KV_LORA, 2 * qk + 2 * rv + Q_LORA + KV_LORA + MLA_ROPE
    w_r, w_c = w_in[:, :o_lat], w_in[:, o_lat:o_kpe]
    w_kpe = jnp.pad(w_in[:, o_kpe:o_gate], ((0, 0), (0, LANES - MLA_ROPE)))
    w_g = w_in[:, o_gate:]
    w_uq = jnp.pad(w["w_uq"].reshape(Q_LORA, H, MLA_NOPE + MLA_ROPE),
                   ((0, 0), (0, 0), (0, MLA_QK - MLA_NOPE - MLA_ROPE))).reshape(Q_LORA, H * MLA_QK)
    w_ukv = w["w_ukv"].reshape(KV_LORA, H, MLA_NOPE + MLA_DV)
    w_uk = w_ukv[:, :, :MLA_NOPE].reshape(KV_LORA, H * MLA_NOPE)
    w_uv = w_ukv[:, :, MLA_NOPE:].reshape(KV_LORA, H * MLA_DV)

    h1, z0, a1 = _ffn_fwd("ffn1_fwd", x, w["ffn1_w_in"], w["ffn1_w_out"], lng[0], lnb[0], tm)
    rq, rk, rvv, rg = _proj_ret(h1, w_r, cos_r, sin_r, tm)
    lat, gates, q, k, v, latn, qt, vt = _proj_mla(h1, tabs, w_c, w_kpe, w_g, w_uq, w_uk, w_uv, qn_g, kvn_g, tm)
    y = _ret_fwd(rq, rk, rvv, lgam)
    o, lse, lse_rows = _attn_fwd(k, qt, vt)
    h2, z1, yret, ymla, yr, mix = _mix_fwd(y, rg, o, gates, h1, gn_g, w["w_ret_o"], w["w_mla_o"], w["w_out"],
                                           lng[1], lnb[1], tm)
    h3, z2, a2 = _ffn_fwd("ffn2_fwd", h2, w["ffn2_w_in"], w["ffn2_w_out"], lng[2], lnb[2], tm)

    dh3, dgp, dpp, loss, dg3, db3 = _ple_loss(h3, p, target, w["ple_w_gate"], w["ple_w_proj"], lng[3], lnb[3], tm)
    dh2, da2, s2, df2, dg2, db2 = _ffn_bwd("ffn2_bwd", dh3, z2, a2, w["ffn2_w_in"], w["ffn2_w_out"], lng[2], tm)
    (dz1, dgates, drg, dy, do, dyret, dymla, dg1, db1, dgn, dot_) = _mix_bwd(
        dh2, z1, gates, yret, ymla, y, rg, gn_g, w["w_ret_o"], w["w_mla_o"], w["w_out"], lng[1], tm)
    drq = _ret_bwd_q(rq, rk, rvv, dy, lgam)
    drk, drv = _ret_bwd_kv(rq, rk, rvv, dy, lgam)
    dq, delta_rows = _attn_bwd_q(q, k, v, do, o, lse)
    dk, dv = _attn_bwd_kv(q, k, v, do, qt, dot_, lse_rows, delta_rows)
    dlat, dkpe, dqb, dkn, dqg, dkg = _proj_mla_bwd(dq, dk, dv, lat, tabs, w_uq, w_uk, w_uv, qn_g, kvn_g, tm)
    dh1, dpr = _proj_bwd(drq, drk, drv, drg, dz1, dlat, dkpe, dgates, cos_r, sin_r, w_r, w_c, w_kpe, w_g, tm)
    dx, da1, s1, df1, dg0, db0 = _ffn_bwd("ffn1_bwd", dh1, z0, a1, w["ffn1_w_in"], w["ffn1_w_out"], lng[0], tm)

    g_uq = _mm_tn("wg_uq", latn[:, :Q_LORA], dqb).reshape(Q_LORA, H, MLA_QK)[:, :, :MLA_NOPE + MLA_ROPE]
    g_uk = _mm_tn("wg_uk", latn[:, Q_LORA:], dkn).reshape(KV_LORA, H, MLA_NOPE)
    g_uv = _mm_tn("wg_uv", latn[:, Q_LORA:], dv).reshape(KV_LORA, H, MLA_DV)
    grads = {
        "ffn1_w_in": _mm_tn("wg_ffn1_in", x, da1, n_split=N_CHIPS),
        "ffn1_w_out": _mm_tn("wg_ffn1_out", s1, df1),
        "w_in": jnp.concatenate([_mm_tn("wg_in_r", h1, dpr), _mm_tn("wg_in_c", h1, dlat),
                                 _mm_tn("wg_in_kpe", h1, dkpe)[:, :MLA_ROPE], _mm_tn("wg_in_g", h1, dgates)], axis=1),
        "w_ret_o": _mm_tn("wg_ret_o", yr, dyret),
        "w_uq": g_uq.reshape(Q_LORA, H * (MLA_NOPE + MLA_ROPE)),
        "w_ukv": jnp.concatenate([g_uk, g_uv], axis=2).reshape(KV_LORA, H * (MLA_NOPE + MLA_DV)),
        "w_mla_o": _mm_tn("wg_mla_o", o, dymla),
        "w_out": _mm_tn("wg_out", mix, dz1),
        "ffn2_w_in": _mm_tn("wg_ffn2_in", h2, da2, n_split=N_CHIPS),
        "ffn2_w_out": _mm_tn("wg_ffn2_out", s2, df2),
        "ple_w_gate": _mm_tn("wg_ple_gate", h3, dgp),
        "ple_w_proj": _mm_tn("wg_ple_proj", p, dpp),
    }
    small = {"ln_g": jnp.concatenate([dg0, dg1, dg2, dg3], axis=0), "ln_b": jnp.concatenate([db0, db1, db2, db3], axis=0),
             "ret_gn_g": dgn, "q_norm_g": dqg, "kv_norm_g": dkg}
    return loss[0, 0], dx, grads, small


def _gather_weights(shards):
    own = _to_groups({n: _bf(shards[n]) for n in BIG_WEIGHTS}, 0)
    mine = _chips_exchange("gather_chips", [_halves(g, 0) for g in own], True)
    theirs = _sibling_swap("gather_cores", mine, False)
    groups = [_chip_order(g, _by_core(m, t, 1)) for g, m, t in zip(own, mine, theirs)]
    parts = _from_groups(groups, {n: shards[n].shape for n in BIG_WEIGHTS}, 1)
    return {n: t if n in ("ffn1_w_in", "ffn2_w_in") else _join_shards(n, t) for n, t in parts.items()}


def _reduce_grads(grads, shapes):
    c = lax.axis_index("c")
    me = 2 * lax.axis_index("x") + lax.axis_index("y")
    groups = [_halves(g, 1) for g in _to_groups({n: _split_shards(n, grads[n]) for n in BIG_WEIGHTS}, 1)]
    theirs = _sibling_swap("reduce_cores", groups, True)
    chip_sums = []
    for i, (g, t) in enumerate(zip(groups, theirs)):
        mine = lax.dynamic_index_in_dim(g, c, axis=1, keepdims=False)
        k, r, cc = mine.shape
        chip_sums.append(_ewise("reduce_cores_add%d" % i, lambda a, b: (a.astype(F32) + b.astype(F32),),
                                [mine.reshape(k * r, cc), t.reshape(k * r, cc)], 1, BF16)[0].reshape(k, r, cc))
    parts = _chips_exchange("reduce_chips", chip_sums, False)
    totals = []
    for i, (s, pt) in enumerate(zip(chip_sums, parts)):
        own = lax.dynamic_index_in_dim(s, me, axis=0, keepdims=False)
        totals.append(_ewise("reduce_chips_add%d" % i,
                             lambda a, b, c_, d: (((a.astype(F32) + b.astype(F32)) + c_.astype(F32)) + d.astype(F32),),
                             [own, pt[0], pt[1], pt[2]], 1, F32)[0])
    others = _sibling_swap("reduce_join", totals, False)
    return _from_groups([_by_core(t, o, 0) for t, o in zip(totals, others)], shapes, 0)


def kernel(x, p, positions, ln_g, ln_b, ffn1_w_in, ffn1_w_out, w_in, ret_gn_g, w_ret_o, q_norm_g, kv_norm_g, w_uq, w_ukv, w_mla_o, w_out, ffn2_w_in, ffn2_w_out, ple_w_gate, ple_w_proj, loss_target, m_ln_g, m_ln_b, m_ffn1_w_in, m_ffn1_w_out, m_w_in, m_ret_gn_g, m_w_ret_o, m_q_norm_g, m_kv_norm_g, m_w_uq, m_w_ukv, m_w_mla_o, m_w_out, m_ffn2_w_in, m_ffn2_w_out, m_ple_w_gate, m_ple_w_proj, v_ln_g, v_ln_b, v_ffn1_w_in, v_ffn1_w_out, v_w_in, v_ret_gn_g, v_w_ret_o, v_q_norm_g, v_kv_norm_g, v_w_uq, v_w_ukv, v_w_mla_o, v_w_out, v_ffn2_w_in, v_ffn2_w_out, v_ple_w_gate, v_ple_w_proj):
    names = ("ln_g", "ln_b", "ffn1_w_in", "ffn1_w_out", "w_in", "ret_gn_g", "w_ret_o", "q_norm_g", "kv_norm_g", "w_uq",
             "w_ukv", "w_mla_o", "w_out", "ffn2_w_in", "ffn2_w_out", "ple_w_gate", "ple_w_proj")
    weights = dict(zip(names, (ln_g, ln_b, ffn1_w_in, ffn1_w_out, w_in, ret_gn_g, w_ret_o, q_norm_g, kv_norm_g, w_uq,
                               w_ukv, w_mla_o, w_out, ffn2_w_in, ffn2_w_out, ple_w_gate, ple_w_proj)))
    m_in = dict(zip(names, (m_ln_g, m_ln_b, m_ffn1_w_in, m_ffn1_w_out, m_w_in, m_ret_gn_g, m_w_ret_o, m_q_norm_g,
                            m_kv_norm_g, m_w_uq, m_w_ukv, m_w_mla_o, m_w_out, m_ffn2_w_in, m_ffn2_w_out, m_ple_w_gate,
                            m_ple_w_proj)))
    v_in = dict(zip(names, (v_ln_g, v_ln_b, v_ffn1_w_in, v_ffn1_w_out, v_w_in, v_ret_gn_g, v_w_ret_o, v_q_norm_g,
                            v_kv_norm_g, v_w_uq, v_w_ukv, v_w_mla_o, v_w_out, v_ffn2_w_in, v_ffn2_w_out, v_ple_w_gate,
                            v_ple_w_proj)))
    chip = 2 * lax.axis_index("x") + lax.axis_index("y")
    D = x.shape[-1]
    dq = D // N_CHIPS

    shards = {n: weights[n][0] for n in BIG_WEIGHTS}
    w = _gather_weights(shards)
    ln_all = _all_devices("gather_ln", jnp.concatenate([ln_g[0], ln_b[0]], axis=0), False)
    ln_full = ln_all[::2].transpose(1, 0, 2).reshape(2 * N_LN, D)
    loss, dx, grads, small = _local_step(x[0], p[0, 0], positions, loss_target[0], w, ln_full[:N_LN], ln_full[N_LN:],
                                         ret_gn_g, q_norm_g, kv_norm_g)

    loss = lax.psum(loss, ("x", "y", "c"))
    big = _reduce_grads(grads, {n: shards[n].shape for n in BIG_WEIGHTS})
    small_names = ("ln_g", "ln_b", "ret_gn_g", "q_norm_g", "kv_norm_g")
    flat = jnp.concatenate([small[n].reshape(-1) for n in small_names])
    rows = -(-flat.shape[0] // LANES // 8) * 8
    flat = jnp.pad(flat, (0, rows * LANES - flat.shape[0])).reshape(rows, LANES)
    flat = _all_devices("reduce_small", flat, True).reshape(-1)
    off = 0
    for n in small_names:
        size = small[n].size
        small[n] = flat[off:off + size].reshape(small[n].shape)
        off += size
    g_out = dict(big)
    for n in ("ln_g", "ln_b"):
        g_out[n] = lax.dynamic_slice_in_dim(small[n], chip * dq, dq, axis=1)
    for n in ("ret_gn_g", "q_norm_g", "kv_norm_g"):
        g_out[n] = small[n]

    deltas, new_m, new_v = {}, {}, {}
    for n in names:
        g = g_out[n].reshape(weights[n].shape)
        g_out[n] = g
        deltas[n], new_m[n], new_v[n] = _adamw("adamw_" + n, weights[n], g, m_in[n], v_in[n])
    return (loss, dx[None], *[g_out[n] for n in names], *[deltas[n] for n in names], *[new_m[n] for n in names],
            *[new_v[n] for n in names])
```

```python
import functools

import jax
import jax.numpy as jnp
from jax import lax
from jax.experimental import pallas as pl
from jax.experimental.pallas import tpu as pltpu

D_MODEL = 1024
CHUNK = 64
D_PLE = 256
D_FF = 2816
RET_HEADS = 8
RET_DK = 128
RET_DV = 256
MLA_HEADS = 8
MLA_NOPE = 128
MLA_ROPE = 64
MLA_DV = 128
MLA_QK = 256
Q_LORA = 256
KV_LORA = 256
ROPE_BASE = 10000.0
EPS = 1e-5
N_LN = 4
ALPHA = 2.0 ** 0.25
ADAM_LR = 0.001
ADAM_B1 = 0.9
ADAM_B2 = 0.999
ADAM_EPS = 1e-08
ADAM_WD = 0.01
ADAM_STEP = 10

LANES = 128
VMEM_LIMIT = 60 << 20
N_CHIPS = 4

F32 = jnp.float32
BF16 = jnp.bfloat16
MESH = pl.DeviceIdType.MESH
HBM_SPEC = pl.BlockSpec(memory_space=pltpu.HBM)
VMEM_SPEC = pl.BlockSpec(memory_space=pltpu.VMEM)

BIG_WEIGHTS = ("ffn1_w_in", "ffn1_w_out", "w_in", "w_ret_o", "w_uq", "w_ukv", "w_mla_o", "w_out",
               "ffn2_w_in", "ffn2_w_out", "ple_w_gate", "ple_w_proj")
COL_SHARDED = ("ffn1_w_in", "w_in", "w_uq", "w_ukv", "ffn2_w_in", "ple_w_proj")


def _dot(a, b):
    return jnp.dot(a, b, preferred_element_type=F32)


def _dot_nt(a, b):
    return lax.dot_general(a, b, (((1,), (1,)), ((), ())), preferred_element_type=F32)


def _dot_tn(a, b):
    return lax.dot_general(a, b, (((0,), (0,)), ((), ())), preferred_element_type=F32)


def _bf(x):
    return x.astype(BF16)


def _sigmoid(x):
    return 1.0 / (1.0 + jnp.exp(-x))


def _mean(x):
    return jnp.mean(x, axis=-1, keepdims=True)


def _ln_stats(z):
    zc = z - _mean(z)
    rstd = lax.rsqrt(_mean(zc * zc) + EPS)
    return zc * rstd, rstd


def _ln_bwd(dy, xhat, rstd, g):
    dxhat = dy * g
    dz = rstd * (dxhat - _mean(dxhat) - xhat * _mean(dxhat * xhat))
    return dz, jnp.sum(dy * xhat, axis=0, keepdims=True), jnp.sum(dy, axis=0, keepdims=True)


def _roll(x, shift):
    return pltpu.roll(x, shift, 1)


def _chunk_of(idx):
    return jnp.right_shift(idx, CHUNK.bit_length() - 1)


def _tile(n, cap, mult=LANES):
    if n <= cap:
        return n
    for t in range(cap - cap % mult, 0, -mult):
        if n % t == 0:
            return t
    return n


def _zero_map(nd, *_):
    return (0,) * nd


def _params(sem):
    return pltpu.CompilerParams(dimension_semantics=sem, vmem_limit_bytes=VMEM_LIMIT)


def _rowcall(name, body, n_rows, tm, row_ins, full_ins, row_outs, acc_outs=(), tiled_outs=(), tiled_ins=()):
    def kern(*refs):
        body(pl.program_id(0), *refs)

    in_specs = [pl.BlockSpec((tm, a.shape[1]), lambda i: (i, 0)) for a in row_ins]
    in_specs += [spec for (_, spec) in tiled_ins]
    row_ins = list(row_ins) + [a for (a, _) in tiled_ins]
    in_specs += [pl.BlockSpec(a.shape, functools.partial(_zero_map, a.ndim), pipeline_mode=pl.Buffered(1))
                 for a in full_ins]
    out_specs = [pl.BlockSpec((tm, w), lambda i: (i, 0)) for (w, _) in row_outs]
    out_specs += [pl.BlockSpec(s, functools.partial(_zero_map, len(s))) for (s, _) in acc_outs]
    out_specs += [spec for (_, spec) in tiled_outs]
    out_shape = [jax.ShapeDtypeStruct((n_rows, w), dt) for (w, dt) in row_outs]
    out_shape += [jax.ShapeDtypeStruct(s, dt) for (s, dt) in acc_outs]
    out_shape += [shape for (shape, _) in tiled_outs]
    return pl.pallas_call(kern, grid=(n_rows // tm,), in_specs=in_specs, out_specs=out_specs,
                          out_shape=out_shape, name=name, compiler_params=_params(("arbitrary",)))(
                              *row_ins, *full_ins)


def _acc(step, ref, val):
    @pl.when(step == 0)
    def _():
        ref[...] = val

    @pl.when(step != 0)
    def _():
        ref[...] += val


def _ffn_fwd(name, x, w_in4, w_out, ln_g, ln_b, tm):
    T, D = x.shape
    fh = w_in4.shape[2]

    def body(i, x_ref, w4_ref, wo_ref, g_ref, b_ref, h_ref, z_ref, a_ref):
        xv = x_ref[...]
        xb = _bf(xv)
        f = jnp.zeros((tm, D), F32)
        for k in range(2):
            gk = _dot(xb, w4_ref[k])
            uk = _dot(xb, w4_ref[2 + k])
            a_ref[:, k * fh:(k + 1) * fh] = _bf(gk)
            a_ref[:, (2 + k) * fh:(3 + k) * fh] = _bf(uk)
            f += _dot(_bf(gk * _sigmoid(gk) * uk), wo_ref[k * fh:(k + 1) * fh, :])
        z = ALPHA * xv + 0.5 * f
        xhat, _ = _ln_stats(z)
        z_ref[...] = z
        h_ref[...] = xhat * g_ref[...] + b_ref[...]

    return _rowcall(name, body, T, tm, [x], [w_in4, w_out, ln_g, ln_b],
                    [(D, F32), (D, F32), (4 * fh, BF16)])


def _ffn_bwd(name, dh, z, a, w_in4, w_out, ln_g, tm):
    T, D = dh.shape
    fh = w_in4.shape[2]

    def body(i, dh_ref, z_ref, a_ref, w4_ref, wo_ref, g_ref, dx_ref, da_ref, s_ref, df_ref, dg_ref, db_ref):
        xhat, rstd = _ln_stats(z_ref[...])
        dz, dg, db = _ln_bwd(dh_ref[...], xhat, rstd, g_ref[...])
        _acc(i, dg_ref, dg)
        _acc(i, db_ref, db)
        dfb = _bf(0.5 * dz)
        df_ref[...] = dfb
        dx = ALPHA * dz
        for k in range(2):
            gk = a_ref[:, k * fh:(k + 1) * fh].astype(F32)
            uk = a_ref[:, (2 + k) * fh:(3 + k) * fh].astype(F32)
            ds = _dot_nt(dfb, wo_ref[k * fh:(k + 1) * fh, :])
            sig = _sigmoid(gk)
            silu = gk * sig
            dgk = _bf(ds * uk * sig * (1.0 + gk * (1.0 - sig)))
            duk = _bf(ds * silu)
            s_ref[:, k * fh:(k + 1) * fh] = _bf(silu * uk)
            da_ref[:, k * fh:(k + 1) * fh] = dgk
            da_ref[:, (2 + k) * fh:(3 + k) * fh] = duk
            dx += _dot_nt(dgk, w4_ref[k]) + _dot_nt(duk, w4_ref[2 + k])
        dx_ref[...] = dx

    return _rowcall(name, body, T, tm, [dh, z, a], [w_in4, w_out, ln_g],
                    [(D, F32), (4 * fh, BF16), (2 * fh, BF16), (D, BF16)],
                    [((1, D), F32), ((1, D), F32)])


def _mm_tn(name, a, b, out_dtype=BF16, n_split=1):
    T, M = a.shape
    N = b.shape[1]
    tk = _tile(T, 512, 8)
    tm = _tile(M, 1408)
    tn = _tile(N // n_split, 1536)
    per = N // n_split // tn
    nk = T // tk
    if n_split > 1:
        out_spec = pl.BlockSpec((None, tm, tn), lambda i, j, k: (j // per, i, j % per))
        out_shape = jax.ShapeDtypeStruct((n_split, M, N // n_split), out_dtype)
    else:
        out_spec = pl.BlockSpec((tm, tn), lambda i, j, k: (i, j))
        out_shape = jax.ShapeDtypeStruct((M, N), out_dtype)

    def kern(a_ref, b_ref, o_ref, acc_ref):
        k = pl.program_id(2)
        part = _dot_tn(_bf(a_ref[...]), _bf(b_ref[...]))

        @pl.when(k == 0)
        def _():
            acc_ref[...] = part

        @pl.when(k != 0)
        def _():
            acc_ref[...] += part

        @pl.when(k == nk - 1)
        def _():
            o_ref[...] = acc_ref[...].astype(out_dtype)

    return pl.pallas_call(
        kern, grid=(M // tm, N // tn, nk),
        in_specs=[pl.BlockSpec((tk, tm), lambda i, j, k: (k, i)), pl.BlockSpec((tk, tn), lambda i, j, k: (k, j))],
        out_specs=out_spec, out_shape=out_shape,
        scratch_shapes=[pltpu.VMEM((tm, tn), F32)], name=name,
        compiler_params=_params(("arbitrary", "arbitrary", "arbitrary")))(a, b)


def _proj_ret(h1, w_r, cos_r, sin_r, tm):
    T, D = h1.shape
    qk = RET_HEADS * RET_DK
    rv = RET_HEADS * RET_DV

    def body(i, h_ref, cos_ref, sin_ref, w_ref, q_ref, k_ref, v_ref, g_ref):
        hb = _bf(h_ref[...])
        cos, sin = cos_ref[...], sin_ref[...]
        for out_ref, off, scale in ((q_ref, 0, 1.0), (k_ref, qk, RET_DK ** -0.5)):
            pr = _dot(hb, w_ref[:, off:off + qk])
            for h in range(RET_HEADS):
                t = pr[:, h * RET_DK:(h + 1) * RET_DK]
                out_ref[:, h * RET_DK:(h + 1) * RET_DK] = _bf((t * cos + _roll(t, RET_DK // 2) * sin) * scale)
        v_ref[...] = _bf(_dot(hb, w_ref[:, 2 * qk:2 * qk + rv]))
        g_ref[...] = _dot(hb, w_ref[:, 2 * qk + rv:2 * qk + 2 * rv])

    return _rowcall("proj_ret", body, T, tm, [h1, cos_r, sin_r], [w_r],
                    [(qk, BF16), (qk, BF16), (rv, BF16), (rv, F32)])


def _rope_pe(t, c, s1, s2):
    return t * c + _roll(t, LANES - MLA_ROPE // 2) * s1 + _roll(t, MLA_ROPE // 2) * s2


def _rope_pe_bwd(dy, c, s1, s2):
    return dy * c + _roll(dy * s1, MLA_ROPE // 2) + _roll(dy * s2, LANES - MLA_ROPE // 2)


def _rms(x, g):
    r = lax.rsqrt(_mean(x * x) + EPS)
    return x * r, r


def _attn_block(T):
    return min(512, T)


def _transposed_blocks(T, tm, w, dtype):
    tb = _attn_block(T)
    per = tb // tm
    return (jax.ShapeDtypeStruct((T // tb, MLA_HEADS, w, tb), dtype),
            pl.BlockSpec((None, MLA_HEADS, w, tm), lambda i: (i // per, 0, 0, i % per)))


def _proj_mla(h1, tabs, w_c, w_kpe, w_g, w_uq, w_uk, w_uv, qn_g, kvn_g, tm):
    T, D = h1.shape
    H = MLA_HEADS

    def body(i, h_ref, c_ref, s1_ref, s2_ref, wc_ref, wk_ref, wg_ref, wuq_ref, wuk_ref, wuv_ref, qg_ref, kg_ref,
             lat_ref, gt_ref, q_ref, k_ref, v_ref, ln_ref, qt_ref, kt_ref, vt_ref):
        hb = _bf(h_ref[...])
        c, s1, s2 = c_ref[...], s1_ref[...], s2_ref[...]
        lat = _dot(hb, wc_ref[...])
        lat_ref[...] = lat
        gt_ref[...] = _dot(hb, wg_ref[...])
        cqn, _ = _rms(lat[:, :Q_LORA], None)
        ckn, _ = _rms(lat[:, Q_LORA:], None)
        cqn = _bf(cqn * qg_ref[...])
        ckn = _bf(ckn * kg_ref[...])
        ln_ref[:, :Q_LORA] = cqn
        ln_ref[:, Q_LORA:] = ckn
        q = _dot(cqn, wuq_ref[...])
        kn = _dot(ckn, wuk_ref[...])
        vv = _dot(ckn, wuv_ref[...])
        v_ref[...] = _bf(vv)
        kpe = _rope_pe(_dot(hb, wk_ref[...]), c, s1, s2)
        for h in range(H):
            o = h * MLA_QK
            qh = jnp.concatenate([q[:, o:o + MLA_NOPE], _rope_pe(q[:, o + MLA_NOPE:o + MLA_QK], c, s1, s2)], axis=1)
            kh = jnp.concatenate([kn[:, h * MLA_NOPE:(h + 1) * MLA_NOPE], kpe], axis=1)
            q_ref[:, o:o + MLA_QK] = _bf(qh)
            k_ref[:, o:o + MLA_QK] = _bf(kh)
            qt_ref[h] = _bf(qh.T)
            kt_ref[h] = _bf(kh.T)
            vt_ref[h] = _bf(vv[:, h * MLA_DV:(h + 1) * MLA_DV].T)

    lat_w = Q_LORA + KV_LORA
    return _rowcall("proj_mla", body, T, tm, [h1, *tabs], [w_c, w_kpe, w_g, w_uq, w_uk, w_uv, qn_g, kvn_g],
                    [(lat_w, F32), (2 * D, F32), (H * MLA_QK, BF16), (H * MLA_QK, BF16), (H * MLA_DV, BF16),
                     (lat_w, BF16)],
                    tiled_outs=[_transposed_blocks(T, tm, MLA_QK, BF16), _transposed_blocks(T, tm, MLA_QK, BF16),
                                _transposed_blocks(T, tm, MLA_DV, BF16)])


def _ret_block(T):
    return min(256, T)


def _ret_decay(lg, bt):
    n = lax.broadcasted_iota(jnp.int32, (bt, bt), 0)
    m = lax.broadcasted_iota(jnp.int32, (bt, bt), 1)
    dmat = jnp.where(_chunk_of(m) <= _chunk_of(n), jnp.exp(lg * jnp.abs(n - m).astype(F32)), 0.0)
    pos = lax.broadcasted_iota(jnp.int32, (bt, 1), 0).astype(F32)
    xi = jnp.exp(lg * (pos + 1.0))
    zeta = jnp.exp(lg * (bt - 1.0 - pos))
    return dmat, xi, zeta, jnp.exp(lg * bt)


def _ret_specs(bt, rev, nb):
    def blk(w):
        if rev:
            return pl.BlockSpec((bt, w), lambda h, b: (nb - 1 - b, h))
        return pl.BlockSpec((bt, w), lambda h, b: (b, h))
    return pl.BlockSpec((None, 1, LANES), lambda h, b: (h, 0, 0)), blk


def _ret_fwd(rq, rk, rv, lgam):
    T = rq.shape[0]
    bt = _ret_block(T)
    nb = T // bt
    lg_spec, blk = _ret_specs(bt, False, nb)

    def kern(lg_ref, q_ref, k_ref, v_ref, y_ref, s_ref):
        @pl.when(pl.program_id(1) == 0)
        def _():
            s_ref[...] = jnp.zeros_like(s_ref)

        dmat, xi, zeta, gb = _ret_decay(lg_ref[:, :1], bt)
        q, k, v = q_ref[...], k_ref[...], v_ref[...]
        sc = _dot_nt(q, k) * dmat
        y_ref[...] = _dot(_bf(sc), v) + _dot(q, _bf(s_ref[...])) * xi
        s_ref[...] = s_ref[...] * gb + _dot_tn(_bf(k.astype(F32) * zeta), v)

    return pl.pallas_call(
        kern, grid=(RET_HEADS, nb), in_specs=[lg_spec, blk(RET_DK), blk(RET_DK), blk(RET_DV)],
        out_specs=blk(RET_DV), out_shape=jax.ShapeDtypeStruct((T, RET_HEADS * RET_DV), F32),
        scratch_shapes=[pltpu.VMEM((RET_DK, RET_DV), F32)], name="ret_fwd",
        compiler_params=_params(("arbitrary", "arbitrary")))(lgam, rq, rk, rv)


def _ret_bwd_q(rq, rk, rv, dy, lgam):
    T = rq.shape[0]
    bt = _ret_block(T)
    nb = T // bt
    lg_spec, blk = _ret_specs(bt, False, nb)

    def kern(lg_ref, k_ref, v_ref, dy_ref, dq_ref, s_ref):
        @pl.when(pl.program_id(1) == 0)
        def _():
            s_ref[...] = jnp.zeros_like(s_ref)

        dmat, xi, zeta, gb = _ret_decay(lg_ref[:, :1], bt)
        k, v, dy = k_ref[...], v_ref[...], dy_ref[...]
        dp = _dot_nt(dy, v) * dmat
        dq_ref[...] = _dot(_bf(dp), k) + _dot_nt(dy, _bf(s_ref[...])) * xi
        s_ref[...] = s_ref[...] * gb + _dot_tn(_bf(k.astype(F32) * zeta), v)

    return pl.pallas_call(
        kern, grid=(RET_HEADS, nb), in_specs=[lg_spec, blk(RET_DK), blk(RET_DV), blk(RET_DV)],
        out_specs=blk(RET_DK), out_shape=jax.ShapeDtypeStruct((T, RET_HEADS * RET_DK), F32),
        scratch_shapes=[pltpu.VMEM((RET_DK, RET_DV), F32)], name="ret_bwd_q",
        compiler_params=_params(("arbitrary", "arbitrary")))(lgam, rk, rv, dy)


def _ret_bwd_kv(rq, rk, rv, dy, lgam):
    T = rq.shape[0]
    bt = _ret_block(T)
    nb = T // bt
    lg_spec, blk = _ret_specs(bt, True, nb)

    def kern(lg_ref, q_ref, k_ref, v_ref, dy_ref, dk_ref, dv_ref, g_ref):
        @pl.when(pl.program_id(1) == 0)
        def _():
            g_ref[...] = jnp.zeros_like(g_ref)

        dmat, xi, zeta, gb = _ret_decay(lg_ref[:, :1], bt)
        q, k, v, dy = q_ref[...], k_ref[...], v_ref[...], dy_ref[...]
        gs = _bf(g_ref[...])
        p = _dot_nt(q, k) * dmat
        dp = _dot_nt(dy, v) * dmat
        dv_ref[...] = _bf(_dot_tn(_bf(p), dy) + _dot(k, gs) * zeta)
        dk_ref[...] = _dot_tn(_bf(dp), q) + _dot_nt(v, gs) * zeta
        g_ref[...] = g_ref[...] * gb + _dot_tn(_bf(q.astype(F32) * xi), dy)

    return pl.pallas_call(
        kern, grid=(RET_HEADS, nb), in_specs=[lg_spec, blk(RET_DK), blk(RET_DK), blk(RET_DV), blk(RET_DV)],
        out_specs=[blk(RET_DK), blk(RET_DV)],
        out_shape=[jax.ShapeDtypeStruct((T, RET_HEADS * RET_DK), F32),
                   jax.ShapeDtypeStruct((T, RET_HEADS * RET_DV), BF16)],
        scratch_shapes=[pltpu.VMEM((RET_DK, RET_DV), F32)], name="ret_bwd_kv",
        compiler_params=_params(("arbitrary", "arbitrary")))(lgam, rq, rk, rv, dy)


def _attn_mask(tb):
    r = lax.broadcasted_iota(jnp.int32, (tb, tb), 0)
    c = lax.broadcasted_iota(jnp.int32, (tb, tb), 1)
    return _chunk_of(c) <= _chunk_of(r)


def _attn_mask_t(tb):
    key = lax.broadcasted_iota(jnp.int32, (tb, tb), 0)
    qry = lax.broadcasted_iota(jnp.int32, (tb, tb), 1)
    return _chunk_of(key) <= _chunk_of(qry)


ATTN_SCALE = (MLA_NOPE + MLA_ROPE) ** -0.5
MASKED = -1e30
LOG2E = 1.4426950408889634
SUBLANES = 8


def _head_blocks(nb, w, tb):
    return pl.BlockSpec((nb, None, w, tb), lambda h, i: (0, h, 0, 0))


def _one_block(w, tb):
    return pl.BlockSpec((None, None, w, tb), lambda h, i: (i, h, 0, 0))


def _attn_fwd(k, qt, vt):
    T = k.shape[0]
    tb = _attn_block(T)
    nb = T // tb

    def kern(qt_ref, k_ref, vt_ref, o_ref, lser_ref, m_ref, l_ref, acc_ref):
        qb = pl.program_id(1)
        qt = qt_ref[...]
        m_ref[...] = jnp.full_like(m_ref, MASKED)
        l_ref[...] = jnp.zeros_like(l_ref)
        acc_ref[...] = jnp.zeros_like(acc_ref)

        def step(kb, diagonal):
            rows = pl.ds(pl.multiple_of(kb * tb, tb), tb)
            s = _dot(k_ref[rows, :], qt) * (ATTN_SCALE * LOG2E)
            if diagonal:
                s = jnp.where(_attn_mask_t(tb), s, MASKED)
            m_old = m_ref[...]
            m_new = jnp.maximum(m_old, jnp.max(s, axis=0, keepdims=True))
            p = jnp.exp2(s - m_new)
            corr = jnp.exp2(m_old - m_new)
            l_ref[...] = l_ref[...] * corr + jnp.sum(p, axis=0, keepdims=True)
            acc_ref[...] = acc_ref[...] * corr + _dot(vt_ref[kb], _bf(p))
            m_ref[...] = m_new

        def pair_body(i, carry):
            step(2 * i, False)
            step(2 * i + 1, False)
            return carry

        lax.fori_loop(0, qb // 2, pair_body, 0)

        @pl.when(qb % 2 == 1)
        def _():
            step(qb - 1, False)

        step(qb, True)
        o_ref[...] = (acc_ref[...] / l_ref[...]).T
        lser_ref[...] = jnp.broadcast_to(m_ref[...] + jnp.log2(l_ref[...]), (SUBLANES, tb))

    return pl.pallas_call(
        kern, grid=(MLA_HEADS, nb),
        in_specs=[_one_block(MLA_QK, tb), pl.BlockSpec((T, MLA_QK), lambda h, i: (0, h)),
                  _head_blocks(nb, MLA_DV, tb)],
        out_specs=[pl.BlockSpec((tb, MLA_DV), lambda h, i: (i, h)), _one_block(SUBLANES, tb)],
        out_shape=[jax.ShapeDtypeStruct((T, MLA_HEADS * MLA_DV), F32),
                   jax.ShapeDtypeStruct((nb, MLA_HEADS, SUBLANES, tb), F32)],
        scratch_shapes=[pltpu.VMEM((1, tb), F32), pltpu.VMEM((1, tb), F32), pltpu.VMEM((MLA_DV, tb), F32)],
        name="attn_fwd", compiler_params=_params(("arbitrary", "arbitrary")))(qt, k, vt)


def _attn_bwd(q, k, v, do, qt, kt, dot_, lse_rows, delta_rows):
    T = q.shape[0]
    tb = _attn_block(T)
    nb = T // tb

    def kern(q_ref, k_ref, v_ref, do_ref, qt_ref, kt_ref, dot_ref, lse_ref, dl_ref, dk_ref, dv_ref, dqt_ref, dv_acc):
        kb = pl.program_id(1)
        kv, vv, ktv = k_ref[...], v_ref[...], kt_ref[...]
        dk_ref[...] = jnp.zeros_like(dk_ref)
        dv_acc[...] = jnp.zeros_like(dv_acc)

        @pl.when(kb == 0)
        def _():
            dqt_ref[...] = jnp.zeros_like(dqt_ref)

        def step(qb, diagonal):
            rows = pl.ds(pl.multiple_of(qb * tb, tb), tb)
            s = _dot(kv, qt_ref[qb]) * (ATTN_SCALE * LOG2E)
            if diagonal:
                s = jnp.where(_attn_mask_t(tb), s, MASKED)
            p = jnp.exp2(s - lse_ref[qb][:1, :])
            dv_acc[...] += _dot(_bf(p), do_ref[rows, :])
            ds = _bf(p * (_dot(vv, dot_ref[qb]) - dl_ref[qb][:1, :]) * ATTN_SCALE)
            dk_ref[...] += _dot(ds, q_ref[rows, :])
            dqt_ref[qb] += _dot(ktv, ds)

        def loop_body(qb, carry):
            step(qb, False)
            return carry

        step(kb, True)
        lax.fori_loop(kb + 1, nb, loop_body, 0)
        dv_ref[...] = _bf(dv_acc[...])

    def blk(w):
        return pl.BlockSpec((tb, w), lambda h, i: (i, h))

    def full(w):
        return pl.BlockSpec((T, w), lambda h, i: (0, h))

    return pl.pallas_call(
        kern, grid=(MLA_HEADS, nb),
        in_specs=[full(MLA_QK), blk(MLA_QK), blk(MLA_DV), full(MLA_DV), _head_blocks(nb, MLA_QK, tb),
                  _one_block(MLA_QK, tb), _head_blocks(nb, MLA_DV, tb), _head_blocks(nb, SUBLANES, tb),
                  _head_blocks(nb, SUBLANES, tb)],
        out_specs=[blk(MLA_QK), blk(MLA_DV), _head_blocks(nb, MLA_QK, tb)],
        out_shape=[jax.ShapeDtypeStruct((T, MLA_HEADS * MLA_QK), F32),
                   jax.ShapeDtypeStruct((T, MLA_HEADS * MLA_DV), BF16),
                   jax.ShapeDtypeStruct((nb, MLA_HEADS, MLA_QK, tb), F32)],
        scratch_shapes=[pltpu.VMEM((tb, MLA_DV), F32)],
        name="attn_bwd", compiler_params=_params(("arbitrary", "arbitrary")))(
            q, k, v, do, qt, kt, dot_, lse_rows, delta_rows)


def _group_norm(y):
    yc = y - _mean(y)
    rstd = lax.rsqrt(_mean(yc * yc) + EPS)
    return yc * rstd, rstd


def _mix_fwd(y, rg, o, gates, h1, gn_g, w_ret_o, w_mla_o, w_out, ln_g, ln_b, tm):
    T, D = h1.shape

    def body(i, y_ref, rg_ref, o_ref, gt_ref, h_ref, gn_ref, wr_ref, wm_ref, wo_ref, g_ref, b_ref,
             h2_ref, z_ref, yret_ref, ymla_ref, yr_ref, mix_ref):
        for h in range(RET_HEADS):
            sl = slice(h * RET_DV, (h + 1) * RET_DV)
            yn, _ = _group_norm(y_ref[:, sl])
            r = rg_ref[:, sl]
            yr_ref[:, sl] = _bf(r * _sigmoid(r) * (yn * gn_ref[:, sl]))
        yret = _dot(yr_ref[...], wr_ref[...])
        ymla = _dot(_bf(o_ref[...]), wm_ref[...])
        yret_ref[...] = yret
        ymla_ref[...] = ymla
        mix = _bf(_sigmoid(gt_ref[:, :D]) * yret + _sigmoid(gt_ref[:, D:]) * ymla)
        mix_ref[...] = mix
        z = ALPHA * h_ref[...] + _dot(mix, wo_ref[...])
        xhat, _ = _ln_stats(z)
        z_ref[...] = z
        h2_ref[...] = xhat * g_ref[...] + b_ref[...]

    return _rowcall("mix_fwd", body, T, tm, [y, rg, o, gates, h1], [gn_g, w_ret_o, w_mla_o, w_out, ln_g, ln_b],
                    [(D, F32), (D, F32), (D, F32), (D, F32), (RET_HEADS * RET_DV, BF16), (D, BF16)])


def _mix_bwd(dh2, z1, gates, yret, ymla, y, rg, o, gn_g, w_ret_o, w_mla_o, w_out, ln_g, tm):
    T, D = dh2.shape
    rv = RET_HEADS * RET_DV

    def body(i, dh_ref, z_ref, gt_ref, yret_ref, ymla_ref, y_ref, rg_ref, o_ref, gn_ref, wr_ref, wm_ref, wo_ref, g_ref,
             dz_ref, dgt_ref, drg_ref, dy_ref, do_ref, dyret_ref, dymla_ref, dg_ref, db_ref, dgn_ref, dot_ref,
             dl_ref):
        xhat, rstd = _ln_stats(z_ref[...])
        dz, dg, db = _ln_bwd(dh_ref[...], xhat, rstd, g_ref[...])
        _acc(i, dg_ref, dg)
        _acc(i, db_ref, db)
        dz_ref[...] = dz
        dmix = _dot_nt(_bf(dz), wo_ref[...])
        sr = _sigmoid(gt_ref[:, :D])
        sm = _sigmoid(gt_ref[:, D:])
        dgt_ref[:, :D] = _bf(dmix * yret_ref[...] * sr * (1.0 - sr))
        dgt_ref[:, D:] = _bf(dmix * ymla_ref[...] * sm * (1.0 - sm))
        dyret = _bf(dmix * sr)
        dymla = _bf(dmix * sm)
        dyret_ref[...] = dyret
        dymla_ref[...] = dymla
        dov = _dot_nt(dymla, wm_ref[...])
        do_ref[...] = _bf(dov)
        for h in range(MLA_HEADS):
            sl = slice(h * MLA_DV, (h + 1) * MLA_DV)
            dot_ref[h] = _bf(dov[:, sl].T)
            delta = jnp.sum(dov[:, sl] * o_ref[:, sl], axis=-1, keepdims=True)
            dl_ref[h] = jnp.broadcast_to(delta, (tm, LANES)).T[:SUBLANES, :]
        dyr = _dot_nt(dyret, wr_ref[...])
        dgn = []
        for h in range(RET_HEADS):
            sl = slice(h * RET_DV, (h + 1) * RET_DV)
            yn, grstd = _group_norm(y_ref[:, sl])
            r = rg_ref[:, sl]
            sig = _sigmoid(r)
            d = dyr[:, sl]
            drg_ref[:, sl] = _bf(d * (yn * gn_ref[:, sl]) * sig * (1.0 + r * (1.0 - sig)))
            dt = d * (r * sig)
            dgn.append(jnp.sum(dt * yn, axis=0, keepdims=True))
            dyn = dt * gn_ref[:, sl]
            dy_ref[:, sl] = _bf(grstd * (dyn - _mean(dyn) - yn * _mean(dyn * yn)))
        _acc(i, dgn_ref, jnp.concatenate(dgn, axis=1))

    return _rowcall("mix_bwd", body, T, tm, [dh2, z1, gates, yret, ymla, y, rg, o],
                    [gn_g, w_ret_o, w_mla_o, w_out, ln_g],
                    [(D, F32), (2 * D, BF16), (rv, BF16), (rv, BF16), (MLA_HEADS * MLA_DV, BF16), (D, BF16), (D, BF16)],
                    [((1, D), F32), ((1, D), F32), ((1, rv), F32)],
                    tiled_outs=[_transposed_blocks(T, tm, MLA_DV, BF16), _transposed_blocks(T, tm, SUBLANES, F32)])


def _proj_mla_bwd(dqt, dk, dv, lat, tabs, w_uq, w_uk, w_uv, qn_g, kvn_g, tm):
    T = dk.shape[0]
    H = MLA_HEADS
    lat_w = Q_LORA + KV_LORA

    def body(i, dk_ref, dv_ref, lat_ref, c_ref, s1_ref, s2_ref, dqt_ref, wuq_ref, wuk_ref, wuv_ref, qg_ref, kg_ref,
             dlat_ref, dkpe_ref, dqb_ref, dkn_ref, dqg_ref, dkg_ref):
        c, s1, s2 = c_ref[...], s1_ref[...], s2_ref[...]
        dkpe = jnp.zeros((tm, LANES), F32)
        for h in range(H):
            o = h * MLA_QK
            dqh = dqt_ref[h].T
            dqb_ref[:, o:o + MLA_NOPE] = _bf(dqh[:, :MLA_NOPE])
            dqb_ref[:, o + MLA_NOPE:o + MLA_QK] = _bf(_rope_pe_bwd(dqh[:, MLA_NOPE:], c, s1, s2))
            dkn_ref[:, h * MLA_NOPE:(h + 1) * MLA_NOPE] = _bf(dk_ref[:, o:o + MLA_NOPE])
            dkpe += dk_ref[:, o + MLA_NOPE:o + MLA_QK]
        dkpe_ref[...] = _bf(_rope_pe_bwd(dkpe, c, s1, s2))
        dcqn = _dot_nt(dqb_ref[...], wuq_ref[...])
        dckn = _dot_nt(dkn_ref[...], wuk_ref[...]) + _dot_nt(dv_ref[...], wuv_ref[...])
        for dn, x, g_ref, dg_ref, sl in ((dcqn, lat_ref[:, :Q_LORA], qg_ref, dqg_ref, slice(0, Q_LORA)),
                                         (dckn, lat_ref[:, Q_LORA:], kg_ref, dkg_ref, slice(Q_LORA, lat_w))):
            xn, r = _rms(x, None)
            _acc(i, dg_ref, jnp.sum(dn * xn, axis=0, keepdims=True))
            dxn = dn * g_ref[...]
            dlat_ref[:, sl] = _bf(r * (dxn - xn * _mean(dxn * xn)))

    dqt_shape, dqt_spec = _transposed_blocks(T, tm, MLA_QK, F32)
    assert dqt.shape == dqt_shape.shape
    return _rowcall("proj_mla_bwd", body, T, tm, [dk, dv, lat, *tabs], [w_uq, w_uk, w_uv, qn_g, kvn_g],
                    [(lat_w, BF16), (LANES, BF16), (H * MLA_QK, BF16), (H * MLA_NOPE, BF16)],
                    [((1, Q_LORA), F32), ((1, KV_LORA), F32)], tiled_ins=[(dqt, dqt_spec)])


def _proj_bwd(drq, drk, drv, drg, dz1, dlat, dkpe, dgates, cos_r, sin_r, w_r, w_c, w_kpe, w_g, tm):
    T, D = dz1.shape
    qk = RET_HEADS * RET_DK
    rv = RET_HEADS * RET_DV

    def body(i, drq_ref, drk_ref, drv_ref, drg_ref, dz_ref, dlat_ref, dkpe_ref, dgt_ref, cos_ref, sin_ref,
             wr_ref, wc_ref, wk_ref, wg_ref, dh_ref, dpr_ref):
        cos, sin = cos_ref[...], sin_ref[...]
        for src, off, scale in ((drq_ref, 0, 1.0), (drk_ref, qk, RET_DK ** -0.5)):
            for h in range(RET_HEADS):
                d = src[:, h * RET_DK:(h + 1) * RET_DK]
                dpr_ref[:, off + h * RET_DK:off + (h + 1) * RET_DK] = _bf(
                    (d * cos + _roll(d * sin, RET_DK // 2)) * scale)
        dpr_ref[:, 2 * qk:2 * qk + rv] = drv_ref[...]
        dpr_ref[:, 2 * qk + rv:] = drg_ref[...]
        dh_ref[...] = (ALPHA * dz_ref[...] + _dot_nt(dpr_ref[...], wr_ref[...]) + _dot_nt(dlat_ref[...], wc_ref[...])
                       + _dot_nt(dkpe_ref[...], wk_ref[...]) + _dot_nt(dgt_ref[...], wg_ref[...]))

    return _rowcall("proj_bwd", body, T, tm, [drq, drk, drv, drg, dz1, dlat, dkpe, dgates, cos_r, sin_r],
                    [w_r, w_c, w_kpe, w_g], [(D, F32), (2 * qk + 2 * rv, BF16)])


def _ple_loss(h3, p, target, w_gate, w_proj, ln_g, ln_b, tm):
    T, D = h3.shape

    def body(i, h_ref, p_ref, t_ref, wg_ref, wp_ref, g_ref, b_ref, dh_ref, dgp_ref, dpp_ref, loss_ref, dg_ref, db_ref):
        hv = h_ref[...]
        sg = _sigmoid(_dot(_bf(hv), wg_ref[...]))
        pp = _dot(_bf(p_ref[...]), wp_ref[...])
        xhat, rstd = _ln_stats(ALPHA * hv + sg * pp)
        err = xhat * g_ref[...] + b_ref[...] - t_ref[...]
        row_loss = 0.5 * _mean(err * err)
        _acc(i, loss_ref, jnp.broadcast_to(jnp.sum(row_loss, axis=0, keepdims=True), (1, LANES)))
        dz, dg, db = _ln_bwd(err * (1.0 / D), xhat, rstd, g_ref[...])
        _acc(i, dg_ref, dg)
        _acc(i, db_ref, db)
        dgp = _bf(dz * pp * sg * (1.0 - sg))
        dgp_ref[...] = dgp
        dpp_ref[...] = _bf(dz * sg)
        dh_ref[...] = ALPHA * dz + _dot_nt(dgp, wg_ref[...])

    return _rowcall("ple_loss", body, T, tm, [h3, p, target], [w_gate, w_proj, ln_g, ln_b],
                    [(D, F32), (D, BF16), (D, BF16)], [((1, LANES), F32), ((1, D), F32), ((1, D), F32)])


def _ewise(name, fn, ins, n_out, out_dtype=F32):
    r, c = ins[0].shape
    tr = _tile(r, max(8, (1 << 19) // c // 8 * 8), 8)

    def kern(*refs):
        outs = fn(*[x[...] for x in refs[:len(ins)]])
        for o_ref, o in zip(refs[len(ins):], outs):
            o_ref[...] = o.astype(out_dtype)

    spec = pl.BlockSpec((tr, c), lambda i: (i, 0))
    return pl.pallas_call(kern, grid=(r // tr,), in_specs=[spec] * len(ins), out_specs=[spec] * n_out,
                          out_shape=[jax.ShapeDtypeStruct((r, c), out_dtype)] * n_out, name=name,
                          compiler_params=_params(("arbitrary",)))(*ins)


def _adamw_math(w, g, m, v):
    m = ADAM_B1 * m + (1.0 - ADAM_B1) * g
    v = ADAM_B2 * v + (1.0 - ADAM_B2) * (g * g)
    m_hat = m / (1.0 - ADAM_B1 ** ADAM_STEP)
    v_hat = v / (1.0 - ADAM_B2 ** ADAM_STEP)
    return -ADAM_LR * (m_hat / (jnp.sqrt(v_hat) + ADAM_EPS) + ADAM_WD * w), m, v


def _adamw(name, w, g, m, v):
    shape = w.shape
    c = shape[-1]
    flat = [t.reshape(-1, c) for t in (w, g, m, v)]
    return [t.reshape(shape) for t in _ewise(name, _adamw_math, flat, 3)]


def _place():
    return lax.axis_index("x"), lax.axis_index("y"), lax.axis_index("c")


def _dma_sems(n):
    return [pltpu.SemaphoreType.DMA((n,)), pltpu.SemaphoreType.DMA((n,))]


N_PEER_CHIPS = N_CHIPS - 1


def _chips_exchange(name, srcs, broadcast):
    n = len(srcs)

    def kern(*refs):
        src_refs, out_refs = refs[:n], refs[n:2 * n]
        send_sems, recv_sems = refs[2 * n:]
        x, y, c = _place()
        peers = [(1 - x, y), (x, 1 - y), (1 - x, 1 - y)]

        def copy(a, j):
            px, py = peers[j]
            piece = src_refs[a].at[c] if broadcast else src_refs[a].at[2 * px + py]
            return pltpu.make_async_remote_copy(
                src_ref=piece, dst_ref=out_refs[a].at[j], send_sem=send_sems.at[a * N_PEER_CHIPS + j],
                recv_sem=recv_sems.at[a * N_PEER_CHIPS + j], device_id=(px, py, c), device_id_type=MESH)

        cps = [copy(a, j) for j in range(N_PEER_CHIPS) for a in range(n)]
        for cp in cps:
            cp.start()
        for cp in cps:
            cp.wait_recv()
        for cp in cps:
            cp.wait_send()

    return pl.pallas_call(
        kern, out_shape=[jax.ShapeDtypeStruct((N_PEER_CHIPS,) + s.shape[1:], s.dtype) for s in srcs],
        in_specs=[HBM_SPEC] * n, out_specs=[HBM_SPEC] * n, scratch_shapes=_dma_sems(n * N_PEER_CHIPS),
        name=name)(*srcs)


def _sibling_swap(name, srcs, halves):
    n = len(srcs)

    def kern(*refs):
        src_refs, out_refs = refs[:n], refs[n:2 * n]
        send_sems, recv_sems = refs[2 * n:]
        x, y, c = _place()

        def copy(a):
            piece = src_refs[a].at[:, 1 - c] if halves else src_refs[a]
            return pltpu.make_async_remote_copy(
                src_ref=piece, dst_ref=out_refs[a], send_sem=send_sems.at[a], recv_sem=recv_sems.at[a],
                device_id=(x, y, 1 - c), device_id_type=MESH)

        cps = [copy(a) for a in range(n)]
        for cp in cps:
            cp.start()
        for cp in cps:
            cp.wait_recv()
        for cp in cps:
            cp.wait_send()

    def out_shape(s):
        return jax.ShapeDtypeStruct((s.shape[0],) + s.shape[2:] if halves else s.shape, s.dtype)

    return pl.pallas_call(
        kern, out_shape=[out_shape(s) for s in srcs], in_specs=[HBM_SPEC] * n, out_specs=[HBM_SPEC] * n,
        scratch_shapes=_dma_sems(n), name=name)(*srcs)


def _all_devices(name, src, reduce):
    r, c = src.shape
    n_dev = 2 * N_CHIPS

    def kern(src_ref, out_ref, *scratch):
        if reduce:
            gat_ref, send_sems, recv_sems = scratch
        else:
            gat_ref = out_ref
            send_sems, recv_sems = scratch
        x, y, cc = _place()
        me = 4 * x + 2 * y + cc
        gat_ref[me] = src_ref[...]
        peers = []
        for j in range(1, n_dev):
            px = 1 - x if j & 4 else x
            py = 1 - y if j & 2 else y
            pc = 1 - cc if j & 1 else cc
            peers.append((px, py, pc))

        def copy(j, peer, slot):
            return pltpu.make_async_remote_copy(
                src_ref=src_ref, dst_ref=gat_ref.at[slot], send_sem=send_sems.at[j], recv_sem=recv_sems.at[j],
                device_id=peer, device_id_type=MESH)

        sends = [copy(j, peer, me) for j, peer in enumerate(peers)]
        for cp in sends:
            cp.start()
        for j, (px, py, pc) in enumerate(peers):
            copy(j, (px, py, pc), 4 * px + 2 * py + pc).wait_recv()
        for cp in sends:
            cp.wait_send()
        if reduce:
            total = gat_ref[0]
            for d in range(1, n_dev):
                total = total + gat_ref[d]
            out_ref[...] = total

    out_shape = jax.ShapeDtypeStruct((r, c) if reduce else (n_dev, r, c), src.dtype)
    scratch = ([pltpu.VMEM((n_dev, r, c), src.dtype)] if reduce else []) + _dma_sems(n_dev - 1)
    return pl.pallas_call(kern, out_shape=out_shape, in_specs=[VMEM_SPEC], out_specs=VMEM_SPEC,
                          scratch_shapes=scratch, name=name)(src)


ROW_GROUP = ("ffn1_w_out", "w_ret_o", "w_mla_o", "w_out", "ffn2_w_out", "ple_w_gate")
COL_GROUP = ("w_uq", "w_ukv", "ple_w_proj")


def _to_groups(t, axis):
    return [jnp.concatenate([t[n] for n in ROW_GROUP], axis=axis), t["ffn1_w_in"], t["ffn2_w_in"], t["w_in"],
            jnp.concatenate([t[n] for n in COL_GROUP], axis=axis + 1)]


def _from_groups(groups, shapes, axis):
    rows, ffn1, ffn2, w_in, cols = groups
    out = {"ffn1_w_in": ffn1, "ffn2_w_in": ffn2, "w_in": w_in}
    off = 0
    for n in ROW_GROUP:
        out[n] = lax.slice_in_dim(rows, off, off + shapes[n][0], axis=axis)
        off += shapes[n][0]
    off = 0
    for n in COL_GROUP:
        out[n] = lax.slice_in_dim(cols, off, off + shapes[n][1], axis=axis + 1)
        off += shapes[n][1]
    return out


def _halves(t, axis):
    return t.reshape(t.shape[:axis] + (2, t.shape[axis] // 2) + t.shape[axis + 1:])


def _by_core(mine, theirs, axis):
    c = lax.axis_index("c")
    both = jnp.where(c == 0, jnp.stack([mine, theirs], axis), jnp.stack([theirs, mine], axis))
    return both.reshape(both.shape[:axis] + (2 * both.shape[axis + 1],) + both.shape[axis + 2:])


def _chip_order(own, others):
    me = 2 * lax.axis_index("x") + lax.axis_index("y")
    cands = jnp.concatenate([own[None], others], axis=0)
    slot_of_flip = (0, 2, 1, 3)
    pick = jnp.asarray(slot_of_flip, jnp.int32)[jnp.arange(N_CHIPS, dtype=jnp.int32) ^ me]
    return jnp.stack([lax.dynamic_index_in_dim(cands, pick[k], 0, keepdims=False) for k in range(N_CHIPS)])


def _join_shards(name, shards):
    _, r, c = shards.shape
    if name in COL_SHARDED:
        return shards.transpose(1, 0, 2).reshape(r, N_CHIPS * c)
    return shards.reshape(N_CHIPS * r, c)


def _split_shards(name, full):
    if full.ndim == 3:
        return full
    r, c = full.shape
    if name in COL_SHARDED:
        return jnp.stack([full[:, k * (c // N_CHIPS):(k + 1) * (c // N_CHIPS)] for k in range(N_CHIPS)])
    return full.reshape(N_CHIPS, r // N_CHIPS, c)


def _rope_tables(positions):
    pos = positions.reshape(-1).astype(F32)[:, None]
    half = RET_DK // 2
    ang = pos * (ROPE_BASE ** (-jnp.arange(half, dtype=F32) / half))
    cos_r = jnp.concatenate([jnp.cos(ang)] * 2, axis=1)
    sin_r = jnp.concatenate([-jnp.sin(ang), jnp.sin(ang)], axis=1)
    half = MLA_ROPE // 2
    ang = pos * (ROPE_BASE ** (-jnp.arange(half, dtype=F32) / half))
    zeros = jnp.zeros_like(ang)
    rest = LANES - MLA_ROPE
    c = jnp.concatenate([jnp.cos(ang)] * 2 + [jnp.ones((ang.shape[0], rest), F32)], axis=1)
    s1 = jnp.concatenate([-jnp.sin(ang), zeros, jnp.zeros((ang.shape[0], rest), F32)], axis=1)
    s2 = jnp.concatenate([zeros, jnp.sin(ang), jnp.zeros((ang.shape[0], rest), F32)], axis=1)
    return cos_r, sin_r, (c, s1, s2)


def _local_step(x, p, positions, target, w, ln_g, ln_b, gn_g, qn_g, kvn_g):
    T, D = x.shape
    tm = min(256, T)
    H = MLA_HEADS
    qk, rv = RET_HEADS * RET_DK, RET_HEADS * RET_DV
    cos_r, sin_r, tabs = _rope_tables(positions)
    lgam = jnp.broadcast_to(jnp.log(1.0 - 2.0 ** (-5.0 - jnp.arange(RET_HEADS, dtype=F32)))[:, None, None],
                            (RET_HEADS, 1, LANES))
    lng = [ln_g[k:k + 1] for k in range(N_LN)]
    lnb = [ln_b[k:k + 1] for k in range(N_LN)]

    w_in = w["w_in"]
    o_lat, o_kpe, o_gate = 2 * qk + 2 * rv, 2 * qk + 2 * rv + Q_LORA + KV_LORA, 2 * qk + 2 * rv + Q_LORA + KV_LORA + MLA_ROPE
    w_r, w_c = w_in[:, :o_lat], w_in[:, o_lat:o_kpe]
    w_kpe = jnp.pad(w_in[:, o_kpe:o_gate], ((0, 0), (0, LANES - MLA_ROPE)))
    w_g = w_in[:, o_gate:]
    w_uq = jnp.pad(w["w_uq"].reshape(Q_LORA, H, MLA_NOPE + MLA_ROPE),
                   ((0, 0), (0, 0), (0, MLA_QK - MLA_NOPE - MLA_ROPE))).reshape(Q_LORA, H * MLA_QK)
    w_ukv = w["w_ukv"].reshape(KV_LORA, H, MLA_NOPE + MLA_DV)
    w_uk = w_ukv[:, :, :MLA_NOPE].reshape(KV_LORA, H * MLA_NOPE)
    w_uv = w_ukv[:, :, MLA_NOPE:].reshape(KV_LORA, H * MLA_DV)

    h1, z0, a1 = _ffn_fwd("ffn1_fwd", x, w["ffn1_w_in"], w["ffn1_w_out"], lng[0], lnb[0], tm)
    rq, rk, rvv, rg = _proj_ret(h1, w_r, cos_r, sin_r, tm)
    lat, gates, q, k, v, latn, qt, kt, vt = _proj_mla(h1, tabs, w_c, w_kpe, w_g, w_uq, w_uk, w_uv, qn_g, kvn_g, tm)
    y = _ret_fwd(rq, rk, rvv, lgam)
    o, lse_rows = _attn_fwd(k, qt, vt)
    h2, z1, yret, ymla, yr, mix = _mix_fwd(y, rg, o, gates, h1, gn_g, w["w_ret_o"], w["w_mla_o"], w["w_out"],
                                           lng[1], lnb[1], tm)
    h3, z2, a2 = _ffn_fwd("ffn2_fwd", h2, w["ffn2_w_in"], w["ffn2_w_out"], lng[2], lnb[2], tm)

    dh3, dgp, dpp, loss, dg3, db3 = _ple_loss(h3, p, target, w["ple_w_gate"], w["ple_w_proj"], lng[3], lnb[3], tm)
    dh2, da2, s2, df2, dg2, db2 = _ffn_bwd("ffn2_bwd", dh3, z2, a2, w["ffn2_w_in"], w["ffn2_w_out"], lng[2], tm)
    (dz1, dgates, drg, dy, do, dyret, dymla, dg1, db1, dgn, dot_, delta_rows) = _mix_bwd(
        dh2, z1, gates, yret, ymla, y, rg, o, gn_g, w["w_ret_o"], w["w_mla_o"], w["w_out"], lng[1], tm)
    drq = _ret_bwd_q(rq, rk, rvv, dy, lgam)
    drk, drv = _ret_bwd_kv(rq, rk, rvv, dy, lgam)
    dk, dv, dqt = _attn_bwd(q, k, v, do, qt, kt, dot_, lse_rows, delta_rows)
    dlat, dkpe, dqb, dkn, dqg, dkg = _proj_mla_bwd(dqt, dk, dv, lat, tabs, w_uq, w_uk, w_uv, qn_g, kvn_g, tm)
    dh1, dpr = _proj_bwd(drq, drk, drv, drg, dz1, dlat, dkpe, dgates, cos_r, sin_r, w_r, w_c, w_kpe, w_g, tm)
    dx, da1, s1, df1, dg0, db0 = _ffn_bwd("ffn1_bwd", dh1, z0, a1, w["ffn1_w_in"], w["ffn1_w_out"], lng[0], tm)

    g_uq = _mm_tn("wg_uq", latn[:, :Q_LORA], dqb).reshape(Q_LORA, H, MLA_QK)[:, :, :MLA_NOPE + MLA_ROPE]
    g_uk = _mm_tn("wg_uk", latn[:, Q_LORA:], dkn).reshape(KV_LORA, H, MLA_NOPE)
    g_uv = _mm_tn("wg_uv", latn[:, Q_LORA:], dv).reshape(KV_LORA, H, MLA_DV)
    grads = {
        "ffn1_w_in": _mm_tn("wg_ffn1_in", x, da1, n_split=N_CHIPS),
        "ffn1_w_out": _mm_tn("wg_ffn1_out", s1, df1),
        "w_in": jnp.concatenate([_mm_tn("wg_in_r", h1, dpr), _mm_tn("wg_in_c", h1, dlat),
                                 _mm_tn("wg_in_kpe", h1, dkpe)[:, :MLA_ROPE], _mm_tn("wg_in_g", h1, dgates)], axis=1),
        "w_ret_o": _mm_tn("wg_ret_o", yr, dyret),
        "w_uq": g_uq.reshape(Q_LORA, H * (MLA_NOPE + MLA_ROPE)),
        "w_ukv": jnp.concatenate([g_uk, g_uv], axis=2).reshape(KV_LORA, H * (MLA_NOPE + MLA_DV)),
        "w_mla_o": _mm_tn("wg_mla_o", o, dymla),
        "w_out": _mm_tn("wg_out", mix, dz1),
        "ffn2_w_in": _mm_tn("wg_ffn2_in", h2, da2, n_split=N_CHIPS),
        "ffn2_w_out": _mm_tn("wg_ffn2_out", s2, df2),
        "ple_w_gate": _mm_tn("wg_ple_gate", h3, dgp),
        "ple_w_proj": _mm_tn("wg_ple_proj", p, dpp),
    }
    small = {"ln_g": jnp.concatenate([dg0, dg1, dg2, dg3], axis=0), "ln_b": jnp.concatenate([db0, db1, db2, db3], axis=0),
             "ret_gn_g": dgn, "q_norm_g": dqg, "kv_norm_g": dkg}
    return loss[0, 0], dx, grads, small


def _gather_weights(shards):
    own = _to_groups({n: _bf(shards[n]) for n in BIG_WEIGHTS}, 0)
    mine = _chips_exchange("gather_chips", [_halves(g, 0) for g in own], True)
    theirs = _sibling_swap("gather_cores", mine, False)
    groups = [_chip_order(g, _by_core(m, t, 1)) for g, m, t in zip(own, mine, theirs)]
    parts = _from_groups(groups, {n: shards[n].shape for n in BIG_WEIGHTS}, 1)
    return {n: t if n in ("ffn1_w_in", "ffn2_w_in") else _join_shards(n, t) for n, t in parts.items()}


def _reduce_grads(grads, shapes):
    c = lax.axis_index("c")
    me = 2 * lax.axis_index("x") + lax.axis_index("y")
    groups = [_halves(g, 1) for g in _to_groups({n: _split_shards(n, grads[n]) for n in BIG_WEIGHTS}, 1)]
    theirs = _sibling_swap("reduce_cores", groups, True)
    chip_sums = []
    for i, (g, t) in enumerate(zip(groups, theirs)):
        mine = lax.dynamic_index_in_dim(g, c, axis=1, keepdims=False)
        k, r, cc = mine.shape
        chip_sums.append(_ewise("reduce_cores_add%d" % i, lambda a, b: (a.astype(F32) + b.astype(F32),),
                                [mine.reshape(k * r, cc), t.reshape(k * r, cc)], 1, BF16)[0].reshape(k, r, cc))
    parts = _chips_exchange("reduce_chips", chip_sums, False)
    totals = []
    for i, (s, pt) in enumerate(zip(chip_sums, parts)):
        own = lax.dynamic_index_in_dim(s, me, axis=0, keepdims=False)
        totals.append(_ewise("reduce_chips_add%d" % i,
                             lambda a, b, c_, d: (((a.astype(F32) + b.astype(F32)) + c_.astype(F32)) + d.astype(F32),),
                             [own, pt[0], pt[1], pt[2]], 1, F32)[0])
    others = _sibling_swap("reduce_join", totals, False)
    return _from_groups([_by_core(t, o, 0) for t, o in zip(totals, others)], shapes, 0)


def kernel(x, p, positions, ln_g, ln_b, ffn1_w_in, ffn1_w_out, w_in, ret_gn_g, w_ret_o, q_norm_g, kv_norm_g, w_uq, w_ukv, w_mla_o, w_out, ffn2_w_in, ffn2_w_out, ple_w_gate, ple_w_proj, loss_target, m_ln_g, m_ln_b, m_ffn1_w_in, m_ffn1_w_out, m_w_in, m_ret_gn_g, m_w_ret_o, m_q_norm_g, m_kv_norm_g, m_w_uq, m_w_ukv, m_w_mla_o, m_w_out, m_ffn2_w_in, m_ffn2_w_out, m_ple_w_gate, m_ple_w_proj, v_ln_g, v_ln_b, v_ffn1_w_in, v_ffn1_w_out, v_w_in, v_ret_gn_g, v_w_ret_o, v_q_norm_g, v_kv_norm_g, v_w_uq, v_w_ukv, v_w_mla_o, v_w_out, v_ffn2_w_in, v_ffn2_w_out, v_ple_w_gate, v_ple_w_proj):
    names = ("ln_g", "ln_b", "ffn1_w_in", "ffn1_w_out", "w_in", "ret_gn_g", "w_ret_o", "q_norm_g", "kv_norm_g", "w_uq",
             "w_ukv", "w_mla_o", "w_out", "ffn2_w_in", "ffn2_w_out", "ple_w_gate", "ple_w_proj")
    weights = dict(zip(names, (ln_g, ln_b, ffn1_w_in, ffn1_w_out, w_in, ret_gn_g, w_ret_o, q_norm_g, kv_norm_g, w_uq,
                               w_ukv, w_mla_o, w_out, ffn2_w_in, ffn2_w_out, ple_w_gate, ple_w_proj)))
    m_in = dict(zip(names, (m_ln_g, m_ln_b, m_ffn1_w_in, m_ffn1_w_out, m_w_in, m_ret_gn_g, m_w_ret_o, m_q_norm_g,
                            m_kv_norm_g, m_w_uq, m_w_ukv, m_w_mla_o, m_w_out, m_ffn2_w_in, m_ffn2_w_out, m_ple_w_gate,
                            m_ple_w_proj)))
    v_in = dict(zip(names, (v_ln_g, v_ln_b, v_ffn1_w_in, v_ffn1_w_out, v_w_in, v_ret_gn_g, v_w_ret_o, v_q_norm_g,
                            v_kv_norm_g, v_w_uq, v_w_ukv, v_w_mla_o, v_w_out, v_ffn2_w_in, v_ffn2_w_out, v_ple_w_gate,
                            v_ple_w_proj)))
    chip = 2 * lax.axis_index("x") + lax.axis_index("y")
    D = x.shape[-1]
    dq = D // N_CHIPS

    shards = {n: weights[n][0] for n in BIG_WEIGHTS}
    w = _gather_weights(shards)
    ln_all = _all_devices("gather_ln", jnp.concatenate([ln_g[0], ln_b[0]], axis=0), False)
    ln_full = ln_all[::2].transpose(1, 0, 2).reshape(2 * N_LN, D)
    loss, dx, grads, small = _local_step(x[0], p[0, 0], positions, loss_target[0], w, ln_full[:N_LN], ln_full[N_LN:],
                                         ret_gn_g, q_norm_g, kv_norm_g)

    loss = lax.psum(loss, ("x", "y", "c"))
    big = _reduce_grads(grads, {n: shards[n].shape for n in BIG_WEIGHTS})
    small_names = ("ln_g", "ln_b", "ret_gn_g", "q_norm_g", "kv_norm_g")
    flat = jnp.concatenate([small[n].reshape(-1) for n in small_names])
    rows = -(-flat.shape[0] // LANES // 8) * 8
    flat = jnp.pad(flat, (0, rows * LANES - flat.shape[0])).reshape(rows, LANES)
    flat = _all_devices("reduce_small", flat, True).reshape(-1)
    off = 0
    for n in small_names:
        size = small[n].size
        small[n] = flat[off:off + size].reshape(small[n].shape)
        off += size
    g_out = dict(big)
    for n in ("ln_g", "ln_b"):
        g_out[n] = lax.dynamic_slice_in_dim(small[n], chip * dq, dq, axis=1)
    for n in ("ret_gn_g", "q_norm_g", "kv_norm_g"):
        g_out[n] = small[n]

    deltas, new_m, new_v = {}, {}, {}
    for n in names:
        g = g_out[n].reshape(weights[n].shape)
        g_out[n] = g
        deltas[n], new_m[n], new_v[n] = _adamw("adamw_" + n, weights[n], g, m_in[n], v_in[n])
    return (loss, dx[None], *[g_out[n] for n in names], *[deltas[n] for n in names], *[new_m[n] for n in names],
            *[new_v[n] for n in names])
```

```python
import functools

import jax
import jax.numpy as jnp
from jax import lax
from jax.experimental import pallas as pl
from jax.experimental.pallas import tpu as pltpu

D_MODEL = 1024
CHUNK = 64
D_PLE = 256
D_FF = 2816
RET_HEADS = 8
RET_DK = 128
RET_DV = 256
MLA_HEADS = 8
MLA_NOPE = 128
MLA_ROPE = 64
MLA_DV = 128
MLA_QK = 256
Q_LORA = 256
KV_LORA = 256
ROPE_BASE = 10000.0
EPS = 1e-5
N_LN = 4
ALPHA = 2.0 ** 0.25
ADAM_LR = 0.001
ADAM_B1 = 0.9
ADAM_B2 = 0.999
ADAM_EPS = 1e-08
ADAM_WD = 0.01
ADAM_STEP = 10

LANES = 128
VMEM_LIMIT = 60 << 20
N_CHIPS = 4

F32 = jnp.float32
BF16 = jnp.bfloat16
MESH = pl.DeviceIdType.MESH
HBM_SPEC = pl.BlockSpec(memory_space=pltpu.HBM)
VMEM_SPEC = pl.BlockSpec(memory_space=pltpu.VMEM)

BIG_WEIGHTS = ("ffn1_w_in", "ffn1_w_out", "w_in", "w_ret_o", "w_uq", "w_ukv", "w_mla_o", "w_out",
               "ffn2_w_in", "ffn2_w_out", "ple_w_gate", "ple_w_proj")
COL_SHARDED = ("ffn1_w_in", "w_in", "w_uq", "w_ukv", "ffn2_w_in", "ple_w_proj")


def _dot(a, b):
    return jnp.dot(a, b, preferred_element_type=F32)


def _dot_nt(a, b):
    return lax.dot_general(a, b, (((1,), (1,)), ((), ())), preferred_element_type=F32)


def _dot_tn(a, b):
    return lax.dot_general(a, b, (((0,), (0,)), ((), ())), preferred_element_type=F32)


def _bf(x):
    return x.astype(BF16)


def _sigmoid(x):
    return 1.0 / (1.0 + jnp.exp(-x))


def _mean(x):
    return jnp.mean(x, axis=-1, keepdims=True)


def _ln_stats(z):
    zc = z - _mean(z)
    rstd = lax.rsqrt(_mean(zc * zc) + EPS)
    return zc * rstd, rstd


def _ln_bwd(dy, xhat, rstd, g):
    dxhat = dy * g
    dz = rstd * (dxhat - _mean(dxhat) - xhat * _mean(dxhat * xhat))
    return dz, jnp.sum(dy * xhat, axis=0, keepdims=True), jnp.sum(dy, axis=0, keepdims=True)


def _roll(x, shift):
    return pltpu.roll(x, shift, 1)


def _chunk_of(idx):
    return jnp.right_shift(idx, CHUNK.bit_length() - 1)


def _tile(n, cap, mult=LANES):
    if n <= cap:
        return n
    for t in range(cap - cap % mult, 0, -mult):
        if n % t == 0:
            return t
    return n


def _zero_map(nd, *_):
    return (0,) * nd


def _params(sem):
    return pltpu.CompilerParams(dimension_semantics=sem, vmem_limit_bytes=VMEM_LIMIT)


def _rowcall(name, body, n_rows, tm, row_ins, full_ins, row_outs, acc_outs=(), tiled_outs=(), tiled_ins=(),
             exchange=None):
    n_steps = n_rows // tm
    ex_srcs, broadcast = exchange if exchange else ((), False)
    n_ex = len(ex_srcs)
    n_in = len(row_ins) + len(tiled_ins) + len(full_ins)
    n_out = len(row_outs) + len(acc_outs) + len(tiled_outs)

    def kern(*refs):
        step = pl.program_id(0)
        ex_in, ex_out = refs[n_in:n_in + n_ex], refs[n_in + n_ex + n_out:n_in + 2 * n_ex + n_out]
        sems = refs[n_in + 2 * n_ex + n_out:]
        if n_ex:
            @pl.when(step == 0)
            def _():
                for cp in _chip_copies(ex_in, ex_out, *sems, broadcast):
                    cp.start()

        body(step, *refs[:n_in], *refs[n_in + n_ex:n_in + n_ex + n_out])
        if n_ex:
            @pl.when(step == n_steps - 1)
            def _():
                _wait_copies(_chip_copies(ex_in, ex_out, *sems, broadcast))

    in_specs = [pl.BlockSpec((tm, a.shape[1]), lambda i: (i, 0)) for a in row_ins]
    in_specs += [spec for (_, spec) in tiled_ins]
    row_ins = list(row_ins) + [a for (a, _) in tiled_ins]
    in_specs += [pl.BlockSpec(a.shape, functools.partial(_zero_map, a.ndim), pipeline_mode=pl.Buffered(1))
                 for a in full_ins]
    in_specs += [HBM_SPEC] * n_ex
    out_specs = [pl.BlockSpec((tm, w), lambda i: (i, 0)) for (w, _) in row_outs]
    out_specs += [pl.BlockSpec(s, functools.partial(_zero_map, len(s))) for (s, _) in acc_outs]
    out_specs += [spec for (_, spec) in tiled_outs]
    out_specs += [HBM_SPEC] * n_ex
    out_shape = [jax.ShapeDtypeStruct((n_rows, w), dt) for (w, dt) in row_outs]
    out_shape += [jax.ShapeDtypeStruct(s, dt) for (s, dt) in acc_outs]
    out_shape += [shape for (shape, _) in tiled_outs]
    out_shape += _exchange_shapes(ex_srcs)
    return pl.pallas_call(kern, grid=(n_steps,), in_specs=in_specs, out_specs=out_specs, out_shape=out_shape,
                          scratch_shapes=_dma_sems(n_ex * N_PEER_CHIPS) if n_ex else [], name=name,
                          compiler_params=_params(("arbitrary",)))(*row_ins, *full_ins, *ex_srcs)


def _acc(step, ref, val):
    @pl.when(step == 0)
    def _():
        ref[...] = val

    @pl.when(step != 0)
    def _():
        ref[...] += val


def _ffn_fwd(name, x, w_in4, w_out, ln_g, ln_b, tm, exchange=None):
    T, D = x.shape
    fh = w_in4.shape[2]

    def body(i, x_ref, w4_ref, wo_ref, g_ref, b_ref, h_ref, z_ref, a_ref):
        xv = x_ref[...]
        xb = _bf(xv)
        f = jnp.zeros((tm, D), F32)
        for k in range(2):
            gk = _dot(xb, w4_ref[k])
            uk = _dot(xb, w4_ref[2 + k])
            a_ref[:, k * fh:(k + 1) * fh] = _bf(gk)
            a_ref[:, (2 + k) * fh:(3 + k) * fh] = _bf(uk)
            f += _dot(_bf(gk * _sigmoid(gk) * uk), wo_ref[k * fh:(k + 1) * fh, :])
        z = ALPHA * xv + 0.5 * f
        xhat, _ = _ln_stats(z)
        z_ref[...] = z
        h_ref[...] = xhat * g_ref[...] + b_ref[...]

    return _rowcall(name, body, T, tm, [x], [w_in4, w_out, ln_g, ln_b],
                    [(D, F32), (D, F32), (4 * fh, BF16)], exchange=exchange)


def _ffn_bwd(name, dh, z, a, w_in4, w_out, ln_g, tm, exchange=None):
    T, D = dh.shape
    fh = w_in4.shape[2]

    def body(i, dh_ref, z_ref, a_ref, w4_ref, wo_ref, g_ref, dx_ref, da_ref, s_ref, df_ref, dg_ref, db_ref):
        xhat, rstd = _ln_stats(z_ref[...])
        dz, dg, db = _ln_bwd(dh_ref[...], xhat, rstd, g_ref[...])
        _acc(i, dg_ref, dg)
        _acc(i, db_ref, db)
        dfb = _bf(0.5 * dz)
        df_ref[...] = dfb
        dx = ALPHA * dz
        for k in range(2):
            gk = a_ref[:, k * fh:(k + 1) * fh].astype(F32)
            uk = a_ref[:, (2 + k) * fh:(3 + k) * fh].astype(F32)
            ds = _dot_nt(dfb, wo_ref[k * fh:(k + 1) * fh, :])
            sig = _sigmoid(gk)
            silu = gk * sig
            dgk = _bf(ds * uk * sig * (1.0 + gk * (1.0 - sig)))
            duk = _bf(ds * silu)
            s_ref[:, k * fh:(k + 1) * fh] = _bf(silu * uk)
            da_ref[:, k * fh:(k + 1) * fh] = dgk
            da_ref[:, (2 + k) * fh:(3 + k) * fh] = duk
            dx += _dot_nt(dgk, w4_ref[k]) + _dot_nt(duk, w4_ref[2 + k])
        dx_ref[...] = dx

    return _rowcall(name, body, T, tm, [dh, z, a], [w_in4, w_out, ln_g],
                    [(D, F32), (4 * fh, BF16), (2 * fh, BF16), (D, BF16)],
                    [((1, D), F32), ((1, D), F32)], exchange=exchange)


def _mm_tn(name, a, b, out_dtype=BF16, n_split=1):
    T, M = a.shape
    N = b.shape[1]
    tk = _tile(T, 512, 8)
    tm = _tile(M, 1408)
    tn = _tile(N // n_split, 1536)
    per = N // n_split // tn
    nk = T // tk
    if n_split > 1:
        out_spec = pl.BlockSpec((None, tm, tn), lambda i, j, k: (j // per, i, j % per))
        out_shape = jax.ShapeDtypeStruct((n_split, M, N // n_split), out_dtype)
    else:
        out_spec = pl.BlockSpec((tm, tn), lambda i, j, k: (i, j))
        out_shape = jax.ShapeDtypeStruct((M, N), out_dtype)

    def kern(a_ref, b_ref, o_ref, acc_ref):
        k = pl.program_id(2)
        part = _dot_tn(_bf(a_ref[...]), _bf(b_ref[...]))

        @pl.when(k == 0)
        def _():
            acc_ref[...] = part

        @pl.when(k != 0)
        def _():
            acc_ref[...] += part

        @pl.when(k == nk - 1)
        def _():
            o_ref[...] = acc_ref[...].astype(out_dtype)

    return pl.pallas_call(
        kern, grid=(M // tm, N // tn, nk),
        in_specs=[pl.BlockSpec((tk, tm), lambda i, j, k: (k, i)), pl.BlockSpec((tk, tn), lambda i, j, k: (k, j))],
        out_specs=out_spec, out_shape=out_shape,
        scratch_shapes=[pltpu.VMEM((tm, tn), F32)], name=name,
        compiler_params=_params(("arbitrary", "arbitrary", "arbitrary")))(a, b)


def _proj_ret(h1, w_r, cos_r, sin_r, tm):
    T, D = h1.shape
    qk = RET_HEADS * RET_DK
    rv = RET_HEADS * RET_DV

    def body(i, h_ref, cos_ref, sin_ref, w_ref, q_ref, k_ref, v_ref, g_ref):
        hb = _bf(h_ref[...])
        cos, sin = cos_ref[...], sin_ref[...]
        for out_ref, off, scale in ((q_ref, 0, 1.0), (k_ref, qk, RET_DK ** -0.5)):
            pr = _dot(hb, w_ref[:, off:off + qk])
            for h in range(RET_HEADS):
                t = pr[:, h * RET_DK:(h + 1) * RET_DK]
                out_ref[:, h * RET_DK:(h + 1) * RET_DK] = _bf((t * cos + _roll(t, RET_DK // 2) * sin) * scale)
        v_ref[...] = _bf(_dot(hb, w_ref[:, 2 * qk:2 * qk + rv]))
        g_ref[...] = _dot(hb, w_ref[:, 2 * qk + rv:2 * qk + 2 * rv])

    return _rowcall("proj_ret", body, T, tm, [h1, cos_r, sin_r], [w_r],
                    [(qk, BF16), (qk, BF16), (rv, BF16), (rv, F32)])


def _rope_pe(t, c, s1, s2):
    return t * c + _roll(t, LANES - MLA_ROPE // 2) * s1 + _roll(t, MLA_ROPE // 2) * s2


def _rope_pe_bwd(dy, c, s1, s2):
    return dy * c + _roll(dy * s1, MLA_ROPE // 2) + _roll(dy * s2, LANES - MLA_ROPE // 2)


def _rms(x, g):
    r = lax.rsqrt(_mean(x * x) + EPS)
    return x * r, r


def _attn_block(T):
    return min(512, T)


def _transposed_blocks(T, tm, w, dtype):
    tb = _attn_block(T)
    per = tb // tm
    return (jax.ShapeDtypeStruct((T // tb, MLA_HEADS, w, tb), dtype),
            pl.BlockSpec((None, MLA_HEADS, w, tm), lambda i: (i // per, 0, 0, i % per)))


def _proj_mla(h1, tabs, w_c, w_kpe, w_g, w_uq, w_uk, w_uv, qn_g, kvn_g, tm):
    T, D = h1.shape
    H = MLA_HEADS

    def body(i, h_ref, c_ref, s1_ref, s2_ref, wc_ref, wk_ref, wg_ref, wuq_ref, wuk_ref, wuv_ref, qg_ref, kg_ref,
             lat_ref, gt_ref, q_ref, k_ref, v_ref, ln_ref, qt_ref, kt_ref, vt_ref):
        hb = _bf(h_ref[...])
        c, s1, s2 = c_ref[...], s1_ref[...], s2_ref[...]
        lat = _dot(hb, wc_ref[...])
        lat_ref[...] = lat
        gt_ref[...] = _dot(hb, wg_ref[...])
        cqn, _ = _rms(lat[:, :Q_LORA], None)
        ckn, _ = _rms(lat[:, Q_LORA:], None)
        cqn = _bf(cqn * qg_ref[...])
        ckn = _bf(ckn * kg_ref[...])
        ln_ref[:, :Q_LORA] = cqn
        ln_ref[:, Q_LORA:] = ckn
        q = _dot(cqn, wuq_ref[...])
        kn = _dot(ckn, wuk_ref[...])
        vv = _dot(ckn, wuv_ref[...])
        v_ref[...] = _bf(vv)
        kpe = _rope_pe(_dot(hb, wk_ref[...]), c, s1, s2)
        for h in range(H):
            o = h * MLA_QK
            qh = jnp.concatenate([q[:, o:o + MLA_NOPE], _rope_pe(q[:, o + MLA_NOPE:o + MLA_QK], c, s1, s2)], axis=1)
            kh = jnp.concatenate([kn[:, h * MLA_NOPE:(h + 1) * MLA_NOPE], kpe], axis=1)
            q_ref[:, o:o + MLA_QK] = _bf(qh)
            k_ref[:, o:o + MLA_QK] = _bf(kh)
            qt_ref[h] = _bf(qh.T)
            kt_ref[h] = _bf(kh.T)
            vt_ref[h] = _bf(vv[:, h * MLA_DV:(h + 1) * MLA_DV].T)

    lat_w = Q_LORA + KV_LORA
    return _rowcall("proj_mla", body, T, tm, [h1, *tabs], [w_c, w_kpe, w_g, w_uq, w_uk, w_uv, qn_g, kvn_g],
                    [(lat_w, F32), (2 * D, F32), (H * MLA_QK, BF16), (H * MLA_QK, BF16), (H * MLA_DV, BF16),
                     (lat_w, BF16)],
                    tiled_outs=[_transposed_blocks(T, tm, MLA_QK, BF16), _transposed_blocks(T, tm, MLA_QK, BF16),
                                _transposed_blocks(T, tm, MLA_DV, BF16)])


def _ret_block(T):
    return min(256, T)


def _ret_decay(lg, bt):
    n = lax.broadcasted_iota(jnp.int32, (bt, bt), 0)
    m = lax.broadcasted_iota(jnp.int32, (bt, bt), 1)
    dmat = jnp.where(_chunk_of(m) <= _chunk_of(n), jnp.exp(lg * jnp.abs(n - m).astype(F32)), 0.0)
    pos = lax.broadcasted_iota(jnp.int32, (bt, 1), 0).astype(F32)
    xi = jnp.exp(lg * (pos + 1.0))
    zeta = jnp.exp(lg * (bt - 1.0 - pos))
    return dmat, xi, zeta, jnp.exp(lg * bt)


def _ret_specs(bt, rev, nb):
    def blk(w):
        if rev:
            return pl.BlockSpec((bt, w), lambda h, b: (nb - 1 - b, h))
        return pl.BlockSpec((bt, w), lambda h, b: (b, h))
    return pl.BlockSpec((None, 1, LANES), lambda h, b: (h, 0, 0)), blk


def _ret_fwd(rq, rk, rv, lgam):
    T = rq.shape[0]
    bt = _ret_block(T)
    nb = T // bt
    lg_spec, blk = _ret_specs(bt, False, nb)

    def kern(lg_ref, q_ref, k_ref, v_ref, y_ref, s_ref):
        @pl.when(pl.program_id(1) == 0)
        def _():
            s_ref[...] = jnp.zeros_like(s_ref)

        dmat, xi, zeta, gb = _ret_decay(lg_ref[:, :1], bt)
        q, k, v = q_ref[...], k_ref[...], v_ref[...]
        sc = _dot_nt(q, k) * dmat
        y_ref[...] = _dot(_bf(sc), v) + _dot(q, _bf(s_ref[...])) * xi
        s_ref[...] = s_ref[...] * gb + _dot_tn(_bf(k.astype(F32) * zeta), v)

    return pl.pallas_call(
        kern, grid=(RET_HEADS, nb), in_specs=[lg_spec, blk(RET_DK), blk(RET_DK), blk(RET_DV)],
        out_specs=blk(RET_DV), out_shape=jax.ShapeDtypeStruct((T, RET_HEADS * RET_DV), F32),
        scratch_shapes=[pltpu.VMEM((RET_DK, RET_DV), F32)], name="ret_fwd",
        compiler_params=_params(("arbitrary", "arbitrary")))(lgam, rq, rk, rv)


def _ret_bwd_q(rq, rk, rv, dy, lgam):
    T = rq.shape[0]
    bt = _ret_block(T)
    nb = T // bt
    lg_spec, blk = _ret_specs(bt, False, nb)

    def kern(lg_ref, k_ref, v_ref, dy_ref, dq_ref, s_ref):
        @pl.when(pl.program_id(1) == 0)
        def _():
            s_ref[...] = jnp.zeros_like(s_ref)

        dmat, xi, zeta, gb = _ret_decay(lg_ref[:, :1], bt)
        k, v, dy = k_ref[...], v_ref[...], dy_ref[...]
        dp = _dot_nt(dy, v) * dmat
        dq_ref[...] = _dot(_bf(dp), k) + _dot_nt(dy, _bf(s_ref[...])) * xi
        s_ref[...] = s_ref[...] * gb + _dot_tn(_bf(k.astype(F32) * zeta), v)

    return pl.pallas_call(
        kern, grid=(RET_HEADS, nb), in_specs=[lg_spec, blk(RET_DK), blk(RET_DV), blk(RET_DV)],
        out_specs=blk(RET_DK), out_shape=jax.ShapeDtypeStruct((T, RET_HEADS * RET_DK), F32),
        scratch_shapes=[pltpu.VMEM((RET_DK, RET_DV), F32)], name="ret_bwd_q",
        compiler_params=_params(("arbitrary", "arbitrary")))(lgam, rk, rv, dy)


def _ret_bwd_kv(rq, rk, rv, dy, lgam):
    T = rq.shape[0]
    bt = _ret_block(T)
    nb = T // bt
    lg_spec, blk = _ret_specs(bt, True, nb)

    def kern(lg_ref, q_ref, k_ref, v_ref, dy_ref, dk_ref, dv_ref, g_ref):
        @pl.when(pl.program_id(1) == 0)
        def _():
            g_ref[...] = jnp.zeros_like(g_ref)

        dmat, xi, zeta, gb = _ret_decay(lg_ref[:, :1], bt)
        q, k, v, dy = q_ref[...], k_ref[...], v_ref[...], dy_ref[...]
        gs = _bf(g_ref[...])
        p = _dot_nt(q, k) * dmat
        dp = _dot_nt(dy, v) * dmat
        dv_ref[...] = _bf(_dot_tn(_bf(p), dy) + _dot(k, gs) * zeta)
        dk_ref[...] = _dot_tn(_bf(dp), q) + _dot_nt(v, gs) * zeta
        g_ref[...] = g_ref[...] * gb + _dot_tn(_bf(q.astype(F32) * xi), dy)

    return pl.pallas_call(
        kern, grid=(RET_HEADS, nb), in_specs=[lg_spec, blk(RET_DK), blk(RET_DK), blk(RET_DV), blk(RET_DV)],
        out_specs=[blk(RET_DK), blk(RET_DV)],
        out_shape=[jax.ShapeDtypeStruct((T, RET_HEADS * RET_DK), F32),
                   jax.ShapeDtypeStruct((T, RET_HEADS * RET_DV), BF16)],
        scratch_shapes=[pltpu.VMEM((RET_DK, RET_DV), F32)], name="ret_bwd_kv",
        compiler_params=_params(("arbitrary", "arbitrary")))(lgam, rq, rk, rv, dy)


def _attn_mask(tb):
    r = lax.broadcasted_iota(jnp.int32, (tb, tb), 0)
    c = lax.broadcasted_iota(jnp.int32, (tb, tb), 1)
    return _chunk_of(c) <= _chunk_of(r)


def _attn_mask_t(tb):
    key = lax.broadcasted_iota(jnp.int32, (tb, tb), 0)
    qry = lax.broadcasted_iota(jnp.int32, (tb, tb), 1)
    return _chunk_of(key) <= _chunk_of(qry)


ATTN_SCALE = (MLA_NOPE + MLA_ROPE) ** -0.5
MASKED = -1e30
LOG2E = 1.4426950408889634
SUBLANES = 8


def _head_blocks(nb, w, tb):
    return pl.BlockSpec((nb, None, w, tb), lambda h, i: (0, h, 0, 0))


def _one_block(w, tb):
    return pl.BlockSpec((None, None, w, tb), lambda h, i: (i, h, 0, 0))


def _attn_fwd(k, qt, vt, exchange=()):
    T = k.shape[0]
    tb = _attn_block(T)
    nb = T // tb

    n_ex = len(exchange)

    def kern(qt_ref, k_ref, vt_ref, *refs):
        ex_in, (o_ref, lser_ref), ex_out = refs[:n_ex], refs[n_ex:n_ex + 2], refs[n_ex + 2:2 * n_ex + 2]
        m_ref, l_ref, acc_ref = refs[2 * n_ex + 2:2 * n_ex + 5]
        sems = refs[2 * n_ex + 5:]
        qb = pl.program_id(1)
        first = jnp.logical_and(pl.program_id(0) == 0, qb == 0)
        last = jnp.logical_and(pl.program_id(0) == MLA_HEADS - 1, qb == nb - 1)
        if n_ex:
            @pl.when(first)
            def _():
                for cp in _chip_copies(ex_in, ex_out, *sems, True):
                    cp.start()

        qt = qt_ref[...]
        m_ref[...] = jnp.full_like(m_ref, MASKED)
        l_ref[...] = jnp.zeros_like(l_ref)
        acc_ref[...] = jnp.zeros_like(acc_ref)

        def step(kb, diagonal):
            rows = pl.ds(pl.multiple_of(kb * tb, tb), tb)
            s = _dot(k_ref[rows, :], qt) * (ATTN_SCALE * LOG2E)
            if diagonal:
                s = jnp.where(_attn_mask_t(tb), s, MASKED)
            m_old = m_ref[...]
            m_new = jnp.maximum(m_old, jnp.max(s, axis=0, keepdims=True))
            p = jnp.exp2(s - m_new)
            corr = jnp.exp2(m_old - m_new)
            l_ref[...] = l_ref[...] * corr + jnp.sum(p, axis=0, keepdims=True)
            acc_ref[...] = acc_ref[...] * corr + _dot(vt_ref[kb], _bf(p))
            m_ref[...] = m_new

        def pair_body(i, carry):
            step(2 * i, False)
            step(2 * i + 1, False)
            return carry

        lax.fori_loop(0, qb // 2, pair_body, 0)

        @pl.when(qb % 2 == 1)
        def _():
            step(qb - 1, False)

        step(qb, True)
        o_ref[...] = (acc_ref[...] / l_ref[...]).T
        lser_ref[...] = jnp.broadcast_to(m_ref[...] + jnp.log2(l_ref[...]), (SUBLANES, tb))
        if n_ex:
            @pl.when(last)
            def _():
                _wait_copies(_chip_copies(ex_in, ex_out, *sems, True))

    return pl.pallas_call(
        kern, grid=(MLA_HEADS, nb),
        in_specs=[_one_block(MLA_QK, tb), pl.BlockSpec((T, MLA_QK), lambda h, i: (0, h)),
                  _head_blocks(nb, MLA_DV, tb)] + [HBM_SPEC] * n_ex,
        out_specs=[pl.BlockSpec((tb, MLA_DV), lambda h, i: (i, h)), _one_block(SUBLANES, tb)] + [HBM_SPEC] * n_ex,
        out_shape=[jax.ShapeDtypeStruct((T, MLA_HEADS * MLA_DV), F32),
                   jax.ShapeDtypeStruct((nb, MLA_HEADS, SUBLANES, tb), F32)] + _exchange_shapes(exchange),
        scratch_shapes=[pltpu.VMEM((1, tb), F32), pltpu.VMEM((1, tb), F32), pltpu.VMEM((MLA_DV, tb), F32)]
        + (_dma_sems(n_ex * N_PEER_CHIPS) if n_ex else []),
        name="attn_fwd", compiler_params=_params(("arbitrary", "arbitrary")))(qt, k, vt, *exchange)


def _attn_bwd(q, k, v, do, qt, kt, dot_, lse_rows, delta_rows):
    T = q.shape[0]
    tb = _attn_block(T)
    nb = T // tb

    def kern(q_ref, k_ref, v_ref, do_ref, qt_ref, kt_ref, dot_ref, lse_ref, dl_ref, dk_ref, dv_ref, dqt_ref, dv_acc):
        kb = pl.program_id(1)
        kv, vv, ktv = k_ref[...], v_ref[...], kt_ref[...]
        dk_ref[...] = jnp.zeros_like(dk_ref)
        dv_acc[...] = jnp.zeros_like(dv_acc)

        @pl.when(kb == 0)
        def _():
            dqt_ref[...] = jnp.zeros_like(dqt_ref)

        def step(qb, diagonal):
            rows = pl.ds(pl.multiple_of(qb * tb, tb), tb)
            s = _dot(kv, qt_ref[qb]) * (ATTN_SCALE * LOG2E)
            if diagonal:
                s = jnp.where(_attn_mask_t(tb), s, MASKED)
            p = jnp.exp2(s - lse_ref[qb][:1, :])
            dv_acc[...] += _dot(_bf(p), do_ref[rows, :])
            ds = _bf(p * (_dot(vv, dot_ref[qb]) - dl_ref[qb][:1, :]) * ATTN_SCALE)
            dk_ref[...] += _dot(ds, q_ref[rows, :])
            dqt_ref[qb] += _dot(ktv, ds)

        def loop_body(qb, carry):
            step(qb, False)
            return carry

        step(kb, True)
        lax.fori_loop(kb + 1, nb, loop_body, 0)
        dv_ref[...] = _bf(dv_acc[...])

    def blk(w):
        return pl.BlockSpec((tb, w), lambda h, i: (i, h))

    def full(w):
        return pl.BlockSpec((T, w), lambda h, i: (0, h))

    return pl.pallas_call(
        kern, grid=(MLA_HEADS, nb),
        in_specs=[full(MLA_QK), blk(MLA_QK), blk(MLA_DV), full(MLA_DV), _head_blocks(nb, MLA_QK, tb),
                  _one_block(MLA_QK, tb), _head_blocks(nb, MLA_DV, tb), _head_blocks(nb, SUBLANES, tb),
                  _head_blocks(nb, SUBLANES, tb)],
        out_specs=[blk(MLA_QK), blk(MLA_DV), _head_blocks(nb, MLA_QK, tb)],
        out_shape=[jax.ShapeDtypeStruct((T, MLA_HEADS * MLA_QK), F32),
                   jax.ShapeDtypeStruct((T, MLA_HEADS * MLA_DV), BF16),
                   jax.ShapeDtypeStruct((nb, MLA_HEADS, MLA_QK, tb), F32)],
        scratch_shapes=[pltpu.VMEM((tb, MLA_DV), F32)],
        name="attn_bwd", compiler_params=_params(("arbitrary", "arbitrary")))(
            q, k, v, do, qt, kt, dot_, lse_rows, delta_rows)


def _group_norm(y):
    yc = y - _mean(y)
    rstd = lax.rsqrt(_mean(yc * yc) + EPS)
    return yc * rstd, rstd


def _mix_fwd(y, rg, o, gates, h1, gn_g, w_ret_o, w_mla_o, w_out, ln_g, ln_b, tm):
    T, D = h1.shape

    def body(i, y_ref, rg_ref, o_ref, gt_ref, h_ref, gn_ref, wr_ref, wm_ref, wo_ref, g_ref, b_ref,
             h2_ref, z_ref, yret_ref, ymla_ref, yr_ref, mix_ref):
        for h in range(RET_HEADS):
            sl = slice(h * RET_DV, (h + 1) * RET_DV)
            yn, _ = _group_norm(y_ref[:, sl])
            r = rg_ref[:, sl]
            yr_ref[:, sl] = _bf(r * _sigmoid(r) * (yn * gn_ref[:, sl]))
        yret = _dot(yr_ref[...], wr_ref[...])
        ymla = _dot(_bf(o_ref[...]), wm_ref[...])
        yret_ref[...] = yret
        ymla_ref[...] = ymla
        mix = _bf(_sigmoid(gt_ref[:, :D]) * yret + _sigmoid(gt_ref[:, D:]) * ymla)
        mix_ref[...] = mix
        z = ALPHA * h_ref[...] + _dot(mix, wo_ref[...])
        xhat, _ = _ln_stats(z)
        z_ref[...] = z
        h2_ref[...] = xhat * g_ref[...] + b_ref[...]

    return _rowcall("mix_fwd", body, T, tm, [y, rg, o, gates, h1], [gn_g, w_ret_o, w_mla_o, w_out, ln_g, ln_b],
                    [(D, F32), (D, F32), (D, F32), (D, F32), (RET_HEADS * RET_DV, BF16), (D, BF16)])


def _mix_bwd(dh2, z1, gates, yret, ymla, y, rg, o, gn_g, w_ret_o, w_mla_o, w_out, ln_g, tm, exchange=None):
    T, D = dh2.shape
    rv = RET_HEADS * RET_DV

    def body(i, dh_ref, z_ref, gt_ref, yret_ref, ymla_ref, y_ref, rg_ref, o_ref, gn_ref, wr_ref, wm_ref, wo_ref, g_ref,
             dz_ref, dgt_ref, drg_ref, dy_ref, do_ref, dyret_ref, dymla_ref, dg_ref, db_ref, dgn_ref, dot_ref,
             dl_ref):
        xhat, rstd = _ln_stats(z_ref[...])
        dz, dg, db = _ln_bwd(dh_ref[...], xhat, rstd, g_ref[...])
        _acc(i, dg_ref, dg)
        _acc(i, db_ref, db)
        dz_ref[...] = dz
        dmix = _dot_nt(_bf(dz), wo_ref[...])
        sr = _sigmoid(gt_ref[:, :D])
        sm = _sigmoid(gt_ref[:, D:])
        dgt_ref[:, :D] = _bf(dmix * yret_ref[...] * sr * (1.0 - sr))
        dgt_ref[:, D:] = _bf(dmix * ymla_ref[...] * sm * (1.0 - sm))
        dyret = _bf(dmix * sr)
        dymla = _bf(dmix * sm)
        dyret_ref[...] = dyret
        dymla_ref[...] = dymla
        dov = _dot_nt(dymla, wm_ref[...])
        do_ref[...] = _bf(dov)
        for h in range(MLA_HEADS):
            sl = slice(h * MLA_DV, (h + 1) * MLA_DV)
            dot_ref[h] = _bf(dov[:, sl].T)
            delta = jnp.sum(dov[:, sl] * o_ref[:, sl], axis=-1, keepdims=True)
            dl_ref[h] = jnp.broadcast_to(delta, (tm, LANES)).T[:SUBLANES, :]
        dyr = _dot_nt(dyret, wr_ref[...])
        dgn = []
        for h in range(RET_HEADS):
            sl = slice(h * RET_DV, (h + 1) * RET_DV)
            yn, grstd = _group_norm(y_ref[:, sl])
            r = rg_ref[:, sl]
            sig = _sigmoid(r)
            d = dyr[:, sl]
            drg_ref[:, sl] = _bf(d * (yn * gn_ref[:, sl]) * sig * (1.0 + r * (1.0 - sig)))
            dt = d * (r * sig)
            dgn.append(jnp.sum(dt * yn, axis=0, keepdims=True))
            dyn = dt * gn_ref[:, sl]
            dy_ref[:, sl] = _bf(grstd * (dyn - _mean(dyn) - yn * _mean(dyn * yn)))
        _acc(i, dgn_ref, jnp.concatenate(dgn, axis=1))

    return _rowcall("mix_bwd", body, T, tm, [dh2, z1, gates, yret, ymla, y, rg, o],
                    [gn_g, w_ret_o, w_mla_o, w_out, ln_g],
                    [(D, F32), (2 * D, BF16), (rv, BF16), (rv, BF16), (MLA_HEADS * MLA_DV, BF16), (D, BF16), (D, BF16)],
                    [((1, D), F32), ((1, D), F32), ((1, rv), F32)],
                    tiled_outs=[_transposed_blocks(T, tm, MLA_DV, BF16), _transposed_blocks(T, tm, SUBLANES, F32)],
                    exchange=exchange)


def _proj_mla_bwd(dqt, dk, dv, lat, tabs, w_uq, w_uk, w_uv, qn_g, kvn_g, tm):
    T = dk.shape[0]
    H = MLA_HEADS
    lat_w = Q_LORA + KV_LORA

    def body(i, dk_ref, dv_ref, lat_ref, c_ref, s1_ref, s2_ref, dqt_ref, wuq_ref, wuk_ref, wuv_ref, qg_ref, kg_ref,
             dlat_ref, dkpe_ref, dqb_ref, dkn_ref, dqg_ref, dkg_ref):
        c, s1, s2 = c_ref[...], s1_ref[...], s2_ref[...]
        dkpe = jnp.zeros((tm, LANES), F32)
        for h in range(H):
            o = h * MLA_QK
            dqh = dqt_ref[h].T
            dqb_ref[:, o:o + MLA_NOPE] = _bf(dqh[:, :MLA_NOPE])
            dqb_ref[:, o + MLA_NOPE:o + MLA_QK] = _bf(_rope_pe_bwd(dqh[:, MLA_NOPE:], c, s1, s2))
            dkn_ref[:, h * MLA_NOPE:(h + 1) * MLA_NOPE] = _bf(dk_ref[:, o:o + MLA_NOPE])
            dkpe += dk_ref[:, o + MLA_NOPE:o + MLA_QK]
        dkpe_ref[...] = _bf(_rope_pe_bwd(dkpe, c, s1, s2))
        dcqn = _dot_nt(dqb_ref[...], wuq_ref[...])
        dckn = _dot_nt(dkn_ref[...], wuk_ref[...]) + _dot_nt(dv_ref[...], wuv_ref[...])
        for dn, x, g_ref, dg_ref, sl in ((dcqn, lat_ref[:, :Q_LORA], qg_ref, dqg_ref, slice(0, Q_LORA)),
                                         (dckn, lat_ref[:, Q_LORA:], kg_ref, dkg_ref, slice(Q_LORA, lat_w))):
            xn, r = _rms(x, None)
            _acc(i, dg_ref, jnp.sum(dn * xn, axis=0, keepdims=True))
            dxn = dn * g_ref[...]
            dlat_ref[:, sl] = _bf(r * (dxn - xn * _mean(dxn * xn)))

    dqt_shape, dqt_spec = _transposed_blocks(T, tm, MLA_QK, F32)
    assert dqt.shape == dqt_shape.shape
    return _rowcall("proj_mla_bwd", body, T, tm, [dk, dv, lat, *tabs], [w_uq, w_uk, w_uv, qn_g, kvn_g],
                    [(lat_w, BF16), (LANES, BF16), (H * MLA_QK, BF16), (H * MLA_NOPE, BF16)],
                    [((1, Q_LORA), F32), ((1, KV_LORA), F32)], tiled_ins=[(dqt, dqt_spec)])


def _proj_bwd(drq, drk, drv, drg, dz1, dlat, dkpe, dgates, cos_r, sin_r, w_r, w_c, w_kpe, w_g, tm):
    T, D = dz1.shape
    qk = RET_HEADS * RET_DK
    rv = RET_HEADS * RET_DV

    def body(i, drq_ref, drk_ref, drv_ref, drg_ref, dz_ref, dlat_ref, dkpe_ref, dgt_ref, cos_ref, sin_ref,
             wr_ref, wc_ref, wk_ref, wg_ref, dh_ref, dpr_ref):
        cos, sin = cos_ref[...], sin_ref[...]
        for src, off, scale in ((drq_ref, 0, 1.0), (drk_ref, qk, RET_DK ** -0.5)):
            for h in range(RET_HEADS):
                d = src[:, h * RET_DK:(h + 1) * RET_DK]
                dpr_ref[:, off + h * RET_DK:off + (h + 1) * RET_DK] = _bf(
                    (d * cos + _roll(d * sin, RET_DK // 2)) * scale)
        dpr_ref[:, 2 * qk:2 * qk + rv] = drv_ref[...]
        dpr_ref[:, 2 * qk + rv:] = drg_ref[...]
        dh_ref[...] = (ALPHA * dz_ref[...] + _dot_nt(dpr_ref[...], wr_ref[...]) + _dot_nt(dlat_ref[...], wc_ref[...])
                       + _dot_nt(dkpe_ref[...], wk_ref[...]) + _dot_nt(dgt_ref[...], wg_ref[...]))

    return _rowcall("proj_bwd", body, T, tm, [drq, drk, drv, drg, dz1, dlat, dkpe, dgates, cos_r, sin_r],
                    [w_r, w_c, w_kpe, w_g], [(D, F32), (2 * qk + 2 * rv, BF16)])


def _ple_loss(h3, p, target, w_gate, w_proj, ln_g, ln_b, tm):
    T, D = h3.shape

    def body(i, h_ref, p_ref, t_ref, wg_ref, wp_ref, g_ref, b_ref, dh_ref, dgp_ref, dpp_ref, loss_ref, dg_ref, db_ref):
        hv = h_ref[...]
        sg = _sigmoid(_dot(_bf(hv), wg_ref[...]))
        pp = _dot(_bf(p_ref[...]), wp_ref[...])
        xhat, rstd = _ln_stats(ALPHA * hv + sg * pp)
        err = xhat * g_ref[...] + b_ref[...] - t_ref[...]
        row_loss = 0.5 * _mean(err * err)
        _acc(i, loss_ref, jnp.broadcast_to(jnp.sum(row_loss, axis=0, keepdims=True), (1, LANES)))
        dz, dg, db = _ln_bwd(err * (1.0 / D), xhat, rstd, g_ref[...])
        _acc(i, dg_ref, dg)
        _acc(i, db_ref, db)
        dgp = _bf(dz * pp * sg * (1.0 - sg))
        dgp_ref[...] = dgp
        dpp_ref[...] = _bf(dz * sg)
        dh_ref[...] = ALPHA * dz + _dot_nt(dgp, wg_ref[...])

    return _rowcall("ple_loss", body, T, tm, [h3, p, target], [w_gate, w_proj, ln_g, ln_b],
                    [(D, F32), (D, BF16), (D, BF16)], [((1, LANES), F32), ((1, D), F32), ((1, D), F32)])


def _ewise(name, fn, ins, n_out, out_dtype=F32):
    r, c = ins[0].shape
    tr = _tile(r, max(8, (1 << 19) // c // 8 * 8), 8)

    def kern(*refs):
        outs = fn(*[x[...] for x in refs[:len(ins)]])
        for o_ref, o in zip(refs[len(ins):], outs):
            o_ref[...] = o.astype(out_dtype)

    spec = pl.BlockSpec((tr, c), lambda i: (i, 0))
    return pl.pallas_call(kern, grid=(r // tr,), in_specs=[spec] * len(ins), out_specs=[spec] * n_out,
                          out_shape=[jax.ShapeDtypeStruct((r, c), out_dtype)] * n_out, name=name,
                          compiler_params=_params(("arbitrary",)))(*ins)


def _adamw_math(w, g, m, v):
    m = ADAM_B1 * m + (1.0 - ADAM_B1) * g
    v = ADAM_B2 * v + (1.0 - ADAM_B2) * (g * g)
    m_hat = m / (1.0 - ADAM_B1 ** ADAM_STEP)
    v_hat = v / (1.0 - ADAM_B2 ** ADAM_STEP)
    return -ADAM_LR * (m_hat / (jnp.sqrt(v_hat) + ADAM_EPS) + ADAM_WD * w), m, v


def _adamw(name, w, g, m, v):
    shape = w.shape
    c = shape[-1]
    flat = [t.reshape(-1, c) for t in (w, g, m, v)]
    return [t.reshape(shape) for t in _ewise(name, _adamw_math, flat, 3)]


def _place():
    return lax.axis_index("x"), lax.axis_index("y"), lax.axis_index("c")


def _dma_sems(n):
    return [pltpu.SemaphoreType.DMA((n,)), pltpu.SemaphoreType.DMA((n,))]


N_PEER_CHIPS = N_CHIPS - 1


def _chips_exchange(name, srcs, broadcast):
    n = len(srcs)

    def kern(*refs):
        cps = _chip_copies(refs[:n], refs[n:2 * n], refs[2 * n], refs[2 * n + 1], broadcast)
        for cp in cps:
            cp.start()
        _wait_copies(cps)

    return pl.pallas_call(
        kern, out_shape=_exchange_shapes(srcs), in_specs=[HBM_SPEC] * n, out_specs=[HBM_SPEC] * n,
        scratch_shapes=_dma_sems(n * N_PEER_CHIPS), name=name)(*srcs)


def _exchange_shapes(srcs):
    return [jax.ShapeDtypeStruct((N_PEER_CHIPS,) + s.shape[1:], s.dtype) for s in srcs]


def _chip_copies(src_refs, out_refs, send_sems, recv_sems, broadcast):
    x, y, c = _place()
    peers = [(1 - x, y), (x, 1 - y), (1 - x, 1 - y)]
    cps = []
    for j, (px, py) in enumerate(peers):
        for a, (src_ref, out_ref) in enumerate(zip(src_refs, out_refs)):
            piece = src_ref.at[c] if broadcast else src_ref.at[2 * px + py]
            cps.append(pltpu.make_async_remote_copy(
                src_ref=piece, dst_ref=out_ref.at[j], send_sem=send_sems.at[a * N_PEER_CHIPS + j],
                recv_sem=recv_sems.at[a * N_PEER_CHIPS + j], device_id=(px, py, c), device_id_type=MESH))
    return cps


def _wait_copies(cps):
    for cp in cps:
        cp.wait_recv()
    for cp in cps:
        cp.wait_send()


def _sibling_swap(name, srcs, halves):
    n = len(srcs)

    def kern(*refs):
        src_refs, out_refs = refs[:n], refs[n:2 * n]
        send_sems, recv_sems = refs[2 * n:]
        x, y, c = _place()

        def copy(a):
            piece = src_refs[a].at[:, 1 - c] if halves else src_refs[a]
            return pltpu.make_async_remote_copy(
                src_ref=piece, dst_ref=out_refs[a], send_sem=send_sems.at[a], recv_sem=recv_sems.at[a],
                device_id=(x, y, 1 - c), device_id_type=MESH)

        cps = [copy(a) for a in range(n)]
        for cp in cps:
            cp.start()
        for cp in cps:
            cp.wait_recv()
        for cp in cps:
            cp.wait_send()

    def out_shape(s):
        return jax.ShapeDtypeStruct((s.shape[0],) + s.shape[2:] if halves else s.shape, s.dtype)

    return pl.pallas_call(
        kern, out_shape=[out_shape(s) for s in srcs], in_specs=[HBM_SPEC] * n, out_specs=[HBM_SPEC] * n,
        scratch_shapes=_dma_sems(n), name=name)(*srcs)


def _all_devices(name, src, reduce):
    r, c = src.shape
    n_dev = 2 * N_CHIPS

    def kern(src_ref, out_ref, *scratch):
        if reduce:
            gat_ref, send_sems, recv_sems = scratch
        else:
            gat_ref = out_ref
            send_sems, recv_sems = scratch
        x, y, cc = _place()
        me = 4 * x + 2 * y + cc
        gat_ref[me] = src_ref[...]
        peers = []
        for j in range(1, n_dev):
            px = 1 - x if j & 4 else x
            py = 1 - y if j & 2 else y
            pc = 1 - cc if j & 1 else cc
            peers.append((px, py, pc))

        def copy(j, peer, slot):
            return pltpu.make_async_remote_copy(
                src_ref=src_ref, dst_ref=gat_ref.at[slot], send_sem=send_sems.at[j], recv_sem=recv_sems.at[j],
                device_id=peer, device_id_type=MESH)

        sends = [copy(j, peer, me) for j, peer in enumerate(peers)]
        for cp in sends:
            cp.start()
        for j, (px, py, pc) in enumerate(peers):
            copy(j, (px, py, pc), 4 * px + 2 * py + pc).wait_recv()
        for cp in sends:
            cp.wait_send()
        if reduce:
            total = gat_ref[0]
            for d in range(1, n_dev):
                total = total + gat_ref[d]
            out_ref[...] = total

    out_shape = jax.ShapeDtypeStruct((r, c) if reduce else (n_dev, r, c), src.dtype)
    scratch = ([pltpu.VMEM((n_dev, r, c), src.dtype)] if reduce else []) + _dma_sems(n_dev - 1)
    return pl.pallas_call(kern, out_shape=out_shape, in_specs=[VMEM_SPEC], out_specs=VMEM_SPEC,
                          scratch_shapes=scratch, name=name)(src)


def _halves(t, axis):
    return t.reshape(t.shape[:axis] + (2, t.shape[axis] // 2) + t.shape[axis + 1:])


def _by_core(mine, theirs, axis):
    c = lax.axis_index("c")
    both = jnp.where(c == 0, jnp.stack([mine, theirs], axis), jnp.stack([theirs, mine], axis))
    return both.reshape(both.shape[:axis] + (2 * both.shape[axis + 1],) + both.shape[axis + 2:])


def _chip_order(own, others):
    me = 2 * lax.axis_index("x") + lax.axis_index("y")
    cands = jnp.concatenate([own[None], others], axis=0)
    slot_of_flip = (0, 2, 1, 3)
    pick = jnp.asarray(slot_of_flip, jnp.int32)[jnp.arange(N_CHIPS, dtype=jnp.int32) ^ me]
    return jnp.stack([lax.dynamic_index_in_dim(cands, pick[k], 0, keepdims=False) for k in range(N_CHIPS)])


def _join_shards(name, shards):
    _, r, c = shards.shape
    if name in COL_SHARDED:
        return shards.transpose(1, 0, 2).reshape(r, N_CHIPS * c)
    return shards.reshape(N_CHIPS * r, c)


def _split_shards(name, full):
    if full.ndim == 3:
        return full
    r, c = full.shape
    if name in COL_SHARDED:
        return jnp.stack([full[:, k * (c // N_CHIPS):(k + 1) * (c // N_CHIPS)] for k in range(N_CHIPS)])
    return full.reshape(N_CHIPS, r // N_CHIPS, c)


def _rope_tables(positions):
    pos = positions.reshape(-1).astype(F32)[:, None]
    half = RET_DK // 2
    ang = pos * (ROPE_BASE ** (-jnp.arange(half, dtype=F32) / half))
    cos_r = jnp.concatenate([jnp.cos(ang)] * 2, axis=1)
    sin_r = jnp.concatenate([-jnp.sin(ang), jnp.sin(ang)], axis=1)
    half = MLA_ROPE // 2
    ang = pos * (ROPE_BASE ** (-jnp.arange(half, dtype=F32) / half))
    zeros = jnp.zeros_like(ang)
    rest = LANES - MLA_ROPE
    c = jnp.concatenate([jnp.cos(ang)] * 2 + [jnp.ones((ang.shape[0], rest), F32)], axis=1)
    s1 = jnp.concatenate([-jnp.sin(ang), zeros, jnp.zeros((ang.shape[0], rest), F32)], axis=1)
    s2 = jnp.concatenate([zeros, jnp.sin(ang), jnp.zeros((ang.shape[0], rest), F32)], axis=1)
    return cos_r, sin_r, (c, s1, s2)


GATHER_GROUPS = (("ffn1_w_in", "ffn1_w_out"), ("w_in", "w_uq", "w_ukv"),
                 ("w_ret_o", "w_mla_o", "w_out", "ffn2_w_in", "ffn2_w_out", "ple_w_gate", "ple_w_proj"))
REDUCE_GROUPS = (("ple_w_gate", "ple_w_proj", "ffn2_w_in", "ffn2_w_out"),
                 ("w_out", "w_ret_o", "w_mla_o", "w_uq", "w_ukv", "w_in"), ("ffn1_w_in", "ffn1_w_out"))


def _gathered(tag, names, own, mine):
    theirs = _sibling_swap("gather_cores_" + tag, mine, False)
    out = {}
    for n, m, t in zip(names, mine, theirs):
        full = _chip_order(own[n], _by_core(m, t, 1))
        out[n] = full if n in ("ffn1_w_in", "ffn2_w_in") else _join_shards(n, full)
    return out


def _chip_sums(tag, names, grads):
    c = lax.axis_index("c")
    halves = [_halves(_split_shards(n, grads[n]), 1) for n in names]
    theirs = _sibling_swap("reduce_cores_" + tag, halves, True)
    sums = []
    for n, g, t in zip(names, halves, theirs):
        mine = lax.dynamic_index_in_dim(g, c, axis=1, keepdims=False)
        k, r, cc = mine.shape
        sums.append(_ewise("reduce_cores_add_" + n, lambda a, b: (a.astype(F32) + b.astype(F32),),
                           [mine.reshape(k * r, cc), t.reshape(k * r, cc)], 1, BF16)[0].reshape(k, r, cc))
    return sums


def _block_totals(names, sums, parts):
    me = 2 * lax.axis_index("x") + lax.axis_index("y")
    totals = []
    for n, s, pt in zip(names, sums, parts):
        own = lax.dynamic_index_in_dim(s, me, axis=0, keepdims=False)
        totals.append(_ewise("reduce_chips_add_" + n,
                             lambda a, b, c_, d: (((a.astype(F32) + b.astype(F32)) + c_.astype(F32)) + d.astype(F32),),
                             [own, pt[0], pt[1], pt[2]], 1, F32)[0])
    return totals


def _local_step(x, p, positions, target, shards, ln_g, ln_b, gn_g, qn_g, kvn_g):
    T, D = x.shape
    tm = min(256, T)
    H = MLA_HEADS
    qk, rv = RET_HEADS * RET_DK, RET_HEADS * RET_DV
    cos_r, sin_r, tabs = _rope_tables(positions)
    lgam = jnp.broadcast_to(jnp.log(1.0 - 2.0 ** (-5.0 - jnp.arange(RET_HEADS, dtype=F32)))[:, None, None],
                            (RET_HEADS, 1, LANES))
    lng = [ln_g[k:k + 1] for k in range(N_LN)]
    lnb = [ln_b[k:k + 1] for k in range(N_LN)]
    own = {n: _bf(shards[n]) for n in BIG_WEIGHTS}
    to_send = [[_halves(own[n], 0) for n in names] for names in GATHER_GROUPS]

    w = _gathered("a", GATHER_GROUPS[0], own, _chips_exchange("gather_chips_a", to_send[0], True))
    h1, z0, a1, *arrived = _ffn_fwd("ffn1_fwd", x, w["ffn1_w_in"], w["ffn1_w_out"], lng[0], lnb[0], tm,
                                    exchange=(to_send[1], True))
    w.update(_gathered("b", GATHER_GROUPS[1], own, arrived))

    w_in = w["w_in"]
    o_lat, o_kpe, o_gate = 2 * qk + 2 * rv, 2 * qk + 2 * rv + Q_LORA + KV_LORA, 2 * qk + 2 * rv + Q_LORA + KV_LORA + MLA_ROPE
    w_r, w_c = w_in[:, :o_lat], w_in[:, o_lat:o_kpe]
    w_kpe = jnp.pad(w_in[:, o_kpe:o_gate], ((0, 0), (0, LANES - MLA_ROPE)))
    w_g = w_in[:, o_gate:]
    w_uq = jnp.pad(w["w_uq"].reshape(Q_LORA, H, MLA_NOPE + MLA_ROPE),
                   ((0, 0), (0, 0), (0, MLA_QK - MLA_NOPE - MLA_ROPE))).reshape(Q_LORA, H * MLA_QK)
    w_ukv = w["w_ukv"].reshape(KV_LORA, H, MLA_NOPE + MLA_DV)
    w_uk = w_ukv[:, :, :MLA_NOPE].reshape(KV_LORA, H * MLA_NOPE)
    w_uv = w_ukv[:, :, MLA_NOPE:].reshape(KV_LORA, H * MLA_DV)

    rq, rk, rvv, rg = _proj_ret(h1, w_r, cos_r, sin_r, tm)
    lat, gates, q, k, v, latn, qt, kt, vt = _proj_mla(h1, tabs, w_c, w_kpe, w_g, w_uq, w_uk, w_uv, qn_g, kvn_g, tm)
    y = _ret_fwd(rq, rk, rvv, lgam)
    o, lse_rows, *arrived = _attn_fwd(k, qt, vt, exchange=to_send[2])
    w.update(_gathered("c", GATHER_GROUPS[2], own, arrived))
    h2, z1, yret, ymla, yr, mix = _mix_fwd(y, rg, o, gates, h1, gn_g, w["w_ret_o"], w["w_mla_o"], w["w_out"],
                                           lng[1], lnb[1], tm)
    h3, z2, a2 = _ffn_fwd("ffn2_fwd", h2, w["ffn2_w_in"], w["ffn2_w_out"], lng[2], lnb[2], tm)

    dh3, dgp, dpp, loss, dg3, db3 = _ple_loss(h3, p, target, w["ple_w_gate"], w["ple_w_proj"], lng[3], lnb[3], tm)
    dh2, da2, s2, df2, dg2, db2 = _ffn_bwd("ffn2_bwd", dh3, z2, a2, w["ffn2_w_in"], w["ffn2_w_out"], lng[2], tm)
    grads = {"ple_w_gate": _mm_tn("wg_ple_gate", h3, dgp), "ple_w_proj": _mm_tn("wg_ple_proj", p, dpp),
             "ffn2_w_in": _mm_tn("wg_ffn2_in", h2, da2, n_split=N_CHIPS), "ffn2_w_out": _mm_tn("wg_ffn2_out", s2, df2)}
    sums1 = _chip_sums("1", REDUCE_GROUPS[0], grads)
    (dz1, dgates, drg, dy, do, dyret, dymla, dg1, db1, dgn, dot_, delta_rows, *parts1) = _mix_bwd(
        dh2, z1, gates, yret, ymla, y, rg, o, gn_g, w["w_ret_o"], w["w_mla_o"], w["w_out"], lng[1], tm,
        exchange=(sums1, False))
    drq = _ret_bwd_q(rq, rk, rvv, dy, lgam)
    drk, drv = _ret_bwd_kv(rq, rk, rvv, dy, lgam)
    dk, dv, dqt = _attn_bwd(q, k, v, do, qt, kt, dot_, lse_rows, delta_rows)
    dlat, dkpe, dqb, dkn, dqg, dkg = _proj_mla_bwd(dqt, dk, dv, lat, tabs, w_uq, w_uk, w_uv, qn_g, kvn_g, tm)
    dh1, dpr = _proj_bwd(drq, drk, drv, drg, dz1, dlat, dkpe, dgates, cos_r, sin_r, w_r, w_c, w_kpe, w_g, tm)
    g_uq = _mm_tn("wg_uq", latn[:, :Q_LORA], dqb).reshape(Q_LORA, H, MLA_QK)[:, :, :MLA_NOPE + MLA_ROPE]
    g_uk = _mm_tn("wg_uk", latn[:, Q_LORA:], dkn).reshape(KV_LORA, H, MLA_NOPE)
    g_uv = _mm_tn("wg_uv", latn[:, Q_LORA:], dv).reshape(KV_LORA, H, MLA_DV)
    grads.update({
        "w_in": jnp.concatenate([_mm_tn("wg_in_r", h1, dpr), _mm_tn("wg_in_c", h1, dlat),
                                 _mm_tn("wg_in_kpe", h1, dkpe)[:, :MLA_ROPE], _mm_tn("wg_in_g", h1, dgates)], axis=1),
        "w_ret_o": _mm_tn("wg_ret_o", yr, dyret),
        "w_uq": g_uq.reshape(Q_LORA, H * (MLA_NOPE + MLA_ROPE)),
        "w_ukv": jnp.concatenate([g_uk, g_uv], axis=2).reshape(KV_LORA, H * (MLA_NOPE + MLA_DV)),
        "w_mla_o": _mm_tn("wg_mla_o", o, dymla),
        "w_out": _mm_tn("wg_out", mix, dz1)})
    sums2 = _chip_sums("2", REDUCE_GROUPS[1], grads)
    dx, da1, s1, df1, dg0, db0, *parts2 = _ffn_bwd("ffn1_bwd", dh1, z0, a1, w["ffn1_w_in"], w["ffn1_w_out"], lng[0], tm,
                                                   exchange=(sums2, False))
    grads.update({"ffn1_w_in": _mm_tn("wg_ffn1_in", x, da1, n_split=N_CHIPS),
                  "ffn1_w_out": _mm_tn("wg_ffn1_out", s1, df1)})
    sums3 = _chip_sums("3", REDUCE_GROUPS[2], grads)
    parts3 = _chips_exchange("reduce_chips_3", sums3, False)

    names = [n for group in REDUCE_GROUPS for n in group]
    totals = _block_totals(names, sums1 + sums2 + sums3, list(parts1) + list(parts2) + list(parts3))
    others = _sibling_swap("reduce_join", totals, False)
    reduced = {n: _by_core(t, o_, 0) for n, t, o_ in zip(names, totals, others)}
    small = {"ln_g": jnp.concatenate([dg0, dg1, dg2, dg3], axis=0), "ln_b": jnp.concatenate([db0, db1, db2, db3], axis=0),
             "ret_gn_g": dgn, "q_norm_g": dqg, "kv_norm_g": dkg}
    return loss[0, 0], dx, reduced, small


def kernel(x, p, positions, ln_g, ln_b, ffn1_w_in, ffn1_w_out, w_in, ret_gn_g, w_ret_o, q_norm_g, kv_norm_g, w_uq, w_ukv, w_mla_o, w_out, ffn2_w_in, ffn2_w_out, ple_w_gate, ple_w_proj, loss_target, m_ln_g, m_ln_b, m_ffn1_w_in, m_ffn1_w_out, m_w_in, m_ret_gn_g, m_w_ret_o, m_q_norm_g, m_kv_norm_g, m_w_uq, m_w_ukv, m_w_mla_o, m_w_out, m_ffn2_w_in, m_ffn2_w_out, m_ple_w_gate, m_ple_w_proj, v_ln_g, v_ln_b, v_ffn1_w_in, v_ffn1_w_out, v_w_in, v_ret_gn_g, v_w_ret_o, v_q_norm_g, v_kv_norm_g, v_w_uq, v_w_ukv, v_w_mla_o, v_w_out, v_ffn2_w_in, v_ffn2_w_out, v_ple_w_gate, v_ple_w_proj):
    names = ("ln_g", "ln_b", "ffn1_w_in", "ffn1_w_out", "w_in", "ret_gn_g", "w_ret_o", "q_norm_g", "kv_norm_g", "w_uq",
             "w_ukv", "w_mla_o", "w_out", "ffn2_w_in", "ffn2_w_out", "ple_w_gate", "ple_w_proj")
    weights = dict(zip(names, (ln_g, ln_b, ffn1_w_in, ffn1_w_out, w_in, ret_gn_g, w_ret_o, q_norm_g, kv_norm_g, w_uq,
                               w_ukv, w_mla_o, w_out, ffn2_w_in, ffn2_w_out, ple_w_gate, ple_w_proj)))
    m_in = dict(zip(names, (m_ln_g, m_ln_b, m_ffn1_w_in, m_ffn1_w_out, m_w_in, m_ret_gn_g, m_w_ret_o, m_q_norm_g,
                            m_kv_norm_g, m_w_uq, m_w_ukv, m_w_mla_o, m_w_out, m_ffn2_w_in, m_ffn2_w_out, m_ple_w_gate,
                            m_ple_w_proj)))
    v_in = dict(zip(names, (v_ln_g, v_ln_b, v_ffn1_w_in, v_ffn1_w_out, v_w_in, v_ret_gn_g, v_w_ret_o, v_q_norm_g,
                            v_kv_norm_g, v_w_uq, v_w_ukv, v_w_mla_o, v_w_out, v_ffn2_w_in, v_ffn2_w_out, v_ple_w_gate,
                            v_ple_w_proj)))
    chip = 2 * lax.axis_index("x") + lax.axis_index("y")
    D = x.shape[-1]
    dq = D // N_CHIPS

    shards = {n: weights[n][0] for n in BIG_WEIGHTS}
    ln_all = _all_devices("gather_ln", jnp.concatenate([ln_g[0], ln_b[0]], axis=0), False)
    ln_full = ln_all[::2].transpose(1, 0, 2).reshape(2 * N_LN, D)
    loss, dx, big, small = _local_step(x[0], p[0, 0], positions, loss_target[0], shards, ln_full[:N_LN],
                                       ln_full[N_LN:], ret_gn_g, q_norm_g, kv_norm_g)

    loss = lax.psum(loss, ("x", "y", "c"))
    small_names = ("ln_g", "ln_b", "ret_gn_g", "q_norm_g", "kv_norm_g")
    flat = jnp.concatenate([small[n].reshape(-1) for n in small_names])
    rows = -(-flat.shape[0] // LANES // 8) * 8
    flat = jnp.pad(flat, (0, rows * LANES - flat.shape[0])).reshape(rows, LANES)
    flat = _all_devices("reduce_small", flat, True).reshape(-1)
    off = 0
    for n in small_names:
        size = small[n].size
        small[n] = flat[off:off + size].reshape(small[n].shape)
        off += size
    g_out = dict(big)
    for n in ("ln_g", "ln_b"):
        g_out[n] = lax.dynamic_slice_in_dim(small[n], chip * dq, dq, axis=1)
    for n in ("ret_gn_g", "q_norm_g", "kv_norm_g"):
        g_out[n] = small[n]

    deltas, new_m, new_v = {}, {}, {}
    for n in names:
        g = g_out[n].reshape(weights[n].shape)
        g_out[n] = g
        deltas[n], new_m[n], new_v[n] = _adamw("adamw_" + n, weights[n], g, m_in[n], v_in[n])
    return (loss, dx[None], *[g_out[n] for n in names], *[deltas[n] for n in names], *[new_m[n] for n in names],
            *[new_v[n] for n in names])
```

```python
import functools

import jax
import jax.numpy as jnp
from jax import lax
from jax.experimental import pallas as pl
from jax.experimental.pallas import tpu as pltpu

D_MODEL = 1024
CHUNK = 64
D_PLE = 256
D_FF = 2816
RET_HEADS = 8
RET_DK = 128
RET_DV = 256
MLA_HEADS = 8
MLA_NOPE = 128
MLA_ROPE = 64
MLA_DV = 128
MLA_QK = 256
Q_LORA = 256
KV_LORA = 256
ROPE_BASE = 10000.0
EPS = 1e-5
N_LN = 4
ALPHA = 2.0 ** 0.25
ADAM_LR = 0.001
ADAM_B1 = 0.9
ADAM_B2 = 0.999
ADAM_EPS = 1e-08
ADAM_WD = 0.01
ADAM_STEP = 10

LANES = 128
VMEM_LIMIT = 60 << 20
N_CHIPS = 4

F32 = jnp.float32
BF16 = jnp.bfloat16
MESH = pl.DeviceIdType.MESH
HBM_SPEC = pl.BlockSpec(memory_space=pltpu.HBM)
VMEM_SPEC = pl.BlockSpec(memory_space=pltpu.VMEM)

BIG_WEIGHTS = ("ffn1_w_in", "ffn1_w_out", "w_in", "w_ret_o", "w_uq", "w_ukv", "w_mla_o", "w_out",
               "ffn2_w_in", "ffn2_w_out", "ple_w_gate", "ple_w_proj")
COL_SHARDED = ("ffn1_w_in", "w_in", "w_uq", "w_ukv", "ffn2_w_in", "ple_w_proj")


def _dot(a, b):
    return jnp.dot(a, b, preferred_element_type=F32)


def _dot_nt(a, b):
    return lax.dot_general(a, b, (((1,), (1,)), ((), ())), preferred_element_type=F32)


def _dot_tn(a, b):
    return lax.dot_general(a, b, (((0,), (0,)), ((), ())), preferred_element_type=F32)


def _bf(x):
    return x.astype(BF16)


def _sigmoid(x):
    return 1.0 / (1.0 + jnp.exp(-x))


def _mean(x):
    return jnp.mean(x, axis=-1, keepdims=True)


def _ln_stats(z):
    zc = z - _mean(z)
    rstd = lax.rsqrt(_mean(zc * zc) + EPS)
    return zc * rstd, rstd


def _ln_bwd(dy, xhat, rstd, g):
    dxhat = dy * g
    dz = rstd * (dxhat - _mean(dxhat) - xhat * _mean(dxhat * xhat))
    return dz, jnp.sum(dy * xhat, axis=0, keepdims=True), jnp.sum(dy, axis=0, keepdims=True)


def _roll(x, shift):
    return pltpu.roll(x, shift, 1)


def _chunk_of(idx):
    return jnp.right_shift(idx, CHUNK.bit_length() - 1)


def _tile(n, cap, mult=LANES):
    if n <= cap:
        return n
    for t in range(cap - cap % mult, 0, -mult):
        if n % t == 0:
            return t
    return n


def _zero_map(nd, *_):
    return (0,) * nd


def _params(sem):
    return pltpu.CompilerParams(dimension_semantics=sem, vmem_limit_bytes=VMEM_LIMIT)


def _rowcall(name, body, n_rows, tm, row_ins, full_ins, row_outs, acc_outs=(), tiled_outs=(), tiled_ins=(),
             exchange=None):
    n_steps = n_rows // tm
    ex_srcs, broadcast = exchange if exchange else ((), False)
    n_ex = len(ex_srcs)
    n_in = len(row_ins) + len(tiled_ins) + len(full_ins)
    n_out = len(row_outs) + len(acc_outs) + len(tiled_outs)

    def kern(*refs):
        step = pl.program_id(0)
        ex_in, ex_out = refs[n_in:n_in + n_ex], refs[n_in + n_ex + n_out:n_in + 2 * n_ex + n_out]
        sems = refs[n_in + 2 * n_ex + n_out:]
        if n_ex:
            @pl.when(step == 0)
            def _():
                for cp in _chip_copies(ex_in, ex_out, *sems, broadcast):
                    cp.start()

        body(step, *refs[:n_in], *refs[n_in + n_ex:n_in + n_ex + n_out])
        if n_ex:
            @pl.when(step == n_steps - 1)
            def _():
                _wait_copies(_chip_copies(ex_in, ex_out, *sems, broadcast))

    in_specs = [pl.BlockSpec((tm, a.shape[1]), lambda i: (i, 0)) for a in row_ins]
    in_specs += [spec for (_, spec) in tiled_ins]
    row_ins = list(row_ins) + [a for (a, _) in tiled_ins]
    in_specs += [pl.BlockSpec(a.shape, functools.partial(_zero_map, a.ndim), pipeline_mode=pl.Buffered(1))
                 for a in full_ins]
    in_specs += [HBM_SPEC] * n_ex
    out_specs = [pl.BlockSpec((tm, w), lambda i: (i, 0)) for (w, _) in row_outs]
    out_specs += [pl.BlockSpec(s, functools.partial(_zero_map, len(s))) for (s, _) in acc_outs]
    out_specs += [spec for (_, spec) in tiled_outs]
    out_specs += [HBM_SPEC] * n_ex
    out_shape = [jax.ShapeDtypeStruct((n_rows, w), dt) for (w, dt) in row_outs]
    out_shape += [jax.ShapeDtypeStruct(s, dt) for (s, dt) in acc_outs]
    out_shape += [shape for (shape, _) in tiled_outs]
    out_shape += _exchange_shapes(ex_srcs)
    return pl.pallas_call(kern, grid=(n_steps,), in_specs=in_specs, out_specs=out_specs, out_shape=out_shape,
                          scratch_shapes=_dma_sems(n_ex * N_PEER_CHIPS) if n_ex else [], name=name,
                          compiler_params=_params(("arbitrary",)))(*row_ins, *full_ins, *ex_srcs)


def _acc(step, ref, val):
    @pl.when(step == 0)
    def _():
        ref[...] = val

    @pl.when(step != 0)
    def _():
        ref[...] += val


def _ffn_fwd(name, x, w_in4, w_out, ln_g, ln_b, tm, exchange=None):
    T, D = x.shape
    fh = w_in4.shape[2]

    def body(i, x_ref, w4_ref, wo_ref, g_ref, b_ref, h_ref, z_ref, a_ref):
        xv = x_ref[...]
        xb = _bf(xv)
        f = jnp.zeros((tm, D), F32)
        for k in range(2):
            gk = _dot(xb, w4_ref[k])
            uk = _dot(xb, w4_ref[2 + k])
            a_ref[:, k * fh:(k + 1) * fh] = _bf(gk)
            a_ref[:, (2 + k) * fh:(3 + k) * fh] = _bf(uk)
            f += _dot(_bf(gk * _sigmoid(gk) * uk), wo_ref[k * fh:(k + 1) * fh, :])
        z = ALPHA * xv + 0.5 * f
        xhat, _ = _ln_stats(z)
        z_ref[...] = z
        h_ref[...] = xhat * g_ref[...] + b_ref[...]

    return _rowcall(name, body, T, tm, [x], [w_in4, w_out, ln_g, ln_b],
                    [(D, F32), (D, F32), (4 * fh, BF16)], exchange=exchange)


def _ffn_bwd(name, dh, z, a, w_in4, w_out, ln_g, tm, exchange=None):
    T, D = dh.shape
    fh = w_in4.shape[2]

    def body(i, dh_ref, z_ref, a_ref, w4_ref, wo_ref, g_ref, dx_ref, da_ref, s_ref, df_ref, dg_ref, db_ref):
        xhat, rstd = _ln_stats(z_ref[...])
        dz, dg, db = _ln_bwd(dh_ref[...], xhat, rstd, g_ref[...])
        _acc(i, dg_ref, dg)
        _acc(i, db_ref, db)
        dfb = _bf(0.5 * dz)
        df_ref[...] = dfb
        dx = ALPHA * dz
        for k in range(2):
            gk = a_ref[:, k * fh:(k + 1) * fh].astype(F32)
            uk = a_ref[:, (2 + k) * fh:(3 + k) * fh].astype(F32)
            ds = _dot_nt(dfb, wo_ref[k * fh:(k + 1) * fh, :])
            sig = _sigmoid(gk)
            silu = gk * sig
            dgk = _bf(ds * uk * sig * (1.0 + gk * (1.0 - sig)))
            duk = _bf(ds * silu)
            s_ref[:, k * fh:(k + 1) * fh] = _bf(silu * uk)
            da_ref[:, k * fh:(k + 1) * fh] = dgk
            da_ref[:, (2 + k) * fh:(3 + k) * fh] = duk
            dx += _dot_nt(dgk, w4_ref[k]) + _dot_nt(duk, w4_ref[2 + k])
        dx_ref[...] = dx

    return _rowcall(name, body, T, tm, [dh, z, a], [w_in4, w_out, ln_g],
                    [(D, F32), (4 * fh, BF16), (2 * fh, BF16), (D, BF16)],
                    [((1, D), F32), ((1, D), F32)], exchange=exchange)


def _mm_tn(name, a, b, out_dtype=BF16, n_split=1):
    T, M = a.shape
    N = b.shape[1]
    tk = _tile(T, 2048, 8)
    tm = _tile(M, 1408)
    tn = _tile(N // n_split, 1536)
    per = N // n_split // tn
    nk = T // tk
    if n_split > 1:
        out_spec = pl.BlockSpec((None, tm, tn), lambda i, j, k: (j // per, i, j % per))
        out_shape = jax.ShapeDtypeStruct((n_split, M, N // n_split), out_dtype)
    else:
        out_spec = pl.BlockSpec((tm, tn), lambda i, j, k: (i, j))
        out_shape = jax.ShapeDtypeStruct((M, N), out_dtype)

    def kern(a_ref, b_ref, o_ref, acc_ref):
        k = pl.program_id(2)
        part = _dot_tn(_bf(a_ref[...]), _bf(b_ref[...]))

        @pl.when(k == 0)
        def _():
            acc_ref[...] = part

        @pl.when(k != 0)
        def _():
            acc_ref[...] += part

        @pl.when(k == nk - 1)
        def _():
            o_ref[...] = acc_ref[...].astype(out_dtype)

    return pl.pallas_call(
        kern, grid=(M // tm, N // tn, nk),
        in_specs=[pl.BlockSpec((tk, tm), lambda i, j, k: (k, i)), pl.BlockSpec((tk, tn), lambda i, j, k: (k, j))],
        out_specs=out_spec, out_shape=out_shape,
        scratch_shapes=[pltpu.VMEM((tm, tn), F32)], name=name,
        compiler_params=_params(("arbitrary", "arbitrary", "arbitrary")))(a, b)


def _proj_ret(h1, w_r, cos_r, sin_r, tm):
    T, D = h1.shape
    qk = RET_HEADS * RET_DK
    rv = RET_HEADS * RET_DV

    def body(i, h_ref, cos_ref, sin_ref, w_ref, q_ref, k_ref, v_ref, g_ref):
        hb = _bf(h_ref[...])
        cos, sin = cos_ref[...], sin_ref[...]
        for out_ref, off, scale in ((q_ref, 0, 1.0), (k_ref, qk, RET_DK ** -0.5)):
            pr = _dot(hb, w_ref[:, off:off + qk])
            for h in range(RET_HEADS):
                t = pr[:, h * RET_DK:(h + 1) * RET_DK]
                out_ref[:, h * RET_DK:(h + 1) * RET_DK] = _bf((t * cos + _roll(t, RET_DK // 2) * sin) * scale)
        v_ref[...] = _bf(_dot(hb, w_ref[:, 2 * qk:2 * qk + rv]))
        g_ref[...] = _dot(hb, w_ref[:, 2 * qk + rv:2 * qk + 2 * rv])

    return _rowcall("proj_ret", body, T, tm, [h1, cos_r, sin_r], [w_r],
                    [(qk, BF16), (qk, BF16), (rv, BF16), (rv, F32)])


def _rope_pe(t, c, s1, s2):
    return t * c + _roll(t, LANES - MLA_ROPE // 2) * s1 + _roll(t, MLA_ROPE // 2) * s2


def _rope_pe_bwd(dy, c, s1, s2):
    return dy * c + _roll(dy * s1, MLA_ROPE // 2) + _roll(dy * s2, LANES - MLA_ROPE // 2)


def _rms(x, g):
    r = lax.rsqrt(_mean(x * x) + EPS)
    return x * r, r


def _attn_block(T):
    return min(512, T)


def _transposed_blocks(T, tm, w, dtype):
    tb = _attn_block(T)
    per = tb // tm
    return (jax.ShapeDtypeStruct((T // tb, MLA_HEADS, w, tb), dtype),
            pl.BlockSpec((None, MLA_HEADS, w, tm), lambda i: (i // per, 0, 0, i % per)))


def _proj_mla(h1, tabs, w_c, w_kpe, w_g, w_uq, w_uk, w_uv, qn_g, kvn_g, tm):
    T, D = h1.shape
    H = MLA_HEADS

    def body(i, h_ref, c_ref, s1_ref, s2_ref, wc_ref, wk_ref, wg_ref, wuq_ref, wuk_ref, wuv_ref, qg_ref, kg_ref,
             lat_ref, gt_ref, q_ref, k_ref, v_ref, ln_ref, qt_ref, kt_ref, vt_ref):
        hb = _bf(h_ref[...])
        c, s1, s2 = c_ref[...], s1_ref[...], s2_ref[...]
        lat = _dot(hb, wc_ref[...])
        lat_ref[...] = lat
        gt_ref[...] = _dot(hb, wg_ref[...])
        cqn, _ = _rms(lat[:, :Q_LORA], None)
        ckn, _ = _rms(lat[:, Q_LORA:], None)
        cqn = _bf(cqn * qg_ref[...])
        ckn = _bf(ckn * kg_ref[...])
        ln_ref[:, :Q_LORA] = cqn
        ln_ref[:, Q_LORA:] = ckn
        q = _dot(cqn, wuq_ref[...])
        kn = _dot(ckn, wuk_ref[...])
        vv = _dot(ckn, wuv_ref[...])
        v_ref[...] = _bf(vv)
        kpe = _rope_pe(_dot(hb, wk_ref[...]), c, s1, s2)
        for h in range(H):
            o = h * MLA_QK
            qh = jnp.concatenate([q[:, o:o + MLA_NOPE], _rope_pe(q[:, o + MLA_NOPE:o + MLA_QK], c, s1, s2)], axis=1)
            kh = jnp.concatenate([kn[:, h * MLA_NOPE:(h + 1) * MLA_NOPE], kpe], axis=1)
            q_ref[:, o:o + MLA_QK] = _bf(qh)
            k_ref[:, o:o + MLA_QK] = _bf(kh)
            qt_ref[h] = _bf(qh.T)
            kt_ref[h] = _bf(kh.T)
            vt_ref[h] = _bf(vv[:, h * MLA_DV:(h + 1) * MLA_DV].T)

    lat_w = Q_LORA + KV_LORA
    return _rowcall("proj_mla", body, T, tm, [h1, *tabs], [w_c, w_kpe, w_g, w_uq, w_uk, w_uv, qn_g, kvn_g],
                    [(lat_w, F32), (2 * D, F32), (H * MLA_QK, BF16), (H * MLA_QK, BF16), (H * MLA_DV, BF16),
                     (lat_w, BF16)],
                    tiled_outs=[_transposed_blocks(T, tm, MLA_QK, BF16), _transposed_blocks(T, tm, MLA_QK, BF16),
                                _transposed_blocks(T, tm, MLA_DV, BF16)])


def _ret_block(T):
    return min(256, T)


def _ret_decay(lg, bt):
    n = lax.broadcasted_iota(jnp.int32, (bt, bt), 0)
    m = lax.broadcasted_iota(jnp.int32, (bt, bt), 1)
    dmat = jnp.where(_chunk_of(m) <= _chunk_of(n), jnp.exp(lg * jnp.abs(n - m).astype(F32)), 0.0)
    pos = lax.broadcasted_iota(jnp.int32, (bt, 1), 0).astype(F32)
    xi = jnp.exp(lg * (pos + 1.0))
    zeta = jnp.exp(lg * (bt - 1.0 - pos))
    return dmat, xi, zeta, jnp.exp(lg * bt)


def _ret_specs(bt, rev, nb):
    def blk(w):
        if rev:
            return pl.BlockSpec((bt, w), lambda h, b: (nb - 1 - b, h))
        return pl.BlockSpec((bt, w), lambda h, b: (b, h))
    return pl.BlockSpec((None, 1, LANES), lambda h, b: (h, 0, 0)), blk


def _ret_fwd(rq, rk, rv, lgam):
    T = rq.shape[0]
    bt = _ret_block(T)
    nb = T // bt
    lg_spec, blk = _ret_specs(bt, False, nb)

    def kern(lg_ref, q_ref, k_ref, v_ref, y_ref, s_ref):
        @pl.when(pl.program_id(1) == 0)
        def _():
            s_ref[...] = jnp.zeros_like(s_ref)

        dmat, xi, zeta, gb = _ret_decay(lg_ref[:, :1], bt)
        q, k, v = q_ref[...], k_ref[...], v_ref[...]
        sc = _dot_nt(q, k) * dmat
        y_ref[...] = _dot(_bf(sc), v) + _dot(q, _bf(s_ref[...])) * xi
        s_ref[...] = s_ref[...] * gb + _dot_tn(_bf(k.astype(F32) * zeta), v)

    return pl.pallas_call(
        kern, grid=(RET_HEADS, nb), in_specs=[lg_spec, blk(RET_DK), blk(RET_DK), blk(RET_DV)],
        out_specs=blk(RET_DV), out_shape=jax.ShapeDtypeStruct((T, RET_HEADS * RET_DV), F32),
        scratch_shapes=[pltpu.VMEM((RET_DK, RET_DV), F32)], name="ret_fwd",
        compiler_params=_params(("arbitrary", "arbitrary")))(lgam, rq, rk, rv)


def _ret_bwd_q(rq, rk, rv, dy, lgam):
    T = rq.shape[0]
    bt = _ret_block(T)
    nb = T // bt
    lg_spec, blk = _ret_specs(bt, False, nb)

    def kern(lg_ref, k_ref, v_ref, dy_ref, dq_ref, s_ref):
        @pl.when(pl.program_id(1) == 0)
        def _():
            s_ref[...] = jnp.zeros_like(s_ref)

        dmat, xi, zeta, gb = _ret_decay(lg_ref[:, :1], bt)
        k, v, dy = k_ref[...], v_ref[...], dy_ref[...]
        dp = _dot_nt(dy, v) * dmat
        dq_ref[...] = _dot(_bf(dp), k) + _dot_nt(dy, _bf(s_ref[...])) * xi
        s_ref[...] = s_ref[...] * gb + _dot_tn(_bf(k.astype(F32) * zeta), v)

    return pl.pallas_call(
        kern, grid=(RET_HEADS, nb), in_specs=[lg_spec, blk(RET_DK), blk(RET_DV), blk(RET_DV)],
        out_specs=blk(RET_DK), out_shape=jax.ShapeDtypeStruct((T, RET_HEADS * RET_DK), F32),
        scratch_shapes=[pltpu.VMEM((RET_DK, RET_DV), F32)], name="ret_bwd_q",
        compiler_params=_params(("arbitrary", "arbitrary")))(lgam, rk, rv, dy)


def _ret_bwd_kv(rq, rk, rv, dy, lgam):
    T = rq.shape[0]
    bt = _ret_block(T)
    nb = T // bt
    lg_spec, blk = _ret_specs(bt, True, nb)

    def kern(lg_ref, q_ref, k_ref, v_ref, dy_ref, dk_ref, dv_ref, g_ref):
        @pl.when(pl.program_id(1) == 0)
        def _():
            g_ref[...] = jnp.zeros_like(g_ref)

        dmat, xi, zeta, gb = _ret_decay(lg_ref[:, :1], bt)
        q, k, v, dy = q_ref[...], k_ref[...], v_ref[...], dy_ref[...]
        gs = _bf(g_ref[...])
        p = _dot_nt(q, k) * dmat
        dp = _dot_nt(dy, v) * dmat
        dv_ref[...] = _bf(_dot_tn(_bf(p), dy) + _dot(k, gs) * zeta)
        dk_ref[...] = _dot_tn(_bf(dp), q) + _dot_nt(v, gs) * zeta
        g_ref[...] = g_ref[...] * gb + _dot_tn(_bf(q.astype(F32) * xi), dy)

    return pl.pallas_call(
        kern, grid=(RET_HEADS, nb), in_specs=[lg_spec, blk(RET_DK), blk(RET_DK), blk(RET_DV), blk(RET_DV)],
        out_specs=[blk(RET_DK), blk(RET_DV)],
        out_shape=[jax.ShapeDtypeStruct((T, RET_HEADS * RET_DK), F32),
                   jax.ShapeDtypeStruct((T, RET_HEADS * RET_DV), BF16)],
        scratch_shapes=[pltpu.VMEM((RET_DK, RET_DV), F32)], name="ret_bwd_kv",
        compiler_params=_params(("arbitrary", "arbitrary")))(lgam, rq, rk, rv, dy)


def _attn_mask(tb):
    r = lax.broadcasted_iota(jnp.int32, (tb, tb), 0)
    c = lax.broadcasted_iota(jnp.int32, (tb, tb), 1)
    return _chunk_of(c) <= _chunk_of(r)


def _attn_mask_t(tb):
    key = lax.broadcasted_iota(jnp.int32, (tb, tb), 0)
    qry = lax.broadcasted_iota(jnp.int32, (tb, tb), 1)
    return _chunk_of(key) <= _chunk_of(qry)


ATTN_SCALE = (MLA_NOPE + MLA_ROPE) ** -0.5
MASKED = -1e30
LOG2E = 1.4426950408889634
SUBLANES = 8


def _head_blocks(nb, w, tb):
    return pl.BlockSpec((nb, None, w, tb), lambda h, i: (0, h, 0, 0))


def _one_block(w, tb):
    return pl.BlockSpec((None, None, w, tb), lambda h, i: (i, h, 0, 0))


def _attn_fwd(k, qt, vt, exchange=()):
    T = k.shape[0]
    tb = _attn_block(T)
    nb = T // tb

    n_ex = len(exchange)

    def kern(qt_ref, k_ref, vt_ref, *refs):
        ex_in, (o_ref, lser_ref), ex_out = refs[:n_ex], refs[n_ex:n_ex + 2], refs[n_ex + 2:2 * n_ex + 2]
        m_ref, l_ref, acc_ref = refs[2 * n_ex + 2:2 * n_ex + 5]
        sems = refs[2 * n_ex + 5:]
        qb = pl.program_id(1)
        first = jnp.logical_and(pl.program_id(0) == 0, qb == 0)
        last = jnp.logical_and(pl.program_id(0) == MLA_HEADS - 1, qb == nb - 1)
        if n_ex:
            @pl.when(first)
            def _():
                for cp in _chip_copies(ex_in, ex_out, *sems, True):
                    cp.start()

        qt = qt_ref[...]
        m_ref[...] = jnp.full_like(m_ref, MASKED)
        l_ref[...] = jnp.zeros_like(l_ref)
        acc_ref[...] = jnp.zeros_like(acc_ref)

        def step(kb, diagonal):
            rows = pl.ds(pl.multiple_of(kb * tb, tb), tb)
            s = _dot(k_ref[rows, :], qt) * (ATTN_SCALE * LOG2E)
            if diagonal:
                s = jnp.where(_attn_mask_t(tb), s, MASKED)
            m_old = m_ref[...]
            m_new = jnp.maximum(m_old, jnp.max(s, axis=0, keepdims=True))
            p = jnp.exp2(s - m_new)
            corr = jnp.exp2(m_old - m_new)
            l_ref[...] = l_ref[...] * corr + jnp.sum(p, axis=0, keepdims=True)
            acc_ref[...] = acc_ref[...] * corr + _dot(vt_ref[kb], _bf(p))
            m_ref[...] = m_new

        def pair_body(i, carry):
            step(2 * i, False)
            step(2 * i + 1, False)
            return carry

        lax.fori_loop(0, qb // 2, pair_body, 0)

        @pl.when(qb % 2 == 1)
        def _():
            step(qb - 1, False)

        step(qb, True)
        o_ref[...] = (acc_ref[...] / l_ref[...]).T
        lser_ref[...] = jnp.broadcast_to(m_ref[...] + jnp.log2(l_ref[...]), (SUBLANES, tb))
        if n_ex:
            @pl.when(last)
            def _():
                _wait_copies(_chip_copies(ex_in, ex_out, *sems, True))

    return pl.pallas_call(
        kern, grid=(MLA_HEADS, nb),
        in_specs=[_one_block(MLA_QK, tb), pl.BlockSpec((T, MLA_QK), lambda h, i: (0, h)),
                  _head_blocks(nb, MLA_DV, tb)] + [HBM_SPEC] * n_ex,
        out_specs=[pl.BlockSpec((tb, MLA_DV), lambda h, i: (i, h)), _one_block(SUBLANES, tb)] + [HBM_SPEC] * n_ex,
        out_shape=[jax.ShapeDtypeStruct((T, MLA_HEADS * MLA_DV), F32),
                   jax.ShapeDtypeStruct((nb, MLA_HEADS, SUBLANES, tb), F32)] + _exchange_shapes(exchange),
        scratch_shapes=[pltpu.VMEM((1, tb), F32), pltpu.VMEM((1, tb), F32), pltpu.VMEM((MLA_DV, tb), F32)]
        + (_dma_sems(n_ex * N_PEER_CHIPS) if n_ex else []),
        name="attn_fwd", compiler_params=_params(("arbitrary", "arbitrary")))(qt, k, vt, *exchange)


def _attn_bwd(q, k, v, do, qt, kt, dot_, lse_rows, delta_rows):
    T = q.shape[0]
    tb = _attn_block(T)
    nb = T // tb

    def kern(q_ref, k_ref, v_ref, do_ref, qt_ref, kt_ref, dot_ref, lse_ref, dl_ref, dk_ref, dv_ref, dqt_ref, dv_acc):
        kb = pl.program_id(1)
        kv, vv, ktv = k_ref[...], v_ref[...], kt_ref[...]
        dk_ref[...] = jnp.zeros_like(dk_ref)
        dv_acc[...] = jnp.zeros_like(dv_acc)

        @pl.when(kb == 0)
        def _():
            dqt_ref[...] = jnp.zeros_like(dqt_ref)

        def step(qb, diagonal):
            rows = pl.ds(pl.multiple_of(qb * tb, tb), tb)
            s = _dot(kv, qt_ref[qb]) * (ATTN_SCALE * LOG2E)
            if diagonal:
                s = jnp.where(_attn_mask_t(tb), s, MASKED)
            p = jnp.exp2(s - lse_ref[qb][:1, :])
            dv_acc[...] += _dot(_bf(p), do_ref[rows, :])
            ds = _bf(p * (_dot(vv, dot_ref[qb]) - dl_ref[qb][:1, :]) * ATTN_SCALE)
            dk_ref[...] += _dot(ds, q_ref[rows, :])
            dqt_ref[qb] += _dot(ktv, ds)

        def loop_body(qb, carry):
            step(qb, False)
            return carry

        step(kb, True)
        lax.fori_loop(kb + 1, nb, loop_body, 0)
        dv_ref[...] = _bf(dv_acc[...])

    def blk(w):
        return pl.BlockSpec((tb, w), lambda h, i: (i, h))

    def full(w):
        return pl.BlockSpec((T, w), lambda h, i: (0, h))

    return pl.pallas_call(
        kern, grid=(MLA_HEADS, nb),
        in_specs=[full(MLA_QK), blk(MLA_QK), blk(MLA_DV), full(MLA_DV), _head_blocks(nb, MLA_QK, tb),
                  _one_block(MLA_QK, tb), _head_blocks(nb, MLA_DV, tb), _head_blocks(nb, SUBLANES, tb),
                  _head_blocks(nb, SUBLANES, tb)],
        out_specs=[blk(MLA_QK), blk(MLA_DV), _head_blocks(nb, MLA_QK, tb)],
        out_shape=[jax.ShapeDtypeStruct((T, MLA_HEADS * MLA_QK), F32),
                   jax.ShapeDtypeStruct((T, MLA_HEADS * MLA_DV), BF16),
                   jax.ShapeDtypeStruct((nb, MLA_HEADS, MLA_QK, tb), F32)],
        scratch_shapes=[pltpu.VMEM((tb, MLA_DV), F32)],
        name="attn_bwd", compiler_params=_params(("arbitrary", "arbitrary")))(
            q, k, v, do, qt, kt, dot_, lse_rows, delta_rows)


def _group_norm(y):
    yc = y - _mean(y)
    rstd = lax.rsqrt(_mean(yc * yc) + EPS)
    return yc * rstd, rstd


def _mix_fwd(y, rg, o, gates, h1, gn_g, w_ret_o, w_mla_o, w_out, ln_g, ln_b, tm):
    T, D = h1.shape

    def body(i, y_ref, rg_ref, o_ref, gt_ref, h_ref, gn_ref, wr_ref, wm_ref, wo_ref, g_ref, b_ref,
             h2_ref, z_ref, yret_ref, ymla_ref, yr_ref, mix_ref):
        for h in range(RET_HEADS):
            sl = slice(h * RET_DV, (h + 1) * RET_DV)
            yn, _ = _group_norm(y_ref[:, sl])
            r = rg_ref[:, sl]
            yr_ref[:, sl] = _bf(r * _sigmoid(r) * (yn * gn_ref[:, sl]))
        yret = _dot(yr_ref[...], wr_ref[...])
        ymla = _dot(_bf(o_ref[...]), wm_ref[...])
        yret_ref[...] = yret
        ymla_ref[...] = ymla
        mix = _bf(_sigmoid(gt_ref[:, :D]) * yret + _sigmoid(gt_ref[:, D:]) * ymla)
        mix_ref[...] = mix
        z = ALPHA * h_ref[...] + _dot(mix, wo_ref[...])
        xhat, _ = _ln_stats(z)
        z_ref[...] = z
        h2_ref[...] = xhat * g_ref[...] + b_ref[...]

    return _rowcall("mix_fwd", body, T, tm, [y, rg, o, gates, h1], [gn_g, w_ret_o, w_mla_o, w_out, ln_g, ln_b],
                    [(D, F32), (D, F32), (D, F32), (D, F32), (RET_HEADS * RET_DV, BF16), (D, BF16)])


def _mix_bwd(dh2, z1, gates, yret, ymla, y, rg, o, gn_g, w_ret_o, w_mla_o, w_out, ln_g, tm, exchange=None):
    T, D = dh2.shape
    rv = RET_HEADS * RET_DV

    def body(i, dh_ref, z_ref, gt_ref, yret_ref, ymla_ref, y_ref, rg_ref, o_ref, gn_ref, wr_ref, wm_ref, wo_ref, g_ref,
             dz_ref, dgt_ref, drg_ref, dy_ref, do_ref, dyret_ref, dymla_ref, dg_ref, db_ref, dgn_ref, dot_ref,
             dl_ref):
        xhat, rstd = _ln_stats(z_ref[...])
        dz, dg, db = _ln_bwd(dh_ref[...], xhat, rstd, g_ref[...])
        _acc(i, dg_ref, dg)
        _acc(i, db_ref, db)
        dz_ref[...] = dz
        dmix = _dot_nt(_bf(dz), wo_ref[...])
        sr = _sigmoid(gt_ref[:, :D])
        sm = _sigmoid(gt_ref[:, D:])
        dgt_ref[:, :D] = _bf(dmix * yret_ref[...] * sr * (1.0 - sr))
        dgt_ref[:, D:] = _bf(dmix * ymla_ref[...] * sm * (1.0 - sm))
        dyret = _bf(dmix * sr)
        dymla = _bf(dmix * sm)
        dyret_ref[...] = dyret
        dymla_ref[...] = dymla
        dov = _dot_nt(dymla, wm_ref[...])
        do_ref[...] = _bf(dov)
        for h in range(MLA_HEADS):
            sl = slice(h * MLA_DV, (h + 1) * MLA_DV)
            dot_ref[h] = _bf(dov[:, sl].T)
            delta = jnp.sum(dov[:, sl] * o_ref[:, sl], axis=-1, keepdims=True)
            dl_ref[h] = jnp.broadcast_to(delta, (tm, LANES)).T[:SUBLANES, :]
        dyr = _dot_nt(dyret, wr_ref[...])
        dgn = []
        for h in range(RET_HEADS):
            sl = slice(h * RET_DV, (h + 1) * RET_DV)
            yn, grstd = _group_norm(y_ref[:, sl])
            r = rg_ref[:, sl]
            sig = _sigmoid(r)
            d = dyr[:, sl]
            drg_ref[:, sl] = _bf(d * (yn * gn_ref[:, sl]) * sig * (1.0 + r * (1.0 - sig)))
            dt = d * (r * sig)
            dgn.append(jnp.sum(dt * yn, axis=0, keepdims=True))
            dyn = dt * gn_ref[:, sl]
            dy_ref[:, sl] = _bf(grstd * (dyn - _mean(dyn) - yn * _mean(dyn * yn)))
        _acc(i, dgn_ref, jnp.concatenate(dgn, axis=1))

    return _rowcall("mix_bwd", body, T, tm, [dh2, z1, gates, yret, ymla, y, rg, o],
                    [gn_g, w_ret_o, w_mla_o, w_out, ln_g],
                    [(D, F32), (2 * D, BF16), (rv, BF16), (rv, BF16), (MLA_HEADS * MLA_DV, BF16), (D, BF16), (D, BF16)],
                    [((1, D), F32), ((1, D), F32), ((1, rv), F32)],
                    tiled_outs=[_transposed_blocks(T, tm, MLA_DV, BF16), _transposed_blocks(T, tm, SUBLANES, F32)],
                    exchange=exchange)


def _proj_mla_bwd(dqt, dk, dv, lat, tabs, w_uq, w_uk, w_uv, qn_g, kvn_g, tm):
    T = dk.shape[0]
    H = MLA_HEADS
    lat_w = Q_LORA + KV_LORA

    def body(i, dk_ref, dv_ref, lat_ref, c_ref, s1_ref, s2_ref, dqt_ref, wuq_ref, wuk_ref, wuv_ref, qg_ref, kg_ref,
             dlat_ref, dkpe_ref, dqb_ref, dkn_ref, dqg_ref, dkg_ref):
        c, s1, s2 = c_ref[...], s1_ref[...], s2_ref[...]
        dkpe = jnp.zeros((tm, LANES), F32)
        for h in range(H):
            o = h * MLA_QK
            dqh = dqt_ref[h].T
            dqb_ref[:, o:o + MLA_NOPE] = _bf(dqh[:, :MLA_NOPE])
            dqb_ref[:, o + MLA_NOPE:o + MLA_QK] = _bf(_rope_pe_bwd(dqh[:, MLA_NOPE:], c, s1, s2))
            dkn_ref[:, h * MLA_NOPE:(h + 1) * MLA_NOPE] = _bf(dk_ref[:, o:o + MLA_NOPE])
            dkpe += dk_ref[:, o + MLA_NOPE:o + MLA_QK]
        dkpe_ref[...] = _bf(_rope_pe_bwd(dkpe, c, s1, s2))
        dcqn = _dot_nt(dqb_ref[...], wuq_ref[...])
        dckn = _dot_nt(dkn_ref[...], wuk_ref[...]) + _dot_nt(dv_ref[...], wuv_ref[...])
        for dn, x, g_ref, dg_ref, sl in ((dcqn, lat_ref[:, :Q_LORA], qg_ref, dqg_ref, slice(0, Q_LORA)),
                                         (dckn, lat_ref[:, Q_LORA:], kg_ref, dkg_ref, slice(Q_LORA, lat_w))):
            xn, r = _rms(x, None)
            _acc(i, dg_ref, jnp.sum(dn * xn, axis=0, keepdims=True))
            dxn = dn * g_ref[...]
            dlat_ref[:, sl] = _bf(r * (dxn - xn * _mean(dxn * xn)))

    dqt_shape, dqt_spec = _transposed_blocks(T, tm, MLA_QK, F32)
    assert dqt.shape == dqt_shape.shape
    return _rowcall("proj_mla_bwd", body, T, tm, [dk, dv, lat, *tabs], [w_uq, w_uk, w_uv, qn_g, kvn_g],
                    [(lat_w, BF16), (LANES, BF16), (H * MLA_QK, BF16), (H * MLA_NOPE, BF16)],
                    [((1, Q_LORA), F32), ((1, KV_LORA), F32)], tiled_ins=[(dqt, dqt_spec)])


def _proj_bwd(drq, drk, drv, drg, dz1, dlat, dkpe, dgates, cos_r, sin_r, w_r, w_c, w_kpe, w_g, tm):
    T, D = dz1.shape
    qk = RET_HEADS * RET_DK
    rv = RET_HEADS * RET_DV

    def body(i, drq_ref, drk_ref, drv_ref, drg_ref, dz_ref, dlat_ref, dkpe_ref, dgt_ref, cos_ref, sin_ref,
             wr_ref, wc_ref, wk_ref, wg_ref, dh_ref, dpr_ref):
        cos, sin = cos_ref[...], sin_ref[...]
        for src, off, scale in ((drq_ref, 0, 1.0), (drk_ref, qk, RET_DK ** -0.5)):
            for h in range(RET_HEADS):
                d = src[:, h * RET_DK:(h + 1) * RET_DK]
                dpr_ref[:, off + h * RET_DK:off + (h + 1) * RET_DK] = _bf(
                    (d * cos + _roll(d * sin, RET_DK // 2)) * scale)
        dpr_ref[:, 2 * qk:2 * qk + rv] = drv_ref[...]
        dpr_ref[:, 2 * qk + rv:] = drg_ref[...]
        dh_ref[...] = (ALPHA * dz_ref[...] + _dot_nt(dpr_ref[...], wr_ref[...]) + _dot_nt(dlat_ref[...], wc_ref[...])
                       + _dot_nt(dkpe_ref[...], wk_ref[...]) + _dot_nt(dgt_ref[...], wg_ref[...]))

    return _rowcall("proj_bwd", body, T, tm, [drq, drk, drv, drg, dz1, dlat, dkpe, dgates, cos_r, sin_r],
                    [w_r, w_c, w_kpe, w_g], [(D, F32), (2 * qk + 2 * rv, BF16)])


def _ple_loss(h3, p, target, w_gate, w_proj, ln_g, ln_b, tm):
    T, D = h3.shape

    def body(i, h_ref, p_ref, t_ref, wg_ref, wp_ref, g_ref, b_ref, dh_ref, dgp_ref, dpp_ref, loss_ref, dg_ref, db_ref):
        hv = h_ref[...]
        sg = _sigmoid(_dot(_bf(hv), wg_ref[...]))
        pp = _dot(_bf(p_ref[...]), wp_ref[...])
        xhat, rstd = _ln_stats(ALPHA * hv + sg * pp)
        err = xhat * g_ref[...] + b_ref[...] - t_ref[...]
        row_loss = 0.5 * _mean(err * err)
        _acc(i, loss_ref, jnp.broadcast_to(jnp.sum(row_loss, axis=0, keepdims=True), (1, LANES)))
        dz, dg, db = _ln_bwd(err * (1.0 / D), xhat, rstd, g_ref[...])
        _acc(i, dg_ref, dg)
        _acc(i, db_ref, db)
        dgp = _bf(dz * pp * sg * (1.0 - sg))
        dgp_ref[...] = dgp
        dpp_ref[...] = _bf(dz * sg)
        dh_ref[...] = ALPHA * dz + _dot_nt(dgp, wg_ref[...])

    return _rowcall("ple_loss", body, T, tm, [h3, p, target], [w_gate, w_proj, ln_g, ln_b],
                    [(D, F32), (D, BF16), (D, BF16)], [((1, LANES), F32), ((1, D), F32), ((1, D), F32)])


def _ewise(name, fn, ins, n_out, out_dtype=F32):
    r, c = ins[0].shape
    tr = _tile(r, max(8, (1 << 19) // c // 8 * 8), 8)

    def kern(*refs):
        outs = fn(*[x[...] for x in refs[:len(ins)]])
        for o_ref, o in zip(refs[len(ins):], outs):
            o_ref[...] = o.astype(out_dtype)

    spec = pl.BlockSpec((tr, c), lambda i: (i, 0))
    return pl.pallas_call(kern, grid=(r // tr,), in_specs=[spec] * len(ins), out_specs=[spec] * n_out,
                          out_shape=[jax.ShapeDtypeStruct((r, c), out_dtype)] * n_out, name=name,
                          compiler_params=_params(("arbitrary",)))(*ins)


def _adamw_math(w, g, m, v):
    m = ADAM_B1 * m + (1.0 - ADAM_B1) * g
    v = ADAM_B2 * v + (1.0 - ADAM_B2) * (g * g)
    m_hat = m / (1.0 - ADAM_B1 ** ADAM_STEP)
    v_hat = v / (1.0 - ADAM_B2 ** ADAM_STEP)
    return -ADAM_LR * (m_hat / (jnp.sqrt(v_hat) + ADAM_EPS) + ADAM_WD * w), m, v


def _adamw(name, w, g, m, v):
    shape = w.shape
    c = shape[-1]
    flat = [t.reshape(-1, c) for t in (w, g, m, v)]
    return [t.reshape(shape) for t in _ewise(name, _adamw_math, flat, 3)]


def _place():
    return lax.axis_index("x"), lax.axis_index("y"), lax.axis_index("c")


def _dma_sems(n):
    return [pltpu.SemaphoreType.DMA((n,)), pltpu.SemaphoreType.DMA((n,))]


N_PEER_CHIPS = N_CHIPS - 1


def _chips_exchange(name, srcs, broadcast):
    n = len(srcs)

    def kern(*refs):
        cps = _chip_copies(refs[:n], refs[n:2 * n], refs[2 * n], refs[2 * n + 1], broadcast)
        for cp in cps:
            cp.start()
        _wait_copies(cps)

    return pl.pallas_call(
        kern, out_shape=_exchange_shapes(srcs), in_specs=[HBM_SPEC] * n, out_specs=[HBM_SPEC] * n,
        scratch_shapes=_dma_sems(n * N_PEER_CHIPS), name=name)(*srcs)


def _exchange_shapes(srcs):
    return [jax.ShapeDtypeStruct((N_PEER_CHIPS,) + s.shape[1:], s.dtype) for s in srcs]


def _chip_copies(src_refs, out_refs, send_sems, recv_sems, broadcast):
    x, y, c = _place()
    peers = [(1 - x, y), (x, 1 - y), (1 - x, 1 - y)]
    cps = []
    for j, (px, py) in enumerate(peers):
        for a, (src_ref, out_ref) in enumerate(zip(src_refs, out_refs)):
            piece = src_ref.at[c] if broadcast else src_ref.at[2 * px + py]
            cps.append(pltpu.make_async_remote_copy(
                src_ref=piece, dst_ref=out_ref.at[j], send_sem=send_sems.at[a * N_PEER_CHIPS + j],
                recv_sem=recv_sems.at[a * N_PEER_CHIPS + j], device_id=(px, py, c), device_id_type=MESH))
    return cps


def _wait_copies(cps):
    for cp in cps:
        cp.wait_recv()
    for cp in cps:
        cp.wait_send()


def _sibling_swap(name, srcs, halves):
    n = len(srcs)

    def kern(*refs):
        src_refs, out_refs = refs[:n], refs[n:2 * n]
        send_sems, recv_sems = refs[2 * n:]
        x, y, c = _place()

        def copy(a):
            piece = src_refs[a].at[:, 1 - c] if halves else src_refs[a]
            return pltpu.make_async_remote_copy(
                src_ref=piece, dst_ref=out_refs[a], send_sem=send_sems.at[a], recv_sem=recv_sems.at[a],
                device_id=(x, y, 1 - c), device_id_type=MESH)

        cps = [copy(a) for a in range(n)]
        for cp in cps:
            cp.start()
        for cp in cps:
            cp.wait_recv()
        for cp in cps:
            cp.wait_send()

    def out_shape(s):
        return jax.ShapeDtypeStruct((s.shape[0],) + s.shape[2:] if halves else s.shape, s.dtype)

    return pl.pallas_call(
        kern, out_shape=[out_shape(s) for s in srcs], in_specs=[HBM_SPEC] * n, out_specs=[HBM_SPEC] * n,
        scratch_shapes=_dma_sems(n), name=name)(*srcs)


def _all_devices(name, src, reduce):
    r, c = src.shape
    n_dev = 2 * N_CHIPS

    def kern(src_ref, out_ref, *scratch):
        if reduce:
            gat_ref, send_sems, recv_sems = scratch
        else:
            gat_ref = out_ref
            send_sems, recv_sems = scratch
        x, y, cc = _place()
        me = 4 * x + 2 * y + cc
        gat_ref[me] = src_ref[...]
        peers = []
        for j in range(1, n_dev):
            px = 1 - x if j & 4 else x
            py = 1 - y if j & 2 else y
            pc = 1 - cc if j & 1 else cc
            peers.append((px, py, pc))

        def copy(j, peer, slot):
            return pltpu.make_async_remote_copy(
                src_ref=src_ref, dst_ref=gat_ref.at[slot], send_sem=send_sems.at[j], recv_sem=recv_sems.at[j],
                device_id=peer, device_id_type=MESH)

        sends = [copy(j, peer, me) for j, peer in enumerate(peers)]
        for cp in sends:
            cp.start()
        for j, (px, py, pc) in enumerate(peers):
            copy(j, (px, py, pc), 4 * px + 2 * py + pc).wait_recv()
        for cp in sends:
            cp.wait_send()
        if reduce:
            total = gat_ref[0]
            for d in range(1, n_dev):
                total = total + gat_ref[d]
            out_ref[...] = total

    out_shape = jax.ShapeDtypeStruct((r, c) if reduce else (n_dev, r, c), src.dtype)
    scratch = ([pltpu.VMEM((n_dev, r, c), src.dtype)] if reduce else []) + _dma_sems(n_dev - 1)
    return pl.pallas_call(kern, out_shape=out_shape, in_specs=[VMEM_SPEC], out_specs=VMEM_SPEC,
                          scratch_shapes=scratch, name=name)(src)


def _halves(t, axis):
    return t.reshape(t.shape[:axis] + (2, t.shape[axis] // 2) + t.shape[axis + 1:])


def _by_core(mine, theirs, axis):
    c = lax.axis_index("c")
    both = jnp.where(c == 0, jnp.stack([mine, theirs], axis), jnp.stack([theirs, mine], axis))
    return both.reshape(both.shape[:axis] + (2 * both.shape[axis + 1],) + both.shape[axis + 2:])


def _chip_order(own, others):
    me = 2 * lax.axis_index("x") + lax.axis_index("y")
    cands = jnp.concatenate([own[None], others], axis=0)
    slot_of_flip = (0, 2, 1, 3)
    pick = jnp.asarray(slot_of_flip, jnp.int32)[jnp.arange(N_CHIPS, dtype=jnp.int32) ^ me]
    return jnp.stack([lax.dynamic_index_in_dim(cands, pick[k], 0, keepdims=False) for k in range(N_CHIPS)])


def _join_shards(name, shards):
    _, r, c = shards.shape
    if name in COL_SHARDED:
        return shards.transpose(1, 0, 2).reshape(r, N_CHIPS * c)
    return shards.reshape(N_CHIPS * r, c)


def _split_shards(name, full):
    if full.ndim == 3:
        return full
    r, c = full.shape
    if name in COL_SHARDED:
        return jnp.stack([full[:, k * (c // N_CHIPS):(k + 1) * (c // N_CHIPS)] for k in range(N_CHIPS)])
    return full.reshape(N_CHIPS, r // N_CHIPS, c)


def _rope_tables(positions):
    pos = positions.reshape(-1).astype(F32)[:, None]
    half = RET_DK // 2
    ang = pos * (ROPE_BASE ** (-jnp.arange(half, dtype=F32) / half))
    cos_r = jnp.concatenate([jnp.cos(ang)] * 2, axis=1)
    sin_r = jnp.concatenate([-jnp.sin(ang), jnp.sin(ang)], axis=1)
    half = MLA_ROPE // 2
    ang = pos * (ROPE_BASE ** (-jnp.arange(half, dtype=F32) / half))
    zeros = jnp.zeros_like(ang)
    rest = LANES - MLA_ROPE
    c = jnp.concatenate([jnp.cos(ang)] * 2 + [jnp.ones((ang.shape[0], rest), F32)], axis=1)
    s1 = jnp.concatenate([-jnp.sin(ang), zeros, jnp.zeros((ang.shape[0], rest), F32)], axis=1)
    s2 = jnp.concatenate([zeros, jnp.sin(ang), jnp.zeros((ang.shape[0], rest), F32)], axis=1)
    return cos_r, sin_r, (c, s1, s2)


GATHER_GROUPS = (("ffn1_w_in", "ffn1_w_out"), ("w_in", "w_uq", "w_ukv"),
                 ("w_ret_o", "w_mla_o", "w_out", "ffn2_w_in", "ffn2_w_out", "ple_w_gate", "ple_w_proj"))
REDUCE_GROUPS = (("ple_w_gate", "ple_w_proj", "ffn2_w_in", "ffn2_w_out"),
                 ("w_out", "w_ret_o", "w_mla_o", "w_uq", "w_ukv", "w_in"), ("ffn1_w_in", "ffn1_w_out"))


def _gathered(tag, names, own, mine):
    theirs = _sibling_swap("gather_cores_" + tag, mine, False)
    out = {}
    for n, m, t in zip(names, mine, theirs):
        full = _chip_order(own[n], _by_core(m, t, 1))
        out[n] = full if n in ("ffn1_w_in", "ffn2_w_in") else _join_shards(n, full)
    return out


def _chip_sums(tag, names, grads):
    c = lax.axis_index("c")
    halves = [_halves(_split_shards(n, grads[n]), 1) for n in names]
    theirs = _sibling_swap("reduce_cores_" + tag, halves, True)
    sums = []
    for n, g, t in zip(names, halves, theirs):
        mine = lax.dynamic_index_in_dim(g, c, axis=1, keepdims=False)
        k, r, cc = mine.shape
        sums.append(_ewise("reduce_cores_add_" + n, lambda a, b: (a.astype(F32) + b.astype(F32),),
                           [mine.reshape(k * r, cc), t.reshape(k * r, cc)], 1, BF16)[0].reshape(k, r, cc))
    return sums


def _block_totals(names, sums, parts):
    me = 2 * lax.axis_index("x") + lax.axis_index("y")
    totals = []
    for n, s, pt in zip(names, sums, parts):
        own = lax.dynamic_index_in_dim(s, me, axis=0, keepdims=False)
        totals.append(_ewise("reduce_chips_add_" + n,
                             lambda a, b, c_, d: (((a.astype(F32) + b.astype(F32)) + c_.astype(F32)) + d.astype(F32),),
                             [own, pt[0], pt[1], pt[2]], 1, F32)[0])
    return totals


def _local_step(x, p, positions, target, shards, ln_g, ln_b, gn_g, qn_g, kvn_g):
    T, D = x.shape
    tm = min(256, T)
    H = MLA_HEADS
    qk, rv = RET_HEADS * RET_DK, RET_HEADS * RET_DV
    cos_r, sin_r, tabs = _rope_tables(positions)
    lgam = jnp.broadcast_to(jnp.log(1.0 - 2.0 ** (-5.0 - jnp.arange(RET_HEADS, dtype=F32)))[:, None, None],
                            (RET_HEADS, 1, LANES))
    lng = [ln_g[k:k + 1] for k in range(N_LN)]
    lnb = [ln_b[k:k + 1] for k in range(N_LN)]
    own = {n: _bf(shards[n]) for n in BIG_WEIGHTS}
    to_send = [[_halves(own[n], 0) for n in names] for names in GATHER_GROUPS]

    w = _gathered("a", GATHER_GROUPS[0], own, _chips_exchange("gather_chips_a", to_send[0], True))
    h1, z0, a1, *arrived = _ffn_fwd("ffn1_fwd", x, w["ffn1_w_in"], w["ffn1_w_out"], lng[0], lnb[0], tm,
                                    exchange=(to_send[1], True))
    w.update(_gathered("b", GATHER_GROUPS[1], own, arrived))

    w_in = w["w_in"]
    o_lat, o_kpe, o_gate = 2 * qk + 2 * rv, 2 * qk + 2 * rv + Q_LORA + KV_LORA, 2 * qk + 2 * rv + Q_LORA + KV_LORA + MLA_ROPE
    w_r, w_c = w_in[:, :o_lat], w_in[:, o_lat:o_kpe]
    w_kpe = jnp.pad(w_in[:, o_kpe:o_gate], ((0, 0), (0, LANES - MLA_ROPE)))
    w_g = w_in[:, o_gate:]
    w_uq = jnp.pad(w["w_uq"].reshape(Q_LORA, H, MLA_NOPE + MLA_ROPE),
                   ((0, 0), (0, 0), (0, MLA_QK - MLA_NOPE - MLA_ROPE))).reshape(Q_LORA, H * MLA_QK)
    w_ukv = w["w_ukv"].reshape(KV_LORA, H, MLA_NOPE + MLA_DV)
    w_uk = w_ukv[:, :, :MLA_NOPE].reshape(KV_LORA, H * MLA_NOPE)
    w_uv = w_ukv[:, :, MLA_NOPE:].reshape(KV_LORA, H * MLA_DV)

    rq, rk, rvv, rg = _proj_ret(h1, w_r, cos_r, sin_r, tm)
    lat, gates, q, k, v, latn, qt, kt, vt = _proj_mla(h1, tabs, w_c, w_kpe, w_g, w_uq, w_uk, w_uv, qn_g, kvn_g, tm)
    y = _ret_fwd(rq, rk, rvv, lgam)
    o, lse_rows, *arrived = _attn_fwd(k, qt, vt, exchange=to_send[2])
    w.update(_gathered("c", GATHER_GROUPS[2], own, arrived))
    h2, z1, yret, ymla, yr, mix = _mix_fwd(y, rg, o, gates, h1, gn_g, w["w_ret_o"], w["w_mla_o"], w["w_out"],
                                           lng[1], lnb[1], tm)
    h3, z2, a2 = _ffn_fwd("ffn2_fwd", h2, w["ffn2_w_in"], w["ffn2_w_out"], lng[2], lnb[2], tm)

    dh3, dgp, dpp, loss, dg3, db3 = _ple_loss(h3, p, target, w["ple_w_gate"], w["ple_w_proj"], lng[3], lnb[3], tm)
    dh2, da2, s2, df2, dg2, db2 = _ffn_bwd("ffn2_bwd", dh3, z2, a2, w["ffn2_w_in"], w["ffn2_w_out"], lng[2], tm)
    grads = {"ple_w_gate": _mm_tn("wg_ple_gate", h3, dgp), "ple_w_proj": _mm_tn("wg_ple_proj", p, dpp),
             "ffn2_w_in": _mm_tn("wg_ffn2_in", h2, da2, n_split=N_CHIPS), "ffn2_w_out": _mm_tn("wg_ffn2_out", s2, df2)}
    sums1 = _chip_sums("1", REDUCE_GROUPS[0], grads)
    (dz1, dgates, drg, dy, do, dyret, dymla, dg1, db1, dgn, dot_, delta_rows, *parts1) = _mix_bwd(
        dh2, z1, gates, yret, ymla, y, rg, o, gn_g, w["w_ret_o"], w["w_mla_o"], w["w_out"], lng[1], tm,
        exchange=(sums1, False))
    drq = _ret_bwd_q(rq, rk, rvv, dy, lgam)
    drk, drv = _ret_bwd_kv(rq, rk, rvv, dy, lgam)
    dk, dv, dqt = _attn_bwd(q, k, v, do, qt, kt, dot_, lse_rows, delta_rows)
    dlat, dkpe, dqb, dkn, dqg, dkg = _proj_mla_bwd(dqt, dk, dv, lat, tabs, w_uq, w_uk, w_uv, qn_g, kvn_g, tm)
    dh1, dpr = _proj_bwd(drq, drk, drv, drg, dz1, dlat, dkpe, dgates, cos_r, sin_r, w_r, w_c, w_kpe, w_g, tm)
    g_uq = _mm_tn("wg_uq", latn[:, :Q_LORA], dqb).reshape(Q_LORA, H, MLA_QK)[:, :, :MLA_NOPE + MLA_ROPE]
    g_uk = _mm_tn("wg_uk", latn[:, Q_LORA:], dkn).reshape(KV_LORA, H, MLA_NOPE)
    g_uv = _mm_tn("wg_uv", latn[:, Q_LORA:], dv).reshape(KV_LORA, H, MLA_DV)
    grads.update({
        "w_in": jnp.concatenate([_mm_tn("wg_in_r", h1, dpr), _mm_tn("wg_in_c", h1, dlat),
                                 _mm_tn("wg_in_kpe", h1, dkpe)[:, :MLA_ROPE], _mm_tn("wg_in_g", h1, dgates)], axis=1),
        "w_ret_o": _mm_tn("wg_ret_o", yr, dyret),
        "w_uq": g_uq.reshape(Q_LORA, H * (MLA_NOPE + MLA_ROPE)),
        "w_ukv": jnp.concatenate([g_uk, g_uv], axis=2).reshape(KV_LORA, H * (MLA_NOPE + MLA_DV)),
        "w_mla_o": _mm_tn("wg_mla_o", o, dymla),
        "w_out": _mm_tn("wg_out", mix, dz1)})
    sums2 = _chip_sums("2", REDUCE_GROUPS[1], grads)
    dx, da1, s1, df1, dg0, db0, *parts2 = _ffn_bwd("ffn1_bwd", dh1, z0, a1, w["ffn1_w_in"], w["ffn1_w_out"], lng[0], tm,
                                                   exchange=(sums2, False))
    grads.update({"ffn1_w_in": _mm_tn("wg_ffn1_in", x, da1, n_split=N_CHIPS),
                  "ffn1_w_out": _mm_tn("wg_ffn1_out", s1, df1)})
    sums3 = _chip_sums("3", REDUCE_GROUPS[2], grads)
    parts3 = _chips_exchange("reduce_chips_3", sums3, False)

    names = [n for group in REDUCE_GROUPS for n in group]
    totals = _block_totals(names, sums1 + sums2 + sums3, list(parts1) + list(parts2) + list(parts3))
    others = _sibling_swap("reduce_join", totals, False)
    reduced = {n: _by_core(t, o_, 0) for n, t, o_ in zip(names, totals, others)}
    small = {"ln_g": jnp.concatenate([dg0, dg1, dg2, dg3], axis=0), "ln_b": jnp.concatenate([db0, db1, db2, db3], axis=0),
             "ret_gn_g": dgn, "q_norm_g": dqg, "kv_norm_g": dkg}
    return loss[0, 0], dx, reduced, small


def kernel(x, p, positions, ln_g, ln_b, ffn1_w_in, ffn1_w_out, w_in, ret_gn_g, w_ret_o, q_norm_g, kv_norm_g, w_uq, w_ukv, w_mla_o, w_out, ffn2_w_in, ffn2_w_out, ple_w_gate, ple_w_proj, loss_target, m_ln_g, m_ln_b, m_ffn1_w_in, m_ffn1_w_out, m_w_in, m_ret_gn_g, m_w_ret_o, m_q_norm_g, m_kv_norm_g, m_w_uq, m_w_ukv, m_w_mla_o, m_w_out, m_ffn2_w_in, m_ffn2_w_out, m_ple_w_gate, m_ple_w_proj, v_ln_g, v_ln_b, v_ffn1_w_in, v_ffn1_w_out, v_w_in, v_ret_gn_g, v_w_ret_o, v_q_norm_g, v_kv_norm_g, v_w_uq, v_w_ukv, v_w_mla_o, v_w_out, v_ffn2_w_in, v_ffn2_w_out, v_ple_w_gate, v_ple_w_proj):
    names = ("ln_g", "ln_b", "ffn1_w_in", "ffn1_w_out", "w_in", "ret_gn_g", "w_ret_o", "q_norm_g", "kv_norm_g", "w_uq",
             "w_ukv", "w_mla_o", "w_out", "ffn2_w_in", "ffn2_w_out", "ple_w_gate", "ple_w_proj")
    weights = dict(zip(names, (ln_g, ln_b, ffn1_w_in, ffn1_w_out, w_in, ret_gn_g, w_ret_o, q_norm_g, kv_norm_g, w_uq,
                               w_ukv, w_mla_o, w_out, ffn2_w_in, ffn2_w_out, ple_w_gate, ple_w_proj)))
    m_in = dict(zip(names, (m_ln_g, m_ln_b, m_ffn1_w_in, m_ffn1_w_out, m_w_in, m_ret_gn_g, m_w_ret_o, m_q_norm_g,
                            m_kv_norm_g, m_w_uq, m_w_ukv, m_w_mla_o, m_w_out, m_ffn2_w_in, m_ffn2_w_out, m_ple_w_gate,
                            m_ple_w_proj)))
    v_in = dict(zip(names, (v_ln_g, v_ln_b, v_ffn1_w_in, v_ffn1_w_out, v_w_in, v_ret_gn_g, v_w_ret_o, v_q_norm_g,
                            v_kv_norm_g, v_w_uq, v_w_ukv, v_w_mla_o, v_w_out, v_ffn2_w_in, v_ffn2_w_out, v_ple_w_gate,
                            v_ple_w_proj)))
    chip = 2 * lax.axis_index("x") + lax.axis_index("y")
    D = x.shape[-1]
    dq = D // N_CHIPS

    shards = {n: weights[n][0] for n in BIG_WEIGHTS}
    ln_all = _all_devices("gather_ln", jnp.concatenate([ln_g[0], ln_b[0]], axis=0), False)
    ln_full = ln_all[::2].transpose(1, 0, 2).reshape(2 * N_LN, D)
    loss, dx, big, small = _local_step(x[0], p[0, 0], positions, loss_target[0], shards, ln_full[:N_LN],
                                       ln_full[N_LN:], ret_gn_g, q_norm_g, kv_norm_g)

    loss = lax.psum(loss, ("x", "y", "c"))
    small_names = ("ln_g", "ln_b", "ret_gn_g", "q_norm_g", "kv_norm_g")
    flat = jnp.concatenate([small[n].reshape(-1) for n in small_names])
    rows = -(-flat.shape[0] // LANES // 8) * 8
    flat = jnp.pad(flat, (0, rows * LANES - flat.shape[0])).reshape(rows, LANES)
    flat = _all_devices("reduce_small", flat, True).reshape(-1)
    off = 0
    for n in small_names:
        size = small[n].size
        small[n] = flat[off:off + size].reshape(small[n].shape)
        off += size
    g_out = dict(big)
    for n in ("ln_g", "ln_b"):
        g_out[n] = lax.dynamic_slice_in_dim(small[n], chip * dq, dq, axis=1)
    for n in ("ret_gn_g", "q_norm_g", "kv_norm_g"):
        g_out[n] = small[n]

    deltas, new_m, new_v = {}, {}, {}
    for n in names:
        g = g_out[n].reshape(weights[n].shape)
        g_out[n] = g
        deltas[n], new_m[n], new_v[n] = _adamw("adamw_" + n, weights[n], g, m_in[n], v_in[n])
    return (loss, dx[None], *[g_out[n] for n in names], *[deltas[n] for n in names], *[new_m[n] for n in names],
            *[new_v[n] for n in names])
```

```python
import functools

import jax
import jax.numpy as jnp
from jax import lax
from jax.experimental import pallas as pl
from jax.experimental.pallas import tpu as pltpu

D_MODEL = 1024
CHUNK = 64
D_PLE = 256
D_FF = 2816
RET_HEADS = 8
RET_DK = 128
RET_DV = 256
MLA_HEADS = 8
MLA_NOPE = 128
MLA_ROPE = 64
MLA_DV = 128
MLA_QK = 256
Q_LORA = 256
KV_LORA = 256
ROPE_BASE = 10000.0
EPS = 1e-5
N_LN = 4
ALPHA = 2.0 ** 0.25
ADAM_LR = 0.001
ADAM_B1 = 0.9
ADAM_B2 = 0.999
ADAM_EPS = 1e-08
ADAM_WD = 0.01
ADAM_STEP = 10

LANES = 128
VMEM_LIMIT = 60 << 20
N_CHIPS = 4

F32 = jnp.float32
BF16 = jnp.bfloat16
MESH = pl.DeviceIdType.MESH
HBM_SPEC = pl.BlockSpec(memory_space=pltpu.HBM)
VMEM_SPEC = pl.BlockSpec(memory_space=pltpu.VMEM)

BIG_WEIGHTS = ("ffn1_w_in", "ffn1_w_out", "w_in", "w_ret_o", "w_uq", "w_ukv", "w_mla_o", "w_out",
               "ffn2_w_in", "ffn2_w_out", "ple_w_gate", "ple_w_proj")
COL_SHARDED = ("ffn1_w_in", "w_in", "w_uq", "w_ukv", "ffn2_w_in", "ple_w_proj")


def _dot(a, b):
    return jnp.dot(a, b, preferred_element_type=F32)


def _dot_nt(a, b):
    return lax.dot_general(a, b, (((1,), (1,)), ((), ())), preferred_element_type=F32)


def _dot_tn(a, b):
    return lax.dot_general(a, b, (((0,), (0,)), ((), ())), preferred_element_type=F32)


def _bf(x):
    return x.astype(BF16)


def _sigmoid(x):
    return 1.0 / (1.0 + jnp.exp(-x))


def _mean(x):
    return jnp.mean(x, axis=-1, keepdims=True)


def _ln_stats(z):
    zc = z - _mean(z)
    rstd = lax.rsqrt(_mean(zc * zc) + EPS)
    return zc * rstd, rstd


def _ln_bwd(dy, xhat, rstd, g):
    dxhat = dy * g
    dz = rstd * (dxhat - _mean(dxhat) - xhat * _mean(dxhat * xhat))
    return dz, jnp.sum(dy * xhat, axis=0, keepdims=True), jnp.sum(dy, axis=0, keepdims=True)


def _roll(x, shift):
    return pltpu.roll(x, shift, 1)


def _chunk_of(idx):
    return jnp.right_shift(idx, CHUNK.bit_length() - 1)


def _tile(n, cap, mult=LANES):
    if n <= cap:
        return n
    for t in range(cap - cap % mult, 0, -mult):
        if n % t == 0:
            return t
    return n


def _zero_map(nd, *_):
    return (0,) * nd


def _params(sem):
    return pltpu.CompilerParams(dimension_semantics=sem, vmem_limit_bytes=VMEM_LIMIT)


def _rowcall(name, body, n_rows, tm, row_ins, full_ins, row_outs, acc_outs=(), tiled_outs=(), tiled_ins=(),
             exchange=None):
    n_steps = n_rows // tm
    ex_srcs, broadcast = exchange if exchange else ((), False)
    n_ex = len(ex_srcs)
    n_in = len(row_ins) + len(tiled_ins) + len(full_ins)
    n_out = len(row_outs) + len(acc_outs) + len(tiled_outs)

    def kern(*refs):
        step = pl.program_id(0)
        ex_in, ex_out = refs[n_in:n_in + n_ex], refs[n_in + n_ex + n_out:n_in + 2 * n_ex + n_out]
        sems = refs[n_in + 2 * n_ex + n_out:]
        if n_ex:
            @pl.when(step == 0)
            def _():
                for cp in _chip_copies(ex_in, ex_out, *sems, broadcast):
                    cp.start()

        body(step, *refs[:n_in], *refs[n_in + n_ex:n_in + n_ex + n_out])
        if n_ex:
            @pl.when(step == n_steps - 1)
            def _():
                _wait_copies(_chip_copies(ex_in, ex_out, *sems, broadcast))

    in_specs = [pl.BlockSpec((tm, a.shape[1]), lambda i: (i, 0)) for a in row_ins]
    in_specs += [spec for (_, spec) in tiled_ins]
    row_ins = list(row_ins) + [a for (a, _) in tiled_ins]
    in_specs += [pl.BlockSpec(a.shape, functools.partial(_zero_map, a.ndim), pipeline_mode=pl.Buffered(1))
                 for a in full_ins]
    in_specs += [HBM_SPEC] * n_ex
    out_specs = [pl.BlockSpec((tm, w), lambda i: (i, 0)) for (w, _) in row_outs]
    out_specs += [pl.BlockSpec(s, functools.partial(_zero_map, len(s))) for (s, _) in acc_outs]
    out_specs += [spec for (_, spec) in tiled_outs]
    out_specs += [HBM_SPEC] * n_ex
    out_shape = [jax.ShapeDtypeStruct((n_rows, w), dt) for (w, dt) in row_outs]
    out_shape += [jax.ShapeDtypeStruct(s, dt) for (s, dt) in acc_outs]
    out_shape += [shape for (shape, _) in tiled_outs]
    out_shape += _exchange_shapes(ex_srcs)
    return pl.pallas_call(kern, grid=(n_steps,), in_specs=in_specs, out_specs=out_specs, out_shape=out_shape,
                          scratch_shapes=_dma_sems(n_ex * N_PEER_CHIPS) if n_ex else [], name=name,
                          compiler_params=_params(("arbitrary",)))(*row_ins, *full_ins, *ex_srcs)


def _acc(step, ref, val):
    @pl.when(step == 0)
    def _():
        ref[...] = val

    @pl.when(step != 0)
    def _():
        ref[...] += val


def _ffn_fwd(name, x, w_in4, w_out, ln_g, ln_b, tm, exchange=None):
    T, D = x.shape
    fh = w_in4.shape[2]

    def body(i, x_ref, w4_ref, wo_ref, g_ref, b_ref, h_ref, z_ref, a_ref):
        xv = x_ref[...]
        xb = _bf(xv)
        f = jnp.zeros((tm, D), F32)
        for k in range(2):
            gk = _dot(xb, w4_ref[k])
            uk = _dot(xb, w4_ref[2 + k])
            a_ref[:, k * fh:(k + 1) * fh] = _bf(gk)
            a_ref[:, (2 + k) * fh:(3 + k) * fh] = _bf(uk)
            f += _dot(_bf(gk * _sigmoid(gk) * uk), wo_ref[k * fh:(k + 1) * fh, :])
        z = ALPHA * xv + 0.5 * f
        xhat, _ = _ln_stats(z)
        z_ref[...] = z
        h_ref[...] = xhat * g_ref[...] + b_ref[...]

    return _rowcall(name, body, T, tm, [x], [w_in4, w_out, ln_g, ln_b],
                    [(D, F32), (D, F32), (4 * fh, BF16)], exchange=exchange)


def _ffn_bwd(name, dh, z, a, w_in4, w_out, ln_g, tm, exchange=None):
    T, D = dh.shape
    fh = w_in4.shape[2]

    def body(i, dh_ref, z_ref, a_ref, w4_ref, wo_ref, g_ref, dx_ref, da_ref, s_ref, df_ref, dg_ref, db_ref):
        xhat, rstd = _ln_stats(z_ref[...])
        dz, dg, db = _ln_bwd(dh_ref[...], xhat, rstd, g_ref[...])
        _acc(i, dg_ref, dg)
        _acc(i, db_ref, db)
        dfb = _bf(0.5 * dz)
        df_ref[...] = dfb
        dx = ALPHA * dz
        for k in range(2):
            gk = a_ref[:, k * fh:(k + 1) * fh].astype(F32)
            uk = a_ref[:, (2 + k) * fh:(3 + k) * fh].astype(F32)
            ds = _dot_nt(dfb, wo_ref[k * fh:(k + 1) * fh, :])
            sig = _sigmoid(gk)
            silu = gk * sig
            dgk = _bf(ds * uk * sig * (1.0 + gk * (1.0 - sig)))
            duk = _bf(ds * silu)
            s_ref[:, k * fh:(k + 1) * fh] = _bf(silu * uk)
            da_ref[:, k * fh:(k + 1) * fh] = dgk
            da_ref[:, (2 + k) * fh:(3 + k) * fh] = duk
            dx += _dot_nt(dgk, w4_ref[k]) + _dot_nt(duk, w4_ref[2 + k])
        dx_ref[...] = dx

    return _rowcall(name, body, T, tm, [dh, z, a], [w_in4, w_out, ln_g],
                    [(D, F32), (4 * fh, BF16), (2 * fh, BF16), (D, BF16)],
                    [((1, D), F32), ((1, D), F32)], exchange=exchange)


def _mm_tn(name, a, b, out_dtype=BF16, n_split=1):
    T, M = a.shape
    N = b.shape[1]
    tk = _tile(T, 2048, 8)
    tm = _tile(M, 1408)
    tn = _tile(N // n_split, 1536)
    per = N // n_split // tn
    nk = T // tk
    if n_split > 1:
        out_spec = pl.BlockSpec((None, tm, tn), lambda i, j, k: (j // per, i, j % per))
        out_shape = jax.ShapeDtypeStruct((n_split, M, N // n_split), out_dtype)
    else:
        out_spec = pl.BlockSpec((tm, tn), lambda i, j, k: (i, j))
        out_shape = jax.ShapeDtypeStruct((M, N), out_dtype)

    def kern(a_ref, b_ref, o_ref, acc_ref):
        k = pl.program_id(2)
        part = _dot_tn(_bf(a_ref[...]), _bf(b_ref[...]))

        @pl.when(k == 0)
        def _():
            acc_ref[...] = part

        @pl.when(k != 0)
        def _():
            acc_ref[...] += part

        @pl.when(k == nk - 1)
        def _():
            o_ref[...] = acc_ref[...].astype(out_dtype)

    return pl.pallas_call(
        kern, grid=(M // tm, N // tn, nk),
        in_specs=[pl.BlockSpec((tk, tm), lambda i, j, k: (k, i)), pl.BlockSpec((tk, tn), lambda i, j, k: (k, j))],
        out_specs=out_spec, out_shape=out_shape,
        scratch_shapes=[pltpu.VMEM((tm, tn), F32)], name=name,
        compiler_params=_params(("arbitrary", "arbitrary", "arbitrary")))(a, b)


def _proj_ret(h1, w_r, cos_r, sin_r, tm):
    T, D = h1.shape
    qk = RET_HEADS * RET_DK
    rv = RET_HEADS * RET_DV

    def body(i, h_ref, cos_ref, sin_ref, w_ref, q_ref, k_ref, v_ref, g_ref):
        hb = _bf(h_ref[...])
        cos, sin = cos_ref[...], sin_ref[...]
        for out_ref, off, scale in ((q_ref, 0, 1.0), (k_ref, qk, RET_DK ** -0.5)):
            pr = _dot(hb, w_ref[:, off:off + qk])
            for h in range(RET_HEADS):
                t = pr[:, h * RET_DK:(h + 1) * RET_DK]
                out_ref[:, h * RET_DK:(h + 1) * RET_DK] = _bf((t * cos + _roll(t, RET_DK // 2) * sin) * scale)
        v_ref[...] = _bf(_dot(hb, w_ref[:, 2 * qk:2 * qk + rv]))
        g_ref[...] = _dot(hb, w_ref[:, 2 * qk + rv:2 * qk + 2 * rv])

    return _rowcall("proj_ret", body, T, tm, [h1, cos_r, sin_r], [w_r],
                    [(qk, BF16), (qk, BF16), (rv, BF16), (rv, F32)])


def _rope_pe(t, c, s1, s2):
    return t * c + _roll(t, LANES - MLA_ROPE // 2) * s1 + _roll(t, MLA_ROPE // 2) * s2


def _rope_pe_bwd(dy, c, s1, s2):
    return dy * c + _roll(dy * s1, MLA_ROPE // 2) + _roll(dy * s2, LANES - MLA_ROPE // 2)


def _rms(x, g):
    r = lax.rsqrt(_mean(x * x) + EPS)
    return x * r, r


def _attn_block(T):
    return min(512, T)


def _transposed_blocks(T, tm, w, dtype):
    tb = _attn_block(T)
    per = tb // tm
    return (jax.ShapeDtypeStruct((T // tb, MLA_HEADS, w, tb), dtype),
            pl.BlockSpec((None, MLA_HEADS, w, tm), lambda i: (i // per, 0, 0, i % per)))


def _proj_mla(h1, tabs, w_c, w_kpe, w_g, w_uq, w_uk, w_uv, qn_g, kvn_g, tm):
    T, D = h1.shape
    H = MLA_HEADS

    def body(i, h_ref, c_ref, s1_ref, s2_ref, wc_ref, wk_ref, wg_ref, wuq_ref, wuk_ref, wuv_ref, qg_ref, kg_ref,
             lat_ref, gt_ref, q_ref, k_ref, v_ref, ln_ref, qt_ref, kt_ref, vt_ref):
        hb = _bf(h_ref[...])
        c, s1, s2 = c_ref[...], s1_ref[...], s2_ref[...]
        lat = _dot(hb, wc_ref[...])
        lat_ref[...] = lat
        gt_ref[...] = _dot(hb, wg_ref[...])
        cqn, _ = _rms(lat[:, :Q_LORA], None)
        ckn, _ = _rms(lat[:, Q_LORA:], None)
        cqn = _bf(cqn * qg_ref[...])
        ckn = _bf(ckn * kg_ref[...])
        ln_ref[:, :Q_LORA] = cqn
        ln_ref[:, Q_LORA:] = ckn
        q = _dot(cqn, wuq_ref[...])
        kn = _dot(ckn, wuk_ref[...])
        vv = _dot(ckn, wuv_ref[...])
        v_ref[...] = _bf(vv)
        kpe = _rope_pe(_dot(hb, wk_ref[...]), c, s1, s2)
        for h in range(H):
            o = h * MLA_QK
            qh = jnp.concatenate([q[:, o:o + MLA_NOPE], _rope_pe(q[:, o + MLA_NOPE:o + MLA_QK], c, s1, s2)], axis=1)
            kh = jnp.concatenate([kn[:, h * MLA_NOPE:(h + 1) * MLA_NOPE], kpe], axis=1)
            q_ref[:, o:o + MLA_QK] = _bf(qh)
            k_ref[:, o:o + MLA_QK] = _bf(kh)
            qt_ref[h] = _bf(qh.T)
            kt_ref[h] = _bf(kh.T)
            vt_ref[h] = _bf(vv[:, h * MLA_DV:(h + 1) * MLA_DV].T)

    lat_w = Q_LORA + KV_LORA
    return _rowcall("proj_mla", body, T, tm, [h1, *tabs], [w_c, w_kpe, w_g, w_uq, w_uk, w_uv, qn_g, kvn_g],
                    [(lat_w, F32), (2 * D, F32), (H * MLA_QK, BF16), (H * MLA_QK, BF16), (H * MLA_DV, BF16),
                     (lat_w, BF16)],
                    tiled_outs=[_transposed_blocks(T, tm, MLA_QK, BF16), _transposed_blocks(T, tm, MLA_QK, BF16),
                                _transposed_blocks(T, tm, MLA_DV, BF16)])


def _ret_block(T):
    return min(256, T)


def _ret_decay(lg, bt):
    n = lax.broadcasted_iota(jnp.int32, (bt, bt), 0)
    m = lax.broadcasted_iota(jnp.int32, (bt, bt), 1)
    dmat = jnp.where(_chunk_of(m) <= _chunk_of(n), jnp.exp(lg * jnp.abs(n - m).astype(F32)), 0.0)
    pos = lax.broadcasted_iota(jnp.int32, (bt, 1), 0).astype(F32)
    xi = jnp.exp(lg * (pos + 1.0))
    zeta = jnp.exp(lg * (bt - 1.0 - pos))
    return dmat, xi, zeta, jnp.exp(lg * bt)


def _ret_specs(bt, rev, nb):
    def blk(w):
        if rev:
            return pl.BlockSpec((bt, w), lambda h, b: (nb - 1 - b, h))
        return pl.BlockSpec((bt, w), lambda h, b: (b, h))
    return pl.BlockSpec((None, 1, LANES), lambda h, b: (h, 0, 0)), blk


def _ret_fwd(rq, rk, rv, lgam):
    T = rq.shape[0]
    bt = _ret_block(T)
    nb = T // bt
    lg_spec, blk = _ret_specs(bt, False, nb)

    def kern(lg_ref, q_ref, k_ref, v_ref, y_ref, s_ref):
        @pl.when(pl.program_id(1) == 0)
        def _():
            s_ref[...] = jnp.zeros_like(s_ref)

        dmat, xi, zeta, gb = _ret_decay(lg_ref[:, :1], bt)
        q, k, v = q_ref[...], k_ref[...], v_ref[...]
        sc = _dot_nt(q, k) * dmat
        y_ref[...] = _dot(_bf(sc), v) + _dot(q, _bf(s_ref[...])) * xi
        s_ref[...] = s_ref[...] * gb + _dot_tn(_bf(k.astype(F32) * zeta), v)

    return pl.pallas_call(
        kern, grid=(RET_HEADS, nb), in_specs=[lg_spec, blk(RET_DK), blk(RET_DK), blk(RET_DV)],
        out_specs=blk(RET_DV), out_shape=jax.ShapeDtypeStruct((T, RET_HEADS * RET_DV), F32),
        scratch_shapes=[pltpu.VMEM((RET_DK, RET_DV), F32)], name="ret_fwd",
        compiler_params=_params(("arbitrary", "arbitrary")))(lgam, rq, rk, rv)


def _ret_bwd_q(rq, rk, rv, dy, lgam):
    T = rq.shape[0]
    bt = _ret_block(T)
    nb = T // bt
    lg_spec, blk = _ret_specs(bt, False, nb)

    def kern(lg_ref, k_ref, v_ref, dy_ref, dq_ref, s_ref):
        @pl.when(pl.program_id(1) == 0)
        def _():
            s_ref[...] = jnp.zeros_like(s_ref)

        dmat, xi, zeta, gb = _ret_decay(lg_ref[:, :1], bt)
        k, v, dy = k_ref[...], v_ref[...], dy_ref[...]
        dp = _dot_nt(dy, v) * dmat
        dq_ref[...] = _dot(_bf(dp), k) + _dot_nt(dy, _bf(s_ref[...])) * xi
        s_ref[...] = s_ref[...] * gb + _dot_tn(_bf(k.astype(F32) * zeta), v)

    return pl.pallas_call(
        kern, grid=(RET_HEADS, nb), in_specs=[lg_spec, blk(RET_DK), blk(RET_DV), blk(RET_DV)],
        out_specs=blk(RET_DK), out_shape=jax.ShapeDtypeStruct((T, RET_HEADS * RET_DK), F32),
        scratch_shapes=[pltpu.VMEM((RET_DK, RET_DV), F32)], name="ret_bwd_q",
        compiler_params=_params(("arbitrary", "arbitrary")))(lgam, rk, rv, dy)


def _ret_bwd_kv(rq, rk, rv, dy, lgam):
    T = rq.shape[0]
    bt = _ret_block(T)
    nb = T // bt
    lg_spec, blk = _ret_specs(bt, True, nb)

    def kern(lg_ref, q_ref, k_ref, v_ref, dy_ref, dk_ref, dv_ref, g_ref):
        @pl.when(pl.program_id(1) == 0)
        def _():
            g_ref[...] = jnp.zeros_like(g_ref)

        dmat, xi, zeta, gb = _ret_decay(lg_ref[:, :1], bt)
        q, k, v, dy = q_ref[...], k_ref[...], v_ref[...], dy_ref[...]
        gs = _bf(g_ref[...])
        p = _dot_nt(q, k) * dmat
        dp = _dot_nt(dy, v) * dmat
        dv_ref[...] = _bf(_dot_tn(_bf(p), dy) + _dot(k, gs) * zeta)
        dk_ref[...] = _dot_tn(_bf(dp), q) + _dot_nt(v, gs) * zeta
        g_ref[...] = g_ref[...] * gb + _dot_tn(_bf(q.astype(F32) * xi), dy)

    return pl.pallas_call(
        kern, grid=(RET_HEADS, nb), in_specs=[lg_spec, blk(RET_DK), blk(RET_DK), blk(RET_DV), blk(RET_DV)],
        out_specs=[blk(RET_DK), blk(RET_DV)],
        out_shape=[jax.ShapeDtypeStruct((T, RET_HEADS * RET_DK), F32),
                   jax.ShapeDtypeStruct((T, RET_HEADS * RET_DV), BF16)],
        scratch_shapes=[pltpu.VMEM((RET_DK, RET_DV), F32)], name="ret_bwd_kv",
        compiler_params=_params(("arbitrary", "arbitrary")))(lgam, rq, rk, rv, dy)


def _attn_mask(tb):
    r = lax.broadcasted_iota(jnp.int32, (tb, tb), 0)
    c = lax.broadcasted_iota(jnp.int32, (tb, tb), 1)
    return _chunk_of(c) <= _chunk_of(r)


def _attn_mask_t(tb):
    key = lax.broadcasted_iota(jnp.int32, (tb, tb), 0)
    qry = lax.broadcasted_iota(jnp.int32, (tb, tb), 1)
    return _chunk_of(key) <= _chunk_of(qry)


ATTN_SCALE = (MLA_NOPE + MLA_ROPE) ** -0.5
MASKED = -1e30
LOG2E = 1.4426950408889634
SUBLANES = 8


def _head_blocks(nb, w, tb):
    return pl.BlockSpec((nb, None, w, tb), lambda h, i: (0, h, 0, 0))


def _one_block(w, tb):
    return pl.BlockSpec((None, None, w, tb), lambda h, i: (i, h, 0, 0))


def _attn_fwd(k, qt, vt, exchange=()):
    T = k.shape[0]
    tb = _attn_block(T)
    nb = T // tb

    n_ex = len(exchange)

    def kern(qt_ref, k_ref, vt_ref, *refs):
        ex_in, (o_ref, lser_ref), ex_out = refs[:n_ex], refs[n_ex:n_ex + 2], refs[n_ex + 2:2 * n_ex + 2]
        m_ref, l_ref, acc_ref, sa_ref, sb_ref = refs[2 * n_ex + 2:2 * n_ex + 7]
        sems = refs[2 * n_ex + 7:]
        qb = pl.program_id(1)
        first = jnp.logical_and(pl.program_id(0) == 0, qb == 0)
        last = jnp.logical_and(pl.program_id(0) == MLA_HEADS - 1, qb == nb - 1)
        if n_ex:
            @pl.when(first)
            def _():
                for cp in _chip_copies(ex_in, ex_out, *sems, True):
                    cp.start()

        qt = qt_ref[...]
        m_ref[...] = jnp.full_like(m_ref, MASKED)
        l_ref[...] = jnp.zeros_like(l_ref)
        acc_ref[...] = jnp.zeros_like(acc_ref)

        def scores(kb):
            rows = pl.ds(pl.multiple_of(kb * tb, tb), tb)
            return _dot(k_ref[rows, :], qt) * (ATTN_SCALE * LOG2E)

        def update(s, kb):
            m_old = m_ref[...]
            m_new = jnp.maximum(m_old, jnp.max(s, axis=0, keepdims=True))
            p = jnp.exp2(s - m_new)
            corr = jnp.exp2(m_old - m_new)
            l_ref[...] = l_ref[...] * corr + jnp.sum(p, axis=0, keepdims=True)
            acc_ref[...] = acc_ref[...] * corr + _dot(vt_ref[kb], _bf(p))
            m_ref[...] = m_new

        def masked(s):
            return jnp.where(_attn_mask_t(tb), s, MASKED)

        sa_ref[...] = scores(0)

        def pair_body(j, carry):
            sb_ref[...] = scores(2 * j + 1)
            update(sa_ref[...], 2 * j)
            sa_ref[...] = scores(2 * j + 2)
            update(sb_ref[...], 2 * j + 1)
            return carry

        lax.fori_loop(0, qb // 2, pair_body, 0)

        @pl.when(qb % 2 == 0)
        def _():
            update(masked(sa_ref[...]), qb)

        @pl.when(qb % 2 == 1)
        def _():
            sb_ref[...] = masked(scores(qb))
            update(sa_ref[...], qb - 1)
            update(sb_ref[...], qb)

        o_ref[...] = (acc_ref[...] / l_ref[...]).T
        lser_ref[...] = jnp.broadcast_to(m_ref[...] + jnp.log2(l_ref[...]), (SUBLANES, tb))
        if n_ex:
            @pl.when(last)
            def _():
                _wait_copies(_chip_copies(ex_in, ex_out, *sems, True))

    return pl.pallas_call(
        kern, grid=(MLA_HEADS, nb),
        in_specs=[_one_block(MLA_QK, tb), pl.BlockSpec((T, MLA_QK), lambda h, i: (0, h)),
                  _head_blocks(nb, MLA_DV, tb)] + [HBM_SPEC] * n_ex,
        out_specs=[pl.BlockSpec((tb, MLA_DV), lambda h, i: (i, h)), _one_block(SUBLANES, tb)] + [HBM_SPEC] * n_ex,
        out_shape=[jax.ShapeDtypeStruct((T, MLA_HEADS * MLA_DV), F32),
                   jax.ShapeDtypeStruct((nb, MLA_HEADS, SUBLANES, tb), F32)] + _exchange_shapes(exchange),
        scratch_shapes=[pltpu.VMEM((1, tb), F32), pltpu.VMEM((1, tb), F32), pltpu.VMEM((MLA_DV, tb), F32),
                        pltpu.VMEM((tb, tb), F32), pltpu.VMEM((tb, tb), F32)]
        + (_dma_sems(n_ex * N_PEER_CHIPS) if n_ex else []),
        name="attn_fwd", compiler_params=_params(("arbitrary", "arbitrary")))(qt, k, vt, *exchange)


def _attn_bwd(q, k, v, do, qt, kt, dot_, lse_rows, delta_rows):
    T = q.shape[0]
    tb = _attn_block(T)
    nb = T // tb

    def kern(q_ref, k_ref, v_ref, do_ref, qt_ref, kt_ref, dot_ref, lse_ref, dl_ref, dk_ref, dv_ref, dqt_ref, dv_acc,
             sa_ref, pa_ref, sb_ref, pb_ref):
        kb = pl.program_id(1)
        kv, vv, ktv = k_ref[...], v_ref[...], kt_ref[...]
        dk_ref[...] = jnp.zeros_like(dk_ref)
        dv_acc[...] = jnp.zeros_like(dv_acc)

        @pl.when(kb == 0)
        def _():
            dqt_ref[...] = jnp.zeros_like(dqt_ref)

        def products(qb, s_ref, dp_ref, diagonal=False):
            s = _dot(kv, qt_ref[qb]) * (ATTN_SCALE * LOG2E)
            s_ref[...] = jnp.where(_attn_mask_t(tb), s, MASKED) if diagonal else s
            dp_ref[...] = _dot(vv, dot_ref[qb])

        def consume(qb, s_ref, dp_ref):
            rows = pl.ds(pl.multiple_of(qb * tb, tb), tb)
            p = jnp.exp2(s_ref[...] - lse_ref[qb][:1, :])
            dv_acc[...] += _dot(_bf(p), do_ref[rows, :])
            ds = _bf(p * (dp_ref[...] - dl_ref[qb][:1, :]) * ATTN_SCALE)
            dk_ref[...] += _dot(ds, q_ref[rows, :])
            dqt_ref[qb] += _dot(ktv, ds)

        n_full = nb - 1 - kb
        products(kb, sa_ref, pa_ref, diagonal=True)

        def pair_body(j, carry):
            q1 = kb + 1 + 2 * j
            products(q1, sb_ref, pb_ref)
            consume(q1 - 1, sa_ref, pa_ref)
            products(q1 + 1, sa_ref, pa_ref)
            consume(q1, sb_ref, pb_ref)
            return carry

        lax.fori_loop(0, n_full // 2, pair_body, 0)

        @pl.when(n_full % 2 == 0)
        def _():
            consume(nb - 1, sa_ref, pa_ref)

        @pl.when(n_full % 2 == 1)
        def _():
            products(nb - 1, sb_ref, pb_ref)
            consume(nb - 2, sa_ref, pa_ref)
            consume(nb - 1, sb_ref, pb_ref)

        dv_ref[...] = _bf(dv_acc[...])

    def blk(w):
        return pl.BlockSpec((tb, w), lambda h, i: (i, h))

    def full(w):
        return pl.BlockSpec((T, w), lambda h, i: (0, h))

    return pl.pallas_call(
        kern, grid=(MLA_HEADS, nb),
        in_specs=[full(MLA_QK), blk(MLA_QK), blk(MLA_DV), full(MLA_DV), _head_blocks(nb, MLA_QK, tb),
                  _one_block(MLA_QK, tb), _head_blocks(nb, MLA_DV, tb), _head_blocks(nb, SUBLANES, tb),
                  _head_blocks(nb, SUBLANES, tb)],
        out_specs=[blk(MLA_QK), blk(MLA_DV), _head_blocks(nb, MLA_QK, tb)],
        out_shape=[jax.ShapeDtypeStruct((T, MLA_HEADS * MLA_QK), F32),
                   jax.ShapeDtypeStruct((T, MLA_HEADS * MLA_DV), BF16),
                   jax.ShapeDtypeStruct((nb, MLA_HEADS, MLA_QK, tb), F32)],
        scratch_shapes=[pltpu.VMEM((tb, MLA_DV), F32)] + [pltpu.VMEM((tb, tb), F32)] * 4,
        name="attn_bwd", compiler_params=_params(("arbitrary", "arbitrary")))(
            q, k, v, do, qt, kt, dot_, lse_rows, delta_rows)


def _group_norm(y):
    yc = y - _mean(y)
    rstd = lax.rsqrt(_mean(yc * yc) + EPS)
    return yc * rstd, rstd


def _mix_fwd(y, rg, o, gates, h1, gn_g, w_ret_o, w_mla_o, w_out, ln_g, ln_b, tm):
    T, D = h1.shape

    def body(i, y_ref, rg_ref, o_ref, gt_ref, h_ref, gn_ref, wr_ref, wm_ref, wo_ref, g_ref, b_ref,
             h2_ref, z_ref, yret_ref, ymla_ref, yr_ref, mix_ref):
        for h in range(RET_HEADS):
            sl = slice(h * RET_DV, (h + 1) * RET_DV)
            yn, _ = _group_norm(y_ref[:, sl])
            r = rg_ref[:, sl]
            yr_ref[:, sl] = _bf(r * _sigmoid(r) * (yn * gn_ref[:, sl]))
        yret = _dot(yr_ref[...], wr_ref[...])
        ymla = _dot(_bf(o_ref[...]), wm_ref[...])
        yret_ref[...] = yret
        ymla_ref[...] = ymla
        mix = _bf(_sigmoid(gt_ref[:, :D]) * yret + _sigmoid(gt_ref[:, D:]) * ymla)
        mix_ref[...] = mix
        z = ALPHA * h_ref[...] + _dot(mix, wo_ref[...])
        xhat, _ = _ln_stats(z)
        z_ref[...] = z
        h2_ref[...] = xhat * g_ref[...] + b_ref[...]

    return _rowcall("mix_fwd", body, T, tm, [y, rg, o, gates, h1], [gn_g, w_ret_o, w_mla_o, w_out, ln_g, ln_b],
                    [(D, F32), (D, F32), (D, F32), (D, F32), (RET_HEADS * RET_DV, BF16), (D, BF16)])


def _mix_bwd(dh2, z1, gates, yret, ymla, y, rg, o, gn_g, w_ret_o, w_mla_o, w_out, ln_g, tm, exchange=None):
    T, D = dh2.shape
    rv = RET_HEADS * RET_DV

    def body(i, dh_ref, z_ref, gt_ref, yret_ref, ymla_ref, y_ref, rg_ref, o_ref, gn_ref, wr_ref, wm_ref, wo_ref, g_ref,
             dz_ref, dgt_ref, drg_ref, dy_ref, do_ref, dyret_ref, dymla_ref, dg_ref, db_ref, dgn_ref, dot_ref,
             dl_ref):
        xhat, rstd = _ln_stats(z_ref[...])
        dz, dg, db = _ln_bwd(dh_ref[...], xhat, rstd, g_ref[...])
        _acc(i, dg_ref, dg)
        _acc(i, db_ref, db)
        dz_ref[...] = dz
        dmix = _dot_nt(_bf(dz), wo_ref[...])
        sr = _sigmoid(gt_ref[:, :D])
        sm = _sigmoid(gt_ref[:, D:])
        dgt_ref[:, :D] = _bf(dmix * yret_ref[...] * sr * (1.0 - sr))
        dgt_ref[:, D:] = _bf(dmix * ymla_ref[...] * sm * (1.0 - sm))
        dyret = _bf(dmix * sr)
        dymla = _bf(dmix * sm)
        dyret_ref[...] = dyret
        dymla_ref[...] = dymla
        dov = _dot_nt(dymla, wm_ref[...])
        do_ref[...] = _bf(dov)
        for h in range(MLA_HEADS):
            sl = slice(h * MLA_DV, (h + 1) * MLA_DV)
            dot_ref[h] = _bf(dov[:, sl].T)
            delta = jnp.sum(dov[:, sl] * o_ref[:, sl], axis=-1, keepdims=True)
            dl_ref[h] = jnp.broadcast_to(delta, (tm, LANES)).T[:SUBLANES, :]
        dyr = _dot_nt(dyret, wr_ref[...])
        dgn = []
        for h in range(RET_HEADS):
            sl = slice(h * RET_DV, (h + 1) * RET_DV)
            yn, grstd = _group_norm(y_ref[:, sl])
            r = rg_ref[:, sl]
            sig = _sigmoid(r)
            d = dyr[:, sl]
            drg_ref[:, sl] = _bf(d * (yn * gn_ref[:, sl]) * sig * (1.0 + r * (1.0 - sig)))
            dt = d * (r * sig)
            dgn.append(jnp.sum(dt * yn, axis=0, keepdims=True))
            dyn = dt * gn_ref[:, sl]
            dy_ref[:, sl] = _bf(grstd * (dyn - _mean(dyn) - yn * _mean(dyn * yn)))
        _acc(i, dgn_ref, jnp.concatenate(dgn, axis=1))

    return _rowcall("mix_bwd", body, T, tm, [dh2, z1, gates, yret, ymla, y, rg, o],
                    [gn_g, w_ret_o, w_mla_o, w_out, ln_g],
                    [(D, F32), (2 * D, BF16), (rv, BF16), (rv, BF16), (MLA_HEADS * MLA_DV, BF16), (D, BF16), (D, BF16)],
                    [((1, D), F32), ((1, D), F32), ((1, rv), F32)],
                    tiled_outs=[_transposed_blocks(T, tm, MLA_DV, BF16), _transposed_blocks(T, tm, SUBLANES, F32)],
                    exchange=exchange)


def _proj_mla_bwd(dqt, dk, dv, lat, tabs, w_uq, w_uk, w_uv, qn_g, kvn_g, tm):
    T = dk.shape[0]
    H = MLA_HEADS
    lat_w = Q_LORA + KV_LORA

    def body(i, dk_ref, dv_ref, lat_ref, c_ref, s1_ref, s2_ref, dqt_ref, wuq_ref, wuk_ref, wuv_ref, qg_ref, kg_ref,
             dlat_ref, dkpe_ref, dqb_ref, dkn_ref, dqg_ref, dkg_ref):
        c, s1, s2 = c_ref[...], s1_ref[...], s2_ref[...]
        dkpe = jnp.zeros((tm, LANES), F32)
        for h in range(H):
            o = h * MLA_QK
            dqh = dqt_ref[h].T
            dqb_ref[:, o:o + MLA_NOPE] = _bf(dqh[:, :MLA_NOPE])
            dqb_ref[:, o + MLA_NOPE:o + MLA_QK] = _bf(_rope_pe_bwd(dqh[:, MLA_NOPE:], c, s1, s2))
            dkn_ref[:, h * MLA_NOPE:(h + 1) * MLA_NOPE] = _bf(dk_ref[:, o:o + MLA_NOPE])
            dkpe += dk_ref[:, o + MLA_NOPE:o + MLA_QK]
        dkpe_ref[...] = _bf(_rope_pe_bwd(dkpe, c, s1, s2))
        dcqn = _dot_nt(dqb_ref[...], wuq_ref[...])
        dckn = _dot_nt(dkn_ref[...], wuk_ref[...]) + _dot_nt(dv_ref[...], wuv_ref[...])
        for dn, x, g_ref, dg_ref, sl in ((dcqn, lat_ref[:, :Q_LORA], qg_ref, dqg_ref, slice(0, Q_LORA)),
                                         (dckn, lat_ref[:, Q_LORA:], kg_ref, dkg_ref, slice(Q_LORA, lat_w))):
            xn, r = _rms(x, None)
            _acc(i, dg_ref, jnp.sum(dn * xn, axis=0, keepdims=True))
            dxn = dn * g_ref[...]
            dlat_ref[:, sl] = _bf(r * (dxn - xn * _mean(dxn * xn)))

    dqt_shape, dqt_spec = _transposed_blocks(T, tm, MLA_QK, F32)
    assert dqt.shape == dqt_shape.shape
    return _rowcall("proj_mla_bwd", body, T, tm, [dk, dv, lat, *tabs], [w_uq, w_uk, w_uv, qn_g, kvn_g],
                    [(lat_w, BF16), (LANES, BF16), (H * MLA_QK, BF16), (H * MLA_NOPE, BF16)],
                    [((1, Q_LORA), F32), ((1, KV_LORA), F32)], tiled_ins=[(dqt, dqt_spec)])


def _proj_bwd(drq, drk, drv, drg, dz1, dlat, dkpe, dgates, cos_r, sin_r, w_r, w_c, w_kpe, w_g, tm):
    T, D = dz1.shape
    qk = RET_HEADS * RET_DK
    rv = RET_HEADS * RET_DV

    def body(i, drq_ref, drk_ref, drv_ref, drg_ref, dz_ref, dlat_ref, dkpe_ref, dgt_ref, cos_ref, sin_ref,
             wr_ref, wc_ref, wk_ref, wg_ref, dh_ref, dpr_ref):
        cos, sin = cos_ref[...], sin_ref[...]
        for src, off, scale in ((drq_ref, 0, 1.0), (drk_ref, qk, RET_DK ** -0.5)):
            for h in range(RET_HEADS):
                d = src[:, h * RET_DK:(h + 1) * RET_DK]
                dpr_ref[:, off + h * RET_DK:off + (h + 1) * RET_DK] = _bf(
                    (d * cos + _roll(d * sin, RET_DK // 2)) * scale)
        dpr_ref[:, 2 * qk:2 * qk + rv] = drv_ref[...]
        dpr_ref[:, 2 * qk + rv:] = drg_ref[...]
        dh_ref[...] = (ALPHA * dz_ref[...] + _dot_nt(dpr_ref[...], wr_ref[...]) + _dot_nt(dlat_ref[...], wc_ref[...])
                       + _dot_nt(dkpe_ref[...], wk_ref[...]) + _dot_nt(dgt_ref[...], wg_ref[...]))

    return _rowcall("proj_bwd", body, T, tm, [drq, drk, drv, drg, dz1, dlat, dkpe, dgates, cos_r, sin_r],
                    [w_r, w_c, w_kpe, w_g], [(D, F32), (2 * qk + 2 * rv, BF16)])


def _ple_loss(h3, p, target, w_gate, w_proj, ln_g, ln_b, tm):
    T, D = h3.shape

    def body(i, h_ref, p_ref, t_ref, wg_ref, wp_ref, g_ref, b_ref, dh_ref, dgp_ref, dpp_ref, loss_ref, dg_ref, db_ref):
        hv = h_ref[...]
        sg = _sigmoid(_dot(_bf(hv), wg_ref[...]))
        pp = _dot(_bf(p_ref[...]), wp_ref[...])
        xhat, rstd = _ln_stats(ALPHA * hv + sg * pp)
        err = xhat * g_ref[...] + b_ref[...] - t_ref[...]
        row_loss = 0.5 * _mean(err * err)
        _acc(i, loss_ref, jnp.broadcast_to(jnp.sum(row_loss, axis=0, keepdims=True), (1, LANES)))
        dz, dg, db = _ln_bwd(err * (1.0 / D), xhat, rstd, g_ref[...])
        _acc(i, dg_ref, dg)
        _acc(i, db_ref, db)
        dgp = _bf(dz * pp * sg * (1.0 - sg))
        dgp_ref[...] = dgp
        dpp_ref[...] = _bf(dz * sg)
        dh_ref[...] = ALPHA * dz + _dot_nt(dgp, wg_ref[...])

    return _rowcall("ple_loss", body, T, tm, [h3, p, target], [w_gate, w_proj, ln_g, ln_b],
                    [(D, F32), (D, BF16), (D, BF16)], [((1, LANES), F32), ((1, D), F32), ((1, D), F32)])


def _ewise(name, fn, ins, n_out, out_dtype=F32):
    r, c = ins[0].shape
    tr = _tile(r, max(8, (1 << 19) // c // 8 * 8), 8)

    def kern(*refs):
        outs = fn(*[x[...] for x in refs[:len(ins)]])
        for o_ref, o in zip(refs[len(ins):], outs):
            o_ref[...] = o.astype(out_dtype)

    spec = pl.BlockSpec((tr, c), lambda i: (i, 0))
    return pl.pallas_call(kern, grid=(r // tr,), in_specs=[spec] * len(ins), out_specs=[spec] * n_out,
                          out_shape=[jax.ShapeDtypeStruct((r, c), out_dtype)] * n_out, name=name,
                          compiler_params=_params(("arbitrary",)))(*ins)


def _adamw_math(w, g, m, v):
    m = ADAM_B1 * m + (1.0 - ADAM_B1) * g
    v = ADAM_B2 * v + (1.0 - ADAM_B2) * (g * g)
    m_hat = m / (1.0 - ADAM_B1 ** ADAM_STEP)
    v_hat = v / (1.0 - ADAM_B2 ** ADAM_STEP)
    return -ADAM_LR * (m_hat / (jnp.sqrt(v_hat) + ADAM_EPS) + ADAM_WD * w), m, v


def _adamw(name, w, g, m, v):
    shape = w.shape
    c = shape[-1]
    flat = [t.reshape(-1, c) for t in (w, g, m, v)]
    return [t.reshape(shape) for t in _ewise(name, _adamw_math, flat, 3)]


def _place():
    return lax.axis_index("x"), lax.axis_index("y"), lax.axis_index("c")


def _dma_sems(n):
    return [pltpu.SemaphoreType.DMA((n,)), pltpu.SemaphoreType.DMA((n,))]


N_PEER_CHIPS = N_CHIPS - 1


def _chips_exchange(name, srcs, broadcast):
    n = len(srcs)

    def kern(*refs):
        cps = _chip_copies(refs[:n], refs[n:2 * n], refs[2 * n], refs[2 * n + 1], broadcast)
        for cp in cps:
            cp.start()
        _wait_copies(cps)

    return pl.pallas_call(
        kern, out_shape=_exchange_shapes(srcs), in_specs=[HBM_SPEC] * n, out_specs=[HBM_SPEC] * n,
        scratch_shapes=_dma_sems(n * N_PEER_CHIPS), name=name)(*srcs)


def _exchange_shapes(srcs):
    return [jax.ShapeDtypeStruct((N_PEER_CHIPS,) + s.shape[1:], s.dtype) for s in srcs]


def _chip_copies(src_refs, out_refs, send_sems, recv_sems, broadcast):
    x, y, c = _place()
    peers = [(1 - x, y), (x, 1 - y), (1 - x, 1 - y)]
    cps = []
    for j, (px, py) in enumerate(peers):
        for a, (src_ref, out_ref) in enumerate(zip(src_refs, out_refs)):
            piece = src_ref.at[c] if broadcast else src_ref.at[2 * px + py]
            cps.append(pltpu.make_async_remote_copy(
                src_ref=piece, dst_ref=out_ref.at[j], send_sem=send_sems.at[a * N_PEER_CHIPS + j],
                recv_sem=recv_sems.at[a * N_PEER_CHIPS + j], device_id=(px, py, c), device_id_type=MESH))
    return cps


def _wait_copies(cps):
    for cp in cps:
        cp.wait_recv()
    for cp in cps:
        cp.wait_send()


def _sibling_swap(name, srcs, halves):
    n = len(srcs)

    def kern(*refs):
        src_refs, out_refs = refs[:n], refs[n:2 * n]
        send_sems, recv_sems = refs[2 * n:]
        x, y, c = _place()

        def copy(a):
            piece = src_refs[a].at[:, 1 - c] if halves else src_refs[a]
            return pltpu.make_async_remote_copy(
                src_ref=piece, dst_ref=out_refs[a], send_sem=send_sems.at[a], recv_sem=recv_sems.at[a],
                device_id=(x, y, 1 - c), device_id_type=MESH)

        cps = [copy(a) for a in range(n)]
        for cp in cps:
            cp.start()
        for cp in cps:
            cp.wait_recv()
        for cp in cps:
            cp.wait_send()

    def out_shape(s):
        return jax.ShapeDtypeStruct((s.shape[0],) + s.shape[2:] if halves else s.shape, s.dtype)

    return pl.pallas_call(
        kern, out_shape=[out_shape(s) for s in srcs], in_specs=[HBM_SPEC] * n, out_specs=[HBM_SPEC] * n,
        scratch_shapes=_dma_sems(n), name=name)(*srcs)


def _all_devices(name, src, reduce):
    r, c = src.shape
    n_dev = 2 * N_CHIPS

    def kern(src_ref, out_ref, *scratch):
        if reduce:
            gat_ref, send_sems, recv_sems = scratch
        else:
            gat_ref = out_ref
            send_sems, recv_sems = scratch
        x, y, cc = _place()
        me = 4 * x + 2 * y + cc
        gat_ref[me] = src_ref[...]
        peers = []
        for j in range(1, n_dev):
            px = 1 - x if j & 4 else x
            py = 1 - y if j & 2 else y
            pc = 1 - cc if j & 1 else cc
            peers.append((px, py, pc))

        def copy(j, peer, slot):
            return pltpu.make_async_remote_copy(
                src_ref=src_ref, dst_ref=gat_ref.at[slot], send_sem=send_sems.at[j], recv_sem=recv_sems.at[j],
                device_id=peer, device_id_type=MESH)

        sends = [copy(j, peer, me) for j, peer in enumerate(peers)]
        for cp in sends:
            cp.start()
        for j, (px, py, pc) in enumerate(peers):
            copy(j, (px, py, pc), 4 * px + 2 * py + pc).wait_recv()
        for cp in sends:
            cp.wait_send()
        if reduce:
            total = gat_ref[0]
            for d in range(1, n_dev):
                total = total + gat_ref[d]
            out_ref[...] = total

    out_shape = jax.ShapeDtypeStruct((r, c) if reduce else (n_dev, r, c), src.dtype)
    scratch = ([pltpu.VMEM((n_dev, r, c), src.dtype)] if reduce else []) + _dma_sems(n_dev - 1)
    return pl.pallas_call(kern, out_shape=out_shape, in_specs=[VMEM_SPEC], out_specs=VMEM_SPEC,
                          scratch_shapes=scratch, name=name)(src)


def _halves(t, axis):
    return t.reshape(t.shape[:axis] + (2, t.shape[axis] // 2) + t.shape[axis + 1:])


def _by_core(mine, theirs, axis):
    c = lax.axis_index("c")
    both = jnp.where(c == 0, jnp.stack([mine, theirs], axis), jnp.stack([theirs, mine], axis))
    return both.reshape(both.shape[:axis] + (2 * both.shape[axis + 1],) + both.shape[axis + 2:])


def _chip_order(own, others):
    me = 2 * lax.axis_index("x") + lax.axis_index("y")
    cands = jnp.concatenate([own[None], others], axis=0)
    slot_of_flip = (0, 2, 1, 3)
    pick = jnp.asarray(slot_of_flip, jnp.int32)[jnp.arange(N_CHIPS, dtype=jnp.int32) ^ me]
    return jnp.stack([lax.dynamic_index_in_dim(cands, pick[k], 0, keepdims=False) for k in range(N_CHIPS)])


def _join_shards(name, shards):
    _, r, c = shards.shape
    if name in COL_SHARDED:
        return shards.transpose(1, 0, 2).reshape(r, N_CHIPS * c)
    return shards.reshape(N_CHIPS * r, c)


def _split_shards(name, full):
    if full.ndim == 3:
        return full
    r, c = full.shape
    if name in COL_SHARDED:
        return jnp.stack([full[:, k * (c // N_CHIPS):(k + 1) * (c // N_CHIPS)] for k in range(N_CHIPS)])
    return full.reshape(N_CHIPS, r // N_CHIPS, c)


def _rope_tables(positions):
    pos = positions.reshape(-1).astype(F32)[:, None]
    half = RET_DK // 2
    ang = pos * (ROPE_BASE ** (-jnp.arange(half, dtype=F32) / half))
    cos_r = jnp.concatenate([jnp.cos(ang)] * 2, axis=1)
    sin_r = jnp.concatenate([-jnp.sin(ang), jnp.sin(ang)], axis=1)
    half = MLA_ROPE // 2
    ang = pos * (ROPE_BASE ** (-jnp.arange(half, dtype=F32) / half))
    zeros = jnp.zeros_like(ang)
    rest = LANES - MLA_ROPE
    c = jnp.concatenate([jnp.cos(ang)] * 2 + [jnp.ones((ang.shape[0], rest), F32)], axis=1)
    s1 = jnp.concatenate([-jnp.sin(ang), zeros, jnp.zeros((ang.shape[0], rest), F32)], axis=1)
    s2 = jnp.concatenate([zeros, jnp.sin(ang), jnp.zeros((ang.shape[0], rest), F32)], axis=1)
    return cos_r, sin_r, (c, s1, s2)


GATHER_GROUPS = (("ffn1_w_in", "ffn1_w_out"), ("w_in", "w_uq", "w_ukv"),
                 ("w_ret_o", "w_mla_o", "w_out", "ffn2_w_in", "ffn2_w_out", "ple_w_gate", "ple_w_proj"))
REDUCE_GROUPS = (("ple_w_gate", "ple_w_proj", "ffn2_w_in", "ffn2_w_out"),
                 ("w_out", "w_ret_o", "w_mla_o", "w_uq", "w_ukv", "w_in"), ("ffn1_w_in", "ffn1_w_out"))


def _gathered(tag, names, own, mine):
    theirs = _sibling_swap("gather_cores_" + tag, mine, False)
    out = {}
    for n, m, t in zip(names, mine, theirs):
        full = _chip_order(own[n], _by_core(m, t, 1))
        out[n] = full if n in ("ffn1_w_in", "ffn2_w_in") else _join_shards(n, full)
    return out


def _chip_sums(tag, names, grads):
    c = lax.axis_index("c")
    halves = [_halves(_split_shards(n, grads[n]), 1) for n in names]
    theirs = _sibling_swap("reduce_cores_" + tag, halves, True)
    sums = []
    for n, g, t in zip(names, halves, theirs):
        mine = lax.dynamic_index_in_dim(g, c, axis=1, keepdims=False)
        k, r, cc = mine.shape
        sums.append(_ewise("reduce_cores_add_" + n, lambda a, b: (a.astype(F32) + b.astype(F32),),
                           [mine.reshape(k * r, cc), t.reshape(k * r, cc)], 1, BF16)[0].reshape(k, r, cc))
    return sums


def _block_totals(names, sums, parts):
    me = 2 * lax.axis_index("x") + lax.axis_index("y")
    totals = []
    for n, s, pt in zip(names, sums, parts):
        own = lax.dynamic_index_in_dim(s, me, axis=0, keepdims=False)
        totals.append(_ewise("reduce_chips_add_" + n,
                             lambda a, b, c_, d: (((a.astype(F32) + b.astype(F32)) + c_.astype(F32)) + d.astype(F32),),
                             [own, pt[0], pt[1], pt[2]], 1, F32)[0])
    return totals


def _local_step(x, p, positions, target, shards, ln_g, ln_b, gn_g, qn_g, kvn_g):
    T, D = x.shape
    tm = min(256, T)
    H = MLA_HEADS
    qk, rv = RET_HEADS * RET_DK, RET_HEADS * RET_DV
    cos_r, sin_r, tabs = _rope_tables(positions)
    lgam = jnp.broadcast_to(jnp.log(1.0 - 2.0 ** (-5.0 - jnp.arange(RET_HEADS, dtype=F32)))[:, None, None],
                            (RET_HEADS, 1, LANES))
    lng = [ln_g[k:k + 1] for k in range(N_LN)]
    lnb = [ln_b[k:k + 1] for k in range(N_LN)]
    own = {n: _bf(shards[n]) for n in BIG_WEIGHTS}
    to_send = [[_halves(own[n], 0) for n in names] for names in GATHER_GROUPS]

    w = _gathered("a", GATHER_GROUPS[0], own, _chips_exchange("gather_chips_a", to_send[0], True))
    h1, z0, a1, *arrived = _ffn_fwd("ffn1_fwd", x, w["ffn1_w_in"], w["ffn1_w_out"], lng[0], lnb[0], tm,
                                    exchange=(to_send[1], True))
    w.update(_gathered("b", GATHER_GROUPS[1], own, arrived))

    w_in = w["w_in"]
    o_lat, o_kpe, o_gate = 2 * qk + 2 * rv, 2 * qk + 2 * rv + Q_LORA + KV_LORA, 2 * qk + 2 * rv + Q_LORA + KV_LORA + MLA_ROPE
    w_r, w_c = w_in[:, :o_lat], w_in[:, o_lat:o_kpe]
    w_kpe = jnp.pad(w_in[:, o_kpe:o_gate], ((0, 0), (0, LANES - MLA_ROPE)))
    w_g = w_in[:, o_gate:]
    w_uq = jnp.pad(w["w_uq"].reshape(Q_LORA, H, MLA_NOPE + MLA_ROPE),
                   ((0, 0), (0, 0), (0, MLA_QK - MLA_NOPE - MLA_ROPE))).reshape(Q_LORA, H * MLA_QK)
    w_ukv = w["w_ukv"].reshape(KV_LORA, H, MLA_NOPE + MLA_DV)
    w_uk = w_ukv[:, :, :MLA_NOPE].reshape(KV_LORA, H * MLA_NOPE)
    w_uv = w_ukv[:, :, MLA_NOPE:].reshape(KV_LORA, H * MLA_DV)

    rq, rk, rvv, rg = _proj_ret(h1, w_r, cos_r, sin_r, tm)
    lat, gates, q, k, v, latn, qt, kt, vt = _proj_mla(h1, tabs, w_c, w_kpe, w_g, w_uq, w_uk, w_uv, qn_g, kvn_g, tm)
    y = _ret_fwd(rq, rk, rvv, lgam)
    o, lse_rows, *arrived = _attn_fwd(k, qt, vt, exchange=to_send[2])
    w.update(_gathered("c", GATHER_GROUPS[2], own, arrived))
    h2, z1, yret, ymla, yr, mix = _mix_fwd(y, rg, o, gates, h1, gn_g, w["w_ret_o"], w["w_mla_o"], w["w_out"],
                                           lng[1], lnb[1], tm)
    h3, z2, a2 = _ffn_fwd("ffn2_fwd", h2, w["ffn2_w_in"], w["ffn2_w_out"], lng[2], lnb[2], tm)

    dh3, dgp, dpp, loss, dg3, db3 = _ple_loss(h3, p, target, w["ple_w_gate"], w["ple_w_proj"], lng[3], lnb[3], tm)
    dh2, da2, s2, df2, dg2, db2 = _ffn_bwd("ffn2_bwd", dh3, z2, a2, w["ffn2_w_in"], w["ffn2_w_out"], lng[2], tm)
    grads = {"ple_w_gate": _mm_tn("wg_ple_gate", h3, dgp), "ple_w_proj": _mm_tn("wg_ple_proj", p, dpp),
             "ffn2_w_in": _mm_tn("wg_ffn2_in", h2, da2, n_split=N_CHIPS), "ffn2_w_out": _mm_tn("wg_ffn2_out", s2, df2)}
    sums1 = _chip_sums("1", REDUCE_GROUPS[0], grads)
    (dz1, dgates, drg, dy, do, dyret, dymla, dg1, db1, dgn, dot_, delta_rows, *parts1) = _mix_bwd(
        dh2, z1, gates, yret, ymla, y, rg, o, gn_g, w["w_ret_o"], w["w_mla_o"], w["w_out"], lng[1], tm,
        exchange=(sums1, False))
    drq = _ret_bwd_q(rq, rk, rvv, dy, lgam)
    drk, drv = _ret_bwd_kv(rq, rk, rvv, dy, lgam)
    dk, dv, dqt = _attn_bwd(q, k, v, do, qt, kt, dot_, lse_rows, delta_rows)
    dlat, dkpe, dqb, dkn, dqg, dkg = _proj_mla_bwd(dqt, dk, dv, lat, tabs, w_uq, w_uk, w_uv, qn_g, kvn_g, tm)
    dh1, dpr = _proj_bwd(drq, drk, drv, drg, dz1, dlat, dkpe, dgates, cos_r, sin_r, w_r, w_c, w_kpe, w_g, tm)
    g_uq = _mm_tn("wg_uq", latn[:, :Q_LORA], dqb).reshape(Q_LORA, H, MLA_QK)[:, :, :MLA_NOPE + MLA_ROPE]
    g_uk = _mm_tn("wg_uk", latn[:, Q_LORA:], dkn).reshape(KV_LORA, H, MLA_NOPE)
    g_uv = _mm_tn("wg_uv", latn[:, Q_LORA:], dv).reshape(KV_LORA, H, MLA_DV)
    grads.update({
        "w_in": jnp.concatenate([_mm_tn("wg_in_r", h1, dpr), _mm_tn("wg_in_c", h1, dlat),
                                 _mm_tn("wg_in_kpe", h1, dkpe)[:, :MLA_ROPE], _mm_tn("wg_in_g", h1, dgates)], axis=1),
        "w_ret_o": _mm_tn("wg_ret_o", yr, dyret),
        "w_uq": g_uq.reshape(Q_LORA, H * (MLA_NOPE + MLA_ROPE)),
        "w_ukv": jnp.concatenate([g_uk, g_uv], axis=2).reshape(KV_LORA, H * (MLA_NOPE + MLA_DV)),
        "w_mla_o": _mm_tn("wg_mla_o", o, dymla),
        "w_out": _mm_tn("wg_out", mix, dz1)})
    sums2 = _chip_sums("2", REDUCE_GROUPS[1], grads)
    dx, da1, s1, df1, dg0, db0, *parts2 = _ffn_bwd("ffn1_bwd", dh1, z0, a1, w["ffn1_w_in"], w["ffn1_w_out"], lng[0], tm,
                                                   exchange=(sums2, False))
    grads.update({"ffn1_w_in": _mm_tn("wg_ffn1_in", x, da1, n_split=N_CHIPS),
                  "ffn1_w_out": _mm_tn("wg_ffn1_out", s1, df1)})
    sums3 = _chip_sums("3", REDUCE_GROUPS[2], grads)
    parts3 = _chips_exchange("reduce_chips_3", sums3, False)

    names = [n for group in REDUCE_GROUPS for n in group]
    totals = _block_totals(names, sums1 + sums2 + sums3, list(parts1) + list(parts2) + list(parts3))
    others = _sibling_swap("reduce_join", totals, False)
    reduced = {n: _by_core(t, o_, 0) for n, t, o_ in zip(names, totals, others)}
    small = {"ln_g": jnp.concatenate([dg0, dg1, dg2, dg3], axis=0), "ln_b": jnp.concatenate([db0, db1, db2, db3], axis=0),
             "ret_gn_g": dgn, "q_norm_g": dqg, "kv_norm_g": dkg}
    return loss[0, 0], dx, reduced, small


def kernel(x, p, positions, ln_g, ln_b, ffn1_w_in, ffn1_w_out, w_in, ret_gn_g, w_ret_o, q_norm_g, kv_norm_g, w_uq, w_ukv, w_mla_o, w_out, ffn2_w_in, ffn2_w_out, ple_w_gate, ple_w_proj, loss_target, m_ln_g, m_ln_b, m_ffn1_w_in, m_ffn1_w_out, m_w_in, m_ret_gn_g, m_w_ret_o, m_q_norm_g, m_kv_norm_g, m_w_uq, m_w_ukv, m_w_mla_o, m_w_out, m_ffn2_w_in, m_ffn2_w_out, m_ple_w_gate, m_ple_w_proj, v_ln_g, v_ln_b, v_ffn1_w_in, v_ffn1_w_out, v_w_in, v_ret_gn_g, v_w_ret_o, v_q_norm_g, v_kv_norm_g, v_w_uq, v_w_ukv, v_w_mla_o, v_w_out, v_ffn2_w_in, v_ffn2_w_out, v_ple_w_gate, v_ple_w_proj):
    names = ("ln_g", "ln_b", "ffn1_w_in", "ffn1_w_out", "w_in", "ret_gn_g", "w_ret_o", "q_norm_g", "kv_norm_g", "w_uq",
             "w_ukv", "w_mla_o", "w_out", "ffn2_w_in", "ffn2_w_out", "ple_w_gate", "ple_w_proj")
    weights = dict(zip(names, (ln_g, ln_b, ffn1_w_in, ffn1_w_out, w_in, ret_gn_g, w_ret_o, q_norm_g, kv_norm_g, w_uq,
                               w_ukv, w_mla_o, w_out, ffn2_w_in, ffn2_w_out, ple_w_gate, ple_w_proj)))
    m_in = dict(zip(names, (m_ln_g, m_ln_b, m_ffn1_w_in, m_ffn1_w_out, m_w_in, m_ret_gn_g, m_w_ret_o, m_q_norm_g,
                            m_kv_norm_g, m_w_uq, m_w_ukv, m_w_mla_o, m_w_out, m_ffn2_w_in, m_ffn2_w_out, m_ple_w_gate,
                            m_ple_w_proj)))
    v_in = dict(zip(names, (v_ln_g, v_ln_b, v_ffn1_w_in, v_ffn1_w_out, v_w_in, v_ret_gn_g, v_w_ret_o, v_q_norm_g,
                            v_kv_norm_g, v_w_uq, v_w_ukv, v_w_mla_o, v_w_out, v_ffn2_w_in, v_ffn2_w_out, v_ple_w_gate,
                            v_ple_w_proj)))
    chip = 2 * lax.axis_index("x") + lax.axis_index("y")
    D = x.shape[-1]
    dq = D // N_CHIPS

    shards = {n: weights[n][0] for n in BIG_WEIGHTS}
    ln_all = _all_devices("gather_ln", jnp.concatenate([ln_g[0], ln_b[0]], axis=0), False)
    ln_full = ln_all[::2].transpose(1, 0, 2).reshape(2 * N_LN, D)
    loss, dx, big, small = _local_step(x[0], p[0, 0], positions, loss_target[0], shards, ln_full[:N_LN],
                                       ln_full[N_LN:], ret_gn_g, q_norm_g, kv_norm_g)

    loss = lax.psum(loss, ("x", "y", "c"))
    small_names = ("ln_g", "ln_b", "ret_gn_g", "q_norm_g", "kv_norm_g")
    flat = jnp.concatenate([small[n].reshape(-1) for n in small_names])
    rows = -(-flat.shape[0] // LANES // 8) * 8
    flat = jnp.pad(flat, (0, rows * LANES - flat.shape[0])).reshape(rows, LANES)
    flat = _all_devices("reduce_small", flat, True).reshape(-1)
    off = 0
    for n in small_names:
        size = small[n].size
        small[n] = flat[off:off + size].reshape(small[n].shape)
        off += size
    g_out = dict(big)
    for n in ("ln_g", "ln_b"):
        g_out[n] = lax.dynamic_slice_in_dim(small[n], chip * dq, dq, axis=1)
    for n in ("ret_gn_g", "q_norm_g", "kv_norm_g"):
        g_out[n] = small[n]

    deltas, new_m, new_v = {}, {}, {}
    for n in names:
        g = g_out[n].reshape(weights[n].shape)
        g_out[n] = g
        deltas[n], new_m[n], new_v[n] = _adamw("adamw_" + n, weights[n], g, m_in[n], v_in[n])
    return (loss, dx[None], *[g_out[n] for n in names], *[deltas[n] for n in names], *[new_m[n] for n in names],
            *[new_v[n] for n in names])
```

```python
import functools

import jax
import jax.numpy as jnp
from jax import lax
from jax.experimental import pallas as pl
from jax.experimental.pallas import tpu as pltpu

D_MODEL = 1024
CHUNK = 64
D_PLE = 256
D_FF = 2816
RET_HEADS = 8
RET_DK = 128
RET_DV = 256
MLA_HEADS = 8
MLA_NOPE = 128
MLA_ROPE = 64
MLA_DV = 128
MLA_QK = 256
Q_LORA = 256
KV_LORA = 256
ROPE_BASE = 10000.0
EPS = 1e-5
N_LN = 4
ALPHA = 2.0 ** 0.25
ADAM_LR = 0.001
ADAM_B1 = 0.9
ADAM_B2 = 0.999
ADAM_EPS = 1e-08
ADAM_WD = 0.01
ADAM_STEP = 10

LANES = 128
VMEM_LIMIT = 60 << 20
N_CHIPS = 4

F32 = jnp.float32
BF16 = jnp.bfloat16
MESH = pl.DeviceIdType.MESH
HBM_SPEC = pl.BlockSpec(memory_space=pltpu.HBM)
VMEM_SPEC = pl.BlockSpec(memory_space=pltpu.VMEM)

BIG_WEIGHTS = ("ffn1_w_in", "ffn1_w_out", "w_in", "w_ret_o", "w_uq", "w_ukv", "w_mla_o", "w_out",
               "ffn2_w_in", "ffn2_w_out", "ple_w_gate", "ple_w_proj")
COL_SHARDED = ("ffn1_w_in", "w_in", "w_uq", "w_ukv", "ffn2_w_in", "ple_w_proj")


def _dot(a, b):
    return jnp.dot(a, b, preferred_element_type=F32)


def _dot_nt(a, b):
    return lax.dot_general(a, b, (((1,), (1,)), ((), ())), preferred_element_type=F32)


def _dot_tn(a, b):
    return lax.dot_general(a, b, (((0,), (0,)), ((), ())), preferred_element_type=F32)


def _bf(x):
    return x.astype(BF16)


def _sigmoid(x):
    return 1.0 / (1.0 + jnp.exp(-x))


def _mean(x):
    return jnp.mean(x, axis=-1, keepdims=True)


def _ln_stats(z):
    zc = z - _mean(z)
    rstd = lax.rsqrt(_mean(zc * zc) + EPS)
    return zc * rstd, rstd


def _ln_bwd(dy, xhat, rstd, g):
    dxhat = dy * g
    dz = rstd * (dxhat - _mean(dxhat) - xhat * _mean(dxhat * xhat))
    return dz, jnp.sum(dy * xhat, axis=0, keepdims=True), jnp.sum(dy, axis=0, keepdims=True)


def _roll(x, shift):
    return pltpu.roll(x, shift, 1)


def _chunk_of(idx):
    return jnp.right_shift(idx, CHUNK.bit_length() - 1)


def _tile(n, cap, mult=LANES):
    if n <= cap:
        return n
    for t in range(cap - cap % mult, 0, -mult):
        if n % t == 0:
            return t
    return n


def _zero_map(nd, *_):
    return (0,) * nd


def _params(sem):
    return pltpu.CompilerParams(dimension_semantics=sem, vmem_limit_bytes=VMEM_LIMIT)


def _rowcall(name, body, n_rows, tm, row_ins, full_ins, row_outs, acc_outs=(), tiled_outs=(), tiled_ins=(),
             exchange=None):
    n_steps = n_rows // tm
    ex_srcs, broadcast = exchange if exchange else ((), False)
    n_ex = len(ex_srcs)
    n_in = len(row_ins) + len(tiled_ins) + len(full_ins)
    n_out = len(row_outs) + len(acc_outs) + len(tiled_outs)

    def kern(*refs):
        step = pl.program_id(0)
        ex_in, ex_out = refs[n_in:n_in + n_ex], refs[n_in + n_ex + n_out:n_in + 2 * n_ex + n_out]
        sems = refs[n_in + 2 * n_ex + n_out:]
        if n_ex:
            @pl.when(step == 0)
            def _():
                for cp in _chip_copies(ex_in, ex_out, *sems, broadcast):
                    cp.start()

        body(step, *refs[:n_in], *refs[n_in + n_ex:n_in + n_ex + n_out])
        if n_ex:
            @pl.when(step == n_steps - 1)
            def _():
                _wait_copies(_chip_copies(ex_in, ex_out, *sems, broadcast))

    in_specs = [pl.BlockSpec((tm, a.shape[1]), lambda i: (i, 0)) for a in row_ins]
    in_specs += [spec for (_, spec) in tiled_ins]
    row_ins = list(row_ins) + [a for (a, _) in tiled_ins]
    in_specs += [pl.BlockSpec(a.shape, functools.partial(_zero_map, a.ndim), pipeline_mode=pl.Buffered(1))
                 for a in full_ins]
    in_specs += [HBM_SPEC] * n_ex
    out_specs = [pl.BlockSpec((tm, w), lambda i: (i, 0)) for (w, _) in row_outs]
    out_specs += [pl.BlockSpec(s, functools.partial(_zero_map, len(s))) for (s, _) in acc_outs]
    out_specs += [spec for (_, spec) in tiled_outs]
    out_specs += [HBM_SPEC] * n_ex
    out_shape = [jax.ShapeDtypeStruct((n_rows, w), dt) for (w, dt) in row_outs]
    out_shape += [jax.ShapeDtypeStruct(s, dt) for (s, dt) in acc_outs]
    out_shape += [shape for (shape, _) in tiled_outs]
    out_shape += _exchange_shapes(ex_srcs)
    return pl.pallas_call(kern, grid=(n_steps,), in_specs=in_specs, out_specs=out_specs, out_shape=out_shape,
                          scratch_shapes=_dma_sems(n_ex * N_PEER_CHIPS) if n_ex else [], name=name,
                          compiler_params=_params(("arbitrary",)))(*row_ins, *full_ins, *ex_srcs)


def _acc(step, ref, val):
    @pl.when(step == 0)
    def _():
        ref[...] = val

    @pl.when(step != 0)
    def _():
        ref[...] += val


def _ffn_fwd(name, x, w_in4, w_out, ln_g, ln_b, tm, exchange=None):
    T, D = x.shape
    fh = w_in4.shape[2]

    def body(i, x_ref, w4_ref, wo_ref, g_ref, b_ref, h_ref, z_ref, a_ref):
        xv = x_ref[...]
        xb = _bf(xv)
        f = jnp.zeros((tm, D), F32)
        for k in range(2):
            gk = _dot(xb, w4_ref[k])
            uk = _dot(xb, w4_ref[2 + k])
            a_ref[:, k * fh:(k + 1) * fh] = _bf(gk)
            a_ref[:, (2 + k) * fh:(3 + k) * fh] = _bf(uk)
            f += _dot(_bf(gk * _sigmoid(gk) * uk), wo_ref[k * fh:(k + 1) * fh, :])
        z = ALPHA * xv + 0.5 * f
        xhat, _ = _ln_stats(z)
        z_ref[...] = z
        h_ref[...] = xhat * g_ref[...] + b_ref[...]

    return _rowcall(name, body, T, tm, [x], [w_in4, w_out, ln_g, ln_b],
                    [(D, F32), (D, F32), (4 * fh, BF16)], exchange=exchange)


def _ffn_bwd(name, dh, z, a, w_in4, w_out, ln_g, tm, exchange=None):
    T, D = dh.shape
    fh = w_in4.shape[2]

    def body(i, dh_ref, z_ref, a_ref, w4_ref, wo_ref, g_ref, dx_ref, da_ref, s_ref, df_ref, dg_ref, db_ref):
        xhat, rstd = _ln_stats(z_ref[...])
        dz, dg, db = _ln_bwd(dh_ref[...], xhat, rstd, g_ref[...])
        _acc(i, dg_ref, dg)
        _acc(i, db_ref, db)
        dfb = _bf(0.5 * dz)
        df_ref[...] = dfb
        dx = ALPHA * dz
        for k in range(2):
            gk = a_ref[:, k * fh:(k + 1) * fh].astype(F32)
            uk = a_ref[:, (2 + k) * fh:(3 + k) * fh].astype(F32)
            ds = _dot_nt(dfb, wo_ref[k * fh:(k + 1) * fh, :])
            sig = _sigmoid(gk)
            silu = gk * sig
            dgk = _bf(ds * uk * sig * (1.0 + gk * (1.0 - sig)))
            duk = _bf(ds * silu)
            s_ref[:, k * fh:(k + 1) * fh] = _bf(silu * uk)
            da_ref[:, k * fh:(k + 1) * fh] = dgk
            da_ref[:, (2 + k) * fh:(3 + k) * fh] = duk
            dx += _dot_nt(dgk, w4_ref[k]) + _dot_nt(duk, w4_ref[2 + k])
        dx_ref[...] = dx

    return _rowcall(name, body, T, tm, [dh, z, a], [w_in4, w_out, ln_g],
                    [(D, F32), (4 * fh, BF16), (2 * fh, BF16), (D, BF16)],
                    [((1, D), F32), ((1, D), F32)], exchange=exchange)


def _mm_tn(name, a, b, out_dtype=BF16, n_split=1):
    T, M = a.shape
    N = b.shape[1]
    tk = _tile(T, 2048, 8)
    tm = _tile(M, 1408)
    tn = _tile(N // n_split, 1536)
    per = N // n_split // tn
    nk = T // tk
    if n_split > 1:
        out_spec = pl.BlockSpec((None, tm, tn), lambda i, j, k: (j // per, i, j % per))
        out_shape = jax.ShapeDtypeStruct((n_split, M, N // n_split), out_dtype)
    else:
        out_spec = pl.BlockSpec((tm, tn), lambda i, j, k: (i, j))
        out_shape = jax.ShapeDtypeStruct((M, N), out_dtype)

    def kern(a_ref, b_ref, o_ref, acc_ref):
        k = pl.program_id(2)
        part = _dot_tn(_bf(a_ref[...]), _bf(b_ref[...]))

        @pl.when(k == 0)
        def _():
            acc_ref[...] = part

        @pl.when(k != 0)
        def _():
            acc_ref[...] += part

        @pl.when(k == nk - 1)
        def _():
            o_ref[...] = acc_ref[...].astype(out_dtype)

    return pl.pallas_call(
        kern, grid=(M // tm, N // tn, nk),
        in_specs=[pl.BlockSpec((tk, tm), lambda i, j, k: (k, i)), pl.BlockSpec((tk, tn), lambda i, j, k: (k, j))],
        out_specs=out_spec, out_shape=out_shape,
        scratch_shapes=[pltpu.VMEM((tm, tn), F32)], name=name,
        compiler_params=_params(("arbitrary", "arbitrary", "arbitrary")))(a, b)


def _proj_ret(h1, w_r, cos_r, sin_r, tm):
    T, D = h1.shape
    qk = RET_HEADS * RET_DK
    rv = RET_HEADS * RET_DV

    def body(i, h_ref, cos_ref, sin_ref, w_ref, q_ref, k_ref, v_ref, g_ref):
        hb = _bf(h_ref[...])
        cos, sin = cos_ref[...], sin_ref[...]
        for out_ref, off, scale in ((q_ref, 0, 1.0), (k_ref, qk, RET_DK ** -0.5)):
            pr = _dot(hb, w_ref[:, off:off + qk])
            for h in range(RET_HEADS):
                t = pr[:, h * RET_DK:(h + 1) * RET_DK]
                out_ref[:, h * RET_DK:(h + 1) * RET_DK] = _bf((t * cos + _roll(t, RET_DK // 2) * sin) * scale)
        v_ref[...] = _bf(_dot(hb, w_ref[:, 2 * qk:2 * qk + rv]))
        g_ref[...] = _dot(hb, w_ref[:, 2 * qk + rv:2 * qk + 2 * rv])

    return _rowcall("proj_ret", body, T, tm, [h1, cos_r, sin_r], [w_r],
                    [(qk, BF16), (qk, BF16), (rv, BF16), (rv, F32)])


def _rope_pe(t, c, s1, s2):
    return t * c + _roll(t, LANES - MLA_ROPE // 2) * s1 + _roll(t, MLA_ROPE // 2) * s2


def _rope_pe_bwd(dy, c, s1, s2):
    return dy * c + _roll(dy * s1, MLA_ROPE // 2) + _roll(dy * s2, LANES - MLA_ROPE // 2)


def _rms(x, g):
    r = lax.rsqrt(_mean(x * x) + EPS)
    return x * r, r


def _attn_block(T):
    return min(512, T)


def _transposed_blocks(T, tm, w, dtype):
    tb = _attn_block(T)
    per = tb // tm
    return (jax.ShapeDtypeStruct((T // tb, MLA_HEADS, w, tb), dtype),
            pl.BlockSpec((None, MLA_HEADS, w, tm), lambda i: (i // per, 0, 0, i % per)))


def _proj_mla(h1, tabs, w_c, w_kpe, w_g, w_uq, w_uk, w_uv, qn_g, kvn_g, tm):
    T, D = h1.shape
    H = MLA_HEADS

    def body(i, h_ref, c_ref, s1_ref, s2_ref, wc_ref, wk_ref, wg_ref, wuq_ref, wuk_ref, wuv_ref, qg_ref, kg_ref,
             lat_ref, gt_ref, q_ref, k_ref, v_ref, ln_ref, qt_ref, kt_ref, vt_ref):
        hb = _bf(h_ref[...])
        c, s1, s2 = c_ref[...], s1_ref[...], s2_ref[...]
        lat = _dot(hb, wc_ref[...])
        lat_ref[...] = lat
        gt_ref[...] = _dot(hb, wg_ref[...])
        cqn, _ = _rms(lat[:, :Q_LORA], None)
        ckn, _ = _rms(lat[:, Q_LORA:], None)
        cqn = _bf(cqn * qg_ref[...])
        ckn = _bf(ckn * kg_ref[...])
        ln_ref[:, :Q_LORA] = cqn
        ln_ref[:, Q_LORA:] = ckn
        q = _dot(cqn, wuq_ref[...])
        kn = _dot(ckn, wuk_ref[...])
        vv = _dot(ckn, wuv_ref[...])
        v_ref[...] = _bf(vv)
        kpe = _rope_pe(_dot(hb, wk_ref[...]), c, s1, s2)
        for h in range(H):
            o = h * MLA_QK
            qh = jnp.concatenate([q[:, o:o + MLA_NOPE], _rope_pe(q[:, o + MLA_NOPE:o + MLA_QK], c, s1, s2)], axis=1)
            kh = jnp.concatenate([kn[:, h * MLA_NOPE:(h + 1) * MLA_NOPE], kpe], axis=1)
            q_ref[:, o:o + MLA_QK] = _bf(qh)
            k_ref[:, o:o + MLA_QK] = _bf(kh)
            qt_ref[h] = _bf(qh.T)
            kt_ref[h] = _bf(kh.T)
            vt_ref[h] = _bf(vv[:, h * MLA_DV:(h + 1) * MLA_DV].T)

    lat_w = Q_LORA + KV_LORA
    return _rowcall("proj_mla", body, T, tm, [h1, *tabs], [w_c, w_kpe, w_g, w_uq, w_uk, w_uv, qn_g, kvn_g],
                    [(lat_w, F32), (2 * D, F32), (H * MLA_QK, BF16), (H * MLA_QK, BF16), (H * MLA_DV, BF16),
                     (lat_w, BF16)],
                    tiled_outs=[_transposed_blocks(T, tm, MLA_QK, BF16), _transposed_blocks(T, tm, MLA_QK, BF16),
                                _transposed_blocks(T, tm, MLA_DV, BF16)])


def _ret_block(T):
    return min(256, T)


RET_HEADS_PER_STEP = 8


def _ret_dmat(lg, bt):
    n = lax.broadcasted_iota(jnp.int32, (bt, bt), 0)
    m = lax.broadcasted_iota(jnp.int32, (bt, bt), 1)
    return jnp.where(_chunk_of(m) <= _chunk_of(n), jnp.exp(lg * jnp.abs(n - m).astype(F32)), 0.0)


def _ret_scan(name, per_head, lgam, ins, outs, rev):
    T = ins[0][0].shape[0]
    bt = _ret_block(T)
    nb = T // bt
    hps = min(RET_HEADS_PER_STEP, RET_HEADS)
    n_in, n_out = len(ins), len(outs)

    def kern(lg_ref, *refs):
        in_refs, out_refs = refs[:n_in], refs[n_in:n_in + n_out]
        state_ref, dmat_ref = refs[n_in + n_out:]

        @pl.when(pl.program_id(1) == 0)
        def _():
            state_ref[...] = jnp.zeros_like(state_ref)
            for hh in range(hps):
                dmat_ref[hh] = _ret_dmat(lg_ref[hh][:, :1], bt)

        pos = lax.broadcasted_iota(jnp.int32, (bt, 1), 0).astype(F32)
        for hh in range(hps):
            lg = lg_ref[hh][:, :1]
            xi, zeta, gb = jnp.exp(lg * (pos + 1.0)), jnp.exp(lg * (bt - 1.0 - pos)), jnp.exp(lg * bt)
            tiles = [r[:, hh * w:(hh + 1) * w] for r, (_, w) in zip(in_refs, ins)]
            res = per_head(dmat_ref[hh], xi, zeta, gb, state_ref.at[hh], *tiles)
            for o_ref, (w, _), val in zip(out_refs, outs, res):
                o_ref[:, hh * w:(hh + 1) * w] = val.astype(o_ref.dtype)

    def blk(w):
        if rev:
            return pl.BlockSpec((bt, hps * w), lambda g, b: (nb - 1 - b, g))
        return pl.BlockSpec((bt, hps * w), lambda g, b: (b, g))

    return pl.pallas_call(
        kern, grid=(RET_HEADS // hps, nb),
        in_specs=[pl.BlockSpec((hps, 1, LANES), lambda g, b: (g, 0, 0))] + [blk(w) for _, w in ins],
        out_specs=[blk(w) for w, _ in outs],
        out_shape=[jax.ShapeDtypeStruct((T, RET_HEADS * w), dt) for w, dt in outs],
        scratch_shapes=[pltpu.VMEM((hps, RET_DK, RET_DV), F32), pltpu.VMEM((hps, bt, bt), F32)], name=name,
        compiler_params=_params(("arbitrary", "arbitrary")))(lgam, *[a for a, _ in ins])


def _ret_fwd(rq, rk, rv, lgam):
    def per_head(dmat, xi, zeta, gb, s_ref, q, k, v):
        sc = _dot_nt(q, k) * dmat
        y = _dot(_bf(sc), v) + _dot(q, _bf(s_ref[...])) * xi
        s_ref[...] = s_ref[...] * gb + _dot_tn(_bf(k.astype(F32) * zeta), v)
        return (y,)

    return _ret_scan("ret_fwd", per_head, lgam, [(rq, RET_DK), (rk, RET_DK), (rv, RET_DV)], [(RET_DV, F32)], False)[0]


def _ret_bwd_q(rq, rk, rv, dy, lgam):
    def per_head(dmat, xi, zeta, gb, s_ref, k, v, dy):
        dp = _dot_nt(dy, v) * dmat
        dq = _dot(_bf(dp), k) + _dot_nt(dy, _bf(s_ref[...])) * xi
        s_ref[...] = s_ref[...] * gb + _dot_tn(_bf(k.astype(F32) * zeta), v)
        return (dq,)

    return _ret_scan("ret_bwd_q", per_head, lgam, [(rk, RET_DK), (rv, RET_DV), (dy, RET_DV)], [(RET_DK, F32)], False)[0]


def _ret_bwd_kv(rq, rk, rv, dy, lgam):
    def per_head(dmat, xi, zeta, gb, g_ref, q, k, v, dy):
        gs = _bf(g_ref[...])
        p = _dot_nt(q, k) * dmat
        dp = _dot_nt(dy, v) * dmat
        dv = _dot_tn(_bf(p), dy) + _dot(k, gs) * zeta
        dk = _dot_tn(_bf(dp), q) + _dot_nt(v, gs) * zeta
        g_ref[...] = g_ref[...] * gb + _dot_tn(_bf(q.astype(F32) * xi), dy)
        return dk, dv

    return _ret_scan("ret_bwd_kv", per_head, lgam, [(rq, RET_DK), (rk, RET_DK), (rv, RET_DV), (dy, RET_DV)],
                     [(RET_DK, F32), (RET_DV, BF16)], True)


def _attn_mask_t(tb):
    key = lax.broadcasted_iota(jnp.int32, (tb, tb), 0)
    qry = lax.broadcasted_iota(jnp.int32, (tb, tb), 1)
    return _chunk_of(key) <= _chunk_of(qry)


ATTN_SCALE = (MLA_NOPE + MLA_ROPE) ** -0.5
MASKED = -1e30
LOG2E = 1.4426950408889634
SUBLANES = 8


def _head_blocks(nb, w, tb):
    return pl.BlockSpec((nb, None, w, tb), lambda h, i: (0, h, 0, 0))


def _one_block(w, tb):
    return pl.BlockSpec((None, None, w, tb), lambda h, i: (i, h, 0, 0))


def _attn_fwd(k, qt, vt, exchange=()):
    T = k.shape[0]
    tb = _attn_block(T)
    nb = T // tb

    n_ex = len(exchange)

    def kern(qt_ref, k_ref, vt_ref, *refs):
        ex_in, (o_ref, lser_ref), ex_out = refs[:n_ex], refs[n_ex:n_ex + 2], refs[n_ex + 2:2 * n_ex + 2]
        m_ref, l_ref, acc_ref, sa_ref, sb_ref = refs[2 * n_ex + 2:2 * n_ex + 7]
        sems = refs[2 * n_ex + 7:]
        qb = pl.program_id(1)
        first = jnp.logical_and(pl.program_id(0) == 0, qb == 0)
        last = jnp.logical_and(pl.program_id(0) == MLA_HEADS - 1, qb == nb - 1)
        if n_ex:
            @pl.when(first)
            def _():
                for cp in _chip_copies(ex_in, ex_out, *sems, True):
                    cp.start()

        qt = qt_ref[...]
        m_ref[...] = jnp.full_like(m_ref, MASKED)
        l_ref[...] = jnp.zeros_like(l_ref)
        acc_ref[...] = jnp.zeros_like(acc_ref)

        def scores(kb):
            rows = pl.ds(pl.multiple_of(kb * tb, tb), tb)
            return _dot(k_ref[rows, :], qt) * (ATTN_SCALE * LOG2E)

        def update(s, kb):
            m_old = m_ref[...]
            m_new = jnp.maximum(m_old, jnp.max(s, axis=0, keepdims=True))
            p = jnp.exp2(s - m_new)
            corr = jnp.exp2(m_old - m_new)
            l_ref[...] = l_ref[...] * corr + jnp.sum(p, axis=0, keepdims=True)
            acc_ref[...] = acc_ref[...] * corr + _dot(vt_ref[kb], _bf(p))
            m_ref[...] = m_new

        def masked(s):
            return jnp.where(_attn_mask_t(tb), s, MASKED)

        sa_ref[...] = scores(0)

        def pair_body(j, carry):
            sb_ref[...] = scores(2 * j + 1)
            update(sa_ref[...], 2 * j)
            sa_ref[...] = scores(2 * j + 2)
            update(sb_ref[...], 2 * j + 1)
            return carry

        lax.fori_loop(0, qb // 2, pair_body, 0)

        @pl.when(qb % 2 == 0)
        def _():
            update(masked(sa_ref[...]), qb)

        @pl.when(qb % 2 == 1)
        def _():
            sb_ref[...] = masked(scores(qb))
            update(sa_ref[...], qb - 1)
            update(sb_ref[...], qb)

        o_ref[...] = (acc_ref[...] / l_ref[...]).T
        lser_ref[...] = jnp.broadcast_to(m_ref[...] + jnp.log2(l_ref[...]), (SUBLANES, tb))
        if n_ex:
            @pl.when(last)
            def _():
                _wait_copies(_chip_copies(ex_in, ex_out, *sems, True))

    return pl.pallas_call(
        kern, grid=(MLA_HEADS, nb),
        in_specs=[_one_block(MLA_QK, tb), pl.BlockSpec((T, MLA_QK), lambda h, i: (0, h)),
                  _head_blocks(nb, MLA_DV, tb)] + [HBM_SPEC] * n_ex,
        out_specs=[pl.BlockSpec((tb, MLA_DV), lambda h, i: (i, h)), _one_block(SUBLANES, tb)] + [HBM_SPEC] * n_ex,
        out_shape=[jax.ShapeDtypeStruct((T, MLA_HEADS * MLA_DV), F32),
                   jax.ShapeDtypeStruct((nb, MLA_HEADS, SUBLANES, tb), F32)] + _exchange_shapes(exchange),
        scratch_shapes=[pltpu.VMEM((1, tb), F32), pltpu.VMEM((1, tb), F32), pltpu.VMEM((MLA_DV, tb), F32),
                        pltpu.VMEM((tb, tb), F32), pltpu.VMEM((tb, tb), F32)]
        + (_dma_sems(n_ex * N_PEER_CHIPS) if n_ex else []),
        name="attn_fwd", compiler_params=_params(("arbitrary", "arbitrary")))(qt, k, vt, *exchange)


def _attn_bwd(q, k, v, do, qt, kt, dot_, lse_rows, delta_rows):
    T = q.shape[0]
    tb = _attn_block(T)
    nb = T // tb

    def kern(q_ref, k_ref, v_ref, do_ref, qt_ref, kt_ref, dot_ref, lse_ref, dl_ref, dk_ref, dv_ref, dqt_ref, dv_acc,
             sa_ref, pa_ref, sb_ref, pb_ref):
        kb = pl.program_id(1)
        kv, vv, ktv = k_ref[...], v_ref[...], kt_ref[...]
        dk_ref[...] = jnp.zeros_like(dk_ref)
        dv_acc[...] = jnp.zeros_like(dv_acc)

        @pl.when(kb == 0)
        def _():
            dqt_ref[...] = jnp.zeros_like(dqt_ref)

        def products(qb, s_ref, dp_ref, diagonal=False):
            s = _dot(kv, qt_ref[qb]) * (ATTN_SCALE * LOG2E)
            s_ref[...] = jnp.where(_attn_mask_t(tb), s, MASKED) if diagonal else s
            dp_ref[...] = _dot(vv, dot_ref[qb])

        def consume(qb, s_ref, dp_ref):
            rows = pl.ds(pl.multiple_of(qb * tb, tb), tb)
            p = jnp.exp2(s_ref[...] - lse_ref[qb][:1, :])
            dv_acc[...] += _dot(_bf(p), do_ref[rows, :])
            ds = _bf(p * (dp_ref[...] - dl_ref[qb][:1, :]) * ATTN_SCALE)
            dk_ref[...] += _dot(ds, q_ref[rows, :])
            dqt_ref[qb] += _dot(ktv, ds)

        n_full = nb - 1 - kb
        products(kb, sa_ref, pa_ref, diagonal=True)

        def pair_body(j, carry):
            q1 = kb + 1 + 2 * j
            products(q1, sb_ref, pb_ref)
            consume(q1 - 1, sa_ref, pa_ref)
            products(q1 + 1, sa_ref, pa_ref)
            consume(q1, sb_ref, pb_ref)
            return carry

        lax.fori_loop(0, n_full // 2, pair_body, 0)

        @pl.when(n_full % 2 == 0)
        def _():
            consume(nb - 1, sa_ref, pa_ref)

        @pl.when(n_full % 2 == 1)
        def _():
            products(nb - 1, sb_ref, pb_ref)
            consume(nb - 2, sa_ref, pa_ref)
            consume(nb - 1, sb_ref, pb_ref)

        dv_ref[...] = _bf(dv_acc[...])

    def blk(w):
        return pl.BlockSpec((tb, w), lambda h, i: (i, h))

    def full(w):
        return pl.BlockSpec((T, w), lambda h, i: (0, h))

    return pl.pallas_call(
        kern, grid=(MLA_HEADS, nb),
        in_specs=[full(MLA_QK), blk(MLA_QK), blk(MLA_DV), full(MLA_DV), _head_blocks(nb, MLA_QK, tb),
                  _one_block(MLA_QK, tb), _head_blocks(nb, MLA_DV, tb), _head_blocks(nb, SUBLANES, tb),
                  _head_blocks(nb, SUBLANES, tb)],
        out_specs=[blk(MLA_QK), blk(MLA_DV), _head_blocks(nb, MLA_QK, tb)],
        out_shape=[jax.ShapeDtypeStruct((T, MLA_HEADS * MLA_QK), F32),
                   jax.ShapeDtypeStruct((T, MLA_HEADS * MLA_DV), BF16),
                   jax.ShapeDtypeStruct((nb, MLA_HEADS, MLA_QK, tb), F32)],
        scratch_shapes=[pltpu.VMEM((tb, MLA_DV), F32)] + [pltpu.VMEM((tb, tb), F32)] * 4,
        name="attn_bwd", compiler_params=_params(("arbitrary", "arbitrary")))(
            q, k, v, do, qt, kt, dot_, lse_rows, delta_rows)


def _group_norm(y):
    yc = y - _mean(y)
    rstd = lax.rsqrt(_mean(yc * yc) + EPS)
    return yc * rstd, rstd


def _mix_fwd(y, rg, o, gates, h1, gn_g, w_ret_o, w_mla_o, w_out, ln_g, ln_b, tm):
    T, D = h1.shape

    def body(i, y_ref, rg_ref, o_ref, gt_ref, h_ref, gn_ref, wr_ref, wm_ref, wo_ref, g_ref, b_ref,
             h2_ref, z_ref, yret_ref, ymla_ref, yr_ref, mix_ref):
        for h in range(RET_HEADS):
            sl = slice(h * RET_DV, (h + 1) * RET_DV)
            yn, _ = _group_norm(y_ref[:, sl])
            r = rg_ref[:, sl]
            yr_ref[:, sl] = _bf(r * _sigmoid(r) * (yn * gn_ref[:, sl]))
        yret = _dot(yr_ref[...], wr_ref[...])
        ymla = _dot(_bf(o_ref[...]), wm_ref[...])
        yret_ref[...] = yret
        ymla_ref[...] = ymla
        mix = _bf(_sigmoid(gt_ref[:, :D]) * yret + _sigmoid(gt_ref[:, D:]) * ymla)
        mix_ref[...] = mix
        z = ALPHA * h_ref[...] + _dot(mix, wo_ref[...])
        xhat, _ = _ln_stats(z)
        z_ref[...] = z
        h2_ref[...] = xhat * g_ref[...] + b_ref[...]

    return _rowcall("mix_fwd", body, T, tm, [y, rg, o, gates, h1], [gn_g, w_ret_o, w_mla_o, w_out, ln_g, ln_b],
                    [(D, F32), (D, F32), (D, F32), (D, F32), (RET_HEADS * RET_DV, BF16), (D, BF16)])


def _mix_bwd(dh2, z1, gates, yret, ymla, y, rg, o, gn_g, w_ret_o, w_mla_o, w_out, ln_g, tm, exchange=None):
    T, D = dh2.shape
    rv = RET_HEADS * RET_DV

    def body(i, dh_ref, z_ref, gt_ref, yret_ref, ymla_ref, y_ref, rg_ref, o_ref, gn_ref, wr_ref, wm_ref, wo_ref, g_ref,
             dz_ref, dgt_ref, drg_ref, dy_ref, do_ref, dyret_ref, dymla_ref, dg_ref, db_ref, dgn_ref, dot_ref,
             dl_ref):
        xhat, rstd = _ln_stats(z_ref[...])
        dz, dg, db = _ln_bwd(dh_ref[...], xhat, rstd, g_ref[...])
        _acc(i, dg_ref, dg)
        _acc(i, db_ref, db)
        dz_ref[...] = dz
        dmix = _dot_nt(_bf(dz), wo_ref[...])
        sr = _sigmoid(gt_ref[:, :D])
        sm = _sigmoid(gt_ref[:, D:])
        dgt_ref[:, :D] = _bf(dmix * yret_ref[...] * sr * (1.0 - sr))
        dgt_ref[:, D:] = _bf(dmix * ymla_ref[...] * sm * (1.0 - sm))
        dyret = _bf(dmix * sr)
        dymla = _bf(dmix * sm)
        dyret_ref[...] = dyret
        dymla_ref[...] = dymla
        dov = _dot_nt(dymla, wm_ref[...])
        do_ref[...] = _bf(dov)
        for h in range(MLA_HEADS):
            sl = slice(h * MLA_DV, (h + 1) * MLA_DV)
            dot_ref[h] = _bf(dov[:, sl].T)
            delta = jnp.sum(dov[:, sl] * o_ref[:, sl], axis=-1, keepdims=True)
            dl_ref[h] = jnp.broadcast_to(delta, (tm, LANES)).T[:SUBLANES, :]
        dyr = _dot_nt(dyret, wr_ref[...])
        dgn = []
        for h in range(RET_HEADS):
            sl = slice(h * RET_DV, (h + 1) * RET_DV)
            yn, grstd = _group_norm(y_ref[:, sl])
            r = rg_ref[:, sl]
            sig = _sigmoid(r)
            d = dyr[:, sl]
            drg_ref[:, sl] = _bf(d * (yn * gn_ref[:, sl]) * sig * (1.0 + r * (1.0 - sig)))
            dt = d * (r * sig)
            dgn.append(jnp.sum(dt * yn, axis=0, keepdims=True))
            dyn = dt * gn_ref[:, sl]
            dy_ref[:, sl] = _bf(grstd * (dyn - _mean(dyn) - yn * _mean(dyn * yn)))
        _acc(i, dgn_ref, jnp.concatenate(dgn, axis=1))

    return _rowcall("mix_bwd", body, T, tm, [dh2, z1, gates, yret, ymla, y, rg, o],
                    [gn_g, w_ret_o, w_mla_o, w_out, ln_g],
                    [(D, F32), (2 * D, BF16), (rv, BF16), (rv, BF16), (MLA_HEADS * MLA_DV, BF16), (D, BF16), (D, BF16)],
                    [((1, D), F32), ((1, D), F32), ((1, rv), F32)],
                    tiled_outs=[_transposed_blocks(T, tm, MLA_DV, BF16), _transposed_blocks(T, tm, SUBLANES, F32)],
                    exchange=exchange)


def _proj_mla_bwd(dqt, dk, dv, lat, tabs, w_uq, w_uk, w_uv, qn_g, kvn_g, tm):
    T = dk.shape[0]
    H = MLA_HEADS
    lat_w = Q_LORA + KV_LORA

    def body(i, dk_ref, dv_ref, lat_ref, c_ref, s1_ref, s2_ref, dqt_ref, wuq_ref, wuk_ref, wuv_ref, qg_ref, kg_ref,
             dlat_ref, dkpe_ref, dqb_ref, dkn_ref, dqg_ref, dkg_ref):
        c, s1, s2 = c_ref[...], s1_ref[...], s2_ref[...]
        dkpe = jnp.zeros((tm, LANES), F32)
        for h in range(H):
            o = h * MLA_QK
            dqh = dqt_ref[h].T
            dqb_ref[:, o:o + MLA_NOPE] = _bf(dqh[:, :MLA_NOPE])
            dqb_ref[:, o + MLA_NOPE:o + MLA_QK] = _bf(_rope_pe_bwd(dqh[:, MLA_NOPE:], c, s1, s2))
            dkn_ref[:, h * MLA_NOPE:(h + 1) * MLA_NOPE] = _bf(dk_ref[:, o:o + MLA_NOPE])
            dkpe += dk_ref[:, o + MLA_NOPE:o + MLA_QK]
        dkpe_ref[...] = _bf(_rope_pe_bwd(dkpe, c, s1, s2))
        dcqn = _dot_nt(dqb_ref[...], wuq_ref[...])
        dckn = _dot_nt(dkn_ref[...], wuk_ref[...]) + _dot_nt(dv_ref[...], wuv_ref[...])
        for dn, x, g_ref, dg_ref, sl in ((dcqn, lat_ref[:, :Q_LORA], qg_ref, dqg_ref, slice(0, Q_LORA)),
                                         (dckn, lat_ref[:, Q_LORA:], kg_ref, dkg_ref, slice(Q_LORA, lat_w))):
            xn, r = _rms(x, None)
            _acc(i, dg_ref, jnp.sum(dn * xn, axis=0, keepdims=True))
            dxn = dn * g_ref[...]
            dlat_ref[:, sl] = _bf(r * (dxn - xn * _mean(dxn * xn)))

    dqt_shape, dqt_spec = _transposed_blocks(T, tm, MLA_QK, F32)
    assert dqt.shape == dqt_shape.shape
    return _rowcall("proj_mla_bwd", body, T, tm, [dk, dv, lat, *tabs], [w_uq, w_uk, w_uv, qn_g, kvn_g],
                    [(lat_w, BF16), (LANES, BF16), (H * MLA_QK, BF16), (H * MLA_NOPE, BF16)],
                    [((1, Q_LORA), F32), ((1, KV_LORA), F32)], tiled_ins=[(dqt, dqt_spec)])


def _proj_bwd(drq, drk, drv, drg, dz1, dlat, dkpe, dgates, cos_r, sin_r, w_r, w_c, w_kpe, w_g, tm):
    T, D = dz1.shape
    qk = RET_HEADS * RET_DK
    rv = RET_HEADS * RET_DV

    def body(i, drq_ref, drk_ref, drv_ref, drg_ref, dz_ref, dlat_ref, dkpe_ref, dgt_ref, cos_ref, sin_ref,
             wr_ref, wc_ref, wk_ref, wg_ref, dh_ref, dpr_ref):
        cos, sin = cos_ref[...], sin_ref[...]
        for src, off, scale in ((drq_ref, 0, 1.0), (drk_ref, qk, RET_DK ** -0.5)):
            for h in range(RET_HEADS):
                d = src[:, h * RET_DK:(h + 1) * RET_DK]
                dpr_ref[:, off + h * RET_DK:off + (h + 1) * RET_DK] = _bf(
                    (d * cos + _roll(d * sin, RET_DK // 2)) * scale)
        dpr_ref[:, 2 * qk:2 * qk + rv] = drv_ref[...]
        dpr_ref[:, 2 * qk + rv:] = drg_ref[...]
        dh_ref[...] = (ALPHA * dz_ref[...] + _dot_nt(dpr_ref[...], wr_ref[...]) + _dot_nt(dlat_ref[...], wc_ref[...])
                       + _dot_nt(dkpe_ref[...], wk_ref[...]) + _dot_nt(dgt_ref[...], wg_ref[...]))

    return _rowcall("proj_bwd", body, T, tm, [drq, drk, drv, drg, dz1, dlat, dkpe, dgates, cos_r, sin_r],
                    [w_r, w_c, w_kpe, w_g], [(D, F32), (2 * qk + 2 * rv, BF16)])


def _ple_loss(h3, p, target, w_gate, w_proj, ln_g, ln_b, tm):
    T, D = h3.shape

    def body(i, h_ref, p_ref, t_ref, wg_ref, wp_ref, g_ref, b_ref, dh_ref, dgp_ref, dpp_ref, loss_ref, dg_ref, db_ref):
        hv = h_ref[...]
        sg = _sigmoid(_dot(_bf(hv), wg_ref[...]))
        pp = _dot(_bf(p_ref[...]), wp_ref[...])
        xhat, rstd = _ln_stats(ALPHA * hv + sg * pp)
        err = xhat * g_ref[...] + b_ref[...] - t_ref[...]
        row_loss = 0.5 * _mean(err * err)
        _acc(i, loss_ref, jnp.broadcast_to(jnp.sum(row_loss, axis=0, keepdims=True), (1, LANES)))
        dz, dg, db = _ln_bwd(err * (1.0 / D), xhat, rstd, g_ref[...])
        _acc(i, dg_ref, dg)
        _acc(i, db_ref, db)
        dgp = _bf(dz * pp * sg * (1.0 - sg))
        dgp_ref[...] = dgp
        dpp_ref[...] = _bf(dz * sg)
        dh_ref[...] = ALPHA * dz + _dot_nt(dgp, wg_ref[...])

    return _rowcall("ple_loss", body, T, tm, [h3, p, target], [w_gate, w_proj, ln_g, ln_b],
                    [(D, F32), (D, BF16), (D, BF16)], [((1, LANES), F32), ((1, D), F32), ((1, D), F32)])


def _ewise(name, fn, ins, n_out, out_dtype=F32):
    r, c = ins[0].shape
    tr = _tile(r, max(8, (1 << 19) // c // 8 * 8), 8)

    def kern(*refs):
        outs = fn(*[x[...] for x in refs[:len(ins)]])
        for o_ref, o in zip(refs[len(ins):], outs):
            o_ref[...] = o.astype(out_dtype)

    spec = pl.BlockSpec((tr, c), lambda i: (i, 0))
    return pl.pallas_call(kern, grid=(r // tr,), in_specs=[spec] * len(ins), out_specs=[spec] * n_out,
                          out_shape=[jax.ShapeDtypeStruct((r, c), out_dtype)] * n_out, name=name,
                          compiler_params=_params(("arbitrary",)))(*ins)


def _adamw_math(w, g, m, v):
    m = ADAM_B1 * m + (1.0 - ADAM_B1) * g
    v = ADAM_B2 * v + (1.0 - ADAM_B2) * (g * g)
    m_hat = m / (1.0 - ADAM_B1 ** ADAM_STEP)
    v_hat = v / (1.0 - ADAM_B2 ** ADAM_STEP)
    return -ADAM_LR * (m_hat / (jnp.sqrt(v_hat) + ADAM_EPS) + ADAM_WD * w), m, v


def _adamw(name, w, g, m, v):
    shape = w.shape
    c = shape[-1]
    flat = [t.reshape(-1, c) for t in (w, g, m, v)]
    return [t.reshape(shape) for t in _ewise(name, _adamw_math, flat, 3)]


def _place():
    return lax.axis_index("x"), lax.axis_index("y"), lax.axis_index("c")


def _dma_sems(n):
    return [pltpu.SemaphoreType.DMA((n,)), pltpu.SemaphoreType.DMA((n,))]


N_PEER_CHIPS = N_CHIPS - 1


def _chips_exchange(name, srcs, broadcast):
    n = len(srcs)

    def kern(*refs):
        cps = _chip_copies(refs[:n], refs[n:2 * n], refs[2 * n], refs[2 * n + 1], broadcast)
        for cp in cps:
            cp.start()
        _wait_copies(cps)

    return pl.pallas_call(
        kern, out_shape=_exchange_shapes(srcs), in_specs=[HBM_SPEC] * n, out_specs=[HBM_SPEC] * n,
        scratch_shapes=_dma_sems(n * N_PEER_CHIPS), name=name)(*srcs)


def _exchange_shapes(srcs):
    return [jax.ShapeDtypeStruct((N_PEER_CHIPS,) + s.shape[1:], s.dtype) for s in srcs]


def _chip_copies(src_refs, out_refs, send_sems, recv_sems, broadcast):
    x, y, c = _place()
    peers = [(1 - x, y), (x, 1 - y), (1 - x, 1 - y)]
    cps = []
    for j, (px, py) in enumerate(peers):
        for a, (src_ref, out_ref) in enumerate(zip(src_refs, out_refs)):
            piece = src_ref.at[c] if broadcast else src_ref.at[2 * px + py]
            cps.append(pltpu.make_async_remote_copy(
                src_ref=piece, dst_ref=out_ref.at[j], send_sem=send_sems.at[a * N_PEER_CHIPS + j],
                recv_sem=recv_sems.at[a * N_PEER_CHIPS + j], device_id=(px, py, c), device_id_type=MESH))
    return cps


def _wait_copies(cps):
    for cp in cps:
        cp.wait_recv()
    for cp in cps:
        cp.wait_send()


def _sibling_swap(name, srcs, halves):
    n = len(srcs)

    def kern(*refs):
        src_refs, out_refs = refs[:n], refs[n:2 * n]
        send_sems, recv_sems = refs[2 * n:]
        x, y, c = _place()

        def copy(a):
            piece = src_refs[a].at[:, 1 - c] if halves else src_refs[a]
            return pltpu.make_async_remote_copy(
                src_ref=piece, dst_ref=out_refs[a], send_sem=send_sems.at[a], recv_sem=recv_sems.at[a],
                device_id=(x, y, 1 - c), device_id_type=MESH)

        cps = [copy(a) for a in range(n)]
        for cp in cps:
            cp.start()
        for cp in cps:
            cp.wait_recv()
        for cp in cps:
            cp.wait_send()

    def out_shape(s):
        return jax.ShapeDtypeStruct((s.shape[0],) + s.shape[2:] if halves else s.shape, s.dtype)

    return pl.pallas_call(
        kern, out_shape=[out_shape(s) for s in srcs], in_specs=[HBM_SPEC] * n, out_specs=[HBM_SPEC] * n,
        scratch_shapes=_dma_sems(n), name=name)(*srcs)


def _all_devices(name, src, reduce):
    r, c = src.shape
    n_dev = 2 * N_CHIPS

    def kern(src_ref, out_ref, *scratch):
        if reduce:
            gat_ref, send_sems, recv_sems = scratch
        else:
            gat_ref = out_ref
            send_sems, recv_sems = scratch
        x, y, cc = _place()
        me = 4 * x + 2 * y + cc
        gat_ref[me] = src_ref[...]
        peers = []
        for j in range(1, n_dev):
            px = 1 - x if j & 4 else x
            py = 1 - y if j & 2 else y
            pc = 1 - cc if j & 1 else cc
            peers.append((px, py, pc))

        def copy(j, peer, slot):
            return pltpu.make_async_remote_copy(
                src_ref=src_ref, dst_ref=gat_ref.at[slot], send_sem=send_sems.at[j], recv_sem=recv_sems.at[j],
                device_id=peer, device_id_type=MESH)

        sends = [copy(j, peer, me) for j, peer in enumerate(peers)]
        for cp in sends:
            cp.start()
        for j, (px, py, pc) in enumerate(peers):
            copy(j, (px, py, pc), 4 * px + 2 * py + pc).wait_recv()
        for cp in sends:
            cp.wait_send()
        if reduce:
            total = gat_ref[0]
            for d in range(1, n_dev):
                total = total + gat_ref[d]
            out_ref[...] = total

    out_shape = jax.ShapeDtypeStruct((r, c) if reduce else (n_dev, r, c), src.dtype)
    scratch = ([pltpu.VMEM((n_dev, r, c), src.dtype)] if reduce else []) + _dma_sems(n_dev - 1)
    return pl.pallas_call(kern, out_shape=out_shape, in_specs=[VMEM_SPEC], out_specs=VMEM_SPEC,
                          scratch_shapes=scratch, name=name)(src)


def _halves(t, axis):
    return t.reshape(t.shape[:axis] + (2, t.shape[axis] // 2) + t.shape[axis + 1:])


def _by_core(mine, theirs, axis):
    c = lax.axis_index("c")
    both = jnp.where(c == 0, jnp.stack([mine, theirs], axis), jnp.stack([theirs, mine], axis))
    return both.reshape(both.shape[:axis] + (2 * both.shape[axis + 1],) + both.shape[axis + 2:])


def _chip_order(own, others):
    me = 2 * lax.axis_index("x") + lax.axis_index("y")
    cands = jnp.concatenate([own[None], others], axis=0)
    slot_of_flip = (0, 2, 1, 3)
    pick = jnp.asarray(slot_of_flip, jnp.int32)[jnp.arange(N_CHIPS, dtype=jnp.int32) ^ me]
    return jnp.stack([lax.dynamic_index_in_dim(cands, pick[k], 0, keepdims=False) for k in range(N_CHIPS)])


def _join_shards(name, shards):
    _, r, c = shards.shape
    if name in COL_SHARDED:
        return shards.transpose(1, 0, 2).reshape(r, N_CHIPS * c)
    return shards.reshape(N_CHIPS * r, c)


def _split_shards(name, full):
    if full.ndim == 3:
        return full
    r, c = full.shape
    if name in COL_SHARDED:
        return jnp.stack([full[:, k * (c // N_CHIPS):(k + 1) * (c // N_CHIPS)] for k in range(N_CHIPS)])
    return full.reshape(N_CHIPS, r // N_CHIPS, c)


def _rope_tables(positions):
    pos = positions.reshape(-1).astype(F32)[:, None]
    half = RET_DK // 2
    ang = pos * (ROPE_BASE ** (-jnp.arange(half, dtype=F32) / half))
    cos_r = jnp.concatenate([jnp.cos(ang)] * 2, axis=1)
    sin_r = jnp.concatenate([-jnp.sin(ang), jnp.sin(ang)], axis=1)
    half = MLA_ROPE // 2
    ang = pos * (ROPE_BASE ** (-jnp.arange(half, dtype=F32) / half))
    zeros = jnp.zeros_like(ang)
    rest = LANES - MLA_ROPE
    c = jnp.concatenate([jnp.cos(ang)] * 2 + [jnp.ones((ang.shape[0], rest), F32)], axis=1)
    s1 = jnp.concatenate([-jnp.sin(ang), zeros, jnp.zeros((ang.shape[0], rest), F32)], axis=1)
    s2 = jnp.concatenate([zeros, jnp.sin(ang), jnp.zeros((ang.shape[0], rest), F32)], axis=1)
    return cos_r, sin_r, (c, s1, s2)


GATHER_GROUPS = (("ffn1_w_in", "ffn1_w_out"), ("w_in", "w_uq", "w_ukv"),
                 ("w_ret_o", "w_mla_o", "w_out", "ffn2_w_in", "ffn2_w_out", "ple_w_gate", "ple_w_proj"))
REDUCE_GROUPS = (("ple_w_gate", "ple_w_proj", "ffn2_w_in", "ffn2_w_out"),
                 ("w_out", "w_ret_o", "w_mla_o", "w_uq", "w_ukv", "w_in"), ("ffn1_w_in", "ffn1_w_out"))


def _gathered(tag, names, own, mine):
    theirs = _sibling_swap("gather_cores_" + tag, mine, False)
    out = {}
    for n, m, t in zip(names, mine, theirs):
        full = _chip_order(own[n], _by_core(m, t, 1))
        out[n] = full if n in ("ffn1_w_in", "ffn2_w_in") else _join_shards(n, full)
    return out


def _chip_sums(tag, names, grads):
    c = lax.axis_index("c")
    halves = [_halves(_split_shards(n, grads[n]), 1) for n in names]
    theirs = _sibling_swap("reduce_cores_" + tag, halves, True)
    sums = []
    for n, g, t in zip(names, halves, theirs):
        mine = lax.dynamic_index_in_dim(g, c, axis=1, keepdims=False)
        k, r, cc = mine.shape
        sums.append(_ewise("reduce_cores_add_" + n, lambda a, b: (a.astype(F32) + b.astype(F32),),
                           [mine.reshape(k * r, cc), t.reshape(k * r, cc)], 1, BF16)[0].reshape(k, r, cc))
    return sums


def _block_totals(names, sums, parts):
    me = 2 * lax.axis_index("x") + lax.axis_index("y")
    totals = []
    for n, s, pt in zip(names, sums, parts):
        own = lax.dynamic_index_in_dim(s, me, axis=0, keepdims=False)
        totals.append(_ewise("reduce_chips_add_" + n,
                             lambda a, b, c_, d: (((a.astype(F32) + b.astype(F32)) + c_.astype(F32)) + d.astype(F32),),
                             [own, pt[0], pt[1], pt[2]], 1, F32)[0])
    return totals


def _local_step(x, p, positions, target, shards, ln_g, ln_b, gn_g, qn_g, kvn_g):
    T, D = x.shape
    tm = min(256, T)
    H = MLA_HEADS
    qk, rv = RET_HEADS * RET_DK, RET_HEADS * RET_DV
    cos_r, sin_r, tabs = _rope_tables(positions)
    lgam = jnp.broadcast_to(jnp.log(1.0 - 2.0 ** (-5.0 - jnp.arange(RET_HEADS, dtype=F32)))[:, None, None],
                            (RET_HEADS, 1, LANES))
    lng = [ln_g[k:k + 1] for k in range(N_LN)]
    lnb = [ln_b[k:k + 1] for k in range(N_LN)]
    own = {n: _bf(shards[n]) for n in BIG_WEIGHTS}
    to_send = [[_halves(own[n], 0) for n in names] for names in GATHER_GROUPS]

    w = _gathered("a", GATHER_GROUPS[0], own, _chips_exchange("gather_chips_a", to_send[0], True))
    h1, z0, a1, *arrived = _ffn_fwd("ffn1_fwd", x, w["ffn1_w_in"], w["ffn1_w_out"], lng[0], lnb[0], tm,
                                    exchange=(to_send[1], True))
    w.update(_gathered("b", GATHER_GROUPS[1], own, arrived))

    w_in = w["w_in"]
    o_lat, o_kpe, o_gate = 2 * qk + 2 * rv, 2 * qk + 2 * rv + Q_LORA + KV_LORA, 2 * qk + 2 * rv + Q_LORA + KV_LORA + MLA_ROPE
    w_r, w_c = w_in[:, :o_lat], w_in[:, o_lat:o_kpe]
    w_kpe = jnp.pad(w_in[:, o_kpe:o_gate], ((0, 0), (0, LANES - MLA_ROPE)))
    w_g = w_in[:, o_gate:]
    w_uq = jnp.pad(w["w_uq"].reshape(Q_LORA, H, MLA_NOPE + MLA_ROPE),
                   ((0, 0), (0, 0), (0, MLA_QK - MLA_NOPE - MLA_ROPE))).reshape(Q_LORA, H * MLA_QK)
    w_ukv = w["w_ukv"].reshape(KV_LORA, H, MLA_NOPE + MLA_DV)
    w_uk = w_ukv[:, :, :MLA_NOPE].reshape(KV_LORA, H * MLA_NOPE)
    w_uv = w_ukv[:, :, MLA_NOPE:].reshape(KV_LORA, H * MLA_DV)

    rq, rk, rvv, rg = _proj_ret(h1, w_r, cos_r, sin_r, tm)
    lat, gates, q, k, v, latn, qt, kt, vt = _proj_mla(h1, tabs, w_c, w_kpe, w_g, w_uq, w_uk, w_uv, qn_g, kvn_g, tm)
    y = _ret_fwd(rq, rk, rvv, lgam)
    o, lse_rows, *arrived = _attn_fwd(k, qt, vt, exchange=to_send[2])
    w.update(_gathered("c", GATHER_GROUPS[2], own, arrived))
    h2, z1, yret, ymla, yr, mix = _mix_fwd(y, rg, o, gates, h1, gn_g, w["w_ret_o"], w["w_mla_o"], w["w_out"],
                                           lng[1], lnb[1], tm)
    h3, z2, a2 = _ffn_fwd("ffn2_fwd", h2, w["ffn2_w_in"], w["ffn2_w_out"], lng[2], lnb[2], tm)

    dh3, dgp, dpp, loss, dg3, db3 = _ple_loss(h3, p, target, w["ple_w_gate"], w["ple_w_proj"], lng[3], lnb[3], tm)
    dh2, da2, s2, df2, dg2, db2 = _ffn_bwd("ffn2_bwd", dh3, z2, a2, w["ffn2_w_in"], w["ffn2_w_out"], lng[2], tm)
    grads = {"ple_w_gate": _mm_tn("wg_ple_gate", h3, dgp), "ple_w_proj": _mm_tn("wg_ple_proj", p, dpp),
             "ffn2_w_in": _mm_tn("wg_ffn2_in", h2, da2, n_split=N_CHIPS), "ffn2_w_out": _mm_tn("wg_ffn2_out", s2, df2)}
    sums1 = _chip_sums("1", REDUCE_GROUPS[0], grads)
    (dz1, dgates, drg, dy, do, dyret, dymla, dg1, db1, dgn, dot_, delta_rows, *parts1) = _mix_bwd(
        dh2, z1, gates, yret, ymla, y, rg, o, gn_g, w["w_ret_o"], w["w_mla_o"], w["w_out"], lng[1], tm,
        exchange=(sums1, False))
    drq = _ret_bwd_q(rq, rk, rvv, dy, lgam)
    drk, drv = _ret_bwd_kv(rq, rk, rvv, dy, lgam)
    dk, dv, dqt = _attn_bwd(q, k, v, do, qt, kt, dot_, lse_rows, delta_rows)
    dlat, dkpe, dqb, dkn, dqg, dkg = _proj_mla_bwd(dqt, dk, dv, lat, tabs, w_uq, w_uk, w_uv, qn_g, kvn_g, tm)
    dh1, dpr = _proj_bwd(drq, drk, drv, drg, dz1, dlat, dkpe, dgates, cos_r, sin_r, w_r, w_c, w_kpe, w_g, tm)
    g_uq = _mm_tn("wg_uq", latn[:, :Q_LORA], dqb).reshape(Q_LORA, H, MLA_QK)[:, :, :MLA_NOPE + MLA_ROPE]
    g_uk = _mm_tn("wg_uk", latn[:, Q_LORA:], dkn).reshape(KV_LORA, H, MLA_NOPE)
    g_uv = _mm_tn("wg_uv", latn[:, Q_LORA:], dv).reshape(KV_LORA, H, MLA_DV)
    grads.update({
        "w_in": jnp.concatenate([_mm_tn("wg_in_r", h1, dpr), _mm_tn("wg_in_c", h1, dlat),
                                 _mm_tn("wg_in_kpe", h1, dkpe)[:, :MLA_ROPE], _mm_tn("wg_in_g", h1, dgates)], axis=1),
        "w_ret_o": _mm_tn("wg_ret_o", yr, dyret),
        "w_uq": g_uq.reshape(Q_LORA, H * (MLA_NOPE + MLA_ROPE)),
        "w_ukv": jnp.concatenate([g_uk, g_uv], axis=2).reshape(KV_LORA, H * (MLA_NOPE + MLA_DV)),
        "w_mla_o": _mm_tn("wg_mla_o", o, dymla),
        "w_out": _mm_tn("wg_out", mix, dz1)})
    sums2 = _chip_sums("2", REDUCE_GROUPS[1], grads)
    dx, da1, s1, df1, dg0, db0, *parts2 = _ffn_bwd("ffn1_bwd", dh1, z0, a1, w["ffn1_w_in"], w["ffn1_w_out"], lng[0], tm,
                                                   exchange=(sums2, False))
    grads.update({"ffn1_w_in": _mm_tn("wg_ffn1_in", x, da1, n_split=N_CHIPS),
                  "ffn1_w_out": _mm_tn("wg_ffn1_out", s1, df1)})
    sums3 = _chip_sums("3", REDUCE_GROUPS[2], grads)
    parts3 = _chips_exchange("reduce_chips_3", sums3, False)

    names = [n for group in REDUCE_GROUPS for n in group]
    totals = _block_totals(names, sums1 + sums2 + sums3, list(parts1) + list(parts2) + list(parts3))
    others = _sibling_swap("reduce_join", totals, False)
    reduced = {n: _by_core(t, o_, 0) for n, t, o_ in zip(names, totals, others)}
    small = {"ln_g": jnp.concatenate([dg0, dg1, dg2, dg3], axis=0), "ln_b": jnp.concatenate([db0, db1, db2, db3], axis=0),
             "ret_gn_g": dgn, "q_norm_g": dqg, "kv_norm_g": dkg}
    return loss[0, 0], dx, reduced, small


def kernel(x, p, positions, ln_g, ln_b, ffn1_w_in, ffn1_w_out, w_in, ret_gn_g, w_ret_o, q_norm_g, kv_norm_g, w_uq, w_ukv, w_mla_o, w_out, ffn2_w_in, ffn2_w_out, ple_w_gate, ple_w_proj, loss_target, m_ln_g, m_ln_b, m_ffn1_w_in, m_ffn1_w_out, m_w_in, m_ret_gn_g, m_w_ret_o, m_q_norm_g, m_kv_norm_g, m_w_uq, m_w_ukv, m_w_mla_o, m_w_out, m_ffn2_w_in, m_ffn2_w_out, m_ple_w_gate, m_ple_w_proj, v_ln_g, v_ln_b, v_ffn1_w_in, v_ffn1_w_out, v_w_in, v_ret_gn_g, v_w_ret_o, v_q_norm_g, v_kv_norm_g, v_w_uq, v_w_ukv, v_w_mla_o, v_w_out, v_ffn2_w_in, v_ffn2_w_out, v_ple_w_gate, v_ple_w_proj):
    names = ("ln_g", "ln_b", "ffn1_w_in", "ffn1_w_out", "w_in", "ret_gn_g", "w_ret_o", "q_norm_g", "kv_norm_g", "w_uq",
             "w_ukv", "w_mla_o", "w_out", "ffn2_w_in", "ffn2_w_out", "ple_w_gate", "ple_w_proj")
    weights = dict(zip(names, (ln_g, ln_b, ffn1_w_in, ffn1_w_out, w_in, ret_gn_g, w_ret_o, q_norm_g, kv_norm_g, w_uq,
                               w_ukv, w_mla_o, w_out, ffn2_w_in, ffn2_w_out, ple_w_gate, ple_w_proj)))
    m_in = dict(zip(names, (m_ln_g, m_ln_b, m_ffn1_w_in, m_ffn1_w_out, m_w_in, m_ret_gn_g, m_w_ret_o, m_q_norm_g,
                            m_kv_norm_g, m_w_uq, m_w_ukv, m_w_mla_o, m_w_out, m_ffn2_w_in, m_ffn2_w_out, m_ple_w_gate,
                            m_ple_w_proj)))
    v_in = dict(zip(names, (v_ln_g, v_ln_b, v_ffn1_w_in, v_ffn1_w_out, v_w_in, v_ret_gn_g, v_w_ret_o, v_q_norm_g,
                            v_kv_norm_g, v_w_uq, v_w_ukv, v_w_mla_o, v_w_out, v_ffn2_w_in, v_ffn2_w_out, v_ple_w_gate,
                            v_ple_w_proj)))
    chip = 2 * lax.axis_index("x") + lax.axis_index("y")
    D = x.shape[-1]
    dq = D // N_CHIPS

    shards = {n: weights[n][0] for n in BIG_WEIGHTS}
    ln_all = _all_devices("gather_ln", jnp.concatenate([ln_g[0], ln_b[0]], axis=0), False)
    ln_full = ln_all[::2].transpose(1, 0, 2).reshape(2 * N_LN, D)
    loss, dx, big, small = _local_step(x[0], p[0, 0], positions, loss_target[0], shards, ln_full[:N_LN],
                                       ln_full[N_LN:], ret_gn_g, q_norm_g, kv_norm_g)

    loss = lax.psum(loss, ("x", "y", "c"))
    small_names = ("ln_g", "ln_b", "ret_gn_g", "q_norm_g", "kv_norm_g")
    flat = jnp.concatenate([small[n].reshape(-1) for n in small_names])
    rows = -(-flat.shape[0] // LANES // 8) * 8
    flat = jnp.pad(flat, (0, rows * LANES - flat.shape[0])).reshape(rows, LANES)
    flat = _all_devices("reduce_small", flat, True).reshape(-1)
    off = 0
    for n in small_names:
        size = small[n].size
        small[n] = flat[off:off + size].reshape(small[n].shape)
        off += size
    g_out = dict(big)
    for n in ("ln_g", "ln_b"):
        g_out[n] = lax.dynamic_slice_in_dim(small[n], chip * dq, dq, axis=1)
    for n in ("ret_gn_g", "q_norm_g", "kv_norm_g"):
        g_out[n] = small[n]

    deltas, new_m, new_v = {}, {}, {}
    for n in names:
        g = g_out[n].reshape(weights[n].shape)
        g_out[n] = g
        deltas[n], new_m[n], new_v[n] = _adamw("adamw_" + n, weights[n], g, m_in[n], v_in[n])
    return (loss, dx[None], *[g_out[n] for n in names], *[deltas[n] for n in names], *[new_m[n] for n in names],
            *[new_v[n] for n in names])
```

```python
import functools

import jax
import jax.numpy as jnp
from jax import lax
from jax.experimental import pallas as pl
from jax.experimental.pallas import tpu as pltpu

D_MODEL = 1024
CHUNK = 64
D_PLE = 256
D_FF = 2816
RET_HEADS = 8
RET_DK = 128
RET_DV = 256
MLA_HEADS = 8
MLA_NOPE = 128
MLA_ROPE = 64
MLA_DV = 128
MLA_QK = 256
Q_LORA = 256
KV_LORA = 256
ROPE_BASE = 10000.0
EPS = 1e-5
N_LN = 4
ALPHA = 2.0 ** 0.25
ADAM_LR = 0.001
ADAM_B1 = 0.9
ADAM_B2 = 0.999
ADAM_EPS = 1e-08
ADAM_WD = 0.01
ADAM_STEP = 10

LANES = 128
VMEM_LIMIT = 60 << 20
N_CHIPS = 4

F32 = jnp.float32
BF16 = jnp.bfloat16
MESH = pl.DeviceIdType.MESH
HBM_SPEC = pl.BlockSpec(memory_space=pltpu.HBM)
VMEM_SPEC = pl.BlockSpec(memory_space=pltpu.VMEM)

BIG_WEIGHTS = ("ffn1_w_in", "ffn1_w_out", "w_in", "w_ret_o", "w_uq", "w_ukv", "w_mla_o", "w_out",
               "ffn2_w_in", "ffn2_w_out", "ple_w_gate", "ple_w_proj")
COL_SHARDED = ("ffn1_w_in", "w_in", "w_uq", "w_ukv", "ffn2_w_in", "ple_w_proj")


def _dot(a, b):
    return jnp.dot(a, b, preferred_element_type=F32)


def _dot_nt(a, b):
    return lax.dot_general(a, b, (((1,), (1,)), ((), ())), preferred_element_type=F32)


def _dot_tn(a, b):
    return lax.dot_general(a, b, (((0,), (0,)), ((), ())), preferred_element_type=F32)


def _bf(x):
    return x.astype(BF16)


def _sigmoid(x):
    return 0.5 * jnp.tanh(0.5 * x) + 0.5


def _mean(x):
    return jnp.mean(x, axis=-1, keepdims=True)


def _ln_stats(z):
    zc = z - _mean(z)
    rstd = lax.rsqrt(_mean(zc * zc) + EPS)
    return zc * rstd, rstd


def _ln_bwd(dy, xhat, rstd, g):
    dxhat = dy * g
    dz = rstd * (dxhat - _mean(dxhat) - xhat * _mean(dxhat * xhat))
    return dz, jnp.sum(dy * xhat, axis=0, keepdims=True), jnp.sum(dy, axis=0, keepdims=True)


def _roll(x, shift):
    return pltpu.roll(x, shift, 1)


def _chunk_of(idx):
    return jnp.right_shift(idx, CHUNK.bit_length() - 1)


def _tile(n, cap, mult=LANES):
    if n <= cap:
        return n
    for t in range(cap - cap % mult, 0, -mult):
        if n % t == 0:
            return t
    return n


def _zero_map(nd, *_):
    return (0,) * nd


def _params(sem):
    return pltpu.CompilerParams(dimension_semantics=sem, vmem_limit_bytes=VMEM_LIMIT)


def _rowcall(name, body, n_rows, tm, row_ins, full_ins, row_outs, acc_outs=(), tiled_outs=(), tiled_ins=(),
             exchange=None):
    n_steps = n_rows // tm
    ex_srcs, broadcast = exchange if exchange else ((), False)
    n_ex = len(ex_srcs)
    n_in = len(row_ins) + len(tiled_ins) + len(full_ins)
    n_out = len(row_outs) + len(acc_outs) + len(tiled_outs)

    def kern(*refs):
        step = pl.program_id(0)
        ex_in, ex_out = refs[n_in:n_in + n_ex], refs[n_in + n_ex + n_out:n_in + 2 * n_ex + n_out]
        sems = refs[n_in + 2 * n_ex + n_out:]
        if n_ex:
            @pl.when(step == 0)
            def _():
                for cp in _chip_copies(ex_in, ex_out, *sems, broadcast):
                    cp.start()

        body(step, *refs[:n_in], *refs[n_in + n_ex:n_in + n_ex + n_out])
        if n_ex:
            @pl.when(step == n_steps - 1)
            def _():
                _wait_copies(_chip_copies(ex_in, ex_out, *sems, broadcast))

    in_specs = [pl.BlockSpec((tm, a.shape[1]), lambda i: (i, 0)) for a in row_ins]
    in_specs += [spec for (_, spec) in tiled_ins]
    row_ins = list(row_ins) + [a for (a, _) in tiled_ins]
    in_specs += [pl.BlockSpec(a.shape, functools.partial(_zero_map, a.ndim), pipeline_mode=pl.Buffered(1))
                 for a in full_ins]
    in_specs += [HBM_SPEC] * n_ex
    out_specs = [pl.BlockSpec((tm, w), lambda i: (i, 0)) for (w, _) in row_outs]
    out_specs += [pl.BlockSpec(s, functools.partial(_zero_map, len(s))) for (s, _) in acc_outs]
    out_specs += [spec for (_, spec) in tiled_outs]
    out_specs += [HBM_SPEC] * n_ex
    out_shape = [jax.ShapeDtypeStruct((n_rows, w), dt) for (w, dt) in row_outs]
    out_shape += [jax.ShapeDtypeStruct(s, dt) for (s, dt) in acc_outs]
    out_shape += [shape for (shape, _) in tiled_outs]
    out_shape += _exchange_shapes(ex_srcs)
    return pl.pallas_call(kern, grid=(n_steps,), in_specs=in_specs, out_specs=out_specs, out_shape=out_shape,
                          scratch_shapes=_dma_sems(n_ex * N_PEER_CHIPS) if n_ex else [], name=name,
                          compiler_params=_params(("arbitrary",)))(*row_ins, *full_ins, *ex_srcs)


def _acc(step, ref, val):
    @pl.when(step == 0)
    def _():
        ref[...] = val

    @pl.when(step != 0)
    def _():
        ref[...] += val


def _ffn_fwd(name, x, w_in4, w_out, ln_g, ln_b, tm, exchange=None):
    T, D = x.shape
    fh = w_in4.shape[2]

    def body(i, x_ref, w4_ref, wo_ref, g_ref, b_ref, h_ref, z_ref, a_ref):
        xv = x_ref[...]
        xb = _bf(xv)
        f = jnp.zeros((tm, D), F32)
        for k in range(2):
            gk = _dot(xb, w4_ref[k])
            uk = _dot(xb, w4_ref[2 + k])
            a_ref[:, k * fh:(k + 1) * fh] = _bf(gk)
            a_ref[:, (2 + k) * fh:(3 + k) * fh] = _bf(uk)
            f += _dot(_bf(gk * _sigmoid(gk) * uk), wo_ref[k * fh:(k + 1) * fh, :])
        z = ALPHA * xv + 0.5 * f
        xhat, _ = _ln_stats(z)
        z_ref[...] = z
        h_ref[...] = xhat * g_ref[...] + b_ref[...]

    return _rowcall(name, body, T, tm, [x], [w_in4, w_out, ln_g, ln_b],
                    [(D, F32), (D, F32), (4 * fh, BF16)], exchange=exchange)


def _ffn_bwd(name, dh, z, a, w_in4, w_out, ln_g, tm, exchange=None):
    T, D = dh.shape
    fh = w_in4.shape[2]

    def body(i, dh_ref, z_ref, a_ref, w4_ref, wo_ref, g_ref, dx_ref, da_ref, s_ref, df_ref, dg_ref, db_ref):
        xhat, rstd = _ln_stats(z_ref[...])
        dz, dg, db = _ln_bwd(dh_ref[...], xhat, rstd, g_ref[...])
        _acc(i, dg_ref, dg)
        _acc(i, db_ref, db)
        dfb = _bf(0.5 * dz)
        df_ref[...] = dfb
        dx = ALPHA * dz
        for k in range(2):
            gk = a_ref[:, k * fh:(k + 1) * fh].astype(F32)
            uk = a_ref[:, (2 + k) * fh:(3 + k) * fh].astype(F32)
            ds = _dot_nt(dfb, wo_ref[k * fh:(k + 1) * fh, :])
            sig = _sigmoid(gk)
            silu = gk * sig
            dgk = _bf(ds * uk * sig * (1.0 + gk * (1.0 - sig)))
            duk = _bf(ds * silu)
            s_ref[:, k * fh:(k + 1) * fh] = _bf(silu * uk)
            da_ref[:, k * fh:(k + 1) * fh] = dgk
            da_ref[:, (2 + k) * fh:(3 + k) * fh] = duk
            dx += _dot_nt(dgk, w4_ref[k]) + _dot_nt(duk, w4_ref[2 + k])
        dx_ref[...] = dx

    return _rowcall(name, body, T, tm, [dh, z, a], [w_in4, w_out, ln_g],
                    [(D, F32), (4 * fh, BF16), (2 * fh, BF16), (D, BF16)],
                    [((1, D), F32), ((1, D), F32)], exchange=exchange)


def _mm_tn(name, a, b, out_dtype=BF16, n_split=1):
    T, M = a.shape
    N = b.shape[1]
    tk = _tile(T, 2048, 8)
    tm = _tile(M, 1408)
    tn = _tile(N // n_split, 1536)
    per = N // n_split // tn
    nk = T // tk
    if n_split > 1:
        out_spec = pl.BlockSpec((None, tm, tn), lambda i, j, k: (j // per, i, j % per))
        out_shape = jax.ShapeDtypeStruct((n_split, M, N // n_split), out_dtype)
    else:
        out_spec = pl.BlockSpec((tm, tn), lambda i, j, k: (i, j))
        out_shape = jax.ShapeDtypeStruct((M, N), out_dtype)

    def kern(a_ref, b_ref, o_ref, acc_ref):
        k = pl.program_id(2)
        part = _dot_tn(_bf(a_ref[...]), _bf(b_ref[...]))

        @pl.when(k == 0)
        def _():
            acc_ref[...] = part

        @pl.when(k != 0)
        def _():
            acc_ref[...] += part

        @pl.when(k == nk - 1)
        def _():
            o_ref[...] = acc_ref[...].astype(out_dtype)

    return pl.pallas_call(
        kern, grid=(M // tm, N // tn, nk),
        in_specs=[pl.BlockSpec((tk, tm), lambda i, j, k: (k, i)), pl.BlockSpec((tk, tn), lambda i, j, k: (k, j))],
        out_specs=out_spec, out_shape=out_shape,
        scratch_shapes=[pltpu.VMEM((tm, tn), F32)], name=name,
        compiler_params=_params(("arbitrary", "arbitrary", "arbitrary")))(a, b)


def _proj_ret(h1, w_r, cos_r, sin_r, tm):
    T, D = h1.shape
    qk = RET_HEADS * RET_DK
    rv = RET_HEADS * RET_DV

    def body(i, h_ref, cos_ref, sin_ref, w_ref, q_ref, k_ref, v_ref, g_ref):
        hb = _bf(h_ref[...])
        cos, sin = cos_ref[...], sin_ref[...]
        for out_ref, off, scale in ((q_ref, 0, 1.0), (k_ref, qk, RET_DK ** -0.5)):
            pr = _dot(hb, w_ref[:, off:off + qk])
            for h in range(RET_HEADS):
                t = pr[:, h * RET_DK:(h + 1) * RET_DK]
                out_ref[:, h * RET_DK:(h + 1) * RET_DK] = _bf((t * cos + _roll(t, RET_DK // 2) * sin) * scale)
        v_ref[...] = _bf(_dot(hb, w_ref[:, 2 * qk:2 * qk + rv]))
        g_ref[...] = _dot(hb, w_ref[:, 2 * qk + rv:2 * qk + 2 * rv])

    return _rowcall("proj_ret", body, T, tm, [h1, cos_r, sin_r], [w_r],
                    [(qk, BF16), (qk, BF16), (rv, BF16), (rv, F32)])


def _rope_pe(t, c, s1, s2):
    return t * c + _roll(t, LANES - MLA_ROPE // 2) * s1 + _roll(t, MLA_ROPE // 2) * s2


def _rope_pe_bwd(dy, c, s1, s2):
    return dy * c + _roll(dy * s1, MLA_ROPE // 2) + _roll(dy * s2, LANES - MLA_ROPE // 2)


def _rms(x, g):
    r = lax.rsqrt(_mean(x * x) + EPS)
    return x * r, r


def _attn_block(T):
    return min(512, T)


def _transposed_blocks(T, tm, w, dtype):
    tb = _attn_block(T)
    per = tb // tm
    return (jax.ShapeDtypeStruct((T // tb, MLA_HEADS, w, tb), dtype),
            pl.BlockSpec((None, MLA_HEADS, w, tm), lambda i: (i // per, 0, 0, i % per)))


ATTN_SCALE = (MLA_NOPE + MLA_ROPE) ** -0.5
LOG2E = 1.4426950408889634
Q_PRESCALE = ATTN_SCALE * LOG2E
V_ONES = 16


def _proj_mla(h1, tabs, w_c, w_kpe, w_g, w_uq, w_uk, w_uv, qn_g, kvn_g, tm):
    T, D = h1.shape
    H = MLA_HEADS

    def body(i, h_ref, c_ref, s1_ref, s2_ref, wc_ref, wk_ref, wg_ref, wuq_ref, wuk_ref, wuv_ref, qg_ref, kg_ref,
             lat_ref, gt_ref, q_ref, k_ref, v_ref, ln_ref, qt_ref, kt_ref, vt_ref):
        hb = _bf(h_ref[...])
        c, s1, s2 = c_ref[...], s1_ref[...], s2_ref[...]
        lat = _dot(hb, wc_ref[...])
        lat_ref[...] = lat
        gt_ref[...] = _dot(hb, wg_ref[...])
        cqn, _ = _rms(lat[:, :Q_LORA], None)
        ckn, _ = _rms(lat[:, Q_LORA:], None)
        cqn = _bf(cqn * qg_ref[...])
        ckn = _bf(ckn * kg_ref[...])
        ln_ref[:, :Q_LORA] = cqn
        ln_ref[:, Q_LORA:] = ckn
        q = _dot(cqn, wuq_ref[...])
        kn = _dot(ckn, wuk_ref[...])
        vv = _dot(ckn, wuv_ref[...])
        v_ref[...] = _bf(vv)
        kpe = _rope_pe(_dot(hb, wk_ref[...]), c, s1, s2)
        ones = jnp.ones((V_ONES, tm), BF16)
        for h in range(H):
            o = h * MLA_QK
            qh = jnp.concatenate([q[:, o:o + MLA_NOPE], _rope_pe(q[:, o + MLA_NOPE:o + MLA_QK], c, s1, s2)], axis=1)
            qh = qh * Q_PRESCALE
            kh = jnp.concatenate([kn[:, h * MLA_NOPE:(h + 1) * MLA_NOPE], kpe], axis=1)
            q_ref[:, o:o + MLA_QK] = _bf(qh)
            k_ref[:, o:o + MLA_QK] = _bf(kh)
            qt_ref[h] = _bf(qh.T)
            kt_ref[h] = _bf(kh.T)
            vt_ref[h] = jnp.concatenate([_bf(vv[:, h * MLA_DV:(h + 1) * MLA_DV].T), ones], axis=0)

    lat_w = Q_LORA + KV_LORA
    return _rowcall("proj_mla", body, T, tm, [h1, *tabs], [w_c, w_kpe, w_g, w_uq, w_uk, w_uv, qn_g, kvn_g],
                    [(lat_w, F32), (2 * D, F32), (H * MLA_QK, BF16), (H * MLA_QK, BF16), (H * MLA_DV, BF16),
                     (lat_w, BF16)],
                    tiled_outs=[_transposed_blocks(T, tm, MLA_QK, BF16), _transposed_blocks(T, tm, MLA_QK, BF16),
                                _transposed_blocks(T, tm, MLA_DV + V_ONES, BF16)])


def _ret_block(T):
    return min(256, T)


RET_HEADS_PER_STEP = 8


def _ret_dmat(lg, bt):
    n = lax.broadcasted_iota(jnp.int32, (bt, bt), 0)
    m = lax.broadcasted_iota(jnp.int32, (bt, bt), 1)
    return jnp.where(_chunk_of(m) <= _chunk_of(n), jnp.exp(lg * jnp.abs(n - m).astype(F32)), 0.0)


def _ret_scan(name, per_head, lgam, ins, outs, rev):
    T = ins[0][0].shape[0]
    bt = _ret_block(T)
    nb = T // bt
    hps = min(RET_HEADS_PER_STEP, RET_HEADS)
    n_in, n_out = len(ins), len(outs)

    def kern(lg_ref, *refs):
        in_refs, out_refs = refs[:n_in], refs[n_in:n_in + n_out]
        state_ref, dmat_ref = refs[n_in + n_out:]

        @pl.when(pl.program_id(1) == 0)
        def _():
            state_ref[...] = jnp.zeros_like(state_ref)
            for hh in range(hps):
                dmat_ref[hh] = _ret_dmat(lg_ref[hh][:, :1], bt)

        pos = lax.broadcasted_iota(jnp.int32, (bt, 1), 0).astype(F32)
        for hh in range(hps):
            lg = lg_ref[hh][:, :1]
            xi, zeta, gb = jnp.exp(lg * (pos + 1.0)), jnp.exp(lg * (bt - 1.0 - pos)), jnp.exp(lg * bt)
            tiles = [r[:, hh * w:(hh + 1) * w] for r, (_, w) in zip(in_refs, ins)]
            res = per_head(dmat_ref[hh], xi, zeta, gb, state_ref.at[hh], *tiles)
            for o_ref, (w, _), val in zip(out_refs, outs, res):
                o_ref[:, hh * w:(hh + 1) * w] = val.astype(o_ref.dtype)

    def blk(w):
        if rev:
            return pl.BlockSpec((bt, hps * w), lambda g, b: (nb - 1 - b, g))
        return pl.BlockSpec((bt, hps * w), lambda g, b: (b, g))

    return pl.pallas_call(
        kern, grid=(RET_HEADS // hps, nb),
        in_specs=[pl.BlockSpec((hps, 1, LANES), lambda g, b: (g, 0, 0))] + [blk(w) for _, w in ins],
        out_specs=[blk(w) for w, _ in outs],
        out_shape=[jax.ShapeDtypeStruct((T, RET_HEADS * w), dt) for w, dt in outs],
        scratch_shapes=[pltpu.VMEM((hps, RET_DK, RET_DV), F32), pltpu.VMEM((hps, bt, bt), F32)], name=name,
        compiler_params=_params(("arbitrary", "arbitrary")))(lgam, *[a for a, _ in ins])


def _ret_fwd(rq, rk, rv, lgam):
    def per_head(dmat, xi, zeta, gb, s_ref, q, k, v):
        sc = _dot_nt(q, k) * dmat
        y = _dot(_bf(sc), v) + _dot(q, _bf(s_ref[...])) * xi
        s_ref[...] = s_ref[...] * gb + _dot_tn(_bf(k.astype(F32) * zeta), v)
        return (y,)

    return _ret_scan("ret_fwd", per_head, lgam, [(rq, RET_DK), (rk, RET_DK), (rv, RET_DV)], [(RET_DV, F32)], False)[0]


def _ret_bwd_q(rq, rk, rv, dy, lgam):
    def per_head(dmat, xi, zeta, gb, s_ref, k, v, dy):
        dp = _dot_nt(dy, v) * dmat
        dq = _dot(_bf(dp), k) + _dot_nt(dy, _bf(s_ref[...])) * xi
        s_ref[...] = s_ref[...] * gb + _dot_tn(_bf(k.astype(F32) * zeta), v)
        return (dq,)

    return _ret_scan("ret_bwd_q", per_head, lgam, [(rk, RET_DK), (rv, RET_DV), (dy, RET_DV)], [(RET_DK, F32)], False)[0]


def _ret_bwd_kv(rq, rk, rv, dy, lgam):
    def per_head(dmat, xi, zeta, gb, g_ref, q, k, v, dy):
        gs = _bf(g_ref[...])
        p = _dot_nt(q, k) * dmat
        dp = _dot_nt(dy, v) * dmat
        dv = _dot_tn(_bf(p), dy) + _dot(k, gs) * zeta
        dk = _dot_tn(_bf(dp), q) + _dot_nt(v, gs) * zeta
        g_ref[...] = g_ref[...] * gb + _dot_tn(_bf(q.astype(F32) * xi), dy)
        return dk, dv

    return _ret_scan("ret_bwd_kv", per_head, lgam, [(rq, RET_DK), (rk, RET_DK), (rv, RET_DV), (dy, RET_DV)],
                     [(RET_DK, F32), (RET_DV, BF16)], True)


def _attn_mask_t(tb):
    key = lax.broadcasted_iota(jnp.int32, (tb, tb), 0)
    qry = lax.broadcasted_iota(jnp.int32, (tb, tb), 1)
    return _chunk_of(key) <= _chunk_of(qry)


MASKED = -1e30
SUBLANES = 8


def _head_blocks(nb, w, tb):
    return pl.BlockSpec((nb, None, w, tb), lambda h, i: (0, h, 0, 0))


def _one_block(w, tb):
    return pl.BlockSpec((None, None, w, tb), lambda h, i: (i, h, 0, 0))


def _attn_fwd(k, qt, vt, exchange=()):
    T = k.shape[0]
    tb = _attn_block(T)
    nb = T // tb

    n_ex = len(exchange)

    def kern(qt_ref, k_ref, vt_ref, *refs):
        ex_in, (o_ref, lser_ref), ex_out = refs[:n_ex], refs[n_ex:n_ex + 2], refs[n_ex + 2:2 * n_ex + 2]
        m_ref, acc_ref, sa_ref, sb_ref = refs[2 * n_ex + 2:2 * n_ex + 6]
        sems = refs[2 * n_ex + 6:]
        qb = pl.program_id(1)
        first = jnp.logical_and(pl.program_id(0) == 0, qb == 0)
        last = jnp.logical_and(pl.program_id(0) == MLA_HEADS - 1, qb == nb - 1)
        if n_ex:
            @pl.when(first)
            def _():
                for cp in _chip_copies(ex_in, ex_out, *sems, True):
                    cp.start()

        qt = qt_ref[...]
        m_ref[...] = jnp.full_like(m_ref, MASKED)
        acc_ref[...] = jnp.zeros_like(acc_ref)

        def scores(kb):
            rows = pl.ds(pl.multiple_of(kb * tb, tb), tb)
            return _dot(k_ref[rows, :], qt)

        def update(s, kb):
            m_old = m_ref[...]
            m_new = jnp.maximum(m_old, jnp.max(s, axis=0, keepdims=True))
            p = jnp.exp2(s - m_new)
            acc_ref[...] = acc_ref[...] * jnp.exp2(m_old - m_new) + _dot(vt_ref[kb], _bf(p))
            m_ref[...] = m_new

        def masked(s):
            return jnp.where(_attn_mask_t(tb), s, MASKED)

        sa_ref[...] = scores(0)

        def pair_body(j, carry):
            sb_ref[...] = scores(2 * j + 1)
            update(sa_ref[...], 2 * j)
            sa_ref[...] = scores(2 * j + 2)
            update(sb_ref[...], 2 * j + 1)
            return carry

        lax.fori_loop(0, qb // 2, pair_body, 0)

        @pl.when(qb % 2 == 0)
        def _():
            update(masked(sa_ref[...]), qb)

        @pl.when(qb % 2 == 1)
        def _():
            sb_ref[...] = masked(scores(qb))
            update(sa_ref[...], qb - 1)
            update(sb_ref[...], qb)

        l = acc_ref[MLA_DV:MLA_DV + 1, :]
        o_ref[...] = (acc_ref[:MLA_DV, :] / l).T
        lser_ref[...] = jnp.broadcast_to(m_ref[...] + jnp.log2(l), (SUBLANES, tb))
        if n_ex:
            @pl.when(last)
            def _():
                _wait_copies(_chip_copies(ex_in, ex_out, *sems, True))

    return pl.pallas_call(
        kern, grid=(MLA_HEADS, nb),
        in_specs=[_one_block(MLA_QK, tb), pl.BlockSpec((T, MLA_QK), lambda h, i: (0, h)),
                  _head_blocks(nb, MLA_DV + V_ONES, tb)] + [HBM_SPEC] * n_ex,
        out_specs=[pl.BlockSpec((tb, MLA_DV), lambda h, i: (i, h)), _one_block(SUBLANES, tb)] + [HBM_SPEC] * n_ex,
        out_shape=[jax.ShapeDtypeStruct((T, MLA_HEADS * MLA_DV), F32),
                   jax.ShapeDtypeStruct((nb, MLA_HEADS, SUBLANES, tb), F32)] + _exchange_shapes(exchange),
        scratch_shapes=[pltpu.VMEM((1, tb), F32), pltpu.VMEM((MLA_DV + V_ONES, tb), F32),
                        pltpu.VMEM((tb, tb), F32), pltpu.VMEM((tb, tb), F32)]
        + (_dma_sems(n_ex * N_PEER_CHIPS) if n_ex else []),
        name="attn_fwd", compiler_params=_params(("arbitrary", "arbitrary")))(qt, k, vt, *exchange)


def _attn_bwd(q, k, v, do, qt, kt, dot_, lse_rows, delta_rows):
    T = q.shape[0]
    tb = _attn_block(T)
    nb = T // tb

    def kern(q_ref, k_ref, v_ref, do_ref, qt_ref, kt_ref, dot_ref, lse_ref, dl_ref, dk_ref, dv_ref, dqt_ref, dv_acc,
             sa_ref, pa_ref, sb_ref, pb_ref):
        kb = pl.program_id(1)
        kv, vv, ktv = k_ref[...], v_ref[...], kt_ref[...]
        dk_ref[...] = jnp.zeros_like(dk_ref)
        dv_acc[...] = jnp.zeros_like(dv_acc)

        @pl.when(kb == 0)
        def _():
            dqt_ref[...] = jnp.zeros_like(dqt_ref)

        def products(qb, s_ref, dp_ref, diagonal=False):
            s = _dot(kv, qt_ref[qb])
            s_ref[...] = jnp.where(_attn_mask_t(tb), s, MASKED) if diagonal else s
            dp_ref[...] = _dot(vv, dot_ref[qb])

        def consume(qb, s_ref, dp_ref):
            rows = pl.ds(pl.multiple_of(qb * tb, tb), tb)
            p = jnp.exp2(s_ref[...] - lse_ref[qb][:1, :])
            dv_acc[...] += _dot(_bf(p), do_ref[rows, :])
            ds = _bf(p * (dp_ref[...] - dl_ref[qb][:1, :]))
            dk_ref[...] += _dot(ds, q_ref[rows, :])
            dqt_ref[qb] += _dot(ktv, ds)

        n_full = nb - 1 - kb
        products(kb, sa_ref, pa_ref, diagonal=True)

        def pair_body(j, carry):
            q1 = kb + 1 + 2 * j
            products(q1, sb_ref, pb_ref)
            consume(q1 - 1, sa_ref, pa_ref)
            products(q1 + 1, sa_ref, pa_ref)
            consume(q1, sb_ref, pb_ref)
            return carry

        lax.fori_loop(0, n_full // 2, pair_body, 0)

        @pl.when(n_full % 2 == 0)
        def _():
            consume(nb - 1, sa_ref, pa_ref)

        @pl.when(n_full % 2 == 1)
        def _():
            products(nb - 1, sb_ref, pb_ref)
            consume(nb - 2, sa_ref, pa_ref)
            consume(nb - 1, sb_ref, pb_ref)

        dk_ref[...] = dk_ref[...] * (ATTN_SCALE / Q_PRESCALE)
        dv_ref[...] = _bf(dv_acc[...])

    def blk(w):
        return pl.BlockSpec((tb, w), lambda h, i: (i, h))

    def full(w):
        return pl.BlockSpec((T, w), lambda h, i: (0, h))

    return pl.pallas_call(
        kern, grid=(MLA_HEADS, nb),
        in_specs=[full(MLA_QK), blk(MLA_QK), blk(MLA_DV), full(MLA_DV), _head_blocks(nb, MLA_QK, tb),
                  _one_block(MLA_QK, tb), _head_blocks(nb, MLA_DV, tb), _head_blocks(nb, SUBLANES, tb),
                  _head_blocks(nb, SUBLANES, tb)],
        out_specs=[blk(MLA_QK), blk(MLA_DV), _head_blocks(nb, MLA_QK, tb)],
        out_shape=[jax.ShapeDtypeStruct((T, MLA_HEADS * MLA_QK), F32),
                   jax.ShapeDtypeStruct((T, MLA_HEADS * MLA_DV), BF16),
                   jax.ShapeDtypeStruct((nb, MLA_HEADS, MLA_QK, tb), F32)],
        scratch_shapes=[pltpu.VMEM((tb, MLA_DV), F32)] + [pltpu.VMEM((tb, tb), F32)] * 4,
        name="attn_bwd", compiler_params=_params(("arbitrary", "arbitrary")))(
            q, k, v, do, qt, kt, dot_, lse_rows, delta_rows)


def _group_norm(y):
    yc = y - _mean(y)
    rstd = lax.rsqrt(_mean(yc * yc) + EPS)
    return yc * rstd, rstd


def _mix_fwd(y, rg, o, gates, h1, gn_g, w_ret_o, w_mla_o, w_out, ln_g, ln_b, tm):
    T, D = h1.shape

    def body(i, y_ref, rg_ref, o_ref, gt_ref, h_ref, gn_ref, wr_ref, wm_ref, wo_ref, g_ref, b_ref,
             h2_ref, z_ref, yret_ref, ymla_ref, yr_ref, mix_ref):
        for h in range(RET_HEADS):
            sl = slice(h * RET_DV, (h + 1) * RET_DV)
            yn, _ = _group_norm(y_ref[:, sl])
            r = rg_ref[:, sl]
            yr_ref[:, sl] = _bf(r * _sigmoid(r) * (yn * gn_ref[:, sl]))
        yret = _dot(yr_ref[...], wr_ref[...])
        ymla = _dot(_bf(o_ref[...]), wm_ref[...])
        yret_ref[...] = yret
        ymla_ref[...] = ymla
        mix = _bf(_sigmoid(gt_ref[:, :D]) * yret + _sigmoid(gt_ref[:, D:]) * ymla)
        mix_ref[...] = mix
        z = ALPHA * h_ref[...] + _dot(mix, wo_ref[...])
        xhat, _ = _ln_stats(z)
        z_ref[...] = z
        h2_ref[...] = xhat * g_ref[...] + b_ref[...]

    return _rowcall("mix_fwd", body, T, tm, [y, rg, o, gates, h1], [gn_g, w_ret_o, w_mla_o, w_out, ln_g, ln_b],
                    [(D, F32), (D, F32), (D, F32), (D, F32), (RET_HEADS * RET_DV, BF16), (D, BF16)])


def _mix_bwd(dh2, z1, gates, yret, ymla, y, rg, o, gn_g, w_ret_o, w_mla_o, w_out, ln_g, tm, exchange=None):
    T, D = dh2.shape
    rv = RET_HEADS * RET_DV

    def body(i, dh_ref, z_ref, gt_ref, yret_ref, ymla_ref, y_ref, rg_ref, o_ref, gn_ref, wr_ref, wm_ref, wo_ref, g_ref,
             dz_ref, dgt_ref, drg_ref, dy_ref, do_ref, dyret_ref, dymla_ref, dg_ref, db_ref, dgn_ref, dot_ref,
             dl_ref):
        xhat, rstd = _ln_stats(z_ref[...])
        dz, dg, db = _ln_bwd(dh_ref[...], xhat, rstd, g_ref[...])
        _acc(i, dg_ref, dg)
        _acc(i, db_ref, db)
        dz_ref[...] = dz
        dmix = _dot_nt(_bf(dz), wo_ref[...])
        sr = _sigmoid(gt_ref[:, :D])
        sm = _sigmoid(gt_ref[:, D:])
        dgt_ref[:, :D] = _bf(dmix * yret_ref[...] * sr * (1.0 - sr))
        dgt_ref[:, D:] = _bf(dmix * ymla_ref[...] * sm * (1.0 - sm))
        dyret = _bf(dmix * sr)
        dymla = _bf(dmix * sm)
        dyret_ref[...] = dyret
        dymla_ref[...] = dymla
        dov = _dot_nt(dymla, wm_ref[...])
        do_ref[...] = _bf(dov)
        for h in range(MLA_HEADS):
            sl = slice(h * MLA_DV, (h + 1) * MLA_DV)
            dot_ref[h] = _bf(dov[:, sl].T)
            delta = jnp.sum(dov[:, sl] * o_ref[:, sl], axis=-1, keepdims=True)
            dl_ref[h] = jnp.broadcast_to(delta, (tm, LANES)).T[:SUBLANES, :]
        dyr = _dot_nt(dyret, wr_ref[...])
        dgn = []
        for h in range(RET_HEADS):
            sl = slice(h * RET_DV, (h + 1) * RET_DV)
            yn, grstd = _group_norm(y_ref[:, sl])
            r = rg_ref[:, sl]
            sig = _sigmoid(r)
            d = dyr[:, sl]
            drg_ref[:, sl] = _bf(d * (yn * gn_ref[:, sl]) * sig * (1.0 + r * (1.0 - sig)))
            dt = d * (r * sig)
            dgn.append(jnp.sum(dt * yn, axis=0, keepdims=True))
            dyn = dt * gn_ref[:, sl]
            dy_ref[:, sl] = _bf(grstd * (dyn - _mean(dyn) - yn * _mean(dyn * yn)))
        _acc(i, dgn_ref, jnp.concatenate(dgn, axis=1))

    return _rowcall("mix_bwd", body, T, tm, [dh2, z1, gates, yret, ymla, y, rg, o],
                    [gn_g, w_ret_o, w_mla_o, w_out, ln_g],
                    [(D, F32), (2 * D, BF16), (rv, BF16), (rv, BF16), (MLA_HEADS * MLA_DV, BF16), (D, BF16), (D, BF16)],
                    [((1, D), F32), ((1, D), F32), ((1, rv), F32)],
                    tiled_outs=[_transposed_blocks(T, tm, MLA_DV, BF16), _transposed_blocks(T, tm, SUBLANES, F32)],
                    exchange=exchange)


def _proj_mla_bwd(dqt, dk, dv, lat, tabs, w_uq, w_uk, w_uv, qn_g, kvn_g, tm):
    T = dk.shape[0]
    H = MLA_HEADS
    lat_w = Q_LORA + KV_LORA

    def body(i, dk_ref, dv_ref, lat_ref, c_ref, s1_ref, s2_ref, dqt_ref, wuq_ref, wuk_ref, wuv_ref, qg_ref, kg_ref,
             dlat_ref, dkpe_ref, dqb_ref, dkn_ref, dqg_ref, dkg_ref):
        c, s1, s2 = c_ref[...], s1_ref[...], s2_ref[...]
        dkpe = jnp.zeros((tm, LANES), F32)
        for h in range(H):
            o = h * MLA_QK
            dqh = dqt_ref[h].T * ATTN_SCALE
            dqb_ref[:, o:o + MLA_NOPE] = _bf(dqh[:, :MLA_NOPE])
            dqb_ref[:, o + MLA_NOPE:o + MLA_QK] = _bf(_rope_pe_bwd(dqh[:, MLA_NOPE:], c, s1, s2))
            dkn_ref[:, h * MLA_NOPE:(h + 1) * MLA_NOPE] = _bf(dk_ref[:, o:o + MLA_NOPE])
            dkpe += dk_ref[:, o + MLA_NOPE:o + MLA_QK]
        dkpe_ref[...] = _bf(_rope_pe_bwd(dkpe, c, s1, s2))
        dcqn = _dot_nt(dqb_ref[...], wuq_ref[...])
        dckn = _dot_nt(dkn_ref[...], wuk_ref[...]) + _dot_nt(dv_ref[...], wuv_ref[...])
        for dn, x, g_ref, dg_ref, sl in ((dcqn, lat_ref[:, :Q_LORA], qg_ref, dqg_ref, slice(0, Q_LORA)),
                                         (dckn, lat_ref[:, Q_LORA:], kg_ref, dkg_ref, slice(Q_LORA, lat_w))):
            xn, r = _rms(x, None)
            _acc(i, dg_ref, jnp.sum(dn * xn, axis=0, keepdims=True))
            dxn = dn * g_ref[...]
            dlat_ref[:, sl] = _bf(r * (dxn - xn * _mean(dxn * xn)))

    dqt_shape, dqt_spec = _transposed_blocks(T, tm, MLA_QK, F32)
    assert dqt.shape == dqt_shape.shape
    return _rowcall("proj_mla_bwd", body, T, tm, [dk, dv, lat, *tabs], [w_uq, w_uk, w_uv, qn_g, kvn_g],
                    [(lat_w, BF16), (LANES, BF16), (H * MLA_QK, BF16), (H * MLA_NOPE, BF16)],
                    [((1, Q_LORA), F32), ((1, KV_LORA), F32)], tiled_ins=[(dqt, dqt_spec)])


def _proj_bwd(drq, drk, drv, drg, dz1, dlat, dkpe, dgates, cos_r, sin_r, w_r, w_c, w_kpe, w_g, tm):
    T, D = dz1.shape
    qk = RET_HEADS * RET_DK
    rv = RET_HEADS * RET_DV

    def body(i, drq_ref, drk_ref, drv_ref, drg_ref, dz_ref, dlat_ref, dkpe_ref, dgt_ref, cos_ref, sin_ref,
             wr_ref, wc_ref, wk_ref, wg_ref, dh_ref, dpr_ref):
        cos, sin = cos_ref[...], sin_ref[...]
        for src, off, scale in ((drq_ref, 0, 1.0), (drk_ref, qk, RET_DK ** -0.5)):
            for h in range(RET_HEADS):
                d = src[:, h * RET_DK:(h + 1) * RET_DK]
                dpr_ref[:, off + h * RET_DK:off + (h + 1) * RET_DK] = _bf(
                    (d * cos + _roll(d * sin, RET_DK // 2)) * scale)
        dpr_ref[:, 2 * qk:2 * qk + rv] = drv_ref[...]
        dpr_ref[:, 2 * qk + rv:] = drg_ref[...]
        dh_ref[...] = (ALPHA * dz_ref[...] + _dot_nt(dpr_ref[...], wr_ref[...]) + _dot_nt(dlat_ref[...], wc_ref[...])
                       + _dot_nt(dkpe_ref[...], wk_ref[...]) + _dot_nt(dgt_ref[...], wg_ref[...]))

    return _rowcall("proj_bwd", body, T, tm, [drq, drk, drv, drg, dz1, dlat, dkpe, dgates, cos_r, sin_r],
                    [w_r, w_c, w_kpe, w_g], [(D, F32), (2 * qk + 2 * rv, BF16)])


def _ple_loss(h3, p, target, w_gate, w_proj, ln_g, ln_b, tm):
    T, D = h3.shape

    def body(i, h_ref, p_ref, t_ref, wg_ref, wp_ref, g_ref, b_ref, dh_ref, dgp_ref, dpp_ref, loss_ref, dg_ref, db_ref):
        hv = h_ref[...]
        sg = _sigmoid(_dot(_bf(hv), wg_ref[...]))
        pp = _dot(_bf(p_ref[...]), wp_ref[...])
        xhat, rstd = _ln_stats(ALPHA * hv + sg * pp)
        err = xhat * g_ref[...] + b_ref[...] - t_ref[...]
        row_loss = 0.5 * _mean(err * err)
        _acc(i, loss_ref, jnp.broadcast_to(jnp.sum(row_loss, axis=0, keepdims=True), (1, LANES)))
        dz, dg, db = _ln_bwd(err * (1.0 / D), xhat, rstd, g_ref[...])
        _acc(i, dg_ref, dg)
        _acc(i, db_ref, db)
        dgp = _bf(dz * pp * sg * (1.0 - sg))
        dgp_ref[...] = dgp
        dpp_ref[...] = _bf(dz * sg)
        dh_ref[...] = ALPHA * dz + _dot_nt(dgp, wg_ref[...])

    return _rowcall("ple_loss", body, T, tm, [h3, p, target], [w_gate, w_proj, ln_g, ln_b],
                    [(D, F32), (D, BF16), (D, BF16)], [((1, LANES), F32), ((1, D), F32), ((1, D), F32)])


def _ewise(name, fn, ins, n_out, out_dtype=F32):
    r, c = ins[0].shape
    tr = _tile(r, max(8, (1 << 19) // c // 8 * 8), 8)

    def kern(*refs):
        outs = fn(*[x[...] for x in refs[:len(ins)]])
        for o_ref, o in zip(refs[len(ins):], outs):
            o_ref[...] = o.astype(out_dtype)

    spec = pl.BlockSpec((tr, c), lambda i: (i, 0))
    return pl.pallas_call(kern, grid=(r // tr,), in_specs=[spec] * len(ins), out_specs=[spec] * n_out,
                          out_shape=[jax.ShapeDtypeStruct((r, c), out_dtype)] * n_out, name=name,
                          compiler_params=_params(("arbitrary",)))(*ins)


def _adamw_math(w, g, m, v):
    m = ADAM_B1 * m + (1.0 - ADAM_B1) * g
    v = ADAM_B2 * v + (1.0 - ADAM_B2) * (g * g)
    m_hat = m / (1.0 - ADAM_B1 ** ADAM_STEP)
    v_hat = v / (1.0 - ADAM_B2 ** ADAM_STEP)
    return -ADAM_LR * (m_hat / (jnp.sqrt(v_hat) + ADAM_EPS) + ADAM_WD * w), m, v


def _adamw(name, w, g, m, v):
    shape = w.shape
    c = shape[-1]
    flat = [t.reshape(-1, c) for t in (w, g, m, v)]
    return [t.reshape(shape) for t in _ewise(name, _adamw_math, flat, 3)]


def _place():
    return lax.axis_index("x"), lax.axis_index("y"), lax.axis_index("c")


def _dma_sems(n):
    return [pltpu.SemaphoreType.DMA((n,)), pltpu.SemaphoreType.DMA((n,))]


N_PEER_CHIPS = N_CHIPS - 1


def _chips_exchange(name, srcs, broadcast):
    n = len(srcs)

    def kern(*refs):
        cps = _chip_copies(refs[:n], refs[n:2 * n], refs[2 * n], refs[2 * n + 1], broadcast)
        for cp in cps:
            cp.start()
        _wait_copies(cps)

    return pl.pallas_call(
        kern, out_shape=_exchange_shapes(srcs), in_specs=[HBM_SPEC] * n, out_specs=[HBM_SPEC] * n,
        scratch_shapes=_dma_sems(n * N_PEER_CHIPS), name=name)(*srcs)


def _exchange_shapes(srcs):
    return [jax.ShapeDtypeStruct((N_PEER_CHIPS,) + s.shape[1:], s.dtype) for s in srcs]


def _chip_copies(src_refs, out_refs, send_sems, recv_sems, broadcast):
    x, y, c = _place()
    peers = [(1 - x, y), (x, 1 - y), (1 - x, 1 - y)]
    cps = []
    for j, (px, py) in enumerate(peers):
        for a, (src_ref, out_ref) in enumerate(zip(src_refs, out_refs)):
            piece = src_ref.at[c] if broadcast else src_ref.at[2 * px + py]
            cps.append(pltpu.make_async_remote_copy(
                src_ref=piece, dst_ref=out_ref.at[j], send_sem=send_sems.at[a * N_PEER_CHIPS + j],
                recv_sem=recv_sems.at[a * N_PEER_CHIPS + j], device_id=(px, py, c), device_id_type=MESH))
    return cps


def _wait_copies(cps):
    for cp in cps:
        cp.wait_recv()
    for cp in cps:
        cp.wait_send()


def _sibling_swap(name, srcs, halves):
    n = len(srcs)

    def kern(*refs):
        src_refs, out_refs = refs[:n], refs[n:2 * n]
        send_sems, recv_sems = refs[2 * n:]
        x, y, c = _place()

        def copy(a):
            piece = src_refs[a].at[:, 1 - c] if halves else src_refs[a]
            return pltpu.make_async_remote_copy(
                src_ref=piece, dst_ref=out_refs[a], send_sem=send_sems.at[a], recv_sem=recv_sems.at[a],
                device_id=(x, y, 1 - c), device_id_type=MESH)

        cps = [copy(a) for a in range(n)]
        for cp in cps:
            cp.start()
        for cp in cps:
            cp.wait_recv()
        for cp in cps:
            cp.wait_send()

    def out_shape(s):
        return jax.ShapeDtypeStruct((s.shape[0],) + s.shape[2:] if halves else s.shape, s.dtype)

    return pl.pallas_call(
        kern, out_shape=[out_shape(s) for s in srcs], in_specs=[HBM_SPEC] * n, out_specs=[HBM_SPEC] * n,
        scratch_shapes=_dma_sems(n), name=name)(*srcs)


def _all_devices(name, src, reduce):
    r, c = src.shape
    n_dev = 2 * N_CHIPS

    def kern(src_ref, out_ref, *scratch):
        if reduce:
            gat_ref, send_sems, recv_sems = scratch
        else:
            gat_ref = out_ref
            send_sems, recv_sems = scratch
        x, y, cc = _place()
        me = 4 * x + 2 * y + cc
        gat_ref[me] = src_ref[...]
        peers = []
        for j in range(1, n_dev):
            px = 1 - x if j & 4 else x
            py = 1 - y if j & 2 else y
            pc = 1 - cc if j & 1 else cc
            peers.append((px, py, pc))

        def copy(j, peer, slot):
            return pltpu.make_async_remote_copy(
                src_ref=src_ref, dst_ref=gat_ref.at[slot], send_sem=send_sems.at[j], recv_sem=recv_sems.at[j],
                device_id=peer, device_id_type=MESH)

        sends = [copy(j, peer, me) for j, peer in enumerate(peers)]
        for cp in sends:
            cp.start()
        for j, (px, py, pc) in enumerate(peers):
            copy(j, (px, py, pc), 4 * px + 2 * py + pc).wait_recv()
        for cp in sends:
            cp.wait_send()
        if reduce:
            total = gat_ref[0]
            for d in range(1, n_dev):
                total = total + gat_ref[d]
            out_ref[...] = total

    out_shape = jax.ShapeDtypeStruct((r, c) if reduce else (n_dev, r, c), src.dtype)
    scratch = ([pltpu.VMEM((n_dev, r, c), src.dtype)] if reduce else []) + _dma_sems(n_dev - 1)
    return pl.pallas_call(kern, out_shape=out_shape, in_specs=[VMEM_SPEC], out_specs=VMEM_SPEC,
                          scratch_shapes=scratch, name=name)(src)


def _halves(t, axis):
    return t.reshape(t.shape[:axis] + (2, t.shape[axis] // 2) + t.shape[axis + 1:])


def _by_core(mine, theirs, axis):
    c = lax.axis_index("c")
    both = jnp.where(c == 0, jnp.stack([mine, theirs], axis), jnp.stack([theirs, mine], axis))
    return both.reshape(both.shape[:axis] + (2 * both.shape[axis + 1],) + both.shape[axis + 2:])


def _chip_order(own, others):
    me = 2 * lax.axis_index("x") + lax.axis_index("y")
    cands = jnp.concatenate([own[None], others], axis=0)
    slot_of_flip = (0, 2, 1, 3)
    pick = jnp.asarray(slot_of_flip, jnp.int32)[jnp.arange(N_CHIPS, dtype=jnp.int32) ^ me]
    return jnp.stack([lax.dynamic_index_in_dim(cands, pick[k], 0, keepdims=False) for k in range(N_CHIPS)])


def _join_shards(name, shards):
    _, r, c = shards.shape
    if name in COL_SHARDED:
        return shards.transpose(1, 0, 2).reshape(r, N_CHIPS * c)
    return shards.reshape(N_CHIPS * r, c)


def _split_shards(name, full):
    if full.ndim == 3:
        return full
    r, c = full.shape
    if name in COL_SHARDED:
        return jnp.stack([full[:, k * (c // N_CHIPS):(k + 1) * (c // N_CHIPS)] for k in range(N_CHIPS)])
    return full.reshape(N_CHIPS, r // N_CHIPS, c)


def _rope_tables(positions):
    pos = positions.reshape(-1).astype(F32)[:, None]
    half = RET_DK // 2
    ang = pos * (ROPE_BASE ** (-jnp.arange(half, dtype=F32) / half))
    cos_r = jnp.concatenate([jnp.cos(ang)] * 2, axis=1)
    sin_r = jnp.concatenate([-jnp.sin(ang), jnp.sin(ang)], axis=1)
    half = MLA_ROPE // 2
    ang = pos * (ROPE_BASE ** (-jnp.arange(half, dtype=F32) / half))
    zeros = jnp.zeros_like(ang)
    rest = LANES - MLA_ROPE
    c = jnp.concatenate([jnp.cos(ang)] * 2 + [jnp.ones((ang.shape[0], rest), F32)], axis=1)
    s1 = jnp.concatenate([-jnp.sin(ang), zeros, jnp.zeros((ang.shape[0], rest), F32)], axis=1)
    s2 = jnp.concatenate([zeros, jnp.sin(ang), jnp.zeros((ang.shape[0], rest), F32)], axis=1)
    return cos_r, sin_r, (c, s1, s2)


GATHER_GROUPS = (("ffn1_w_in", "ffn1_w_out"), ("w_in", "w_uq", "w_ukv"),
                 ("w_ret_o", "w_mla_o", "w_out", "ffn2_w_in", "ffn2_w_out", "ple_w_gate", "ple_w_proj"))
REDUCE_GROUPS = (("ple_w_gate", "ple_w_proj", "ffn2_w_in", "ffn2_w_out"),
                 ("w_out", "w_ret_o", "w_mla_o", "w_uq", "w_ukv", "w_in"), ("ffn1_w_in", "ffn1_w_out"))


def _gathered(tag, names, own, mine):
    theirs = _sibling_swap("gather_cores_" + tag, mine, False)
    out = {}
    for n, m, t in zip(names, mine, theirs):
        full = _chip_order(own[n], _by_core(m, t, 1))
        out[n] = full if n in ("ffn1_w_in", "ffn2_w_in") else _join_shards(n, full)
    return out


def _chip_sums(tag, names, grads):
    c = lax.axis_index("c")
    halves = [_halves(_split_shards(n, grads[n]), 1) for n in names]
    theirs = _sibling_swap("reduce_cores_" + tag, halves, True)
    sums = []
    for n, g, t in zip(names, halves, theirs):
        mine = lax.dynamic_index_in_dim(g, c, axis=1, keepdims=False)
        k, r, cc = mine.shape
        sums.append(_ewise("reduce_cores_add_" + n, lambda a, b: (a.astype(F32) + b.astype(F32),),
                           [mine.reshape(k * r, cc), t.reshape(k * r, cc)], 1, BF16)[0].reshape(k, r, cc))
    return sums


def _block_totals(names, sums, parts):
    me = 2 * lax.axis_index("x") + lax.axis_index("y")
    totals = []
    for n, s, pt in zip(names, sums, parts):
        own = lax.dynamic_index_in_dim(s, me, axis=0, keepdims=False)
        totals.append(_ewise("reduce_chips_add_" + n,
                             lambda a, b, c_, d: (((a.astype(F32) + b.astype(F32)) + c_.astype(F32)) + d.astype(F32),),
                             [own, pt[0], pt[1], pt[2]], 1, F32)[0])
    return totals


def _local_step(x, p, positions, target, shards, ln_g, ln_b, gn_g, qn_g, kvn_g):
    T, D = x.shape
    tm = min(256, T)
    H = MLA_HEADS
    qk, rv = RET_HEADS * RET_DK, RET_HEADS * RET_DV
    cos_r, sin_r, tabs = _rope_tables(positions)
    lgam = jnp.broadcast_to(jnp.log(1.0 - 2.0 ** (-5.0 - jnp.arange(RET_HEADS, dtype=F32)))[:, None, None],
                            (RET_HEADS, 1, LANES))
    lng = [ln_g[k:k + 1] for k in range(N_LN)]
    lnb = [ln_b[k:k + 1] for k in range(N_LN)]
    own = {n: _bf(shards[n]) for n in BIG_WEIGHTS}
    to_send = [[_halves(own[n], 0) for n in names] for names in GATHER_GROUPS]

    w = _gathered("a", GATHER_GROUPS[0], own, _chips_exchange("gather_chips_a", to_send[0], True))
    h1, z0, a1, *arrived = _ffn_fwd("ffn1_fwd", x, w["ffn1_w_in"], w["ffn1_w_out"], lng[0], lnb[0], tm,
                                    exchange=(to_send[1], True))
    w.update(_gathered("b", GATHER_GROUPS[1], own, arrived))

    w_in = w["w_in"]
    o_lat, o_kpe, o_gate = 2 * qk + 2 * rv, 2 * qk + 2 * rv + Q_LORA + KV_LORA, 2 * qk + 2 * rv + Q_LORA + KV_LORA + MLA_ROPE
    w_r, w_c = w_in[:, :o_lat], w_in[:, o_lat:o_kpe]
    w_kpe = jnp.pad(w_in[:, o_kpe:o_gate], ((0, 0), (0, LANES - MLA_ROPE)))
    w_g = w_in[:, o_gate:]
    w_uq = jnp.pad(w["w_uq"].reshape(Q_LORA, H, MLA_NOPE + MLA_ROPE),
                   ((0, 0), (0, 0), (0, MLA_QK - MLA_NOPE - MLA_ROPE))).reshape(Q_LORA, H * MLA_QK)
    w_ukv = w["w_ukv"].reshape(KV_LORA, H, MLA_NOPE + MLA_DV)
    w_uk = w_ukv[:, :, :MLA_NOPE].reshape(KV_LORA, H * MLA_NOPE)
    w_uv = w_ukv[:, :, MLA_NOPE:].reshape(KV_LORA, H * MLA_DV)

    rq, rk, rvv, rg = _proj_ret(h1, w_r, cos_r, sin_r, tm)
    lat, gates, q, k, v, latn, qt, kt, vt = _proj_mla(h1, tabs, w_c, w_kpe, w_g, w_uq, w_uk, w_uv, qn_g, kvn_g, tm)
    y = _ret_fwd(rq, rk, rvv, lgam)
    o, lse_rows, *arrived = _attn_fwd(k, qt, vt, exchange=to_send[2])
    w.update(_gathered("c", GATHER_GROUPS[2], own, arrived))
    h2, z1, yret, ymla, yr, mix = _mix_fwd(y, rg, o, gates, h1, gn_g, w["w_ret_o"], w["w_mla_o"], w["w_out"],
                                           lng[1], lnb[1], tm)
    h3, z2, a2 = _ffn_fwd("ffn2_fwd", h2, w["ffn2_w_in"], w["ffn2_w_out"], lng[2], lnb[2], tm)

    dh3, dgp, dpp, loss, dg3, db3 = _ple_loss(h3, p, target, w["ple_w_gate"], w["ple_w_proj"], lng[3], lnb[3], tm)
    dh2, da2, s2, df2, dg2, db2 = _ffn_bwd("ffn2_bwd", dh3, z2, a2, w["ffn2_w_in"], w["ffn2_w_out"], lng[2], tm)
    grads = {"ple_w_gate": _mm_tn("wg_ple_gate", h3, dgp), "ple_w_proj": _mm_tn("wg_ple_proj", p, dpp),
             "ffn2_w_in": _mm_tn("wg_ffn2_in", h2, da2, n_split=N_CHIPS), "ffn2_w_out": _mm_tn("wg_ffn2_out", s2, df2)}
    sums1 = _chip_sums("1", REDUCE_GROUPS[0], grads)
    (dz1, dgates, drg, dy, do, dyret, dymla, dg1, db1, dgn, dot_, delta_rows, *parts1) = _mix_bwd(
        dh2, z1, gates, yret, ymla, y, rg, o, gn_g, w["w_ret_o"], w["w_mla_o"], w["w_out"], lng[1], tm,
        exchange=(sums1, False))
    drq = _ret_bwd_q(rq, rk, rvv, dy, lgam)
    drk, drv = _ret_bwd_kv(rq, rk, rvv, dy, lgam)
    dk, dv, dqt = _attn_bwd(q, k, v, do, qt, kt, dot_, lse_rows, delta_rows)
    dlat, dkpe, dqb, dkn, dqg, dkg = _proj_mla_bwd(dqt, dk, dv, lat, tabs, w_uq, w_uk, w_uv, qn_g, kvn_g, tm)
    dh1, dpr = _proj_bwd(drq, drk, drv, drg, dz1, dlat, dkpe, dgates, cos_r, sin_r, w_r, w_c, w_kpe, w_g, tm)
    g_uq = _mm_tn("wg_uq", latn[:, :Q_LORA], dqb).reshape(Q_LORA, H, MLA_QK)[:, :, :MLA_NOPE + MLA_ROPE]
    g_uk = _mm_tn("wg_uk", latn[:, Q_LORA:], dkn).reshape(KV_LORA, H, MLA_NOPE)
    g_uv = _mm_tn("wg_uv", latn[:, Q_LORA:], dv).reshape(KV_LORA, H, MLA_DV)
    grads.update({
        "w_in": jnp.concatenate([_mm_tn("wg_in_r", h1, dpr), _mm_tn("wg_in_c", h1, dlat),
                                 _mm_tn("wg_in_kpe", h1, dkpe)[:, :MLA_ROPE], _mm_tn("wg_in_g", h1, dgates)], axis=1),
        "w_ret_o": _mm_tn("wg_ret_o", yr, dyret),
        "w_uq": g_uq.reshape(Q_LORA, H * (MLA_NOPE + MLA_ROPE)),
        "w_ukv": jnp.concatenate([g_uk, g_uv], axis=2).reshape(KV_LORA, H * (MLA_NOPE + MLA_DV)),
        "w_mla_o": _mm_tn("wg_mla_o", o, dymla),
        "w_out": _mm_tn("wg_out", mix, dz1)})
    sums2 = _chip_sums("2", REDUCE_GROUPS[1], grads)
    dx, da1, s1, df1, dg0, db0, *parts2 = _ffn_bwd("ffn1_bwd", dh1, z0, a1, w["ffn1_w_in"], w["ffn1_w_out"], lng[0], tm,
                                                   exchange=(sums2, False))
    grads.update({"ffn1_w_in": _mm_tn("wg_ffn1_in", x, da1, n_split=N_CHIPS),
                  "ffn1_w_out": _mm_tn("wg_ffn1_out", s1, df1)})
    sums3 = _chip_sums("3", REDUCE_GROUPS[2], grads)
    parts3 = _chips_exchange("reduce_chips_3", sums3, False)

    names = [n for group in REDUCE_GROUPS for n in group]
    totals = _block_totals(names, sums1 + sums2 + sums3, list(parts1) + list(parts2) + list(parts3))
    others = _sibling_swap("reduce_join", totals, False)
    reduced = {n: _by_core(t, o_, 0) for n, t, o_ in zip(names, totals, others)}
    small = {"ln_g": jnp.concatenate([dg0, dg1, dg2, dg3], axis=0), "ln_b": jnp.concatenate([db0, db1, db2, db3], axis=0),
             "ret_gn_g": dgn, "q_norm_g": dqg, "kv_norm_g": dkg}
    return loss[0, 0], dx, reduced, small


def kernel(x, p, positions, ln_g, ln_b, ffn1_w_in, ffn1_w_out, w_in, ret_gn_g, w_ret_o, q_norm_g, kv_norm_g, w_uq, w_ukv, w_mla_o, w_out, ffn2_w_in, ffn2_w_out, ple_w_gate, ple_w_proj, loss_target, m_ln_g, m_ln_b, m_ffn1_w_in, m_ffn1_w_out, m_w_in, m_ret_gn_g, m_w_ret_o, m_q_norm_g, m_kv_norm_g, m_w_uq, m_w_ukv, m_w_mla_o, m_w_out, m_ffn2_w_in, m_ffn2_w_out, m_ple_w_gate, m_ple_w_proj, v_ln_g, v_ln_b, v_ffn1_w_in, v_ffn1_w_out, v_w_in, v_ret_gn_g, v_w_ret_o, v_q_norm_g, v_kv_norm_g, v_w_uq, v_w_ukv, v_w_mla_o, v_w_out, v_ffn2_w_in, v_ffn2_w_out, v_ple_w_gate, v_ple_w_proj):
    names = ("ln_g", "ln_b", "ffn1_w_in", "ffn1_w_out", "w_in", "ret_gn_g", "w_ret_o", "q_norm_g", "kv_norm_g", "w_uq",
             "w_ukv", "w_mla_o", "w_out", "ffn2_w_in", "ffn2_w_out", "ple_w_gate", "ple_w_proj")
    weights = dict(zip(names, (ln_g, ln_b, ffn1_w_in, ffn1_w_out, w_in, ret_gn_g, w_ret_o, q_norm_g, kv_norm_g, w_uq,
                               w_ukv, w_mla_o, w_out, ffn2_w_in, ffn2_w_out, ple_w_gate, ple_w_proj)))
    m_in = dict(zip(names, (m_ln_g, m_ln_b, m_ffn1_w_in, m_ffn1_w_out, m_w_in, m_ret_gn_g, m_w_ret_o, m_q_norm_g,
                            m_kv_norm_g, m_w_uq, m_w_ukv, m_w_mla_o, m_w_out, m_ffn2_w_in, m_ffn2_w_out, m_ple_w_gate,
                            m_ple_w_proj)))
    v_in = dict(zip(names, (v_ln_g, v_ln_b, v_ffn1_w_in, v_ffn1_w_out, v_w_in, v_ret_gn_g, v_w_ret_o, v_q_norm_g,
                            v_kv_norm_g, v_w_uq, v_w_ukv, v_w_mla_o, v_w_out, v_ffn2_w_in, v_ffn2_w_out, v_ple_w_gate,
                            v_ple_w_proj)))
    chip = 2 * lax.axis_index("x") + lax.axis_index("y")
    D = x.shape[-1]
    dq = D // N_CHIPS

    shards = {n: weights[n][0] for n in BIG_WEIGHTS}
    ln_all = _all_devices("gather_ln", jnp.concatenate([ln_g[0], ln_b[0]], axis=0), False)
    ln_full = ln_all[::2].transpose(1, 0, 2).reshape(2 * N_LN, D)
    loss, dx, big, small = _local_step(x[0], p[0, 0], positions, loss_target[0], shards, ln_full[:N_LN],
                                       ln_full[N_LN:], ret_gn_g, q_norm_g, kv_norm_g)

    loss = lax.psum(loss, ("x", "y", "c"))
    small_names = ("ln_g", "ln_b", "ret_gn_g", "q_norm_g", "kv_norm_g")
    flat = jnp.concatenate([small[n].reshape(-1) for n in small_names])
    rows = -(-flat.shape[0] // LANES // 8) * 8
    flat = jnp.pad(flat, (0, rows * LANES - flat.shape[0])).reshape(rows, LANES)
    flat = _all_devices("reduce_small", flat, True).reshape(-1)
    off = 0
    for n in small_names:
        size = small[n].size
        small[n] = flat[off:off + size].reshape(small[n].shape)
        off += size
    g_out = dict(big)
    for n in ("ln_g", "ln_b"):
        g_out[n] = lax.dynamic_slice_in_dim(small[n], chip * dq, dq, axis=1)
    for n in ("ret_gn_g", "q_norm_g", "kv_norm_g"):
        g_out[n] = small[n]

    deltas, new_m, new_v = {}, {}, {}
    for n in names:
        g = g_out[n].reshape(weights[n].shape)
        g_out[n] = g
        deltas[n], new_m[n], new_v[n] = _adamw("adamw_" + n, weights[n], g, m_in[n], v_in[n])
    return (loss, dx[None], *[g_out[n] for n in names], *[deltas[n] for n in names], *[new_m[n] for n in names],
            *[new_v[n] for n in names])
```

```python
import functools

import jax
import jax.numpy as jnp
from jax import lax
from jax.experimental import pallas as pl
from jax.experimental.pallas import tpu as pltpu

D_MODEL = 1024
CHUNK = 64
D_PLE = 256
D_FF = 2816
RET_HEADS = 8
RET_DK = 128
RET_DV = 256
MLA_HEADS = 8
MLA_NOPE = 128
MLA_ROPE = 64
MLA_DV = 128
MLA_QK = 256
Q_LORA = 256
KV_LORA = 256
ROPE_BASE = 10000.0
EPS = 1e-5
N_LN = 4
ALPHA = 2.0 ** 0.25
ADAM_LR = 0.001
ADAM_B1 = 0.9
ADAM_B2 = 0.999
ADAM_EPS = 1e-08
ADAM_WD = 0.01
ADAM_STEP = 10

LANES = 128
VMEM_LIMIT = 60 << 20
N_CHIPS = 4

F32 = jnp.float32
BF16 = jnp.bfloat16
MESH = pl.DeviceIdType.MESH
HBM_SPEC = pl.BlockSpec(memory_space=pltpu.HBM)
VMEM_SPEC = pl.BlockSpec(memory_space=pltpu.VMEM)

BIG_WEIGHTS = ("ffn1_w_in", "ffn1_w_out", "w_in", "w_ret_o", "w_uq", "w_ukv", "w_mla_o", "w_out",
               "ffn2_w_in", "ffn2_w_out", "ple_w_gate", "ple_w_proj")
COL_SHARDED = ("ffn1_w_in", "w_in", "w_uq", "w_ukv", "ffn2_w_in", "ple_w_proj")


def _dot(a, b):
    return jnp.dot(a, b, preferred_element_type=F32)


def _dot_nt(a, b):
    return lax.dot_general(a, b, (((1,), (1,)), ((), ())), preferred_element_type=F32)


def _dot_tn(a, b):
    return lax.dot_general(a, b, (((0,), (0,)), ((), ())), preferred_element_type=F32)


def _bf(x):
    return x.astype(BF16)


def _sigmoid(x):
    return 0.5 * jnp.tanh(0.5 * x) + 0.5


def _mean(x):
    return jnp.mean(x, axis=-1, keepdims=True)


def _ln_stats(z):
    zc = z - _mean(z)
    rstd = lax.rsqrt(_mean(zc * zc) + EPS)
    return zc * rstd, rstd


def _ln_bwd(dy, xhat, rstd, g):
    dxhat = dy * g
    dz = rstd * (dxhat - _mean(dxhat) - xhat * _mean(dxhat * xhat))
    return dz, jnp.sum(dy * xhat, axis=0, keepdims=True), jnp.sum(dy, axis=0, keepdims=True)


def _roll(x, shift):
    return pltpu.roll(x, shift, 1)


def _chunk_of(idx):
    return jnp.right_shift(idx, CHUNK.bit_length() - 1)


def _tile(n, cap, mult=LANES):
    if n <= cap:
        return n
    for t in range(cap - cap % mult, 0, -mult):
        if n % t == 0:
            return t
    return n


def _zero_map(nd, *_):
    return (0,) * nd


def _params(sem):
    return pltpu.CompilerParams(dimension_semantics=sem, vmem_limit_bytes=VMEM_LIMIT)


def _rowcall(name, body, n_rows, tm, row_ins, full_ins, row_outs, acc_outs=(), tiled_outs=(), tiled_ins=(),
             exchange=None):
    n_steps = n_rows // tm
    ex_srcs, broadcast = exchange if exchange else ((), False)
    n_ex = len(ex_srcs)
    n_in = len(row_ins) + len(tiled_ins) + len(full_ins)
    n_out = len(row_outs) + len(acc_outs) + len(tiled_outs)

    def kern(*refs):
        step = pl.program_id(0)
        ex_in, ex_out = refs[n_in:n_in + n_ex], refs[n_in + n_ex + n_out:n_in + 2 * n_ex + n_out]
        sems = refs[n_in + 2 * n_ex + n_out:]
        if n_ex:
            @pl.when(step == 0)
            def _():
                for cp in _chip_copies(ex_in, ex_out, *sems, broadcast):
                    cp.start()

        body(step, *refs[:n_in], *refs[n_in + n_ex:n_in + n_ex + n_out])
        if n_ex:
            @pl.when(step == n_steps - 1)
            def _():
                _wait_copies(_chip_copies(ex_in, ex_out, *sems, broadcast))

    in_specs = [pl.BlockSpec((tm, a.shape[1]), lambda i: (i, 0)) for a in row_ins]
    in_specs += [spec for (_, spec) in tiled_ins]
    row_ins = list(row_ins) + [a for (a, _) in tiled_ins]
    in_specs += [pl.BlockSpec(a.shape, functools.partial(_zero_map, a.ndim), pipeline_mode=pl.Buffered(1))
                 for a in full_ins]
    in_specs += [HBM_SPEC] * n_ex
    out_specs = [pl.BlockSpec((tm, w), lambda i: (i, 0)) for (w, _) in row_outs]
    out_specs += [pl.BlockSpec(s, functools.partial(_zero_map, len(s))) for (s, _) in acc_outs]
    out_specs += [spec for (_, spec) in tiled_outs]
    out_specs += [HBM_SPEC] * n_ex
    out_shape = [jax.ShapeDtypeStruct((n_rows, w), dt) for (w, dt) in row_outs]
    out_shape += [jax.ShapeDtypeStruct(s, dt) for (s, dt) in acc_outs]
    out_shape += [shape for (shape, _) in tiled_outs]
    out_shape += _exchange_shapes(ex_srcs)
    return pl.pallas_call(kern, grid=(n_steps,), in_specs=in_specs, out_specs=out_specs, out_shape=out_shape,
                          scratch_shapes=_dma_sems(n_ex * N_PEER_CHIPS) if n_ex else [], name=name,
                          compiler_params=_params(("arbitrary",)))(*row_ins, *full_ins, *ex_srcs)


def _acc(step, ref, val):
    @pl.when(step == 0)
    def _():
        ref[...] = val

    @pl.when(step != 0)
    def _():
        ref[...] += val


def _ffn_fwd(name, x, w_in4, w_out, ln_g, ln_b, tm, exchange=None):
    T, D = x.shape
    fh = w_in4.shape[2]

    def body(i, x_ref, w4_ref, wo_ref, g_ref, b_ref, h_ref, z_ref, a_ref):
        xv = x_ref[...]
        xb = _bf(xv)
        f = jnp.zeros((tm, D), F32)
        for k in range(2):
            gk = _dot(xb, w4_ref[k])
            uk = _dot(xb, w4_ref[2 + k])
            a_ref[:, k * fh:(k + 1) * fh] = _bf(gk)
            a_ref[:, (2 + k) * fh:(3 + k) * fh] = _bf(uk)
            f += _dot(_bf(gk * _sigmoid(gk) * uk), wo_ref[k * fh:(k + 1) * fh, :])
        z = ALPHA * xv + 0.5 * f
        xhat, _ = _ln_stats(z)
        z_ref[...] = z
        h_ref[...] = xhat * g_ref[...] + b_ref[...]

    return _rowcall(name, body, T, tm, [x], [w_in4, w_out, ln_g, ln_b],
                    [(D, F32), (D, F32), (4 * fh, BF16)], exchange=exchange)


def _ffn_bwd(name, dh, z, a, w_in4, w_out, ln_g, tm, exchange=None):
    T, D = dh.shape
    fh = w_in4.shape[2]

    def body(i, dh_ref, z_ref, a_ref, w4_ref, wo_ref, g_ref, dx_ref, da_ref, s_ref, df_ref, dg_ref, db_ref):
        xhat, rstd = _ln_stats(z_ref[...])
        dz, dg, db = _ln_bwd(dh_ref[...], xhat, rstd, g_ref[...])
        _acc(i, dg_ref, dg)
        _acc(i, db_ref, db)
        dfb = _bf(0.5 * dz)
        df_ref[...] = dfb
        dx = ALPHA * dz
        for k in range(2):
            gk = a_ref[:, k * fh:(k + 1) * fh].astype(F32)
            uk = a_ref[:, (2 + k) * fh:(3 + k) * fh].astype(F32)
            ds = _dot_nt(dfb, wo_ref[k * fh:(k + 1) * fh, :])
            sig = _sigmoid(gk)
            silu = gk * sig
            dgk = _bf(ds * uk * sig * (1.0 + gk * (1.0 - sig)))
            duk = _bf(ds * silu)
            s_ref[:, k * fh:(k + 1) * fh] = _bf(silu * uk)
            da_ref[:, k * fh:(k + 1) * fh] = dgk
            da_ref[:, (2 + k) * fh:(3 + k) * fh] = duk
            dx += _dot_nt(dgk, w4_ref[k]) + _dot_nt(duk, w4_ref[2 + k])
        dx_ref[...] = dx

    return _rowcall(name, body, T, tm, [dh, z, a], [w_in4, w_out, ln_g],
                    [(D, F32), (4 * fh, BF16), (2 * fh, BF16), (D, BF16)],
                    [((1, D), F32), ((1, D), F32)], exchange=exchange)


def _mm_tn(name, a, b, out_dtype=BF16, n_split=1):
    T, M = a.shape
    N = b.shape[1]
    tk = _tile(T, 2048, 8)
    tm = _tile(M, 1408)
    tn = _tile(N // n_split, 1536)
    per = N // n_split // tn
    nk = T // tk
    if n_split > 1:
        out_spec = pl.BlockSpec((None, tm, tn), lambda i, j, k: (j // per, i, j % per))
        out_shape = jax.ShapeDtypeStruct((n_split, M, N // n_split), out_dtype)
    else:
        out_spec = pl.BlockSpec((tm, tn), lambda i, j, k: (i, j))
        out_shape = jax.ShapeDtypeStruct((M, N), out_dtype)

    def kern(a_ref, b_ref, o_ref, acc_ref):
        k = pl.program_id(2)
        part = _dot_tn(_bf(a_ref[...]), _bf(b_ref[...]))

        @pl.when(k == 0)
        def _():
            acc_ref[...] = part

        @pl.when(k != 0)
        def _():
            acc_ref[...] += part

        @pl.when(k == nk - 1)
        def _():
            o_ref[...] = acc_ref[...].astype(out_dtype)

    return pl.pallas_call(
        kern, grid=(M // tm, N // tn, nk),
        in_specs=[pl.BlockSpec((tk, tm), lambda i, j, k: (k, i)), pl.BlockSpec((tk, tn), lambda i, j, k: (k, j))],
        out_specs=out_spec, out_shape=out_shape,
        scratch_shapes=[pltpu.VMEM((tm, tn), F32)], name=name,
        compiler_params=_params(("arbitrary", "arbitrary", "arbitrary")))(a, b)


def _proj_ret(h1, w_r, cos_r, sin_r, tm):
    T, D = h1.shape
    qk = RET_HEADS * RET_DK
    rv = RET_HEADS * RET_DV

    def body(i, h_ref, cos_ref, sin_ref, w_ref, q_ref, k_ref, v_ref, g_ref):
        hb = _bf(h_ref[...])
        cos, sin = cos_ref[...], sin_ref[...]
        for out_ref, off, scale in ((q_ref, 0, 1.0), (k_ref, qk, RET_DK ** -0.5)):
            pr = _dot(hb, w_ref[:, off:off + qk])
            for h in range(RET_HEADS):
                t = pr[:, h * RET_DK:(h + 1) * RET_DK]
                out_ref[:, h * RET_DK:(h + 1) * RET_DK] = _bf((t * cos + _roll(t, RET_DK // 2) * sin) * scale)
        v_ref[...] = _bf(_dot(hb, w_ref[:, 2 * qk:2 * qk + rv]))
        g_ref[...] = _dot(hb, w_ref[:, 2 * qk + rv:2 * qk + 2 * rv])

    return _rowcall("proj_ret", body, T, tm, [h1, cos_r, sin_r], [w_r],
                    [(qk, BF16), (qk, BF16), (rv, BF16), (rv, F32)])


def _rope_pe(t, c, s1, s2):
    return t * c + _roll(t, LANES - MLA_ROPE // 2) * s1 + _roll(t, MLA_ROPE // 2) * s2


def _rope_pe_bwd(dy, c, s1, s2):
    return dy * c + _roll(dy * s1, MLA_ROPE // 2) + _roll(dy * s2, LANES - MLA_ROPE // 2)


def _rms(x, g):
    r = lax.rsqrt(_mean(x * x) + EPS)
    return x * r, r


def _attn_block(T):
    return min(512, T)


def _transposed_blocks(T, tm, w, dtype):
    tb = _attn_block(T)
    per = tb // tm
    return (jax.ShapeDtypeStruct((T // tb, MLA_HEADS, w, tb), dtype),
            pl.BlockSpec((None, MLA_HEADS, w, tm), lambda i: (i // per, 0, 0, i % per)))


ATTN_SCALE = (MLA_NOPE + MLA_ROPE) ** -0.5
LOG2E = 1.4426950408889634
Q_PRESCALE = ATTN_SCALE * LOG2E
V_ONES = 16


def _proj_mla(h1, tabs, w_c, w_kpe, w_g, w_uq, w_uk, w_uv, qn_g, kvn_g, tm):
    T, D = h1.shape
    H = MLA_HEADS

    def body(i, h_ref, c_ref, s1_ref, s2_ref, wc_ref, wk_ref, wg_ref, wuq_ref, wuk_ref, wuv_ref, qg_ref, kg_ref,
             lat_ref, gt_ref, q_ref, k_ref, v_ref, ln_ref, qt_ref, kt_ref, vt_ref):
        hb = _bf(h_ref[...])
        c, s1, s2 = c_ref[...], s1_ref[...], s2_ref[...]
        lat = _dot(hb, wc_ref[...])
        lat_ref[...] = lat
        gt_ref[...] = _dot(hb, wg_ref[...])
        cqn, _ = _rms(lat[:, :Q_LORA], None)
        ckn, _ = _rms(lat[:, Q_LORA:], None)
        cqn = _bf(cqn * qg_ref[...])
        ckn = _bf(ckn * kg_ref[...])
        ln_ref[:, :Q_LORA] = cqn
        ln_ref[:, Q_LORA:] = ckn
        q = _dot(cqn, wuq_ref[...])
        kn = _dot(ckn, wuk_ref[...])
        vv = _dot(ckn, wuv_ref[...])
        v_ref[...] = _bf(vv)
        kpe = _rope_pe(_dot(hb, wk_ref[...]), c, s1, s2)
        ones = jnp.ones((V_ONES, tm), BF16)
        for h in range(H):
            o = h * MLA_QK
            qh = jnp.concatenate([q[:, o:o + MLA_NOPE], _rope_pe(q[:, o + MLA_NOPE:o + MLA_QK], c, s1, s2)], axis=1)
            qh = qh * Q_PRESCALE
            kh = jnp.concatenate([kn[:, h * MLA_NOPE:(h + 1) * MLA_NOPE], kpe], axis=1)
            q_ref[:, o:o + MLA_QK] = _bf(qh)
            k_ref[:, o:o + MLA_QK] = _bf(kh)
            qt_ref[h] = _bf(qh.T)
            kt_ref[h] = _bf(kh.T)
            vt_ref[h] = jnp.concatenate([_bf(vv[:, h * MLA_DV:(h + 1) * MLA_DV].T), ones], axis=0)

    lat_w = Q_LORA + KV_LORA
    return _rowcall("proj_mla", body, T, tm, [h1, *tabs], [w_c, w_kpe, w_g, w_uq, w_uk, w_uv, qn_g, kvn_g],
                    [(lat_w, F32), (2 * D, F32), (H * MLA_QK, BF16), (H * MLA_QK, BF16), (H * MLA_DV, BF16),
                     (lat_w, BF16)],
                    tiled_outs=[_transposed_blocks(T, tm, MLA_QK, BF16), _transposed_blocks(T, tm, MLA_QK, BF16),
                                _transposed_blocks(T, tm, MLA_DV + V_ONES, BF16)])


def _ret_block(T):
    return min(256, T)


RET_HEADS_PER_STEP = 8


def _ret_dmat(lg, bt):
    n = lax.broadcasted_iota(jnp.int32, (bt, bt), 0)
    m = lax.broadcasted_iota(jnp.int32, (bt, bt), 1)
    return jnp.where(_chunk_of(m) <= _chunk_of(n), jnp.exp(lg * jnp.abs(n - m).astype(F32)), 0.0)


def _ret_scan(name, per_head, lgam, ins, outs, rev):
    T = ins[0][0].shape[0]
    bt = _ret_block(T)
    nb = T // bt
    hps = min(RET_HEADS_PER_STEP, RET_HEADS)
    n_in, n_out = len(ins), len(outs)

    def kern(lg_ref, *refs):
        in_refs, out_refs = refs[:n_in], refs[n_in:n_in + n_out]
        state_ref, dmat_ref = refs[n_in + n_out:]

        @pl.when(pl.program_id(1) == 0)
        def _():
            state_ref[...] = jnp.zeros_like(state_ref)
            for hh in range(hps):
                dmat_ref[hh] = _ret_dmat(lg_ref[hh][:, :1], bt)

        pos = lax.broadcasted_iota(jnp.int32, (bt, 1), 0).astype(F32)
        for hh in range(hps):
            lg = lg_ref[hh][:, :1]
            xi, zeta, gb = jnp.exp(lg * (pos + 1.0)), jnp.exp(lg * (bt - 1.0 - pos)), jnp.exp(lg * bt)
            tiles = [r[:, hh * w:(hh + 1) * w] for r, (_, w) in zip(in_refs, ins)]
            res = per_head(dmat_ref[hh], xi, zeta, gb, state_ref.at[hh], *tiles)
            for o_ref, (w, _), val in zip(out_refs, outs, res):
                o_ref[:, hh * w:(hh + 1) * w] = val.astype(o_ref.dtype)

    def blk(w):
        if rev:
            return pl.BlockSpec((bt, hps * w), lambda g, b: (nb - 1 - b, g))
        return pl.BlockSpec((bt, hps * w), lambda g, b: (b, g))

    return pl.pallas_call(
        kern, grid=(RET_HEADS // hps, nb),
        in_specs=[pl.BlockSpec((hps, 1, LANES), lambda g, b: (g, 0, 0))] + [blk(w) for _, w in ins],
        out_specs=[blk(w) for w, _ in outs],
        out_shape=[jax.ShapeDtypeStruct((T, RET_HEADS * w), dt) for w, dt in outs],
        scratch_shapes=[pltpu.VMEM((hps, RET_DK, RET_DV), F32), pltpu.VMEM((hps, bt, bt), F32)], name=name,
        compiler_params=_params(("arbitrary", "arbitrary")))(lgam, *[a for a, _ in ins])


def _ret_fwd(rq, rk, rv, lgam):
    def per_head(dmat, xi, zeta, gb, s_ref, q, k, v):
        sc = _dot_nt(q, k) * dmat
        y = _dot(_bf(sc), v) + _dot(q, _bf(s_ref[...])) * xi
        s_ref[...] = s_ref[...] * gb + _dot_tn(_bf(k.astype(F32) * zeta), v)
        return (y,)

    return _ret_scan("ret_fwd", per_head, lgam, [(rq, RET_DK), (rk, RET_DK), (rv, RET_DV)], [(RET_DV, F32)], False)[0]


def _ret_bwd_q(rq, rk, rv, dy, lgam):
    def per_head(dmat, xi, zeta, gb, s_ref, k, v, dy):
        dp = _dot_nt(dy, v) * dmat
        dq = _dot(_bf(dp), k) + _dot_nt(dy, _bf(s_ref[...])) * xi
        s_ref[...] = s_ref[...] * gb + _dot_tn(_bf(k.astype(F32) * zeta), v)
        return (dq,)

    return _ret_scan("ret_bwd_q", per_head, lgam, [(rk, RET_DK), (rv, RET_DV), (dy, RET_DV)], [(RET_DK, F32)], False)[0]


def _ret_bwd_kv(rq, rk, rv, dy, lgam):
    def per_head(dmat, xi, zeta, gb, g_ref, q, k, v, dy):
        gs = _bf(g_ref[...])
        p = _dot_nt(q, k) * dmat
        dp = _dot_nt(dy, v) * dmat
        dv = _dot_tn(_bf(p), dy) + _dot(k, gs) * zeta
        dk = _dot_tn(_bf(dp), q) + _dot_nt(v, gs) * zeta
        g_ref[...] = g_ref[...] * gb + _dot_tn(_bf(q.astype(F32) * xi), dy)
        return dk, dv

    return _ret_scan("ret_bwd_kv", per_head, lgam, [(rq, RET_DK), (rk, RET_DK), (rv, RET_DV), (dy, RET_DV)],
                     [(RET_DK, F32), (RET_DV, BF16)], True)


def _attn_mask_t(tb):
    key = lax.broadcasted_iota(jnp.int32, (tb, tb), 0)
    qry = lax.broadcasted_iota(jnp.int32, (tb, tb), 1)
    return _chunk_of(key) <= _chunk_of(qry)


MASKED = -1e30
SUBLANES = 8


def _head_blocks(nb, w, tb):
    return pl.BlockSpec((nb, None, w, tb), lambda h, i: (0, h, 0, 0))


def _one_block(w, tb):
    return pl.BlockSpec((None, None, w, tb), lambda h, i: (i, h, 0, 0))


def _attn_fwd(k, qt, vt, exchange=()):
    T = k.shape[0]
    tb = _attn_block(T)
    nb = T // tb

    n_ex = len(exchange)

    def kern(qt_ref, k_ref, vt_ref, *refs):
        ex_in, (o_ref, lser_ref), ex_out = refs[:n_ex], refs[n_ex:n_ex + 2], refs[n_ex + 2:2 * n_ex + 2]
        m_ref, acc_ref, sa_ref, sb_ref = refs[2 * n_ex + 2:2 * n_ex + 6]
        sems = refs[2 * n_ex + 6:]
        qb = pl.program_id(1)
        first = jnp.logical_and(pl.program_id(0) == 0, qb == 0)
        last = jnp.logical_and(pl.program_id(0) == MLA_HEADS - 1, qb == nb - 1)
        if n_ex:
            @pl.when(first)
            def _():
                for cp in _chip_copies(ex_in, ex_out, *sems, True):
                    cp.start()

        qt = qt_ref[...]
        m_ref[...] = jnp.full_like(m_ref, MASKED)
        acc_ref[...] = jnp.zeros_like(acc_ref)

        def scores(kb):
            rows = pl.ds(pl.multiple_of(kb * tb, tb), tb)
            return _dot(k_ref[rows, :], qt)

        def update(s, kb):
            m_old = m_ref[...]
            m_new = jnp.maximum(m_old, jnp.max(s, axis=0, keepdims=True))
            p = jnp.exp2(s - m_new)
            acc_ref[...] = acc_ref[...] * jnp.exp2(m_old - m_new) + _dot(vt_ref[kb], _bf(p))
            m_ref[...] = m_new

        def masked(s):
            return jnp.where(_attn_mask_t(tb), s, MASKED)

        sa_ref[...] = scores(0)

        def pair_body(j, carry):
            sb_ref[...] = scores(2 * j + 1)
            update(sa_ref[...], 2 * j)
            sa_ref[...] = scores(2 * j + 2)
            update(sb_ref[...], 2 * j + 1)
            return carry

        lax.fori_loop(0, qb // 2, pair_body, 0)

        @pl.when(qb % 2 == 0)
        def _():
            update(masked(sa_ref[...]), qb)

        @pl.when(qb % 2 == 1)
        def _():
            sb_ref[...] = masked(scores(qb))
            update(sa_ref[...], qb - 1)
            update(sb_ref[...], qb)

        l = acc_ref[MLA_DV:MLA_DV + 1, :]
        o_ref[...] = (acc_ref[:MLA_DV, :] / l).T
        lser_ref[...] = jnp.broadcast_to(m_ref[...] + jnp.log2(l), (SUBLANES, tb))
        if n_ex:
            @pl.when(last)
            def _():
                _wait_copies(_chip_copies(ex_in, ex_out, *sems, True))

    return pl.pallas_call(
        kern, grid=(MLA_HEADS, nb),
        in_specs=[_one_block(MLA_QK, tb), pl.BlockSpec((T, MLA_QK), lambda h, i: (0, h)),
                  _head_blocks(nb, MLA_DV + V_ONES, tb)] + [HBM_SPEC] * n_ex,
        out_specs=[pl.BlockSpec((tb, MLA_DV), lambda h, i: (i, h)), _one_block(SUBLANES, tb)] + [HBM_SPEC] * n_ex,
        out_shape=[jax.ShapeDtypeStruct((T, MLA_HEADS * MLA_DV), F32),
                   jax.ShapeDtypeStruct((nb, MLA_HEADS, SUBLANES, tb), F32)] + _exchange_shapes(exchange),
        scratch_shapes=[pltpu.VMEM((1, tb), F32), pltpu.VMEM((MLA_DV + V_ONES, tb), F32),
                        pltpu.VMEM((tb, tb), F32), pltpu.VMEM((tb, tb), F32)]
        + (_dma_sems(n_ex * N_PEER_CHIPS) if n_ex else []),
        name="attn_fwd", compiler_params=_params(("arbitrary", "arbitrary")))(qt, k, vt, *exchange)


def _attn_bwd(q, k, v, do, qt, kt, dot_, lse_rows, delta_rows):
    T = q.shape[0]
    tb = _attn_block(T)
    nb = T // tb

    def kern(q_ref, k_ref, v_ref, do_ref, qt_ref, kt_ref, dot_ref, lse_ref, dl_ref, dk_ref, dv_ref, dqt_ref, dv_acc,
             sa_ref, pa_ref, sb_ref, pb_ref):
        kb = pl.program_id(1)
        kv, vv, ktv = k_ref[...], v_ref[...], kt_ref[...]
        dk_ref[...] = jnp.zeros_like(dk_ref)
        dv_acc[...] = jnp.zeros_like(dv_acc)

        @pl.when(kb == 0)
        def _():
            dqt_ref[...] = jnp.zeros_like(dqt_ref)

        def products(qb, s_ref, dp_ref, diagonal=False):
            s = _dot(kv, qt_ref[qb])
            s_ref[...] = jnp.where(_attn_mask_t(tb), s, MASKED) if diagonal else s
            dp_ref[...] = _dot(vv, dot_ref[qb])

        def consume(qb, s_ref, dp_ref):
            rows = pl.ds(pl.multiple_of(qb * tb, tb), tb)
            p = jnp.exp2(s_ref[...] - lse_ref[qb][:1, :])
            dv_acc[...] += _dot(_bf(p), do_ref[rows, :])
            ds = _bf(p * (dp_ref[...] - dl_ref[qb][:1, :]))
            dk_ref[...] += _dot(ds, q_ref[rows, :])
            dqt_ref[qb] += _dot(ktv, ds)

        n_full = nb - 1 - kb
        products(kb, sa_ref, pa_ref, diagonal=True)

        def pair_body(j, carry):
            q1 = kb + 1 + 2 * j
            products(q1, sb_ref, pb_ref)
            consume(q1 - 1, sa_ref, pa_ref)
            products(q1 + 1, sa_ref, pa_ref)
            consume(q1, sb_ref, pb_ref)
            return carry

        lax.fori_loop(0, n_full // 2, pair_body, 0)

        @pl.when(n_full % 2 == 0)
        def _():
            consume(nb - 1, sa_ref, pa_ref)

        @pl.when(n_full % 2 == 1)
        def _():
            products(nb - 1, sb_ref, pb_ref)
            consume(nb - 2, sa_ref, pa_ref)
            consume(nb - 1, sb_ref, pb_ref)

        dk_ref[...] = dk_ref[...] * (ATTN_SCALE / Q_PRESCALE)
        dv_ref[...] = _bf(dv_acc[...])

    def blk(w):
        return pl.BlockSpec((tb, w), lambda h, i: (i, h))

    def full(w):
        return pl.BlockSpec((T, w), lambda h, i: (0, h))

    return pl.pallas_call(
        kern, grid=(MLA_HEADS, nb),
        in_specs=[full(MLA_QK), blk(MLA_QK), blk(MLA_DV), full(MLA_DV), _head_blocks(nb, MLA_QK, tb),
                  _one_block(MLA_QK, tb), _head_blocks(nb, MLA_DV, tb), _head_blocks(nb, SUBLANES, tb),
                  _head_blocks(nb, SUBLANES, tb)],
        out_specs=[blk(MLA_QK), blk(MLA_DV), _head_blocks(nb, MLA_QK, tb)],
        out_shape=[jax.ShapeDtypeStruct((T, MLA_HEADS * MLA_QK), F32),
                   jax.ShapeDtypeStruct((T, MLA_HEADS * MLA_DV), BF16),
                   jax.ShapeDtypeStruct((nb, MLA_HEADS, MLA_QK, tb), F32)],
        scratch_shapes=[pltpu.VMEM((tb, MLA_DV), F32)] + [pltpu.VMEM((tb, tb), F32)] * 4,
        name="attn_bwd", compiler_params=_params(("arbitrary", "arbitrary")))(
            q, k, v, do, qt, kt, dot_, lse_rows, delta_rows)


def _group_norm(y):
    yc = y - _mean(y)
    rstd = lax.rsqrt(_mean(yc * yc) + EPS)
    return yc * rstd, rstd


def _mix_fwd(y, rg, o, gates, h1, gn_g, w_ret_o, w_mla_o, w_out, ln_g, ln_b, tm):
    T, D = h1.shape

    def body(i, y_ref, rg_ref, o_ref, gt_ref, h_ref, gn_ref, wr_ref, wm_ref, wo_ref, g_ref, b_ref,
             h2_ref, z_ref, yret_ref, ymla_ref, yr_ref, mix_ref):
        for h in range(RET_HEADS):
            sl = slice(h * RET_DV, (h + 1) * RET_DV)
            yn, _ = _group_norm(y_ref[:, sl])
            r = rg_ref[:, sl]
            yr_ref[:, sl] = _bf(r * _sigmoid(r) * (yn * gn_ref[:, sl]))
        yret = _dot(yr_ref[...], wr_ref[...])
        ymla = _dot(_bf(o_ref[...]), wm_ref[...])
        yret_ref[...] = yret
        ymla_ref[...] = ymla
        mix = _bf(_sigmoid(gt_ref[:, :D]) * yret + _sigmoid(gt_ref[:, D:]) * ymla)
        mix_ref[...] = mix
        z = ALPHA * h_ref[...] + _dot(mix, wo_ref[...])
        xhat, _ = _ln_stats(z)
        z_ref[...] = z
        h2_ref[...] = xhat * g_ref[...] + b_ref[...]

    return _rowcall("mix_fwd", body, T, tm, [y, rg, o, gates, h1], [gn_g, w_ret_o, w_mla_o, w_out, ln_g, ln_b],
                    [(D, F32), (D, F32), (D, F32), (D, F32), (RET_HEADS * RET_DV, BF16), (D, BF16)])


def _mix_bwd(dh2, z1, gates, yret, ymla, y, rg, o, gn_g, w_ret_o, w_mla_o, w_out, ln_g, tm, exchange=None):
    T, D = dh2.shape
    rv = RET_HEADS * RET_DV

    def body(i, dh_ref, z_ref, gt_ref, yret_ref, ymla_ref, y_ref, rg_ref, o_ref, gn_ref, wr_ref, wm_ref, wo_ref, g_ref,
             dz_ref, dgt_ref, drg_ref, dy_ref, do_ref, dyret_ref, dymla_ref, dg_ref, db_ref, dgn_ref, dot_ref,
             dl_ref):
        xhat, rstd = _ln_stats(z_ref[...])
        dz, dg, db = _ln_bwd(dh_ref[...], xhat, rstd, g_ref[...])
        _acc(i, dg_ref, dg)
        _acc(i, db_ref, db)
        dz_ref[...] = dz
        dmix = _dot_nt(_bf(dz), wo_ref[...])
        sr = _sigmoid(gt_ref[:, :D])
        sm = _sigmoid(gt_ref[:, D:])
        dgt_ref[:, :D] = _bf(dmix * yret_ref[...] * sr * (1.0 - sr))
        dgt_ref[:, D:] = _bf(dmix * ymla_ref[...] * sm * (1.0 - sm))
        dyret = _bf(dmix * sr)
        dymla = _bf(dmix * sm)
        dyret_ref[...] = dyret
        dymla_ref[...] = dymla
        dov = _dot_nt(dymla, wm_ref[...])
        do_ref[...] = _bf(dov)
        for h in range(MLA_HEADS):
            sl = slice(h * MLA_DV, (h + 1) * MLA_DV)
            dot_ref[h] = _bf(dov[:, sl].T)
            delta = jnp.sum(dov[:, sl] * o_ref[:, sl], axis=-1, keepdims=True)
            dl_ref[h] = jnp.broadcast_to(delta, (tm, LANES)).T[:SUBLANES, :]
        dyr = _dot_nt(dyret, wr_ref[...])
        dgn = []
        for h in range(RET_HEADS):
            sl = slice(h * RET_DV, (h + 1) * RET_DV)
            yn, grstd = _group_norm(y_ref[:, sl])
            r = rg_ref[:, sl]
            sig = _sigmoid(r)
            d = dyr[:, sl]
            drg_ref[:, sl] = _bf(d * (yn * gn_ref[:, sl]) * sig * (1.0 + r * (1.0 - sig)))
            dt = d * (r * sig)
            dgn.append(jnp.sum(dt * yn, axis=0, keepdims=True))
            dyn = dt * gn_ref[:, sl]
            dy_ref[:, sl] = _bf(grstd * (dyn - _mean(dyn) - yn * _mean(dyn * yn)))
        _acc(i, dgn_ref, jnp.concatenate(dgn, axis=1))

    return _rowcall("mix_bwd", body, T, tm, [dh2, z1, gates, yret, ymla, y, rg, o],
                    [gn_g, w_ret_o, w_mla_o, w_out, ln_g],
                    [(D, F32), (2 * D, BF16), (rv, BF16), (rv, BF16), (MLA_HEADS * MLA_DV, BF16), (D, BF16), (D, BF16)],
                    [((1, D), F32), ((1, D), F32), ((1, rv), F32)],
                    tiled_outs=[_transposed_blocks(T, tm, MLA_DV, BF16), _transposed_blocks(T, tm, SUBLANES, F32)],
                    exchange=exchange)


def _proj_mla_bwd(dqt, dk, dv, lat, tabs, w_uq, w_uk, w_uv, qn_g, kvn_g, tm):
    T = dk.shape[0]
    H = MLA_HEADS
    lat_w = Q_LORA + KV_LORA

    def body(i, dk_ref, dv_ref, lat_ref, c_ref, s1_ref, s2_ref, dqt_ref, wuq_ref, wuk_ref, wuv_ref, qg_ref, kg_ref,
             dlat_ref, dkpe_ref, dqb_ref, dkn_ref, dqg_ref, dkg_ref):
        c, s1, s2 = c_ref[...], s1_ref[...], s2_ref[...]
        dkpe = jnp.zeros((tm, LANES), F32)
        for h in range(H):
            o = h * MLA_QK
            dqh = dqt_ref[h].T * ATTN_SCALE
            dqb_ref[:, o:o + MLA_NOPE] = _bf(dqh[:, :MLA_NOPE])
            dqb_ref[:, o + MLA_NOPE:o + MLA_QK] = _bf(_rope_pe_bwd(dqh[:, MLA_NOPE:], c, s1, s2))
            dkn_ref[:, h * MLA_NOPE:(h + 1) * MLA_NOPE] = _bf(dk_ref[:, o:o + MLA_NOPE])
            dkpe += dk_ref[:, o + MLA_NOPE:o + MLA_QK]
        dkpe_ref[...] = _bf(_rope_pe_bwd(dkpe, c, s1, s2))
        dcqn = _dot_nt(dqb_ref[...], wuq_ref[...])
        dckn = _dot_nt(dkn_ref[...], wuk_ref[...]) + _dot_nt(dv_ref[...], wuv_ref[...])
        for dn, x, g_ref, dg_ref, sl in ((dcqn, lat_ref[:, :Q_LORA], qg_ref, dqg_ref, slice(0, Q_LORA)),
                                         (dckn, lat_ref[:, Q_LORA:], kg_ref, dkg_ref, slice(Q_LORA, lat_w))):
            xn, r = _rms(x, None)
            _acc(i, dg_ref, jnp.sum(dn * xn, axis=0, keepdims=True))
            dxn = dn * g_ref[...]
            dlat_ref[:, sl] = _bf(r * (dxn - xn * _mean(dxn * xn)))

    dqt_shape, dqt_spec = _transposed_blocks(T, tm, MLA_QK, F32)
    assert dqt.shape == dqt_shape.shape
    return _rowcall("proj_mla_bwd", body, T, tm, [dk, dv, lat, *tabs], [w_uq, w_uk, w_uv, qn_g, kvn_g],
                    [(lat_w, BF16), (LANES, BF16), (H * MLA_QK, BF16), (H * MLA_NOPE, BF16)],
                    [((1, Q_LORA), F32), ((1, KV_LORA), F32)], tiled_ins=[(dqt, dqt_spec)])


def _proj_bwd(drq, drk, drv, drg, dz1, dlat, dkpe, dgates, cos_r, sin_r, w_r, w_c, w_kpe, w_g, tm):
    T, D = dz1.shape
    qk = RET_HEADS * RET_DK
    rv = RET_HEADS * RET_DV

    def body(i, drq_ref, drk_ref, drv_ref, drg_ref, dz_ref, dlat_ref, dkpe_ref, dgt_ref, cos_ref, sin_ref,
             wr_ref, wc_ref, wk_ref, wg_ref, dh_ref, dpr_ref):
        cos, sin = cos_ref[...], sin_ref[...]
        for src, off, scale in ((drq_ref, 0, 1.0), (drk_ref, qk, RET_DK ** -0.5)):
            for h in range(RET_HEADS):
                d = src[:, h * RET_DK:(h + 1) * RET_DK]
                dpr_ref[:, off + h * RET_DK:off + (h + 1) * RET_DK] = _bf(
                    (d * cos + _roll(d * sin, RET_DK // 2)) * scale)
        dpr_ref[:, 2 * qk:2 * qk + rv] = drv_ref[...]
        dpr_ref[:, 2 * qk + rv:] = drg_ref[...]
        dh_ref[...] = (ALPHA * dz_ref[...] + _dot_nt(dpr_ref[...], wr_ref[...]) + _dot_nt(dlat_ref[...], wc_ref[...])
                       + _dot_nt(dkpe_ref[...], wk_ref[...]) + _dot_nt(dgt_ref[...], wg_ref[...]))

    return _rowcall("proj_bwd", body, T, tm, [drq, drk, drv, drg, dz1, dlat, dkpe, dgates, cos_r, sin_r],
                    [w_r, w_c, w_kpe, w_g], [(D, F32), (2 * qk + 2 * rv, BF16)])


def _ple_loss(h3, p, target, w_gate, w_proj, ln_g, ln_b, tm):
    T, D = h3.shape

    def body(i, h_ref, p_ref, t_ref, wg_ref, wp_ref, g_ref, b_ref, dh_ref, dgp_ref, dpp_ref, loss_ref, dg_ref, db_ref):
        hv = h_ref[...]
        sg = _sigmoid(_dot(_bf(hv), wg_ref[...]))
        pp = _dot(_bf(p_ref[...]), wp_ref[...])
        xhat, rstd = _ln_stats(ALPHA * hv + sg * pp)
        err = xhat * g_ref[...] + b_ref[...] - t_ref[...]
        row_loss = 0.5 * _mean(err * err)
        _acc(i, loss_ref, jnp.broadcast_to(jnp.sum(row_loss, axis=0, keepdims=True), (1, LANES)))
        dz, dg, db = _ln_bwd(err * (1.0 / D), xhat, rstd, g_ref[...])
        _acc(i, dg_ref, dg)
        _acc(i, db_ref, db)
        dgp = _bf(dz * pp * sg * (1.0 - sg))
        dgp_ref[...] = dgp
        dpp_ref[...] = _bf(dz * sg)
        dh_ref[...] = ALPHA * dz + _dot_nt(dgp, wg_ref[...])

    return _rowcall("ple_loss", body, T, tm, [h3, p, target], [w_gate, w_proj, ln_g, ln_b],
                    [(D, F32), (D, BF16), (D, BF16)], [((1, LANES), F32), ((1, D), F32), ((1, D), F32)])


def _ewise(name, fn, ins, n_out, out_dtype=F32):
    r, c = ins[0].shape
    tr = _tile(r, max(8, (1 << 19) // c // 8 * 8), 8)

    def kern(*refs):
        outs = fn(*[x[...] for x in refs[:len(ins)]])
        for o_ref, o in zip(refs[len(ins):], outs):
            o_ref[...] = o.astype(out_dtype)

    spec = pl.BlockSpec((tr, c), lambda i: (i, 0))
    return pl.pallas_call(kern, grid=(r // tr,), in_specs=[spec] * len(ins), out_specs=[spec] * n_out,
                          out_shape=[jax.ShapeDtypeStruct((r, c), out_dtype)] * n_out, name=name,
                          compiler_params=_params(("arbitrary",)))(*ins)


def _adamw_math(w, g, m, v):
    m = ADAM_B1 * m + (1.0 - ADAM_B1) * g
    v = ADAM_B2 * v + (1.0 - ADAM_B2) * (g * g)
    m_hat = m / (1.0 - ADAM_B1 ** ADAM_STEP)
    v_hat = v / (1.0 - ADAM_B2 ** ADAM_STEP)
    return -ADAM_LR * (m_hat / (jnp.sqrt(v_hat) + ADAM_EPS) + ADAM_WD * w), m, v


def _adamw(name, w, g, m, v):
    shape = w.shape
    c = shape[-1]
    flat = [t.reshape(-1, c) for t in (w, g, m, v)]
    return [t.reshape(shape) for t in _ewise(name, _adamw_math, flat, 3)]


def _place():
    return lax.axis_index("x"), lax.axis_index("y"), lax.axis_index("c")


def _dma_sems(n):
    return [pltpu.SemaphoreType.DMA((n,)), pltpu.SemaphoreType.DMA((n,))]


N_PEER_CHIPS = N_CHIPS - 1


def _chips_exchange(name, srcs, broadcast):
    n = len(srcs)

    def kern(*refs):
        cps = _chip_copies(refs[:n], refs[n:2 * n], refs[2 * n], refs[2 * n + 1], broadcast)
        for cp in cps:
            cp.start()
        _wait_copies(cps)

    return pl.pallas_call(
        kern, out_shape=_exchange_shapes(srcs), in_specs=[HBM_SPEC] * n, out_specs=[HBM_SPEC] * n,
        scratch_shapes=_dma_sems(n * N_PEER_CHIPS), name=name)(*srcs)


def _exchange_shapes(srcs):
    return [jax.ShapeDtypeStruct((N_PEER_CHIPS,) + s.shape[1:], s.dtype) for s in srcs]


def _chip_copies(src_refs, out_refs, send_sems, recv_sems, broadcast):
    x, y, c = _place()
    peers = [(1 - x, y), (x, 1 - y), (1 - x, 1 - y)]
    cps = []
    for j, (px, py) in enumerate(peers):
        for a, (src_ref, out_ref) in enumerate(zip(src_refs, out_refs)):
            piece = src_ref.at[c] if broadcast else src_ref.at[2 * px + py]
            cps.append(pltpu.make_async_remote_copy(
                src_ref=piece, dst_ref=out_ref.at[j], send_sem=send_sems.at[a * N_PEER_CHIPS + j],
                recv_sem=recv_sems.at[a * N_PEER_CHIPS + j], device_id=(px, py, c), device_id_type=MESH))
    return cps


def _wait_copies(cps):
    for cp in cps:
        cp.wait_recv()
    for cp in cps:
        cp.wait_send()


def _sibling_swap(name, srcs, halves):
    n = len(srcs)

    def kern(*refs):
        src_refs, out_refs = refs[:n], refs[n:2 * n]
        send_sems, recv_sems = refs[2 * n:]
        x, y, c = _place()

        def copy(a):
            piece = src_refs[a].at[:, 1 - c] if halves else src_refs[a]
            return pltpu.make_async_remote_copy(
                src_ref=piece, dst_ref=out_refs[a], send_sem=send_sems.at[a], recv_sem=recv_sems.at[a],
                device_id=(x, y, 1 - c), device_id_type=MESH)

        cps = [copy(a) for a in range(n)]
        for cp in cps:
            cp.start()
        for cp in cps:
            cp.wait_recv()
        for cp in cps:
            cp.wait_send()

    def out_shape(s):
        return jax.ShapeDtypeStruct((s.shape[0],) + s.shape[2:] if halves else s.shape, s.dtype)

    return pl.pallas_call(
        kern, out_shape=[out_shape(s) for s in srcs], in_specs=[HBM_SPEC] * n, out_specs=[HBM_SPEC] * n,
        scratch_shapes=_dma_sems(n), name=name)(*srcs)


def _all_devices(name, src, reduce):
    r, c = src.shape
    n_dev = 2 * N_CHIPS

    def kern(src_ref, out_ref, *scratch):
        if reduce:
            gat_ref, send_sems, recv_sems = scratch
        else:
            gat_ref = out_ref
            send_sems, recv_sems = scratch
        x, y, cc = _place()
        me = 4 * x + 2 * y + cc
        gat_ref[me] = src_ref[...]
        peers = []
        for j in range(1, n_dev):
            px = 1 - x if j & 4 else x
            py = 1 - y if j & 2 else y
            pc = 1 - cc if j & 1 else cc
            peers.append((px, py, pc))

        def copy(j, peer, slot):
            return pltpu.make_async_remote_copy(
                src_ref=src_ref, dst_ref=gat_ref.at[slot], send_sem=send_sems.at[j], recv_sem=recv_sems.at[j],
                device_id=peer, device_id_type=MESH)

        sends = [copy(j, peer, me) for j, peer in enumerate(peers)]
        for cp in sends:
            cp.start()
        for j, (px, py, pc) in enumerate(peers):
            copy(j, (px, py, pc), 4 * px + 2 * py + pc).wait_recv()
        for cp in sends:
            cp.wait_send()
        if reduce:
            total = gat_ref[0]
            for d in range(1, n_dev):
                total = total + gat_ref[d]
            out_ref[...] = total

    out_shape = jax.ShapeDtypeStruct((r, c) if reduce else (n_dev, r, c), src.dtype)
    scratch = ([pltpu.VMEM((n_dev, r, c), src.dtype)] if reduce else []) + _dma_sems(n_dev - 1)
    return pl.pallas_call(kern, out_shape=out_shape, in_specs=[VMEM_SPEC], out_specs=VMEM_SPEC,
                          scratch_shapes=scratch, name=name)(src)


def _halves(t, axis):
    return t.reshape(t.shape[:axis] + (2, t.shape[axis] // 2) + t.shape[axis + 1:])


def _by_core(mine, theirs, axis):
    c = lax.axis_index("c")
    both = jnp.where(c == 0, jnp.stack([mine, theirs], axis), jnp.stack([theirs, mine], axis))
    return both.reshape(both.shape[:axis] + (2 * both.shape[axis + 1],) + both.shape[axis + 2:])


def _whole_weight(name, own, mine, theirs):
    me = 2 * lax.axis_index("x") + lax.axis_index("y")
    c = lax.axis_index("c")
    slot_of_flip = jnp.asarray((0, 1, 0, 2), jnp.int32)
    blocks = []
    for k in range(N_CHIPS):
        slot = slot_of_flip[k ^ me]
        m = lax.dynamic_index_in_dim(mine, slot, 0, keepdims=False)
        t = lax.dynamic_index_in_dim(theirs, slot, 0, keepdims=False)
        other = jnp.where(c == 0, jnp.concatenate([m, t], axis=0), jnp.concatenate([t, m], axis=0))
        blocks.append(jnp.where(k == me, own, other))
    if name in ("ffn1_w_in", "ffn2_w_in"):
        return jnp.stack(blocks)
    return jnp.concatenate(blocks, axis=1 if name in COL_SHARDED else 0)


def _split_shards(name, full):
    if full.ndim == 3:
        return full
    r, c = full.shape
    if name in COL_SHARDED:
        return jnp.stack([full[:, k * (c // N_CHIPS):(k + 1) * (c // N_CHIPS)] for k in range(N_CHIPS)])
    return full.reshape(N_CHIPS, r // N_CHIPS, c)


def _rope_tables(positions):
    pos = positions.reshape(-1).astype(F32)[:, None]
    half = RET_DK // 2
    ang = pos * (ROPE_BASE ** (-jnp.arange(half, dtype=F32) / half))
    cos_r = jnp.concatenate([jnp.cos(ang)] * 2, axis=1)
    sin_r = jnp.concatenate([-jnp.sin(ang), jnp.sin(ang)], axis=1)
    half = MLA_ROPE // 2
    ang = pos * (ROPE_BASE ** (-jnp.arange(half, dtype=F32) / half))
    zeros = jnp.zeros_like(ang)
    rest = LANES - MLA_ROPE
    c = jnp.concatenate([jnp.cos(ang)] * 2 + [jnp.ones((ang.shape[0], rest), F32)], axis=1)
    s1 = jnp.concatenate([-jnp.sin(ang), zeros, jnp.zeros((ang.shape[0], rest), F32)], axis=1)
    s2 = jnp.concatenate([zeros, jnp.sin(ang), jnp.zeros((ang.shape[0], rest), F32)], axis=1)
    return cos_r, sin_r, (c, s1, s2)


GATHER_GROUPS = (("ffn1_w_in", "ffn1_w_out"), ("w_in", "w_uq", "w_ukv"),
                 ("w_ret_o", "w_mla_o", "w_out", "ffn2_w_in", "ffn2_w_out", "ple_w_gate", "ple_w_proj"))
REDUCE_GROUPS = (("ple_w_gate", "ple_w_proj", "ffn2_w_in", "ffn2_w_out"),
                 ("w_out", "w_ret_o", "w_mla_o", "w_uq", "w_ukv", "w_in"), ("ffn1_w_in", "ffn1_w_out"))


def _gathered(tag, names, own, mine):
    theirs = _sibling_swap("gather_cores_" + tag, mine, False)
    return {n: _whole_weight(n, own[n], m, t) for n, m, t in zip(names, mine, theirs)}


def _chip_sums(tag, names, grads):
    c = lax.axis_index("c")
    halves = [_halves(_split_shards(n, grads[n]), 1) for n in names]
    theirs = _sibling_swap("reduce_cores_" + tag, halves, True)
    sums = []
    for n, g, t in zip(names, halves, theirs):
        mine = lax.dynamic_index_in_dim(g, c, axis=1, keepdims=False)
        k, r, cc = mine.shape
        sums.append(_ewise("reduce_cores_add_" + n, lambda a, b: (a.astype(F32) + b.astype(F32),),
                           [mine.reshape(k * r, cc), t.reshape(k * r, cc)], 1, BF16)[0].reshape(k, r, cc))
    return sums


def _block_totals(names, sums, parts):
    me = 2 * lax.axis_index("x") + lax.axis_index("y")
    totals = []
    for n, s, pt in zip(names, sums, parts):
        own = lax.dynamic_index_in_dim(s, me, axis=0, keepdims=False)
        totals.append(_ewise("reduce_chips_add_" + n,
                             lambda a, b, c_, d: (((a.astype(F32) + b.astype(F32)) + c_.astype(F32)) + d.astype(F32),),
                             [own, pt[0], pt[1], pt[2]], 1, F32)[0])
    return totals


def _local_step(x, p, positions, target, shards, ln_g, ln_b, gn_g, qn_g, kvn_g):
    T, D = x.shape
    tm = min(256, T)
    H = MLA_HEADS
    qk, rv = RET_HEADS * RET_DK, RET_HEADS * RET_DV
    cos_r, sin_r, tabs = _rope_tables(positions)
    lgam = jnp.broadcast_to(jnp.log(1.0 - 2.0 ** (-5.0 - jnp.arange(RET_HEADS, dtype=F32)))[:, None, None],
                            (RET_HEADS, 1, LANES))
    lng = [ln_g[k:k + 1] for k in range(N_LN)]
    lnb = [ln_b[k:k + 1] for k in range(N_LN)]
    own = {n: _bf(shards[n]) for n in BIG_WEIGHTS}
    to_send = [[_halves(own[n], 0) for n in names] for names in GATHER_GROUPS]

    w = _gathered("a", GATHER_GROUPS[0], own, _chips_exchange("gather_chips_a", to_send[0], True))
    h1, z0, a1, *arrived = _ffn_fwd("ffn1_fwd", x, w["ffn1_w_in"], w["ffn1_w_out"], lng[0], lnb[0], tm,
                                    exchange=(to_send[1], True))
    w.update(_gathered("b", GATHER_GROUPS[1], own, arrived))

    w_in = w["w_in"]
    o_lat, o_kpe, o_gate = 2 * qk + 2 * rv, 2 * qk + 2 * rv + Q_LORA + KV_LORA, 2 * qk + 2 * rv + Q_LORA + KV_LORA + MLA_ROPE
    w_r, w_c = w_in[:, :o_lat], w_in[:, o_lat:o_kpe]
    w_kpe = jnp.pad(w_in[:, o_kpe:o_gate], ((0, 0), (0, LANES - MLA_ROPE)))
    w_g = w_in[:, o_gate:]
    w_uq = jnp.pad(w["w_uq"].reshape(Q_LORA, H, MLA_NOPE + MLA_ROPE),
                   ((0, 0), (0, 0), (0, MLA_QK - MLA_NOPE - MLA_ROPE))).reshape(Q_LORA, H * MLA_QK)
    w_ukv = w["w_ukv"].reshape(KV_LORA, H, MLA_NOPE + MLA_DV)
    w_uk = w_ukv[:, :, :MLA_NOPE].reshape(KV_LORA, H * MLA_NOPE)
    w_uv = w_ukv[:, :, MLA_NOPE:].reshape(KV_LORA, H * MLA_DV)

    rq, rk, rvv, rg = _proj_ret(h1, w_r, cos_r, sin_r, tm)
    lat, gates, q, k, v, latn, qt, kt, vt = _proj_mla(h1, tabs, w_c, w_kpe, w_g, w_uq, w_uk, w_uv, qn_g, kvn_g, tm)
    y = _ret_fwd(rq, rk, rvv, lgam)
    o, lse_rows, *arrived = _attn_fwd(k, qt, vt, exchange=to_send[2])
    w.update(_gathered("c", GATHER_GROUPS[2], own, arrived))
    h2, z1, yret, ymla, yr, mix = _mix_fwd(y, rg, o, gates, h1, gn_g, w["w_ret_o"], w["w_mla_o"], w["w_out"],
                                           lng[1], lnb[1], tm)
    h3, z2, a2 = _ffn_fwd("ffn2_fwd", h2, w["ffn2_w_in"], w["ffn2_w_out"], lng[2], lnb[2], tm)

    dh3, dgp, dpp, loss, dg3, db3 = _ple_loss(h3, p, target, w["ple_w_gate"], w["ple_w_proj"], lng[3], lnb[3], tm)
    dh2, da2, s2, df2, dg2, db2 = _ffn_bwd("ffn2_bwd", dh3, z2, a2, w["ffn2_w_in"], w["ffn2_w_out"], lng[2], tm)
    grads = {"ple_w_gate": _mm_tn("wg_ple_gate", h3, dgp), "ple_w_proj": _mm_tn("wg_ple_proj", p, dpp),
             "ffn2_w_in": _mm_tn("wg_ffn2_in", h2, da2, n_split=N_CHIPS), "ffn2_w_out": _mm_tn("wg_ffn2_out", s2, df2)}
    sums1 = _chip_sums("1", REDUCE_GROUPS[0], grads)
    (dz1, dgates, drg, dy, do, dyret, dymla, dg1, db1, dgn, dot_, delta_rows, *parts1) = _mix_bwd(
        dh2, z1, gates, yret, ymla, y, rg, o, gn_g, w["w_ret_o"], w["w_mla_o"], w["w_out"], lng[1], tm,
        exchange=(sums1, False))
    drq = _ret_bwd_q(rq, rk, rvv, dy, lgam)
    drk, drv = _ret_bwd_kv(rq, rk, rvv, dy, lgam)
    dk, dv, dqt = _attn_bwd(q, k, v, do, qt, kt, dot_, lse_rows, delta_rows)
    dlat, dkpe, dqb, dkn, dqg, dkg = _proj_mla_bwd(dqt, dk, dv, lat, tabs, w_uq, w_uk, w_uv, qn_g, kvn_g, tm)
    dh1, dpr = _proj_bwd(drq, drk, drv, drg, dz1, dlat, dkpe, dgates, cos_r, sin_r, w_r, w_c, w_kpe, w_g, tm)
    g_uq = _mm_tn("wg_uq", latn[:, :Q_LORA], dqb).reshape(Q_LORA, H, MLA_QK)[:, :, :MLA_NOPE + MLA_ROPE]
    g_uk = _mm_tn("wg_uk", latn[:, Q_LORA:], dkn).reshape(KV_LORA, H, MLA_NOPE)
    g_uv = _mm_tn("wg_uv", latn[:, Q_LORA:], dv).reshape(KV_LORA, H, MLA_DV)
    grads.update({
        "w_in": jnp.concatenate([_mm_tn("wg_in_r", h1, dpr), _mm_tn("wg_in_c", h1, dlat),
                                 _mm_tn("wg_in_kpe", h1, dkpe)[:, :MLA_ROPE], _mm_tn("wg_in_g", h1, dgates)], axis=1),
        "w_ret_o": _mm_tn("wg_ret_o", yr, dyret),
        "w_uq": g_uq.reshape(Q_LORA, H * (MLA_NOPE + MLA_ROPE)),
        "w_ukv": jnp.concatenate([g_uk, g_uv], axis=2).reshape(KV_LORA, H * (MLA_NOPE + MLA_DV)),
        "w_mla_o": _mm_tn("wg_mla_o", o, dymla),
        "w_out": _mm_tn("wg_out", mix, dz1)})
    sums2 = _chip_sums("2", REDUCE_GROUPS[1], grads)
    dx, da1, s1, df1, dg0, db0, *parts2 = _ffn_bwd("ffn1_bwd", dh1, z0, a1, w["ffn1_w_in"], w["ffn1_w_out"], lng[0], tm,
                                                   exchange=(sums2, False))
    grads.update({"ffn1_w_in": _mm_tn("wg_ffn1_in", x, da1, n_split=N_CHIPS),
                  "ffn1_w_out": _mm_tn("wg_ffn1_out", s1, df1)})
    sums3 = _chip_sums("3", REDUCE_GROUPS[2], grads)
    parts3 = _chips_exchange("reduce_chips_3", sums3, False)

    names = [n for group in REDUCE_GROUPS for n in group]
    totals = _block_totals(names, sums1 + sums2 + sums3, list(parts1) + list(parts2) + list(parts3))
    others = _sibling_swap("reduce_join", totals, False)
    reduced = {n: _by_core(t, o_, 0) for n, t, o_ in zip(names, totals, others)}
    small = {"ln_g": jnp.concatenate([dg0, dg1, dg2, dg3], axis=0), "ln_b": jnp.concatenate([db0, db1, db2, db3], axis=0),
             "ret_gn_g": dgn, "q_norm_g": dqg, "kv_norm_g": dkg}
    return loss[0, 0], dx, reduced, small


def kernel(x, p, positions, ln_g, ln_b, ffn1_w_in, ffn1_w_out, w_in, ret_gn_g, w_ret_o, q_norm_g, kv_norm_g, w_uq, w_ukv, w_mla_o, w_out, ffn2_w_in, ffn2_w_out, ple_w_gate, ple_w_proj, loss_target, m_ln_g, m_ln_b, m_ffn1_w_in, m_ffn1_w_out, m_w_in, m_ret_gn_g, m_w_ret_o, m_q_norm_g, m_kv_norm_g, m_w_uq, m_w_ukv, m_w_mla_o, m_w_out, m_ffn2_w_in, m_ffn2_w_out, m_ple_w_gate, m_ple_w_proj, v_ln_g, v_ln_b, v_ffn1_w_in, v_ffn1_w_out, v_w_in, v_ret_gn_g, v_w_ret_o, v_q_norm_g, v_kv_norm_g, v_w_uq, v_w_ukv, v_w_mla_o, v_w_out, v_ffn2_w_in, v_ffn2_w_out, v_ple_w_gate, v_ple_w_proj):
    names = ("ln_g", "ln_b", "ffn1_w_in", "ffn1_w_out", "w_in", "ret_gn_g", "w_ret_o", "q_norm_g", "kv_norm_g", "w_uq",
             "w_ukv", "w_mla_o", "w_out", "ffn2_w_in", "ffn2_w_out", "ple_w_gate", "ple_w_proj")
    weights = dict(zip(names, (ln_g, ln_b, ffn1_w_in, ffn1_w_out, w_in, ret_gn_g, w_ret_o, q_norm_g, kv_norm_g, w_uq,
                               w_ukv, w_mla_o, w_out, ffn2_w_in, ffn2_w_out, ple_w_gate, ple_w_proj)))
    m_in = dict(zip(names, (m_ln_g, m_ln_b, m_ffn1_w_in, m_ffn1_w_out, m_w_in, m_ret_gn_g, m_w_ret_o, m_q_norm_g,
                            m_kv_norm_g, m_w_uq, m_w_ukv, m_w_mla_o, m_w_out, m_ffn2_w_in, m_ffn2_w_out, m_ple_w_gate,
                            m_ple_w_proj)))
    v_in = dict(zip(names, (v_ln_g, v_ln_b, v_ffn1_w_in, v_ffn1_w_out, v_w_in, v_ret_gn_g, v_w_ret_o, v_q_norm_g,
                            v_kv_norm_g, v_w_uq, v_w_ukv, v_w_mla_o, v_w_out, v_ffn2_w_in, v_ffn2_w_out, v_ple_w_gate,
                            v_ple_w_proj)))
    chip = 2 * lax.axis_index("x") + lax.axis_index("y")
    D = x.shape[-1]
    dq = D // N_CHIPS

    shards = {n: weights[n][0] for n in BIG_WEIGHTS}
    ln_all = _all_devices("gather_ln", jnp.concatenate([ln_g[0], ln_b[0]], axis=0), False)
    ln_full = ln_all[::2].transpose(1, 0, 2).reshape(2 * N_LN, D)
    loss, dx, big, small = _local_step(x[0], p[0, 0], positions, loss_target[0], shards, ln_full[:N_LN],
                                       ln_full[N_LN:], ret_gn_g, q_norm_g, kv_norm_g)

    loss = lax.psum(loss, ("x", "y", "c"))
    small_names = ("ln_g", "ln_b", "ret_gn_g", "q_norm_g", "kv_norm_g")
    flat = jnp.concatenate([small[n].reshape(-1) for n in small_names])
    rows = -(-flat.shape[0] // LANES // 8) * 8
    flat = jnp.pad(flat, (0, rows * LANES - flat.shape[0])).reshape(rows, LANES)
    flat = _all_devices("reduce_small", flat, True).reshape(-1)
    off = 0
    for n in small_names:
        size = small[n].size
        small[n] = flat[off:off + size].reshape(small[n].shape)
        off += size
    g_out = dict(big)
    for n in ("ln_g", "ln_b"):
        g_out[n] = lax.dynamic_slice_in_dim(small[n], chip * dq, dq, axis=1)
    for n in ("ret_gn_g", "q_norm_g", "kv_norm_g"):
        g_out[n] = small[n]

    deltas, new_m, new_v = {}, {}, {}
    for n in names:
        g = g_out[n].reshape(weights[n].shape)
        g_out[n] = g
        deltas[n], new_m[n], new_v[n] = _adamw("adamw_" + n, weights[n], g, m_in[n], v_in[n])
    return (loss, dx[None], *[g_out[n] for n in names], *[deltas[n] for n in names], *[new_m[n] for n in names],
            *[new_v[n] for n in names])
```

```python
import functools

import jax
import jax.numpy as jnp
from jax import lax
from jax.experimental import pallas as pl
from jax.experimental.pallas import tpu as pltpu

D_MODEL = 1024
CHUNK = 64
D_PLE = 256
D_FF = 2816
RET_HEADS = 8
RET_DK = 128
RET_DV = 256
MLA_HEADS = 8
MLA_NOPE = 128
MLA_ROPE = 64
MLA_DV = 128
MLA_QK = 256
Q_LORA = 256
KV_LORA = 256
ROPE_BASE = 10000.0
EPS = 1e-5
N_LN = 4
ALPHA = 2.0 ** 0.25
ADAM_LR = 0.001
ADAM_B1 = 0.9
ADAM_B2 = 0.999
ADAM_EPS = 1e-08
ADAM_WD = 0.01
ADAM_STEP = 10

LANES = 128
VMEM_LIMIT = 60 << 20
N_CHIPS = 4

F32 = jnp.float32
BF16 = jnp.bfloat16
MESH = pl.DeviceIdType.MESH
HBM_SPEC = pl.BlockSpec(memory_space=pltpu.HBM)
VMEM_SPEC = pl.BlockSpec(memory_space=pltpu.VMEM)

BIG_WEIGHTS = ("ffn1_w_in", "ffn1_w_out", "w_in", "w_ret_o", "w_uq", "w_ukv", "w_mla_o", "w_out",
               "ffn2_w_in", "ffn2_w_out", "ple_w_gate", "ple_w_proj")
COL_SHARDED = ("ffn1_w_in", "w_in", "w_uq", "w_ukv", "ffn2_w_in", "ple_w_proj")


def _dot(a, b):
    return jnp.dot(a, b, preferred_element_type=F32)


def _dot_nt(a, b):
    return lax.dot_general(a, b, (((1,), (1,)), ((), ())), preferred_element_type=F32)


def _dot_tn(a, b):
    return lax.dot_general(a, b, (((0,), (0,)), ((), ())), preferred_element_type=F32)


def _bf(x):
    return x.astype(BF16)


def _sigmoid(x):
    return 0.5 * jnp.tanh(0.5 * x) + 0.5


def _mean(x):
    return jnp.mean(x, axis=-1, keepdims=True)


def _ln_stats(z):
    zc = z - _mean(z)
    rstd = lax.rsqrt(_mean(zc * zc) + EPS)
    return zc * rstd, rstd


def _ln_bwd(dy, xhat, rstd, g):
    dxhat = dy * g
    dz = rstd * (dxhat - _mean(dxhat) - xhat * _mean(dxhat * xhat))
    return dz, jnp.sum(dy * xhat, axis=0, keepdims=True), jnp.sum(dy, axis=0, keepdims=True)


def _roll(x, shift):
    return pltpu.roll(x, shift, 1)


def _chunk_of(idx):
    return jnp.right_shift(idx, CHUNK.bit_length() - 1)


def _tile(n, cap, mult=LANES):
    if n <= cap:
        return n
    for t in range(cap - cap % mult, 0, -mult):
        if n % t == 0:
            return t
    return n


def _zero_map(nd, *_):
    return (0,) * nd


def _params(sem):
    return pltpu.CompilerParams(dimension_semantics=sem, vmem_limit_bytes=VMEM_LIMIT)


def _rowcall(name, body, n_rows, tm, row_ins, full_ins, row_outs, acc_outs=(), tiled_outs=(), tiled_ins=(),
             exchange=None):
    n_steps = n_rows // tm
    ex_srcs, broadcast = exchange if exchange else ((), False)
    n_ex = len(ex_srcs)
    n_in = len(row_ins) + len(tiled_ins) + len(full_ins)
    n_out = len(row_outs) + len(acc_outs) + len(tiled_outs)

    def kern(*refs):
        step = pl.program_id(0)
        ex_in, ex_out = refs[n_in:n_in + n_ex], refs[n_in + n_ex + n_out:n_in + 2 * n_ex + n_out]
        sems = refs[n_in + 2 * n_ex + n_out:]
        if n_ex:
            @pl.when(step == 0)
            def _():
                for cp in _chip_copies(ex_in, ex_out, *sems, broadcast):
                    cp.start()

        body(step, *refs[:n_in], *refs[n_in + n_ex:n_in + n_ex + n_out])
        if n_ex:
            @pl.when(step == n_steps - 1)
            def _():
                _wait_copies(_chip_copies(ex_in, ex_out, *sems, broadcast))

    in_specs = [pl.BlockSpec((tm, a.shape[1]), lambda i: (i, 0)) for a in row_ins]
    in_specs += [spec for (_, spec) in tiled_ins]
    row_ins = list(row_ins) + [a for (a, _) in tiled_ins]
    in_specs += [pl.BlockSpec(a.shape, functools.partial(_zero_map, a.ndim), pipeline_mode=pl.Buffered(1))
                 for a in full_ins]
    in_specs += [HBM_SPEC] * n_ex
    out_specs = [pl.BlockSpec((tm, w), lambda i: (i, 0)) for (w, _) in row_outs]
    out_specs += [pl.BlockSpec(s, functools.partial(_zero_map, len(s))) for (s, _) in acc_outs]
    out_specs += [spec for (_, spec) in tiled_outs]
    out_specs += [HBM_SPEC] * n_ex
    out_shape = [jax.ShapeDtypeStruct((n_rows, w), dt) for (w, dt) in row_outs]
    out_shape += [jax.ShapeDtypeStruct(s, dt) for (s, dt) in acc_outs]
    out_shape += [shape for (shape, _) in tiled_outs]
    out_shape += _exchange_shapes(ex_srcs)
    return pl.pallas_call(kern, grid=(n_steps,), in_specs=in_specs, out_specs=out_specs, out_shape=out_shape,
                          scratch_shapes=_dma_sems(n_ex * N_PEER_CHIPS) if n_ex else [], name=name,
                          compiler_params=_params(("arbitrary",)))(*row_ins, *full_ins, *ex_srcs)


def _acc(step, ref, val):
    @pl.when(step == 0)
    def _():
        ref[...] = val

    @pl.when(step != 0)
    def _():
        ref[...] += val


def _ffn_fwd(name, x, w_in4, w_out, ln_g, ln_b, tm, exchange=None):
    T, D = x.shape
    fh = w_in4.shape[2]

    def body(i, x_ref, w4_ref, wo_ref, g_ref, b_ref, h_ref, z_ref, a_ref):
        xv = x_ref[...]
        xb = _bf(xv)
        f = jnp.zeros((tm, D), F32)
        for k in range(2):
            gk = _dot(xb, w4_ref[k])
            uk = _dot(xb, w4_ref[2 + k])
            a_ref[:, k * fh:(k + 1) * fh] = _bf(gk)
            a_ref[:, (2 + k) * fh:(3 + k) * fh] = _bf(uk)
            f += _dot(_bf(gk * _sigmoid(gk) * uk), wo_ref[k * fh:(k + 1) * fh, :])
        z = ALPHA * xv + 0.5 * f
        xhat, _ = _ln_stats(z)
        z_ref[...] = z
        h_ref[...] = xhat * g_ref[...] + b_ref[...]

    return _rowcall(name, body, T, tm, [x], [w_in4, w_out, ln_g, ln_b],
                    [(D, F32), (D, F32), (4 * fh, BF16)], exchange=exchange)


def _ffn_bwd(name, dh, z, a, w_in4, w_out, ln_g, tm, exchange=None):
    T, D = dh.shape
    fh = w_in4.shape[2]

    def body(i, dh_ref, z_ref, a_ref, w4_ref, wo_ref, g_ref, dx_ref, da_ref, s_ref, df_ref, dg_ref, db_ref):
        xhat, rstd = _ln_stats(z_ref[...])
        dz, dg, db = _ln_bwd(dh_ref[...], xhat, rstd, g_ref[...])
        _acc(i, dg_ref, dg)
        _acc(i, db_ref, db)
        dfb = _bf(0.5 * dz)
        df_ref[...] = dfb
        dx = ALPHA * dz
        for k in range(2):
            gk = a_ref[:, k * fh:(k + 1) * fh].astype(F32)
            uk = a_ref[:, (2 + k) * fh:(3 + k) * fh].astype(F32)
            ds = _dot_nt(dfb, wo_ref[k * fh:(k + 1) * fh, :])
            sig = _sigmoid(gk)
            silu = gk * sig
            dgk = _bf(ds * uk * sig * (1.0 + gk * (1.0 - sig)))
            duk = _bf(ds * silu)
            s_ref[:, k * fh:(k + 1) * fh] = _bf(silu * uk)
            da_ref[:, k * fh:(k + 1) * fh] = dgk
            da_ref[:, (2 + k) * fh:(3 + k) * fh] = duk
            dx += _dot_nt(dgk, w4_ref[k]) + _dot_nt(duk, w4_ref[2 + k])
        dx_ref[...] = dx

    return _rowcall(name, body, T, tm, [dh, z, a], [w_in4, w_out, ln_g],
                    [(D, F32), (4 * fh, BF16), (2 * fh, BF16), (D, BF16)],
                    [((1, D), F32), ((1, D), F32)], exchange=exchange)


WG_TILE_N = 1536


def _mm_tn(name, a, b, out_dtype=BF16, n_split=1):
    T, M = a.shape
    N = b.shape[1]
    tk = _tile(T, 2048, 8)
    tm = _tile(M, 1408)
    tn = _tile(N // n_split, WG_TILE_N)
    per = N // n_split // tn
    nk = T // tk
    if n_split > 1:
        out_spec = pl.BlockSpec((None, tm, tn), lambda i, j, k: (j // per, i, j % per))
        out_shape = jax.ShapeDtypeStruct((n_split, M, N // n_split), out_dtype)
    else:
        out_spec = pl.BlockSpec((tm, tn), lambda i, j, k: (i, j))
        out_shape = jax.ShapeDtypeStruct((M, N), out_dtype)

    def kern(a_ref, b_ref, o_ref, acc_ref):
        k = pl.program_id(2)
        part = _dot_tn(_bf(a_ref[...]), _bf(b_ref[...]))

        @pl.when(k == 0)
        def _():
            acc_ref[...] = part

        @pl.when(k != 0)
        def _():
            acc_ref[...] += part

        @pl.when(k == nk - 1)
        def _():
            o_ref[...] = acc_ref[...].astype(out_dtype)

    return pl.pallas_call(
        kern, grid=(M // tm, N // tn, nk),
        in_specs=[pl.BlockSpec((tk, tm), lambda i, j, k: (k, i)), pl.BlockSpec((tk, tn), lambda i, j, k: (k, j))],
        out_specs=out_spec, out_shape=out_shape,
        scratch_shapes=[pltpu.VMEM((tm, tn), F32)], name=name,
        compiler_params=_params(("arbitrary", "arbitrary", "arbitrary")))(a, b)


def _proj_ret(h1, w_r, cos_r, sin_r, tm):
    T, D = h1.shape
    qk = RET_HEADS * RET_DK
    rv = RET_HEADS * RET_DV

    def body(i, h_ref, cos_ref, sin_ref, w_ref, q_ref, k_ref, v_ref, g_ref):
        hb = _bf(h_ref[...])
        cos, sin = cos_ref[...], sin_ref[...]
        for out_ref, off, scale in ((q_ref, 0, 1.0), (k_ref, qk, RET_DK ** -0.5)):
            pr = _dot(hb, w_ref[:, off:off + qk])
            for h in range(RET_HEADS):
                t = pr[:, h * RET_DK:(h + 1) * RET_DK]
                out_ref[:, h * RET_DK:(h + 1) * RET_DK] = _bf((t * cos + _roll(t, RET_DK // 2) * sin) * scale)
        v_ref[...] = _bf(_dot(hb, w_ref[:, 2 * qk:2 * qk + rv]))
        g_ref[...] = _dot(hb, w_ref[:, 2 * qk + rv:2 * qk + 2 * rv])

    return _rowcall("proj_ret", body, T, tm, [h1, cos_r, sin_r], [w_r],
                    [(qk, BF16), (qk, BF16), (rv, BF16), (rv, F32)])


def _rope_pe(t, c, s1, s2):
    return t * c + _roll(t, LANES - MLA_ROPE // 2) * s1 + _roll(t, MLA_ROPE // 2) * s2


def _rope_pe_bwd(dy, c, s1, s2):
    return dy * c + _roll(dy * s1, MLA_ROPE // 2) + _roll(dy * s2, LANES - MLA_ROPE // 2)


def _rms(x, g):
    r = lax.rsqrt(_mean(x * x) + EPS)
    return x * r, r


def _attn_block(T):
    return min(512, T)


def _transposed_blocks(T, tm, w, dtype):
    tb = _attn_block(T)
    per = tb // tm
    return (jax.ShapeDtypeStruct((T // tb, MLA_HEADS, w, tb), dtype),
            pl.BlockSpec((None, MLA_HEADS, w, tm), lambda i: (i // per, 0, 0, i % per)))


ATTN_SCALE = (MLA_NOPE + MLA_ROPE) ** -0.5
LOG2E = 1.4426950408889634
Q_PRESCALE = ATTN_SCALE * LOG2E
V_ONES = 16


def _proj_mla(h1, tabs, w_c, w_kpe, w_g, w_uq, w_uk, w_uv, qn_g, kvn_g, tm):
    T, D = h1.shape
    H = MLA_HEADS

    def body(i, h_ref, c_ref, s1_ref, s2_ref, wc_ref, wk_ref, wg_ref, wuq_ref, wuk_ref, wuv_ref, qg_ref, kg_ref,
             lat_ref, gt_ref, q_ref, k_ref, v_ref, ln_ref, qt_ref, kt_ref, vt_ref):
        hb = _bf(h_ref[...])
        c, s1, s2 = c_ref[...], s1_ref[...], s2_ref[...]
        lat = _dot(hb, wc_ref[...])
        lat_ref[...] = lat
        gt_ref[...] = _dot(hb, wg_ref[...])
        cqn, _ = _rms(lat[:, :Q_LORA], None)
        ckn, _ = _rms(lat[:, Q_LORA:], None)
        cqn = _bf(cqn * qg_ref[...])
        ckn = _bf(ckn * kg_ref[...])
        ln_ref[:, :Q_LORA] = cqn
        ln_ref[:, Q_LORA:] = ckn
        q = _dot(cqn, wuq_ref[...])
        kn = _dot(ckn, wuk_ref[...])
        vv = _dot(ckn, wuv_ref[...])
        v_ref[...] = _bf(vv)
        kpe = _rope_pe(_dot(hb, wk_ref[...]), c, s1, s2)
        ones = jnp.ones((V_ONES, tm), BF16)
        for h in range(H):
            o = h * MLA_QK
            qh = jnp.concatenate([q[:, o:o + MLA_NOPE], _rope_pe(q[:, o + MLA_NOPE:o + MLA_QK], c, s1, s2)], axis=1)
            qh = qh * Q_PRESCALE
            kh = jnp.concatenate([kn[:, h * MLA_NOPE:(h + 1) * MLA_NOPE], kpe], axis=1)
            q_ref[:, o:o + MLA_QK] = _bf(qh)
            k_ref[:, o:o + MLA_QK] = _bf(kh)
            qt_ref[h] = _bf(qh.T)
            kt_ref[h] = _bf(kh.T)
            vt_ref[h] = jnp.concatenate([_bf(vv[:, h * MLA_DV:(h + 1) * MLA_DV].T), ones], axis=0)

    lat_w = Q_LORA + KV_LORA
    return _rowcall("proj_mla", body, T, tm, [h1, *tabs], [w_c, w_kpe, w_g, w_uq, w_uk, w_uv, qn_g, kvn_g],
                    [(lat_w, F32), (2 * D, F32), (H * MLA_QK, BF16), (H * MLA_QK, BF16), (H * MLA_DV, BF16),
                     (lat_w, BF16)],
                    tiled_outs=[_transposed_blocks(T, tm, MLA_QK, BF16), _transposed_blocks(T, tm, MLA_QK, BF16),
                                _transposed_blocks(T, tm, MLA_DV + V_ONES, BF16)])


def _ret_block(T):
    return min(256, T)


RET_HEADS_PER_STEP = 8


def _ret_dmat(lg, bt):
    n = lax.broadcasted_iota(jnp.int32, (bt, bt), 0)
    m = lax.broadcasted_iota(jnp.int32, (bt, bt), 1)
    return jnp.where(_chunk_of(m) <= _chunk_of(n), jnp.exp(lg * jnp.abs(n - m).astype(F32)), 0.0)


def _ret_scan(name, per_head, lgam, ins, outs, rev):
    T = ins[0][0].shape[0]
    bt = _ret_block(T)
    nb = T // bt
    hps = min(RET_HEADS_PER_STEP, RET_HEADS)
    n_in, n_out = len(ins), len(outs)

    def kern(lg_ref, *refs):
        in_refs, out_refs = refs[:n_in], refs[n_in:n_in + n_out]
        state_ref, dmat_ref = refs[n_in + n_out:]

        @pl.when(pl.program_id(1) == 0)
        def _():
            state_ref[...] = jnp.zeros_like(state_ref)
            for hh in range(hps):
                dmat_ref[hh] = _ret_dmat(lg_ref[hh][:, :1], bt)

        pos = lax.broadcasted_iota(jnp.int32, (bt, 1), 0).astype(F32)
        for hh in range(hps):
            lg = lg_ref[hh][:, :1]
            xi, zeta, gb = jnp.exp(lg * (pos + 1.0)), jnp.exp(lg * (bt - 1.0 - pos)), jnp.exp(lg * bt)
            tiles = [r[:, hh * w:(hh + 1) * w] for r, (_, w) in zip(in_refs, ins)]
            res = per_head(dmat_ref[hh], xi, zeta, gb, state_ref.at[hh], *tiles)
            for o_ref, (w, _), val in zip(out_refs, outs, res):
                o_ref[:, hh * w:(hh + 1) * w] = val.astype(o_ref.dtype)

    def blk(w):
        if rev:
            return pl.BlockSpec((bt, hps * w), lambda g, b: (nb - 1 - b, g))
        return pl.BlockSpec((bt, hps * w), lambda g, b: (b, g))

    return pl.pallas_call(
        kern, grid=(RET_HEADS // hps, nb),
        in_specs=[pl.BlockSpec((hps, 1, LANES), lambda g, b: (g, 0, 0))] + [blk(w) for _, w in ins],
        out_specs=[blk(w) for w, _ in outs],
        out_shape=[jax.ShapeDtypeStruct((T, RET_HEADS * w), dt) for w, dt in outs],
        scratch_shapes=[pltpu.VMEM((hps, RET_DK, RET_DV), F32), pltpu.VMEM((hps, bt, bt), F32)], name=name,
        compiler_params=_params(("arbitrary", "arbitrary")))(lgam, *[a for a, _ in ins])


def _ret_fwd(rq, rk, rv, lgam):
    def per_head(dmat, xi, zeta, gb, s_ref, q, k, v):
        sc = _dot_nt(q, k) * dmat
        y = _dot(_bf(sc), v) + _dot(q, _bf(s_ref[...])) * xi
        s_ref[...] = s_ref[...] * gb + _dot_tn(_bf(k.astype(F32) * zeta), v)
        return (y,)

    return _ret_scan("ret_fwd", per_head, lgam, [(rq, RET_DK), (rk, RET_DK), (rv, RET_DV)], [(RET_DV, F32)], False)[0]


def _ret_bwd_q(rq, rk, rv, dy, lgam):
    def per_head(dmat, xi, zeta, gb, s_ref, k, v, dy):
        dp = _dot_nt(dy, v) * dmat
        dq = _dot(_bf(dp), k) + _dot_nt(dy, _bf(s_ref[...])) * xi
        s_ref[...] = s_ref[...] * gb + _dot_tn(_bf(k.astype(F32) * zeta), v)
        return (dq,)

    return _ret_scan("ret_bwd_q", per_head, lgam, [(rk, RET_DK), (rv, RET_DV), (dy, RET_DV)], [(RET_DK, F32)], False)[0]


def _ret_bwd_kv(rq, rk, rv, dy, lgam):
    def per_head(dmat, xi, zeta, gb, g_ref, q, k, v, dy):
        gs = _bf(g_ref[...])
        p = _dot_nt(q, k) * dmat
        dp = _dot_nt(dy, v) * dmat
        dv = _dot_tn(_bf(p), dy) + _dot(k, gs) * zeta
        dk = _dot_tn(_bf(dp), q) + _dot_nt(v, gs) * zeta
        g_ref[...] = g_ref[...] * gb + _dot_tn(_bf(q.astype(F32) * xi), dy)
        return dk, dv

    return _ret_scan("ret_bwd_kv", per_head, lgam, [(rq, RET_DK), (rk, RET_DK), (rv, RET_DV), (dy, RET_DV)],
                     [(RET_DK, F32), (RET_DV, BF16)], True)


def _attn_mask_t(tb):
    key = lax.broadcasted_iota(jnp.int32, (tb, tb), 0)
    qry = lax.broadcasted_iota(jnp.int32, (tb, tb), 1)
    return _chunk_of(key) <= _chunk_of(qry)


MASKED = -1e30
SUBLANES = 8


def _head_blocks(nb, w, tb):
    return pl.BlockSpec((nb, None, w, tb), lambda h, i: (0, h, 0, 0))


def _one_block(w, tb):
    return pl.BlockSpec((None, None, w, tb), lambda h, i: (i, h, 0, 0))


def _attn_fwd(k, qt, vt, exchange=()):
    T = k.shape[0]
    tb = _attn_block(T)
    nb = T // tb

    n_ex = len(exchange)

    def kern(qt_ref, k_ref, vt_ref, *refs):
        ex_in, (o_ref, lser_ref), ex_out = refs[:n_ex], refs[n_ex:n_ex + 2], refs[n_ex + 2:2 * n_ex + 2]
        m_ref, acc_ref, sa_ref, sb_ref = refs[2 * n_ex + 2:2 * n_ex + 6]
        sems = refs[2 * n_ex + 6:]
        qb = pl.program_id(1)
        first = jnp.logical_and(pl.program_id(0) == 0, qb == 0)
        last = jnp.logical_and(pl.program_id(0) == MLA_HEADS - 1, qb == nb - 1)
        if n_ex:
            @pl.when(first)
            def _():
                for cp in _chip_copies(ex_in, ex_out, *sems, True):
                    cp.start()

        qt = qt_ref[...]
        m_ref[...] = jnp.full_like(m_ref, MASKED)
        acc_ref[...] = jnp.zeros_like(acc_ref)

        def scores(kb):
            rows = pl.ds(pl.multiple_of(kb * tb, tb), tb)
            return _dot(k_ref[rows, :], qt)

        def update(s, kb):
            m_old = m_ref[...]
            m_new = jnp.maximum(m_old, jnp.max(s, axis=0, keepdims=True))
            p = jnp.exp2(s - m_new)
            acc_ref[...] = acc_ref[...] * jnp.exp2(m_old - m_new) + _dot(vt_ref[kb], _bf(p))
            m_ref[...] = m_new

        def masked(s):
            return jnp.where(_attn_mask_t(tb), s, MASKED)

        sa_ref[...] = scores(0)

        def pair_body(j, carry):
            sb_ref[...] = scores(2 * j + 1)
            update(sa_ref[...], 2 * j)
            sa_ref[...] = scores(2 * j + 2)
            update(sb_ref[...], 2 * j + 1)
            return carry

        lax.fori_loop(0, qb // 2, pair_body, 0)

        @pl.when(qb % 2 == 0)
        def _():
            update(masked(sa_ref[...]), qb)

        @pl.when(qb % 2 == 1)
        def _():
            sb_ref[...] = masked(scores(qb))
            update(sa_ref[...], qb - 1)
            update(sb_ref[...], qb)

        l = acc_ref[MLA_DV:MLA_DV + 1, :]
        o_ref[...] = (acc_ref[:MLA_DV, :] / l).T
        lser_ref[...] = jnp.broadcast_to(m_ref[...] + jnp.log2(l), (SUBLANES, tb))
        if n_ex:
            @pl.when(last)
            def _():
                _wait_copies(_chip_copies(ex_in, ex_out, *sems, True))

    return pl.pallas_call(
        kern, grid=(MLA_HEADS, nb),
        in_specs=[_one_block(MLA_QK, tb), pl.BlockSpec((T, MLA_QK), lambda h, i: (0, h)),
                  _head_blocks(nb, MLA_DV + V_ONES, tb)] + [HBM_SPEC] * n_ex,
        out_specs=[pl.BlockSpec((tb, MLA_DV), lambda h, i: (i, h)), _one_block(SUBLANES, tb)] + [HBM_SPEC] * n_ex,
        out_shape=[jax.ShapeDtypeStruct((T, MLA_HEADS * MLA_DV), F32),
                   jax.ShapeDtypeStruct((nb, MLA_HEADS, SUBLANES, tb), F32)] + _exchange_shapes(exchange),
        scratch_shapes=[pltpu.VMEM((1, tb), F32), pltpu.VMEM((MLA_DV + V_ONES, tb), F32),
                        pltpu.VMEM((tb, tb), F32), pltpu.VMEM((tb, tb), F32)]
        + (_dma_sems(n_ex * N_PEER_CHIPS) if n_ex else []),
        name="attn_fwd", compiler_params=_params(("arbitrary", "arbitrary")))(qt, k, vt, *exchange)


def _attn_bwd(q, k, v, do, qt, kt, dot_, lse_rows, delta_rows):
    T = q.shape[0]
    tb = _attn_block(T)
    nb = T // tb

    def kern(q_ref, k_ref, v_ref, do_ref, qt_ref, kt_ref, dot_ref, lse_ref, dl_ref, dk_ref, dv_ref, dqt_ref, dv_acc,
             sa_ref, pa_ref, sb_ref, pb_ref):
        kb = pl.program_id(1)
        kv, vv, ktv = k_ref[...], v_ref[...], kt_ref[...]
        dk_ref[...] = jnp.zeros_like(dk_ref)
        dv_acc[...] = jnp.zeros_like(dv_acc)

        @pl.when(kb == 0)
        def _():
            dqt_ref[...] = jnp.zeros_like(dqt_ref)

        def products(qb, s_ref, dp_ref, diagonal=False):
            s = _dot(kv, qt_ref[qb])
            s_ref[...] = jnp.where(_attn_mask_t(tb), s, MASKED) if diagonal else s
            dp_ref[...] = _dot(vv, dot_ref[qb])

        def consume(qb, s_ref, dp_ref):
            rows = pl.ds(pl.multiple_of(qb * tb, tb), tb)
            p = jnp.exp2(s_ref[...] - lse_ref[qb][:1, :])
            dv_acc[...] += _dot(_bf(p), do_ref[rows, :])
            ds = _bf(p * (dp_ref[...] - dl_ref[qb][:1, :]))
            dk_ref[...] += _dot(ds, q_ref[rows, :])
            dqt_ref[qb] += _dot(ktv, ds)

        n_full = nb - 1 - kb
        products(kb, sa_ref, pa_ref, diagonal=True)

        def pair_body(j, carry):
            q1 = kb + 1 + 2 * j
            products(q1, sb_ref, pb_ref)
            consume(q1 - 1, sa_ref, pa_ref)
            products(q1 + 1, sa_ref, pa_ref)
            consume(q1, sb_ref, pb_ref)
            return carry

        lax.fori_loop(0, n_full // 2, pair_body, 0)

        @pl.when(n_full % 2 == 0)
        def _():
            consume(nb - 1, sa_ref, pa_ref)

        @pl.when(n_full % 2 == 1)
        def _():
            products(nb - 1, sb_ref, pb_ref)
            consume(nb - 2, sa_ref, pa_ref)
            consume(nb - 1, sb_ref, pb_ref)

        dk_ref[...] = dk_ref[...] * (ATTN_SCALE / Q_PRESCALE)
        dv_ref[...] = _bf(dv_acc[...])

    def blk(w):
        return pl.BlockSpec((tb, w), lambda h, i: (i, h))

    def full(w):
        return pl.BlockSpec((T, w), lambda h, i: (0, h))

    return pl.pallas_call(
        kern, grid=(MLA_HEADS, nb),
        in_specs=[full(MLA_QK), blk(MLA_QK), blk(MLA_DV), full(MLA_DV), _head_blocks(nb, MLA_QK, tb),
                  _one_block(MLA_QK, tb), _head_blocks(nb, MLA_DV, tb), _head_blocks(nb, SUBLANES, tb),
                  _head_blocks(nb, SUBLANES, tb)],
        out_specs=[blk(MLA_QK), blk(MLA_DV), _head_blocks(nb, MLA_QK, tb)],
        out_shape=[jax.ShapeDtypeStruct((T, MLA_HEADS * MLA_QK), F32),
                   jax.ShapeDtypeStruct((T, MLA_HEADS * MLA_DV), BF16),
                   jax.ShapeDtypeStruct((nb, MLA_HEADS, MLA_QK, tb), F32)],
        scratch_shapes=[pltpu.VMEM((tb, MLA_DV), F32)] + [pltpu.VMEM((tb, tb), F32)] * 4,
        name="attn_bwd", compiler_params=_params(("arbitrary", "arbitrary")))(
            q, k, v, do, qt, kt, dot_, lse_rows, delta_rows)


def _group_norm(y):
    yc = y - _mean(y)
    rstd = lax.rsqrt(_mean(yc * yc) + EPS)
    return yc * rstd, rstd


def _mix_fwd(y, rg, o, gates, h1, gn_g, w_ret_o, w_mla_o, w_out, ln_g, ln_b, tm):
    T, D = h1.shape

    def body(i, y_ref, rg_ref, o_ref, gt_ref, h_ref, gn_ref, wr_ref, wm_ref, wo_ref, g_ref, b_ref,
             h2_ref, z_ref, yret_ref, ymla_ref, yr_ref, mix_ref):
        for h in range(RET_HEADS):
            sl = slice(h * RET_DV, (h + 1) * RET_DV)
            yn, _ = _group_norm(y_ref[:, sl])
            r = rg_ref[:, sl]
            yr_ref[:, sl] = _bf(r * _sigmoid(r) * (yn * gn_ref[:, sl]))
        yret = _dot(yr_ref[...], wr_ref[...])
        ymla = _dot(_bf(o_ref[...]), wm_ref[...])
        yret_ref[...] = yret
        ymla_ref[...] = ymla
        mix = _bf(_sigmoid(gt_ref[:, :D]) * yret + _sigmoid(gt_ref[:, D:]) * ymla)
        mix_ref[...] = mix
        z = ALPHA * h_ref[...] + _dot(mix, wo_ref[...])
        xhat, _ = _ln_stats(z)
        z_ref[...] = z
        h2_ref[...] = xhat * g_ref[...] + b_ref[...]

    return _rowcall("mix_fwd", body, T, tm, [y, rg, o, gates, h1], [gn_g, w_ret_o, w_mla_o, w_out, ln_g, ln_b],
                    [(D, F32), (D, F32), (D, F32), (D, F32), (RET_HEADS * RET_DV, BF16), (D, BF16)])


def _mix_bwd(dh2, z1, gates, yret, ymla, y, rg, o, gn_g, w_ret_o, w_mla_o, w_out, ln_g, tm, exchange=None):
    T, D = dh2.shape
    rv = RET_HEADS * RET_DV

    def body(i, dh_ref, z_ref, gt_ref, yret_ref, ymla_ref, y_ref, rg_ref, o_ref, gn_ref, wr_ref, wm_ref, wo_ref, g_ref,
             dz_ref, dgt_ref, drg_ref, dy_ref, do_ref, dyret_ref, dymla_ref, dg_ref, db_ref, dgn_ref, dot_ref,
             dl_ref):
        xhat, rstd = _ln_stats(z_ref[...])
        dz, dg, db = _ln_bwd(dh_ref[...], xhat, rstd, g_ref[...])
        _acc(i, dg_ref, dg)
        _acc(i, db_ref, db)
        dz_ref[...] = dz
        dmix = _dot_nt(_bf(dz), wo_ref[...])
        sr = _sigmoid(gt_ref[:, :D])
        sm = _sigmoid(gt_ref[:, D:])
        dgt_ref[:, :D] = _bf(dmix * yret_ref[...] * sr * (1.0 - sr))
        dgt_ref[:, D:] = _bf(dmix * ymla_ref[...] * sm * (1.0 - sm))
        dyret = _bf(dmix * sr)
        dymla = _bf(dmix * sm)
        dyret_ref[...] = dyret
        dymla_ref[...] = dymla
        dov = _dot_nt(dymla, wm_ref[...])
        do_ref[...] = _bf(dov)
        for h in range(MLA_HEADS):
            sl = slice(h * MLA_DV, (h + 1) * MLA_DV)
            dot_ref[h] = _bf(dov[:, sl].T)
            delta = jnp.sum(dov[:, sl] * o_ref[:, sl], axis=-1, keepdims=True)
            dl_ref[h] = jnp.broadcast_to(delta, (tm, LANES)).T[:SUBLANES, :]
        dyr = _dot_nt(dyret, wr_ref[...])
        dgn = []
        for h in range(RET_HEADS):
            sl = slice(h * RET_DV, (h + 1) * RET_DV)
            yn, grstd = _group_norm(y_ref[:, sl])
            r = rg_ref[:, sl]
            sig = _sigmoid(r)
            d = dyr[:, sl]
            drg_ref[:, sl] = _bf(d * (yn * gn_ref[:, sl]) * sig * (1.0 + r * (1.0 - sig)))
            dt = d * (r * sig)
            dgn.append(jnp.sum(dt * yn, axis=0, keepdims=True))
            dyn = dt * gn_ref[:, sl]
            dy_ref[:, sl] = _bf(grstd * (dyn - _mean(dyn) - yn * _mean(dyn * yn)))
        _acc(i, dgn_ref, jnp.concatenate(dgn, axis=1))

    return _rowcall("mix_bwd", body, T, tm, [dh2, z1, gates, yret, ymla, y, rg, o],
                    [gn_g, w_ret_o, w_mla_o, w_out, ln_g],
                    [(D, F32), (2 * D, BF16), (rv, BF16), (rv, BF16), (MLA_HEADS * MLA_DV, BF16), (D, BF16), (D, BF16)],
                    [((1, D), F32), ((1, D), F32), ((1, rv), F32)],
                    tiled_outs=[_transposed_blocks(T, tm, MLA_DV, BF16), _transposed_blocks(T, tm, SUBLANES, F32)],
                    exchange=exchange)


def _proj_mla_bwd(dqt, dk, dv, lat, tabs, w_uq, w_uk, w_uv, qn_g, kvn_g, tm):
    T = dk.shape[0]
    H = MLA_HEADS
    lat_w = Q_LORA + KV_LORA

    def body(i, dk_ref, dv_ref, lat_ref, c_ref, s1_ref, s2_ref, dqt_ref, wuq_ref, wuk_ref, wuv_ref, qg_ref, kg_ref,
             dlat_ref, dkpe_ref, dqb_ref, dkn_ref, dqg_ref, dkg_ref):
        c, s1, s2 = c_ref[...], s1_ref[...], s2_ref[...]
        dkpe = jnp.zeros((tm, LANES), F32)
        for h in range(H):
            o = h * MLA_QK
            dqh = dqt_ref[h].T * ATTN_SCALE
            dqb_ref[:, o:o + MLA_NOPE] = _bf(dqh[:, :MLA_NOPE])
            dqb_ref[:, o + MLA_NOPE:o + MLA_QK] = _bf(_rope_pe_bwd(dqh[:, MLA_NOPE:], c, s1, s2))
            dkn_ref[:, h * MLA_NOPE:(h + 1) * MLA_NOPE] = _bf(dk_ref[:, o:o + MLA_NOPE])
            dkpe += dk_ref[:, o + MLA_NOPE:o + MLA_QK]
        dkn_ref[:, H * MLA_NOPE:] = dv_ref[...]
        dkpe_ref[...] = _bf(_rope_pe_bwd(dkpe, c, s1, s2))
        dcqn = _dot_nt(dqb_ref[...], wuq_ref[...])
        dckn = _dot_nt(dkn_ref[:, :H * MLA_NOPE], wuk_ref[...]) + _dot_nt(dv_ref[...], wuv_ref[...])
        for dn, x, g_ref, dg_ref, sl in ((dcqn, lat_ref[:, :Q_LORA], qg_ref, dqg_ref, slice(0, Q_LORA)),
                                         (dckn, lat_ref[:, Q_LORA:], kg_ref, dkg_ref, slice(Q_LORA, lat_w))):
            xn, r = _rms(x, None)
            _acc(i, dg_ref, jnp.sum(dn * xn, axis=0, keepdims=True))
            dxn = dn * g_ref[...]
            dlat_ref[:, sl] = _bf(r * (dxn - xn * _mean(dxn * xn)))

    dqt_shape, dqt_spec = _transposed_blocks(T, tm, MLA_QK, F32)
    assert dqt.shape == dqt_shape.shape
    return _rowcall("proj_mla_bwd", body, T, tm, [dk, dv, lat, *tabs], [w_uq, w_uk, w_uv, qn_g, kvn_g],
                    [(lat_w, BF16), (LANES, BF16), (H * MLA_QK, BF16), (H * (MLA_NOPE + MLA_DV), BF16)],
                    [((1, Q_LORA), F32), ((1, KV_LORA), F32)], tiled_ins=[(dqt, dqt_spec)])


def _proj_bwd(drq, drk, drv, drg, dz1, dlat, dkpe, dgates, cos_r, sin_r, w_r, w_c, w_kpe, w_g, tm):
    T, D = dz1.shape
    qk = RET_HEADS * RET_DK
    rv = RET_HEADS * RET_DV
    o_lat = 2 * qk + 2 * rv
    o_kpe = o_lat + dlat.shape[1]
    o_gate = o_kpe + LANES
    o_end = o_gate + dgates.shape[1]
    width = -(-o_end // WG_TILE_N) * WG_TILE_N

    def body(i, drq_ref, drk_ref, drv_ref, drg_ref, dz_ref, dlat_ref, dkpe_ref, dgt_ref, cos_ref, sin_ref,
             wr_ref, wc_ref, wk_ref, wg_ref, dh_ref, dpr_ref):
        cos, sin = cos_ref[...], sin_ref[...]
        for src, off, scale in ((drq_ref, 0, 1.0), (drk_ref, qk, RET_DK ** -0.5)):
            for h in range(RET_HEADS):
                d = src[:, h * RET_DK:(h + 1) * RET_DK]
                dpr_ref[:, off + h * RET_DK:off + (h + 1) * RET_DK] = _bf(
                    (d * cos + _roll(d * sin, RET_DK // 2)) * scale)
        dpr_ref[:, 2 * qk:2 * qk + rv] = drv_ref[...]
        dpr_ref[:, 2 * qk + rv:o_lat] = drg_ref[...]
        dpr_ref[:, o_lat:o_kpe] = dlat_ref[...]
        dpr_ref[:, o_kpe:o_gate] = dkpe_ref[...]
        dpr_ref[:, o_gate:o_end] = dgt_ref[...]
        if width > o_end:
            dpr_ref[:, o_end:] = jnp.zeros((tm, width - o_end), BF16)
        dh_ref[...] = (ALPHA * dz_ref[...] + _dot_nt(dpr_ref[:, :o_lat], wr_ref[...])
                       + _dot_nt(dlat_ref[...], wc_ref[...]) + _dot_nt(dkpe_ref[...], wk_ref[...])
                       + _dot_nt(dgt_ref[...], wg_ref[...]))

    return _rowcall("proj_bwd", body, T, tm, [drq, drk, drv, drg, dz1, dlat, dkpe, dgates, cos_r, sin_r],
                    [w_r, w_c, w_kpe, w_g], [(D, F32), (width, BF16)])


def _ple_loss(h3, p, target, w_gate, w_proj, ln_g, ln_b, tm):
    T, D = h3.shape

    def body(i, h_ref, p_ref, t_ref, wg_ref, wp_ref, g_ref, b_ref, dh_ref, dgp_ref, dpp_ref, loss_ref, dg_ref, db_ref):
        hv = h_ref[...]
        sg = _sigmoid(_dot(_bf(hv), wg_ref[...]))
        pp = _dot(_bf(p_ref[...]), wp_ref[...])
        xhat, rstd = _ln_stats(ALPHA * hv + sg * pp)
        err = xhat * g_ref[...] + b_ref[...] - t_ref[...]
        row_loss = 0.5 * _mean(err * err)
        _acc(i, loss_ref, jnp.broadcast_to(jnp.sum(row_loss, axis=0, keepdims=True), (1, LANES)))
        dz, dg, db = _ln_bwd(err * (1.0 / D), xhat, rstd, g_ref[...])
        _acc(i, dg_ref, dg)
        _acc(i, db_ref, db)
        dgp = _bf(dz * pp * sg * (1.0 - sg))
        dgp_ref[...] = dgp
        dpp_ref[...] = _bf(dz * sg)
        dh_ref[...] = ALPHA * dz + _dot_nt(dgp, wg_ref[...])

    return _rowcall("ple_loss", body, T, tm, [h3, p, target], [w_gate, w_proj, ln_g, ln_b],
                    [(D, F32), (D, BF16), (D, BF16)], [((1, LANES), F32), ((1, D), F32), ((1, D), F32)])


def _ewise(name, fn, ins, n_out, out_dtype=F32):
    r, c = ins[0].shape
    tr = _tile(r, max(8, (1 << 19) // c // 8 * 8), 8)

    def kern(*refs):
        outs = fn(*[x[...] for x in refs[:len(ins)]])
        for o_ref, o in zip(refs[len(ins):], outs):
            o_ref[...] = o.astype(out_dtype)

    spec = pl.BlockSpec((tr, c), lambda i: (i, 0))
    return pl.pallas_call(kern, grid=(r // tr,), in_specs=[spec] * len(ins), out_specs=[spec] * n_out,
                          out_shape=[jax.ShapeDtypeStruct((r, c), out_dtype)] * n_out, name=name,
                          compiler_params=_params(("arbitrary",)))(*ins)


def _adamw_math(w, g, m, v):
    m = ADAM_B1 * m + (1.0 - ADAM_B1) * g
    v = ADAM_B2 * v + (1.0 - ADAM_B2) * (g * g)
    m_hat = m / (1.0 - ADAM_B1 ** ADAM_STEP)
    v_hat = v / (1.0 - ADAM_B2 ** ADAM_STEP)
    return -ADAM_LR * (m_hat / (jnp.sqrt(v_hat) + ADAM_EPS) + ADAM_WD * w), m, v


def _adamw(name, w, g, m, v):
    shape = w.shape
    c = shape[-1]
    flat = [t.reshape(-1, c) for t in (w, g, m, v)]
    return [t.reshape(shape) for t in _ewise(name, _adamw_math, flat, 3)]


def _place():
    return lax.axis_index("x"), lax.axis_index("y"), lax.axis_index("c")


def _dma_sems(n):
    return [pltpu.SemaphoreType.DMA((n,)), pltpu.SemaphoreType.DMA((n,))]


N_PEER_CHIPS = N_CHIPS - 1


def _chips_exchange(name, srcs, broadcast):
    n = len(srcs)

    def kern(*refs):
        cps = _chip_copies(refs[:n], refs[n:2 * n], refs[2 * n], refs[2 * n + 1], broadcast)
        for cp in cps:
            cp.start()
        _wait_copies(cps)

    return pl.pallas_call(
        kern, out_shape=_exchange_shapes(srcs), in_specs=[HBM_SPEC] * n, out_specs=[HBM_SPEC] * n,
        scratch_shapes=_dma_sems(n * N_PEER_CHIPS), name=name)(*srcs)


def _exchange_shapes(srcs):
    return [jax.ShapeDtypeStruct((N_PEER_CHIPS,) + s.shape[1:], s.dtype) for s in srcs]


def _chip_copies(src_refs, out_refs, send_sems, recv_sems, broadcast):
    x, y, c = _place()
    peers = [(1 - x, y), (x, 1 - y), (1 - x, 1 - y)]
    cps = []
    for j, (px, py) in enumerate(peers):
        for a, (src_ref, out_ref) in enumerate(zip(src_refs, out_refs)):
            piece = src_ref.at[c] if broadcast else src_ref.at[2 * px + py]
            cps.append(pltpu.make_async_remote_copy(
                src_ref=piece, dst_ref=out_ref.at[j], send_sem=send_sems.at[a * N_PEER_CHIPS + j],
                recv_sem=recv_sems.at[a * N_PEER_CHIPS + j], device_id=(px, py, c), device_id_type=MESH))
    return cps


def _wait_copies(cps):
    for cp in cps:
        cp.wait_recv()
    for cp in cps:
        cp.wait_send()


def _sibling_swap(name, srcs, halves):
    n = len(srcs)

    def kern(*refs):
        src_refs, out_refs = refs[:n], refs[n:2 * n]
        send_sems, recv_sems = refs[2 * n:]
        x, y, c = _place()

        def copy(a):
            piece = src_refs[a].at[:, 1 - c] if halves else src_refs[a]
            return pltpu.make_async_remote_copy(
                src_ref=piece, dst_ref=out_refs[a], send_sem=send_sems.at[a], recv_sem=recv_sems.at[a],
                device_id=(x, y, 1 - c), device_id_type=MESH)

        cps = [copy(a) for a in range(n)]
        for cp in cps:
            cp.start()
        for cp in cps:
            cp.wait_recv()
        for cp in cps:
            cp.wait_send()

    def out_shape(s):
        return jax.ShapeDtypeStruct((s.shape[0],) + s.shape[2:] if halves else s.shape, s.dtype)

    return pl.pallas_call(
        kern, out_shape=[out_shape(s) for s in srcs], in_specs=[HBM_SPEC] * n, out_specs=[HBM_SPEC] * n,
        scratch_shapes=_dma_sems(n), name=name)(*srcs)


def _all_devices(name, src, reduce):
    r, c = src.shape
    n_dev = 2 * N_CHIPS

    def kern(src_ref, out_ref, *scratch):
        if reduce:
            gat_ref, send_sems, recv_sems = scratch
        else:
            gat_ref = out_ref
            send_sems, recv_sems = scratch
        x, y, cc = _place()
        me = 4 * x + 2 * y + cc
        gat_ref[me] = src_ref[...]
        peers = []
        for j in range(1, n_dev):
            px = 1 - x if j & 4 else x
            py = 1 - y if j & 2 else y
            pc = 1 - cc if j & 1 else cc
            peers.append((px, py, pc))

        def copy(j, peer, slot):
            return pltpu.make_async_remote_copy(
                src_ref=src_ref, dst_ref=gat_ref.at[slot], send_sem=send_sems.at[j], recv_sem=recv_sems.at[j],
                device_id=peer, device_id_type=MESH)

        sends = [copy(j, peer, me) for j, peer in enumerate(peers)]
        for cp in sends:
            cp.start()
        for j, (px, py, pc) in enumerate(peers):
            copy(j, (px, py, pc), 4 * px + 2 * py + pc).wait_recv()
        for cp in sends:
            cp.wait_send()
        if reduce:
            total = gat_ref[0]
            for d in range(1, n_dev):
                total = total + gat_ref[d]
            out_ref[...] = total

    out_shape = jax.ShapeDtypeStruct((r, c) if reduce else (n_dev, r, c), src.dtype)
    scratch = ([pltpu.VMEM((n_dev, r, c), src.dtype)] if reduce else []) + _dma_sems(n_dev - 1)
    return pl.pallas_call(kern, out_shape=out_shape, in_specs=[VMEM_SPEC], out_specs=VMEM_SPEC,
                          scratch_shapes=scratch, name=name)(src)


def _halves(t, axis):
    return t.reshape(t.shape[:axis] + (2, t.shape[axis] // 2) + t.shape[axis + 1:])


def _by_core(mine, theirs, axis):
    c = lax.axis_index("c")
    both = jnp.where(c == 0, jnp.stack([mine, theirs], axis), jnp.stack([theirs, mine], axis))
    return both.reshape(both.shape[:axis] + (2 * both.shape[axis + 1],) + both.shape[axis + 2:])


def _chip_order(own, others):
    me = 2 * lax.axis_index("x") + lax.axis_index("y")
    cands = jnp.concatenate([own[None], others], axis=0)
    slot_of_flip = (0, 2, 1, 3)
    pick = jnp.asarray(slot_of_flip, jnp.int32)[jnp.arange(N_CHIPS, dtype=jnp.int32) ^ me]
    return jnp.stack([lax.dynamic_index_in_dim(cands, pick[k], 0, keepdims=False) for k in range(N_CHIPS)])


def _join_shards(name, shards):
    _, r, c = shards.shape
    if name in COL_SHARDED:
        return shards.transpose(1, 0, 2).reshape(r, N_CHIPS * c)
    return shards.reshape(N_CHIPS * r, c)


def _split_shards(name, full):
    if full.ndim == 3:
        return full
    r, c = full.shape
    if name in COL_SHARDED:
        return jnp.stack([full[:, k * (c // N_CHIPS):(k + 1) * (c // N_CHIPS)] for k in range(N_CHIPS)])
    return full.reshape(N_CHIPS, r // N_CHIPS, c)


def _rope_tables(positions):
    pos = positions.reshape(-1).astype(F32)[:, None]
    half = RET_DK // 2
    ang = pos * (ROPE_BASE ** (-jnp.arange(half, dtype=F32) / half))
    cos_r = jnp.concatenate([jnp.cos(ang)] * 2, axis=1)
    sin_r = jnp.concatenate([-jnp.sin(ang), jnp.sin(ang)], axis=1)
    half = MLA_ROPE // 2
    ang = pos * (ROPE_BASE ** (-jnp.arange(half, dtype=F32) / half))
    zeros = jnp.zeros_like(ang)
    rest = LANES - MLA_ROPE
    c = jnp.concatenate([jnp.cos(ang)] * 2 + [jnp.ones((ang.shape[0], rest), F32)], axis=1)
    s1 = jnp.concatenate([-jnp.sin(ang), zeros, jnp.zeros((ang.shape[0], rest), F32)], axis=1)
    s2 = jnp.concatenate([zeros, jnp.sin(ang), jnp.zeros((ang.shape[0], rest), F32)], axis=1)
    return cos_r, sin_r, (c, s1, s2)


GATHER_GROUPS = (("ffn1_w_in", "ffn1_w_out"), ("w_in", "w_uq", "w_ukv"),
                 ("w_ret_o", "w_mla_o", "w_out", "ffn2_w_in", "ffn2_w_out", "ple_w_gate", "ple_w_proj"))
REDUCE_GROUPS = (("ple_w_gate", "ple_w_proj", "ffn2_w_in", "ffn2_w_out"),
                 ("w_out", "w_ret_o", "w_mla_o", "w_uq", "w_ukv", "w_in"), ("ffn1_w_in", "ffn1_w_out"))


def _gathered(tag, names, own, mine):
    theirs = _sibling_swap("gather_cores_" + tag, mine, False)
    out = {}
    for n, m, t in zip(names, mine, theirs):
        full = _chip_order(own[n], _by_core(m, t, 1))
        out[n] = full if n in ("ffn1_w_in", "ffn2_w_in") else _join_shards(n, full)
    return out


def _chip_sums(tag, names, grads):
    c = lax.axis_index("c")
    halves = [_halves(_split_shards(n, grads[n]), 1) for n in names]
    theirs = _sibling_swap("reduce_cores_" + tag, halves, True)
    sums = []
    for n, g, t in zip(names, halves, theirs):
        mine = lax.dynamic_index_in_dim(g, c, axis=1, keepdims=False)
        k, r, cc = mine.shape
        sums.append(_ewise("reduce_cores_add_" + n, lambda a, b: (a.astype(F32) + b.astype(F32),),
                           [mine.reshape(k * r, cc), t.reshape(k * r, cc)], 1, BF16)[0].reshape(k, r, cc))
    return sums


def _block_totals(names, sums, parts):
    me = 2 * lax.axis_index("x") + lax.axis_index("y")
    totals = []
    for n, s, pt in zip(names, sums, parts):
        own = lax.dynamic_index_in_dim(s, me, axis=0, keepdims=False)
        totals.append(_ewise("reduce_chips_add_" + n,
                             lambda a, b, c_, d: (((a.astype(F32) + b.astype(F32)) + c_.astype(F32)) + d.astype(F32),),
                             [own, pt[0], pt[1], pt[2]], 1, F32)[0])
    return totals


def _local_step(x, p, positions, target, shards, ln_g, ln_b, gn_g, qn_g, kvn_g):
    T, D = x.shape
    tm = min(256, T)
    H = MLA_HEADS
    qk, rv = RET_HEADS * RET_DK, RET_HEADS * RET_DV
    cos_r, sin_r, tabs = _rope_tables(positions)
    lgam = jnp.broadcast_to(jnp.log(1.0 - 2.0 ** (-5.0 - jnp.arange(RET_HEADS, dtype=F32)))[:, None, None],
                            (RET_HEADS, 1, LANES))
    lng = [ln_g[k:k + 1] for k in range(N_LN)]
    lnb = [ln_b[k:k + 1] for k in range(N_LN)]
    own = {n: _bf(shards[n]) for n in BIG_WEIGHTS}
    to_send = [[_halves(own[n], 0) for n in names] for names in GATHER_GROUPS]

    w = _gathered("a", GATHER_GROUPS[0], own, _chips_exchange("gather_chips_a", to_send[0], True))
    h1, z0, a1, *arrived = _ffn_fwd("ffn1_fwd", x, w["ffn1_w_in"], w["ffn1_w_out"], lng[0], lnb[0], tm,
                                    exchange=(to_send[1], True))
    w.update(_gathered("b", GATHER_GROUPS[1], own, arrived))

    w_in = w["w_in"]
    o_lat, o_kpe, o_gate = 2 * qk + 2 * rv, 2 * qk + 2 * rv + Q_LORA + KV_LORA, 2 * qk + 2 * rv + Q_LORA + KV_LORA + MLA_ROPE
    w_r, w_c = w_in[:, :o_lat], w_in[:, o_lat:o_kpe]
    w_kpe = jnp.pad(w_in[:, o_kpe:o_gate], ((0, 0), (0, LANES - MLA_ROPE)))
    w_g = w_in[:, o_gate:]
    w_uq = jnp.pad(w["w_uq"].reshape(Q_LORA, H, MLA_NOPE + MLA_ROPE),
                   ((0, 0), (0, 0), (0, MLA_QK - MLA_NOPE - MLA_ROPE))).reshape(Q_LORA, H * MLA_QK)
    w_ukv = w["w_ukv"].reshape(KV_LORA, H, MLA_NOPE + MLA_DV)
    w_uk = w_ukv[:, :, :MLA_NOPE].reshape(KV_LORA, H * MLA_NOPE)
    w_uv = w_ukv[:, :, MLA_NOPE:].reshape(KV_LORA, H * MLA_DV)

    rq, rk, rvv, rg = _proj_ret(h1, w_r, cos_r, sin_r, tm)
    lat, gates, q, k, v, latn, qt, kt, vt = _proj_mla(h1, tabs, w_c, w_kpe, w_g, w_uq, w_uk, w_uv, qn_g, kvn_g, tm)
    y = _ret_fwd(rq, rk, rvv, lgam)
    o, lse_rows, *arrived = _attn_fwd(k, qt, vt, exchange=to_send[2])
    w.update(_gathered("c", GATHER_GROUPS[2], own, arrived))
    h2, z1, yret, ymla, yr, mix = _mix_fwd(y, rg, o, gates, h1, gn_g, w["w_ret_o"], w["w_mla_o"], w["w_out"],
                                           lng[1], lnb[1], tm)
    h3, z2, a2 = _ffn_fwd("ffn2_fwd", h2, w["ffn2_w_in"], w["ffn2_w_out"], lng[2], lnb[2], tm)

    dh3, dgp, dpp, loss, dg3, db3 = _ple_loss(h3, p, target, w["ple_w_gate"], w["ple_w_proj"], lng[3], lnb[3], tm)
    dh2, da2, s2, df2, dg2, db2 = _ffn_bwd("ffn2_bwd", dh3, z2, a2, w["ffn2_w_in"], w["ffn2_w_out"], lng[2], tm)
    grads = {"ple_w_gate": _mm_tn("wg_ple_gate", h3, dgp), "ple_w_proj": _mm_tn("wg_ple_proj", p, dpp),
             "ffn2_w_in": _mm_tn("wg_ffn2_in", h2, da2, n_split=N_CHIPS), "ffn2_w_out": _mm_tn("wg_ffn2_out", s2, df2)}
    sums1 = _chip_sums("1", REDUCE_GROUPS[0], grads)
    (dz1, dgates, drg, dy, do, dyret, dymla, dg1, db1, dgn, dot_, delta_rows, *parts1) = _mix_bwd(
        dh2, z1, gates, yret, ymla, y, rg, o, gn_g, w["w_ret_o"], w["w_mla_o"], w["w_out"], lng[1], tm,
        exchange=(sums1, False))
    drq = _ret_bwd_q(rq, rk, rvv, dy, lgam)
    drk, drv = _ret_bwd_kv(rq, rk, rvv, dy, lgam)
    dk, dv, dqt = _attn_bwd(q, k, v, do, qt, kt, dot_, lse_rows, delta_rows)
    dlat, dkpe, dqb, dkv, dqg, dkg = _proj_mla_bwd(dqt, dk, dv, lat, tabs, w_uq, w_uk, w_uv, qn_g, kvn_g, tm)
    dh1, dpr = _proj_bwd(drq, drk, drv, drg, dz1, dlat, dkpe, dgates, cos_r, sin_r, w_r, w_c, w_kpe, w_g, tm)
    g_uq = _mm_tn("wg_uq", latn[:, :Q_LORA], dqb).reshape(Q_LORA, H, MLA_QK)[:, :, :MLA_NOPE + MLA_ROPE]
    g_ukv = _mm_tn("wg_ukv", latn[:, Q_LORA:], dkv)
    g_uk = g_ukv[:, :H * MLA_NOPE].reshape(KV_LORA, H, MLA_NOPE)
    g_uv = g_ukv[:, H * MLA_NOPE:].reshape(KV_LORA, H, MLA_DV)
    g_in = _mm_tn("wg_in", h1, dpr)
    grads.update({
        "w_in": jnp.concatenate([g_in[:, :o_kpe + MLA_ROPE], g_in[:, o_kpe + LANES:o_kpe + LANES + 2 * D]], axis=1),
        "w_ret_o": _mm_tn("wg_ret_o", yr, dyret),
        "w_uq": g_uq.reshape(Q_LORA, H * (MLA_NOPE + MLA_ROPE)),
        "w_ukv": jnp.concatenate([g_uk, g_uv], axis=2).reshape(KV_LORA, H * (MLA_NOPE + MLA_DV)),
        "w_mla_o": _mm_tn("wg_mla_o", o, dymla),
        "w_out": _mm_tn("wg_out", mix, dz1)})
    sums2 = _chip_sums("2", REDUCE_GROUPS[1], grads)
    dx, da1, s1, df1, dg0, db0, *parts2 = _ffn_bwd("ffn1_bwd", dh1, z0, a1, w["ffn1_w_in"], w["ffn1_w_out"], lng[0], tm,
                                                   exchange=(sums2, False))
    grads.update({"ffn1_w_in": _mm_tn("wg_ffn1_in", x, da1, n_split=N_CHIPS),
                  "ffn1_w_out": _mm_tn("wg_ffn1_out", s1, df1)})
    sums3 = _chip_sums("3", REDUCE_GROUPS[2], grads)
    parts3 = _chips_exchange("reduce_chips_3", sums3, False)

    names = [n for group in REDUCE_GROUPS for n in group]
    totals = _block_totals(names, sums1 + sums2 + sums3, list(parts1) + list(parts2) + list(parts3))
    others = _sibling_swap("reduce_join", totals, False)
    reduced = {n: _by_core(t, o_, 0) for n, t, o_ in zip(names, totals, others)}
    small = {"ln_g": jnp.concatenate([dg0, dg1, dg2, dg3], axis=0), "ln_b": jnp.concatenate([db0, db1, db2, db3], axis=0),
             "ret_gn_g": dgn, "q_norm_g": dqg, "kv_norm_g": dkg}
    return loss[0, 0], dx, reduced, small


def kernel(x, p, positions, ln_g, ln_b, ffn1_w_in, ffn1_w_out, w_in, ret_gn_g, w_ret_o, q_norm_g, kv_norm_g, w_uq, w_ukv, w_mla_o, w_out, ffn2_w_in, ffn2_w_out, ple_w_gate, ple_w_proj, loss_target, m_ln_g, m_ln_b, m_ffn1_w_in, m_ffn1_w_out, m_w_in, m_ret_gn_g, m_w_ret_o, m_q_norm_g, m_kv_norm_g, m_w_uq, m_w_ukv, m_w_mla_o, m_w_out, m_ffn2_w_in, m_ffn2_w_out, m_ple_w_gate, m_ple_w_proj, v_ln_g, v_ln_b, v_ffn1_w_in, v_ffn1_w_out, v_w_in, v_ret_gn_g, v_w_ret_o, v_q_norm_g, v_kv_norm_g, v_w_uq, v_w_ukv, v_w_mla_o, v_w_out, v_ffn2_w_in, v_ffn2_w_out, v_ple_w_gate, v_ple_w_proj):
    names = ("ln_g", "ln_b", "ffn1_w_in", "ffn1_w_out", "w_in", "ret_gn_g", "w_ret_o", "q_norm_g", "kv_norm_g", "w_uq",
             "w_ukv", "w_mla_o", "w_out", "ffn2_w_in", "ffn2_w_out", "ple_w_gate", "ple_w_proj")
    weights = dict(zip(names, (ln_g, ln_b, ffn1_w_in, ffn1_w_out, w_in, ret_gn_g, w_ret_o, q_norm_g, kv_norm_g, w_uq,
                               w_ukv, w_mla_o, w_out, ffn2_w_in, ffn2_w_out, ple_w_gate, ple_w_proj)))
    m_in = dict(zip(names, (m_ln_g, m_ln_b, m_ffn1_w_in, m_ffn1_w_out, m_w_in, m_ret_gn_g, m_w_ret_o, m_q_norm_g,
                            m_kv_norm_g, m_w_uq, m_w_ukv, m_w_mla_o, m_w_out, m_ffn2_w_in, m_ffn2_w_out, m_ple_w_gate,
                            m_ple_w_proj)))
    v_in = dict(zip(names, (v_ln_g, v_ln_b, v_ffn1_w_in, v_ffn1_w_out, v_w_in, v_ret_gn_g, v_w_ret_o, v_q_norm_g,
                            v_kv_norm_g, v_w_uq, v_w_ukv, v_w_mla_o, v_w_out, v_ffn2_w_in, v_ffn2_w_out, v_ple_w_gate,
                            v_ple_w_proj)))
    chip = 2 * lax.axis_index("x") + lax.axis_index("y")
    D = x.shape[-1]
    dq = D // N_CHIPS

    shards = {n: weights[n][0] for n in BIG_WEIGHTS}
    ln_all = _all_devices("gather_ln", jnp.concatenate([ln_g[0], ln_b[0]], axis=0), False)
    ln_full = ln_all[::2].transpose(1, 0, 2).reshape(2 * N_LN, D)
    loss, dx, big, small = _local_step(x[0], p[0, 0], positions, loss_target[0], shards, ln_full[:N_LN],
                                       ln_full[N_LN:], ret_gn_g, q_norm_g, kv_norm_g)

    loss = lax.psum(loss, ("x", "y", "c"))
    small_names = ("ln_g", "ln_b", "ret_gn_g", "q_norm_g", "kv_norm_g")
    flat = jnp.concatenate([small[n].reshape(-1) for n in small_names])
    rows = -(-flat.shape[0] // LANES // 8) * 8
    flat = jnp.pad(flat, (0, rows * LANES - flat.shape[0])).reshape(rows, LANES)
    flat = _all_devices("reduce_small", flat, True).reshape(-1)
    off = 0
    for n in small_names:
        size = small[n].size
        small[n] = flat[off:off + size].reshape(small[n].shape)
        off += size
    g_out = dict(big)
    for n in ("ln_g", "ln_b"):
        g_out[n] = lax.dynamic_slice_in_dim(small[n], chip * dq, dq, axis=1)
    for n in ("ret_gn_g", "q_norm_g", "kv_norm_g"):
        g_out[n] = small[n]

    deltas, new_m, new_v = {}, {}, {}
    for n in names:
        g = g_out[n].reshape(weights[n].shape)
        g_out[n] = g
        deltas[n], new_m[n], new_v[n] = _adamw("adamw_" + n, weights[n], g, m_in[n], v_in[n])
    return (loss, dx[None], *[g_out[n] for n in names], *[deltas[n] for n in names], *[new_m[n] for n in names],
            *[new_v[n] for n in names])
```

```python
import functools

import jax
import jax.numpy as jnp
from jax import lax
from jax.experimental import pallas as pl
from jax.experimental.pallas import tpu as pltpu

D_MODEL = 1024
CHUNK = 64
D_PLE = 256
D_FF = 2816
RET_HEADS = 8
RET_DK = 128
RET_DV = 256
MLA_HEADS = 8
MLA_NOPE = 128
MLA_ROPE = 64
MLA_DV = 128
MLA_QK = 256
Q_LORA = 256
KV_LORA = 256
ROPE_BASE = 10000.0
EPS = 1e-5
N_LN = 4
ALPHA = 2.0 ** 0.25
ADAM_LR = 0.001
ADAM_B1 = 0.9
ADAM_B2 = 0.999
ADAM_EPS = 1e-08
ADAM_WD = 0.01
ADAM_STEP = 10

LANES = 128
VMEM_LIMIT = 60 << 20
N_CHIPS = 4

F32 = jnp.float32
BF16 = jnp.bfloat16
MESH = pl.DeviceIdType.MESH
HBM_SPEC = pl.BlockSpec(memory_space=pltpu.HBM)
VMEM_SPEC = pl.BlockSpec(memory_space=pltpu.VMEM)

BIG_WEIGHTS = ("ffn1_w_in", "ffn1_w_out", "w_in", "w_ret_o", "w_uq", "w_ukv", "w_mla_o", "w_out",
               "ffn2_w_in", "ffn2_w_out", "ple_w_gate", "ple_w_proj")
COL_SHARDED = ("ffn1_w_in", "w_in", "w_uq", "w_ukv", "ffn2_w_in", "ple_w_proj")


def _dot(a, b):
    return jnp.dot(a, b, preferred_element_type=F32)


def _dot_nt(a, b):
    return lax.dot_general(a, b, (((1,), (1,)), ((), ())), preferred_element_type=F32)


def _dot_tn(a, b):
    return lax.dot_general(a, b, (((0,), (0,)), ((), ())), preferred_element_type=F32)


def _bf(x):
    return x.astype(BF16)


def _sigmoid(x):
    return 0.5 * jnp.tanh(0.5 * x) + 0.5


def _mean(x):
    return jnp.mean(x, axis=-1, keepdims=True)


def _ln_stats(z):
    zc = z - _mean(z)
    rstd = lax.rsqrt(_mean(zc * zc) + EPS)
    return zc * rstd, rstd


def _ln_bwd(dy, xhat, rstd, g):
    dxhat = dy * g
    dz = rstd * (dxhat - _mean(dxhat) - xhat * _mean(dxhat * xhat))
    return dz, jnp.sum(dy * xhat, axis=0, keepdims=True), jnp.sum(dy, axis=0, keepdims=True)


def _roll(x, shift):
    return pltpu.roll(x, shift, 1)


def _chunk_of(idx):
    return jnp.right_shift(idx, CHUNK.bit_length() - 1)


def _tile(n, cap, mult=LANES):
    if n <= cap:
        return n
    for t in range(cap - cap % mult, 0, -mult):
        if n % t == 0:
            return t
    return n


def _zero_map(nd, *_):
    return (0,) * nd


def _params(sem):
    return pltpu.CompilerParams(dimension_semantics=sem, vmem_limit_bytes=VMEM_LIMIT)


def _rowcall(name, body, n_rows, tm, row_ins, full_ins, row_outs, acc_outs=(), tiled_outs=(), tiled_ins=(),
             exchange=None):
    n_steps = n_rows // tm
    ex_srcs, broadcast = exchange if exchange else ((), False)
    n_ex = len(ex_srcs)
    n_in = len(row_ins) + len(tiled_ins) + len(full_ins)
    n_out = len(row_outs) + len(acc_outs) + len(tiled_outs)

    def kern(*refs):
        step = pl.program_id(0)
        ex_in, ex_out = refs[n_in:n_in + n_ex], refs[n_in + n_ex + n_out:n_in + 2 * n_ex + n_out]
        sems = refs[n_in + 2 * n_ex + n_out:]
        if n_ex:
            @pl.when(step == 0)
            def _():
                for send, _ in _chip_copies(ex_in, ex_out, *sems, broadcast):
                    send.start()

        body(step, *refs[:n_in], *refs[n_in + n_ex:n_in + n_ex + n_out])
        if n_ex:
            @pl.when(step == n_steps - 1)
            def _():
                _wait_copies(_chip_copies(ex_in, ex_out, *sems, broadcast))

    in_specs = [pl.BlockSpec((tm, a.shape[1]), lambda i: (i, 0)) for a in row_ins]
    in_specs += [spec for (_, spec) in tiled_ins]
    row_ins = list(row_ins) + [a for (a, _) in tiled_ins]
    in_specs += [pl.BlockSpec(a.shape, functools.partial(_zero_map, a.ndim), pipeline_mode=pl.Buffered(1))
                 for a in full_ins]
    in_specs += [HBM_SPEC] * n_ex
    out_specs = [pl.BlockSpec((tm, w), lambda i: (i, 0)) for (w, _) in row_outs]
    out_specs += [pl.BlockSpec(s, functools.partial(_zero_map, len(s))) for (s, _) in acc_outs]
    out_specs += [spec for (_, spec) in tiled_outs]
    out_specs += [HBM_SPEC] * n_ex
    out_shape = [jax.ShapeDtypeStruct((n_rows, w), dt) for (w, dt) in row_outs]
    out_shape += [jax.ShapeDtypeStruct(s, dt) for (s, dt) in acc_outs]
    out_shape += [shape for (shape, _) in tiled_outs]
    out_shape += _exchange_shapes(ex_srcs)
    return pl.pallas_call(kern, grid=(n_steps,), in_specs=in_specs, out_specs=out_specs, out_shape=out_shape,
                          scratch_shapes=_dma_sems(n_ex * N_PEER_CHIPS) if n_ex else [], name=name,
                          compiler_params=_params(("arbitrary",)))(*row_ins, *full_ins, *ex_srcs)


def _acc(step, ref, val):
    @pl.when(step == 0)
    def _():
        ref[...] = val

    @pl.when(step != 0)
    def _():
        ref[...] += val


def _ffn_fwd(name, x, w_in4, w_out, ln_g, ln_b, tm, exchange=None):
    T, D = x.shape
    fh = w_in4.shape[2]

    def body(i, x_ref, w4_ref, wo_ref, g_ref, b_ref, h_ref, z_ref, a_ref):
        xv = x_ref[...]
        xb = _bf(xv)
        f = jnp.zeros((tm, D), F32)
        for k in range(2):
            gk = _dot(xb, w4_ref[k])
            uk = _dot(xb, w4_ref[2 + k])
            a_ref[:, k * fh:(k + 1) * fh] = _bf(gk)
            a_ref[:, (2 + k) * fh:(3 + k) * fh] = _bf(uk)
            f += _dot(_bf(gk * _sigmoid(gk) * uk), wo_ref[k * fh:(k + 1) * fh, :])
        z = ALPHA * xv + 0.5 * f
        xhat, _ = _ln_stats(z)
        z_ref[...] = z
        h_ref[...] = xhat * g_ref[...] + b_ref[...]

    return _rowcall(name, body, T, tm, [x], [w_in4, w_out, ln_g, ln_b],
                    [(D, F32), (D, F32), (4 * fh, BF16)], exchange=exchange)


def _ffn_bwd(name, dh, z, a, w_in4, w_out, ln_g, tm, exchange=None):
    T, D = dh.shape
    fh = w_in4.shape[2]

    def body(i, dh_ref, z_ref, a_ref, w4_ref, wo_ref, g_ref, dx_ref, da_ref, s_ref, df_ref, dg_ref, db_ref):
        xhat, rstd = _ln_stats(z_ref[...])
        dz, dg, db = _ln_bwd(dh_ref[...], xhat, rstd, g_ref[...])
        _acc(i, dg_ref, dg)
        _acc(i, db_ref, db)
        dfb = _bf(0.5 * dz)
        df_ref[...] = dfb
        dx = ALPHA * dz
        for k in range(2):
            gk = a_ref[:, k * fh:(k + 1) * fh].astype(F32)
            uk = a_ref[:, (2 + k) * fh:(3 + k) * fh].astype(F32)
            ds = _dot_nt(dfb, wo_ref[k * fh:(k + 1) * fh, :])
            sig = _sigmoid(gk)
            silu = gk * sig
            dgk = _bf(ds * uk * sig * (1.0 + gk * (1.0 - sig)))
            duk = _bf(ds * silu)
            s_ref[:, k * fh:(k + 1) * fh] = _bf(silu * uk)
            da_ref[:, k * fh:(k + 1) * fh] = dgk
            da_ref[:, (2 + k) * fh:(3 + k) * fh] = duk
            dx += _dot_nt(dgk, w4_ref[k]) + _dot_nt(duk, w4_ref[2 + k])
        dx_ref[...] = dx

    return _rowcall(name, body, T, tm, [dh, z, a], [w_in4, w_out, ln_g],
                    [(D, F32), (4 * fh, BF16), (2 * fh, BF16), (D, BF16)],
                    [((1, D), F32), ((1, D), F32)], exchange=exchange)


WG_TILE_N = 1536


def _mm_tn(name, a, b, out_dtype=BF16, n_split=1):
    T, M = a.shape
    N = b.shape[1]
    tk = _tile(T, 2048, 8)
    tm = _tile(M, 1408)
    tn = _tile(N // n_split, WG_TILE_N)
    per = N // n_split // tn
    nk = T // tk
    if n_split > 1:
        out_spec = pl.BlockSpec((None, tm, tn), lambda i, j, k: (j // per, i, j % per))
        out_shape = jax.ShapeDtypeStruct((n_split, M, N // n_split), out_dtype)
    else:
        out_spec = pl.BlockSpec((tm, tn), lambda i, j, k: (i, j))
        out_shape = jax.ShapeDtypeStruct((M, N), out_dtype)

    def kern(a_ref, b_ref, o_ref, acc_ref):
        k = pl.program_id(2)
        part = _dot_tn(_bf(a_ref[...]), _bf(b_ref[...]))

        @pl.when(k == 0)
        def _():
            acc_ref[...] = part

        @pl.when(k != 0)
        def _():
            acc_ref[...] += part

        @pl.when(k == nk - 1)
        def _():
            o_ref[...] = acc_ref[...].astype(out_dtype)

    return pl.pallas_call(
        kern, grid=(M // tm, N // tn, nk),
        in_specs=[pl.BlockSpec((tk, tm), lambda i, j, k: (k, i)), pl.BlockSpec((tk, tn), lambda i, j, k: (k, j))],
        out_specs=out_spec, out_shape=out_shape,
        scratch_shapes=[pltpu.VMEM((tm, tn), F32)], name=name,
        compiler_params=_params(("arbitrary", "arbitrary", "arbitrary")))(a, b)


def _proj_ret(h1, w_r, cos_r, sin_r, tm):
    T, D = h1.shape
    qk = RET_HEADS * RET_DK
    rv = RET_HEADS * RET_DV

    def body(i, h_ref, cos_ref, sin_ref, w_ref, q_ref, k_ref, v_ref, g_ref):
        hb = _bf(h_ref[...])
        cos, sin = cos_ref[...], sin_ref[...]
        for out_ref, off, scale in ((q_ref, 0, 1.0), (k_ref, qk, RET_DK ** -0.5)):
            pr = _dot(hb, w_ref[:, off:off + qk])
            for h in range(RET_HEADS):
                t = pr[:, h * RET_DK:(h + 1) * RET_DK]
                out_ref[:, h * RET_DK:(h + 1) * RET_DK] = _bf((t * cos + _roll(t, RET_DK // 2) * sin) * scale)
        v_ref[...] = _bf(_dot(hb, w_ref[:, 2 * qk:2 * qk + rv]))
        g_ref[...] = _dot(hb, w_ref[:, 2 * qk + rv:2 * qk + 2 * rv])

    return _rowcall("proj_ret", body, T, tm, [h1, cos_r, sin_r], [w_r],
                    [(qk, BF16), (qk, BF16), (rv, BF16), (rv, F32)])


def _rope_pe(t, c, s1, s2):
    return t * c + _roll(t, LANES - MLA_ROPE // 2) * s1 + _roll(t, MLA_ROPE // 2) * s2


def _rope_pe_bwd(dy, c, s1, s2):
    return dy * c + _roll(dy * s1, MLA_ROPE // 2) + _roll(dy * s2, LANES - MLA_ROPE // 2)


def _rms(x, g):
    r = lax.rsqrt(_mean(x * x) + EPS)
    return x * r, r


def _attn_block(T):
    return min(512, T)


def _transposed_blocks(T, tm, w, dtype):
    tb = _attn_block(T)
    per = tb // tm
    return (jax.ShapeDtypeStruct((T // tb, MLA_HEADS, w, tb), dtype),
            pl.BlockSpec((None, MLA_HEADS, w, tm), lambda i: (i // per, 0, 0, i % per)))


ATTN_SCALE = (MLA_NOPE + MLA_ROPE) ** -0.5
LOG2E = 1.4426950408889634
Q_PRESCALE = ATTN_SCALE * LOG2E
V_ONES = 16


def _proj_mla(h1, tabs, w_c, w_kpe, w_g, w_uq, w_uk, w_uv, qn_g, kvn_g, tm):
    T, D = h1.shape
    H = MLA_HEADS

    def body(i, h_ref, c_ref, s1_ref, s2_ref, wc_ref, wk_ref, wg_ref, wuq_ref, wuk_ref, wuv_ref, qg_ref, kg_ref,
             lat_ref, gt_ref, q_ref, k_ref, v_ref, ln_ref, qt_ref, kt_ref, vt_ref):
        hb = _bf(h_ref[...])
        c, s1, s2 = c_ref[...], s1_ref[...], s2_ref[...]
        lat = _dot(hb, wc_ref[...])
        lat_ref[...] = lat
        gt_ref[...] = _dot(hb, wg_ref[...])
        cqn, _ = _rms(lat[:, :Q_LORA], None)
        ckn, _ = _rms(lat[:, Q_LORA:], None)
        cqn = _bf(cqn * qg_ref[...])
        ckn = _bf(ckn * kg_ref[...])
        ln_ref[:, :Q_LORA] = cqn
        ln_ref[:, Q_LORA:] = ckn
        q = _dot(cqn, wuq_ref[...])
        kn = _dot(ckn, wuk_ref[...])
        vv = _dot(ckn, wuv_ref[...])
        v_ref[...] = _bf(vv)
        kpe = _rope_pe(_dot(hb, wk_ref[...]), c, s1, s2)
        ones = jnp.ones((V_ONES, tm), BF16)
        for h in range(H):
            o = h * MLA_QK
            qh = jnp.concatenate([q[:, o:o + MLA_NOPE], _rope_pe(q[:, o + MLA_NOPE:o + MLA_QK], c, s1, s2)], axis=1)
            qh = qh * Q_PRESCALE
            kh = jnp.concatenate([kn[:, h * MLA_NOPE:(h + 1) * MLA_NOPE], kpe], axis=1)
            q_ref[:, o:o + MLA_QK] = _bf(qh)
            k_ref[:, o:o + MLA_QK] = _bf(kh)
            qt_ref[h] = _bf(qh.T)
            kt_ref[h] = _bf(kh.T)
            vt_ref[h] = jnp.concatenate([_bf(vv[:, h * MLA_DV:(h + 1) * MLA_DV].T), ones], axis=0)

    lat_w = Q_LORA + KV_LORA
    return _rowcall("proj_mla", body, T, tm, [h1, *tabs], [w_c, w_kpe, w_g, w_uq, w_uk, w_uv, qn_g, kvn_g],
                    [(lat_w, F32), (2 * D, F32), (H * MLA_QK, BF16), (H * MLA_QK, BF16), (H * MLA_DV, BF16),
                     (lat_w, BF16)],
                    tiled_outs=[_transposed_blocks(T, tm, MLA_QK, BF16), _transposed_blocks(T, tm, MLA_QK, BF16),
                                _transposed_blocks(T, tm, MLA_DV + V_ONES, BF16)])


def _ret_block(T):
    return min(256, T)


RET_HEADS_PER_STEP = 8


def _ret_dmat(lg, bt):
    n = lax.broadcasted_iota(jnp.int32, (bt, bt), 0)
    m = lax.broadcasted_iota(jnp.int32, (bt, bt), 1)
    return jnp.where(_chunk_of(m) <= _chunk_of(n), jnp.exp(lg * jnp.abs(n - m).astype(F32)), 0.0)


def _ret_scan(name, per_head, lgam, ins, outs, rev):
    T = ins[0][0].shape[0]
    bt = _ret_block(T)
    nb = T // bt
    hps = min(RET_HEADS_PER_STEP, RET_HEADS)
    n_in, n_out = len(ins), len(outs)

    def kern(lg_ref, *refs):
        in_refs, out_refs = refs[:n_in], refs[n_in:n_in + n_out]
        state_ref, dmat_ref = refs[n_in + n_out:]

        @pl.when(pl.program_id(1) == 0)
        def _():
            state_ref[...] = jnp.zeros_like(state_ref)
            for hh in range(hps):
                dmat_ref[hh] = _ret_dmat(lg_ref[hh][:, :1], bt)

        pos = lax.broadcasted_iota(jnp.int32, (bt, 1), 0).astype(F32)
        for hh in range(hps):
            lg = lg_ref[hh][:, :1]
            xi, zeta, gb = jnp.exp(lg * (pos + 1.0)), jnp.exp(lg * (bt - 1.0 - pos)), jnp.exp(lg * bt)
            tiles = [r[:, hh * w:(hh + 1) * w] for r, (_, w) in zip(in_refs, ins)]
            res = per_head(dmat_ref[hh], xi, zeta, gb, state_ref.at[hh], *tiles)
            for o_ref, (w, _), val in zip(out_refs, outs, res):
                o_ref[:, hh * w:(hh + 1) * w] = val.astype(o_ref.dtype)

    def blk(w):
        if rev:
            return pl.BlockSpec((bt, hps * w), lambda g, b: (nb - 1 - b, g))
        return pl.BlockSpec((bt, hps * w), lambda g, b: (b, g))

    return pl.pallas_call(
        kern, grid=(RET_HEADS // hps, nb),
        in_specs=[pl.BlockSpec((hps, 1, LANES), lambda g, b: (g, 0, 0))] + [blk(w) for _, w in ins],
        out_specs=[blk(w) for w, _ in outs],
        out_shape=[jax.ShapeDtypeStruct((T, RET_HEADS * w), dt) for w, dt in outs],
        scratch_shapes=[pltpu.VMEM((hps, RET_DK, RET_DV), F32), pltpu.VMEM((hps, bt, bt), F32)], name=name,
        compiler_params=_params(("arbitrary", "arbitrary")))(lgam, *[a for a, _ in ins])


def _ret_fwd(rq, rk, rv, lgam):
    def per_head(dmat, xi, zeta, gb, s_ref, q, k, v):
        sc = _dot_nt(q, k) * dmat
        y = _dot(_bf(sc), v) + _dot(q, _bf(s_ref[...])) * xi
        s_ref[...] = s_ref[...] * gb + _dot_tn(_bf(k.astype(F32) * zeta), v)
        return (y,)

    return _ret_scan("ret_fwd", per_head, lgam, [(rq, RET_DK), (rk, RET_DK), (rv, RET_DV)], [(RET_DV, F32)], False)[0]


def _ret_bwd_q(rq, rk, rv, dy, lgam):
    def per_head(dmat, xi, zeta, gb, s_ref, k, v, dy):
        dp = _dot_nt(dy, v) * dmat
        dq = _dot(_bf(dp), k) + _dot_nt(dy, _bf(s_ref[...])) * xi
        s_ref[...] = s_ref[...] * gb + _dot_tn(_bf(k.astype(F32) * zeta), v)
        return (dq,)

    return _ret_scan("ret_bwd_q", per_head, lgam, [(rk, RET_DK), (rv, RET_DV), (dy, RET_DV)], [(RET_DK, F32)], False)[0]


def _ret_bwd_kv(rq, rk, rv, dy, lgam):
    def per_head(dmat, xi, zeta, gb, g_ref, q, k, v, dy):
        gs = _bf(g_ref[...])
        p = _dot_nt(q, k) * dmat
        dp = _dot_nt(dy, v) * dmat
        dv = _dot_tn(_bf(p), dy) + _dot(k, gs) * zeta
        dk = _dot_tn(_bf(dp), q) + _dot_nt(v, gs) * zeta
        g_ref[...] = g_ref[...] * gb + _dot_tn(_bf(q.astype(F32) * xi), dy)
        return dk, dv

    return _ret_scan("ret_bwd_kv", per_head, lgam, [(rq, RET_DK), (rk, RET_DK), (rv, RET_DV), (dy, RET_DV)],
                     [(RET_DK, F32), (RET_DV, BF16)], True)


def _attn_mask_t(tb):
    key = lax.broadcasted_iota(jnp.int32, (tb, tb), 0)
    qry = lax.broadcasted_iota(jnp.int32, (tb, tb), 1)
    return _chunk_of(key) <= _chunk_of(qry)


MASKED = -1e30
SUBLANES = 8


def _head_blocks(nb, w, tb):
    return pl.BlockSpec((nb, None, w, tb), lambda h, i: (0, h, 0, 0))


def _one_block(w, tb):
    return pl.BlockSpec((None, None, w, tb), lambda h, i: (i, h, 0, 0))


def _attn_fwd(k, qt, vt, exchange=()):
    T = k.shape[0]
    tb = _attn_block(T)
    nb = T // tb

    n_ex = len(exchange)

    def kern(qt_ref, k_ref, vt_ref, *refs):
        ex_in, (o_ref, lser_ref), ex_out = refs[:n_ex], refs[n_ex:n_ex + 2], refs[n_ex + 2:2 * n_ex + 2]
        m_ref, acc_ref, sa_ref, sb_ref = refs[2 * n_ex + 2:2 * n_ex + 6]
        sems = refs[2 * n_ex + 6:]
        qb = pl.program_id(1)
        first = jnp.logical_and(pl.program_id(0) == 0, qb == 0)
        last = jnp.logical_and(pl.program_id(0) == MLA_HEADS - 1, qb == nb - 1)
        if n_ex:
            @pl.when(first)
            def _():
                for send, _ in _chip_copies(ex_in, ex_out, *sems, True):
                    send.start()

        qt = qt_ref[...]
        m_ref[...] = jnp.full_like(m_ref, MASKED)
        acc_ref[...] = jnp.zeros_like(acc_ref)

        def scores(kb):
            rows = pl.ds(pl.multiple_of(kb * tb, tb), tb)
            return _dot(k_ref[rows, :], qt)

        def update(s, kb):
            m_old = m_ref[...]
            m_new = jnp.maximum(m_old, jnp.max(s, axis=0, keepdims=True))
            p = jnp.exp2(s - m_new)
            acc_ref[...] = acc_ref[...] * jnp.exp2(m_old - m_new) + _dot(vt_ref[kb], _bf(p))
            m_ref[...] = m_new

        def masked(s):
            return jnp.where(_attn_mask_t(tb), s, MASKED)

        sa_ref[...] = scores(0)

        def pair_body(j, carry):
            sb_ref[...] = scores(2 * j + 1)
            update(sa_ref[...], 2 * j)
            sa_ref[...] = scores(2 * j + 2)
            update(sb_ref[...], 2 * j + 1)
            return carry

        lax.fori_loop(0, qb // 2, pair_body, 0)

        @pl.when(qb % 2 == 0)
        def _():
            update(masked(sa_ref[...]), qb)

        @pl.when(qb % 2 == 1)
        def _():
            sb_ref[...] = masked(scores(qb))
            update(sa_ref[...], qb - 1)
            update(sb_ref[...], qb)

        l = acc_ref[MLA_DV:MLA_DV + 1, :]
        o_ref[...] = (acc_ref[:MLA_DV, :] / l).T
        lser_ref[...] = jnp.broadcast_to(m_ref[...] + jnp.log2(l), (SUBLANES, tb))
        if n_ex:
            @pl.when(last)
            def _():
                _wait_copies(_chip_copies(ex_in, ex_out, *sems, True))

    return pl.pallas_call(
        kern, grid=(MLA_HEADS, nb),
        in_specs=[_one_block(MLA_QK, tb), pl.BlockSpec((T, MLA_QK), lambda h, i: (0, h)),
                  _head_blocks(nb, MLA_DV + V_ONES, tb)] + [HBM_SPEC] * n_ex,
        out_specs=[pl.BlockSpec((tb, MLA_DV), lambda h, i: (i, h)), _one_block(SUBLANES, tb)] + [HBM_SPEC] * n_ex,
        out_shape=[jax.ShapeDtypeStruct((T, MLA_HEADS * MLA_DV), F32),
                   jax.ShapeDtypeStruct((nb, MLA_HEADS, SUBLANES, tb), F32)] + _exchange_shapes(exchange),
        scratch_shapes=[pltpu.VMEM((1, tb), F32), pltpu.VMEM((MLA_DV + V_ONES, tb), F32),
                        pltpu.VMEM((tb, tb), F32), pltpu.VMEM((tb, tb), F32)]
        + (_dma_sems(n_ex * N_PEER_CHIPS) if n_ex else []),
        name="attn_fwd", compiler_params=_params(("arbitrary", "arbitrary")))(qt, k, vt, *exchange)


def _attn_bwd(q, k, v, do, qt, kt, dot_, lse_rows, delta_rows):
    T = q.shape[0]
    tb = _attn_block(T)
    nb = T // tb

    def kern(q_ref, k_ref, v_ref, do_ref, qt_ref, kt_ref, dot_ref, lse_ref, dl_ref, dk_ref, dv_ref, dqt_ref, dv_acc,
             sa_ref, pa_ref, sb_ref, pb_ref):
        kb = pl.program_id(1)
        kv, vv, ktv = k_ref[...], v_ref[...], kt_ref[...]
        dk_ref[...] = jnp.zeros_like(dk_ref)
        dv_acc[...] = jnp.zeros_like(dv_acc)

        @pl.when(kb == 0)
        def _():
            dqt_ref[...] = jnp.zeros_like(dqt_ref)

        def products(qb, s_ref, dp_ref, diagonal=False):
            s = _dot(kv, qt_ref[qb])
            s_ref[...] = jnp.where(_attn_mask_t(tb), s, MASKED) if diagonal else s
            dp_ref[...] = _dot(vv, dot_ref[qb])

        def consume(qb, s_ref, dp_ref):
            rows = pl.ds(pl.multiple_of(qb * tb, tb), tb)
            p = jnp.exp2(s_ref[...] - lse_ref[qb][:1, :])
            dv_acc[...] += _dot(_bf(p), do_ref[rows, :])
            ds = _bf(p * (dp_ref[...] - dl_ref[qb][:1, :]))
            dk_ref[...] += _dot(ds, q_ref[rows, :])
            dqt_ref[qb] += _dot(ktv, ds)

        n_full = nb - 1 - kb
        products(kb, sa_ref, pa_ref, diagonal=True)

        def pair_body(j, carry):
            q1 = kb + 1 + 2 * j
            products(q1, sb_ref, pb_ref)
            consume(q1 - 1, sa_ref, pa_ref)
            products(q1 + 1, sa_ref, pa_ref)
            consume(q1, sb_ref, pb_ref)
            return carry

        lax.fori_loop(0, n_full // 2, pair_body, 0)

        @pl.when(n_full % 2 == 0)
        def _():
            consume(nb - 1, sa_ref, pa_ref)

        @pl.when(n_full % 2 == 1)
        def _():
            products(nb - 1, sb_ref, pb_ref)
            consume(nb - 2, sa_ref, pa_ref)
            consume(nb - 1, sb_ref, pb_ref)

        dk_ref[...] = dk_ref[...] * (ATTN_SCALE / Q_PRESCALE)
        dv_ref[...] = _bf(dv_acc[...])

    def blk(w):
        return pl.BlockSpec((tb, w), lambda h, i: (i, h))

    def full(w):
        return pl.BlockSpec((T, w), lambda h, i: (0, h))

    return pl.pallas_call(
        kern, grid=(MLA_HEADS, nb),
        in_specs=[full(MLA_QK), blk(MLA_QK), blk(MLA_DV), full(MLA_DV), _head_blocks(nb, MLA_QK, tb),
                  _one_block(MLA_QK, tb), _head_blocks(nb, MLA_DV, tb), _head_blocks(nb, SUBLANES, tb),
                  _head_blocks(nb, SUBLANES, tb)],
        out_specs=[blk(MLA_QK), blk(MLA_DV), _head_blocks(nb, MLA_QK, tb)],
        out_shape=[jax.ShapeDtypeStruct((T, MLA_HEADS * MLA_QK), F32),
                   jax.ShapeDtypeStruct((T, MLA_HEADS * MLA_DV), BF16),
                   jax.ShapeDtypeStruct((nb, MLA_HEADS, MLA_QK, tb), F32)],
        scratch_shapes=[pltpu.VMEM((tb, MLA_DV), F32)] + [pltpu.VMEM((tb, tb), F32)] * 4,
        name="attn_bwd", compiler_params=_params(("arbitrary", "arbitrary")))(
            q, k, v, do, qt, kt, dot_, lse_rows, delta_rows)


def _group_norm(y):
    yc = y - _mean(y)
    rstd = lax.rsqrt(_mean(yc * yc) + EPS)
    return yc * rstd, rstd


def _mix_fwd(y, rg, o, gates, h1, gn_g, w_ret_o, w_mla_o, w_out, ln_g, ln_b, tm):
    T, D = h1.shape

    def body(i, y_ref, rg_ref, o_ref, gt_ref, h_ref, gn_ref, wr_ref, wm_ref, wo_ref, g_ref, b_ref,
             h2_ref, z_ref, yret_ref, ymla_ref, yr_ref, mix_ref):
        for h in range(RET_HEADS):
            sl = slice(h * RET_DV, (h + 1) * RET_DV)
            yn, _ = _group_norm(y_ref[:, sl])
            r = rg_ref[:, sl]
            yr_ref[:, sl] = _bf(r * _sigmoid(r) * (yn * gn_ref[:, sl]))
        yret = _dot(yr_ref[...], wr_ref[...])
        ymla = _dot(_bf(o_ref[...]), wm_ref[...])
        yret_ref[...] = yret
        ymla_ref[...] = ymla
        mix = _bf(_sigmoid(gt_ref[:, :D]) * yret + _sigmoid(gt_ref[:, D:]) * ymla)
        mix_ref[...] = mix
        z = ALPHA * h_ref[...] + _dot(mix, wo_ref[...])
        xhat, _ = _ln_stats(z)
        z_ref[...] = z
        h2_ref[...] = xhat * g_ref[...] + b_ref[...]

    return _rowcall("mix_fwd", body, T, tm, [y, rg, o, gates, h1], [gn_g, w_ret_o, w_mla_o, w_out, ln_g, ln_b],
                    [(D, F32), (D, F32), (D, F32), (D, F32), (RET_HEADS * RET_DV, BF16), (D, BF16)])


def _mix_bwd(dh2, z1, gates, yret, ymla, y, rg, o, gn_g, w_ret_o, w_mla_o, w_out, ln_g, tm, exchange=None):
    T, D = dh2.shape
    rv = RET_HEADS * RET_DV

    def body(i, dh_ref, z_ref, gt_ref, yret_ref, ymla_ref, y_ref, rg_ref, o_ref, gn_ref, wr_ref, wm_ref, wo_ref, g_ref,
             dz_ref, dgt_ref, drg_ref, dy_ref, do_ref, dyret_ref, dymla_ref, dg_ref, db_ref, dgn_ref, dot_ref,
             dl_ref):
        xhat, rstd = _ln_stats(z_ref[...])
        dz, dg, db = _ln_bwd(dh_ref[...], xhat, rstd, g_ref[...])
        _acc(i, dg_ref, dg)
        _acc(i, db_ref, db)
        dz_ref[...] = dz
        dmix = _dot_nt(_bf(dz), wo_ref[...])
        sr = _sigmoid(gt_ref[:, :D])
        sm = _sigmoid(gt_ref[:, D:])
        dgt_ref[:, :D] = _bf(dmix * yret_ref[...] * sr * (1.0 - sr))
        dgt_ref[:, D:] = _bf(dmix * ymla_ref[...] * sm * (1.0 - sm))
        dyret = _bf(dmix * sr)
        dymla = _bf(dmix * sm)
        dyret_ref[...] = dyret
        dymla_ref[...] = dymla
        dov = _dot_nt(dymla, wm_ref[...])
        do_ref[...] = _bf(dov)
        for h in range(MLA_HEADS):
            sl = slice(h * MLA_DV, (h + 1) * MLA_DV)
            dot_ref[h] = _bf(dov[:, sl].T)
            delta = jnp.sum(dov[:, sl] * o_ref[:, sl], axis=-1, keepdims=True)
            dl_ref[h] = jnp.broadcast_to(delta, (tm, LANES)).T[:SUBLANES, :]
        dyr = _dot_nt(dyret, wr_ref[...])
        dgn = []
        for h in range(RET_HEADS):
            sl = slice(h * RET_DV, (h + 1) * RET_DV)
            yn, grstd = _group_norm(y_ref[:, sl])
            r = rg_ref[:, sl]
            sig = _sigmoid(r)
            d = dyr[:, sl]
            drg_ref[:, sl] = _bf(d * (yn * gn_ref[:, sl]) * sig * (1.0 + r * (1.0 - sig)))
            dt = d * (r * sig)
            dgn.append(jnp.sum(dt * yn, axis=0, keepdims=True))
            dyn = dt * gn_ref[:, sl]
            dy_ref[:, sl] = _bf(grstd * (dyn - _mean(dyn) - yn * _mean(dyn * yn)))
        _acc(i, dgn_ref, jnp.concatenate(dgn, axis=1))

    return _rowcall("mix_bwd", body, T, tm, [dh2, z1, gates, yret, ymla, y, rg, o],
                    [gn_g, w_ret_o, w_mla_o, w_out, ln_g],
                    [(D, F32), (2 * D, BF16), (rv, BF16), (rv, BF16), (MLA_HEADS * MLA_DV, BF16), (D, BF16), (D, BF16)],
                    [((1, D), F32), ((1, D), F32), ((1, rv), F32)],
                    tiled_outs=[_transposed_blocks(T, tm, MLA_DV, BF16), _transposed_blocks(T, tm, SUBLANES, F32)],
                    exchange=exchange)


def _proj_mla_bwd(dqt, dk, dv, lat, tabs, w_uq, w_uk, w_uv, qn_g, kvn_g, tm):
    T = dk.shape[0]
    H = MLA_HEADS
    lat_w = Q_LORA + KV_LORA

    def body(i, dk_ref, dv_ref, lat_ref, c_ref, s1_ref, s2_ref, dqt_ref, wuq_ref, wuk_ref, wuv_ref, qg_ref, kg_ref,
             dlat_ref, dkpe_ref, dqb_ref, dkn_ref, dqg_ref, dkg_ref):
        c, s1, s2 = c_ref[...], s1_ref[...], s2_ref[...]
        dkpe = jnp.zeros((tm, LANES), F32)
        for h in range(H):
            o = h * MLA_QK
            dqh = dqt_ref[h].T * ATTN_SCALE
            dqb_ref[:, o:o + MLA_NOPE] = _bf(dqh[:, :MLA_NOPE])
            dqb_ref[:, o + MLA_NOPE:o + MLA_QK] = _bf(_rope_pe_bwd(dqh[:, MLA_NOPE:], c, s1, s2))
            dkn_ref[:, h * MLA_NOPE:(h + 1) * MLA_NOPE] = _bf(dk_ref[:, o:o + MLA_NOPE])
            dkpe += dk_ref[:, o + MLA_NOPE:o + MLA_QK]
        dkn_ref[:, H * MLA_NOPE:] = dv_ref[...]
        dkpe_ref[...] = _bf(_rope_pe_bwd(dkpe, c, s1, s2))
        dcqn = _dot_nt(dqb_ref[...], wuq_ref[...])
        dckn = _dot_nt(dkn_ref[:, :H * MLA_NOPE], wuk_ref[...]) + _dot_nt(dv_ref[...], wuv_ref[...])
        for dn, x, g_ref, dg_ref, sl in ((dcqn, lat_ref[:, :Q_LORA], qg_ref, dqg_ref, slice(0, Q_LORA)),
                                         (dckn, lat_ref[:, Q_LORA:], kg_ref, dkg_ref, slice(Q_LORA, lat_w))):
            xn, r = _rms(x, None)
            _acc(i, dg_ref, jnp.sum(dn * xn, axis=0, keepdims=True))
            dxn = dn * g_ref[...]
            dlat_ref[:, sl] = _bf(r * (dxn - xn * _mean(dxn * xn)))

    dqt_shape, dqt_spec = _transposed_blocks(T, tm, MLA_QK, F32)
    assert dqt.shape == dqt_shape.shape
    return _rowcall("proj_mla_bwd", body, T, tm, [dk, dv, lat, *tabs], [w_uq, w_uk, w_uv, qn_g, kvn_g],
                    [(lat_w, BF16), (LANES, BF16), (H * MLA_QK, BF16), (H * (MLA_NOPE + MLA_DV), BF16)],
                    [((1, Q_LORA), F32), ((1, KV_LORA), F32)], tiled_ins=[(dqt, dqt_spec)])


def _proj_bwd(drq, drk, drv, drg, dz1, dlat, dkpe, dgates, cos_r, sin_r, w_r, w_c, w_kpe, w_g, tm):
    T, D = dz1.shape
    qk = RET_HEADS * RET_DK
    rv = RET_HEADS * RET_DV
    o_lat = 2 * qk + 2 * rv
    o_kpe = o_lat + dlat.shape[1]
    o_gate = o_kpe + LANES
    o_end = o_gate + dgates.shape[1]
    width = -(-o_end // WG_TILE_N) * WG_TILE_N

    def body(i, drq_ref, drk_ref, drv_ref, drg_ref, dz_ref, dlat_ref, dkpe_ref, dgt_ref, cos_ref, sin_ref,
             wr_ref, wc_ref, wk_ref, wg_ref, dh_ref, dpr_ref):
        cos, sin = cos_ref[...], sin_ref[...]
        for src, off, scale in ((drq_ref, 0, 1.0), (drk_ref, qk, RET_DK ** -0.5)):
            for h in range(RET_HEADS):
                d = src[:, h * RET_DK:(h + 1) * RET_DK]
                dpr_ref[:, off + h * RET_DK:off + (h + 1) * RET_DK] = _bf(
                    (d * cos + _roll(d * sin, RET_DK // 2)) * scale)
        dpr_ref[:, 2 * qk:2 * qk + rv] = drv_ref[...]
        dpr_ref[:, 2 * qk + rv:o_lat] = drg_ref[...]
        dpr_ref[:, o_lat:o_kpe] = dlat_ref[...]
        dpr_ref[:, o_kpe:o_gate] = dkpe_ref[...]
        dpr_ref[:, o_gate:o_end] = dgt_ref[...]
        if width > o_end:
            dpr_ref[:, o_end:] = jnp.zeros((tm, width - o_end), BF16)
        dh_ref[...] = (ALPHA * dz_ref[...] + _dot_nt(dpr_ref[:, :o_lat], wr_ref[...])
                       + _dot_nt(dlat_ref[...], wc_ref[...]) + _dot_nt(dkpe_ref[...], wk_ref[...])
                       + _dot_nt(dgt_ref[...], wg_ref[...]))

    return _rowcall("proj_bwd", body, T, tm, [drq, drk, drv, drg, dz1, dlat, dkpe, dgates, cos_r, sin_r],
                    [w_r, w_c, w_kpe, w_g], [(D, F32), (width, BF16)])


def _ple_loss(h3, p, target, w_gate, w_proj, ln_g, ln_b, tm):
    T, D = h3.shape

    def body(i, h_ref, p_ref, t_ref, wg_ref, wp_ref, g_ref, b_ref, dh_ref, dgp_ref, dpp_ref, loss_ref, dg_ref, db_ref):
        hv = h_ref[...]
        sg = _sigmoid(_dot(_bf(hv), wg_ref[...]))
        pp = _dot(_bf(p_ref[...]), wp_ref[...])
        xhat, rstd = _ln_stats(ALPHA * hv + sg * pp)
        err = xhat * g_ref[...] + b_ref[...] - t_ref[...]
        row_loss = 0.5 * _mean(err * err)
        _acc(i, loss_ref, jnp.broadcast_to(jnp.sum(row_loss, axis=0, keepdims=True), (1, LANES)))
        dz, dg, db = _ln_bwd(err * (1.0 / D), xhat, rstd, g_ref[...])
        _acc(i, dg_ref, dg)
        _acc(i, db_ref, db)
        dgp = _bf(dz * pp * sg * (1.0 - sg))
        dgp_ref[...] = dgp
        dpp_ref[...] = _bf(dz * sg)
        dh_ref[...] = ALPHA * dz + _dot_nt(dgp, wg_ref[...])

    return _rowcall("ple_loss", body, T, tm, [h3, p, target], [w_gate, w_proj, ln_g, ln_b],
                    [(D, F32), (D, BF16), (D, BF16)], [((1, LANES), F32), ((1, D), F32), ((1, D), F32)])


def _ewise(name, fn, ins, n_out, out_dtype=F32):
    r, c = ins[0].shape
    tr = _tile(r, max(8, (1 << 19) // c // 8 * 8), 8)

    def kern(*refs):
        outs = fn(*[x[...] for x in refs[:len(ins)]])
        for o_ref, o in zip(refs[len(ins):], outs):
            o_ref[...] = o.astype(out_dtype)

    spec = pl.BlockSpec((tr, c), lambda i: (i, 0))
    return pl.pallas_call(kern, grid=(r // tr,), in_specs=[spec] * len(ins), out_specs=[spec] * n_out,
                          out_shape=[jax.ShapeDtypeStruct((r, c), out_dtype)] * n_out, name=name,
                          compiler_params=_params(("arbitrary",)))(*ins)


def _adamw_math(w, g, m, v):
    m = ADAM_B1 * m + (1.0 - ADAM_B1) * g
    v = ADAM_B2 * v + (1.0 - ADAM_B2) * (g * g)
    m_hat = m / (1.0 - ADAM_B1 ** ADAM_STEP)
    v_hat = v / (1.0 - ADAM_B2 ** ADAM_STEP)
    return -ADAM_LR * (m_hat / (jnp.sqrt(v_hat) + ADAM_EPS) + ADAM_WD * w), m, v


def _adamw(name, w, g, m, v):
    shape = w.shape
    c = shape[-1]
    flat = [t.reshape(-1, c) for t in (w, g, m, v)]
    return [t.reshape(shape) for t in _ewise(name, _adamw_math, flat, 3)]


def _place():
    return lax.axis_index("x"), lax.axis_index("y"), lax.axis_index("c")


def _dma_sems(n):
    return [pltpu.SemaphoreType.DMA((n,)), pltpu.SemaphoreType.DMA((n,))]


N_PEER_CHIPS = N_CHIPS - 1


def _chips_exchange(name, srcs, broadcast):
    n = len(srcs)

    def kern(*refs):
        cps = _chip_copies(refs[:n], refs[n:2 * n], refs[2 * n], refs[2 * n + 1], broadcast)
        for send, _ in cps:
            send.start()
        _wait_copies(cps)

    return pl.pallas_call(
        kern, out_shape=_exchange_shapes(srcs), in_specs=[HBM_SPEC] * n, out_specs=[HBM_SPEC] * n,
        scratch_shapes=_dma_sems(n * N_PEER_CHIPS), name=name)(*srcs)


def _exchange_shapes(srcs):
    return [jax.ShapeDtypeStruct((N_CHIPS,) + s.shape[1:], s.dtype) for s in srcs]


def _chip_copies(src_refs, out_refs, send_sems, recv_sems, broadcast):
    x, y, c = _place()
    me = 2 * x + y
    peers = [(1 - x, y), (x, 1 - y), (1 - x, 1 - y)]
    cps = []
    for j, (px, py) in enumerate(peers):
        for a, (src_ref, out_ref) in enumerate(zip(src_refs, out_refs)):
            piece = src_ref.at[c] if broadcast else src_ref.at[2 * px + py]

            def copy(slot):
                return pltpu.make_async_remote_copy(
                    src_ref=piece, dst_ref=out_ref.at[slot], send_sem=send_sems.at[a * N_PEER_CHIPS + j],
                    recv_sem=recv_sems.at[a * N_PEER_CHIPS + j], device_id=(px, py, c), device_id_type=MESH)

            cps.append((copy(me), copy(2 * px + py)))
    return cps


def _wait_copies(cps):
    for _, landing in cps:
        landing.wait_recv()
    for send, _ in cps:
        send.wait_send()


def _sibling_swap(name, srcs, halves):
    n = len(srcs)

    def kern(*refs):
        src_refs, out_refs = refs[:n], refs[n:2 * n]
        send_sems, recv_sems = refs[2 * n:]
        x, y, c = _place()

        def copy(a):
            piece = src_refs[a].at[:, 1 - c] if halves else src_refs[a]
            return pltpu.make_async_remote_copy(
                src_ref=piece, dst_ref=out_refs[a], send_sem=send_sems.at[a], recv_sem=recv_sems.at[a],
                device_id=(x, y, 1 - c), device_id_type=MESH)

        cps = [copy(a) for a in range(n)]
        for cp in cps:
            cp.start()
        for cp in cps:
            cp.wait_recv()
        for cp in cps:
            cp.wait_send()

    def out_shape(s):
        return jax.ShapeDtypeStruct((s.shape[0],) + s.shape[2:] if halves else s.shape, s.dtype)

    return pl.pallas_call(
        kern, out_shape=[out_shape(s) for s in srcs], in_specs=[HBM_SPEC] * n, out_specs=[HBM_SPEC] * n,
        scratch_shapes=_dma_sems(n), name=name)(*srcs)


def _all_devices(name, src, reduce):
    r, c = src.shape
    n_dev = 2 * N_CHIPS

    def kern(src_ref, out_ref, *scratch):
        if reduce:
            gat_ref, send_sems, recv_sems = scratch
        else:
            gat_ref = out_ref
            send_sems, recv_sems = scratch
        x, y, cc = _place()
        me = 4 * x + 2 * y + cc
        gat_ref[me] = src_ref[...]
        peers = []
        for j in range(1, n_dev):
            px = 1 - x if j & 4 else x
            py = 1 - y if j & 2 else y
            pc = 1 - cc if j & 1 else cc
            peers.append((px, py, pc))

        def copy(j, peer, slot):
            return pltpu.make_async_remote_copy(
                src_ref=src_ref, dst_ref=gat_ref.at[slot], send_sem=send_sems.at[j], recv_sem=recv_sems.at[j],
                device_id=peer, device_id_type=MESH)

        sends = [copy(j, peer, me) for j, peer in enumerate(peers)]
        for cp in sends:
            cp.start()
        for j, (px, py, pc) in enumerate(peers):
            copy(j, (px, py, pc), 4 * px + 2 * py + pc).wait_recv()
        for cp in sends:
            cp.wait_send()
        if reduce:
            total = gat_ref[0]
            for d in range(1, n_dev):
                total = total + gat_ref[d]
            out_ref[...] = total

    out_shape = jax.ShapeDtypeStruct((r, c) if reduce else (n_dev, r, c), src.dtype)
    scratch = ([pltpu.VMEM((n_dev, r, c), src.dtype)] if reduce else []) + _dma_sems(n_dev - 1)
    return pl.pallas_call(kern, out_shape=out_shape, in_specs=[VMEM_SPEC], out_specs=VMEM_SPEC,
                          scratch_shapes=scratch, name=name)(src)


def _halves(t, axis):
    return t.reshape(t.shape[:axis] + (2, t.shape[axis] // 2) + t.shape[axis + 1:])


def _by_core(mine, theirs, axis):
    c = lax.axis_index("c")
    both = jnp.where(c == 0, jnp.stack([mine, theirs], axis), jnp.stack([theirs, mine], axis))
    return both.reshape(both.shape[:axis] + (2 * both.shape[axis + 1],) + both.shape[axis + 2:])


def _with_own(own, others):
    me = 2 * lax.axis_index("x") + lax.axis_index("y")
    is_me = (jnp.arange(N_CHIPS, dtype=jnp.int32) == me)[:, None, None]
    return jnp.where(is_me, own[None], others)


def _join_shards(name, shards):
    _, r, c = shards.shape
    if name in COL_SHARDED:
        return shards.transpose(1, 0, 2).reshape(r, N_CHIPS * c)
    return shards.reshape(N_CHIPS * r, c)


def _split_shards(name, full):
    if full.ndim == 3:
        return full
    r, c = full.shape
    if name in COL_SHARDED:
        return jnp.stack([full[:, k * (c // N_CHIPS):(k + 1) * (c // N_CHIPS)] for k in range(N_CHIPS)])
    return full.reshape(N_CHIPS, r // N_CHIPS, c)


def _rope_tables(positions):
    pos = positions.reshape(-1).astype(F32)[:, None]
    half = RET_DK // 2
    ang = pos * (ROPE_BASE ** (-jnp.arange(half, dtype=F32) / half))
    cos_r = jnp.concatenate([jnp.cos(ang)] * 2, axis=1)
    sin_r = jnp.concatenate([-jnp.sin(ang), jnp.sin(ang)], axis=1)
    half = MLA_ROPE // 2
    ang = pos * (ROPE_BASE ** (-jnp.arange(half, dtype=F32) / half))
    zeros = jnp.zeros_like(ang)
    rest = LANES - MLA_ROPE
    c = jnp.concatenate([jnp.cos(ang)] * 2 + [jnp.ones((ang.shape[0], rest), F32)], axis=1)
    s1 = jnp.concatenate([-jnp.sin(ang), zeros, jnp.zeros((ang.shape[0], rest), F32)], axis=1)
    s2 = jnp.concatenate([zeros, jnp.sin(ang), jnp.zeros((ang.shape[0], rest), F32)], axis=1)
    return cos_r, sin_r, (c, s1, s2)


GATHER_GROUPS = (("ffn1_w_in", "ffn1_w_out"), ("w_in", "w_uq", "w_ukv"),
                 ("w_ret_o", "w_mla_o", "w_out", "ffn2_w_in", "ffn2_w_out", "ple_w_gate", "ple_w_proj"))
REDUCE_GROUPS = (("ple_w_gate", "ple_w_proj", "ffn2_w_in", "ffn2_w_out"),
                 ("w_out", "w_ret_o", "w_mla_o", "w_uq", "w_ukv", "w_in"), ("ffn1_w_in", "ffn1_w_out"))


def _gathered(tag, names, own, mine):
    theirs = _sibling_swap("gather_cores_" + tag, mine, False)
    out = {}
    for n, m, t in zip(names, mine, theirs):
        full = _with_own(own[n], _by_core(m, t, 1))
        out[n] = full if n in ("ffn1_w_in", "ffn2_w_in") else _join_shards(n, full)
    return out


def _chip_sums(tag, names, grads):
    halves = [_halves(_split_shards(n, grads[n]), 1) for n in names]
    theirs = _sibling_swap("reduce_cores_" + tag, halves, True)

    def one(n, g, t):
        k, _, r, c = g.shape
        tr = _tile(r, max(8, (1 << 17) // c // 8 * 8), 8)

        def kern(g_ref, t_ref, o_ref):
            mine = jnp.where(lax.axis_index("c") == 0, g_ref[:, 0], g_ref[:, 1])
            o_ref[...] = _bf(mine.astype(F32) + t_ref[...].astype(F32))

        spec = pl.BlockSpec((k, tr, c), lambda i: (0, i, 0))
        return pl.pallas_call(kern, grid=(r // tr,),
                              in_specs=[pl.BlockSpec((k, 2, tr, c), lambda i: (0, 0, i, 0)), spec], out_specs=spec,
                              out_shape=jax.ShapeDtypeStruct((k, r, c), BF16), name="reduce_cores_add_" + n,
                              compiler_params=_params(("arbitrary",)))(g, t)

    return [one(n, g, t) for n, g, t in zip(names, halves, theirs)]


def _block_totals(names, sums, parts):
    def one(n, s, pt):
        _, r, c = s.shape
        tr = _tile(r, max(8, (1 << 17) // c // 8 * 8), 8)

        def kern(s_ref, p_ref, o_ref):
            me = 2 * lax.axis_index("x") + lax.axis_index("y")
            terms = [jnp.where(k == me, s_ref[k], p_ref[k]).astype(F32) for k in range(N_CHIPS)]
            o_ref[...] = ((terms[0] + terms[1]) + terms[2]) + terms[3]

        spec = pl.BlockSpec((N_CHIPS, tr, c), lambda i: (0, i, 0))
        return pl.pallas_call(kern, grid=(r // tr,), in_specs=[spec, spec],
                              out_specs=pl.BlockSpec((tr, c), lambda i: (i, 0)),
                              out_shape=jax.ShapeDtypeStruct((r, c), F32), name="reduce_chips_add_" + n,
                              compiler_params=_params(("arbitrary",)))(s, pt)

    return [one(n, s, pt) for n, s, pt in zip(names, sums, parts)]


def _local_step(x, p, positions, target, shards, ln_g, ln_b, gn_g, qn_g, kvn_g):
    T, D = x.shape
    tm = min(256, T)
    H = MLA_HEADS
    qk, rv = RET_HEADS * RET_DK, RET_HEADS * RET_DV
    cos_r, sin_r, tabs = _rope_tables(positions)
    lgam = jnp.broadcast_to(jnp.log(1.0 - 2.0 ** (-5.0 - jnp.arange(RET_HEADS, dtype=F32)))[:, None, None],
                            (RET_HEADS, 1, LANES))
    lng = [ln_g[k:k + 1] for k in range(N_LN)]
    lnb = [ln_b[k:k + 1] for k in range(N_LN)]
    own = {n: _bf(shards[n]) for n in BIG_WEIGHTS}
    to_send = [[_halves(own[n], 0) for n in names] for names in GATHER_GROUPS]

    w = _gathered("a", GATHER_GROUPS[0], own, _chips_exchange("gather_chips_a", to_send[0], True))
    h1, z0, a1, *arrived = _ffn_fwd("ffn1_fwd", x, w["ffn1_w_in"], w["ffn1_w_out"], lng[0], lnb[0], tm,
                                    exchange=(to_send[1], True))
    w.update(_gathered("b", GATHER_GROUPS[1], own, arrived))

    w_in = w["w_in"]
    o_lat, o_kpe, o_gate = 2 * qk + 2 * rv, 2 * qk + 2 * rv + Q_LORA + KV_LORA, 2 * qk + 2 * rv + Q_LORA + KV_LORA + MLA_ROPE
    w_r, w_c = w_in[:, :o_lat], w_in[:, o_lat:o_kpe]
    w_kpe = jnp.pad(w_in[:, o_kpe:o_gate], ((0, 0), (0, LANES - MLA_ROPE)))
    w_g = w_in[:, o_gate:]
    w_uq = jnp.pad(w["w_uq"].reshape(Q_LORA, H, MLA_NOPE + MLA_ROPE),
                   ((0, 0), (0, 0), (0, MLA_QK - MLA_NOPE - MLA_ROPE))).reshape(Q_LORA, H * MLA_QK)
    w_ukv = w["w_ukv"].reshape(KV_LORA, H, MLA_NOPE + MLA_DV)
    w_uk = w_ukv[:, :, :MLA_NOPE].reshape(KV_LORA, H * MLA_NOPE)
    w_uv = w_ukv[:, :, MLA_NOPE:].reshape(KV_LORA, H * MLA_DV)

    rq, rk, rvv, rg = _proj_ret(h1, w_r, cos_r, sin_r, tm)
    lat, gates, q, k, v, latn, qt, kt, vt = _proj_mla(h1, tabs, w_c, w_kpe, w_g, w_uq, w_uk, w_uv, qn_g, kvn_g, tm)
    y = _ret_fwd(rq, rk, rvv, lgam)
    o, lse_rows, *arrived = _attn_fwd(k, qt, vt, exchange=to_send[2])
    w.update(_gathered("c", GATHER_GROUPS[2], own, arrived))
    h2, z1, yret, ymla, yr, mix = _mix_fwd(y, rg, o, gates, h1, gn_g, w["w_ret_o"], w["w_mla_o"], w["w_out"],
                                           lng[1], lnb[1], tm)
    h3, z2, a2 = _ffn_fwd("ffn2_fwd", h2, w["ffn2_w_in"], w["ffn2_w_out"], lng[2], lnb[2], tm)

    dh3, dgp, dpp, loss, dg3, db3 = _ple_loss(h3, p, target, w["ple_w_gate"], w["ple_w_proj"], lng[3], lnb[3], tm)
    dh2, da2, s2, df2, dg2, db2 = _ffn_bwd("ffn2_bwd", dh3, z2, a2, w["ffn2_w_in"], w["ffn2_w_out"], lng[2], tm)
    grads = {"ple_w_gate": _mm_tn("wg_ple_gate", h3, dgp), "ple_w_proj": _mm_tn("wg_ple_proj", p, dpp),
             "ffn2_w_in": _mm_tn("wg_ffn2_in", h2, da2, n_split=N_CHIPS), "ffn2_w_out": _mm_tn("wg_ffn2_out", s2, df2)}
    sums1 = _chip_sums("1", REDUCE_GROUPS[0], grads)
    (dz1, dgates, drg, dy, do, dyret, dymla, dg1, db1, dgn, dot_, delta_rows, *parts1) = _mix_bwd(
        dh2, z1, gates, yret, ymla, y, rg, o, gn_g, w["w_ret_o"], w["w_mla_o"], w["w_out"], lng[1], tm,
        exchange=(sums1, False))
    drq = _ret_bwd_q(rq, rk, rvv, dy, lgam)
    drk, drv = _ret_bwd_kv(rq, rk, rvv, dy, lgam)
    dk, dv, dqt = _attn_bwd(q, k, v, do, qt, kt, dot_, lse_rows, delta_rows)
    dlat, dkpe, dqb, dkv, dqg, dkg = _proj_mla_bwd(dqt, dk, dv, lat, tabs, w_uq, w_uk, w_uv, qn_g, kvn_g, tm)
    dh1, dpr = _proj_bwd(drq, drk, drv, drg, dz1, dlat, dkpe, dgates, cos_r, sin_r, w_r, w_c, w_kpe, w_g, tm)
    g_uq = _mm_tn("wg_uq", latn[:, :Q_LORA], dqb).reshape(Q_LORA, H, MLA_QK)[:, :, :MLA_NOPE + MLA_ROPE]
    g_ukv = _mm_tn("wg_ukv", latn[:, Q_LORA:], dkv)
    g_uk = g_ukv[:, :H * MLA_NOPE].reshape(KV_LORA, H, MLA_NOPE)
    g_uv = g_ukv[:, H * MLA_NOPE:].reshape(KV_LORA, H, MLA_DV)
    g_in = _mm_tn("wg_in", h1, dpr)
    grads.update({
        "w_in": jnp.concatenate([g_in[:, :o_kpe + MLA_ROPE], g_in[:, o_kpe + LANES:o_kpe + LANES + 2 * D]], axis=1),
        "w_ret_o": _mm_tn("wg_ret_o", yr, dyret),
        "w_uq": g_uq.reshape(Q_LORA, H * (MLA_NOPE + MLA_ROPE)),
        "w_ukv": jnp.concatenate([g_uk, g_uv], axis=2).reshape(KV_LORA, H * (MLA_NOPE + MLA_DV)),
        "w_mla_o": _mm_tn("wg_mla_o", o, dymla),
        "w_out": _mm_tn("wg_out", mix, dz1)})
    sums2 = _chip_sums("2", REDUCE_GROUPS[1], grads)
    dx, da1, s1, df1, dg0, db0, *parts2 = _ffn_bwd("ffn1_bwd", dh1, z0, a1, w["ffn1_w_in"], w["ffn1_w_out"], lng[0], tm,
                                                   exchange=(sums2, False))
    grads.update({"ffn1_w_in": _mm_tn("wg_ffn1_in", x, da1, n_split=N_CHIPS),
                  "ffn1_w_out": _mm_tn("wg_ffn1_out", s1, df1)})
    sums3 = _chip_sums("3", REDUCE_GROUPS[2], grads)
    parts3 = _chips_exchange("reduce_chips_3", sums3, False)

    names = [n for group in REDUCE_GROUPS for n in group]
    totals = _block_totals(names, sums1 + sums2 + sums3, list(parts1) + list(parts2) + list(parts3))
    others = _sibling_swap("reduce_join", totals, False)
    reduced = {n: _by_core(t, o_, 0) for n, t, o_ in zip(names, totals, others)}
    small = {"ln_g": jnp.concatenate([dg0, dg1, dg2, dg3], axis=0), "ln_b": jnp.concatenate([db0, db1, db2, db3], axis=0),
             "ret_gn_g": dgn, "q_norm_g": dqg, "kv_norm_g": dkg}
    return loss[0, 0], dx, reduced, small


def kernel(x, p, positions, ln_g, ln_b, ffn1_w_in, ffn1_w_out, w_in, ret_gn_g, w_ret_o, q_norm_g, kv_norm_g, w_uq, w_ukv, w_mla_o, w_out, ffn2_w_in, ffn2_w_out, ple_w_gate, ple_w_proj, loss_target, m_ln_g, m_ln_b, m_ffn1_w_in, m_ffn1_w_out, m_w_in, m_ret_gn_g, m_w_ret_o, m_q_norm_g, m_kv_norm_g, m_w_uq, m_w_ukv, m_w_mla_o, m_w_out, m_ffn2_w_in, m_ffn2_w_out, m_ple_w_gate, m_ple_w_proj, v_ln_g, v_ln_b, v_ffn1_w_in, v_ffn1_w_out, v_w_in, v_ret_gn_g, v_w_ret_o, v_q_norm_g, v_kv_norm_g, v_w_uq, v_w_ukv, v_w_mla_o, v_w_out, v_ffn2_w_in, v_ffn2_w_out, v_ple_w_gate, v_ple_w_proj):
    names = ("ln_g", "ln_b", "ffn1_w_in", "ffn1_w_out", "w_in", "ret_gn_g", "w_ret_o", "q_norm_g", "kv_norm_g", "w_uq",
             "w_ukv", "w_mla_o", "w_out", "ffn2_w_in", "ffn2_w_out", "ple_w_gate", "ple_w_proj")
    weights = dict(zip(names, (ln_g, ln_b, ffn1_w_in, ffn1_w_out, w_in, ret_gn_g, w_ret_o, q_norm_g, kv_norm_g, w_uq,
                               w_ukv, w_mla_o, w_out, ffn2_w_in, ffn2_w_out, ple_w_gate, ple_w_proj)))
    m_in = dict(zip(names, (m_ln_g, m_ln_b, m_ffn1_w_in, m_ffn1_w_out, m_w_in, m_ret_gn_g, m_w_ret_o, m_q_norm_g,
                            m_kv_norm_g, m_w_uq, m_w_ukv, m_w_mla_o, m_w_out, m_ffn2_w_in, m_ffn2_w_out, m_ple_w_gate,
                            m_ple_w_proj)))
    v_in = dict(zip(names, (v_ln_g, v_ln_b, v_ffn1_w_in, v_ffn1_w_out, v_w_in, v_ret_gn_g, v_w_ret_o, v_q_norm_g,
                            v_kv_norm_g, v_w_uq, v_w_ukv, v_w_mla_o, v_w_out, v_ffn2_w_in, v_ffn2_w_out, v_ple_w_gate,
                            v_ple_w_proj)))
    chip = 2 * lax.axis_index("x") + lax.axis_index("y")
    D = x.shape[-1]
    dq = D // N_CHIPS

    shards = {n: weights[n][0] for n in BIG_WEIGHTS}
    ln_all = _all_devices("gather_ln", jnp.concatenate([ln_g[0], ln_b[0]], axis=0), False)
    ln_full = ln_all[::2].transpose(1, 0, 2).reshape(2 * N_LN, D)
    loss, dx, big, small = _local_step(x[0], p[0, 0], positions, loss_target[0], shards, ln_full[:N_LN],
                                       ln_full[N_LN:], ret_gn_g, q_norm_g, kv_norm_g)

    loss = lax.psum(loss, ("x", "y", "c"))
    small_names = ("ln_g", "ln_b", "ret_gn_g", "q_norm_g", "kv_norm_g")
    flat = jnp.concatenate([small[n].reshape(-1) for n in small_names])
    rows = -(-flat.shape[0] // LANES // 8) * 8
    flat = jnp.pad(flat, (0, rows * LANES - flat.shape[0])).reshape(rows, LANES)
    flat = _all_devices("reduce_small", flat, True).reshape(-1)
    off = 0
    for n in small_names:
        size = small[n].size
        small[n] = flat[off:off + size].reshape(small[n].shape)
        off += size
    g_out = dict(big)
    for n in ("ln_g", "ln_b"):
        g_out[n] = lax.dynamic_slice_in_dim(small[n], chip * dq, dq, axis=1)
    for n in ("ret_gn_g", "q_norm_g", "kv_norm_g"):
        g_out[n] = small[n]

    deltas, new_m, new_v = {}, {}, {}
    for n in names:
        g = g_out[n].reshape(weights[n].shape)
        g_out[n] = g
        deltas[n], new_m[n], new_v[n] = _adamw("adamw_" + n, weights[n], g, m_in[n], v_in[n])
    return (loss, dx[None], *[g_out[n] for n in names], *[deltas[n] for n in names], *[new_m[n] for n in names],
            *[new_v[n] for n in names])
```

```python
import functools

import jax
import jax.numpy as jnp
from jax import lax
from jax.experimental import pallas as pl
from jax.experimental.pallas import tpu as pltpu

D_MODEL = 1024
CHUNK = 64
D_PLE = 256
D_FF = 2816
RET_HEADS = 8
RET_DK = 128
RET_DV = 256
MLA_HEADS = 8
MLA_NOPE = 128
MLA_ROPE = 64
MLA_DV = 128
MLA_QK = 256
Q_LORA = 256
KV_LORA = 256
ROPE_BASE = 10000.0
EPS = 1e-5
N_LN = 4
ALPHA = 2.0 ** 0.25
ADAM_LR = 0.001
ADAM_B1 = 0.9
ADAM_B2 = 0.999
ADAM_EPS = 1e-08
ADAM_WD = 0.01
ADAM_STEP = 10

LANES = 128
VMEM_LIMIT = 60 << 20
N_CHIPS = 4

F32 = jnp.float32
BF16 = jnp.bfloat16
MESH = pl.DeviceIdType.MESH
HBM_SPEC = pl.BlockSpec(memory_space=pltpu.HBM)
VMEM_SPEC = pl.BlockSpec(memory_space=pltpu.VMEM)

BIG_WEIGHTS = ("ffn1_w_in", "ffn1_w_out", "w_in", "w_ret_o", "w_uq", "w_ukv", "w_mla_o", "w_out",
               "ffn2_w_in", "ffn2_w_out", "ple_w_gate", "ple_w_proj")
COL_SHARDED = ("ffn1_w_in", "w_in", "w_uq", "w_ukv", "ffn2_w_in", "ple_w_proj")


def _dot(a, b):
    return jnp.dot(a, b, preferred_element_type=F32)


def _dot_nt(a, b):
    return lax.dot_general(a, b, (((1,), (1,)), ((), ())), preferred_element_type=F32)


def _dot_tn(a, b):
    return lax.dot_general(a, b, (((0,), (0,)), ((), ())), preferred_element_type=F32)


def _bf(x):
    return x.astype(BF16)


def _sigmoid(x):
    return 0.5 * jnp.tanh(0.5 * x) + 0.5


def _mean(x):
    return jnp.mean(x, axis=-1, keepdims=True)


def _ln_stats(z):
    zc = z - _mean(z)
    rstd = lax.rsqrt(_mean(zc * zc) + EPS)
    return zc * rstd, rstd


def _ln_bwd(dy, xhat, rstd, g):
    dxhat = dy * g
    dz = rstd * (dxhat - _mean(dxhat) - xhat * _mean(dxhat * xhat))
    return dz, jnp.sum(dy * xhat, axis=0, keepdims=True), jnp.sum(dy, axis=0, keepdims=True)


def _roll(x, shift):
    return pltpu.roll(x, shift, 1)


def _chunk_of(idx):
    return jnp.right_shift(idx, CHUNK.bit_length() - 1)


def _tile(n, cap, mult=LANES):
    if n <= cap:
        return n
    for t in range(cap - cap % mult, 0, -mult):
        if n % t == 0:
            return t
    return n


def _zero_map(nd, *_):
    return (0,) * nd


def _params(sem):
    return pltpu.CompilerParams(dimension_semantics=sem, vmem_limit_bytes=VMEM_LIMIT)


def _rowcall(name, body, n_rows, tm, row_ins, full_ins, row_outs, acc_outs=(), tiled_outs=(), tiled_ins=(),
             exchange=None):
    n_steps = n_rows // tm
    ex_srcs, broadcast = exchange if exchange else ((), False)
    n_ex = len(ex_srcs)
    n_in = len(row_ins) + len(tiled_ins) + len(full_ins)
    n_out = len(row_outs) + len(acc_outs) + len(tiled_outs)

    def kern(*refs):
        step = pl.program_id(0)
        ex_in, ex_out = refs[n_in:n_in + n_ex], refs[n_in + n_ex + n_out:n_in + 2 * n_ex + n_out]
        sems = refs[n_in + 2 * n_ex + n_out:]
        if n_ex:
            @pl.when(step == 0)
            def _():
                for send, _ in _chip_copies(ex_in, ex_out, *sems, broadcast):
                    send.start()

        body(step, *refs[:n_in], *refs[n_in + n_ex:n_in + n_ex + n_out])
        if n_ex:
            @pl.when(step == n_steps - 1)
            def _():
                _wait_copies(_chip_copies(ex_in, ex_out, *sems, broadcast))

    in_specs = [pl.BlockSpec((tm, a.shape[1]), lambda i: (i, 0)) for a in row_ins]
    in_specs += [spec for (_, spec) in tiled_ins]
    row_ins = list(row_ins) + [a for (a, _) in tiled_ins]
    in_specs += [pl.BlockSpec(a.shape, functools.partial(_zero_map, a.ndim), pipeline_mode=pl.Buffered(1))
                 for a in full_ins]
    in_specs += [HBM_SPEC] * n_ex
    out_specs = [pl.BlockSpec((tm, w), lambda i: (i, 0)) for (w, _) in row_outs]
    out_specs += [pl.BlockSpec(s, functools.partial(_zero_map, len(s))) for (s, _) in acc_outs]
    out_specs += [spec for (_, spec) in tiled_outs]
    out_specs += [HBM_SPEC] * n_ex
    out_shape = [jax.ShapeDtypeStruct((n_rows, w), dt) for (w, dt) in row_outs]
    out_shape += [jax.ShapeDtypeStruct(s, dt) for (s, dt) in acc_outs]
    out_shape += [shape for (shape, _) in tiled_outs]
    out_shape += _exchange_shapes(ex_srcs)
    return pl.pallas_call(kern, grid=(n_steps,), in_specs=in_specs, out_specs=out_specs, out_shape=out_shape,
                          scratch_shapes=_dma_sems(n_ex * N_PEER_CHIPS) if n_ex else [], name=name,
                          compiler_params=_params(("arbitrary",)))(*row_ins, *full_ins, *ex_srcs)


def _acc(step, ref, val):
    @pl.when(step == 0)
    def _():
        ref[...] = val

    @pl.when(step != 0)
    def _():
        ref[...] += val


def _ffn_fwd(name, x, w_in4, w_out, ln_g, ln_b, tm, exchange=None):
    T, D = x.shape
    fh = w_in4.shape[2]

    def body(i, x_ref, w4_ref, wo_ref, g_ref, b_ref, h_ref, z_ref, a_ref):
        xv = x_ref[...]
        xb = _bf(xv)
        f = jnp.zeros((tm, D), F32)
        for k in range(2):
            gk = _dot(xb, w4_ref[k])
            uk = _dot(xb, w4_ref[2 + k])
            a_ref[:, k * fh:(k + 1) * fh] = _bf(gk)
            a_ref[:, (2 + k) * fh:(3 + k) * fh] = _bf(uk)
            f += _dot(_bf(gk * _sigmoid(gk) * uk), wo_ref[k * fh:(k + 1) * fh, :])
        z = ALPHA * xv + 0.5 * f
        xhat, _ = _ln_stats(z)
        z_ref[...] = z
        h_ref[...] = xhat * g_ref[...] + b_ref[...]

    return _rowcall(name, body, T, tm, [x], [w_in4, w_out, ln_g, ln_b],
                    [(D, F32), (D, F32), (4 * fh, BF16)], exchange=exchange)


def _ffn_bwd(name, dh, z, a, w_in4, w_out, ln_g, tm, exchange=None):
    T, D = dh.shape
    fh = w_in4.shape[2]

    def body(i, dh_ref, z_ref, a_ref, w4_ref, wo_ref, g_ref, dx_ref, da_ref, s_ref, df_ref, dg_ref, db_ref):
        xhat, rstd = _ln_stats(z_ref[...])
        dz, dg, db = _ln_bwd(dh_ref[...], xhat, rstd, g_ref[...])
        _acc(i, dg_ref, dg)
        _acc(i, db_ref, db)
        dfb = _bf(0.5 * dz)
        df_ref[...] = dfb
        dx = ALPHA * dz
        for k in range(2):
            gk = a_ref[:, k * fh:(k + 1) * fh].astype(F32)
            uk = a_ref[:, (2 + k) * fh:(3 + k) * fh].astype(F32)
            ds = _dot_nt(dfb, wo_ref[k * fh:(k + 1) * fh, :])
            sig = _sigmoid(gk)
            silu = gk * sig
            dgk = _bf(ds * uk * sig * (1.0 + gk * (1.0 - sig)))
            duk = _bf(ds * silu)
            s_ref[:, k * fh:(k + 1) * fh] = _bf(silu * uk)
            da_ref[:, k * fh:(k + 1) * fh] = dgk
            da_ref[:, (2 + k) * fh:(3 + k) * fh] = duk
            dx += _dot_nt(dgk, w4_ref[k]) + _dot_nt(duk, w4_ref[2 + k])
        dx_ref[...] = dx

    return _rowcall(name, body, T, tm, [dh, z, a], [w_in4, w_out, ln_g],
                    [(D, F32), (4 * fh, BF16), (2 * fh, BF16), (D, BF16)],
                    [((1, D), F32), ((1, D), F32)], exchange=exchange)


WG_TILE_N = 1536


def _mm_tn(name, a, b, out_dtype=BF16, n_split=1):
    T, M = a.shape
    N = b.shape[1]
    tk = _tile(T, 2048, 8)
    tm = _tile(M, 1408)
    tn = _tile(N // n_split, WG_TILE_N)
    per = N // n_split // tn
    nk = T // tk
    if n_split > 1:
        out_spec = pl.BlockSpec((None, tm, tn), lambda i, j, k: (j // per, i, j % per))
        out_shape = jax.ShapeDtypeStruct((n_split, M, N // n_split), out_dtype)
    else:
        out_spec = pl.BlockSpec((tm, tn), lambda i, j, k: (i, j))
        out_shape = jax.ShapeDtypeStruct((M, N), out_dtype)

    def kern(a_ref, b_ref, o_ref, acc_ref):
        k = pl.program_id(2)
        part = _dot_tn(_bf(a_ref[...]), _bf(b_ref[...]))

        @pl.when(k == 0)
        def _():
            acc_ref[...] = part

        @pl.when(k != 0)
        def _():
            acc_ref[...] += part

        @pl.when(k == nk - 1)
        def _():
            o_ref[...] = acc_ref[...].astype(out_dtype)

    return pl.pallas_call(
        kern, grid=(M // tm, N // tn, nk),
        in_specs=[pl.BlockSpec((tk, tm), lambda i, j, k: (k, i)), pl.BlockSpec((tk, tn), lambda i, j, k: (k, j))],
        out_specs=out_spec, out_shape=out_shape,
        scratch_shapes=[pltpu.VMEM((tm, tn), F32)], name=name,
        compiler_params=_params(("arbitrary", "arbitrary", "arbitrary")))(a, b)


def _proj_ret(h1, w_r, cos_r, sin_r, tm):
    T, D = h1.shape
    qk = RET_HEADS * RET_DK
    rv = RET_HEADS * RET_DV

    def body(i, h_ref, cos_ref, sin_ref, w_ref, q_ref, k_ref, v_ref, g_ref):
        hb = _bf(h_ref[...])
        cos, sin = cos_ref[...], sin_ref[...]
        for out_ref, off, scale in ((q_ref, 0, 1.0), (k_ref, qk, RET_DK ** -0.5)):
            pr = _dot(hb, w_ref[:, off:off + qk])
            for h in range(RET_HEADS):
                t = pr[:, h * RET_DK:(h + 1) * RET_DK]
                out_ref[:, h * RET_DK:(h + 1) * RET_DK] = _bf((t * cos + _roll(t, RET_DK // 2) * sin) * scale)
        v_ref[...] = _bf(_dot(hb, w_ref[:, 2 * qk:2 * qk + rv]))
        g_ref[...] = _bf(_dot(hb, w_ref[:, 2 * qk + rv:2 * qk + 2 * rv]))

    return _rowcall("proj_ret", body, T, tm, [h1, cos_r, sin_r], [w_r],
                    [(qk, BF16), (qk, BF16), (rv, BF16), (rv, BF16)])


def _rope_pe(t, c, s1, s2):
    return t * c + _roll(t, LANES - MLA_ROPE // 2) * s1 + _roll(t, MLA_ROPE // 2) * s2


def _rope_pe_bwd(dy, c, s1, s2):
    return dy * c + _roll(dy * s1, MLA_ROPE // 2) + _roll(dy * s2, LANES - MLA_ROPE // 2)


def _rms(x, g):
    r = lax.rsqrt(_mean(x * x) + EPS)
    return x * r, r


def _attn_block(T):
    return min(512, T)


def _transposed_blocks(T, tm, w, dtype):
    tb = _attn_block(T)
    per = tb // tm
    return (jax.ShapeDtypeStruct((T // tb, MLA_HEADS, w, tb), dtype),
            pl.BlockSpec((None, MLA_HEADS, w, tm), lambda i: (i // per, 0, 0, i % per)))


ATTN_SCALE = (MLA_NOPE + MLA_ROPE) ** -0.5
LOG2E = 1.4426950408889634
Q_PRESCALE = ATTN_SCALE * LOG2E
V_ONES = 16


def _proj_mla(h1, tabs, w_c, w_kpe, w_g, w_uq, w_uk, w_uv, qn_g, kvn_g, tm):
    T, D = h1.shape
    H = MLA_HEADS

    def body(i, h_ref, c_ref, s1_ref, s2_ref, wc_ref, wk_ref, wg_ref, wuq_ref, wuk_ref, wuv_ref, qg_ref, kg_ref,
             lat_ref, gt_ref, q_ref, k_ref, v_ref, ln_ref, qt_ref, kt_ref, vt_ref):
        hb = _bf(h_ref[...])
        c, s1, s2 = c_ref[...], s1_ref[...], s2_ref[...]
        lat = _dot(hb, wc_ref[...])
        lat_ref[...] = lat
        gt_ref[...] = _bf(_dot(hb, wg_ref[...]))
        cqn, _ = _rms(lat[:, :Q_LORA], None)
        ckn, _ = _rms(lat[:, Q_LORA:], None)
        cqn = _bf(cqn * qg_ref[...])
        ckn = _bf(ckn * kg_ref[...])
        ln_ref[:, :Q_LORA] = cqn
        ln_ref[:, Q_LORA:] = ckn
        q = _dot(cqn, wuq_ref[...])
        kn = _dot(ckn, wuk_ref[...])
        vv = _dot(ckn, wuv_ref[...])
        v_ref[...] = _bf(vv)
        kpe = _rope_pe(_dot(hb, wk_ref[...]), c, s1, s2)
        ones = jnp.ones((V_ONES, tm), BF16)
        for h in range(H):
            o = h * MLA_QK
            qh = jnp.concatenate([q[:, o:o + MLA_NOPE], _rope_pe(q[:, o + MLA_NOPE:o + MLA_QK], c, s1, s2)], axis=1)
            qh = qh * Q_PRESCALE
            kh = jnp.concatenate([kn[:, h * MLA_NOPE:(h + 1) * MLA_NOPE], kpe], axis=1)
            q_ref[:, o:o + MLA_QK] = _bf(qh)
            k_ref[:, o:o + MLA_QK] = _bf(kh)
            qt_ref[h] = _bf(qh.T)
            kt_ref[h] = _bf(kh.T)
            vt_ref[h] = jnp.concatenate([_bf(vv[:, h * MLA_DV:(h + 1) * MLA_DV].T), ones], axis=0)

    lat_w = Q_LORA + KV_LORA
    return _rowcall("proj_mla", body, T, tm, [h1, *tabs], [w_c, w_kpe, w_g, w_uq, w_uk, w_uv, qn_g, kvn_g],
                    [(lat_w, F32), (2 * D, BF16), (H * MLA_QK, BF16), (H * MLA_QK, BF16), (H * MLA_DV, BF16),
                     (lat_w, BF16)],
                    tiled_outs=[_transposed_blocks(T, tm, MLA_QK, BF16), _transposed_blocks(T, tm, MLA_QK, BF16),
                                _transposed_blocks(T, tm, MLA_DV + V_ONES, BF16)])


def _ret_block(T):
    return min(256, T)


RET_HEADS_PER_STEP = 8


def _ret_dmat(lg, bt):
    n = lax.broadcasted_iota(jnp.int32, (bt, bt), 0)
    m = lax.broadcasted_iota(jnp.int32, (bt, bt), 1)
    return jnp.where(_chunk_of(m) <= _chunk_of(n), jnp.exp(lg * jnp.abs(n - m).astype(F32)), 0.0)


def _ret_scan(name, per_head, lgam, ins, outs, rev):
    T = ins[0][0].shape[0]
    bt = _ret_block(T)
    nb = T // bt
    hps = min(RET_HEADS_PER_STEP, RET_HEADS)
    n_in, n_out = len(ins), len(outs)

    def kern(lg_ref, *refs):
        in_refs, out_refs = refs[:n_in], refs[n_in:n_in + n_out]
        state_ref, dmat_ref = refs[n_in + n_out:]

        @pl.when(pl.program_id(1) == 0)
        def _():
            state_ref[...] = jnp.zeros_like(state_ref)
            for hh in range(hps):
                dmat_ref[hh] = _ret_dmat(lg_ref[hh][:, :1], bt)

        pos = lax.broadcasted_iota(jnp.int32, (bt, 1), 0).astype(F32)
        for hh in range(hps):
            lg = lg_ref[hh][:, :1]
            xi, zeta, gb = jnp.exp(lg * (pos + 1.0)), jnp.exp(lg * (bt - 1.0 - pos)), jnp.exp(lg * bt)
            tiles = [r[:, hh * w:(hh + 1) * w] for r, (_, w) in zip(in_refs, ins)]
            res = per_head(dmat_ref[hh], xi, zeta, gb, state_ref.at[hh], *tiles)
            for o_ref, (w, _), val in zip(out_refs, outs, res):
                o_ref[:, hh * w:(hh + 1) * w] = val.astype(o_ref.dtype)

    def blk(w):
        if rev:
            return pl.BlockSpec((bt, hps * w), lambda g, b: (nb - 1 - b, g))
        return pl.BlockSpec((bt, hps * w), lambda g, b: (b, g))

    return pl.pallas_call(
        kern, grid=(RET_HEADS // hps, nb),
        in_specs=[pl.BlockSpec((hps, 1, LANES), lambda g, b: (g, 0, 0))] + [blk(w) for _, w in ins],
        out_specs=[blk(w) for w, _ in outs],
        out_shape=[jax.ShapeDtypeStruct((T, RET_HEADS * w), dt) for w, dt in outs],
        scratch_shapes=[pltpu.VMEM((hps, RET_DK, RET_DV), F32), pltpu.VMEM((hps, bt, bt), F32)], name=name,
        compiler_params=_params(("arbitrary", "arbitrary")))(lgam, *[a for a, _ in ins])


def _ret_fwd(rq, rk, rv, lgam):
    def per_head(dmat, xi, zeta, gb, s_ref, q, k, v):
        sc = _dot_nt(q, k) * dmat
        y = _dot(_bf(sc), v) + _dot(q, _bf(s_ref[...])) * xi
        s_ref[...] = s_ref[...] * gb + _dot_tn(_bf(k.astype(F32) * zeta), v)
        return (y,)

    return _ret_scan("ret_fwd", per_head, lgam, [(rq, RET_DK), (rk, RET_DK), (rv, RET_DV)], [(RET_DV, BF16)], False)[0]


def _ret_bwd_q(rq, rk, rv, dy, lgam):
    def per_head(dmat, xi, zeta, gb, s_ref, k, v, dy):
        dp = _dot_nt(dy, v) * dmat
        dq = _dot(_bf(dp), k) + _dot_nt(dy, _bf(s_ref[...])) * xi
        s_ref[...] = s_ref[...] * gb + _dot_tn(_bf(k.astype(F32) * zeta), v)
        return (dq,)

    return _ret_scan("ret_bwd_q", per_head, lgam, [(rk, RET_DK), (rv, RET_DV), (dy, RET_DV)], [(RET_DK, F32)], False)[0]


def _ret_bwd_kv(rq, rk, rv, dy, lgam):
    def per_head(dmat, xi, zeta, gb, g_ref, q, k, v, dy):
        gs = _bf(g_ref[...])
        p = _dot_nt(q, k) * dmat
        dp = _dot_nt(dy, v) * dmat
        dv = _dot_tn(_bf(p), dy) + _dot(k, gs) * zeta
        dk = _dot_tn(_bf(dp), q) + _dot_nt(v, gs) * zeta
        g_ref[...] = g_ref[...] * gb + _dot_tn(_bf(q.astype(F32) * xi), dy)
        return dk, dv

    return _ret_scan("ret_bwd_kv", per_head, lgam, [(rq, RET_DK), (rk, RET_DK), (rv, RET_DV), (dy, RET_DV)],
                     [(RET_DK, F32), (RET_DV, BF16)], True)


def _attn_mask_t(tb):
    key = lax.broadcasted_iota(jnp.int32, (tb, tb), 0)
    qry = lax.broadcasted_iota(jnp.int32, (tb, tb), 1)
    return _chunk_of(key) <= _chunk_of(qry)


MASKED = -1e30
SUBLANES = 8


def _head_blocks(nb, w, tb):
    return pl.BlockSpec((nb, None, w, tb), lambda h, i: (0, h, 0, 0))


def _one_block(w, tb):
    return pl.BlockSpec((None, None, w, tb), lambda h, i: (i, h, 0, 0))


def _attn_fwd(k, qt, vt, exchange=()):
    T = k.shape[0]
    tb = _attn_block(T)
    nb = T // tb

    n_ex = len(exchange)

    def kern(qt_ref, k_ref, vt_ref, *refs):
        ex_in, (o_ref, lser_ref), ex_out = refs[:n_ex], refs[n_ex:n_ex + 2], refs[n_ex + 2:2 * n_ex + 2]
        m_ref, acc_ref, sa_ref, sb_ref = refs[2 * n_ex + 2:2 * n_ex + 6]
        sems = refs[2 * n_ex + 6:]
        qb = pl.program_id(1)
        first = jnp.logical_and(pl.program_id(0) == 0, qb == 0)
        last = jnp.logical_and(pl.program_id(0) == MLA_HEADS - 1, qb == nb - 1)
        if n_ex:
            @pl.when(first)
            def _():
                for send, _ in _chip_copies(ex_in, ex_out, *sems, True):
                    send.start()

        qt = qt_ref[...]
        m_ref[...] = jnp.full_like(m_ref, MASKED)
        acc_ref[...] = jnp.zeros_like(acc_ref)

        def scores(kb):
            rows = pl.ds(pl.multiple_of(kb * tb, tb), tb)
            return _dot(k_ref[rows, :], qt)

        def update(s, kb):
            m_old = m_ref[...]
            m_new = jnp.maximum(m_old, jnp.max(s, axis=0, keepdims=True))
            p = jnp.exp2(s - m_new)
            acc_ref[...] = acc_ref[...] * jnp.exp2(m_old - m_new) + _dot(vt_ref[kb], _bf(p))
            m_ref[...] = m_new

        def masked(s):
            return jnp.where(_attn_mask_t(tb), s, MASKED)

        sa_ref[...] = scores(0)

        def pair_body(j, carry):
            sb_ref[...] = scores(2 * j + 1)
            update(sa_ref[...], 2 * j)
            sa_ref[...] = scores(2 * j + 2)
            update(sb_ref[...], 2 * j + 1)
            return carry

        lax.fori_loop(0, qb // 2, pair_body, 0)

        @pl.when(qb % 2 == 0)
        def _():
            update(masked(sa_ref[...]), qb)

        @pl.when(qb % 2 == 1)
        def _():
            sb_ref[...] = masked(scores(qb))
            update(sa_ref[...], qb - 1)
            update(sb_ref[...], qb)

        l = acc_ref[MLA_DV:MLA_DV + 1, :]
        o_ref[...] = (acc_ref[:MLA_DV, :] / l).T
        lser_ref[...] = jnp.broadcast_to(m_ref[...] + jnp.log2(l), (SUBLANES, tb))
        if n_ex:
            @pl.when(last)
            def _():
                _wait_copies(_chip_copies(ex_in, ex_out, *sems, True))

    return pl.pallas_call(
        kern, grid=(MLA_HEADS, nb),
        in_specs=[_one_block(MLA_QK, tb), pl.BlockSpec((T, MLA_QK), lambda h, i: (0, h)),
                  _head_blocks(nb, MLA_DV + V_ONES, tb)] + [HBM_SPEC] * n_ex,
        out_specs=[pl.BlockSpec((tb, MLA_DV), lambda h, i: (i, h)), _one_block(SUBLANES, tb)] + [HBM_SPEC] * n_ex,
        out_shape=[jax.ShapeDtypeStruct((T, MLA_HEADS * MLA_DV), F32),
                   jax.ShapeDtypeStruct((nb, MLA_HEADS, SUBLANES, tb), F32)] + _exchange_shapes(exchange),
        scratch_shapes=[pltpu.VMEM((1, tb), F32), pltpu.VMEM((MLA_DV + V_ONES, tb), F32),
                        pltpu.VMEM((tb, tb), F32), pltpu.VMEM((tb, tb), F32)]
        + (_dma_sems(n_ex * N_PEER_CHIPS) if n_ex else []),
        name="attn_fwd", compiler_params=_params(("arbitrary", "arbitrary")))(qt, k, vt, *exchange)


def _attn_bwd(q, k, v, do, qt, kt, dot_, lse_rows, delta_rows):
    T = q.shape[0]
    tb = _attn_block(T)
    nb = T // tb

    def kern(q_ref, k_ref, v_ref, do_ref, qt_ref, kt_ref, dot_ref, lse_ref, dl_ref, dk_ref, dv_ref, dqt_ref, dv_acc,
             sa_ref, pa_ref, sb_ref, pb_ref):
        kb = pl.program_id(1)
        kv, vv, ktv = k_ref[...], v_ref[...], kt_ref[...]
        dk_ref[...] = jnp.zeros_like(dk_ref)
        dv_acc[...] = jnp.zeros_like(dv_acc)

        @pl.when(kb == 0)
        def _():
            dqt_ref[...] = jnp.zeros_like(dqt_ref)

        def products(qb, s_ref, dp_ref, diagonal=False):
            s = _dot(kv, qt_ref[qb])
            s_ref[...] = jnp.where(_attn_mask_t(tb), s, MASKED) if diagonal else s
            dp_ref[...] = _dot(vv, dot_ref[qb])

        def consume(qb, s_ref, dp_ref):
            rows = pl.ds(pl.multiple_of(qb * tb, tb), tb)
            p = jnp.exp2(s_ref[...] - lse_ref[qb][:1, :])
            dv_acc[...] += _dot(_bf(p), do_ref[rows, :])
            ds = _bf(p * (dp_ref[...] - dl_ref[qb][:1, :]))
            dk_ref[...] += _dot(ds, q_ref[rows, :])
            dqt_ref[qb] += _dot(ktv, ds)

        n_full = nb - 1 - kb
        products(kb, sa_ref, pa_ref, diagonal=True)

        def pair_body(j, carry):
            q1 = kb + 1 + 2 * j
            products(q1, sb_ref, pb_ref)
            consume(q1 - 1, sa_ref, pa_ref)
            products(q1 + 1, sa_ref, pa_ref)
            consume(q1, sb_ref, pb_ref)
            return carry

        lax.fori_loop(0, n_full // 2, pair_body, 0)

        @pl.when(n_full % 2 == 0)
        def _():
            consume(nb - 1, sa_ref, pa_ref)

        @pl.when(n_full % 2 == 1)
        def _():
            products(nb - 1, sb_ref, pb_ref)
            consume(nb - 2, sa_ref, pa_ref)
            consume(nb - 1, sb_ref, pb_ref)

        dk_ref[...] = dk_ref[...] * (ATTN_SCALE / Q_PRESCALE)
        dv_ref[...] = _bf(dv_acc[...])

    def blk(w):
        return pl.BlockSpec((tb, w), lambda h, i: (i, h))

    def full(w):
        return pl.BlockSpec((T, w), lambda h, i: (0, h))

    return pl.pallas_call(
        kern, grid=(MLA_HEADS, nb),
        in_specs=[full(MLA_QK), blk(MLA_QK), blk(MLA_DV), full(MLA_DV), _head_blocks(nb, MLA_QK, tb),
                  _one_block(MLA_QK, tb), _head_blocks(nb, MLA_DV, tb), _head_blocks(nb, SUBLANES, tb),
                  _head_blocks(nb, SUBLANES, tb)],
        out_specs=[blk(MLA_QK), blk(MLA_DV), _head_blocks(nb, MLA_QK, tb)],
        out_shape=[jax.ShapeDtypeStruct((T, MLA_HEADS * MLA_QK), F32),
                   jax.ShapeDtypeStruct((T, MLA_HEADS * MLA_DV), BF16),
                   jax.ShapeDtypeStruct((nb, MLA_HEADS, MLA_QK, tb), F32)],
        scratch_shapes=[pltpu.VMEM((tb, MLA_DV), F32)] + [pltpu.VMEM((tb, tb), F32)] * 4,
        name="attn_bwd", compiler_params=_params(("arbitrary", "arbitrary")))(
            q, k, v, do, qt, kt, dot_, lse_rows, delta_rows)


def _group_norm(y):
    yc = y - _mean(y)
    rstd = lax.rsqrt(_mean(yc * yc) + EPS)
    return yc * rstd, rstd


def _mix_fwd(y, rg, o, gates, h1, gn_g, w_ret_o, w_mla_o, w_out, ln_g, ln_b, tm):
    T, D = h1.shape

    def body(i, y_ref, rg_ref, o_ref, gt_ref, h_ref, gn_ref, wr_ref, wm_ref, wo_ref, g_ref, b_ref,
             h2_ref, z_ref, yret_ref, ymla_ref, yr_ref, mix_ref):
        for h in range(RET_HEADS):
            sl = slice(h * RET_DV, (h + 1) * RET_DV)
            yn, _ = _group_norm(y_ref[:, sl].astype(F32))
            r = rg_ref[:, sl].astype(F32)
            yr_ref[:, sl] = _bf(r * _sigmoid(r) * (yn * gn_ref[:, sl]))
        yret = _dot(yr_ref[...], wr_ref[...])
        ymla = _dot(_bf(o_ref[...]), wm_ref[...])
        yret_ref[...] = _bf(yret)
        ymla_ref[...] = _bf(ymla)
        mix = _bf(_sigmoid(gt_ref[:, :D].astype(F32)) * yret + _sigmoid(gt_ref[:, D:].astype(F32)) * ymla)
        mix_ref[...] = mix
        z = ALPHA * h_ref[...] + _dot(mix, wo_ref[...])
        xhat, _ = _ln_stats(z)
        z_ref[...] = z
        h2_ref[...] = xhat * g_ref[...] + b_ref[...]

    return _rowcall("mix_fwd", body, T, tm, [y, rg, o, gates, h1], [gn_g, w_ret_o, w_mla_o, w_out, ln_g, ln_b],
                    [(D, F32), (D, F32), (D, BF16), (D, BF16), (RET_HEADS * RET_DV, BF16), (D, BF16)])


def _mix_bwd(dh2, z1, gates, yret, ymla, y, rg, o, gn_g, w_ret_o, w_mla_o, w_out, ln_g, tm, exchange=None):
    T, D = dh2.shape
    rv = RET_HEADS * RET_DV

    def body(i, dh_ref, z_ref, gt_ref, yret_ref, ymla_ref, y_ref, rg_ref, o_ref, gn_ref, wr_ref, wm_ref, wo_ref, g_ref,
             dz_ref, dgt_ref, drg_ref, dy_ref, do_ref, dyret_ref, dymla_ref, dg_ref, db_ref, dgn_ref, dot_ref,
             dl_ref):
        xhat, rstd = _ln_stats(z_ref[...])
        dz, dg, db = _ln_bwd(dh_ref[...], xhat, rstd, g_ref[...])
        _acc(i, dg_ref, dg)
        _acc(i, db_ref, db)
        dz_ref[...] = dz
        dmix = _dot_nt(_bf(dz), wo_ref[...])
        sr = _sigmoid(gt_ref[:, :D].astype(F32))
        sm = _sigmoid(gt_ref[:, D:].astype(F32))
        dgt_ref[:, :D] = _bf(dmix * yret_ref[...].astype(F32) * sr * (1.0 - sr))
        dgt_ref[:, D:] = _bf(dmix * ymla_ref[...].astype(F32) * sm * (1.0 - sm))
        dyret = _bf(dmix * sr)
        dymla = _bf(dmix * sm)
        dyret_ref[...] = dyret
        dymla_ref[...] = dymla
        dov = _dot_nt(dymla, wm_ref[...])
        do_ref[...] = _bf(dov)
        for h in range(MLA_HEADS):
            sl = slice(h * MLA_DV, (h + 1) * MLA_DV)
            dot_ref[h] = _bf(dov[:, sl].T)
            delta = jnp.sum(dov[:, sl] * o_ref[:, sl], axis=-1, keepdims=True)
            dl_ref[h] = jnp.broadcast_to(delta, (tm, LANES)).T[:SUBLANES, :]
        dyr = _dot_nt(dyret, wr_ref[...])
        dgn = []
        for h in range(RET_HEADS):
            sl = slice(h * RET_DV, (h + 1) * RET_DV)
            yn, grstd = _group_norm(y_ref[:, sl].astype(F32))
            r = rg_ref[:, sl].astype(F32)
            sig = _sigmoid(r)
            d = dyr[:, sl]
            drg_ref[:, sl] = _bf(d * (yn * gn_ref[:, sl]) * sig * (1.0 + r * (1.0 - sig)))
            dt = d * (r * sig)
            dgn.append(jnp.sum(dt * yn, axis=0, keepdims=True))
            dyn = dt * gn_ref[:, sl]
            dy_ref[:, sl] = _bf(grstd * (dyn - _mean(dyn) - yn * _mean(dyn * yn)))
        _acc(i, dgn_ref, jnp.concatenate(dgn, axis=1))

    return _rowcall("mix_bwd", body, T, tm, [dh2, z1, gates, yret, ymla, y, rg, o],
                    [gn_g, w_ret_o, w_mla_o, w_out, ln_g],
                    [(D, F32), (2 * D, BF16), (rv, BF16), (rv, BF16), (MLA_HEADS * MLA_DV, BF16), (D, BF16), (D, BF16)],
                    [((1, D), F32), ((1, D), F32), ((1, rv), F32)],
                    tiled_outs=[_transposed_blocks(T, tm, MLA_DV, BF16), _transposed_blocks(T, tm, SUBLANES, F32)],
                    exchange=exchange)


def _proj_mla_bwd(dqt, dk, dv, lat, tabs, w_uq, w_uk, w_uv, qn_g, kvn_g, tm):
    T = dk.shape[0]
    H = MLA_HEADS
    lat_w = Q_LORA + KV_LORA

    def body(i, dk_ref, dv_ref, lat_ref, c_ref, s1_ref, s2_ref, dqt_ref, wuq_ref, wuk_ref, wuv_ref, qg_ref, kg_ref,
             dlat_ref, dkpe_ref, dqb_ref, dkn_ref, dqg_ref, dkg_ref):
        c, s1, s2 = c_ref[...], s1_ref[...], s2_ref[...]
        dkpe = jnp.zeros((tm, LANES), F32)
        for h in range(H):
            o = h * MLA_QK
            dqh = dqt_ref[h].T * ATTN_SCALE
            dqb_ref[:, o:o + MLA_NOPE] = _bf(dqh[:, :MLA_NOPE])
            dqb_ref[:, o + MLA_NOPE:o + MLA_QK] = _bf(_rope_pe_bwd(dqh[:, MLA_NOPE:], c, s1, s2))
            dkn_ref[:, h * MLA_NOPE:(h + 1) * MLA_NOPE] = _bf(dk_ref[:, o:o + MLA_NOPE])
            dkpe += dk_ref[:, o + MLA_NOPE:o + MLA_QK]
        dkn_ref[:, H * MLA_NOPE:] = dv_ref[...]
        dkpe_ref[...] = _bf(_rope_pe_bwd(dkpe, c, s1, s2))
        dcqn = _dot_nt(dqb_ref[...], wuq_ref[...])
        dckn = _dot_nt(dkn_ref[:, :H * MLA_NOPE], wuk_ref[...]) + _dot_nt(dv_ref[...], wuv_ref[...])
        for dn, x, g_ref, dg_ref, sl in ((dcqn, lat_ref[:, :Q_LORA], qg_ref, dqg_ref, slice(0, Q_LORA)),
                                         (dckn, lat_ref[:, Q_LORA:], kg_ref, dkg_ref, slice(Q_LORA, lat_w))):
            xn, r = _rms(x, None)
            _acc(i, dg_ref, jnp.sum(dn * xn, axis=0, keepdims=True))
            dxn = dn * g_ref[...]
            dlat_ref[:, sl] = _bf(r * (dxn - xn * _mean(dxn * xn)))

    dqt_shape, dqt_spec = _transposed_blocks(T, tm, MLA_QK, F32)
    assert dqt.shape == dqt_shape.shape
    return _rowcall("proj_mla_bwd", body, T, tm, [dk, dv, lat, *tabs], [w_uq, w_uk, w_uv, qn_g, kvn_g],
                    [(lat_w, BF16), (LANES, BF16), (H * MLA_QK, BF16), (H * (MLA_NOPE + MLA_DV), BF16)],
                    [((1, Q_LORA), F32), ((1, KV_LORA), F32)], tiled_ins=[(dqt, dqt_spec)])


def _proj_bwd(drq, drk, drv, drg, dz1, dlat, dkpe, dgates, cos_r, sin_r, w_r, w_c, w_kpe, w_g, tm):
    T, D = dz1.shape
    qk = RET_HEADS * RET_DK
    rv = RET_HEADS * RET_DV
    o_lat = 2 * qk + 2 * rv
    o_kpe = o_lat + dlat.shape[1]
    o_gate = o_kpe + LANES
    o_end = o_gate + dgates.shape[1]
    width = -(-o_end // WG_TILE_N) * WG_TILE_N

    def body(i, drq_ref, drk_ref, drv_ref, drg_ref, dz_ref, dlat_ref, dkpe_ref, dgt_ref, cos_ref, sin_ref,
             wr_ref, wc_ref, wk_ref, wg_ref, dh_ref, dpr_ref):
        cos, sin = cos_ref[...], sin_ref[...]
        for src, off, scale in ((drq_ref, 0, 1.0), (drk_ref, qk, RET_DK ** -0.5)):
            for h in range(RET_HEADS):
                d = src[:, h * RET_DK:(h + 1) * RET_DK]
                dpr_ref[:, off + h * RET_DK:off + (h + 1) * RET_DK] = _bf(
                    (d * cos + _roll(d * sin, RET_DK // 2)) * scale)
        dpr_ref[:, 2 * qk:2 * qk + rv] = drv_ref[...]
        dpr_ref[:, 2 * qk + rv:o_lat] = drg_ref[...]
        dpr_ref[:, o_lat:o_kpe] = dlat_ref[...]
        dpr_ref[:, o_kpe:o_gate] = dkpe_ref[...]
        dpr_ref[:, o_gate:o_end] = dgt_ref[...]
        if width > o_end:
            dpr_ref[:, o_end:] = jnp.zeros((tm, width - o_end), BF16)
        dh_ref[...] = (ALPHA * dz_ref[...] + _dot_nt(dpr_ref[:, :o_lat], wr_ref[...])
                       + _dot_nt(dlat_ref[...], wc_ref[...]) + _dot_nt(dkpe_ref[...], wk_ref[...])
                       + _dot_nt(dgt_ref[...], wg_ref[...]))

    return _rowcall("proj_bwd", body, T, tm, [drq, drk, drv, drg, dz1, dlat, dkpe, dgates, cos_r, sin_r],
                    [w_r, w_c, w_kpe, w_g], [(D, F32), (width, BF16)])


def _ple_loss(h3, p, target, w_gate, w_proj, ln_g, ln_b, tm):
    T, D = h3.shape

    def body(i, h_ref, p_ref, t_ref, wg_ref, wp_ref, g_ref, b_ref, dh_ref, dgp_ref, dpp_ref, loss_ref, dg_ref, db_ref):
        hv = h_ref[...]
        sg = _sigmoid(_dot(_bf(hv), wg_ref[...]))
        pp = _dot(_bf(p_ref[...]), wp_ref[...])
        xhat, rstd = _ln_stats(ALPHA * hv + sg * pp)
        err = xhat * g_ref[...] + b_ref[...] - t_ref[...]
        row_loss = 0.5 * _mean(err * err)
        _acc(i, loss_ref, jnp.broadcast_to(jnp.sum(row_loss, axis=0, keepdims=True), (1, LANES)))
        dz, dg, db = _ln_bwd(err * (1.0 / D), xhat, rstd, g_ref[...])
        _acc(i, dg_ref, dg)
        _acc(i, db_ref, db)
        dgp = _bf(dz * pp * sg * (1.0 - sg))
        dgp_ref[...] = dgp
        dpp_ref[...] = _bf(dz * sg)
        dh_ref[...] = ALPHA * dz + _dot_nt(dgp, wg_ref[...])

    return _rowcall("ple_loss", body, T, tm, [h3, p, target], [w_gate, w_proj, ln_g, ln_b],
                    [(D, F32), (D, BF16), (D, BF16)], [((1, LANES), F32), ((1, D), F32), ((1, D), F32)])


def _ewise(name, fn, ins, n_out, out_dtype=F32):
    r, c = ins[0].shape
    tr = _tile(r, max(8, (1 << 19) // c // 8 * 8), 8)

    def kern(*refs):
        outs = fn(*[x[...] for x in refs[:len(ins)]])
        for o_ref, o in zip(refs[len(ins):], outs):
            o_ref[...] = o.astype(out_dtype)

    spec = pl.BlockSpec((tr, c), lambda i: (i, 0))
    return pl.pallas_call(kern, grid=(r // tr,), in_specs=[spec] * len(ins), out_specs=[spec] * n_out,
                          out_shape=[jax.ShapeDtypeStruct((r, c), out_dtype)] * n_out, name=name,
                          compiler_params=_params(("arbitrary",)))(*ins)


def _adamw_math(w, g, m, v):
    m = ADAM_B1 * m + (1.0 - ADAM_B1) * g
    v = ADAM_B2 * v + (1.0 - ADAM_B2) * (g * g)
    m_hat = m / (1.0 - ADAM_B1 ** ADAM_STEP)
    v_hat = v / (1.0 - ADAM_B2 ** ADAM_STEP)
    return -ADAM_LR * (m_hat / (jnp.sqrt(v_hat) + ADAM_EPS) + ADAM_WD * w), m, v


def _adamw(name, w, g, m, v):
    shape = w.shape
    c = shape[-1]
    flat = [t.reshape(-1, c) for t in (w, g, m, v)]
    return [t.reshape(shape) for t in _ewise(name, _adamw_math, flat, 3)]


def _place():
    return lax.axis_index("x"), lax.axis_index("y"), lax.axis_index("c")


def _dma_sems(n):
    return [pltpu.SemaphoreType.DMA((n,)), pltpu.SemaphoreType.DMA((n,))]


N_PEER_CHIPS = N_CHIPS - 1


def _chips_exchange(name, srcs, broadcast):
    n = len(srcs)

    def kern(*refs):
        cps = _chip_copies(refs[:n], refs[n:2 * n], refs[2 * n], refs[2 * n + 1], broadcast)
        for send, _ in cps:
            send.start()
        _wait_copies(cps)

    return pl.pallas_call(
        kern, out_shape=_exchange_shapes(srcs), in_specs=[HBM_SPEC] * n, out_specs=[HBM_SPEC] * n,
        scratch_shapes=_dma_sems(n * N_PEER_CHIPS), name=name)(*srcs)


def _exchange_shapes(srcs):
    return [jax.ShapeDtypeStruct((N_CHIPS,) + s.shape[1:], s.dtype) for s in srcs]


def _chip_copies(src_refs, out_refs, send_sems, recv_sems, broadcast):
    x, y, c = _place()
    me = 2 * x + y
    peers = [(1 - x, y), (x, 1 - y), (1 - x, 1 - y)]
    cps = []
    for j, (px, py) in enumerate(peers):
        for a, (src_ref, out_ref) in enumerate(zip(src_refs, out_refs)):
            piece = src_ref.at[c] if broadcast else src_ref.at[2 * px + py]

            def copy(slot):
                return pltpu.make_async_remote_copy(
                    src_ref=piece, dst_ref=out_ref.at[slot], send_sem=send_sems.at[a * N_PEER_CHIPS + j],
                    recv_sem=recv_sems.at[a * N_PEER_CHIPS + j], device_id=(px, py, c), device_id_type=MESH)

            cps.append((copy(me), copy(2 * px + py)))
    return cps


def _wait_copies(cps):
    for _, landing in cps:
        landing.wait_recv()
    for send, _ in cps:
        send.wait_send()


def _sibling_swap(name, srcs, halves):
    n = len(srcs)

    def kern(*refs):
        src_refs, out_refs = refs[:n], refs[n:2 * n]
        send_sems, recv_sems = refs[2 * n:]
        x, y, c = _place()

        def copy(a):
            piece = src_refs[a].at[:, 1 - c] if halves else src_refs[a]
            return pltpu.make_async_remote_copy(
                src_ref=piece, dst_ref=out_refs[a], send_sem=send_sems.at[a], recv_sem=recv_sems.at[a],
                device_id=(x, y, 1 - c), device_id_type=MESH)

        cps = [copy(a) for a in range(n)]
        for cp in cps:
            cp.start()
        for cp in cps:
            cp.wait_recv()
        for cp in cps:
            cp.wait_send()

    def out_shape(s):
        return jax.ShapeDtypeStruct((s.shape[0],) + s.shape[2:] if halves else s.shape, s.dtype)

    return pl.pallas_call(
        kern, out_shape=[out_shape(s) for s in srcs], in_specs=[HBM_SPEC] * n, out_specs=[HBM_SPEC] * n,
        scratch_shapes=_dma_sems(n), name=name)(*srcs)


def _all_devices(name, src, reduce):
    r, c = src.shape
    n_dev = 2 * N_CHIPS

    def kern(src_ref, out_ref, *scratch):
        if reduce:
            gat_ref, send_sems, recv_sems = scratch
        else:
            gat_ref = out_ref
            send_sems, recv_sems = scratch
        x, y, cc = _place()
        me = 4 * x + 2 * y + cc
        gat_ref[me] = src_ref[...]
        peers = []
        for j in range(1, n_dev):
            px = 1 - x if j & 4 else x
            py = 1 - y if j & 2 else y
            pc = 1 - cc if j & 1 else cc
            peers.append((px, py, pc))

        def copy(j, peer, slot):
            return pltpu.make_async_remote_copy(
                src_ref=src_ref, dst_ref=gat_ref.at[slot], send_sem=send_sems.at[j], recv_sem=recv_sems.at[j],
                device_id=peer, device_id_type=MESH)

        sends = [copy(j, peer, me) for j, peer in enumerate(peers)]
        for cp in sends:
            cp.start()
        for j, (px, py, pc) in enumerate(peers):
            copy(j, (px, py, pc), 4 * px + 2 * py + pc).wait_recv()
        for cp in sends:
            cp.wait_send()
        if reduce:
            total = gat_ref[0]
            for d in range(1, n_dev):
                total = total + gat_ref[d]
            out_ref[...] = total

    out_shape = jax.ShapeDtypeStruct((r, c) if reduce else (n_dev, r, c), src.dtype)
    scratch = ([pltpu.VMEM((n_dev, r, c), src.dtype)] if reduce else []) + _dma_sems(n_dev - 1)
    return pl.pallas_call(kern, out_shape=out_shape, in_specs=[VMEM_SPEC], out_specs=VMEM_SPEC,
                          scratch_shapes=scratch, name=name)(src)


def _halves(t, axis):
    return t.reshape(t.shape[:axis] + (2, t.shape[axis] // 2) + t.shape[axis + 1:])


def _by_core(mine, theirs, axis):
    c = lax.axis_index("c")
    both = jnp.where(c == 0, jnp.stack([mine, theirs], axis), jnp.stack([theirs, mine], axis))
    return both.reshape(both.shape[:axis] + (2 * both.shape[axis + 1],) + both.shape[axis + 2:])


def _with_own(own, others):
    me = 2 * lax.axis_index("x") + lax.axis_index("y")
    is_me = (jnp.arange(N_CHIPS, dtype=jnp.int32) == me)[:, None, None]
    return jnp.where(is_me, own[None], others)


def _join_shards(name, shards):
    _, r, c = shards.shape
    if name in COL_SHARDED:
        return shards.transpose(1, 0, 2).reshape(r, N_CHIPS * c)
    return shards.reshape(N_CHIPS * r, c)


def _split_shards(name, full):
    if full.ndim == 3:
        return full
    r, c = full.shape
    if name in COL_SHARDED:
        return jnp.stack([full[:, k * (c // N_CHIPS):(k + 1) * (c // N_CHIPS)] for k in range(N_CHIPS)])
    return full.reshape(N_CHIPS, r // N_CHIPS, c)


def _rope_tables(positions):
    pos = positions.reshape(-1).astype(F32)[:, None]
    half = RET_DK // 2
    ang = pos * (ROPE_BASE ** (-jnp.arange(half, dtype=F32) / half))
    cos_r = jnp.concatenate([jnp.cos(ang)] * 2, axis=1)
    sin_r = jnp.concatenate([-jnp.sin(ang), jnp.sin(ang)], axis=1)
    half = MLA_ROPE // 2
    ang = pos * (ROPE_BASE ** (-jnp.arange(half, dtype=F32) / half))
    zeros = jnp.zeros_like(ang)
    rest = LANES - MLA_ROPE
    c = jnp.concatenate([jnp.cos(ang)] * 2 + [jnp.ones((ang.shape[0], rest), F32)], axis=1)
    s1 = jnp.concatenate([-jnp.sin(ang), zeros, jnp.zeros((ang.shape[0], rest), F32)], axis=1)
    s2 = jnp.concatenate([zeros, jnp.sin(ang), jnp.zeros((ang.shape[0], rest), F32)], axis=1)
    return cos_r, sin_r, (c, s1, s2)


GATHER_GROUPS = (("ffn1_w_in", "ffn1_w_out"), ("w_in", "w_uq", "w_ukv"),
                 ("w_ret_o", "w_mla_o", "w_out", "ffn2_w_in", "ffn2_w_out", "ple_w_gate", "ple_w_proj"))
REDUCE_GROUPS = (("ple_w_gate", "ple_w_proj", "ffn2_w_in", "ffn2_w_out"),
                 ("w_out", "w_ret_o", "w_mla_o", "w_uq", "w_ukv", "w_in"), ("ffn1_w_in", "ffn1_w_out"))


def _gathered(tag, names, own, mine):
    theirs = _sibling_swap("gather_cores_" + tag, mine, False)
    out = {}
    for n, m, t in zip(names, mine, theirs):
        full = _with_own(own[n], _by_core(m, t, 1))
        out[n] = full if n in ("ffn1_w_in", "ffn2_w_in") else _join_shards(n, full)
    return out


def _chip_sums(tag, names, grads):
    halves = [_halves(_split_shards(n, grads[n]), 1) for n in names]
    theirs = _sibling_swap("reduce_cores_" + tag, halves, True)

    def one(n, g, t):
        k, _, r, c = g.shape
        tr = _tile(r, max(8, (1 << 17) // c // 8 * 8), 8)

        def kern(g_ref, t_ref, o_ref):
            mine = jnp.where(lax.axis_index("c") == 0, g_ref[:, 0], g_ref[:, 1])
            o_ref[...] = _bf(mine.astype(F32) + t_ref[...].astype(F32))

        spec = pl.BlockSpec((k, tr, c), lambda i: (0, i, 0))
        return pl.pallas_call(kern, grid=(r // tr,),
                              in_specs=[pl.BlockSpec((k, 2, tr, c), lambda i: (0, 0, i, 0)), spec], out_specs=spec,
                              out_shape=jax.ShapeDtypeStruct((k, r, c), BF16), name="reduce_cores_add_" + n,
                              compiler_params=_params(("arbitrary",)))(g, t)

    return [one(n, g, t) for n, g, t in zip(names, halves, theirs)]


def _block_totals(names, sums, parts):
    def one(n, s, pt):
        _, r, c = s.shape
        tr = _tile(r, max(8, (1 << 17) // c // 8 * 8), 8)

        def kern(s_ref, p_ref, o_ref):
            me = 2 * lax.axis_index("x") + lax.axis_index("y")
            terms = [jnp.where(k == me, s_ref[k], p_ref[k]).astype(F32) for k in range(N_CHIPS)]
            o_ref[...] = ((terms[0] + terms[1]) + terms[2]) + terms[3]

        spec = pl.BlockSpec((N_CHIPS, tr, c), lambda i: (0, i, 0))
        return pl.pallas_call(kern, grid=(r // tr,), in_specs=[spec, spec],
                              out_specs=pl.BlockSpec((tr, c), lambda i: (i, 0)),
                              out_shape=jax.ShapeDtypeStruct((r, c), F32), name="reduce_chips_add_" + n,
                              compiler_params=_params(("arbitrary",)))(s, pt)

    return [one(n, s, pt) for n, s, pt in zip(names, sums, parts)]


def _local_step(x, p, positions, target, shards, ln_g, ln_b, gn_g, qn_g, kvn_g):
    T, D = x.shape
    tm = min(256, T)
    H = MLA_HEADS
    qk, rv = RET_HEADS * RET_DK, RET_HEADS * RET_DV
    cos_r, sin_r, tabs = _rope_tables(positions)
    lgam = jnp.broadcast_to(jnp.log(1.0 - 2.0 ** (-5.0 - jnp.arange(RET_HEADS, dtype=F32)))[:, None, None],
                            (RET_HEADS, 1, LANES))
    lng = [ln_g[k:k + 1] for k in range(N_LN)]
    lnb = [ln_b[k:k + 1] for k in range(N_LN)]
    own = {n: _bf(shards[n]) for n in BIG_WEIGHTS}
    to_send = [[_halves(own[n], 0) for n in names] for names in GATHER_GROUPS]

    w = _gathered("a", GATHER_GROUPS[0], own, _chips_exchange("gather_chips_a", to_send[0], True))
    h1, z0, a1, *arrived = _ffn_fwd("ffn1_fwd", x, w["ffn1_w_in"], w["ffn1_w_out"], lng[0], lnb[0], tm,
                                    exchange=(to_send[1], True))
    w.update(_gathered("b", GATHER_GROUPS[1], own, arrived))

    w_in = w["w_in"]
    o_lat, o_kpe, o_gate = 2 * qk + 2 * rv, 2 * qk + 2 * rv + Q_LORA + KV_LORA, 2 * qk + 2 * rv + Q_LORA + KV_LORA + MLA_ROPE
    w_r, w_c = w_in[:, :o_lat], w_in[:, o_lat:o_kpe]
    w_kpe = jnp.pad(w_in[:, o_kpe:o_gate], ((0, 0), (0, LANES - MLA_ROPE)))
    w_g = w_in[:, o_gate:]
    w_uq = jnp.pad(w["w_uq"].reshape(Q_LORA, H, MLA_NOPE + MLA_ROPE),
                   ((0, 0), (0, 0), (0, MLA_QK - MLA_NOPE - MLA_ROPE))).reshape(Q_LORA, H * MLA_QK)
    w_ukv = w["w_ukv"].reshape(KV_LORA, H, MLA_NOPE + MLA_DV)
    w_uk = w_ukv[:, :, :MLA_NOPE].reshape(KV_LORA, H * MLA_NOPE)
    w_uv = w_ukv[:, :, MLA_NOPE:].reshape(KV_LORA, H * MLA_DV)

    rq, rk, rvv, rg = _proj_ret(h1, w_r, cos_r, sin_r, tm)
    lat, gates, q, k, v, latn, qt, kt, vt = _proj_mla(h1, tabs, w_c, w_kpe, w_g, w_uq, w_uk, w_uv, qn_g, kvn_g, tm)
    y = _ret_fwd(rq, rk, rvv, lgam)
    o, lse_rows, *arrived = _attn_fwd(k, qt, vt, exchange=to_send[2])
    w.update(_gathered("c", GATHER_GROUPS[2], own, arrived))
    h2, z1, yret, ymla, yr, mix = _mix_fwd(y, rg, o, gates, h1, gn_g, w["w_ret_o"], w["w_mla_o"], w["w_out"],
                                           lng[1], lnb[1], tm)
    h3, z2, a2 = _ffn_fwd("ffn2_fwd", h2, w["ffn2_w_in"], w["ffn2_w_out"], lng[2], lnb[2], tm)

    dh3, dgp, dpp, loss, dg3, db3 = _ple_loss(h3, p, target, w["ple_w_gate"], w["ple_w_proj"], lng[3], lnb[3], tm)
    dh2, da2, s2, df2, dg2, db2 = _ffn_bwd("ffn2_bwd", dh3, z2, a2, w["ffn2_w_in"], w["ffn2_w_out"], lng[2], tm)
    grads = {"ple_w_gate": _mm_tn("wg_ple_gate", h3, dgp), "ple_w_proj": _mm_tn("wg_ple_proj", p, dpp),
             "ffn2_w_in": _mm_tn("wg_ffn2_in", h2, da2, n_split=N_CHIPS), "ffn2_w_out": _mm_tn("wg_ffn2_out", s2, df2)}
    sums1 = _chip_sums("1", REDUCE_GROUPS[0], grads)
    (dz1, dgates, drg, dy, do, dyret, dymla, dg1, db1, dgn, dot_, delta_rows, *parts1) = _mix_bwd(
        dh2, z1, gates, yret, ymla, y, rg, o, gn_g, w["w_ret_o"], w["w_mla_o"], w["w_out"], lng[1], tm,
        exchange=(sums1, False))
    drq = _ret_bwd_q(rq, rk, rvv, dy, lgam)
    drk, drv = _ret_bwd_kv(rq, rk, rvv, dy, lgam)
    dk, dv, dqt = _attn_bwd(q, k, v, do, qt, kt, dot_, lse_rows, delta_rows)
    dlat, dkpe, dqb, dkv, dqg, dkg = _proj_mla_bwd(dqt, dk, dv, lat, tabs, w_uq, w_uk, w_uv, qn_g, kvn_g, tm)
    dh1, dpr = _proj_bwd(drq, drk, drv, drg, dz1, dlat, dkpe, dgates, cos_r, sin_r, w_r, w_c, w_kpe, w_g, tm)
    g_uq = _mm_tn("wg_uq", latn[:, :Q_LORA], dqb).reshape(Q_LORA, H, MLA_QK)[:, :, :MLA_NOPE + MLA_ROPE]
    g_ukv = _mm_tn("wg_ukv", latn[:, Q_LORA:], dkv)
    g_uk = g_ukv[:, :H * MLA_NOPE].reshape(KV_LORA, H, MLA_NOPE)
    g_uv = g_ukv[:, H * MLA_NOPE:].reshape(KV_LORA, H, MLA_DV)
    g_in = _mm_tn("wg_in", h1, dpr)
    grads.update({
        "w_in": jnp.concatenate([g_in[:, :o_kpe + MLA_ROPE], g_in[:, o_kpe + LANES:o_kpe + LANES + 2 * D]], axis=1),
        "w_ret_o": _mm_tn("wg_ret_o", yr, dyret),
        "w_uq": g_uq.reshape(Q_LORA, H * (MLA_NOPE + MLA_ROPE)),
        "w_ukv": jnp.concatenate([g_uk, g_uv], axis=2).reshape(KV_LORA, H * (MLA_NOPE + MLA_DV)),
        "w_mla_o": _mm_tn("wg_mla_o", o, dymla),
        "w_out": _mm_tn("wg_out", mix, dz1)})
    sums2 = _chip_sums("2", REDUCE_GROUPS[1], grads)
    dx, da1, s1, df1, dg0, db0, *parts2 = _ffn_bwd("ffn1_bwd", dh1, z0, a1, w["ffn1_w_in"], w["ffn1_w_out"], lng[0], tm,
                                                   exchange=(sums2, False))
    grads.update({"ffn1_w_in": _mm_tn("wg_ffn1_in", x, da1, n_split=N_CHIPS),
                  "ffn1_w_out": _mm_tn("wg_ffn1_out", s1, df1)})
    sums3 = _chip_sums("3", REDUCE_GROUPS[2], grads)
    parts3 = _chips_exchange("reduce_chips_3", sums3, False)

    names = [n for group in REDUCE_GROUPS for n in group]
    totals = _block_totals(names, sums1 + sums2 + sums3, list(parts1) + list(parts2) + list(parts3))
    others = _sibling_swap("reduce_join", totals, False)
    reduced = {n: _by_core(t, o_, 0) for n, t, o_ in zip(names, totals, others)}
    small = {"ln_g": jnp.concatenate([dg0, dg1, dg2, dg3], axis=0), "ln_b": jnp.concatenate([db0, db1, db2, db3], axis=0),
             "ret_gn_g": dgn, "q_norm_g": dqg, "kv_norm_g": dkg}
    return loss[0, 0], dx, reduced, small


def kernel(x, p, positions, ln_g, ln_b, ffn1_w_in, ffn1_w_out, w_in, ret_gn_g, w_ret_o, q_norm_g, kv_norm_g, w_uq, w_ukv, w_mla_o, w_out, ffn2_w_in, ffn2_w_out, ple_w_gate, ple_w_proj, loss_target, m_ln_g, m_ln_b, m_ffn1_w_in, m_ffn1_w_out, m_w_in, m_ret_gn_g, m_w_ret_o, m_q_norm_g, m_kv_norm_g, m_w_uq, m_w_ukv, m_w_mla_o, m_w_out, m_ffn2_w_in, m_ffn2_w_out, m_ple_w_gate, m_ple_w_proj, v_ln_g, v_ln_b, v_ffn1_w_in, v_ffn1_w_out, v_w_in, v_ret_gn_g, v_w_ret_o, v_q_norm_g, v_kv_norm_g, v_w_uq, v_w_ukv, v_w_mla_o, v_w_out, v_ffn2_w_in, v_ffn2_w_out, v_ple_w_gate, v_ple_w_proj):
    names = ("ln_g", "ln_b", "ffn1_w_in", "ffn1_w_out", "w_in", "ret_gn_g", "w_ret_o", "q_norm_g", "kv_norm_g", "w_uq",
             "w_ukv", "w_mla_o", "w_out", "ffn2_w_in", "ffn2_w_out", "ple_w_gate", "ple_w_proj")
    weights = dict(zip(names, (ln_g, ln_b, ffn1_w_in, ffn1_w_out, w_in, ret_gn_g, w_ret_o, q_norm_g, kv_norm_g, w_uq,
                               w_ukv, w_mla_o, w_out, ffn2_w_in, ffn2_w_out, ple_w_gate, ple_w_proj)))
    m_in = dict(zip(names, (m_ln_g, m_ln_b, m_ffn1_w_in, m_ffn1_w_out, m_w_in, m_ret_gn_g, m_w_ret_o, m_q_norm_g,
                            m_kv_norm_g, m_w_uq, m_w_ukv, m_w_mla_o, m_w_out, m_ffn2_w_in, m_ffn2_w_out, m_ple_w_gate,
                            m_ple_w_proj)))
    v_in = dict(zip(names, (v_ln_g, v_ln_b, v_ffn1_w_in, v_ffn1_w_out, v_w_in, v_ret_gn_g, v_w_ret_o, v_q_norm_g,
                            v_kv_norm_g, v_w_uq, v_w_ukv, v_w_mla_o, v_w_out, v_ffn2_w_in, v_ffn2_w_out, v_ple_w_gate,
                            v_ple_w_proj)))
    chip = 2 * lax.axis_index("x") + lax.axis_index("y")
    D = x.shape[-1]
    dq = D // N_CHIPS

    shards = {n: weights[n][0] for n in BIG_WEIGHTS}
    ln_all = _all_devices("gather_ln", jnp.concatenate([ln_g[0], ln_b[0]], axis=0), False)
    ln_full = ln_all[::2].transpose(1, 0, 2).reshape(2 * N_LN, D)
    loss, dx, big, small = _local_step(x[0], p[0, 0], positions, loss_target[0], shards, ln_full[:N_LN],
                                       ln_full[N_LN:], ret_gn_g, q_norm_g, kv_norm_g)

    loss = lax.psum(loss, ("x", "y", "c"))
    small_names = ("ln_g", "ln_b", "ret_gn_g", "q_norm_g", "kv_norm_g")
    flat = jnp.concatenate([small[n].reshape(-1) for n in small_names])
    rows = -(-flat.shape[0] // LANES // 8) * 8
    flat = jnp.pad(flat, (0, rows * LANES - flat.shape[0])).reshape(rows, LANES)
    flat = _all_devices("reduce_small", flat, True).reshape(-1)
    off = 0
    for n in small_names:
        size = small[n].size
        small[n] = flat[off:off + size].reshape(small[n].shape)
        off += size
    g_out = dict(big)
    for n in ("ln_g", "ln_b"):
        g_out[n] = lax.dynamic_slice_in_dim(small[n], chip * dq, dq, axis=1)
    for n in ("ret_gn_g", "q_norm_g", "kv_norm_g"):
        g_out[n] = small[n]

    deltas, new_m, new_v = {}, {}, {}
    for n in names:
        g = g_out[n].reshape(weights[n].shape)
        g_out[n] = g
        deltas[n], new_m[n], new_v[n] = _adamw("adamw_" + n, weights[n], g, m_in[n], v_in[n])
    return (loss, dx[None], *[g_out[n] for n in names], *[deltas[n] for n in names], *[new_m[n] for n in names],
            *[new_v[n] for n in names])
```

```python
import functools

import jax
import jax.numpy as jnp
from jax import lax
from jax.experimental import pallas as pl
from jax.experimental.pallas import tpu as pltpu

D_MODEL = 1024
CHUNK = 64
D_PLE = 256
D_FF = 2816
RET_HEADS = 8
RET_DK = 128
RET_DV = 256
MLA_HEADS = 8
MLA_NOPE = 128
MLA_ROPE = 64
MLA_DV = 128
MLA_QK = 256
Q_LORA = 256
KV_LORA = 256
ROPE_BASE = 10000.0
EPS = 1e-5
N_LN = 4
ALPHA = 2.0 ** 0.25
ADAM_LR = 0.001
ADAM_B1 = 0.9
ADAM_B2 = 0.999
ADAM_EPS = 1e-08
ADAM_WD = 0.01
ADAM_STEP = 10

LANES = 128
VMEM_LIMIT = 60 << 20
N_CHIPS = 4

F32 = jnp.float32
BF16 = jnp.bfloat16
MESH = pl.DeviceIdType.MESH
HBM_SPEC = pl.BlockSpec(memory_space=pltpu.HBM)
VMEM_SPEC = pl.BlockSpec(memory_space=pltpu.VMEM)

BIG_WEIGHTS = ("ffn1_w_in", "ffn1_w_out", "w_in", "w_ret_o", "w_uq", "w_ukv", "w_mla_o", "w_out",
               "ffn2_w_in", "ffn2_w_out", "ple_w_gate", "ple_w_proj")
COL_SHARDED = ("ffn1_w_in", "w_in", "w_uq", "w_ukv", "ffn2_w_in", "ple_w_proj")


def _dot(a, b):
    return jnp.dot(a, b, preferred_element_type=F32)


def _dot_nt(a, b):
    return lax.dot_general(a, b, (((1,), (1,)), ((), ())), preferred_element_type=F32)


def _dot_tn(a, b):
    return lax.dot_general(a, b, (((0,), (0,)), ((), ())), preferred_element_type=F32)


def _bf(x):
    return x.astype(BF16)


def _sigmoid(x):
    return 0.5 * jnp.tanh(0.5 * x) + 0.5


def _mean(x):
    return jnp.mean(x, axis=-1, keepdims=True)


def _ln_stats(z):
    zc = z - _mean(z)
    rstd = lax.rsqrt(_mean(zc * zc) + EPS)
    return zc * rstd, rstd


def _ln_bwd(dy, xhat, rstd, g):
    dxhat = dy * g
    dz = rstd * (dxhat - _mean(dxhat) - xhat * _mean(dxhat * xhat))
    return dz, jnp.sum(dy * xhat, axis=0, keepdims=True), jnp.sum(dy, axis=0, keepdims=True)


def _roll(x, shift):
    return pltpu.roll(x, shift, 1)


def _chunk_of(idx):
    return jnp.right_shift(idx, CHUNK.bit_length() - 1)


def _tile(n, cap, mult=LANES):
    if n <= cap:
        return n
    for t in range(cap - cap % mult, 0, -mult):
        if n % t == 0:
            return t
    return n


def _zero_map(nd, *_):
    return (0,) * nd


def _params(sem):
    return pltpu.CompilerParams(dimension_semantics=sem, vmem_limit_bytes=VMEM_LIMIT)


def _rowcall(name, body, n_rows, tm, row_ins, full_ins, row_outs, acc_outs=(), tiled_outs=(), tiled_ins=(),
             exchange=None):
    n_steps = n_rows // tm
    ex_srcs, broadcast = exchange if exchange else ((), False)
    n_ex = len(ex_srcs)
    n_in = len(row_ins) + len(tiled_ins) + len(full_ins)
    n_out = len(row_outs) + len(acc_outs) + len(tiled_outs)

    def kern(*refs):
        step = pl.program_id(0)
        ex_in, ex_out = refs[n_in:n_in + n_ex], refs[n_in + n_ex + n_out:n_in + 2 * n_ex + n_out]
        sems = refs[n_in + 2 * n_ex + n_out:]
        if n_ex:
            @pl.when(step == 0)
            def _():
                for send, _ in _chip_copies(ex_in, ex_out, *sems, broadcast):
                    send.start()

        body(step, *refs[:n_in], *refs[n_in + n_ex:n_in + n_ex + n_out])
        if n_ex:
            @pl.when(step == n_steps - 1)
            def _():
                _wait_copies(_chip_copies(ex_in, ex_out, *sems, broadcast))

    in_specs = [pl.BlockSpec((tm, a.shape[1]), lambda i: (i, 0)) for a in row_ins]
    in_specs += [spec for (_, spec) in tiled_ins]
    row_ins = list(row_ins) + [a for (a, _) in tiled_ins]
    in_specs += [pl.BlockSpec(a.shape, functools.partial(_zero_map, a.ndim), pipeline_mode=pl.Buffered(1))
                 for a in full_ins]
    in_specs += [HBM_SPEC] * n_ex
    out_specs = [pl.BlockSpec((tm, w), lambda i: (i, 0)) for (w, _) in row_outs]
    out_specs += [pl.BlockSpec(s, functools.partial(_zero_map, len(s))) for (s, _) in acc_outs]
    out_specs += [spec for (_, spec) in tiled_outs]
    out_specs += [HBM_SPEC] * n_ex
    out_shape = [jax.ShapeDtypeStruct((n_rows, w), dt) for (w, dt) in row_outs]
    out_shape += [jax.ShapeDtypeStruct(s, dt) for (s, dt) in acc_outs]
    out_shape += [shape for (shape, _) in tiled_outs]
    out_shape += _exchange_shapes(ex_srcs)
    return pl.pallas_call(kern, grid=(n_steps,), in_specs=in_specs, out_specs=out_specs, out_shape=out_shape,
                          scratch_shapes=_dma_sems(n_ex * N_PEER_CHIPS) if n_ex else [], name=name,
                          compiler_params=_params(("arbitrary",)))(*row_ins, *full_ins, *ex_srcs)


def _acc(step, ref, val):
    @pl.when(step == 0)
    def _():
        ref[...] = val

    @pl.when(step != 0)
    def _():
        ref[...] += val


def _ffn_fwd(name, x, w_in4, w_out, ln_g, ln_b, tm, exchange=None):
    T, D = x.shape
    fh = w_in4.shape[2]

    def body(i, x_ref, w4_ref, wo_ref, g_ref, b_ref, h_ref, z_ref, a_ref):
        xv = x_ref[...]
        xb = _bf(xv)
        f = jnp.zeros((tm, D), F32)
        for k in range(2):
            gk = _dot(xb, w4_ref[k])
            uk = _dot(xb, w4_ref[2 + k])
            a_ref[:, k * fh:(k + 1) * fh] = _bf(gk)
            a_ref[:, (2 + k) * fh:(3 + k) * fh] = _bf(uk)
            f += _dot(_bf(gk * _sigmoid(gk) * uk), wo_ref[k * fh:(k + 1) * fh, :])
        z = ALPHA * xv + 0.5 * f
        xhat, _ = _ln_stats(z)
        z_ref[...] = z
        h_ref[...] = xhat * g_ref[...] + b_ref[...]

    return _rowcall(name, body, T, tm, [x], [w_in4, w_out, ln_g, ln_b],
                    [(D, F32), (D, F32), (4 * fh, BF16)], exchange=exchange)


def _ffn_bwd(name, dh, z, a, w_in4, w_out, ln_g, tm, exchange=None):
    T, D = dh.shape
    fh = w_in4.shape[2]

    def body(i, dh_ref, z_ref, a_ref, w4_ref, wo_ref, g_ref, dx_ref, da_ref, s_ref, df_ref, dg_ref, db_ref):
        xhat, rstd = _ln_stats(z_ref[...])
        dz, dg, db = _ln_bwd(dh_ref[...], xhat, rstd, g_ref[...])
        _acc(i, dg_ref, dg)
        _acc(i, db_ref, db)
        dfb = _bf(0.5 * dz)
        df_ref[...] = dfb
        dx = ALPHA * dz
        for k in range(2):
            gk = a_ref[:, k * fh:(k + 1) * fh].astype(F32)
            uk = a_ref[:, (2 + k) * fh:(3 + k) * fh].astype(F32)
            ds = _dot_nt(dfb, wo_ref[k * fh:(k + 1) * fh, :])
            sig = _sigmoid(gk)
            silu = gk * sig
            dgk = _bf(ds * uk * sig * (1.0 + gk * (1.0 - sig)))
            duk = _bf(ds * silu)
            s_ref[:, k * fh:(k + 1) * fh] = _bf(silu * uk)
            da_ref[:, k * fh:(k + 1) * fh] = dgk
            da_ref[:, (2 + k) * fh:(3 + k) * fh] = duk
            dx += _dot_nt(dgk, w4_ref[k]) + _dot_nt(duk, w4_ref[2 + k])
        dx_ref[...] = dx

    return _rowcall(name, body, T, tm, [dh, z, a], [w_in4, w_out, ln_g],
                    [(D, F32), (4 * fh, BF16), (2 * fh, BF16), (D, BF16)],
                    [((1, D), F32), ((1, D), F32)], exchange=exchange)


WG_TILE_N = 1536


def _mm_tn(name, a, b, out_dtype=BF16, n_split=1):
    T, M = a.shape
    N = b.shape[1]
    tk = _tile(T, 2048, 8)
    tm = _tile(M, 1408)
    tn = _tile(N // n_split, WG_TILE_N)
    per = N // n_split // tn
    nk = T // tk
    if n_split > 1:
        out_spec = pl.BlockSpec((None, tm, tn), lambda i, j, k: (j // per, i, j % per))
        out_shape = jax.ShapeDtypeStruct((n_split, M, N // n_split), out_dtype)
    else:
        out_spec = pl.BlockSpec((tm, tn), lambda i, j, k: (i, j))
        out_shape = jax.ShapeDtypeStruct((M, N), out_dtype)

    def kern(a_ref, b_ref, o_ref, acc_ref):
        k = pl.program_id(2)
        part = _dot_tn(_bf(a_ref[...]), _bf(b_ref[...]))

        @pl.when(k == 0)
        def _():
            acc_ref[...] = part

        @pl.when(k != 0)
        def _():
            acc_ref[...] += part

        @pl.when(k == nk - 1)
        def _():
            o_ref[...] = acc_ref[...].astype(out_dtype)

    return pl.pallas_call(
        kern, grid=(M // tm, N // tn, nk),
        in_specs=[pl.BlockSpec((tk, tm), lambda i, j, k: (k, i)), pl.BlockSpec((tk, tn), lambda i, j, k: (k, j))],
        out_specs=out_spec, out_shape=out_shape,
        scratch_shapes=[pltpu.VMEM((tm, tn), F32)], name=name,
        compiler_params=_params(("arbitrary", "arbitrary", "arbitrary")))(a, b)


def _proj_ret(h1, w_r, cos_r, sin_r, tm):
    T, D = h1.shape
    qk = RET_HEADS * RET_DK
    rv = RET_HEADS * RET_DV

    def body(i, h_ref, cos_ref, sin_ref, w_ref, q_ref, k_ref, v_ref, g_ref):
        hb = _bf(h_ref[...])
        cos, sin = cos_ref[...], sin_ref[...]
        for out_ref, off, scale in ((q_ref, 0, 1.0), (k_ref, qk, RET_DK ** -0.5)):
            pr = _dot(hb, w_ref[:, off:off + qk])
            for h in range(RET_HEADS):
                t = pr[:, h * RET_DK:(h + 1) * RET_DK]
                out_ref[:, h * RET_DK:(h + 1) * RET_DK] = _bf((t * cos + _roll(t, RET_DK // 2) * sin) * scale)
        v_ref[...] = _bf(_dot(hb, w_ref[:, 2 * qk:2 * qk + rv]))
        g_ref[...] = _bf(_dot(hb, w_ref[:, 2 * qk + rv:2 * qk + 2 * rv]))

    return _rowcall("proj_ret", body, T, tm, [h1, cos_r, sin_r], [w_r],
                    [(qk, BF16), (qk, BF16), (rv, BF16), (rv, BF16)])


def _rope_pe(t, c, s1, s2):
    return t * c + _roll(t, LANES - MLA_ROPE // 2) * s1 + _roll(t, MLA_ROPE // 2) * s2


def _rope_pe_bwd(dy, c, s1, s2):
    return dy * c + _roll(dy * s1, MLA_ROPE // 2) + _roll(dy * s2, LANES - MLA_ROPE // 2)


def _rms(x, g):
    r = lax.rsqrt(_mean(x * x) + EPS)
    return x * r, r


def _attn_block(T):
    return min(512, T)


def _transposed_blocks(T, tm, w, dtype):
    tb = _attn_block(T)
    per = tb // tm
    return (jax.ShapeDtypeStruct((T // tb, MLA_HEADS, w, tb), dtype),
            pl.BlockSpec((None, MLA_HEADS, w, tm), lambda i: (i // per, 0, 0, i % per)))


ATTN_SCALE = (MLA_NOPE + MLA_ROPE) ** -0.5
LOG2E = 1.4426950408889634
Q_PRESCALE = ATTN_SCALE * LOG2E
V_ONES = 16


def _proj_mla(h1, tabs, w_c, w_kpe, w_g, w_uq, w_uk, w_uv, qn_g, kvn_g, tm):
    T, D = h1.shape
    H = MLA_HEADS

    def body(i, h_ref, c_ref, s1_ref, s2_ref, wc_ref, wk_ref, wg_ref, wuq_ref, wuk_ref, wuv_ref, qg_ref, kg_ref,
             lat_ref, gt_ref, q_ref, k_ref, v_ref, ln_ref, qt_ref, kt_ref, vt_ref):
        hb = _bf(h_ref[...])
        c, s1, s2 = c_ref[...], s1_ref[...], s2_ref[...]
        lat = _dot(hb, wc_ref[...])
        lat_ref[...] = lat
        gt_ref[...] = _bf(_dot(hb, wg_ref[...]))
        cqn, _ = _rms(lat[:, :Q_LORA], None)
        ckn, _ = _rms(lat[:, Q_LORA:], None)
        cqn = _bf(cqn * qg_ref[...])
        ckn = _bf(ckn * kg_ref[...])
        ln_ref[:, :Q_LORA] = cqn
        ln_ref[:, Q_LORA:] = ckn
        q = _dot(cqn, wuq_ref[...])
        kn = _dot(ckn, wuk_ref[...])
        vv = _dot(ckn, wuv_ref[...])
        v_ref[...] = _bf(vv)
        kpe = _rope_pe(_dot(hb, wk_ref[...]), c, s1, s2)
        ones = jnp.ones((V_ONES, tm), BF16)
        for h in range(H):
            o = h * MLA_QK
            qh = jnp.concatenate([q[:, o:o + MLA_NOPE], _rope_pe(q[:, o + MLA_NOPE:o + MLA_QK], c, s1, s2)], axis=1)
            qh = qh * Q_PRESCALE
            kh = jnp.concatenate([kn[:, h * MLA_NOPE:(h + 1) * MLA_NOPE], kpe], axis=1)
            q_ref[:, o:o + MLA_QK] = _bf(qh)
            k_ref[:, o:o + MLA_QK] = _bf(kh)
            qt_ref[h] = _bf(qh.T)
            kt_ref[h] = _bf(kh.T)
            vt_ref[h] = jnp.concatenate([_bf(vv[:, h * MLA_DV:(h + 1) * MLA_DV].T), ones], axis=0)

    lat_w = Q_LORA + KV_LORA
    return _rowcall("proj_mla", body, T, tm, [h1, *tabs], [w_c, w_kpe, w_g, w_uq, w_uk, w_uv, qn_g, kvn_g],
                    [(lat_w, F32), (2 * D, BF16), (H * MLA_QK, BF16), (H * MLA_QK, BF16), (H * MLA_DV, BF16),
                     (lat_w, BF16)],
                    tiled_outs=[_transposed_blocks(T, tm, MLA_QK, BF16), _transposed_blocks(T, tm, MLA_QK, BF16),
                                _transposed_blocks(T, tm, MLA_DV + V_ONES, BF16)])


def _ret_block(T):
    return min(256, T)


RET_HEADS_PER_STEP = 8


def _ret_dmat(lg, bt):
    n = lax.broadcasted_iota(jnp.int32, (bt, bt), 0)
    m = lax.broadcasted_iota(jnp.int32, (bt, bt), 1)
    return jnp.where(_chunk_of(m) <= _chunk_of(n), jnp.exp(lg * jnp.abs(n - m).astype(F32)), 0.0)


def _ret_scan(name, per_head, lgam, ins, outs, rev):
    T = ins[0][0].shape[0]
    bt = _ret_block(T)
    nb = T // bt
    hps = min(RET_HEADS_PER_STEP, RET_HEADS)
    n_in, n_out = len(ins), len(outs)

    def kern(lg_ref, *refs):
        in_refs, out_refs = refs[:n_in], refs[n_in:n_in + n_out]
        state_ref, dmat_ref = refs[n_in + n_out:]

        @pl.when(pl.program_id(1) == 0)
        def _():
            state_ref[...] = jnp.zeros_like(state_ref)
            for hh in range(hps):
                dmat_ref[hh] = _ret_dmat(lg_ref[hh][:, :1], bt)

        pos = lax.broadcasted_iota(jnp.int32, (bt, 1), 0).astype(F32)
        for hh in range(hps):
            lg = lg_ref[hh][:, :1]
            xi, zeta, gb = jnp.exp(lg * (pos + 1.0)), jnp.exp(lg * (bt - 1.0 - pos)), jnp.exp(lg * bt)
            tiles = [r[:, hh * w:(hh + 1) * w] for r, (_, w) in zip(in_refs, ins)]
            res = per_head(dmat_ref[hh], xi, zeta, gb, state_ref.at[hh], *tiles)
            for o_ref, (w, _), val in zip(out_refs, outs, res):
                o_ref[:, hh * w:(hh + 1) * w] = val.astype(o_ref.dtype)

    def blk(w):
        if rev:
            return pl.BlockSpec((bt, hps * w), lambda g, b: (nb - 1 - b, g))
        return pl.BlockSpec((bt, hps * w), lambda g, b: (b, g))

    return pl.pallas_call(
        kern, grid=(RET_HEADS // hps, nb),
        in_specs=[pl.BlockSpec((hps, 1, LANES), lambda g, b: (g, 0, 0))] + [blk(w) for _, w in ins],
        out_specs=[blk(w) for w, _ in outs],
        out_shape=[jax.ShapeDtypeStruct((T, RET_HEADS * w), dt) for w, dt in outs],
        scratch_shapes=[pltpu.VMEM((hps, RET_DK, RET_DV), F32), pltpu.VMEM((hps, bt, bt), F32)], name=name,
        compiler_params=_params(("arbitrary", "arbitrary")))(lgam, *[a for a, _ in ins])


def _ret_fwd(rq, rk, rv, lgam):
    def per_head(dmat, xi, zeta, gb, s_ref, q, k, v):
        sc = _dot_nt(q, k) * dmat
        y = _dot(_bf(sc), v) + _dot(q, _bf(s_ref[...])) * xi
        s_ref[...] = s_ref[...] * gb + _dot_tn(_bf(k.astype(F32) * zeta), v)
        return (y,)

    return _ret_scan("ret_fwd", per_head, lgam, [(rq, RET_DK), (rk, RET_DK), (rv, RET_DV)], [(RET_DV, BF16)], False)[0]


def _ret_bwd_q(rq, rk, rv, dy, lgam):
    def per_head(dmat, xi, zeta, gb, s_ref, k, v, dy):
        dp = _dot_nt(dy, v) * dmat
        dq = _dot(_bf(dp), k) + _dot_nt(dy, _bf(s_ref[...])) * xi
        s_ref[...] = s_ref[...] * gb + _dot_tn(_bf(k.astype(F32) * zeta), v)
        return (dq,)

    return _ret_scan("ret_bwd_q", per_head, lgam, [(rk, RET_DK), (rv, RET_DV), (dy, RET_DV)], [(RET_DK, F32)], False)[0]


def _ret_bwd_kv(rq, rk, rv, dy, lgam):
    def per_head(dmat, xi, zeta, gb, g_ref, q, k, v, dy):
        gs = _bf(g_ref[...])
        p = _dot_nt(q, k) * dmat
        dp = _dot_nt(dy, v) * dmat
        dv = _dot_tn(_bf(p), dy) + _dot(k, gs) * zeta
        dk = _dot_tn(_bf(dp), q) + _dot_nt(v, gs) * zeta
        g_ref[...] = g_ref[...] * gb + _dot_tn(_bf(q.astype(F32) * xi), dy)
        return dk, dv

    return _ret_scan("ret_bwd_kv", per_head, lgam, [(rq, RET_DK), (rk, RET_DK), (rv, RET_DV), (dy, RET_DV)],
                     [(RET_DK, F32), (RET_DV, BF16)], True)


def _attn_mask_t(tb):
    key = lax.broadcasted_iota(jnp.int32, (tb, tb), 0)
    qry = lax.broadcasted_iota(jnp.int32, (tb, tb), 1)
    return _chunk_of(key) <= _chunk_of(qry)


MASKED = -1e30
SUBLANES = 8


def _head_blocks(nb, w, tb):
    return pl.BlockSpec((nb, None, w, tb), lambda h, i: (0, h, 0, 0))


def _one_block(w, tb):
    return pl.BlockSpec((None, None, w, tb), lambda h, i: (i, h, 0, 0))


def _attn_fwd(k, qt, vt, exchange=()):
    T = k.shape[0]
    tb = _attn_block(T)
    nb = T // tb

    n_ex = len(exchange)

    def kern(qt_ref, k_ref, vt_ref, *refs):
        ex_in, (o_ref, lser_ref), ex_out = refs[:n_ex], refs[n_ex:n_ex + 2], refs[n_ex + 2:2 * n_ex + 2]
        m_ref, acc_ref, sa_ref, sb_ref = refs[2 * n_ex + 2:2 * n_ex + 6]
        sems = refs[2 * n_ex + 6:]
        qb = pl.program_id(1)
        first = jnp.logical_and(pl.program_id(0) == 0, qb == 0)
        last = jnp.logical_and(pl.program_id(0) == MLA_HEADS - 1, qb == nb - 1)
        if n_ex:
            @pl.when(first)
            def _():
                for send, _ in _chip_copies(ex_in, ex_out, *sems, True):
                    send.start()

        qt = qt_ref[...]
        m_ref[...] = jnp.full_like(m_ref, MASKED)
        acc_ref[...] = jnp.zeros_like(acc_ref)

        def scores(kb):
            rows = pl.ds(pl.multiple_of(kb * tb, tb), tb)
            return _dot(k_ref[rows, :], qt)

        def update(s, kb):
            m_old = m_ref[...]
            m_new = jnp.maximum(m_old, jnp.max(s, axis=0, keepdims=True))
            p = jnp.exp2(s - m_new)
            acc_ref[...] = acc_ref[...] * jnp.exp2(m_old - m_new) + _dot(vt_ref[kb], _bf(p))
            m_ref[...] = m_new

        def masked(s):
            return jnp.where(_attn_mask_t(tb), s, MASKED)

        sa_ref[...] = scores(0)

        def pair_body(j, carry):
            sb_ref[...] = scores(2 * j + 1)
            update(sa_ref[...], 2 * j)
            sa_ref[...] = scores(2 * j + 2)
            update(sb_ref[...], 2 * j + 1)
            return carry

        lax.fori_loop(0, qb // 2, pair_body, 0)

        @pl.when(qb % 2 == 0)
        def _():
            update(masked(sa_ref[...]), qb)

        @pl.when(qb % 2 == 1)
        def _():
            sb_ref[...] = masked(scores(qb))
            update(sa_ref[...], qb - 1)
            update(sb_ref[...], qb)

        l = acc_ref[MLA_DV:MLA_DV + 1, :]
        o_ref[...] = (acc_ref[:MLA_DV, :] / l).T
        lser_ref[...] = jnp.broadcast_to(m_ref[...] + jnp.log2(l), (SUBLANES, tb))
        if n_ex:
            @pl.when(last)
            def _():
                _wait_copies(_chip_copies(ex_in, ex_out, *sems, True))

    return pl.pallas_call(
        kern, grid=(MLA_HEADS, nb),
        in_specs=[_one_block(MLA_QK, tb), pl.BlockSpec((T, MLA_QK), lambda h, i: (0, h)),
                  _head_blocks(nb, MLA_DV + V_ONES, tb)] + [HBM_SPEC] * n_ex,
        out_specs=[pl.BlockSpec((tb, MLA_DV), lambda h, i: (i, h)), _one_block(SUBLANES, tb)] + [HBM_SPEC] * n_ex,
        out_shape=[jax.ShapeDtypeStruct((T, MLA_HEADS * MLA_DV), F32),
                   jax.ShapeDtypeStruct((nb, MLA_HEADS, SUBLANES, tb), F32)] + _exchange_shapes(exchange),
        scratch_shapes=[pltpu.VMEM((1, tb), F32), pltpu.VMEM((MLA_DV + V_ONES, tb), F32),
                        pltpu.VMEM((tb, tb), F32), pltpu.VMEM((tb, tb), F32)]
        + (_dma_sems(n_ex * N_PEER_CHIPS) if n_ex else []),
        name="attn_fwd", compiler_params=_params(("arbitrary", "arbitrary")))(qt, k, vt, *exchange)


def _attn_bwd(q, k, v, do, qt, kt, dot_, lse_rows, delta_rows):
    T = q.shape[0]
    tb = _attn_block(T)
    nb = T // tb

    def kern(q_ref, k_ref, v_ref, do_ref, qt_ref, kt_ref, dot_ref, lse_ref, dl_ref, dk_ref, dv_ref, dqt_ref, dv_acc,
             sa_ref, pa_ref, sb_ref, pb_ref):
        kb = pl.program_id(1)
        kv, vv, ktv = k_ref[...], v_ref[...], kt_ref[...]
        dk_ref[...] = jnp.zeros_like(dk_ref)
        dv_acc[...] = jnp.zeros_like(dv_acc)

        @pl.when(kb == 0)
        def _():
            dqt_ref[...] = jnp.zeros_like(dqt_ref)

        def products(qb, s_ref, dp_ref, diagonal=False):
            s = _dot(kv, qt_ref[qb])
            s_ref[...] = jnp.where(_attn_mask_t(tb), s, MASKED) if diagonal else s
            dp_ref[...] = _dot(vv, dot_ref[qb])

        def consume(qb, s_ref, dp_ref):
            rows = pl.ds(pl.multiple_of(qb * tb, tb), tb)
            p = jnp.exp2(s_ref[...] - lse_ref[qb][:1, :])
            dv_acc[...] += _dot(_bf(p), do_ref[rows, :])
            ds = _bf(p * (dp_ref[...] - dl_ref[qb][:1, :]))
            dk_ref[...] += _dot(ds, q_ref[rows, :])
            dqt_ref[qb] += _dot(ktv, ds)

        n_full = nb - 1 - kb
        products(kb, sa_ref, pa_ref, diagonal=True)

        def pair_body(j, carry):
            q1 = kb + 1 + 2 * j
            products(q1, sb_ref, pb_ref)
            consume(q1 - 1, sa_ref, pa_ref)
            products(q1 + 1, sa_ref, pa_ref)
            consume(q1, sb_ref, pb_ref)
            return carry

        lax.fori_loop(0, n_full // 2, pair_body, 0)

        @pl.when(n_full % 2 == 0)
        def _():
            consume(nb - 1, sa_ref, pa_ref)

        @pl.when(n_full % 2 == 1)
        def _():
            products(nb - 1, sb_ref, pb_ref)
            consume(nb - 2, sa_ref, pa_ref)
            consume(nb - 1, sb_ref, pb_ref)

        dk_ref[...] = dk_ref[...] * (ATTN_SCALE / Q_PRESCALE)
        dv_ref[...] = _bf(dv_acc[...])

    def blk(w):
        return pl.BlockSpec((tb, w), lambda h, i: (i, h))

    def full(w):
        return pl.BlockSpec((T, w), lambda h, i: (0, h))

    return pl.pallas_call(
        kern, grid=(MLA_HEADS, nb),
        in_specs=[full(MLA_QK), blk(MLA_QK), blk(MLA_DV), full(MLA_DV), _head_blocks(nb, MLA_QK, tb),
                  _one_block(MLA_QK, tb), _head_blocks(nb, MLA_DV, tb), _head_blocks(nb, SUBLANES, tb),
                  _head_blocks(nb, SUBLANES, tb)],
        out_specs=[blk(MLA_QK), blk(MLA_DV), _head_blocks(nb, MLA_QK, tb)],
        out_shape=[jax.ShapeDtypeStruct((T, MLA_HEADS * MLA_QK), F32),
                   jax.ShapeDtypeStruct((T, MLA_HEADS * MLA_DV), BF16),
                   jax.ShapeDtypeStruct((nb, MLA_HEADS, MLA_QK, tb), F32)],
        scratch_shapes=[pltpu.VMEM((tb, MLA_DV), F32)] + [pltpu.VMEM((tb, tb), F32)] * 4,
        name="attn_bwd", compiler_params=_params(("arbitrary", "arbitrary")))(
            q, k, v, do, qt, kt, dot_, lse_rows, delta_rows)


def _group_norm(y):
    yc = y - _mean(y)
    rstd = lax.rsqrt(_mean(yc * yc) + EPS)
    return yc * rstd, rstd


def _mix_fwd(y, rg, o, gates, h1, gn_g, w_ret_o, w_mla_o, w_out, ln_g, ln_b, tm):
    T, D = h1.shape

    def body(i, y_ref, rg_ref, o_ref, gt_ref, h_ref, gn_ref, wr_ref, wm_ref, wo_ref, g_ref, b_ref,
             h2_ref, z_ref, yret_ref, ymla_ref, yr_ref, mix_ref):
        for h in range(RET_HEADS):
            sl = slice(h * RET_DV, (h + 1) * RET_DV)
            yn, _ = _group_norm(y_ref[:, sl].astype(F32))
            r = rg_ref[:, sl].astype(F32)
            yr_ref[:, sl] = _bf(r * _sigmoid(r) * (yn * gn_ref[:, sl]))
        yret = _dot(yr_ref[...], wr_ref[...])
        ymla = _dot(_bf(o_ref[...]), wm_ref[...])
        yret_ref[...] = _bf(yret)
        ymla_ref[...] = _bf(ymla)
        mix = _bf(_sigmoid(gt_ref[:, :D].astype(F32)) * yret + _sigmoid(gt_ref[:, D:].astype(F32)) * ymla)
        mix_ref[...] = mix
        z = ALPHA * h_ref[...] + _dot(mix, wo_ref[...])
        xhat, _ = _ln_stats(z)
        z_ref[...] = z
        h2_ref[...] = xhat * g_ref[...] + b_ref[...]

    return _rowcall("mix_fwd", body, T, tm, [y, rg, o, gates, h1], [gn_g, w_ret_o, w_mla_o, w_out, ln_g, ln_b],
                    [(D, F32), (D, F32), (D, BF16), (D, BF16), (RET_HEADS * RET_DV, BF16), (D, BF16)])


def _mix_bwd(dh2, z1, gates, yret, ymla, y, rg, o, gn_g, w_ret_o, w_mla_o, w_out, ln_g, tm, exchange=None):
    T, D = dh2.shape
    rv = RET_HEADS * RET_DV

    def body(i, dh_ref, z_ref, gt_ref, yret_ref, ymla_ref, y_ref, rg_ref, o_ref, gn_ref, wr_ref, wm_ref, wo_ref, g_ref,
             dz_ref, dgt_ref, drg_ref, dy_ref, do_ref, dyret_ref, dymla_ref, dg_ref, db_ref, dgn_ref, dot_ref,
             dl_ref):
        xhat, rstd = _ln_stats(z_ref[...])
        dz, dg, db = _ln_bwd(dh_ref[...], xhat, rstd, g_ref[...])
        _acc(i, dg_ref, dg)
        _acc(i, db_ref, db)
        dz_ref[...] = dz
        dmix = _dot_nt(_bf(dz), wo_ref[...])
        sr = _sigmoid(gt_ref[:, :D].astype(F32))
        sm = _sigmoid(gt_ref[:, D:].astype(F32))
        dgt_ref[:, :D] = _bf(dmix * yret_ref[...].astype(F32) * sr * (1.0 - sr))
        dgt_ref[:, D:] = _bf(dmix * ymla_ref[...].astype(F32) * sm * (1.0 - sm))
        dyret = _bf(dmix * sr)
        dymla = _bf(dmix * sm)
        dyret_ref[...] = dyret
        dymla_ref[...] = dymla
        dov = _dot_nt(dymla, wm_ref[...])
        do_ref[...] = _bf(dov)
        for h in range(MLA_HEADS):
            sl = slice(h * MLA_DV, (h + 1) * MLA_DV)
            dot_ref[h] = _bf(dov[:, sl].T)
            delta = jnp.sum(dov[:, sl] * o_ref[:, sl], axis=-1, keepdims=True)
            dl_ref[h] = jnp.broadcast_to(delta, (tm, LANES)).T[:SUBLANES, :]
        dyr = _dot_nt(dyret, wr_ref[...])
        dgn = []
        for h in range(RET_HEADS):
            sl = slice(h * RET_DV, (h + 1) * RET_DV)
            yn, grstd = _group_norm(y_ref[:, sl].astype(F32))
            r = rg_ref[:, sl].astype(F32)
            sig = _sigmoid(r)
            d = dyr[:, sl]
            drg_ref[:, sl] = _bf(d * (yn * gn_ref[:, sl]) * sig * (1.0 + r * (1.0 - sig)))
            dt = d * (r * sig)
            dgn.append(jnp.sum(dt * yn, axis=0, keepdims=True))
            dyn = dt * gn_ref[:, sl]
            dy_ref[:, sl] = _bf(grstd * (dyn - _mean(dyn) - yn * _mean(dyn * yn)))
        _acc(i, dgn_ref, jnp.concatenate(dgn, axis=1))

    return _rowcall("mix_bwd", body, T, tm, [dh2, z1, gates, yret, ymla, y, rg, o],
                    [gn_g, w_ret_o, w_mla_o, w_out, ln_g],
                    [(D, F32), (2 * D, BF16), (rv, BF16), (rv, BF16), (MLA_HEADS * MLA_DV, BF16), (D, BF16), (D, BF16)],
                    [((1, D), F32), ((1, D), F32), ((1, rv), F32)],
                    tiled_outs=[_transposed_blocks(T, tm, MLA_DV, BF16), _transposed_blocks(T, tm, SUBLANES, F32)],
                    exchange=exchange)


def _proj_mla_bwd(dqt, dk, dv, lat, tabs, w_uq, w_uk, w_uv, qn_g, kvn_g, tm):
    T = dk.shape[0]
    H = MLA_HEADS
    lat_w = Q_LORA + KV_LORA

    def body(i, dk_ref, dv_ref, lat_ref, c_ref, s1_ref, s2_ref, dqt_ref, wuq_ref, wuk_ref, wuv_ref, qg_ref, kg_ref,
             dlat_ref, dkpe_ref, dqb_ref, dkn_ref, dqg_ref, dkg_ref):
        c, s1, s2 = c_ref[...], s1_ref[...], s2_ref[...]
        dkpe = jnp.zeros((tm, LANES), F32)
        for h in range(H):
            o = h * MLA_QK
            dqh = dqt_ref[h].T * ATTN_SCALE
            dqb_ref[:, o:o + MLA_NOPE] = _bf(dqh[:, :MLA_NOPE])
            dqb_ref[:, o + MLA_NOPE:o + MLA_QK] = _bf(_rope_pe_bwd(dqh[:, MLA_NOPE:], c, s1, s2))
            dkn_ref[:, h * MLA_NOPE:(h + 1) * MLA_NOPE] = _bf(dk_ref[:, o:o + MLA_NOPE])
            dkpe += dk_ref[:, o + MLA_NOPE:o + MLA_QK]
        dkn_ref[:, H * MLA_NOPE:] = dv_ref[...]
        dkpe_ref[...] = _bf(_rope_pe_bwd(dkpe, c, s1, s2))
        dcqn = _dot_nt(dqb_ref[...], wuq_ref[...])
        dckn = _dot_nt(dkn_ref[:, :H * MLA_NOPE], wuk_ref[...]) + _dot_nt(dv_ref[...], wuv_ref[...])
        for dn, x, g_ref, dg_ref, sl in ((dcqn, lat_ref[:, :Q_LORA], qg_ref, dqg_ref, slice(0, Q_LORA)),
                                         (dckn, lat_ref[:, Q_LORA:], kg_ref, dkg_ref, slice(Q_LORA, lat_w))):
            xn, r = _rms(x, None)
            _acc(i, dg_ref, jnp.sum(dn * xn, axis=0, keepdims=True))
            dxn = dn * g_ref[...]
            dlat_ref[:, sl] = _bf(r * (dxn - xn * _mean(dxn * xn)))

    dqt_shape, dqt_spec = _transposed_blocks(T, tm, MLA_QK, F32)
    assert dqt.shape == dqt_shape.shape
    return _rowcall("proj_mla_bwd", body, T, tm, [dk, dv, lat, *tabs], [w_uq, w_uk, w_uv, qn_g, kvn_g],
                    [(lat_w, BF16), (LANES, BF16), (H * MLA_QK, BF16), (H * (MLA_NOPE + MLA_DV), BF16)],
                    [((1, Q_LORA), F32), ((1, KV_LORA), F32)], tiled_ins=[(dqt, dqt_spec)])


def _proj_bwd(drq, drk, drv, drg, dz1, dlat, dkpe, dgates, cos_r, sin_r, w_r, w_c, w_kpe, w_g, tm):
    T, D = dz1.shape
    qk = RET_HEADS * RET_DK
    rv = RET_HEADS * RET_DV
    o_lat = 2 * qk + 2 * rv
    o_kpe = o_lat + dlat.shape[1]
    o_gate = o_kpe + LANES
    o_end = o_gate + dgates.shape[1]
    width = -(-o_end // WG_TILE_N) * WG_TILE_N

    def body(i, drq_ref, drk_ref, drv_ref, drg_ref, dz_ref, dlat_ref, dkpe_ref, dgt_ref, cos_ref, sin_ref,
             wr_ref, wc_ref, wk_ref, wg_ref, dh_ref, dpr_ref):
        cos, sin = cos_ref[...], sin_ref[...]
        for src, off, scale in ((drq_ref, 0, 1.0), (drk_ref, qk, RET_DK ** -0.5)):
            for h in range(RET_HEADS):
                d = src[:, h * RET_DK:(h + 1) * RET_DK]
                dpr_ref[:, off + h * RET_DK:off + (h + 1) * RET_DK] = _bf(
                    (d * cos + _roll(d * sin, RET_DK // 2)) * scale)
        dpr_ref[:, 2 * qk:2 * qk + rv] = drv_ref[...]
        dpr_ref[:, 2 * qk + rv:o_lat] = drg_ref[...]
        dpr_ref[:, o_lat:o_kpe] = dlat_ref[...]
        dpr_ref[:, o_kpe:o_gate] = dkpe_ref[...]
        dpr_ref[:, o_gate:o_end] = dgt_ref[...]
        if width > o_end:
            dpr_ref[:, o_end:] = jnp.zeros((tm, width - o_end), BF16)
        dh_ref[...] = (ALPHA * dz_ref[...] + _dot_nt(dpr_ref[:, :o_lat], wr_ref[...])
                       + _dot_nt(dlat_ref[...], wc_ref[...]) + _dot_nt(dkpe_ref[...], wk_ref[...])
                       + _dot_nt(dgt_ref[...], wg_ref[...]))

    return _rowcall("proj_bwd", body, T, tm, [drq, drk, drv, drg, dz1, dlat, dkpe, dgates, cos_r, sin_r],
                    [w_r, w_c, w_kpe, w_g], [(D, F32), (width, BF16)])


def _ple_loss(h3, p, target, w_gate, w_proj, ln_g, ln_b, tm):
    T, D = h3.shape

    def body(i, h_ref, p_ref, t_ref, wg_ref, wp_ref, g_ref, b_ref, dh_ref, dgp_ref, dpp_ref, loss_ref, dg_ref, db_ref):
        hv = h_ref[...]
        sg = _sigmoid(_dot(_bf(hv), wg_ref[...]))
        pp = _dot(_bf(p_ref[...]), wp_ref[...])
        xhat, rstd = _ln_stats(ALPHA * hv + sg * pp)
        err = xhat * g_ref[...] + b_ref[...] - t_ref[...]
        row_loss = 0.5 * _mean(err * err)
        _acc(i, loss_ref, jnp.broadcast_to(jnp.sum(row_loss, axis=0, keepdims=True), (1, LANES)))
        dz, dg, db = _ln_bwd(err * (1.0 / D), xhat, rstd, g_ref[...])
        _acc(i, dg_ref, dg)
        _acc(i, db_ref, db)
        dgp = _bf(dz * pp * sg * (1.0 - sg))
        dgp_ref[...] = dgp
        dpp_ref[...] = _bf(dz * sg)
        dh_ref[...] = ALPHA * dz + _dot_nt(dgp, wg_ref[...])

    return _rowcall("ple_loss", body, T, tm, [h3, p, target], [w_gate, w_proj, ln_g, ln_b],
                    [(D, F32), (D, BF16), (D, BF16)], [((1, LANES), F32), ((1, D), F32), ((1, D), F32)])


def _ewise(name, fn, ins, n_out, out_dtype=F32):
    r, c = ins[0].shape
    tr = _tile(r, max(8, (1 << 19) // c // 8 * 8), 8)

    def kern(*refs):
        outs = fn(*[x[...] for x in refs[:len(ins)]])
        for o_ref, o in zip(refs[len(ins):], outs):
            o_ref[...] = o.astype(out_dtype)

    spec = pl.BlockSpec((tr, c), lambda i: (i, 0))
    return pl.pallas_call(kern, grid=(r // tr,), in_specs=[spec] * len(ins), out_specs=[spec] * n_out,
                          out_shape=[jax.ShapeDtypeStruct((r, c), out_dtype)] * n_out, name=name,
                          compiler_params=_params(("arbitrary",)))(*ins)


def _adamw_math(w, g, m, v):
    m = ADAM_B1 * m + (1.0 - ADAM_B1) * g
    v = ADAM_B2 * v + (1.0 - ADAM_B2) * (g * g)
    m_hat = m / (1.0 - ADAM_B1 ** ADAM_STEP)
    v_hat = v / (1.0 - ADAM_B2 ** ADAM_STEP)
    return -ADAM_LR * (m_hat / (jnp.sqrt(v_hat) + ADAM_EPS) + ADAM_WD * w), m, v


def _adamw(name, w, g, m, v):
    shape = w.shape
    c = shape[-1]
    flat = [t.reshape(-1, c) for t in (w, g, m, v)]
    return [t.reshape(shape) for t in _ewise(name, _adamw_math, flat, 3)]


def _place():
    return lax.axis_index("x"), lax.axis_index("y"), lax.axis_index("c")


def _dma_sems(n):
    return [pltpu.SemaphoreType.DMA((n,)), pltpu.SemaphoreType.DMA((n,))]


N_PEER_CHIPS = N_CHIPS - 1


def _chips_exchange(name, srcs, broadcast):
    n = len(srcs)

    def kern(*refs):
        cps = _chip_copies(refs[:n], refs[n:2 * n], refs[2 * n], refs[2 * n + 1], broadcast)
        for send, _ in cps:
            send.start()
        _wait_copies(cps)

    return pl.pallas_call(
        kern, out_shape=_exchange_shapes(srcs), in_specs=[HBM_SPEC] * n, out_specs=[HBM_SPEC] * n,
        scratch_shapes=_dma_sems(n * N_PEER_CHIPS), name=name)(*srcs)


def _exchange_shapes(srcs):
    return [jax.ShapeDtypeStruct((N_CHIPS,) + s.shape[1:], s.dtype) for s in srcs]


def _chip_copies(src_refs, out_refs, send_sems, recv_sems, broadcast):
    x, y, c = _place()
    me = 2 * x + y
    peers = [(1 - x, y), (x, 1 - y), (1 - x, 1 - y)]
    cps = []
    for j, (px, py) in enumerate(peers):
        for a, (src_ref, out_ref) in enumerate(zip(src_refs, out_refs)):
            piece = src_ref.at[c] if broadcast else src_ref.at[2 * px + py]

            def copy(slot):
                return pltpu.make_async_remote_copy(
                    src_ref=piece, dst_ref=out_ref.at[slot], send_sem=send_sems.at[a * N_PEER_CHIPS + j],
                    recv_sem=recv_sems.at[a * N_PEER_CHIPS + j], device_id=(px, py, c), device_id_type=MESH)

            cps.append((copy(me), copy(2 * px + py)))
    return cps


def _wait_copies(cps):
    for _, landing in cps:
        landing.wait_recv()
    for send, _ in cps:
        send.wait_send()


def _sibling_swap(name, srcs, halves):
    n = len(srcs)

    def kern(*refs):
        src_refs, out_refs = refs[:n], refs[n:2 * n]
        send_sems, recv_sems = refs[2 * n:]
        x, y, c = _place()

        def copy(a):
            piece = src_refs[a].at[:, 1 - c] if halves else src_refs[a]
            return pltpu.make_async_remote_copy(
                src_ref=piece, dst_ref=out_refs[a], send_sem=send_sems.at[a], recv_sem=recv_sems.at[a],
                device_id=(x, y, 1 - c), device_id_type=MESH)

        cps = [copy(a) for a in range(n)]
        for cp in cps:
            cp.start()
        for cp in cps:
            cp.wait_recv()
        for cp in cps:
            cp.wait_send()

    def out_shape(s):
        return jax.ShapeDtypeStruct((s.shape[0],) + s.shape[2:] if halves else s.shape, s.dtype)

    return pl.pallas_call(
        kern, out_shape=[out_shape(s) for s in srcs], in_specs=[HBM_SPEC] * n, out_specs=[HBM_SPEC] * n,
        scratch_shapes=_dma_sems(n), name=name)(*srcs)


def _all_devices(name, src, reduce):
    r, c = src.shape
    n_dev = 2 * N_CHIPS

    def kern(src_ref, out_ref, *scratch):
        if reduce:
            gat_ref, send_sems, recv_sems = scratch
        else:
            gat_ref = out_ref
            send_sems, recv_sems = scratch
        x, y, cc = _place()
        me = 4 * x + 2 * y + cc
        gat_ref[me] = src_ref[...]
        peers = []
        for j in range(1, n_dev):
            px = 1 - x if j & 4 else x
            py = 1 - y if j & 2 else y
            pc = 1 - cc if j & 1 else cc
            peers.append((px, py, pc))

        def copy(j, peer, slot):
            return pltpu.make_async_remote_copy(
                src_ref=src_ref, dst_ref=gat_ref.at[slot], send_sem=send_sems.at[j], recv_sem=recv_sems.at[j],
                device_id=peer, device_id_type=MESH)

        sends = [copy(j, peer, me) for j, peer in enumerate(peers)]
        for cp in sends:
            cp.start()
        for j, (px, py, pc) in enumerate(peers):
            copy(j, (px, py, pc), 4 * px + 2 * py + pc).wait_recv()
        for cp in sends:
            cp.wait_send()
        if reduce:
            total = gat_ref[0]
            for d in range(1, n_dev):
                total = total + gat_ref[d]
            out_ref[...] = total

    out_shape = jax.ShapeDtypeStruct((r, c) if reduce else (n_dev, r, c), src.dtype)
    scratch = ([pltpu.VMEM((n_dev, r, c), src.dtype)] if reduce else []) + _dma_sems(n_dev - 1)
    return pl.pallas_call(kern, out_shape=out_shape, in_specs=[VMEM_SPEC], out_specs=VMEM_SPEC,
                          scratch_shapes=scratch, name=name)(src)


def _halves(t, axis):
    return t.reshape(t.shape[:axis] + (2, t.shape[axis] // 2) + t.shape[axis + 1:])


def _by_core(mine, theirs, axis):
    c = lax.axis_index("c")
    both = jnp.where(c == 0, jnp.stack([mine, theirs], axis), jnp.stack([theirs, mine], axis))
    return both.reshape(both.shape[:axis] + (2 * both.shape[axis + 1],) + both.shape[axis + 2:])


def _with_own(own, others):
    me = 2 * lax.axis_index("x") + lax.axis_index("y")
    is_me = (jnp.arange(N_CHIPS, dtype=jnp.int32) == me)[:, None, None]
    return jnp.where(is_me, own[None], others)


def _join_shards(name, shards):
    _, r, c = shards.shape
    if name in COL_SHARDED:
        return shards.transpose(1, 0, 2).reshape(r, N_CHIPS * c)
    return shards.reshape(N_CHIPS * r, c)


def _split_shards(name, full):
    if full.ndim == 3:
        return full
    r, c = full.shape
    if name in COL_SHARDED:
        return jnp.stack([full[:, k * (c // N_CHIPS):(k + 1) * (c // N_CHIPS)] for k in range(N_CHIPS)])
    return full.reshape(N_CHIPS, r // N_CHIPS, c)


def _rope_tables(positions):
    pos = positions.reshape(-1).astype(F32)[:, None]
    half = RET_DK // 2
    ang = pos * (ROPE_BASE ** (-jnp.arange(half, dtype=F32) / half))
    cos_r = jnp.concatenate([jnp.cos(ang)] * 2, axis=1)
    sin_r = jnp.concatenate([-jnp.sin(ang), jnp.sin(ang)], axis=1)
    half = MLA_ROPE // 2
    ang = pos * (ROPE_BASE ** (-jnp.arange(half, dtype=F32) / half))
    zeros = jnp.zeros_like(ang)
    rest = LANES - MLA_ROPE
    c = jnp.concatenate([jnp.cos(ang)] * 2 + [jnp.ones((ang.shape[0], rest), F32)], axis=1)
    s1 = jnp.concatenate([-jnp.sin(ang), zeros, jnp.zeros((ang.shape[0], rest), F32)], axis=1)
    s2 = jnp.concatenate([zeros, jnp.sin(ang), jnp.zeros((ang.shape[0], rest), F32)], axis=1)
    return cos_r, sin_r, (c, s1, s2)


GATHER_GROUPS = (("ffn1_w_in", "ffn1_w_out"), ("w_in", "w_uq", "w_ukv"),
                 ("w_ret_o", "w_mla_o", "w_out", "ffn2_w_in", "ffn2_w_out", "ple_w_gate", "ple_w_proj"))
REDUCE_GROUPS = (("ple_w_gate", "ple_w_proj", "ffn2_w_in", "ffn2_w_out"),
                 ("w_out", "w_ret_o", "w_mla_o", "w_uq", "w_ukv", "w_in"), ("ffn1_w_in", "ffn1_w_out"))


def _gathered(tag, names, own, mine):
    theirs = _sibling_swap("gather_cores_" + tag, mine, False)
    out = {}
    for n, m, t in zip(names, mine, theirs):
        full = _with_own(own[n], _by_core(m, t, 1))
        out[n] = full if n in ("ffn1_w_in", "ffn2_w_in") else _join_shards(n, full)
    return out


def _chip_sums(tag, names, grads):
    halves = [_halves(_split_shards(n, grads[n]), 1) for n in names]
    theirs = _sibling_swap("reduce_cores_" + tag, halves, True)

    def one(n, g, t):
        k, _, r, c = g.shape
        tr = _tile(r, max(8, (1 << 17) // c // 8 * 8), 8)

        def kern(g_ref, t_ref, o_ref):
            mine = jnp.where(lax.axis_index("c") == 0, g_ref[:, 0], g_ref[:, 1])
            o_ref[...] = _bf(mine.astype(F32) + t_ref[...].astype(F32))

        spec = pl.BlockSpec((k, tr, c), lambda i: (0, i, 0))
        return pl.pallas_call(kern, grid=(r // tr,),
                              in_specs=[pl.BlockSpec((k, 2, tr, c), lambda i: (0, 0, i, 0)), spec], out_specs=spec,
                              out_shape=jax.ShapeDtypeStruct((k, r, c), BF16), name="reduce_cores_add_" + n,
                              compiler_params=_params(("arbitrary",)))(g, t)

    return [one(n, g, t) for n, g, t in zip(names, halves, theirs)]


def _block_totals(names, sums, parts):
    def one(n, s, pt):
        _, r, c = s.shape
        tr = _tile(r, max(8, (1 << 17) // c // 8 * 8), 8)

        def kern(s_ref, p_ref, o_ref):
            me = 2 * lax.axis_index("x") + lax.axis_index("y")
            terms = [jnp.where(k == me, s_ref[k], p_ref[k]).astype(F32) for k in range(N_CHIPS)]
            o_ref[...] = ((terms[0] + terms[1]) + terms[2]) + terms[3]

        spec = pl.BlockSpec((N_CHIPS, tr, c), lambda i: (0, i, 0))
        return pl.pallas_call(kern, grid=(r // tr,), in_specs=[spec, spec],
                              out_specs=pl.BlockSpec((tr, c), lambda i: (i, 0)),
                              out_shape=jax.ShapeDtypeStruct((r, c), F32), name="reduce_chips_add_" + n,
                              compiler_params=_params(("arbitrary",)))(s, pt)

    return [one(n, s, pt) for n, s, pt in zip(names, sums, parts)]


def _local_step(x, p, positions, target, shards, ln_g, ln_b, gn_g, qn_g, kvn_g):
    T, D = x.shape
    tm = min(256, T)
    H = MLA_HEADS
    qk, rv = RET_HEADS * RET_DK, RET_HEADS * RET_DV
    cos_r, sin_r, tabs = _rope_tables(positions)
    lgam = jnp.broadcast_to(jnp.log(1.0 - 2.0 ** (-5.0 - jnp.arange(RET_HEADS, dtype=F32)))[:, None, None],
                            (RET_HEADS, 1, LANES))
    lng = [ln_g[k:k + 1] for k in range(N_LN)]
    lnb = [ln_b[k:k + 1] for k in range(N_LN)]
    own = {n: _bf(shards[n]) for n in BIG_WEIGHTS}
    to_send = [[_halves(own[n], 0) for n in names] for names in GATHER_GROUPS]

    w = _gathered("a", GATHER_GROUPS[0], own, _chips_exchange("gather_chips_a", to_send[0], True))
    h1, z0, a1, *arrived = _ffn_fwd("ffn1_fwd", x, w["ffn1_w_in"], w["ffn1_w_out"], lng[0], lnb[0], 2 * tm,
                                    exchange=(to_send[1], True))
    w.update(_gathered("b", GATHER_GROUPS[1], own, arrived))

    w_in = w["w_in"]
    o_lat, o_kpe, o_gate = 2 * qk + 2 * rv, 2 * qk + 2 * rv + Q_LORA + KV_LORA, 2 * qk + 2 * rv + Q_LORA + KV_LORA + MLA_ROPE
    w_r, w_c = w_in[:, :o_lat], w_in[:, o_lat:o_kpe]
    w_kpe = jnp.pad(w_in[:, o_kpe:o_gate], ((0, 0), (0, LANES - MLA_ROPE)))
    w_g = w_in[:, o_gate:]
    w_uq = jnp.pad(w["w_uq"].reshape(Q_LORA, H, MLA_NOPE + MLA_ROPE),
                   ((0, 0), (0, 0), (0, MLA_QK - MLA_NOPE - MLA_ROPE))).reshape(Q_LORA, H * MLA_QK)
    w_ukv = w["w_ukv"].reshape(KV_LORA, H, MLA_NOPE + MLA_DV)
    w_uk = w_ukv[:, :, :MLA_NOPE].reshape(KV_LORA, H * MLA_NOPE)
    w_uv = w_ukv[:, :, MLA_NOPE:].reshape(KV_LORA, H * MLA_DV)

    rq, rk, rvv, rg = _proj_ret(h1, w_r, cos_r, sin_r, 2 * tm)
    lat, gates, q, k, v, latn, qt, kt, vt = _proj_mla(h1, tabs, w_c, w_kpe, w_g, w_uq, w_uk, w_uv, qn_g, kvn_g, 2 * tm)
    y = _ret_fwd(rq, rk, rvv, lgam)
    o, lse_rows, *arrived = _attn_fwd(k, qt, vt, exchange=to_send[2])
    w.update(_gathered("c", GATHER_GROUPS[2], own, arrived))
    h2, z1, yret, ymla, yr, mix = _mix_fwd(y, rg, o, gates, h1, gn_g, w["w_ret_o"], w["w_mla_o"], w["w_out"],
                                           lng[1], lnb[1], 2 * tm)
    h3, z2, a2 = _ffn_fwd("ffn2_fwd", h2, w["ffn2_w_in"], w["ffn2_w_out"], lng[2], lnb[2], 2 * tm)

    dh3, dgp, dpp, loss, dg3, db3 = _ple_loss(h3, p, target, w["ple_w_gate"], w["ple_w_proj"], lng[3], lnb[3], 2 * tm)
    dh2, da2, s2, df2, dg2, db2 = _ffn_bwd("ffn2_bwd", dh3, z2, a2, w["ffn2_w_in"], w["ffn2_w_out"], lng[2], tm)
    grads = {"ple_w_gate": _mm_tn("wg_ple_gate", h3, dgp), "ple_w_proj": _mm_tn("wg_ple_proj", p, dpp),
             "ffn2_w_in": _mm_tn("wg_ffn2_in", h2, da2, n_split=N_CHIPS), "ffn2_w_out": _mm_tn("wg_ffn2_out", s2, df2)}
    sums1 = _chip_sums("1", REDUCE_GROUPS[0], grads)
    (dz1, dgates, drg, dy, do, dyret, dymla, dg1, db1, dgn, dot_, delta_rows, *parts1) = _mix_bwd(
        dh2, z1, gates, yret, ymla, y, rg, o, gn_g, w["w_ret_o"], w["w_mla_o"], w["w_out"], lng[1], tm,
        exchange=(sums1, False))
    drq = _ret_bwd_q(rq, rk, rvv, dy, lgam)
    drk, drv = _ret_bwd_kv(rq, rk, rvv, dy, lgam)
    dk, dv, dqt = _attn_bwd(q, k, v, do, qt, kt, dot_, lse_rows, delta_rows)
    dlat, dkpe, dqb, dkv, dqg, dkg = _proj_mla_bwd(dqt, dk, dv, lat, tabs, w_uq, w_uk, w_uv, qn_g, kvn_g, 2 * tm)
    dh1, dpr = _proj_bwd(drq, drk, drv, drg, dz1, dlat, dkpe, dgates, cos_r, sin_r, w_r, w_c, w_kpe, w_g, tm)
    g_uq = _mm_tn("wg_uq", latn[:, :Q_LORA], dqb).reshape(Q_LORA, H, MLA_QK)[:, :, :MLA_NOPE + MLA_ROPE]
    g_ukv = _mm_tn("wg_ukv", latn[:, Q_LORA:], dkv)
    g_uk = g_ukv[:, :H * MLA_NOPE].reshape(KV_LORA, H, MLA_NOPE)
    g_uv = g_ukv[:, H * MLA_NOPE:].reshape(KV_LORA, H, MLA_DV)
    g_in = _mm_tn("wg_in", h1, dpr)
    grads.update({
        "w_in": jnp.concatenate([g_in[:, :o_kpe + MLA_ROPE], g_in[:, o_kpe + LANES:o_kpe + LANES + 2 * D]], axis=1),
        "w_ret_o": _mm_tn("wg_ret_o", yr, dyret),
        "w_uq": g_uq.reshape(Q_LORA, H * (MLA_NOPE + MLA_ROPE)),
        "w_ukv": jnp.concatenate([g_uk, g_uv], axis=2).reshape(KV_LORA, H * (MLA_NOPE + MLA_DV)),
        "w_mla_o": _mm_tn("wg_mla_o", o, dymla),
        "w_out": _mm_tn("wg_out", mix, dz1)})
    sums2 = _chip_sums("2", REDUCE_GROUPS[1], grads)
    dx, da1, s1, df1, dg0, db0, *parts2 = _ffn_bwd("ffn1_bwd", dh1, z0, a1, w["ffn1_w_in"], w["ffn1_w_out"], lng[0], tm,
                                                   exchange=(sums2, False))
    grads.update({"ffn1_w_in": _mm_tn("wg_ffn1_in", x, da1, n_split=N_CHIPS),
                  "ffn1_w_out": _mm_tn("wg_ffn1_out", s1, df1)})
    sums3 = _chip_sums("3", REDUCE_GROUPS[2], grads)
    parts3 = _chips_exchange("reduce_chips_3", sums3, False)

    names = [n for group in REDUCE_GROUPS for n in group]
    totals = _block_totals(names, sums1 + sums2 + sums3, list(parts1) + list(parts2) + list(parts3))
    others = _sibling_swap("reduce_join", totals, False)
    reduced = {n: _by_core(t, o_, 0) for n, t, o_ in zip(names, totals, others)}
    small = {"ln_g": jnp.concatenate([dg0, dg1, dg2, dg3], axis=0), "ln_b": jnp.concatenate([db0, db1, db2, db3], axis=0),
             "ret_gn_g": dgn, "q_norm_g": dqg, "kv_norm_g": dkg}
    return loss[0, 0], dx, reduced, small


def kernel(x, p, positions, ln_g, ln_b, ffn1_w_in, ffn1_w_out, w_in, ret_gn_g, w_ret_o, q_norm_g, kv_norm_g, w_uq, w_ukv, w_mla_o, w_out, ffn2_w_in, ffn2_w_out, ple_w_gate, ple_w_proj, loss_target, m_ln_g, m_ln_b, m_ffn1_w_in, m_ffn1_w_out, m_w_in, m_ret_gn_g, m_w_ret_o, m_q_norm_g, m_kv_norm_g, m_w_uq, m_w_ukv, m_w_mla_o, m_w_out, m_ffn2_w_in, m_ffn2_w_out, m_ple_w_gate, m_ple_w_proj, v_ln_g, v_ln_b, v_ffn1_w_in, v_ffn1_w_out, v_w_in, v_ret_gn_g, v_w_ret_o, v_q_norm_g, v_kv_norm_g, v_w_uq, v_w_ukv, v_w_mla_o, v_w_out, v_ffn2_w_in, v_ffn2_w_out, v_ple_w_gate, v_ple_w_proj):
    names = ("ln_g", "ln_b", "ffn1_w_in", "ffn1_w_out", "w_in", "ret_gn_g", "w_ret_o", "q_norm_g", "kv_norm_g", "w_uq",
             "w_ukv", "w_mla_o", "w_out", "ffn2_w_in", "ffn2_w_out", "ple_w_gate", "ple_w_proj")
    weights = dict(zip(names, (ln_g, ln_b, ffn1_w_in, ffn1_w_out, w_in, ret_gn_g, w_ret_o, q_norm_g, kv_norm_g, w_uq,
                               w_ukv, w_mla_o, w_out, ffn2_w_in, ffn2_w_out, ple_w_gate, ple_w_proj)))
    m_in = dict(zip(names, (m_ln_g, m_ln_b, m_ffn1_w_in, m_ffn1_w_out, m_w_in, m_ret_gn_g, m_w_ret_o, m_q_norm_g,
                            m_kv_norm_g, m_w_uq, m_w_ukv, m_w_mla_o, m_w_out, m_ffn2_w_in, m_ffn2_w_out, m_ple_w_gate,
                            m_ple_w_proj)))
    v_in = dict(zip(names, (v_ln_g, v_ln_b, v_ffn1_w_in, v_ffn1_w_out, v_w_in, v_ret_gn_g, v_w_ret_o, v_q_norm_g,
                            v_kv_norm_g, v_w_uq, v_w_ukv, v_w_mla_o, v_w_out, v_ffn2_w_in, v_ffn2_w_out, v_ple_w_gate,
                            v_ple_w_proj)))
    chip = 2 * lax.axis_index("x") + lax.axis_index("y")
    D = x.shape[-1]
    dq = D // N_CHIPS

    shards = {n: weights[n][0] for n in BIG_WEIGHTS}
    ln_all = _all_devices("gather_ln", jnp.concatenate([ln_g[0], ln_b[0]], axis=0), False)
    ln_full = ln_all[::2].transpose(1, 0, 2).reshape(2 * N_LN, D)
    loss, dx, big, small = _local_step(x[0], p[0, 0], positions, loss_target[0], shards, ln_full[:N_LN],
                                       ln_full[N_LN:], ret_gn_g, q_norm_g, kv_norm_g)

    loss = lax.psum(loss, ("x", "y", "c"))
    small_names = ("ln_g", "ln_b", "ret_gn_g", "q_norm_g", "kv_norm_g")
    flat = jnp.concatenate([small[n].reshape(-1) for n in small_names])
    rows = -(-flat.shape[0] // LANES // 8) * 8
    flat = jnp.pad(flat, (0, rows * LANES - flat.shape[0])).reshape(rows, LANES)
    flat = _all_devices("reduce_small", flat, True).reshape(-1)
    off = 0
    for n in small_names:
        size = small[n].size
        small[n] = flat[off:off + size].reshape(small[n].shape)
        off += size
    g_out = dict(big)
    for n in ("ln_g", "ln_b"):
        g_out[n] = lax.dynamic_slice_in_dim(small[n], chip * dq, dq, axis=1)
    for n in ("ret_gn_g", "q_norm_g", "kv_norm_g"):
        g_out[n] = small[n]

    deltas, new_m, new_v = {}, {}, {}
    for n in names:
        g = g_out[n].reshape(weights[n].shape)
        g_out[n] = g
        deltas[n], new_m[n], new_v[n] = _adamw("adamw_" + n, weights[n], g, m_in[n], v_in[n])
    return (loss, dx[None], *[g_out[n] for n in names], *[deltas[n] for n in names], *[new_m[n] for n in names],
            *[new_v[n] for n in names])
```

```python
import functools

import jax
import jax.numpy as jnp
from jax import lax
from jax.experimental import pallas as pl
from jax.experimental.pallas import tpu as pltpu

CHUNK = 64
RET_HEADS = 8
RET_DK = 128
RET_DV = 256
MLA_HEADS = 8
MLA_NOPE = 128
MLA_ROPE = 64
MLA_DV = 128
MLA_QK = 256
Q_LORA = 256
KV_LORA = 256
ROPE_BASE = 10000.0
EPS = 1e-5
N_LN = 4
ALPHA = 2.0 ** 0.25
ADAM_LR = 0.001
ADAM_B1 = 0.9
ADAM_B2 = 0.999
ADAM_EPS = 1e-08
ADAM_WD = 0.01
ADAM_STEP = 10

LANES = 128
VMEM_LIMIT = 60 << 20
N_CHIPS = 4

F32 = jnp.float32
BF16 = jnp.bfloat16
MESH = pl.DeviceIdType.MESH
HBM_SPEC = pl.BlockSpec(memory_space=pltpu.HBM)
VMEM_SPEC = pl.BlockSpec(memory_space=pltpu.VMEM)

BIG_WEIGHTS = ("ffn1_w_in", "ffn1_w_out", "w_in", "w_ret_o", "w_uq", "w_ukv", "w_mla_o", "w_out",
               "ffn2_w_in", "ffn2_w_out", "ple_w_gate", "ple_w_proj")
COL_SHARDED = ("ffn1_w_in", "w_in", "w_uq", "w_ukv", "ffn2_w_in", "ple_w_proj")


def _dot(a, b):
    return jnp.dot(a, b, preferred_element_type=F32)


def _dot_nt(a, b):
    return lax.dot_general(a, b, (((1,), (1,)), ((), ())), preferred_element_type=F32)


def _dot_tn(a, b):
    return lax.dot_general(a, b, (((0,), (0,)), ((), ())), preferred_element_type=F32)


def _bf(x):
    return x.astype(BF16)


def _sigmoid(x):
    return 0.5 * jnp.tanh(0.5 * x) + 0.5


def _mean(x):
    return jnp.mean(x, axis=-1, keepdims=True)


def _ln_stats(z):
    zc = z - _mean(z)
    rstd = lax.rsqrt(_mean(zc * zc) + EPS)
    return zc * rstd, rstd


def _ln_bwd(dy, xhat, rstd, g):
    dxhat = dy * g
    dz = rstd * (dxhat - _mean(dxhat) - xhat * _mean(dxhat * xhat))
    return dz, jnp.sum(dy * xhat, axis=0, keepdims=True), jnp.sum(dy, axis=0, keepdims=True)


def _roll(x, shift):
    return pltpu.roll(x, shift, 1)


def _chunk_of(idx):
    return jnp.right_shift(idx, CHUNK.bit_length() - 1)


def _tile(n, cap, mult=LANES):
    if n <= cap:
        return n
    for t in range(cap - cap % mult, 0, -mult):
        if n % t == 0:
            return t
    return n


def _zero_map(nd, *_):
    return (0,) * nd


def _params(sem):
    return pltpu.CompilerParams(dimension_semantics=sem, vmem_limit_bytes=VMEM_LIMIT)


def _rowcall(name, body, n_rows, tm, row_ins, full_ins, row_outs, acc_outs=(), tiled_outs=(), tiled_ins=(),
             exchange=None):
    n_steps = n_rows // tm
    ex_srcs, broadcast = exchange if exchange else ((), False)
    n_ex = len(ex_srcs)
    n_in = len(row_ins) + len(tiled_ins) + len(full_ins)
    n_out = len(row_outs) + len(acc_outs) + len(tiled_outs)

    def kern(*refs):
        step = pl.program_id(0)
        ex_in, ex_out = refs[n_in:n_in + n_ex], refs[n_in + n_ex + n_out:n_in + 2 * n_ex + n_out]
        sems = refs[n_in + 2 * n_ex + n_out:]
        if n_ex:
            @pl.when(step == 0)
            def _():
                for send, _ in _chip_copies(ex_in, ex_out, *sems, broadcast):
                    send.start()

        body(step, *refs[:n_in], *refs[n_in + n_ex:n_in + n_ex + n_out])
        if n_ex:
            @pl.when(step == n_steps - 1)
            def _():
                _wait_copies(_chip_copies(ex_in, ex_out, *sems, broadcast))

    in_specs = [pl.BlockSpec((tm, a.shape[1]), lambda i: (i, 0)) for a in row_ins]
    in_specs += [spec for (_, spec) in tiled_ins]
    row_ins = list(row_ins) + [a for (a, _) in tiled_ins]
    in_specs += [pl.BlockSpec(a.shape, functools.partial(_zero_map, a.ndim), pipeline_mode=pl.Buffered(1))
                 for a in full_ins]
    in_specs += [HBM_SPEC] * n_ex
    out_specs = [pl.BlockSpec((tm, w), lambda i: (i, 0)) for (w, _) in row_outs]
    out_specs += [pl.BlockSpec(s, functools.partial(_zero_map, len(s))) for (s, _) in acc_outs]
    out_specs += [spec for (_, spec) in tiled_outs]
    out_specs += [HBM_SPEC] * n_ex
    out_shape = [jax.ShapeDtypeStruct((n_rows, w), dt) for (w, dt) in row_outs]
    out_shape += [jax.ShapeDtypeStruct(s, dt) for (s, dt) in acc_outs]
    out_shape += [shape for (shape, _) in tiled_outs]
    out_shape += _exchange_shapes(ex_srcs)
    return pl.pallas_call(kern, grid=(n_steps,), in_specs=in_specs, out_specs=out_specs, out_shape=out_shape,
                          scratch_shapes=_dma_sems(n_ex * N_PEER_CHIPS) if n_ex else [], name=name,
                          compiler_params=_params(("arbitrary",)))(*row_ins, *full_ins, *ex_srcs)


def _acc(step, ref, val):
    @pl.when(step == 0)
    def _():
        ref[...] = val

    @pl.when(step != 0)
    def _():
        ref[...] += val


def _ffn_fwd(name, x, w_in4, w_out, ln_g, ln_b, tm, exchange=None):
    T, D = x.shape
    fh = w_in4.shape[2]

    def body(i, x_ref, w4_ref, wo_ref, g_ref, b_ref, h_ref, z_ref, a_ref):
        xv = x_ref[...]
        xb = _bf(xv)
        f = jnp.zeros((tm, D), F32)
        for k in range(2):
            gk = _dot(xb, w4_ref[k])
            uk = _dot(xb, w4_ref[2 + k])
            a_ref[:, k * fh:(k + 1) * fh] = _bf(gk)
            a_ref[:, (2 + k) * fh:(3 + k) * fh] = _bf(uk)
            f += _dot(_bf(gk * _sigmoid(gk) * uk), wo_ref[k * fh:(k + 1) * fh, :])
        z = ALPHA * xv + 0.5 * f
        xhat, _ = _ln_stats(z)
        z_ref[...] = z
        h_ref[...] = xhat * g_ref[...] + b_ref[...]

    return _rowcall(name, body, T, tm, [x], [w_in4, w_out, ln_g, ln_b],
                    [(D, F32), (D, F32), (4 * fh, BF16)], exchange=exchange)


def _ffn_bwd(name, dh, z, a, w_in4, w_out, ln_g, tm, exchange=None):
    T, D = dh.shape
    fh = w_in4.shape[2]

    def body(i, dh_ref, z_ref, a_ref, w4_ref, wo_ref, g_ref, dx_ref, da_ref, s_ref, df_ref, dg_ref, db_ref):
        xhat, rstd = _ln_stats(z_ref[...])
        dz, dg, db = _ln_bwd(dh_ref[...], xhat, rstd, g_ref[...])
        _acc(i, dg_ref, dg)
        _acc(i, db_ref, db)
        dfb = _bf(0.5 * dz)
        df_ref[...] = dfb
        dx = ALPHA * dz
        for k in range(2):
            gk = a_ref[:, k * fh:(k + 1) * fh].astype(F32)
            uk = a_ref[:, (2 + k) * fh:(3 + k) * fh].astype(F32)
            ds = _dot_nt(dfb, wo_ref[k * fh:(k + 1) * fh, :])
            sig = _sigmoid(gk)
            silu = gk * sig
            dgk = _bf(ds * uk * sig * (1.0 + gk * (1.0 - sig)))
            duk = _bf(ds * silu)
            s_ref[:, k * fh:(k + 1) * fh] = _bf(silu * uk)
            da_ref[:, k * fh:(k + 1) * fh] = dgk
            da_ref[:, (2 + k) * fh:(3 + k) * fh] = duk
            dx += _dot_nt(dgk, w4_ref[k]) + _dot_nt(duk, w4_ref[2 + k])
        dx_ref[...] = dx

    return _rowcall(name, body, T, tm, [dh, z, a], [w_in4, w_out, ln_g],
                    [(D, F32), (4 * fh, BF16), (2 * fh, BF16), (D, BF16)],
                    [((1, D), F32), ((1, D), F32)], exchange=exchange)


WG_TILE_N = 1536


def _mm_tn(name, a, b, out_dtype=BF16, n_split=1):
    T, M = a.shape
    N = b.shape[1]
    tk = _tile(T, 2048, 8)
    tm = _tile(M, 1408)
    tn = _tile(N // n_split, WG_TILE_N)
    per = N // n_split // tn
    nk = T // tk
    if n_split > 1:
        out_spec = pl.BlockSpec((None, tm, tn), lambda i, j, k: (j // per, i, j % per))
        out_shape = jax.ShapeDtypeStruct((n_split, M, N // n_split), out_dtype)
    else:
        out_spec = pl.BlockSpec((tm, tn), lambda i, j, k: (i, j))
        out_shape = jax.ShapeDtypeStruct((M, N), out_dtype)

    def kern(a_ref, b_ref, o_ref, acc_ref):
        k = pl.program_id(2)
        part = _dot_tn(_bf(a_ref[...]), _bf(b_ref[...]))

        @pl.when(k == 0)
        def _():
            acc_ref[...] = part

        @pl.when(k != 0)
        def _():
            acc_ref[...] += part

        @pl.when(k == nk - 1)
        def _():
            o_ref[...] = acc_ref[...].astype(out_dtype)

    return pl.pallas_call(
        kern, grid=(M // tm, N // tn, nk),
        in_specs=[pl.BlockSpec((tk, tm), lambda i, j, k: (k, i)), pl.BlockSpec((tk, tn), lambda i, j, k: (k, j))],
        out_specs=out_spec, out_shape=out_shape,
        scratch_shapes=[pltpu.VMEM((tm, tn), F32)], name=name,
        compiler_params=_params(("arbitrary", "arbitrary", "arbitrary")))(a, b)


def _proj_ret(h1, w_r, cos_r, sin_r, tm):
    T, D = h1.shape
    qk = RET_HEADS * RET_DK
    rv = RET_HEADS * RET_DV

    def body(i, h_ref, cos_ref, sin_ref, w_ref, q_ref, k_ref, v_ref, g_ref):
        hb = _bf(h_ref[...])
        cos, sin = cos_ref[...], sin_ref[...]
        for out_ref, off, scale in ((q_ref, 0, 1.0), (k_ref, qk, RET_DK ** -0.5)):
            pr = _dot(hb, w_ref[:, off:off + qk])
            for h in range(RET_HEADS):
                t = pr[:, h * RET_DK:(h + 1) * RET_DK]
                out_ref[:, h * RET_DK:(h + 1) * RET_DK] = _bf((t * cos + _roll(t, RET_DK // 2) * sin) * scale)
        v_ref[...] = _bf(_dot(hb, w_ref[:, 2 * qk:2 * qk + rv]))
        g_ref[...] = _bf(_dot(hb, w_ref[:, 2 * qk + rv:2 * qk + 2 * rv]))

    return _rowcall("proj_ret", body, T, tm, [h1, cos_r, sin_r], [w_r],
                    [(qk, BF16), (qk, BF16), (rv, BF16), (rv, BF16)])


def _rope_pe(t, c, s1, s2):
    return t * c + _roll(t, LANES - MLA_ROPE // 2) * s1 + _roll(t, MLA_ROPE // 2) * s2


def _rope_pe_bwd(dy, c, s1, s2):
    return dy * c + _roll(dy * s1, MLA_ROPE // 2) + _roll(dy * s2, LANES - MLA_ROPE // 2)


def _rms(x, g):
    r = lax.rsqrt(_mean(x * x) + EPS)
    return x * r, r


def _attn_block(T):
    return min(512, T)


def _transposed_blocks(T, tm, w, dtype):
    tb = _attn_block(T)
    per = tb // tm
    return (jax.ShapeDtypeStruct((T // tb, MLA_HEADS, w, tb), dtype),
            pl.BlockSpec((None, MLA_HEADS, w, tm), lambda i: (i // per, 0, 0, i % per)))


ATTN_SCALE = (MLA_NOPE + MLA_ROPE) ** -0.5
LOG2E = 1.4426950408889634
Q_PRESCALE = ATTN_SCALE * LOG2E
V_ONES = 16


def _proj_mla(h1, tabs, w_c, w_kpe, w_g, w_uq, w_uk, w_uv, qn_g, kvn_g, tm):
    T, D = h1.shape
    H = MLA_HEADS

    def body(i, h_ref, c_ref, s1_ref, s2_ref, wc_ref, wk_ref, wg_ref, wuq_ref, wuk_ref, wuv_ref, qg_ref, kg_ref,
             lat_ref, gt_ref, q_ref, k_ref, v_ref, ln_ref, qt_ref, kt_ref, vt_ref):
        hb = _bf(h_ref[...])
        c, s1, s2 = c_ref[...], s1_ref[...], s2_ref[...]
        lat = _dot(hb, wc_ref[...])
        lat_ref[...] = lat
        gt_ref[...] = _bf(_dot(hb, wg_ref[...]))
        cqn, _ = _rms(lat[:, :Q_LORA], None)
        ckn, _ = _rms(lat[:, Q_LORA:], None)
        cqn = _bf(cqn * qg_ref[...])
        ckn = _bf(ckn * kg_ref[...])
        ln_ref[:, :Q_LORA] = cqn
        ln_ref[:, Q_LORA:] = ckn
        q = _dot(cqn, wuq_ref[...])
        kn = _dot(ckn, wuk_ref[...])
        vv = _dot(ckn, wuv_ref[...])
        v_ref[...] = _bf(vv)
        kpe = _rope_pe(_dot(hb, wk_ref[...]), c, s1, s2)
        ones = jnp.ones((V_ONES, tm), BF16)
        for h in range(H):
            o = h * MLA_QK
            qh = jnp.concatenate([q[:, o:o + MLA_NOPE], _rope_pe(q[:, o + MLA_NOPE:o + MLA_QK], c, s1, s2)], axis=1)
            qh = qh * Q_PRESCALE
            kh = jnp.concatenate([kn[:, h * MLA_NOPE:(h + 1) * MLA_NOPE], kpe], axis=1)
            q_ref[:, o:o + MLA_QK] = _bf(qh)
            k_ref[:, o:o + MLA_QK] = _bf(kh)
            qt_ref[h] = _bf(qh.T)
            kt_ref[h] = _bf(kh.T)
            vt_ref[h] = jnp.concatenate([_bf(vv[:, h * MLA_DV:(h + 1) * MLA_DV].T), ones], axis=0)

    lat_w = Q_LORA + KV_LORA
    return _rowcall("proj_mla", body, T, tm, [h1, *tabs], [w_c, w_kpe, w_g, w_uq, w_uk, w_uv, qn_g, kvn_g],
                    [(lat_w, F32), (2 * D, BF16), (H * MLA_QK, BF16), (H * MLA_QK, BF16), (H * MLA_DV, BF16),
                     (lat_w, BF16)],
                    tiled_outs=[_transposed_blocks(T, tm, MLA_QK, BF16), _transposed_blocks(T, tm, MLA_QK, BF16),
                                _transposed_blocks(T, tm, MLA_DV + V_ONES, BF16)])


def _ret_block(T):
    return min(256, T)


RET_HEADS_PER_STEP = 8


def _ret_dmat(lg, bt):
    n = lax.broadcasted_iota(jnp.int32, (bt, bt), 0)
    m = lax.broadcasted_iota(jnp.int32, (bt, bt), 1)
    return jnp.where(_chunk_of(m) <= _chunk_of(n), jnp.exp(lg * jnp.abs(n - m).astype(F32)), 0.0)


def _ret_scan(name, per_head, lgam, ins, outs, rev):
    T = ins[0][0].shape[0]
    bt = _ret_block(T)
    nb = T // bt
    hps = min(RET_HEADS_PER_STEP, RET_HEADS)
    n_in, n_out = len(ins), len(outs)

    def kern(lg_ref, *refs):
        in_refs, out_refs = refs[:n_in], refs[n_in:n_in + n_out]
        state_ref, dmat_ref = refs[n_in + n_out:]

        @pl.when(pl.program_id(1) == 0)
        def _():
            state_ref[...] = jnp.zeros_like(state_ref)
            for hh in range(hps):
                dmat_ref[hh] = _ret_dmat(lg_ref[hh][:, :1], bt)

        pos = lax.broadcasted_iota(jnp.int32, (bt, 1), 0).astype(F32)
        for hh in range(hps):
            lg = lg_ref[hh][:, :1]
            xi, zeta, gb = jnp.exp(lg * (pos + 1.0)), jnp.exp(lg * (bt - 1.0 - pos)), jnp.exp(lg * bt)
            tiles = [r[:, hh * w:(hh + 1) * w] for r, (_, w) in zip(in_refs, ins)]
            res = per_head(dmat_ref[hh], xi, zeta, gb, state_ref.at[hh], *tiles)
            for o_ref, (w, _), val in zip(out_refs, outs, res):
                o_ref[:, hh * w:(hh + 1) * w] = val.astype(o_ref.dtype)

    def blk(w):
        if rev:
            return pl.BlockSpec((bt, hps * w), lambda g, b: (nb - 1 - b, g))
        return pl.BlockSpec((bt, hps * w), lambda g, b: (b, g))

    return pl.pallas_call(
        kern, grid=(RET_HEADS // hps, nb),
        in_specs=[pl.BlockSpec((hps, 1, LANES), lambda g, b: (g, 0, 0))] + [blk(w) for _, w in ins],
        out_specs=[blk(w) for w, _ in outs],
        out_shape=[jax.ShapeDtypeStruct((T, RET_HEADS * w), dt) for w, dt in outs],
        scratch_shapes=[pltpu.VMEM((hps, RET_DK, RET_DV), F32), pltpu.VMEM((hps, bt, bt), F32)], name=name,
        compiler_params=_params(("arbitrary", "arbitrary")))(lgam, *[a for a, _ in ins])


def _ret_fwd(rq, rk, rv, lgam):
    def per_head(dmat, xi, zeta, gb, s_ref, q, k, v):
        sc = _dot_nt(q, k) * dmat
        y = _dot(_bf(sc), v) + _dot(q, _bf(s_ref[...])) * xi
        s_ref[...] = s_ref[...] * gb + _dot_tn(_bf(k.astype(F32) * zeta), v)
        return (y,)

    return _ret_scan("ret_fwd", per_head, lgam, [(rq, RET_DK), (rk, RET_DK), (rv, RET_DV)], [(RET_DV, BF16)], False)[0]


def _ret_bwd_q(rq, rk, rv, dy, lgam):
    def per_head(dmat, xi, zeta, gb, s_ref, k, v, dy):
        dp = _dot_nt(dy, v) * dmat
        dq = _dot(_bf(dp), k) + _dot_nt(dy, _bf(s_ref[...])) * xi
        s_ref[...] = s_ref[...] * gb + _dot_tn(_bf(k.astype(F32) * zeta), v)
        return (dq,)

    return _ret_scan("ret_bwd_q", per_head, lgam, [(rk, RET_DK), (rv, RET_DV), (dy, RET_DV)], [(RET_DK, F32)], False)[0]


def _ret_bwd_kv(rq, rk, rv, dy, lgam):
    def per_head(dmat, xi, zeta, gb, g_ref, q, k, v, dy):
        gs = _bf(g_ref[...])
        p = _dot_nt(q, k) * dmat
        dp = _dot_nt(dy, v) * dmat
        dv = _dot_tn(_bf(p), dy) + _dot(k, gs) * zeta
        dk = _dot_tn(_bf(dp), q) + _dot_nt(v, gs) * zeta
        g_ref[...] = g_ref[...] * gb + _dot_tn(_bf(q.astype(F32) * xi), dy)
        return dk, dv

    return _ret_scan("ret_bwd_kv", per_head, lgam, [(rq, RET_DK), (rk, RET_DK), (rv, RET_DV), (dy, RET_DV)],
                     [(RET_DK, F32), (RET_DV, BF16)], True)


def _attn_mask_t(tb):
    key = lax.broadcasted_iota(jnp.int32, (tb, tb), 0)
    qry = lax.broadcasted_iota(jnp.int32, (tb, tb), 1)
    return _chunk_of(key) <= _chunk_of(qry)


MASKED = -1e30
SUBLANES = 8


def _head_blocks(nb, w, tb):
    return pl.BlockSpec((nb, None, w, tb), lambda h, i: (0, h, 0, 0))


def _one_block(w, tb):
    return pl.BlockSpec((None, None, w, tb), lambda h, i: (i, h, 0, 0))


def _attn_fwd(k, qt, vt, exchange=()):
    T = k.shape[0]
    tb = _attn_block(T)
    nb = T // tb

    n_ex = len(exchange)

    def kern(qt_ref, k_ref, vt_ref, *refs):
        ex_in, (o_ref, lser_ref), ex_out = refs[:n_ex], refs[n_ex:n_ex + 2], refs[n_ex + 2:2 * n_ex + 2]
        m_ref, acc_ref, sa_ref, sb_ref = refs[2 * n_ex + 2:2 * n_ex + 6]
        sems = refs[2 * n_ex + 6:]
        qb = pl.program_id(1)
        first = jnp.logical_and(pl.program_id(0) == 0, qb == 0)
        last = jnp.logical_and(pl.program_id(0) == MLA_HEADS - 1, qb == nb - 1)
        if n_ex:
            @pl.when(first)
            def _():
                for send, _ in _chip_copies(ex_in, ex_out, *sems, True):
                    send.start()

        qt = qt_ref[...]
        m_ref[...] = jnp.full_like(m_ref, MASKED)
        acc_ref[...] = jnp.zeros_like(acc_ref)

        def scores(kb):
            rows = pl.ds(pl.multiple_of(kb * tb, tb), tb)
            return _dot(k_ref[rows, :], qt)

        def update(s, kb):
            m_old = m_ref[...]
            m_new = jnp.maximum(m_old, jnp.max(s, axis=0, keepdims=True))
            p = jnp.exp2(s - m_new)
            acc_ref[...] = acc_ref[...] * jnp.exp2(m_old - m_new) + _dot(vt_ref[kb], _bf(p))
            m_ref[...] = m_new

        def masked(s):
            return jnp.where(_attn_mask_t(tb), s, MASKED)

        sa_ref[...] = scores(0)

        def pair_body(j, carry):
            sb_ref[...] = scores(2 * j + 1)
            update(sa_ref[...], 2 * j)
            sa_ref[...] = scores(2 * j + 2)
            update(sb_ref[...], 2 * j + 1)
            return carry

        lax.fori_loop(0, qb // 2, pair_body, 0)

        @pl.when(qb % 2 == 0)
        def _():
            update(masked(sa_ref[...]), qb)

        @pl.when(qb % 2 == 1)
        def _():
            sb_ref[...] = masked(scores(qb))
            update(sa_ref[...], qb - 1)
            update(sb_ref[...], qb)

        l = acc_ref[MLA_DV:MLA_DV + 1, :]
        o_ref[...] = _bf((acc_ref[:MLA_DV, :] / l).T)
        lser_ref[...] = jnp.broadcast_to(m_ref[...] + jnp.log2(l), (SUBLANES, tb))
        if n_ex:
            @pl.when(last)
            def _():
                _wait_copies(_chip_copies(ex_in, ex_out, *sems, True))

    return pl.pallas_call(
        kern, grid=(MLA_HEADS, nb),
        in_specs=[_one_block(MLA_QK, tb), pl.BlockSpec((T, MLA_QK), lambda h, i: (0, h)),
                  _head_blocks(nb, MLA_DV + V_ONES, tb)] + [HBM_SPEC] * n_ex,
        out_specs=[pl.BlockSpec((tb, MLA_DV), lambda h, i: (i, h)), _one_block(SUBLANES, tb)] + [HBM_SPEC] * n_ex,
        out_shape=[jax.ShapeDtypeStruct((T, MLA_HEADS * MLA_DV), BF16),
                   jax.ShapeDtypeStruct((nb, MLA_HEADS, SUBLANES, tb), F32)] + _exchange_shapes(exchange),
        scratch_shapes=[pltpu.VMEM((1, tb), F32), pltpu.VMEM((MLA_DV + V_ONES, tb), F32),
                        pltpu.VMEM((tb, tb), F32), pltpu.VMEM((tb, tb), F32)]
        + (_dma_sems(n_ex * N_PEER_CHIPS) if n_ex else []),
        name="attn_fwd", compiler_params=_params(("arbitrary", "arbitrary")))(qt, k, vt, *exchange)


def _attn_bwd(q, k, v, do, qt, kt, dot_, lse_rows, delta_rows):
    T = q.shape[0]
    tb = _attn_block(T)
    nb = T // tb

    def kern(q_ref, k_ref, v_ref, do_ref, qt_ref, kt_ref, dot_ref, lse_ref, dl_ref, dk_ref, dv_ref, dqt_ref, dv_acc,
             sa_ref, pa_ref, sb_ref, pb_ref):
        kb = pl.program_id(1)
        kv, vv, ktv = k_ref[...], v_ref[...], kt_ref[...]
        dk_ref[...] = jnp.zeros_like(dk_ref)
        dv_acc[...] = jnp.zeros_like(dv_acc)

        @pl.when(kb == 0)
        def _():
            dqt_ref[...] = jnp.zeros_like(dqt_ref)

        def products(qb, s_ref, dp_ref, diagonal=False):
            s = _dot(kv, qt_ref[qb])
            s_ref[...] = jnp.where(_attn_mask_t(tb), s, MASKED) if diagonal else s
            dp_ref[...] = _dot(vv, dot_ref[qb])

        def consume(qb, s_ref, dp_ref):
            rows = pl.ds(pl.multiple_of(qb * tb, tb), tb)
            p = jnp.exp2(s_ref[...] - lse_ref[qb][:1, :])
            dv_acc[...] += _dot(_bf(p), do_ref[rows, :])
            ds = _bf(p * (dp_ref[...] - dl_ref[qb][:1, :]))
            dk_ref[...] += _dot(ds, q_ref[rows, :])
            dqt_ref[qb] += _dot(ktv, ds)

        n_full = nb - 1 - kb
        products(kb, sa_ref, pa_ref, diagonal=True)

        def pair_body(j, carry):
            q1 = kb + 1 + 2 * j
            products(q1, sb_ref, pb_ref)
            consume(q1 - 1, sa_ref, pa_ref)
            products(q1 + 1, sa_ref, pa_ref)
            consume(q1, sb_ref, pb_ref)
            return carry

        lax.fori_loop(0, n_full // 2, pair_body, 0)

        @pl.when(n_full % 2 == 0)
        def _():
            consume(nb - 1, sa_ref, pa_ref)

        @pl.when(n_full % 2 == 1)
        def _():
            products(nb - 1, sb_ref, pb_ref)
            consume(nb - 2, sa_ref, pa_ref)
            consume(nb - 1, sb_ref, pb_ref)

        dk_ref[...] = dk_ref[...] * (ATTN_SCALE / Q_PRESCALE)
        dv_ref[...] = _bf(dv_acc[...])

    def blk(w):
        return pl.BlockSpec((tb, w), lambda h, i: (i, h))

    def full(w):
        return pl.BlockSpec((T, w), lambda h, i: (0, h))

    return pl.pallas_call(
        kern, grid=(MLA_HEADS, nb),
        in_specs=[full(MLA_QK), blk(MLA_QK), blk(MLA_DV), full(MLA_DV), _head_blocks(nb, MLA_QK, tb),
                  _one_block(MLA_QK, tb), _head_blocks(nb, MLA_DV, tb), _head_blocks(nb, SUBLANES, tb),
                  _head_blocks(nb, SUBLANES, tb)],
        out_specs=[blk(MLA_QK), blk(MLA_DV), _head_blocks(nb, MLA_QK, tb)],
        out_shape=[jax.ShapeDtypeStruct((T, MLA_HEADS * MLA_QK), F32),
                   jax.ShapeDtypeStruct((T, MLA_HEADS * MLA_DV), BF16),
                   jax.ShapeDtypeStruct((nb, MLA_HEADS, MLA_QK, tb), F32)],
        scratch_shapes=[pltpu.VMEM((tb, MLA_DV), F32)] + [pltpu.VMEM((tb, tb), F32)] * 4,
        name="attn_bwd", compiler_params=_params(("arbitrary", "arbitrary")))(
            q, k, v, do, qt, kt, dot_, lse_rows, delta_rows)


def _group_norm(y):
    yc = y - _mean(y)
    rstd = lax.rsqrt(_mean(yc * yc) + EPS)
    return yc * rstd, rstd


def _mix_fwd(y, rg, o, gates, h1, gn_g, w_ret_o, w_mla_o, w_out, ln_g, ln_b, tm):
    T, D = h1.shape

    def body(i, y_ref, rg_ref, o_ref, gt_ref, h_ref, gn_ref, wr_ref, wm_ref, wo_ref, g_ref, b_ref,
             h2_ref, z_ref, yret_ref, ymla_ref, yr_ref, mix_ref):
        for h in range(RET_HEADS):
            sl = slice(h * RET_DV, (h + 1) * RET_DV)
            yn, _ = _group_norm(y_ref[:, sl].astype(F32))
            r = rg_ref[:, sl].astype(F32)
            yr_ref[:, sl] = _bf(r * _sigmoid(r) * (yn * gn_ref[:, sl]))
        yret = _dot(yr_ref[...], wr_ref[...])
        ymla = _dot(_bf(o_ref[...]), wm_ref[...])
        yret_ref[...] = _bf(yret)
        ymla_ref[...] = _bf(ymla)
        mix = _bf(_sigmoid(gt_ref[:, :D].astype(F32)) * yret + _sigmoid(gt_ref[:, D:].astype(F32)) * ymla)
        mix_ref[...] = mix
        z = ALPHA * h_ref[...] + _dot(mix, wo_ref[...])
        xhat, _ = _ln_stats(z)
        z_ref[...] = z
        h2_ref[...] = xhat * g_ref[...] + b_ref[...]

    return _rowcall("mix_fwd", body, T, tm, [y, rg, o, gates, h1], [gn_g, w_ret_o, w_mla_o, w_out, ln_g, ln_b],
                    [(D, F32), (D, F32), (D, BF16), (D, BF16), (RET_HEADS * RET_DV, BF16), (D, BF16)])


def _mix_bwd(dh2, z1, gates, yret, ymla, y, rg, o, gn_g, w_ret_o, w_mla_o, w_out, ln_g, tm, exchange=None):
    T, D = dh2.shape
    rv = RET_HEADS * RET_DV

    def body(i, dh_ref, z_ref, gt_ref, yret_ref, ymla_ref, y_ref, rg_ref, o_ref, gn_ref, wr_ref, wm_ref, wo_ref, g_ref,
             dz_ref, dgt_ref, drg_ref, dy_ref, do_ref, dyret_ref, dymla_ref, dg_ref, db_ref, dgn_ref, dot_ref,
             dl_ref):
        xhat, rstd = _ln_stats(z_ref[...])
        dz, dg, db = _ln_bwd(dh_ref[...], xhat, rstd, g_ref[...])
        _acc(i, dg_ref, dg)
        _acc(i, db_ref, db)
        dz_ref[...] = dz
        dmix = _dot_nt(_bf(dz), wo_ref[...])
        sr = _sigmoid(gt_ref[:, :D].astype(F32))
        sm = _sigmoid(gt_ref[:, D:].astype(F32))
        dgt_ref[:, :D] = _bf(dmix * yret_ref[...].astype(F32) * sr * (1.0 - sr))
        dgt_ref[:, D:] = _bf(dmix * ymla_ref[...].astype(F32) * sm * (1.0 - sm))
        dyret = _bf(dmix * sr)
        dymla = _bf(dmix * sm)
        dyret_ref[...] = dyret
        dymla_ref[...] = dymla
        dov = _dot_nt(dymla, wm_ref[...])
        do_ref[...] = _bf(dov)
        for h in range(MLA_HEADS):
            sl = slice(h * MLA_DV, (h + 1) * MLA_DV)
            dot_ref[h] = _bf(dov[:, sl].T)
            delta = jnp.sum(dov[:, sl] * o_ref[:, sl].astype(F32), axis=-1, keepdims=True)
            dl_ref[h] = jnp.broadcast_to(delta, (tm, LANES)).T[:SUBLANES, :]
        dyr = _dot_nt(dyret, wr_ref[...])
        dgn = []
        for h in range(RET_HEADS):
            sl = slice(h * RET_DV, (h + 1) * RET_DV)
            yn, grstd = _group_norm(y_ref[:, sl].astype(F32))
            r = rg_ref[:, sl].astype(F32)
            sig = _sigmoid(r)
            d = dyr[:, sl]
            drg_ref[:, sl] = _bf(d * (yn * gn_ref[:, sl]) * sig * (1.0 + r * (1.0 - sig)))
            dt = d * (r * sig)
            dgn.append(jnp.sum(dt * yn, axis=0, keepdims=True))
            dyn = dt * gn_ref[:, sl]
            dy_ref[:, sl] = _bf(grstd * (dyn - _mean(dyn) - yn * _mean(dyn * yn)))
        _acc(i, dgn_ref, jnp.concatenate(dgn, axis=1))

    return _rowcall("mix_bwd", body, T, tm, [dh2, z1, gates, yret, ymla, y, rg, o],
                    [gn_g, w_ret_o, w_mla_o, w_out, ln_g],
                    [(D, F32), (2 * D, BF16), (rv, BF16), (rv, BF16), (MLA_HEADS * MLA_DV, BF16), (D, BF16), (D, BF16)],
                    [((1, D), F32), ((1, D), F32), ((1, rv), F32)],
                    tiled_outs=[_transposed_blocks(T, tm, MLA_DV, BF16), _transposed_blocks(T, tm, SUBLANES, F32)],
                    exchange=exchange)


def _proj_mla_bwd(dqt, dk, dv, lat, tabs, w_uq, w_uk, w_uv, qn_g, kvn_g, tm):
    T = dk.shape[0]
    H = MLA_HEADS
    lat_w = Q_LORA + KV_LORA

    def body(i, dk_ref, dv_ref, lat_ref, c_ref, s1_ref, s2_ref, dqt_ref, wuq_ref, wuk_ref, wuv_ref, qg_ref, kg_ref,
             dlat_ref, dkpe_ref, dqb_ref, dkn_ref, dqg_ref, dkg_ref):
        c, s1, s2 = c_ref[...], s1_ref[...], s2_ref[...]
        dkpe = jnp.zeros((tm, LANES), F32)
        for h in range(H):
            o = h * MLA_QK
            dqh = dqt_ref[h].T * ATTN_SCALE
            dqb_ref[:, o:o + MLA_NOPE] = _bf(dqh[:, :MLA_NOPE])
            dqb_ref[:, o + MLA_NOPE:o + MLA_QK] = _bf(_rope_pe_bwd(dqh[:, MLA_NOPE:], c, s1, s2))
            dkn_ref[:, h * MLA_NOPE:(h + 1) * MLA_NOPE] = _bf(dk_ref[:, o:o + MLA_NOPE])
            dkpe += dk_ref[:, o + MLA_NOPE:o + MLA_QK]
        dkn_ref[:, H * MLA_NOPE:] = dv_ref[...]
        dkpe_ref[...] = _bf(_rope_pe_bwd(dkpe, c, s1, s2))
        dcqn = _dot_nt(dqb_ref[...], wuq_ref[...])
        dckn = _dot_nt(dkn_ref[:, :H * MLA_NOPE], wuk_ref[...]) + _dot_nt(dv_ref[...], wuv_ref[...])
        for dn, x, g_ref, dg_ref, sl in ((dcqn, lat_ref[:, :Q_LORA], qg_ref, dqg_ref, slice(0, Q_LORA)),
                                         (dckn, lat_ref[:, Q_LORA:], kg_ref, dkg_ref, slice(Q_LORA, lat_w))):
            xn, r = _rms(x, None)
            _acc(i, dg_ref, jnp.sum(dn * xn, axis=0, keepdims=True))
            dxn = dn * g_ref[...]
            dlat_ref[:, sl] = _bf(r * (dxn - xn * _mean(dxn * xn)))

    dqt_shape, dqt_spec = _transposed_blocks(T, tm, MLA_QK, F32)
    assert dqt.shape == dqt_shape.shape
    return _rowcall("proj_mla_bwd", body, T, tm, [dk, dv, lat, *tabs], [w_uq, w_uk, w_uv, qn_g, kvn_g],
                    [(lat_w, BF16), (LANES, BF16), (H * MLA_QK, BF16), (H * (MLA_NOPE + MLA_DV), BF16)],
                    [((1, Q_LORA), F32), ((1, KV_LORA), F32)], tiled_ins=[(dqt, dqt_spec)])


def _proj_bwd(drq, drk, drv, drg, dz1, dlat, dkpe, dgates, cos_r, sin_r, w_r, w_c, w_kpe, w_g, tm):
    T, D = dz1.shape
    qk = RET_HEADS * RET_DK
    rv = RET_HEADS * RET_DV
    o_lat = 2 * qk + 2 * rv
    o_kpe = o_lat + dlat.shape[1]
    o_gate = o_kpe + LANES
    o_end = o_gate + dgates.shape[1]
    width = -(-o_end // WG_TILE_N) * WG_TILE_N

    def body(i, drq_ref, drk_ref, drv_ref, drg_ref, dz_ref, dlat_ref, dkpe_ref, dgt_ref, cos_ref, sin_ref,
             wr_ref, wc_ref, wk_ref, wg_ref, dh_ref, dpr_ref):
        cos, sin = cos_ref[...], sin_ref[...]
        for src, off, scale in ((drq_ref, 0, 1.0), (drk_ref, qk, RET_DK ** -0.5)):
            for h in range(RET_HEADS):
                d = src[:, h * RET_DK:(h + 1) * RET_DK]
                dpr_ref[:, off + h * RET_DK:off + (h + 1) * RET_DK] = _bf(
                    (d * cos + _roll(d * sin, RET_DK // 2)) * scale)
        dpr_ref[:, 2 * qk:2 * qk + rv] = drv_ref[...]
        dpr_ref[:, 2 * qk + rv:o_lat] = drg_ref[...]
        dpr_ref[:, o_lat:o_kpe] = dlat_ref[...]
        dpr_ref[:, o_kpe:o_gate] = dkpe_ref[...]
        dpr_ref[:, o_gate:o_end] = dgt_ref[...]
        if width > o_end:
            dpr_ref[:, o_end:] = jnp.zeros((tm, width - o_end), BF16)
        dh_ref[...] = (ALPHA * dz_ref[...] + _dot_nt(dpr_ref[:, :o_lat], wr_ref[...])
                       + _dot_nt(dlat_ref[...], wc_ref[...]) + _dot_nt(dkpe_ref[...], wk_ref[...])
                       + _dot_nt(dgt_ref[...], wg_ref[...]))

    return _rowcall("proj_bwd", body, T, tm, [drq, drk, drv, drg, dz1, dlat, dkpe, dgates, cos_r, sin_r],
                    [w_r, w_c, w_kpe, w_g], [(D, F32), (width, BF16)])


def _ple_loss(h3, p, target, w_gate, w_proj, ln_g, ln_b, tm):
    T, D = h3.shape

    def body(i, h_ref, p_ref, t_ref, wg_ref, wp_ref, g_ref, b_ref, dh_ref, dgp_ref, dpp_ref, loss_ref, dg_ref, db_ref):
        hv = h_ref[...]
        sg = _sigmoid(_dot(_bf(hv), wg_ref[...]))
        pp = _dot(_bf(p_ref[...]), wp_ref[...])
        xhat, rstd = _ln_stats(ALPHA * hv + sg * pp)
        err = xhat * g_ref[...] + b_ref[...] - t_ref[...]
        row_loss = 0.5 * _mean(err * err)
        _acc(i, loss_ref, jnp.broadcast_to(jnp.sum(row_loss, axis=0, keepdims=True), (1, LANES)))
        dz, dg, db = _ln_bwd(err * (1.0 / D), xhat, rstd, g_ref[...])
        _acc(i, dg_ref, dg)
        _acc(i, db_ref, db)
        dgp = _bf(dz * pp * sg * (1.0 - sg))
        dgp_ref[...] = dgp
        dpp_ref[...] = _bf(dz * sg)
        dh_ref[...] = ALPHA * dz + _dot_nt(dgp, wg_ref[...])

    return _rowcall("ple_loss", body, T, tm, [h3, p, target], [w_gate, w_proj, ln_g, ln_b],
                    [(D, F32), (D, BF16), (D, BF16)], [((1, LANES), F32), ((1, D), F32), ((1, D), F32)])


def _ewise(name, fn, ins, n_out, out_dtype=F32):
    r, c = ins[0].shape
    tr = _tile(r, max(8, (1 << 19) // c // 8 * 8), 8)

    def kern(*refs):
        outs = fn(*[x[...] for x in refs[:len(ins)]])
        for o_ref, o in zip(refs[len(ins):], outs):
            o_ref[...] = o.astype(out_dtype)

    spec = pl.BlockSpec((tr, c), lambda i: (i, 0))
    return pl.pallas_call(kern, grid=(r // tr,), in_specs=[spec] * len(ins), out_specs=[spec] * n_out,
                          out_shape=[jax.ShapeDtypeStruct((r, c), out_dtype)] * n_out, name=name,
                          compiler_params=_params(("arbitrary",)))(*ins)


def _adamw_math(w, g, m, v):
    m = ADAM_B1 * m + (1.0 - ADAM_B1) * g
    v = ADAM_B2 * v + (1.0 - ADAM_B2) * (g * g)
    m_hat = m / (1.0 - ADAM_B1 ** ADAM_STEP)
    v_hat = v / (1.0 - ADAM_B2 ** ADAM_STEP)
    return -ADAM_LR * (m_hat / (jnp.sqrt(v_hat) + ADAM_EPS) + ADAM_WD * w), m, v


def _adamw(name, w, g, m, v):
    shape = w.shape
    c = shape[-1]
    flat = [t.reshape(-1, c) for t in (w, g, m, v)]
    return [t.reshape(shape) for t in _ewise(name, _adamw_math, flat, 3)]


def _place():
    return lax.axis_index("x"), lax.axis_index("y"), lax.axis_index("c")


def _dma_sems(n):
    return [pltpu.SemaphoreType.DMA((n,)), pltpu.SemaphoreType.DMA((n,))]


N_PEER_CHIPS = N_CHIPS - 1


def _chips_exchange(name, srcs, broadcast):
    n = len(srcs)

    def kern(*refs):
        cps = _chip_copies(refs[:n], refs[n:2 * n], refs[2 * n], refs[2 * n + 1], broadcast)
        for send, _ in cps:
            send.start()
        _wait_copies(cps)

    return pl.pallas_call(
        kern, out_shape=_exchange_shapes(srcs), in_specs=[HBM_SPEC] * n, out_specs=[HBM_SPEC] * n,
        scratch_shapes=_dma_sems(n * N_PEER_CHIPS), name=name)(*srcs)


def _exchange_shapes(srcs):
    return [jax.ShapeDtypeStruct((N_CHIPS,) + s.shape[1:], s.dtype) for s in srcs]


def _chip_copies(src_refs, out_refs, send_sems, recv_sems, broadcast):
    x, y, c = _place()
    me = 2 * x + y
    peers = [(1 - x, y), (x, 1 - y), (1 - x, 1 - y)]
    cps = []
    for j, (px, py) in enumerate(peers):
        for a, (src_ref, out_ref) in enumerate(zip(src_refs, out_refs)):
            piece = src_ref.at[c] if broadcast else src_ref.at[2 * px + py]

            def copy(slot):
                return pltpu.make_async_remote_copy(
                    src_ref=piece, dst_ref=out_ref.at[slot], send_sem=send_sems.at[a * N_PEER_CHIPS + j],
                    recv_sem=recv_sems.at[a * N_PEER_CHIPS + j], device_id=(px, py, c), device_id_type=MESH)

            cps.append((copy(me), copy(2 * px + py)))
    return cps


def _wait_copies(cps):
    for _, landing in cps:
        landing.wait_recv()
    for send, _ in cps:
        send.wait_send()


def _sibling_swap(name, srcs, mode):
    n = len(srcs)
    per = N_PEER_CHIPS if mode == "others" else 1

    def kern(*refs):
        src_refs, out_refs = refs[:n], refs[n:2 * n]
        send_sems, recv_sems = refs[2 * n:]
        x, y, c = _place()
        slots = [2 * (1 - x) + y, 2 * x + 1 - y, 2 * (1 - x) + 1 - y]
        cps = []
        for a in range(n):
            if mode == "others":
                pieces = [(src_refs[a].at[k], out_refs[a].at[k]) for k in slots]
            else:
                pieces = [(src_refs[a].at[:, 1 - c] if mode == "halves" else src_refs[a], out_refs[a])]
            for j, (src, dst) in enumerate(pieces):
                cps.append(pltpu.make_async_remote_copy(
                    src_ref=src, dst_ref=dst, send_sem=send_sems.at[a * per + j], recv_sem=recv_sems.at[a * per + j],
                    device_id=(x, y, 1 - c), device_id_type=MESH))
        for cp in cps:
            cp.start()
        for cp in cps:
            cp.wait_recv()
        for cp in cps:
            cp.wait_send()

    def out_shape(s):
        return jax.ShapeDtypeStruct((s.shape[0],) + s.shape[2:] if mode == "halves" else s.shape, s.dtype)

    return pl.pallas_call(
        kern, out_shape=[out_shape(s) for s in srcs], in_specs=[HBM_SPEC] * n, out_specs=[HBM_SPEC] * n,
        scratch_shapes=_dma_sems(n * per), name=name)(*srcs)


def _all_devices(name, src, reduce):
    r, c = src.shape
    n_dev = 2 * N_CHIPS

    def kern(src_ref, out_ref, *scratch):
        if reduce:
            gat_ref, send_sems, recv_sems = scratch
        else:
            gat_ref = out_ref
            send_sems, recv_sems = scratch
        x, y, cc = _place()
        me = 4 * x + 2 * y + cc
        gat_ref[me] = src_ref[...]
        peers = []
        for j in range(1, n_dev):
            px = 1 - x if j & 4 else x
            py = 1 - y if j & 2 else y
            pc = 1 - cc if j & 1 else cc
            peers.append((px, py, pc))

        def copy(j, peer, slot):
            return pltpu.make_async_remote_copy(
                src_ref=src_ref, dst_ref=gat_ref.at[slot], send_sem=send_sems.at[j], recv_sem=recv_sems.at[j],
                device_id=peer, device_id_type=MESH)

        sends = [copy(j, peer, me) for j, peer in enumerate(peers)]
        for cp in sends:
            cp.start()
        for j, (px, py, pc) in enumerate(peers):
            copy(j, (px, py, pc), 4 * px + 2 * py + pc).wait_recv()
        for cp in sends:
            cp.wait_send()
        if reduce:
            total = gat_ref[0]
            for d in range(1, n_dev):
                total = total + gat_ref[d]
            out_ref[...] = total

    out_shape = jax.ShapeDtypeStruct((r, c) if reduce else (n_dev, r, c), src.dtype)
    scratch = ([pltpu.VMEM((n_dev, r, c), src.dtype)] if reduce else []) + _dma_sems(n_dev - 1)
    return pl.pallas_call(kern, out_shape=out_shape, in_specs=[VMEM_SPEC], out_specs=VMEM_SPEC,
                          scratch_shapes=scratch, name=name)(src)


def _halves(t, axis):
    return t.reshape(t.shape[:axis] + (2, t.shape[axis] // 2) + t.shape[axis + 1:])


def _by_core(mine, theirs, axis):
    c = lax.axis_index("c")
    both = jnp.where(c == 0, jnp.stack([mine, theirs], axis), jnp.stack([theirs, mine], axis))
    return both.reshape(both.shape[:axis] + (2 * both.shape[axis + 1],) + both.shape[axis + 2:])


def _with_own(own, others):
    me = 2 * lax.axis_index("x") + lax.axis_index("y")
    is_me = (jnp.arange(N_CHIPS, dtype=jnp.int32) == me)[:, None, None]
    return jnp.where(is_me, own[None], others)


def _join_shards(name, shards):
    _, r, c = shards.shape
    if name in COL_SHARDED:
        return shards.transpose(1, 0, 2).reshape(r, N_CHIPS * c)
    return shards.reshape(N_CHIPS * r, c)


def _split_shards(name, full):
    if full.ndim == 3:
        return full
    r, c = full.shape
    if name in COL_SHARDED:
        return jnp.stack([full[:, k * (c // N_CHIPS):(k + 1) * (c // N_CHIPS)] for k in range(N_CHIPS)])
    return full.reshape(N_CHIPS, r // N_CHIPS, c)


def _rope_tables(positions):
    pos = positions.reshape(-1).astype(F32)[:, None]
    half = RET_DK // 2
    ang = pos * (ROPE_BASE ** (-jnp.arange(half, dtype=F32) / half))
    cos_r = jnp.concatenate([jnp.cos(ang)] * 2, axis=1)
    sin_r = jnp.concatenate([-jnp.sin(ang), jnp.sin(ang)], axis=1)
    half = MLA_ROPE // 2
    ang = pos * (ROPE_BASE ** (-jnp.arange(half, dtype=F32) / half))
    zeros = jnp.zeros_like(ang)
    rest = LANES - MLA_ROPE
    c = jnp.concatenate([jnp.cos(ang)] * 2 + [jnp.ones((ang.shape[0], rest), F32)], axis=1)
    s1 = jnp.concatenate([-jnp.sin(ang), zeros, jnp.zeros((ang.shape[0], rest), F32)], axis=1)
    s2 = jnp.concatenate([zeros, jnp.sin(ang), jnp.zeros((ang.shape[0], rest), F32)], axis=1)
    return cos_r, sin_r, (c, s1, s2)


GATHER_GROUPS = (("ffn1_w_in", "ffn1_w_out"), ("w_in", "w_uq", "w_ukv"),
                 ("w_ret_o", "w_mla_o", "w_out", "ffn2_w_in", "ffn2_w_out", "ple_w_gate", "ple_w_proj"))
REDUCE_GROUPS = (("ple_w_gate", "ple_w_proj", "ffn2_w_in", "ffn2_w_out"),
                 ("w_out", "w_ret_o", "w_mla_o", "w_uq", "w_ukv", "w_in"), ("ffn1_w_in", "ffn1_w_out"))


def _gathered(tag, names, own, mine):
    theirs = _sibling_swap("gather_cores_" + tag, mine, "others")
    out = {}
    for n, m, t in zip(names, mine, theirs):
        full = _with_own(own[n], _by_core(m, t, 1))
        out[n] = full if n in ("ffn1_w_in", "ffn2_w_in") else _join_shards(n, full)
    return out


def _chip_sums(tag, names, grads):
    halves = [_halves(_split_shards(n, grads[n]), 1) for n in names]
    theirs = _sibling_swap("reduce_cores_" + tag, halves, "halves")

    def one(n, g, t):
        k, _, r, c = g.shape
        tr = _tile(r, max(8, (1 << 17) // c // 8 * 8), 8)

        def kern(g_ref, t_ref, o_ref):
            mine = jnp.where(lax.axis_index("c") == 0, g_ref[:, 0], g_ref[:, 1])
            o_ref[...] = _bf(mine.astype(F32) + t_ref[...].astype(F32))

        spec = pl.BlockSpec((k, tr, c), lambda i: (0, i, 0))
        return pl.pallas_call(kern, grid=(r // tr,),
                              in_specs=[pl.BlockSpec((k, 2, tr, c), lambda i: (0, 0, i, 0)), spec], out_specs=spec,
                              out_shape=jax.ShapeDtypeStruct((k, r, c), BF16), name="reduce_cores_add_" + n,
                              compiler_params=_params(("arbitrary",)))(g, t)

    return [one(n, g, t) for n, g, t in zip(names, halves, theirs)]


def _block_totals(names, sums, parts):
    def one(n, s, pt):
        _, r, c = s.shape
        tr = _tile(r, max(8, (1 << 17) // c // 8 * 8), 8)

        def kern(s_ref, p_ref, o_ref):
            me = 2 * lax.axis_index("x") + lax.axis_index("y")
            terms = [jnp.where(k == me, s_ref[k], p_ref[k]).astype(F32) for k in range(N_CHIPS)]
            o_ref[...] = ((terms[0] + terms[1]) + terms[2]) + terms[3]

        spec = pl.BlockSpec((N_CHIPS, tr, c), lambda i: (0, i, 0))
        return pl.pallas_call(kern, grid=(r // tr,), in_specs=[spec, spec],
                              out_specs=pl.BlockSpec((tr, c), lambda i: (i, 0)),
                              out_shape=jax.ShapeDtypeStruct((r, c), F32), name="reduce_chips_add_" + n,
                              compiler_params=_params(("arbitrary",)))(s, pt)

    return [one(n, s, pt) for n, s, pt in zip(names, sums, parts)]


def _local_step(x, p, positions, target, shards, ln_g, ln_b, gn_g, qn_g, kvn_g):
    T, D = x.shape
    tm = min(256, T)
    H = MLA_HEADS
    qk, rv = RET_HEADS * RET_DK, RET_HEADS * RET_DV
    cos_r, sin_r, tabs = _rope_tables(positions)
    lgam = jnp.broadcast_to(jnp.log(1.0 - 2.0 ** (-5.0 - jnp.arange(RET_HEADS, dtype=F32)))[:, None, None],
                            (RET_HEADS, 1, LANES))
    lng = [ln_g[k:k + 1] for k in range(N_LN)]
    lnb = [ln_b[k:k + 1] for k in range(N_LN)]
    own = {n: _bf(shards[n]) for n in BIG_WEIGHTS}
    to_send = [[_halves(own[n], 0) for n in names] for names in GATHER_GROUPS]

    w = _gathered("a", GATHER_GROUPS[0], own, _chips_exchange("gather_chips_a", to_send[0], True))
    h1, z0, a1, *arrived = _ffn_fwd("ffn1_fwd", x, w["ffn1_w_in"], w["ffn1_w_out"], lng[0], lnb[0], 2 * tm,
                                    exchange=(to_send[1], True))
    w.update(_gathered("b", GATHER_GROUPS[1], own, arrived))

    w_in = w["w_in"]
    o_lat, o_kpe, o_gate = 2 * qk + 2 * rv, 2 * qk + 2 * rv + Q_LORA + KV_LORA, 2 * qk + 2 * rv + Q_LORA + KV_LORA + MLA_ROPE
    w_r, w_c = w_in[:, :o_lat], w_in[:, o_lat:o_kpe]
    w_kpe = jnp.pad(w_in[:, o_kpe:o_gate], ((0, 0), (0, LANES - MLA_ROPE)))
    w_g = w_in[:, o_gate:]
    w_uq = jnp.pad(w["w_uq"].reshape(Q_LORA, H, MLA_NOPE + MLA_ROPE),
                   ((0, 0), (0, 0), (0, MLA_QK - MLA_NOPE - MLA_ROPE))).reshape(Q_LORA, H * MLA_QK)
    w_ukv = w["w_ukv"].reshape(KV_LORA, H, MLA_NOPE + MLA_DV)
    w_uk = w_ukv[:, :, :MLA_NOPE].reshape(KV_LORA, H * MLA_NOPE)
    w_uv = w_ukv[:, :, MLA_NOPE:].reshape(KV_LORA, H * MLA_DV)

    rq, rk, rvv, rg = _proj_ret(h1, w_r, cos_r, sin_r, 2 * tm)
    lat, gates, q, k, v, latn, qt, kt, vt = _proj_mla(h1, tabs, w_c, w_kpe, w_g, w_uq, w_uk, w_uv, qn_g, kvn_g, 2 * tm)
    y = _ret_fwd(rq, rk, rvv, lgam)
    o, lse_rows, *arrived = _attn_fwd(k, qt, vt, exchange=to_send[2])
    w.update(_gathered("c", GATHER_GROUPS[2], own, arrived))
    h2, z1, yret, ymla, yr, mix = _mix_fwd(y, rg, o, gates, h1, gn_g, w["w_ret_o"], w["w_mla_o"], w["w_out"],
                                           lng[1], lnb[1], 2 * tm)
    h3, z2, a2 = _ffn_fwd("ffn2_fwd", h2, w["ffn2_w_in"], w["ffn2_w_out"], lng[2], lnb[2], 2 * tm)

    dh3, dgp, dpp, loss, dg3, db3 = _ple_loss(h3, p, target, w["ple_w_gate"], w["ple_w_proj"], lng[3], lnb[3], 2 * tm)
    dh2, da2, s2, df2, dg2, db2 = _ffn_bwd("ffn2_bwd", dh3, z2, a2, w["ffn2_w_in"], w["ffn2_w_out"], lng[2], tm)
    grads = {"ple_w_gate": _mm_tn("wg_ple_gate", h3, dgp), "ple_w_proj": _mm_tn("wg_ple_proj", p, dpp),
             "ffn2_w_in": _mm_tn("wg_ffn2_in", h2, da2, n_split=N_CHIPS), "ffn2_w_out": _mm_tn("wg_ffn2_out", s2, df2)}
    sums1 = _chip_sums("1", REDUCE_GROUPS[0], grads)
    (dz1, dgates, drg, dy, do, dyret, dymla, dg1, db1, dgn, dot_, delta_rows, *parts1) = _mix_bwd(
        dh2, z1, gates, yret, ymla, y, rg, o, gn_g, w["w_ret_o"], w["w_mla_o"], w["w_out"], lng[1], tm,
        exchange=(sums1, False))
    drq = _ret_bwd_q(rq, rk, rvv, dy, lgam)
    drk, drv = _ret_bwd_kv(rq, rk, rvv, dy, lgam)
    dk, dv, dqt = _attn_bwd(q, k, v, do, qt, kt, dot_, lse_rows, delta_rows)
    dlat, dkpe, dqb, dkv, dqg, dkg = _proj_mla_bwd(dqt, dk, dv, lat, tabs, w_uq, w_uk, w_uv, qn_g, kvn_g, 2 * tm)
    dh1, dpr = _proj_bwd(drq, drk, drv, drg, dz1, dlat, dkpe, dgates, cos_r, sin_r, w_r, w_c, w_kpe, w_g, tm)
    g_uq = _mm_tn("wg_uq", latn[:, :Q_LORA], dqb).reshape(Q_LORA, H, MLA_QK)[:, :, :MLA_NOPE + MLA_ROPE]
    g_ukv = _mm_tn("wg_ukv", latn[:, Q_LORA:], dkv)
    g_uk = g_ukv[:, :H * MLA_NOPE].reshape(KV_LORA, H, MLA_NOPE)
    g_uv = g_ukv[:, H * MLA_NOPE:].reshape(KV_LORA, H, MLA_DV)
    g_in = _mm_tn("wg_in", h1, dpr)
    grads.update({
        "w_in": jnp.concatenate([g_in[:, :o_kpe + MLA_ROPE], g_in[:, o_kpe + LANES:o_kpe + LANES + 2 * D]], axis=1),
        "w_ret_o": _mm_tn("wg_ret_o", yr, dyret),
        "w_uq": g_uq.reshape(Q_LORA, H * (MLA_NOPE + MLA_ROPE)),
        "w_ukv": jnp.concatenate([g_uk, g_uv], axis=2).reshape(KV_LORA, H * (MLA_NOPE + MLA_DV)),
        "w_mla_o": _mm_tn("wg_mla_o", o, dymla),
        "w_out": _mm_tn("wg_out", mix, dz1)})
    sums2 = _chip_sums("2", REDUCE_GROUPS[1], grads)
    dx, da1, s1, df1, dg0, db0, *parts2 = _ffn_bwd("ffn1_bwd", dh1, z0, a1, w["ffn1_w_in"], w["ffn1_w_out"], lng[0], tm,
                                                   exchange=(sums2, False))
    grads.update({"ffn1_w_in": _mm_tn("wg_ffn1_in", x, da1, n_split=N_CHIPS),
                  "ffn1_w_out": _mm_tn("wg_ffn1_out", s1, df1)})
    sums3 = _chip_sums("3", REDUCE_GROUPS[2], grads)
    parts3 = _chips_exchange("reduce_chips_3", sums3, False)

    names = [n for group in REDUCE_GROUPS for n in group]
    totals = _block_totals(names, sums1 + sums2 + sums3, list(parts1) + list(parts2) + list(parts3))
    others = _sibling_swap("reduce_join", totals, "whole")
    reduced = {n: _by_core(t, o_, 0) for n, t, o_ in zip(names, totals, others)}
    small = {"ln_g": jnp.concatenate([dg0, dg1, dg2, dg3], axis=0), "ln_b": jnp.concatenate([db0, db1, db2, db3], axis=0),
             "ret_gn_g": dgn, "q_norm_g": dqg, "kv_norm_g": dkg}
    return loss[0, 0], dx, reduced, small


def kernel(x, p, positions, ln_g, ln_b, ffn1_w_in, ffn1_w_out, w_in, ret_gn_g, w_ret_o, q_norm_g, kv_norm_g, w_uq, w_ukv, w_mla_o, w_out, ffn2_w_in, ffn2_w_out, ple_w_gate, ple_w_proj, loss_target, m_ln_g, m_ln_b, m_ffn1_w_in, m_ffn1_w_out, m_w_in, m_ret_gn_g, m_w_ret_o, m_q_norm_g, m_kv_norm_g, m_w_uq, m_w_ukv, m_w_mla_o, m_w_out, m_ffn2_w_in, m_ffn2_w_out, m_ple_w_gate, m_ple_w_proj, v_ln_g, v_ln_b, v_ffn1_w_in, v_ffn1_w_out, v_w_in, v_ret_gn_g, v_w_ret_o, v_q_norm_g, v_kv_norm_g, v_w_uq, v_w_ukv, v_w_mla_o, v_w_out, v_ffn2_w_in, v_ffn2_w_out, v_ple_w_gate, v_ple_w_proj):
    names = ("ln_g", "ln_b", "ffn1_w_in", "ffn1_w_out", "w_in", "ret_gn_g", "w_ret_o", "q_norm_g", "kv_norm_g", "w_uq",
             "w_ukv", "w_mla_o", "w_out", "ffn2_w_in", "ffn2_w_out", "ple_w_gate", "ple_w_proj")
    weights = dict(zip(names, (ln_g, ln_b, ffn1_w_in, ffn1_w_out, w_in, ret_gn_g, w_ret_o, q_norm_g, kv_norm_g, w_uq,
                               w_ukv, w_mla_o, w_out, ffn2_w_in, ffn2_w_out, ple_w_gate, ple_w_proj)))
    m_in = dict(zip(names, (m_ln_g, m_ln_b, m_ffn1_w_in, m_ffn1_w_out, m_w_in, m_ret_gn_g, m_w_ret_o, m_q_norm_g,
                            m_kv_norm_g, m_w_uq, m_w_ukv, m_w_mla_o, m_w_out, m_ffn2_w_in, m_ffn2_w_out, m_ple_w_gate,
                            m_ple_w_proj)))
    v_in = dict(zip(names, (v_ln_g, v_ln_b, v_ffn1_w_in, v_ffn1_w_out, v_w_in, v_ret_gn_g, v_w_ret_o, v_q_norm_g,
                            v_kv_norm_g, v_w_uq, v_w_ukv, v_w_mla_o, v_w_out, v_ffn2_w_in, v_ffn2_w_out, v_ple_w_gate,
                            v_ple_w_proj)))
    chip = 2 * lax.axis_index("x") + lax.axis_index("y")
    D = x.shape[-1]
    dq = D // N_CHIPS

    shards = {n: weights[n][0] for n in BIG_WEIGHTS}
    ln_all = _all_devices("gather_ln", jnp.concatenate([ln_g[0], ln_b[0]], axis=0), False)
    ln_full = ln_all[::2].transpose(1, 0, 2).reshape(2 * N_LN, D)
    loss, dx, big, small = _local_step(x[0], p[0, 0], positions, loss_target[0], shards, ln_full[:N_LN],
                                       ln_full[N_LN:], ret_gn_g, q_norm_g, kv_norm_g)

    loss = lax.psum(loss, ("x", "y", "c"))
    small_names = ("ln_g", "ln_b", "ret_gn_g", "q_norm_g", "kv_norm_g")
    flat = jnp.concatenate([small[n].reshape(-1) for n in small_names])
    rows = -(-flat.shape[0] // LANES // 8) * 8
    flat = jnp.pad(flat, (0, rows * LANES - flat.shape[0])).reshape(rows, LANES)
    flat = _all_devices("reduce_small", flat, True).reshape(-1)
    off = 0
    for n in small_names:
        size = small[n].size
        small[n] = flat[off:off + size].reshape(small[n].shape)
        off += size
    g_out = dict(big)
    for n in ("ln_g", "ln_b"):
        g_out[n] = lax.dynamic_slice_in_dim(small[n], chip * dq, dq, axis=1)
    for n in ("ret_gn_g", "q_norm_g", "kv_norm_g"):
        g_out[n] = small[n]

    deltas, new_m, new_v = {}, {}, {}
    for n in names:
        g = g_out[n].reshape(weights[n].shape)
        g_out[n] = g
        deltas[n], new_m[n], new_v[n] = _adamw("adamw_" + n, weights[n], g, m_in[n], v_in[n])
    return (loss, dx[None], *[g_out[n] for n in names], *[deltas[n] for n in names], *[new_m[n] for n in names],
            *[new_v[n] for n in names])
```

```python
import functools

import jax
import jax.numpy as jnp
from jax import lax
from jax.experimental import pallas as pl
from jax.experimental.pallas import tpu as pltpu

CHUNK = 64
RET_HEADS = 8
RET_DK = 128
RET_DV = 256
MLA_HEADS = 8
MLA_NOPE = 128
MLA_ROPE = 64
MLA_DV = 128
MLA_QK = 256
Q_LORA = 256
KV_LORA = 256
ROPE_BASE = 10000.0
EPS = 1e-5
N_LN = 4
ALPHA = 2.0 ** 0.25
ADAM_LR = 0.001
ADAM_B1 = 0.9
ADAM_B2 = 0.999
ADAM_EPS = 1e-08
ADAM_WD = 0.01
ADAM_STEP = 10

LANES = 128
VMEM_LIMIT = 60 << 20
N_CHIPS = 4

F32 = jnp.float32
BF16 = jnp.bfloat16
MESH = pl.DeviceIdType.MESH
HBM_SPEC = pl.BlockSpec(memory_space=pltpu.HBM)
VMEM_SPEC = pl.BlockSpec(memory_space=pltpu.VMEM)

BIG_WEIGHTS = ("ffn1_w_in", "ffn1_w_out", "w_in", "w_ret_o", "w_uq", "w_ukv", "w_mla_o", "w_out",
               "ffn2_w_in", "ffn2_w_out", "ple_w_gate", "ple_w_proj")
COL_SHARDED = ("ffn1_w_in", "w_in", "w_uq", "w_ukv", "ffn2_w_in", "ple_w_proj")


def _dot(a, b):
    return jnp.dot(a, b, preferred_element_type=F32)


def _dot_nt(a, b):
    return lax.dot_general(a, b, (((1,), (1,)), ((), ())), preferred_element_type=F32)


def _dot_tn(a, b):
    return lax.dot_general(a, b, (((0,), (0,)), ((), ())), preferred_element_type=F32)


def _bf(x):
    return x.astype(BF16)


def _sigmoid(x):
    return 0.5 * jnp.tanh(0.5 * x) + 0.5


def _mean(x):
    return jnp.mean(x, axis=-1, keepdims=True)


def _ln_stats(z):
    zc = z - _mean(z)
    rstd = lax.rsqrt(_mean(zc * zc) + EPS)
    return zc * rstd, rstd


def _ln_bwd(dy, xhat, rstd, g):
    dxhat = dy * g
    dz = rstd * (dxhat - _mean(dxhat) - xhat * _mean(dxhat * xhat))
    return dz, jnp.sum(dy * xhat, axis=0, keepdims=True), jnp.sum(dy, axis=0, keepdims=True)


def _roll(x, shift):
    return pltpu.roll(x, shift, 1)


def _chunk_of(idx):
    return jnp.right_shift(idx, CHUNK.bit_length() - 1)


def _tile(n, cap, mult=LANES):
    if n <= cap:
        return n
    for t in range(cap - cap % mult, 0, -mult):
        if n % t == 0:
            return t
    return n


def _zero_map(nd, *_):
    return (0,) * nd


def _params(sem):
    return pltpu.CompilerParams(dimension_semantics=sem, vmem_limit_bytes=VMEM_LIMIT)


def _rowcall(name, body, n_rows, tm, row_ins, full_ins, row_outs, acc_outs=(), tiled_outs=(), tiled_ins=(),
             exchange=None):
    n_steps = n_rows // tm
    ex_srcs, broadcast = exchange if exchange else ((), False)
    n_ex = len(ex_srcs)
    n_in = len(row_ins) + len(tiled_ins) + len(full_ins)
    n_out = len(row_outs) + len(acc_outs) + len(tiled_outs)

    def kern(*refs):
        step = pl.program_id(0)
        ex_in, ex_out = refs[n_in:n_in + n_ex], refs[n_in + n_ex + n_out:n_in + 2 * n_ex + n_out]
        sems = refs[n_in + 2 * n_ex + n_out:]
        if n_ex:
            @pl.when(step == 0)
            def _():
                for send, _ in _chip_copies(ex_in, ex_out, *sems, broadcast):
                    send.start()

        body(step, *refs[:n_in], *refs[n_in + n_ex:n_in + n_ex + n_out])
        if n_ex:
            @pl.when(step == n_steps - 1)
            def _():
                _wait_copies(_chip_copies(ex_in, ex_out, *sems, broadcast))

    in_specs = [pl.BlockSpec((tm, a.shape[1]), lambda i: (i, 0)) for a in row_ins]
    in_specs += [spec for (_, spec) in tiled_ins]
    row_ins = list(row_ins) + [a for (a, _) in tiled_ins]
    in_specs += [pl.BlockSpec(a.shape, functools.partial(_zero_map, a.ndim), pipeline_mode=pl.Buffered(1))
                 for a in full_ins]
    in_specs += [HBM_SPEC] * n_ex
    out_specs = [pl.BlockSpec((tm, w), lambda i: (i, 0)) for (w, _) in row_outs]
    out_specs += [pl.BlockSpec(s, functools.partial(_zero_map, len(s))) for (s, _) in acc_outs]
    out_specs += [spec for (_, spec) in tiled_outs]
    out_specs += [HBM_SPEC] * n_ex
    out_shape = [jax.ShapeDtypeStruct((n_rows, w), dt) for (w, dt) in row_outs]
    out_shape += [jax.ShapeDtypeStruct(s, dt) for (s, dt) in acc_outs]
    out_shape += [shape for (shape, _) in tiled_outs]
    out_shape += _exchange_shapes(ex_srcs)
    return pl.pallas_call(kern, grid=(n_steps,), in_specs=in_specs, out_specs=out_specs, out_shape=out_shape,
                          scratch_shapes=_dma_sems(n_ex * N_PEER_CHIPS) if n_ex else [], name=name,
                          compiler_params=_params(("arbitrary",)))(*row_ins, *full_ins, *ex_srcs)


def _acc(step, ref, val):
    @pl.when(step == 0)
    def _():
        ref[...] = val

    @pl.when(step != 0)
    def _():
        ref[...] += val


def _ffn_fwd(name, x, w_in4, w_out, ln_g, ln_b, tm, exchange=None):
    T, D = x.shape
    fh = w_in4.shape[2]

    def body(i, x_ref, w4_ref, wo_ref, g_ref, b_ref, h_ref, z_ref, a_ref):
        xv = x_ref[...]
        xb = _bf(xv)
        f = jnp.zeros((tm, D), F32)
        for k in range(2):
            gk = _dot(xb, w4_ref[k])
            uk = _dot(xb, w4_ref[2 + k])
            a_ref[:, k * fh:(k + 1) * fh] = _bf(gk)
            a_ref[:, (2 + k) * fh:(3 + k) * fh] = _bf(uk)
            f += _dot(_bf(gk * _sigmoid(gk) * uk), wo_ref[k * fh:(k + 1) * fh, :])
        z = ALPHA * xv + 0.5 * f
        xhat, _ = _ln_stats(z)
        z_ref[...] = z
        h_ref[...] = xhat * g_ref[...] + b_ref[...]

    return _rowcall(name, body, T, tm, [x], [w_in4, w_out, ln_g, ln_b],
                    [(D, F32), (D, F32), (4 * fh, BF16)], exchange=exchange)


def _ffn_bwd(name, dh, z, a, w_in4, w_out, ln_g, tm, exchange=None):
    T, D = dh.shape
    fh = w_in4.shape[2]

    def body(i, dh_ref, z_ref, a_ref, w4_ref, wo_ref, g_ref, dx_ref, da_ref, s_ref, df_ref, dg_ref, db_ref):
        xhat, rstd = _ln_stats(z_ref[...])
        dz, dg, db = _ln_bwd(dh_ref[...], xhat, rstd, g_ref[...])
        _acc(i, dg_ref, dg)
        _acc(i, db_ref, db)
        dfb = _bf(0.5 * dz)
        df_ref[...] = dfb
        dx = ALPHA * dz
        for k in range(2):
            gk = a_ref[:, k * fh:(k + 1) * fh].astype(F32)
            uk = a_ref[:, (2 + k) * fh:(3 + k) * fh].astype(F32)
            ds = _dot_nt(dfb, wo_ref[k * fh:(k + 1) * fh, :])
            sig = _sigmoid(gk)
            silu = gk * sig
            dgk = _bf(ds * uk * sig * (1.0 + gk * (1.0 - sig)))
            duk = _bf(ds * silu)
            s_ref[:, k * fh:(k + 1) * fh] = _bf(silu * uk)
            da_ref[:, k * fh:(k + 1) * fh] = dgk
            da_ref[:, (2 + k) * fh:(3 + k) * fh] = duk
            dx += _dot_nt(dgk, w4_ref[k]) + _dot_nt(duk, w4_ref[2 + k])
        dx_ref[...] = dx

    return _rowcall(name, body, T, tm, [dh, z, a], [w_in4, w_out, ln_g],
                    [(D, F32), (4 * fh, BF16), (2 * fh, BF16), (D, BF16)],
                    [((1, D), F32), ((1, D), F32)], exchange=exchange)


WG_TILE_N = 1536


def _mm_tn(name, a, b, out_dtype=BF16, n_split=1):
    T, M = a.shape
    N = b.shape[1]
    tk = _tile(T, 2048, 8)
    tm = _tile(M, 1408)
    tn = _tile(N // n_split, WG_TILE_N)
    per = N // n_split // tn
    nk = T // tk
    if n_split > 1:
        out_spec = pl.BlockSpec((None, tm, tn), lambda i, j, k: (j // per, i, j % per))
        out_shape = jax.ShapeDtypeStruct((n_split, M, N // n_split), out_dtype)
    else:
        out_spec = pl.BlockSpec((tm, tn), lambda i, j, k: (i, j))
        out_shape = jax.ShapeDtypeStruct((M, N), out_dtype)

    def kern(a_ref, b_ref, o_ref, acc_ref):
        k = pl.program_id(2)
        part = _dot_tn(_bf(a_ref[...]), _bf(b_ref[...]))

        @pl.when(k == 0)
        def _():
            acc_ref[...] = part

        @pl.when(k != 0)
        def _():
            acc_ref[...] += part

        @pl.when(k == nk - 1)
        def _():
            o_ref[...] = acc_ref[...].astype(out_dtype)

    return pl.pallas_call(
        kern, grid=(M // tm, N // tn, nk),
        in_specs=[pl.BlockSpec((tk, tm), lambda i, j, k: (k, i)), pl.BlockSpec((tk, tn), lambda i, j, k: (k, j))],
        out_specs=out_spec, out_shape=out_shape,
        scratch_shapes=[pltpu.VMEM((tm, tn), F32)], name=name,
        compiler_params=_params(("arbitrary", "arbitrary", "arbitrary")))(a, b)


def _proj_ret(h1, w_r, cos_r, sin_r, tm):
    T, D = h1.shape
    qk = RET_HEADS * RET_DK
    rv = RET_HEADS * RET_DV

    def body(i, h_ref, cos_ref, sin_ref, w_ref, q_ref, k_ref, v_ref, g_ref):
        hb = _bf(h_ref[...])
        cos, sin = cos_ref[...], sin_ref[...]
        for out_ref, off, scale in ((q_ref, 0, 1.0), (k_ref, qk, RET_DK ** -0.5)):
            pr = _dot(hb, w_ref[:, off:off + qk])
            for h in range(RET_HEADS):
                t = pr[:, h * RET_DK:(h + 1) * RET_DK]
                out_ref[:, h * RET_DK:(h + 1) * RET_DK] = _bf((t * cos + _roll(t, RET_DK // 2) * sin) * scale)
        v_ref[...] = _bf(_dot(hb, w_ref[:, 2 * qk:2 * qk + rv]))
        g_ref[...] = _bf(_dot(hb, w_ref[:, 2 * qk + rv:2 * qk + 2 * rv]))

    return _rowcall("proj_ret", body, T, tm, [h1, cos_r, sin_r], [w_r],
                    [(qk, BF16), (qk, BF16), (rv, BF16), (rv, BF16)])


def _rope_pe(t, c, s1, s2):
    return t * c + _roll(t, LANES - MLA_ROPE // 2) * s1 + _roll(t, MLA_ROPE // 2) * s2


def _rope_pe_bwd(dy, c, s1, s2):
    return dy * c + _roll(dy * s1, MLA_ROPE // 2) + _roll(dy * s2, LANES - MLA_ROPE // 2)


def _rms(x, g):
    r = lax.rsqrt(_mean(x * x) + EPS)
    return x * r, r


def _attn_block(T):
    return min(512, T)


def _transposed_blocks(T, tm, w, dtype):
    tb = _attn_block(T)
    per = tb // tm
    return (jax.ShapeDtypeStruct((T // tb, MLA_HEADS, w, tb), dtype),
            pl.BlockSpec((None, MLA_HEADS, w, tm), lambda i: (i // per, 0, 0, i % per)))


ATTN_SCALE = (MLA_NOPE + MLA_ROPE) ** -0.5
LOG2E = 1.4426950408889634
Q_PRESCALE = ATTN_SCALE * LOG2E
V_ONES = 16


def _proj_mla(h1, tabs, w_c, w_kpe, w_g, w_uq, w_uk, w_uv, qn_g, kvn_g, tm):
    T, D = h1.shape
    H = MLA_HEADS

    def body(i, h_ref, c_ref, s1_ref, s2_ref, wc_ref, wk_ref, wg_ref, wuq_ref, wuk_ref, wuv_ref, qg_ref, kg_ref,
             lat_ref, gt_ref, q_ref, k_ref, v_ref, ln_ref, qt_ref, kt_ref, vt_ref):
        hb = _bf(h_ref[...])
        c, s1, s2 = c_ref[...], s1_ref[...], s2_ref[...]
        lat = _dot(hb, wc_ref[...])
        lat_ref[...] = lat
        gt_ref[...] = _bf(_dot(hb, wg_ref[...]))
        cqn, _ = _rms(lat[:, :Q_LORA], None)
        ckn, _ = _rms(lat[:, Q_LORA:], None)
        cqn = _bf(cqn * qg_ref[...])
        ckn = _bf(ckn * kg_ref[...])
        ln_ref[:, :Q_LORA] = cqn
        ln_ref[:, Q_LORA:] = ckn
        q = _dot(cqn, wuq_ref[...])
        kn = _dot(ckn, wuk_ref[...])
        vv = _dot(ckn, wuv_ref[...])
        v_ref[...] = _bf(vv)
        kpe = _rope_pe(_dot(hb, wk_ref[...]), c, s1, s2)
        ones = jnp.ones((V_ONES, tm), BF16)
        for h in range(H):
            o = h * MLA_QK
            qh = jnp.concatenate([q[:, o:o + MLA_NOPE], _rope_pe(q[:, o + MLA_NOPE:o + MLA_QK], c, s1, s2)], axis=1)
            qh = qh * Q_PRESCALE
            kh = jnp.concatenate([kn[:, h * MLA_NOPE:(h + 1) * MLA_NOPE], kpe], axis=1)
            q_ref[:, o:o + MLA_QK] = _bf(qh)
            k_ref[:, o:o + MLA_QK] = _bf(kh)
            qt_ref[h] = _bf(qh.T)
            kt_ref[h] = _bf(kh.T)
            vt_ref[h] = jnp.concatenate([_bf(vv[:, h * MLA_DV:(h + 1) * MLA_DV].T), ones], axis=0)

    lat_w = Q_LORA + KV_LORA
    return _rowcall("proj_mla", body, T, tm, [h1, *tabs], [w_c, w_kpe, w_g, w_uq, w_uk, w_uv, qn_g, kvn_g],
                    [(lat_w, F32), (2 * D, BF16), (H * MLA_QK, BF16), (H * MLA_QK, BF16), (H * MLA_DV, BF16),
                     (lat_w, BF16)],
                    tiled_outs=[_transposed_blocks(T, tm, MLA_QK, BF16), _transposed_blocks(T, tm, MLA_QK, BF16),
                                _transposed_blocks(T, tm, MLA_DV + V_ONES, BF16)])


def _ret_block(T):
    return min(256, T)


RET_HEADS_PER_STEP = 8


def _ret_dmat(lg, bt):
    n = lax.broadcasted_iota(jnp.int32, (bt, bt), 0)
    m = lax.broadcasted_iota(jnp.int32, (bt, bt), 1)
    return jnp.where(_chunk_of(m) <= _chunk_of(n), jnp.exp(lg * jnp.abs(n - m).astype(F32)), 0.0)


def _ret_scan(name, per_head, lgam, ins, outs, rev):
    T = ins[0][0].shape[0]
    bt = _ret_block(T)
    nb = T // bt
    hps = min(RET_HEADS_PER_STEP, RET_HEADS)
    n_in, n_out = len(ins), len(outs)

    def kern(lg_ref, *refs):
        in_refs, out_refs = refs[:n_in], refs[n_in:n_in + n_out]
        state_ref, dmat_ref = refs[n_in + n_out:]

        @pl.when(pl.program_id(1) == 0)
        def _():
            state_ref[...] = jnp.zeros_like(state_ref)
            for hh in range(hps):
                dmat_ref[hh] = _ret_dmat(lg_ref[hh][:, :1], bt)

        pos = lax.broadcasted_iota(jnp.int32, (bt, 1), 0).astype(F32)
        for hh in range(hps):
            lg = lg_ref[hh][:, :1]
            xi, zeta, gb = jnp.exp(lg * (pos + 1.0)), jnp.exp(lg * (bt - 1.0 - pos)), jnp.exp(lg * bt)
            tiles = [r[:, hh * w:(hh + 1) * w] for r, (_, w) in zip(in_refs, ins)]
            res = per_head(dmat_ref[hh], xi, zeta, gb, state_ref.at[hh], *tiles)
            for o_ref, (w, _), val in zip(out_refs, outs, res):
                o_ref[:, hh * w:(hh + 1) * w] = val.astype(o_ref.dtype)

    def blk(w):
        if rev:
            return pl.BlockSpec((bt, hps * w), lambda g, b: (nb - 1 - b, g))
        return pl.BlockSpec((bt, hps * w), lambda g, b: (b, g))

    return pl.pallas_call(
        kern, grid=(RET_HEADS // hps, nb),
        in_specs=[pl.BlockSpec((hps, 1, LANES), lambda g, b: (g, 0, 0))] + [blk(w) for _, w in ins],
        out_specs=[blk(w) for w, _ in outs],
        out_shape=[jax.ShapeDtypeStruct((T, RET_HEADS * w), dt) for w, dt in outs],
        scratch_shapes=[pltpu.VMEM((hps, RET_DK, RET_DV), F32), pltpu.VMEM((hps, bt, bt), F32)], name=name,
        compiler_params=_params(("arbitrary", "arbitrary")))(lgam, *[a for a, _ in ins])


def _ret_fwd(rq, rk, rv, lgam):
    def per_head(dmat, xi, zeta, gb, s_ref, q, k, v):
        sc = _dot_nt(q, k) * dmat
        y = _dot(_bf(sc), v) + _dot(q, _bf(s_ref[...])) * xi
        s_ref[...] = s_ref[...] * gb + _dot_tn(_bf(k.astype(F32) * zeta), v)
        return (y,)

    return _ret_scan("ret_fwd", per_head, lgam, [(rq, RET_DK), (rk, RET_DK), (rv, RET_DV)], [(RET_DV, BF16)], False)[0]


def _ret_bwd_q(rq, rk, rv, dy, lgam):
    def per_head(dmat, xi, zeta, gb, s_ref, k, v, dy):
        dp = _dot_nt(dy, v) * dmat
        dq = _dot(_bf(dp), k) + _dot_nt(dy, _bf(s_ref[...])) * xi
        s_ref[...] = s_ref[...] * gb + _dot_tn(_bf(k.astype(F32) * zeta), v)
        return (dq,)

    return _ret_scan("ret_bwd_q", per_head, lgam, [(rk, RET_DK), (rv, RET_DV), (dy, RET_DV)], [(RET_DK, F32)], False)[0]


def _ret_bwd_kv(rq, rk, rv, dy, lgam):
    def per_head(dmat, xi, zeta, gb, g_ref, q, k, v, dy):
        gs = _bf(g_ref[...])
        p = _dot_nt(q, k) * dmat
        dp = _dot_nt(dy, v) * dmat
        dv = _dot_tn(_bf(p), dy) + _dot(k, gs) * zeta
        dk = _dot_tn(_bf(dp), q) + _dot_nt(v, gs) * zeta
        g_ref[...] = g_ref[...] * gb + _dot_tn(_bf(q.astype(F32) * xi), dy)
        return dk, dv

    return _ret_scan("ret_bwd_kv", per_head, lgam, [(rq, RET_DK), (rk, RET_DK), (rv, RET_DV), (dy, RET_DV)],
                     [(RET_DK, F32), (RET_DV, BF16)], True)


def _attn_mask_t(tb):
    key = lax.broadcasted_iota(jnp.int32, (tb, tb), 0)
    qry = lax.broadcasted_iota(jnp.int32, (tb, tb), 1)
    return _chunk_of(key) <= _chunk_of(qry)


MASKED = -1e30
SUBLANES = 8


def _head_blocks(nb, w, tb):
    return pl.BlockSpec((nb, None, w, tb), lambda h, i: (0, h, 0, 0))


def _one_block(w, tb):
    return pl.BlockSpec((None, None, w, tb), lambda h, i: (i, h, 0, 0))


def _attn_fwd(k, qt, vt, exchange=()):
    T = k.shape[0]
    tb = _attn_block(T)
    nb = T // tb

    n_ex = len(exchange)

    def kern(qt_ref, k_ref, vt_ref, *refs):
        ex_in, (o_ref, lser_ref), ex_out = refs[:n_ex], refs[n_ex:n_ex + 2], refs[n_ex + 2:2 * n_ex + 2]
        m_ref, acc_ref, sa_ref, sb_ref = refs[2 * n_ex + 2:2 * n_ex + 6]
        sems = refs[2 * n_ex + 6:]
        qb = pl.program_id(1)
        first = jnp.logical_and(pl.program_id(0) == 0, qb == 0)
        last = jnp.logical_and(pl.program_id(0) == MLA_HEADS - 1, qb == nb - 1)
        if n_ex:
            @pl.when(first)
            def _():
                for send, _ in _chip_copies(ex_in, ex_out, *sems, True):
                    send.start()

        qt = qt_ref[...]
        m_ref[...] = jnp.full_like(m_ref, MASKED)
        acc_ref[...] = jnp.zeros_like(acc_ref)

        def scores(kb):
            rows = pl.ds(pl.multiple_of(kb * tb, tb), tb)
            return _dot(k_ref[rows, :], qt)

        def update(s, kb):
            m_old = m_ref[...]
            m_new = jnp.maximum(m_old, jnp.max(s, axis=0, keepdims=True))
            p = jnp.exp2(s - m_new)
            acc_ref[...] = acc_ref[...] * jnp.exp2(m_old - m_new) + _dot(vt_ref[kb], _bf(p))
            m_ref[...] = m_new

        def masked(s):
            return jnp.where(_attn_mask_t(tb), s, MASKED)

        sa_ref[...] = scores(0)

        def pair_body(j, carry):
            sb_ref[...] = scores(2 * j + 1)
            update(sa_ref[...], 2 * j)
            sa_ref[...] = scores(2 * j + 2)
            update(sb_ref[...], 2 * j + 1)
            return carry

        lax.fori_loop(0, qb // 2, pair_body, 0)

        @pl.when(qb % 2 == 0)
        def _():
            update(masked(sa_ref[...]), qb)

        @pl.when(qb % 2 == 1)
        def _():
            sb_ref[...] = masked(scores(qb))
            update(sa_ref[...], qb - 1)
            update(sb_ref[...], qb)

        l = acc_ref[MLA_DV:MLA_DV + 1, :]
        o_ref[...] = _bf((acc_ref[:MLA_DV, :] / l).T)
        lser_ref[...] = jnp.broadcast_to(m_ref[...] + jnp.log2(l), (SUBLANES, tb))
        if n_ex:
            @pl.when(last)
            def _():
                _wait_copies(_chip_copies(ex_in, ex_out, *sems, True))

    return pl.pallas_call(
        kern, grid=(MLA_HEADS, nb),
        in_specs=[_one_block(MLA_QK, tb), pl.BlockSpec((T, MLA_QK), lambda h, i: (0, h)),
                  _head_blocks(nb, MLA_DV + V_ONES, tb)] + [HBM_SPEC] * n_ex,
        out_specs=[pl.BlockSpec((tb, MLA_DV), lambda h, i: (i, h)), _one_block(SUBLANES, tb)] + [HBM_SPEC] * n_ex,
        out_shape=[jax.ShapeDtypeStruct((T, MLA_HEADS * MLA_DV), BF16),
                   jax.ShapeDtypeStruct((nb, MLA_HEADS, SUBLANES, tb), F32)] + _exchange_shapes(exchange),
        scratch_shapes=[pltpu.VMEM((1, tb), F32), pltpu.VMEM((MLA_DV + V_ONES, tb), F32),
                        pltpu.VMEM((tb, tb), F32), pltpu.VMEM((tb, tb), F32)]
        + (_dma_sems(n_ex * N_PEER_CHIPS) if n_ex else []),
        name="attn_fwd", compiler_params=_params(("arbitrary", "arbitrary")))(qt, k, vt, *exchange)


def _attn_bwd(q, k, v, do, qt, kt, dot_, lse_rows, delta_rows):
    T = q.shape[0]
    tb = _attn_block(T)
    nb = T // tb

    def kern(q_ref, k_ref, v_ref, do_ref, qt_ref, kt_ref, dot_ref, lse_ref, dl_ref, dk_ref, dv_ref, dqt_ref, dv_acc,
             sa_ref, pa_ref, sb_ref, pb_ref):
        kb = pl.program_id(1)
        kv, vv, ktv = k_ref[...], v_ref[...], kt_ref[...]
        dk_ref[...] = jnp.zeros_like(dk_ref)
        dv_acc[...] = jnp.zeros_like(dv_acc)

        @pl.when(kb == 0)
        def _():
            dqt_ref[...] = jnp.zeros_like(dqt_ref)

        def products(qb, s_ref, dp_ref, diagonal=False):
            s = _dot(kv, qt_ref[qb])
            s_ref[...] = jnp.where(_attn_mask_t(tb), s, MASKED) if diagonal else s
            dp_ref[...] = _dot(vv, dot_ref[qb])

        def consume(qb, s_ref, dp_ref):
            rows = pl.ds(pl.multiple_of(qb * tb, tb), tb)
            p = jnp.exp2(s_ref[...] - lse_ref[qb][:1, :])
            dv_acc[...] += _dot(_bf(p), do_ref[rows, :])
            ds = _bf(p * (dp_ref[...] - dl_ref[qb][:1, :]))
            dk_ref[...] += _dot(ds, q_ref[rows, :])
            dqt_ref[qb] += _dot(ktv, ds)

        n_full = nb - 1 - kb
        products(kb, sa_ref, pa_ref, diagonal=True)

        def pair_body(j, carry):
            q1 = kb + 1 + 2 * j
            products(q1, sb_ref, pb_ref)
            consume(q1 - 1, sa_ref, pa_ref)
            products(q1 + 1, sa_ref, pa_ref)
            consume(q1, sb_ref, pb_ref)
            return carry

        lax.fori_loop(0, n_full // 2, pair_body, 0)

        @pl.when(n_full % 2 == 0)
        def _():
            consume(nb - 1, sa_ref, pa_ref)

        @pl.when(n_full % 2 == 1)
        def _():
            products(nb - 1, sb_ref, pb_ref)
            consume(nb - 2, sa_ref, pa_ref)
            consume(nb - 1, sb_ref, pb_ref)

        dk_ref[...] = dk_ref[...] * (ATTN_SCALE / Q_PRESCALE)
        dv_ref[...] = _bf(dv_acc[...])

    def blk(w):
        return pl.BlockSpec((tb, w), lambda h, i: (i, h))

    def full(w):
        return pl.BlockSpec((T, w), lambda h, i: (0, h))

    return pl.pallas_call(
        kern, grid=(MLA_HEADS, nb),
        in_specs=[full(MLA_QK), blk(MLA_QK), blk(MLA_DV), full(MLA_DV), _head_blocks(nb, MLA_QK, tb),
                  _one_block(MLA_QK, tb), _head_blocks(nb, MLA_DV, tb), _head_blocks(nb, SUBLANES, tb),
                  _head_blocks(nb, SUBLANES, tb)],
        out_specs=[blk(MLA_QK), blk(MLA_DV), _head_blocks(nb, MLA_QK, tb)],
        out_shape=[jax.ShapeDtypeStruct((T, MLA_HEADS * MLA_QK), F32),
                   jax.ShapeDtypeStruct((T, MLA_HEADS * MLA_DV), BF16),
                   jax.ShapeDtypeStruct((nb, MLA_HEADS, MLA_QK, tb), F32)],
        scratch_shapes=[pltpu.VMEM((tb, MLA_DV), F32)] + [pltpu.VMEM((tb, tb), F32)] * 4,
        name="attn_bwd", compiler_params=_params(("arbitrary", "arbitrary")))(
            q, k, v, do, qt, kt, dot_, lse_rows, delta_rows)


def _group_norm(y):
    yc = y - _mean(y)
    rstd = lax.rsqrt(_mean(yc * yc) + EPS)
    return yc * rstd, rstd


def _mix_fwd(y, rg, o, gates, h1, gn_g, w_ret_o, w_mla_o, w_out, ln_g, ln_b, tm):
    T, D = h1.shape

    def body(i, y_ref, rg_ref, o_ref, gt_ref, h_ref, gn_ref, wr_ref, wm_ref, wo_ref, g_ref, b_ref,
             h2_ref, z_ref, yret_ref, ymla_ref, yr_ref, mix_ref):
        for h in range(RET_HEADS):
            sl = slice(h * RET_DV, (h + 1) * RET_DV)
            yn, _ = _group_norm(y_ref[:, sl].astype(F32))
            r = rg_ref[:, sl].astype(F32)
            yr_ref[:, sl] = _bf(r * _sigmoid(r) * (yn * gn_ref[:, sl]))
        yret = _dot(yr_ref[...], wr_ref[...])
        ymla = _dot(_bf(o_ref[...]), wm_ref[...])
        yret_ref[...] = _bf(yret)
        ymla_ref[...] = _bf(ymla)
        mix = _bf(_sigmoid(gt_ref[:, :D].astype(F32)) * yret + _sigmoid(gt_ref[:, D:].astype(F32)) * ymla)
        mix_ref[...] = mix
        z = ALPHA * h_ref[...] + _dot(mix, wo_ref[...])
        xhat, _ = _ln_stats(z)
        z_ref[...] = z
        h2_ref[...] = xhat * g_ref[...] + b_ref[...]

    return _rowcall("mix_fwd", body, T, tm, [y, rg, o, gates, h1], [gn_g, w_ret_o, w_mla_o, w_out, ln_g, ln_b],
                    [(D, F32), (D, F32), (D, BF16), (D, BF16), (RET_HEADS * RET_DV, BF16), (D, BF16)])


def _mix_bwd(dh2, z1, gates, yret, ymla, y, rg, o, gn_g, w_ret_o, w_mla_o, w_out, ln_g, tm, exchange=None):
    T, D = dh2.shape
    rv = RET_HEADS * RET_DV

    def body(i, dh_ref, z_ref, gt_ref, yret_ref, ymla_ref, y_ref, rg_ref, o_ref, gn_ref, wr_ref, wm_ref, wo_ref, g_ref,
             dz_ref, dgt_ref, drg_ref, dy_ref, do_ref, dyret_ref, dymla_ref, dg_ref, db_ref, dgn_ref, dot_ref,
             dl_ref):
        xhat, rstd = _ln_stats(z_ref[...])
        dz, dg, db = _ln_bwd(dh_ref[...], xhat, rstd, g_ref[...])
        _acc(i, dg_ref, dg)
        _acc(i, db_ref, db)
        dz_ref[...] = dz
        dmix = _dot_nt(_bf(dz), wo_ref[...])
        sr = _sigmoid(gt_ref[:, :D].astype(F32))
        sm = _sigmoid(gt_ref[:, D:].astype(F32))
        dgt_ref[:, :D] = _bf(dmix * yret_ref[...].astype(F32) * sr * (1.0 - sr))
        dgt_ref[:, D:] = _bf(dmix * ymla_ref[...].astype(F32) * sm * (1.0 - sm))
        dyret = _bf(dmix * sr)
        dymla = _bf(dmix * sm)
        dyret_ref[...] = dyret
        dymla_ref[...] = dymla
        dov = _dot_nt(dymla, wm_ref[...])
        do_ref[...] = _bf(dov)
        for h in range(MLA_HEADS):
            sl = slice(h * MLA_DV, (h + 1) * MLA_DV)
            dot_ref[h] = _bf(dov[:, sl].T)
            delta = jnp.sum(dov[:, sl] * o_ref[:, sl].astype(F32), axis=-1, keepdims=True)
            dl_ref[h] = jnp.broadcast_to(delta, (tm, LANES)).T[:SUBLANES, :]
        dyr = _dot_nt(dyret, wr_ref[...])
        dgn = []
        for h in range(RET_HEADS):
            sl = slice(h * RET_DV, (h + 1) * RET_DV)
            yn, grstd = _group_norm(y_ref[:, sl].astype(F32))
            r = rg_ref[:, sl].astype(F32)
            sig = _sigmoid(r)
            d = dyr[:, sl]
            drg_ref[:, sl] = _bf(d * (yn * gn_ref[:, sl]) * sig * (1.0 + r * (1.0 - sig)))
            dt = d * (r * sig)
            dgn.append(jnp.sum(dt * yn, axis=0, keepdims=True))
            dyn = dt * gn_ref[:, sl]
            dy_ref[:, sl] = _bf(grstd * (dyn - _mean(dyn) - yn * _mean(dyn * yn)))
        _acc(i, dgn_ref, jnp.concatenate(dgn, axis=1))

    return _rowcall("mix_bwd", body, T, tm, [dh2, z1, gates, yret, ymla, y, rg, o],
                    [gn_g, w_ret_o, w_mla_o, w_out, ln_g],
                    [(D, F32), (2 * D, BF16), (rv, BF16), (rv, BF16), (MLA_HEADS * MLA_DV, BF16), (D, BF16), (D, BF16)],
                    [((1, D), F32), ((1, D), F32), ((1, rv), F32)],
                    tiled_outs=[_transposed_blocks(T, tm, MLA_DV, BF16), _transposed_blocks(T, tm, SUBLANES, F32)],
                    exchange=exchange)


def _proj_mla_bwd(dqt, dk, dv, lat, tabs, w_uq, w_uk, w_uv, qn_g, kvn_g, tm):
    T = dk.shape[0]
    H = MLA_HEADS
    lat_w = Q_LORA + KV_LORA

    def body(i, dk_ref, dv_ref, lat_ref, c_ref, s1_ref, s2_ref, dqt_ref, wuq_ref, wuk_ref, wuv_ref, qg_ref, kg_ref,
             dlat_ref, dkpe_ref, dqb_ref, dkn_ref, dqg_ref, dkg_ref):
        c, s1, s2 = c_ref[...], s1_ref[...], s2_ref[...]
        dkpe = jnp.zeros((tm, LANES), F32)
        for h in range(H):
            o = h * MLA_QK
            dqh = dqt_ref[h].T * ATTN_SCALE
            dqb_ref[:, o:o + MLA_NOPE] = _bf(dqh[:, :MLA_NOPE])
            dqb_ref[:, o + MLA_NOPE:o + MLA_QK] = _bf(_rope_pe_bwd(dqh[:, MLA_NOPE:], c, s1, s2))
            dkn_ref[:, h * MLA_NOPE:(h + 1) * MLA_NOPE] = _bf(dk_ref[:, o:o + MLA_NOPE])
            dkpe += dk_ref[:, o + MLA_NOPE:o + MLA_QK]
        dkn_ref[:, H * MLA_NOPE:] = dv_ref[...]
        dkpe_ref[...] = _bf(_rope_pe_bwd(dkpe, c, s1, s2))
        dcqn = _dot_nt(dqb_ref[...], wuq_ref[...])
        dckn = _dot_nt(dkn_ref[:, :H * MLA_NOPE], wuk_ref[...]) + _dot_nt(dv_ref[...], wuv_ref[...])
        for dn, x, g_ref, dg_ref, sl in ((dcqn, lat_ref[:, :Q_LORA], qg_ref, dqg_ref, slice(0, Q_LORA)),
                                         (dckn, lat_ref[:, Q_LORA:], kg_ref, dkg_ref, slice(Q_LORA, lat_w))):
            xn, r = _rms(x, None)
            _acc(i, dg_ref, jnp.sum(dn * xn, axis=0, keepdims=True))
            dxn = dn * g_ref[...]
            dlat_ref[:, sl] = _bf(r * (dxn - xn * _mean(dxn * xn)))

    dqt_shape, dqt_spec = _transposed_blocks(T, tm, MLA_QK, F32)
    assert dqt.shape == dqt_shape.shape
    return _rowcall("proj_mla_bwd", body, T, tm, [dk, dv, lat, *tabs], [w_uq, w_uk, w_uv, qn_g, kvn_g],
                    [(lat_w, BF16), (LANES, BF16), (H * MLA_QK, BF16), (H * (MLA_NOPE + MLA_DV), BF16)],
                    [((1, Q_LORA), F32), ((1, KV_LORA), F32)], tiled_ins=[(dqt, dqt_spec)])


def _proj_bwd(drq, drk, drv, drg, dz1, dlat, dkpe, dgates, cos_r, sin_r, w_r, w_c, w_kpe, w_g, tm):
    T, D = dz1.shape
    qk = RET_HEADS * RET_DK
    rv = RET_HEADS * RET_DV
    o_lat = 2 * qk + 2 * rv
    o_kpe = o_lat + dlat.shape[1]
    o_gate = o_kpe + LANES
    o_end = o_gate + dgates.shape[1]
    width = -(-o_end // WG_TILE_N) * WG_TILE_N

    def body(i, drq_ref, drk_ref, drv_ref, drg_ref, dz_ref, dlat_ref, dkpe_ref, dgt_ref, cos_ref, sin_ref,
             wr_ref, wc_ref, wk_ref, wg_ref, dh_ref, dpr_ref):
        cos, sin = cos_ref[...], sin_ref[...]
        for src, off, scale in ((drq_ref, 0, 1.0), (drk_ref, qk, RET_DK ** -0.5)):
            for h in range(RET_HEADS):
                d = src[:, h * RET_DK:(h + 1) * RET_DK]
                dpr_ref[:, off + h * RET_DK:off + (h + 1) * RET_DK] = _bf(
                    (d * cos + _roll(d * sin, RET_DK // 2)) * scale)
        dpr_ref[:, 2 * qk:2 * qk + rv] = drv_ref[...]
        dpr_ref[:, 2 * qk + rv:o_lat] = drg_ref[...]
        dpr_ref[:, o_lat:o_kpe] = dlat_ref[...]
        dpr_ref[:, o_kpe:o_gate] = dkpe_ref[...]
        dpr_ref[:, o_gate:o_end] = dgt_ref[...]
        if width > o_end:
            dpr_ref[:, o_end:] = jnp.zeros((tm, width - o_end), BF16)
        dh_ref[...] = (ALPHA * dz_ref[...] + _dot_nt(dpr_ref[:, :o_lat], wr_ref[...])
                       + _dot_nt(dlat_ref[...], wc_ref[...]) + _dot_nt(dkpe_ref[...], wk_ref[...])
                       + _dot_nt(dgt_ref[...], wg_ref[...]))

    return _rowcall("proj_bwd", body, T, tm, [drq, drk, drv, drg, dz1, dlat, dkpe, dgates, cos_r, sin_r],
                    [w_r, w_c, w_kpe, w_g], [(D, F32), (width, BF16)])


def _ple_loss(h3, p, target, w_gate, w_proj, ln_g, ln_b, tm):
    T, D = h3.shape

    def body(i, h_ref, p_ref, t_ref, wg_ref, wp_ref, g_ref, b_ref, dh_ref, dgp_ref, dpp_ref, loss_ref, dg_ref, db_ref):
        hv = h_ref[...]
        sg = _sigmoid(_dot(_bf(hv), wg_ref[...]))
        pp = _dot(_bf(p_ref[...]), wp_ref[...])
        xhat, rstd = _ln_stats(ALPHA * hv + sg * pp)
        err = xhat * g_ref[...] + b_ref[...] - t_ref[...]
        row_loss = 0.5 * _mean(err * err)
        _acc(i, loss_ref, jnp.broadcast_to(jnp.sum(row_loss, axis=0, keepdims=True), (1, LANES)))
        dz, dg, db = _ln_bwd(err * (1.0 / D), xhat, rstd, g_ref[...])
        _acc(i, dg_ref, dg)
        _acc(i, db_ref, db)
        dgp = _bf(dz * pp * sg * (1.0 - sg))
        dgp_ref[...] = dgp
        dpp_ref[...] = _bf(dz * sg)
        dh_ref[...] = ALPHA * dz + _dot_nt(dgp, wg_ref[...])

    return _rowcall("ple_loss", body, T, tm, [h3, p, target], [w_gate, w_proj, ln_g, ln_b],
                    [(D, F32), (D, BF16), (D, BF16)], [((1, LANES), F32), ((1, D), F32), ((1, D), F32)])


def _ewise(name, fn, ins, n_out, out_dtype=F32):
    r, c = ins[0].shape
    tr = _tile(r, max(8, (1 << 19) // c // 8 * 8), 8)

    def kern(*refs):
        outs = fn(*[x[...] for x in refs[:len(ins)]])
        for o_ref, o in zip(refs[len(ins):], outs):
            o_ref[...] = o.astype(out_dtype)

    spec = pl.BlockSpec((tr, c), lambda i: (i, 0))
    return pl.pallas_call(kern, grid=(r // tr,), in_specs=[spec] * len(ins), out_specs=[spec] * n_out,
                          out_shape=[jax.ShapeDtypeStruct((r, c), out_dtype)] * n_out, name=name,
                          compiler_params=_params(("arbitrary",)))(*ins)


def _adamw_math(w, g, m, v):
    m = ADAM_B1 * m + (1.0 - ADAM_B1) * g
    v = ADAM_B2 * v + (1.0 - ADAM_B2) * (g * g)
    m_hat = m / (1.0 - ADAM_B1 ** ADAM_STEP)
    v_hat = v / (1.0 - ADAM_B2 ** ADAM_STEP)
    return -ADAM_LR * (m_hat / (jnp.sqrt(v_hat) + ADAM_EPS) + ADAM_WD * w), m, v


def _adamw(name, w, g, m, v):
    shape = w.shape
    c = shape[-1]
    flat = [t.reshape(-1, c) for t in (w, g, m, v)]
    return [t.reshape(shape) for t in _ewise(name, _adamw_math, flat, 3)]


def _adamw_halves(name, w, mine, other, m, v):
    _, r, c = w.shape
    rh = r // 2
    tr = _tile(rh, max(8, (1 << 18) // c // 8 * 8), 8)

    def kern(w_ref, m_ref, v_ref, mine_ref, other_ref, g_ref, d_ref, nm_ref, nv_ref):
        g = jnp.where(pl.program_id(0) == lax.axis_index("c"), mine_ref[...], other_ref[...])
        g_ref[...] = g
        d_ref[...], nm_ref[...], nv_ref[...] = _adamw_math(w_ref[...], g, m_ref[...], v_ref[...])

    half = pl.BlockSpec((None, tr, c), lambda hh, i: (hh, i, 0))
    part = pl.BlockSpec((tr, c), lambda hh, i: (i, 0))
    outs = pl.pallas_call(kern, grid=(2, rh // tr), in_specs=[half] * 3 + [part] * 2, out_specs=[half] * 4,
                          out_shape=[jax.ShapeDtypeStruct((2, rh, c), F32)] * 4, name=name,
                          compiler_params=_params(("arbitrary", "arbitrary")))(
                              w.reshape(2, rh, c), m.reshape(2, rh, c), v.reshape(2, rh, c), mine, other)
    return [o.reshape(1, r, c) for o in outs]


def _place():
    return lax.axis_index("x"), lax.axis_index("y"), lax.axis_index("c")


def _dma_sems(n):
    return [pltpu.SemaphoreType.DMA((n,)), pltpu.SemaphoreType.DMA((n,))]


N_PEER_CHIPS = N_CHIPS - 1


def _chips_exchange(name, srcs, broadcast):
    n = len(srcs)

    def kern(*refs):
        cps = _chip_copies(refs[:n], refs[n:2 * n], refs[2 * n], refs[2 * n + 1], broadcast)
        for send, _ in cps:
            send.start()
        _wait_copies(cps)

    return pl.pallas_call(
        kern, out_shape=_exchange_shapes(srcs), in_specs=[HBM_SPEC] * n, out_specs=[HBM_SPEC] * n,
        scratch_shapes=_dma_sems(n * N_PEER_CHIPS), name=name)(*srcs)


def _exchange_shapes(srcs):
    return [jax.ShapeDtypeStruct((N_CHIPS,) + s.shape[1:], s.dtype) for s in srcs]


def _chip_copies(src_refs, out_refs, send_sems, recv_sems, broadcast):
    x, y, c = _place()
    me = 2 * x + y
    peers = [(1 - x, y), (x, 1 - y), (1 - x, 1 - y)]
    cps = []
    for j, (px, py) in enumerate(peers):
        for a, (src_ref, out_ref) in enumerate(zip(src_refs, out_refs)):
            piece = src_ref.at[c] if broadcast else src_ref.at[2 * px + py]

            def copy(slot):
                return pltpu.make_async_remote_copy(
                    src_ref=piece, dst_ref=out_ref.at[slot], send_sem=send_sems.at[a * N_PEER_CHIPS + j],
                    recv_sem=recv_sems.at[a * N_PEER_CHIPS + j], device_id=(px, py, c), device_id_type=MESH)

            cps.append((copy(me), copy(2 * px + py)))
    return cps


def _wait_copies(cps):
    for _, landing in cps:
        landing.wait_recv()
    for send, _ in cps:
        send.wait_send()


def _sibling_swap(name, srcs, mode):
    n = len(srcs)
    per = N_PEER_CHIPS if mode == "others" else 1

    def kern(*refs):
        src_refs, out_refs = refs[:n], refs[n:2 * n]
        send_sems, recv_sems = refs[2 * n:]
        x, y, c = _place()
        slots = [2 * (1 - x) + y, 2 * x + 1 - y, 2 * (1 - x) + 1 - y]
        cps = []
        for a in range(n):
            if mode == "others":
                pieces = [(src_refs[a].at[k], out_refs[a].at[k]) for k in slots]
            else:
                pieces = [(src_refs[a].at[:, 1 - c] if mode == "halves" else src_refs[a], out_refs[a])]
            for j, (src, dst) in enumerate(pieces):
                cps.append(pltpu.make_async_remote_copy(
                    src_ref=src, dst_ref=dst, send_sem=send_sems.at[a * per + j], recv_sem=recv_sems.at[a * per + j],
                    device_id=(x, y, 1 - c), device_id_type=MESH))
        for cp in cps:
            cp.start()
        for cp in cps:
            cp.wait_recv()
        for cp in cps:
            cp.wait_send()

    def out_shape(s):
        return jax.ShapeDtypeStruct((s.shape[0],) + s.shape[2:] if mode == "halves" else s.shape, s.dtype)

    return pl.pallas_call(
        kern, out_shape=[out_shape(s) for s in srcs], in_specs=[HBM_SPEC] * n, out_specs=[HBM_SPEC] * n,
        scratch_shapes=_dma_sems(n * per), name=name)(*srcs)


def _all_devices(name, src, reduce):
    r, c = src.shape
    n_dev = 2 * N_CHIPS

    def kern(src_ref, out_ref, *scratch):
        if reduce:
            gat_ref, send_sems, recv_sems = scratch
        else:
            gat_ref = out_ref
            send_sems, recv_sems = scratch
        x, y, cc = _place()
        me = 4 * x + 2 * y + cc
        gat_ref[me] = src_ref[...]
        peers = []
        for j in range(1, n_dev):
            px = 1 - x if j & 4 else x
            py = 1 - y if j & 2 else y
            pc = 1 - cc if j & 1 else cc
            peers.append((px, py, pc))

        def copy(j, peer, slot):
            return pltpu.make_async_remote_copy(
                src_ref=src_ref, dst_ref=gat_ref.at[slot], send_sem=send_sems.at[j], recv_sem=recv_sems.at[j],
                device_id=peer, device_id_type=MESH)

        sends = [copy(j, peer, me) for j, peer in enumerate(peers)]
        for cp in sends:
            cp.start()
        for j, (px, py, pc) in enumerate(peers):
            copy(j, (px, py, pc), 4 * px + 2 * py + pc).wait_recv()
        for cp in sends:
            cp.wait_send()
        if reduce:
            total = gat_ref[0]
            for d in range(1, n_dev):
                total = total + gat_ref[d]
            out_ref[...] = total

    out_shape = jax.ShapeDtypeStruct((r, c) if reduce else (n_dev, r, c), src.dtype)
    scratch = ([pltpu.VMEM((n_dev, r, c), src.dtype)] if reduce else []) + _dma_sems(n_dev - 1)
    return pl.pallas_call(kern, out_shape=out_shape, in_specs=[VMEM_SPEC], out_specs=VMEM_SPEC,
                          scratch_shapes=scratch, name=name)(src)


def _halves(t, axis):
    return t.reshape(t.shape[:axis] + (2, t.shape[axis] // 2) + t.shape[axis + 1:])


def _by_core(mine, theirs, axis):
    c = lax.axis_index("c")
    both = jnp.where(c == 0, jnp.stack([mine, theirs], axis), jnp.stack([theirs, mine], axis))
    return both.reshape(both.shape[:axis] + (2 * both.shape[axis + 1],) + both.shape[axis + 2:])


def _with_own(own, others):
    me = 2 * lax.axis_index("x") + lax.axis_index("y")
    is_me = (jnp.arange(N_CHIPS, dtype=jnp.int32) == me)[:, None, None]
    return jnp.where(is_me, own[None], others)


def _join_shards(name, shards):
    _, r, c = shards.shape
    if name in COL_SHARDED:
        return shards.transpose(1, 0, 2).reshape(r, N_CHIPS * c)
    return shards.reshape(N_CHIPS * r, c)


def _split_shards(name, full):
    if full.ndim == 3:
        return full
    r, c = full.shape
    if name in COL_SHARDED:
        return jnp.stack([full[:, k * (c // N_CHIPS):(k + 1) * (c // N_CHIPS)] for k in range(N_CHIPS)])
    return full.reshape(N_CHIPS, r // N_CHIPS, c)


def _rope_tables(positions):
    pos = positions.reshape(-1).astype(F32)[:, None]
    half = RET_DK // 2
    ang = pos * (ROPE_BASE ** (-jnp.arange(half, dtype=F32) / half))
    cos_r = jnp.concatenate([jnp.cos(ang)] * 2, axis=1)
    sin_r = jnp.concatenate([-jnp.sin(ang), jnp.sin(ang)], axis=1)
    half = MLA_ROPE // 2
    ang = pos * (ROPE_BASE ** (-jnp.arange(half, dtype=F32) / half))
    zeros = jnp.zeros_like(ang)
    rest = LANES - MLA_ROPE
    c = jnp.concatenate([jnp.cos(ang)] * 2 + [jnp.ones((ang.shape[0], rest), F32)], axis=1)
    s1 = jnp.concatenate([-jnp.sin(ang), zeros, jnp.zeros((ang.shape[0], rest), F32)], axis=1)
    s2 = jnp.concatenate([zeros, jnp.sin(ang), jnp.zeros((ang.shape[0], rest), F32)], axis=1)
    return cos_r, sin_r, (c, s1, s2)


GATHER_GROUPS = (("ffn1_w_in", "ffn1_w_out"), ("w_in", "w_uq", "w_ukv"),
                 ("w_ret_o", "w_mla_o", "w_out", "ffn2_w_in", "ffn2_w_out", "ple_w_gate", "ple_w_proj"))
REDUCE_GROUPS = (("ple_w_gate", "ple_w_proj", "ffn2_w_in", "ffn2_w_out"),
                 ("w_out", "w_ret_o", "w_mla_o", "w_uq", "w_ukv", "w_in"), ("ffn1_w_in", "ffn1_w_out"))


def _gathered(tag, names, own, mine):
    theirs = _sibling_swap("gather_cores_" + tag, mine, "others")
    out = {}
    for n, m, t in zip(names, mine, theirs):
        full = _with_own(own[n], _by_core(m, t, 1))
        out[n] = full if n in ("ffn1_w_in", "ffn2_w_in") else _join_shards(n, full)
    return out


def _chip_sums(tag, names, grads):
    halves = [_halves(_split_shards(n, grads[n]), 1) for n in names]
    theirs = _sibling_swap("reduce_cores_" + tag, halves, "halves")

    def one(n, g, t):
        k, _, r, c = g.shape
        tr = _tile(r, max(8, (1 << 17) // c // 8 * 8), 8)

        def kern(g_ref, t_ref, o_ref):
            mine = jnp.where(lax.axis_index("c") == 0, g_ref[:, 0], g_ref[:, 1])
            o_ref[...] = _bf(mine.astype(F32) + t_ref[...].astype(F32))

        spec = pl.BlockSpec((k, tr, c), lambda i: (0, i, 0))
        return pl.pallas_call(kern, grid=(r // tr,),
                              in_specs=[pl.BlockSpec((k, 2, tr, c), lambda i: (0, 0, i, 0)), spec], out_specs=spec,
                              out_shape=jax.ShapeDtypeStruct((k, r, c), BF16), name="reduce_cores_add_" + n,
                              compiler_params=_params(("arbitrary",)))(g, t)

    return [one(n, g, t) for n, g, t in zip(names, halves, theirs)]


def _block_totals(names, sums, parts):
    def one(n, s, pt):
        _, r, c = s.shape
        tr = _tile(r, max(8, (1 << 17) // c // 8 * 8), 8)

        def kern(s_ref, p_ref, o_ref):
            me = 2 * lax.axis_index("x") + lax.axis_index("y")
            terms = [jnp.where(k == me, s_ref[k], p_ref[k]).astype(F32) for k in range(N_CHIPS)]
            o_ref[...] = ((terms[0] + terms[1]) + terms[2]) + terms[3]

        spec = pl.BlockSpec((N_CHIPS, tr, c), lambda i: (0, i, 0))
        return pl.pallas_call(kern, grid=(r // tr,), in_specs=[spec, spec],
                              out_specs=pl.BlockSpec((tr, c), lambda i: (i, 0)),
                              out_shape=jax.ShapeDtypeStruct((r, c), F32), name="reduce_chips_add_" + n,
                              compiler_params=_params(("arbitrary",)))(s, pt)

    return [one(n, s, pt) for n, s, pt in zip(names, sums, parts)]


def _local_step(x, p, positions, target, shards, ln_g, ln_b, gn_g, qn_g, kvn_g):
    T, D = x.shape
    tm = min(256, T)
    H = MLA_HEADS
    qk, rv = RET_HEADS * RET_DK, RET_HEADS * RET_DV
    cos_r, sin_r, tabs = _rope_tables(positions)
    lgam = jnp.broadcast_to(jnp.log(1.0 - 2.0 ** (-5.0 - jnp.arange(RET_HEADS, dtype=F32)))[:, None, None],
                            (RET_HEADS, 1, LANES))
    lng = [ln_g[k:k + 1] for k in range(N_LN)]
    lnb = [ln_b[k:k + 1] for k in range(N_LN)]
    own = {n: _bf(shards[n]) for n in BIG_WEIGHTS}
    to_send = [[_halves(own[n], 0) for n in names] for names in GATHER_GROUPS]

    w = _gathered("a", GATHER_GROUPS[0], own, _chips_exchange("gather_chips_a", to_send[0], True))
    h1, z0, a1, *arrived = _ffn_fwd("ffn1_fwd", x, w["ffn1_w_in"], w["ffn1_w_out"], lng[0], lnb[0], 2 * tm,
                                    exchange=(to_send[1], True))
    w.update(_gathered("b", GATHER_GROUPS[1], own, arrived))

    w_in = w["w_in"]
    o_lat, o_kpe, o_gate = 2 * qk + 2 * rv, 2 * qk + 2 * rv + Q_LORA + KV_LORA, 2 * qk + 2 * rv + Q_LORA + KV_LORA + MLA_ROPE
    w_r, w_c = w_in[:, :o_lat], w_in[:, o_lat:o_kpe]
    w_kpe = jnp.pad(w_in[:, o_kpe:o_gate], ((0, 0), (0, LANES - MLA_ROPE)))
    w_g = w_in[:, o_gate:]
    w_uq = jnp.pad(w["w_uq"].reshape(Q_LORA, H, MLA_NOPE + MLA_ROPE),
                   ((0, 0), (0, 0), (0, MLA_QK - MLA_NOPE - MLA_ROPE))).reshape(Q_LORA, H * MLA_QK)
    w_ukv = w["w_ukv"].reshape(KV_LORA, H, MLA_NOPE + MLA_DV)
    w_uk = w_ukv[:, :, :MLA_NOPE].reshape(KV_LORA, H * MLA_NOPE)
    w_uv = w_ukv[:, :, MLA_NOPE:].reshape(KV_LORA, H * MLA_DV)

    rq, rk, rvv, rg = _proj_ret(h1, w_r, cos_r, sin_r, 2 * tm)
    lat, gates, q, k, v, latn, qt, kt, vt = _proj_mla(h1, tabs, w_c, w_kpe, w_g, w_uq, w_uk, w_uv, qn_g, kvn_g, 2 * tm)
    y = _ret_fwd(rq, rk, rvv, lgam)
    o, lse_rows, *arrived = _attn_fwd(k, qt, vt, exchange=to_send[2])
    w.update(_gathered("c", GATHER_GROUPS[2], own, arrived))
    h2, z1, yret, ymla, yr, mix = _mix_fwd(y, rg, o, gates, h1, gn_g, w["w_ret_o"], w["w_mla_o"], w["w_out"],
                                           lng[1], lnb[1], 2 * tm)
    h3, z2, a2 = _ffn_fwd("ffn2_fwd", h2, w["ffn2_w_in"], w["ffn2_w_out"], lng[2], lnb[2], 2 * tm)

    dh3, dgp, dpp, loss, dg3, db3 = _ple_loss(h3, p, target, w["ple_w_gate"], w["ple_w_proj"], lng[3], lnb[3], 2 * tm)
    dh2, da2, s2, df2, dg2, db2 = _ffn_bwd("ffn2_bwd", dh3, z2, a2, w["ffn2_w_in"], w["ffn2_w_out"], lng[2], tm)
    grads = {"ple_w_gate": _mm_tn("wg_ple_gate", h3, dgp), "ple_w_proj": _mm_tn("wg_ple_proj", p, dpp),
             "ffn2_w_in": _mm_tn("wg_ffn2_in", h2, da2, n_split=N_CHIPS), "ffn2_w_out": _mm_tn("wg_ffn2_out", s2, df2)}
    sums1 = _chip_sums("1", REDUCE_GROUPS[0], grads)
    (dz1, dgates, drg, dy, do, dyret, dymla, dg1, db1, dgn, dot_, delta_rows, *parts1) = _mix_bwd(
        dh2, z1, gates, yret, ymla, y, rg, o, gn_g, w["w_ret_o"], w["w_mla_o"], w["w_out"], lng[1], tm,
        exchange=(sums1, False))
    drq = _ret_bwd_q(rq, rk, rvv, dy, lgam)
    drk, drv = _ret_bwd_kv(rq, rk, rvv, dy, lgam)
    dk, dv, dqt = _attn_bwd(q, k, v, do, qt, kt, dot_, lse_rows, delta_rows)
    dlat, dkpe, dqb, dkv, dqg, dkg = _proj_mla_bwd(dqt, dk, dv, lat, tabs, w_uq, w_uk, w_uv, qn_g, kvn_g, 2 * tm)
    dh1, dpr = _proj_bwd(drq, drk, drv, drg, dz1, dlat, dkpe, dgates, cos_r, sin_r, w_r, w_c, w_kpe, w_g, tm)
    g_uq = _mm_tn("wg_uq", latn[:, :Q_LORA], dqb).reshape(Q_LORA, H, MLA_QK)[:, :, :MLA_NOPE + MLA_ROPE]
    g_ukv = _mm_tn("wg_ukv", latn[:, Q_LORA:], dkv)
    g_uk = g_ukv[:, :H * MLA_NOPE].reshape(KV_LORA, H, MLA_NOPE)
    g_uv = g_ukv[:, H * MLA_NOPE:].reshape(KV_LORA, H, MLA_DV)
    g_in = _mm_tn("wg_in", h1, dpr)
    grads.update({
        "w_in": jnp.concatenate([g_in[:, :o_kpe + MLA_ROPE], g_in[:, o_kpe + LANES:o_kpe + LANES + 2 * D]], axis=1),
        "w_ret_o": _mm_tn("wg_ret_o", yr, dyret),
        "w_uq": g_uq.reshape(Q_LORA, H * (MLA_NOPE + MLA_ROPE)),
        "w_ukv": jnp.concatenate([g_uk, g_uv], axis=2).reshape(KV_LORA, H * (MLA_NOPE + MLA_DV)),
        "w_mla_o": _mm_tn("wg_mla_o", o, dymla),
        "w_out": _mm_tn("wg_out", mix, dz1)})
    sums2 = _chip_sums("2", REDUCE_GROUPS[1], grads)
    dx, da1, s1, df1, dg0, db0, *parts2 = _ffn_bwd("ffn1_bwd", dh1, z0, a1, w["ffn1_w_in"], w["ffn1_w_out"], lng[0], tm,
                                                   exchange=(sums2, False))
    grads.update({"ffn1_w_in": _mm_tn("wg_ffn1_in", x, da1, n_split=N_CHIPS),
                  "ffn1_w_out": _mm_tn("wg_ffn1_out", s1, df1)})
    sums3 = _chip_sums("3", REDUCE_GROUPS[2], grads)
    parts3 = _chips_exchange("reduce_chips_3", sums3, False)

    names = [n for group in REDUCE_GROUPS for n in group]
    totals = _block_totals(names, sums1 + sums2 + sums3, list(parts1) + list(parts2) + list(parts3))
    others = _sibling_swap("reduce_join", totals, "whole")
    reduced = {n: (t, o_) for n, t, o_ in zip(names, totals, others)}
    small = {"ln_g": jnp.concatenate([dg0, dg1, dg2, dg3], axis=0), "ln_b": jnp.concatenate([db0, db1, db2, db3], axis=0),
             "ret_gn_g": dgn, "q_norm_g": dqg, "kv_norm_g": dkg}
    return loss[0, 0], dx, reduced, small


def kernel(x, p, positions, ln_g, ln_b, ffn1_w_in, ffn1_w_out, w_in, ret_gn_g, w_ret_o, q_norm_g, kv_norm_g, w_uq, w_ukv, w_mla_o, w_out, ffn2_w_in, ffn2_w_out, ple_w_gate, ple_w_proj, loss_target, m_ln_g, m_ln_b, m_ffn1_w_in, m_ffn1_w_out, m_w_in, m_ret_gn_g, m_w_ret_o, m_q_norm_g, m_kv_norm_g, m_w_uq, m_w_ukv, m_w_mla_o, m_w_out, m_ffn2_w_in, m_ffn2_w_out, m_ple_w_gate, m_ple_w_proj, v_ln_g, v_ln_b, v_ffn1_w_in, v_ffn1_w_out, v_w_in, v_ret_gn_g, v_w_ret_o, v_q_norm_g, v_kv_norm_g, v_w_uq, v_w_ukv, v_w_mla_o, v_w_out, v_ffn2_w_in, v_ffn2_w_out, v_ple_w_gate, v_ple_w_proj):
    names = ("ln_g", "ln_b", "ffn1_w_in", "ffn1_w_out", "w_in", "ret_gn_g", "w_ret_o", "q_norm_g", "kv_norm_g", "w_uq",
             "w_ukv", "w_mla_o", "w_out", "ffn2_w_in", "ffn2_w_out", "ple_w_gate", "ple_w_proj")
    weights = dict(zip(names, (ln_g, ln_b, ffn1_w_in, ffn1_w_out, w_in, ret_gn_g, w_ret_o, q_norm_g, kv_norm_g, w_uq,
                               w_ukv, w_mla_o, w_out, ffn2_w_in, ffn2_w_out, ple_w_gate, ple_w_proj)))
    m_in = dict(zip(names, (m_ln_g, m_ln_b, m_ffn1_w_in, m_ffn1_w_out, m_w_in, m_ret_gn_g, m_w_ret_o, m_q_norm_g,
                            m_kv_norm_g, m_w_uq, m_w_ukv, m_w_mla_o, m_w_out, m_ffn2_w_in, m_ffn2_w_out, m_ple_w_gate,
                            m_ple_w_proj)))
    v_in = dict(zip(names, (v_ln_g, v_ln_b, v_ffn1_w_in, v_ffn1_w_out, v_w_in, v_ret_gn_g, v_w_ret_o, v_q_norm_g,
                            v_kv_norm_g, v_w_uq, v_w_ukv, v_w_mla_o, v_w_out, v_ffn2_w_in, v_ffn2_w_out, v_ple_w_gate,
                            v_ple_w_proj)))
    chip = 2 * lax.axis_index("x") + lax.axis_index("y")
    D = x.shape[-1]
    dq = D // N_CHIPS

    shards = {n: weights[n][0] for n in BIG_WEIGHTS}
    ln_all = _all_devices("gather_ln", jnp.concatenate([ln_g[0], ln_b[0]], axis=0), False)
    ln_full = ln_all[::2].transpose(1, 0, 2).reshape(2 * N_LN, D)
    loss, dx, big, small = _local_step(x[0], p[0, 0], positions, loss_target[0], shards, ln_full[:N_LN],
                                       ln_full[N_LN:], ret_gn_g, q_norm_g, kv_norm_g)

    loss = lax.psum(loss, ("x", "y", "c"))
    small_names = ("ln_g", "ln_b", "ret_gn_g", "q_norm_g", "kv_norm_g")
    flat = jnp.concatenate([small[n].reshape(-1) for n in small_names])
    rows = -(-flat.shape[0] // LANES // 8) * 8
    flat = jnp.pad(flat, (0, rows * LANES - flat.shape[0])).reshape(rows, LANES)
    flat = _all_devices("reduce_small", flat, True).reshape(-1)
    off = 0
    for n in small_names:
        size = small[n].size
        small[n] = flat[off:off + size].reshape(small[n].shape)
        off += size
    g_out = {}
    for n in ("ln_g", "ln_b"):
        g_out[n] = lax.dynamic_slice_in_dim(small[n], chip * dq, dq, axis=1)
    for n in ("ret_gn_g", "q_norm_g", "kv_norm_g"):
        g_out[n] = small[n]

    deltas, new_m, new_v = {}, {}, {}
    for n in names:
        if n in big:
            g_out[n], deltas[n], new_m[n], new_v[n] = _adamw_halves("adamw_" + n, weights[n], *big[n], m_in[n], v_in[n])
        else:
            g_out[n] = g_out[n].reshape(weights[n].shape)
            deltas[n], new_m[n], new_v[n] = _adamw("adamw_" + n, weights[n], g_out[n], m_in[n], v_in[n])
    return (loss, dx[None], *[g_out[n] for n in names], *[deltas[n] for n in names], *[new_m[n] for n in names],
            *[new_v[n] for n in names])
```

```python
import functools

import jax
import jax.numpy as jnp
from jax import lax
from jax.experimental import pallas as pl
from jax.experimental.pallas import tpu as pltpu

CHUNK = 64
RET_HEADS = 8
RET_DK = 128
RET_DV = 256
MLA_HEADS = 8
MLA_NOPE = 128
MLA_ROPE = 64
MLA_DV = 128
MLA_QK = 256
Q_LORA = 256
KV_LORA = 256
ROPE_BASE = 10000.0
EPS = 1e-5
N_LN = 4
ALPHA = 2.0 ** 0.25
ADAM_LR = 0.001
ADAM_B1 = 0.9
ADAM_B2 = 0.999
ADAM_EPS = 1e-08
ADAM_WD = 0.01
ADAM_STEP = 10

LANES = 128
VMEM_LIMIT = 60 << 20
N_CHIPS = 4

F32 = jnp.float32
BF16 = jnp.bfloat16
MESH = pl.DeviceIdType.MESH
HBM_SPEC = pl.BlockSpec(memory_space=pltpu.HBM)
VMEM_SPEC = pl.BlockSpec(memory_space=pltpu.VMEM)

BIG_WEIGHTS = ("ffn1_w_in", "ffn1_w_out", "w_in", "w_ret_o", "w_uq", "w_ukv", "w_mla_o", "w_out",
               "ffn2_w_in", "ffn2_w_out", "ple_w_gate", "ple_w_proj")
COL_SHARDED = ("ffn1_w_in", "w_in", "w_uq", "w_ukv", "ffn2_w_in", "ple_w_proj")


def _dot(a, b):
    return jnp.dot(a, b, preferred_element_type=F32)


def _dot_nt(a, b):
    return lax.dot_general(a, b, (((1,), (1,)), ((), ())), preferred_element_type=F32)


def _dot_tn(a, b):
    return lax.dot_general(a, b, (((0,), (0,)), ((), ())), preferred_element_type=F32)


def _bf(x):
    return x.astype(BF16)


def _sigmoid(x):
    return 0.5 * jnp.tanh(0.5 * x) + 0.5


def _mean(x):
    return jnp.mean(x, axis=-1, keepdims=True)


def _ln_stats(z):
    zc = z - _mean(z)
    rstd = lax.rsqrt(_mean(zc * zc) + EPS)
    return zc * rstd, rstd


def _ln_bwd(dy, xhat, rstd, g):
    dxhat = dy * g
    dz = rstd * (dxhat - _mean(dxhat) - xhat * _mean(dxhat * xhat))
    return dz, jnp.sum(dy * xhat, axis=0, keepdims=True), jnp.sum(dy, axis=0, keepdims=True)


def _roll(x, shift):
    return pltpu.roll(x, shift, 1)


def _chunk_of(idx):
    return jnp.right_shift(idx, CHUNK.bit_length() - 1)


def _tile(n, cap, mult=LANES):
    if n <= cap:
        return n
    for t in range(cap - cap % mult, 0, -mult):
        if n % t == 0:
            return t
    return n


def _zero_map(nd, *_):
    return (0,) * nd


def _params(sem):
    return pltpu.CompilerParams(dimension_semantics=sem, vmem_limit_bytes=VMEM_LIMIT)


def _rowcall(name, body, n_rows, tm, row_ins, full_ins, row_outs, acc_outs=(), tiled_outs=(), tiled_ins=(),
             exchange=None):
    n_steps = n_rows // tm
    ex_srcs, broadcast = exchange if exchange else ((), False)
    n_ex = len(ex_srcs)
    n_in = len(row_ins) + len(tiled_ins) + len(full_ins)
    n_out = len(row_outs) + len(acc_outs) + len(tiled_outs)

    def kern(*refs):
        step = pl.program_id(0)
        ex_in, ex_out = refs[n_in:n_in + n_ex], refs[n_in + n_ex + n_out:n_in + 2 * n_ex + n_out]
        sems = refs[n_in + 2 * n_ex + n_out:]
        if n_ex:
            @pl.when(step == 0)
            def _():
                for send, _ in _chip_copies(ex_in, ex_out, *sems, broadcast):
                    send.start()

        body(step, *refs[:n_in], *refs[n_in + n_ex:n_in + n_ex + n_out])
        if n_ex:
            @pl.when(step == n_steps - 1)
            def _():
                _wait_copies(_chip_copies(ex_in, ex_out, *sems, broadcast))

    in_specs = [pl.BlockSpec((tm, a.shape[1]), lambda i: (i, 0)) for a in row_ins]
    in_specs += [spec for (_, spec) in tiled_ins]
    row_ins = list(row_ins) + [a for (a, _) in tiled_ins]
    in_specs += [pl.BlockSpec(a.shape, functools.partial(_zero_map, a.ndim), pipeline_mode=pl.Buffered(1))
                 for a in full_ins]
    in_specs += [HBM_SPEC] * n_ex
    out_specs = [pl.BlockSpec((tm, w), lambda i: (i, 0)) for (w, _) in row_outs]
    out_specs += [pl.BlockSpec(s, functools.partial(_zero_map, len(s))) for (s, _) in acc_outs]
    out_specs += [spec for (_, spec) in tiled_outs]
    out_specs += [HBM_SPEC] * n_ex
    out_shape = [jax.ShapeDtypeStruct((n_rows, w), dt) for (w, dt) in row_outs]
    out_shape += [jax.ShapeDtypeStruct(s, dt) for (s, dt) in acc_outs]
    out_shape += [shape for (shape, _) in tiled_outs]
    out_shape += _exchange_shapes(ex_srcs)
    return pl.pallas_call(kern, grid=(n_steps,), in_specs=in_specs, out_specs=out_specs, out_shape=out_shape,
                          scratch_shapes=_dma_sems(n_ex * N_PEER_CHIPS) if n_ex else [], name=name,
                          compiler_params=_params(("arbitrary",)))(*row_ins, *full_ins, *ex_srcs)


def _acc(step, ref, val):
    @pl.when(step == 0)
    def _():
        ref[...] = val

    @pl.when(step != 0)
    def _():
        ref[...] += val


def _ffn_fwd(name, x, w_in4, w_out, ln_g, ln_b, tm, exchange=None):
    T, D = x.shape
    fh = w_in4.shape[2]

    def body(i, x_ref, w4_ref, wo_ref, g_ref, b_ref, h_ref, z_ref, a_ref):
        xv = x_ref[...]
        xb = _bf(xv)
        f = jnp.zeros((tm, D), F32)
        for k in range(2):
            gk = _dot(xb, w4_ref[k])
            uk = _dot(xb, w4_ref[2 + k])
            a_ref[:, k * fh:(k + 1) * fh] = _bf(gk)
            a_ref[:, (2 + k) * fh:(3 + k) * fh] = _bf(uk)
            f += _dot(_bf(gk * _sigmoid(gk) * uk), wo_ref[k * fh:(k + 1) * fh, :])
        z = ALPHA * xv + 0.5 * f
        xhat, _ = _ln_stats(z)
        z_ref[...] = z
        h_ref[...] = xhat * g_ref[...] + b_ref[...]

    return _rowcall(name, body, T, tm, [x], [w_in4, w_out, ln_g, ln_b],
                    [(D, F32), (D, F32), (4 * fh, BF16)], exchange=exchange)


def _ffn_bwd(name, dh, z, a, w_in4, w_out, ln_g, tm, exchange=None):
    T, D = dh.shape
    fh = w_in4.shape[2]

    def body(i, dh_ref, z_ref, a_ref, w4_ref, wo_ref, g_ref, dx_ref, da_ref, s_ref, df_ref, dg_ref, db_ref):
        xhat, rstd = _ln_stats(z_ref[...])
        dz, dg, db = _ln_bwd(dh_ref[...], xhat, rstd, g_ref[...])
        _acc(i, dg_ref, dg)
        _acc(i, db_ref, db)
        dfb = _bf(0.5 * dz)
        df_ref[...] = dfb
        dx = ALPHA * dz
        for k in range(2):
            gk = a_ref[:, k * fh:(k + 1) * fh].astype(F32)
            uk = a_ref[:, (2 + k) * fh:(3 + k) * fh].astype(F32)
            ds = _dot_nt(dfb, wo_ref[k * fh:(k + 1) * fh, :])
            sig = _sigmoid(gk)
            silu = gk * sig
            dgk = _bf(ds * uk * sig * (1.0 + gk * (1.0 - sig)))
            duk = _bf(ds * silu)
            s_ref[:, k * fh:(k + 1) * fh] = _bf(silu * uk)
            da_ref[:, k * fh:(k + 1) * fh] = dgk
            da_ref[:, (2 + k) * fh:(3 + k) * fh] = duk
            dx += _dot_nt(dgk, w4_ref[k]) + _dot_nt(duk, w4_ref[2 + k])
        dx_ref[...] = dx

    return _rowcall(name, body, T, tm, [dh, z, a], [w_in4, w_out, ln_g],
                    [(D, F32), (4 * fh, BF16), (2 * fh, BF16), (D, BF16)],
                    [((1, D), F32), ((1, D), F32)], exchange=exchange)


WG_TILE_N = 1536


def _mm_tn(name, a, b, out_dtype=BF16, n_split=1, exchange=()):
    T, M = a.shape
    N = b.shape[1]
    tk = _tile(T, 2048, 8)
    tm = _tile(M, 1408)
    tn = _tile(N // n_split, WG_TILE_N)
    per = N // n_split // tn
    nk = T // tk
    n_ex = len(exchange)
    grid = (M // tm, N // tn, nk)
    if n_split > 1:
        out_spec = pl.BlockSpec((None, tm, tn), lambda i, j, k: (j // per, i, j % per))
        out_shape = jax.ShapeDtypeStruct((n_split, M, N // n_split), out_dtype)
    else:
        out_spec = pl.BlockSpec((tm, tn), lambda i, j, k: (i, j))
        out_shape = jax.ShapeDtypeStruct((M, N), out_dtype)

    def kern(a_ref, b_ref, *refs):
        ex_in, o_ref, ex_out = refs[:n_ex], refs[n_ex], refs[n_ex + 1:2 * n_ex + 1]
        acc_ref, sems = refs[2 * n_ex + 1], refs[2 * n_ex + 2:]
        k = pl.program_id(2)
        at_end = [pl.program_id(d) == grid[d] - 1 for d in range(3)]
        if n_ex:
            @pl.when(jnp.logical_and(jnp.logical_and(pl.program_id(0) == 0, pl.program_id(1) == 0), k == 0))
            def _():
                for send, _ in _chip_copies(ex_in, ex_out, *sems, False):
                    send.start()

        part = _dot_tn(_bf(a_ref[...]), _bf(b_ref[...]))

        @pl.when(k == 0)
        def _():
            acc_ref[...] = part

        @pl.when(k != 0)
        def _():
            acc_ref[...] += part

        @pl.when(k == nk - 1)
        def _():
            o_ref[...] = acc_ref[...].astype(out_dtype)

        if n_ex:
            @pl.when(jnp.logical_and(jnp.logical_and(at_end[0], at_end[1]), at_end[2]))
            def _():
                _wait_copies(_chip_copies(ex_in, ex_out, *sems, False))

    outs = pl.pallas_call(
        kern, grid=grid,
        in_specs=[pl.BlockSpec((tk, tm), lambda i, j, k: (k, i)), pl.BlockSpec((tk, tn), lambda i, j, k: (k, j))]
        + [HBM_SPEC] * n_ex,
        out_specs=[out_spec] + [HBM_SPEC] * n_ex, out_shape=[out_shape] + _exchange_shapes(exchange),
        scratch_shapes=[pltpu.VMEM((tm, tn), F32)] + (_dma_sems(n_ex * N_PEER_CHIPS) if n_ex else []), name=name,
        compiler_params=_params(("arbitrary", "arbitrary", "arbitrary")))(a, b, *exchange)
    return outs if n_ex else outs[0]


def _proj_ret(h1, w_r, cos_r, sin_r, tm):
    T, D = h1.shape
    qk = RET_HEADS * RET_DK
    rv = RET_HEADS * RET_DV

    def body(i, h_ref, cos_ref, sin_ref, w_ref, q_ref, k_ref, v_ref, g_ref):
        hb = _bf(h_ref[...])
        cos, sin = cos_ref[...], sin_ref[...]
        for out_ref, off, scale in ((q_ref, 0, 1.0), (k_ref, qk, RET_DK ** -0.5)):
            pr = _dot(hb, w_ref[:, off:off + qk])
            for h in range(RET_HEADS):
                t = pr[:, h * RET_DK:(h + 1) * RET_DK]
                out_ref[:, h * RET_DK:(h + 1) * RET_DK] = _bf((t * cos + _roll(t, RET_DK // 2) * sin) * scale)
        v_ref[...] = _bf(_dot(hb, w_ref[:, 2 * qk:2 * qk + rv]))
        g_ref[...] = _bf(_dot(hb, w_ref[:, 2 * qk + rv:2 * qk + 2 * rv]))

    return _rowcall("proj_ret", body, T, tm, [h1, cos_r, sin_r], [w_r],
                    [(qk, BF16), (qk, BF16), (rv, BF16), (rv, BF16)])


def _rope_pe(t, c, s1, s2):
    return t * c + _roll(t, LANES - MLA_ROPE // 2) * s1 + _roll(t, MLA_ROPE // 2) * s2


def _rope_pe_bwd(dy, c, s1, s2):
    return dy * c + _roll(dy * s1, MLA_ROPE // 2) + _roll(dy * s2, LANES - MLA_ROPE // 2)


def _rms(x, g):
    r = lax.rsqrt(_mean(x * x) + EPS)
    return x * r, r


def _attn_block(T):
    return min(512, T)


def _transposed_blocks(T, tm, w, dtype):
    tb = _attn_block(T)
    per = tb // tm
    return (jax.ShapeDtypeStruct((T // tb, MLA_HEADS, w, tb), dtype),
            pl.BlockSpec((None, MLA_HEADS, w, tm), lambda i: (i // per, 0, 0, i % per)))


ATTN_SCALE = (MLA_NOPE + MLA_ROPE) ** -0.5
LOG2E = 1.4426950408889634
Q_PRESCALE = ATTN_SCALE * LOG2E
V_ONES = 16


def _proj_mla(h1, tabs, w_c, w_kpe, w_g, w_uq, w_uk, w_uv, qn_g, kvn_g, tm):
    T, D = h1.shape
    H = MLA_HEADS

    def body(i, h_ref, c_ref, s1_ref, s2_ref, wc_ref, wk_ref, wg_ref, wuq_ref, wuk_ref, wuv_ref, qg_ref, kg_ref,
             lat_ref, gt_ref, q_ref, k_ref, v_ref, ln_ref, qt_ref, kt_ref, vt_ref):
        hb = _bf(h_ref[...])
        c, s1, s2 = c_ref[...], s1_ref[...], s2_ref[...]
        lat = _dot(hb, wc_ref[...])
        lat_ref[...] = lat
        gt_ref[...] = _bf(_dot(hb, wg_ref[...]))
        cqn, _ = _rms(lat[:, :Q_LORA], None)
        ckn, _ = _rms(lat[:, Q_LORA:], None)
        cqn = _bf(cqn * qg_ref[...])
        ckn = _bf(ckn * kg_ref[...])
        ln_ref[:, :Q_LORA] = cqn
        ln_ref[:, Q_LORA:] = ckn
        q = _dot(cqn, wuq_ref[...])
        kn = _dot(ckn, wuk_ref[...])
        vv = _dot(ckn, wuv_ref[...])
        v_ref[...] = _bf(vv)
        kpe = _rope_pe(_dot(hb, wk_ref[...]), c, s1, s2)
        ones = jnp.ones((V_ONES, tm), BF16)
        for h in range(H):
            o = h * MLA_QK
            qh = jnp.concatenate([q[:, o:o + MLA_NOPE], _rope_pe(q[:, o + MLA_NOPE:o + MLA_QK], c, s1, s2)], axis=1)
            qh = qh * Q_PRESCALE
            kh = jnp.concatenate([kn[:, h * MLA_NOPE:(h + 1) * MLA_NOPE], kpe], axis=1)
            q_ref[:, o:o + MLA_QK] = _bf(qh)
            k_ref[:, o:o + MLA_QK] = _bf(kh)
            qt_ref[h] = _bf(qh.T)
            kt_ref[h] = _bf(kh.T)
            vt_ref[h] = jnp.concatenate([_bf(vv[:, h * MLA_DV:(h + 1) * MLA_DV].T), ones], axis=0)

    lat_w = Q_LORA + KV_LORA
    return _rowcall("proj_mla", body, T, tm, [h1, *tabs], [w_c, w_kpe, w_g, w_uq, w_uk, w_uv, qn_g, kvn_g],
                    [(lat_w, F32), (2 * D, BF16), (H * MLA_QK, BF16), (H * MLA_QK, BF16), (H * MLA_DV, BF16),
                     (lat_w, BF16)],
                    tiled_outs=[_transposed_blocks(T, tm, MLA_QK, BF16), _transposed_blocks(T, tm, MLA_QK, BF16),
                                _transposed_blocks(T, tm, MLA_DV + V_ONES, BF16)])


def _ret_block(T):
    return min(256, T)


RET_HEADS_PER_STEP = 8


def _ret_dmat(lg, bt):
    n = lax.broadcasted_iota(jnp.int32, (bt, bt), 0)
    m = lax.broadcasted_iota(jnp.int32, (bt, bt), 1)
    return jnp.where(_chunk_of(m) <= _chunk_of(n), jnp.exp(lg * jnp.abs(n - m).astype(F32)), 0.0)


def _ret_scan(name, per_head, lgam, ins, outs, rev):
    T = ins[0][0].shape[0]
    bt = _ret_block(T)
    nb = T // bt
    hps = min(RET_HEADS_PER_STEP, RET_HEADS)
    n_in, n_out = len(ins), len(outs)

    def kern(lg_ref, *refs):
        in_refs, out_refs = refs[:n_in], refs[n_in:n_in + n_out]
        state_ref, dmat_ref = refs[n_in + n_out:]

        @pl.when(pl.program_id(1) == 0)
        def _():
            state_ref[...] = jnp.zeros_like(state_ref)
            for hh in range(hps):
                dmat_ref[hh] = _ret_dmat(lg_ref[hh][:, :1], bt)

        pos = lax.broadcasted_iota(jnp.int32, (bt, 1), 0).astype(F32)
        for hh in range(hps):
            lg = lg_ref[hh][:, :1]
            xi, zeta, gb = jnp.exp(lg * (pos + 1.0)), jnp.exp(lg * (bt - 1.0 - pos)), jnp.exp(lg * bt)
            tiles = [r[:, hh * w:(hh + 1) * w] for r, (_, w) in zip(in_refs, ins)]
            res = per_head(dmat_ref[hh], xi, zeta, gb, state_ref.at[hh], *tiles)
            for o_ref, (w, _), val in zip(out_refs, outs, res):
                o_ref[:, hh * w:(hh + 1) * w] = val.astype(o_ref.dtype)

    def blk(w):
        if rev:
            return pl.BlockSpec((bt, hps * w), lambda g, b: (nb - 1 - b, g))
        return pl.BlockSpec((bt, hps * w), lambda g, b: (b, g))

    return pl.pallas_call(
        kern, grid=(RET_HEADS // hps, nb),
        in_specs=[pl.BlockSpec((hps, 1, LANES), lambda g, b: (g, 0, 0))] + [blk(w) for _, w in ins],
        out_specs=[blk(w) for w, _ in outs],
        out_shape=[jax.ShapeDtypeStruct((T, RET_HEADS * w), dt) for w, dt in outs],
        scratch_shapes=[pltpu.VMEM((hps, RET_DK, RET_DV), F32), pltpu.VMEM((hps, bt, bt), F32)], name=name,
        compiler_params=_params(("arbitrary", "arbitrary")))(lgam, *[a for a, _ in ins])


def _ret_fwd(rq, rk, rv, lgam):
    def per_head(dmat, xi, zeta, gb, s_ref, q, k, v):
        sc = _dot_nt(q, k) * dmat
        y = _dot(_bf(sc), v) + _dot(q, _bf(s_ref[...])) * xi
        s_ref[...] = s_ref[...] * gb + _dot_tn(_bf(k.astype(F32) * zeta), v)
        return (y,)

    return _ret_scan("ret_fwd", per_head, lgam, [(rq, RET_DK), (rk, RET_DK), (rv, RET_DV)], [(RET_DV, BF16)], False)[0]


def _ret_bwd_q(rq, rk, rv, dy, lgam):
    def per_head(dmat, xi, zeta, gb, s_ref, k, v, dy):
        dp = _dot_nt(dy, v) * dmat
        dq = _dot(_bf(dp), k) + _dot_nt(dy, _bf(s_ref[...])) * xi
        s_ref[...] = s_ref[...] * gb + _dot_tn(_bf(k.astype(F32) * zeta), v)
        return (dq,)

    return _ret_scan("ret_bwd_q", per_head, lgam, [(rk, RET_DK), (rv, RET_DV), (dy, RET_DV)], [(RET_DK, F32)], False)[0]


def _ret_bwd_kv(rq, rk, rv, dy, lgam):
    def per_head(dmat, xi, zeta, gb, g_ref, q, k, v, dy):
        gs = _bf(g_ref[...])
        p = _dot_nt(q, k) * dmat
        dp = _dot_nt(dy, v) * dmat
        dv = _dot_tn(_bf(p), dy) + _dot(k, gs) * zeta
        dk = _dot_tn(_bf(dp), q) + _dot_nt(v, gs) * zeta
        g_ref[...] = g_ref[...] * gb + _dot_tn(_bf(q.astype(F32) * xi), dy)
        return dk, dv

    return _ret_scan("ret_bwd_kv", per_head, lgam, [(rq, RET_DK), (rk, RET_DK), (rv, RET_DV), (dy, RET_DV)],
                     [(RET_DK, F32), (RET_DV, BF16)], True)


def _attn_mask_t(tb):
    key = lax.broadcasted_iota(jnp.int32, (tb, tb), 0)
    qry = lax.broadcasted_iota(jnp.int32, (tb, tb), 1)
    return _chunk_of(key) <= _chunk_of(qry)


MASKED = -1e30
SUBLANES = 8


def _head_blocks(nb, w, tb):
    return pl.BlockSpec((nb, None, w, tb), lambda h, i: (0, h, 0, 0))


def _one_block(w, tb):
    return pl.BlockSpec((None, None, w, tb), lambda h, i: (i, h, 0, 0))


def _attn_fwd(k, qt, vt, exchange=()):
    T = k.shape[0]
    tb = _attn_block(T)
    nb = T // tb

    n_ex = len(exchange)

    def kern(qt_ref, k_ref, vt_ref, *refs):
        ex_in, (o_ref, lser_ref), ex_out = refs[:n_ex], refs[n_ex:n_ex + 2], refs[n_ex + 2:2 * n_ex + 2]
        m_ref, acc_ref, sa_ref, sb_ref = refs[2 * n_ex + 2:2 * n_ex + 6]
        sems = refs[2 * n_ex + 6:]
        qb = pl.program_id(1)
        first = jnp.logical_and(pl.program_id(0) == 0, qb == 0)
        last = jnp.logical_and(pl.program_id(0) == MLA_HEADS - 1, qb == nb - 1)
        if n_ex:
            @pl.when(first)
            def _():
                for send, _ in _chip_copies(ex_in, ex_out, *sems, True):
                    send.start()

        qt = qt_ref[...]
        m_ref[...] = jnp.full_like(m_ref, MASKED)
        acc_ref[...] = jnp.zeros_like(acc_ref)

        def scores(kb):
            rows = pl.ds(pl.multiple_of(kb * tb, tb), tb)
            return _dot(k_ref[rows, :], qt)

        def update(s, kb):
            m_old = m_ref[...]
            m_new = jnp.maximum(m_old, jnp.max(s, axis=0, keepdims=True))
            p = jnp.exp2(s - m_new)
            acc_ref[...] = acc_ref[...] * jnp.exp2(m_old - m_new) + _dot(vt_ref[kb], _bf(p))
            m_ref[...] = m_new

        def masked(s):
            return jnp.where(_attn_mask_t(tb), s, MASKED)

        sa_ref[...] = scores(0)

        def pair_body(j, carry):
            sb_ref[...] = scores(2 * j + 1)
            update(sa_ref[...], 2 * j)
            sa_ref[...] = scores(2 * j + 2)
            update(sb_ref[...], 2 * j + 1)
            return carry

        lax.fori_loop(0, qb // 2, pair_body, 0)

        @pl.when(qb % 2 == 0)
        def _():
            update(masked(sa_ref[...]), qb)

        @pl.when(qb % 2 == 1)
        def _():
            sb_ref[...] = masked(scores(qb))
            update(sa_ref[...], qb - 1)
            update(sb_ref[...], qb)

        l = acc_ref[MLA_DV:MLA_DV + 1, :]
        o_ref[...] = _bf((acc_ref[:MLA_DV, :] / l).T)
        lser_ref[...] = jnp.broadcast_to(m_ref[...] + jnp.log2(l), (SUBLANES, tb))
        if n_ex:
            @pl.when(last)
            def _():
                _wait_copies(_chip_copies(ex_in, ex_out, *sems, True))

    return pl.pallas_call(
        kern, grid=(MLA_HEADS, nb),
        in_specs=[_one_block(MLA_QK, tb), pl.BlockSpec((T, MLA_QK), lambda h, i: (0, h)),
                  _head_blocks(nb, MLA_DV + V_ONES, tb)] + [HBM_SPEC] * n_ex,
        out_specs=[pl.BlockSpec((tb, MLA_DV), lambda h, i: (i, h)), _one_block(SUBLANES, tb)] + [HBM_SPEC] * n_ex,
        out_shape=[jax.ShapeDtypeStruct((T, MLA_HEADS * MLA_DV), BF16),
                   jax.ShapeDtypeStruct((nb, MLA_HEADS, SUBLANES, tb), F32)] + _exchange_shapes(exchange),
        scratch_shapes=[pltpu.VMEM((1, tb), F32), pltpu.VMEM((MLA_DV + V_ONES, tb), F32),
                        pltpu.VMEM((tb, tb), F32), pltpu.VMEM((tb, tb), F32)]
        + (_dma_sems(n_ex * N_PEER_CHIPS) if n_ex else []),
        name="attn_fwd", compiler_params=_params(("arbitrary", "arbitrary")))(qt, k, vt, *exchange)


def _attn_bwd(q, k, v, do, qt, kt, dot_, lse_rows, delta_rows):
    T = q.shape[0]
    tb = _attn_block(T)
    nb = T // tb

    def kern(q_ref, k_ref, v_ref, do_ref, qt_ref, kt_ref, dot_ref, lse_ref, dl_ref, dk_ref, dv_ref, dqt_ref, dv_acc,
             sa_ref, pa_ref, sb_ref, pb_ref):
        kb = pl.program_id(1)
        kv, vv, ktv = k_ref[...], v_ref[...], kt_ref[...]
        dk_ref[...] = jnp.zeros_like(dk_ref)
        dv_acc[...] = jnp.zeros_like(dv_acc)

        @pl.when(kb == 0)
        def _():
            dqt_ref[...] = jnp.zeros_like(dqt_ref)

        def products(qb, s_ref, dp_ref, diagonal=False):
            s = _dot(kv, qt_ref[qb])
            s_ref[...] = jnp.where(_attn_mask_t(tb), s, MASKED) if diagonal else s
            dp_ref[...] = _dot(vv, dot_ref[qb])

        def consume(qb, s_ref, dp_ref):
            rows = pl.ds(pl.multiple_of(qb * tb, tb), tb)
            p = jnp.exp2(s_ref[...] - lse_ref[qb][:1, :])
            dv_acc[...] += _dot(_bf(p), do_ref[rows, :])
            ds = _bf(p * (dp_ref[...] - dl_ref[qb][:1, :]))
            dk_ref[...] += _dot(ds, q_ref[rows, :])
            dqt_ref[qb] += _dot(ktv, ds)

        n_full = nb - 1 - kb
        products(kb, sa_ref, pa_ref, diagonal=True)

        def pair_body(j, carry):
            q1 = kb + 1 + 2 * j
            products(q1, sb_ref, pb_ref)
            consume(q1 - 1, sa_ref, pa_ref)
            products(q1 + 1, sa_ref, pa_ref)
            consume(q1, sb_ref, pb_ref)
            return carry

        lax.fori_loop(0, n_full // 2, pair_body, 0)

        @pl.when(n_full % 2 == 0)
        def _():
            consume(nb - 1, sa_ref, pa_ref)

        @pl.when(n_full % 2 == 1)
        def _():
            products(nb - 1, sb_ref, pb_ref)
            consume(nb - 2, sa_ref, pa_ref)
            consume(nb - 1, sb_ref, pb_ref)

        dk_ref[...] = dk_ref[...] * (ATTN_SCALE / Q_PRESCALE)
        dv_ref[...] = _bf(dv_acc[...])

    def blk(w):
        return pl.BlockSpec((tb, w), lambda h, i: (i, h))

    def full(w):
        return pl.BlockSpec((T, w), lambda h, i: (0, h))

    return pl.pallas_call(
        kern, grid=(MLA_HEADS, nb),
        in_specs=[full(MLA_QK), blk(MLA_QK), blk(MLA_DV), full(MLA_DV), _head_blocks(nb, MLA_QK, tb),
                  _one_block(MLA_QK, tb), _head_blocks(nb, MLA_DV, tb), _head_blocks(nb, SUBLANES, tb),
                  _head_blocks(nb, SUBLANES, tb)],
        out_specs=[blk(MLA_QK), blk(MLA_DV), _head_blocks(nb, MLA_QK, tb)],
        out_shape=[jax.ShapeDtypeStruct((T, MLA_HEADS * MLA_QK), F32),
                   jax.ShapeDtypeStruct((T, MLA_HEADS * MLA_DV), BF16),
                   jax.ShapeDtypeStruct((nb, MLA_HEADS, MLA_QK, tb), F32)],
        scratch_shapes=[pltpu.VMEM((tb, MLA_DV), F32)] + [pltpu.VMEM((tb, tb), F32)] * 4,
        name="attn_bwd", compiler_params=_params(("arbitrary", "arbitrary")))(
            q, k, v, do, qt, kt, dot_, lse_rows, delta_rows)


def _group_norm(y):
    yc = y - _mean(y)
    rstd = lax.rsqrt(_mean(yc * yc) + EPS)
    return yc * rstd, rstd


def _mix_fwd(y, rg, o, gates, h1, gn_g, w_ret_o, w_mla_o, w_out, ln_g, ln_b, tm):
    T, D = h1.shape

    def body(i, y_ref, rg_ref, o_ref, gt_ref, h_ref, gn_ref, wr_ref, wm_ref, wo_ref, g_ref, b_ref,
             h2_ref, z_ref, yret_ref, ymla_ref, yr_ref, mix_ref):
        for h in range(RET_HEADS):
            sl = slice(h * RET_DV, (h + 1) * RET_DV)
            yn, _ = _group_norm(y_ref[:, sl].astype(F32))
            r = rg_ref[:, sl].astype(F32)
            yr_ref[:, sl] = _bf(r * _sigmoid(r) * (yn * gn_ref[:, sl]))
        yret = _dot(yr_ref[...], wr_ref[...])
        ymla = _dot(_bf(o_ref[...]), wm_ref[...])
        yret_ref[...] = _bf(yret)
        ymla_ref[...] = _bf(ymla)
        mix = _bf(_sigmoid(gt_ref[:, :D].astype(F32)) * yret + _sigmoid(gt_ref[:, D:].astype(F32)) * ymla)
        mix_ref[...] = mix
        z = ALPHA * h_ref[...] + _dot(mix, wo_ref[...])
        xhat, _ = _ln_stats(z)
        z_ref[...] = z
        h2_ref[...] = xhat * g_ref[...] + b_ref[...]

    return _rowcall("mix_fwd", body, T, tm, [y, rg, o, gates, h1], [gn_g, w_ret_o, w_mla_o, w_out, ln_g, ln_b],
                    [(D, F32), (D, F32), (D, BF16), (D, BF16), (RET_HEADS * RET_DV, BF16), (D, BF16)])


def _mix_bwd(dh2, z1, gates, yret, ymla, y, rg, o, gn_g, w_ret_o, w_mla_o, w_out, ln_g, tm, exchange=None):
    T, D = dh2.shape
    rv = RET_HEADS * RET_DV

    def body(i, dh_ref, z_ref, gt_ref, yret_ref, ymla_ref, y_ref, rg_ref, o_ref, gn_ref, wr_ref, wm_ref, wo_ref, g_ref,
             dz_ref, dgt_ref, drg_ref, dy_ref, do_ref, dyret_ref, dymla_ref, dg_ref, db_ref, dgn_ref, dot_ref,
             dl_ref):
        xhat, rstd = _ln_stats(z_ref[...])
        dz, dg, db = _ln_bwd(dh_ref[...], xhat, rstd, g_ref[...])
        _acc(i, dg_ref, dg)
        _acc(i, db_ref, db)
        dz_ref[...] = dz
        dmix = _dot_nt(_bf(dz), wo_ref[...])
        sr = _sigmoid(gt_ref[:, :D].astype(F32))
        sm = _sigmoid(gt_ref[:, D:].astype(F32))
        dgt_ref[:, :D] = _bf(dmix * yret_ref[...].astype(F32) * sr * (1.0 - sr))
        dgt_ref[:, D:] = _bf(dmix * ymla_ref[...].astype(F32) * sm * (1.0 - sm))
        dyret = _bf(dmix * sr)
        dymla = _bf(dmix * sm)
        dyret_ref[...] = dyret
        dymla_ref[...] = dymla
        dov = _dot_nt(dymla, wm_ref[...])
        do_ref[...] = _bf(dov)
        for h in range(MLA_HEADS):
            sl = slice(h * MLA_DV, (h + 1) * MLA_DV)
            dot_ref[h] = _bf(dov[:, sl].T)
            delta = jnp.sum(dov[:, sl] * o_ref[:, sl].astype(F32), axis=-1, keepdims=True)
            dl_ref[h] = jnp.broadcast_to(delta, (tm, LANES)).T[:SUBLANES, :]
        dyr = _dot_nt(dyret, wr_ref[...])
        dgn = []
        for h in range(RET_HEADS):
            sl = slice(h * RET_DV, (h + 1) * RET_DV)
            yn, grstd = _group_norm(y_ref[:, sl].astype(F32))
            r = rg_ref[:, sl].astype(F32)
            sig = _sigmoid(r)
            d = dyr[:, sl]
            drg_ref[:, sl] = _bf(d * (yn * gn_ref[:, sl]) * sig * (1.0 + r * (1.0 - sig)))
            dt = d * (r * sig)
            dgn.append(jnp.sum(dt * yn, axis=0, keepdims=True))
            dyn = dt * gn_ref[:, sl]
            dy_ref[:, sl] = _bf(grstd * (dyn - _mean(dyn) - yn * _mean(dyn * yn)))
        _acc(i, dgn_ref, jnp.concatenate(dgn, axis=1))

    return _rowcall("mix_bwd", body, T, tm, [dh2, z1, gates, yret, ymla, y, rg, o],
                    [gn_g, w_ret_o, w_mla_o, w_out, ln_g],
                    [(D, F32), (2 * D, BF16), (rv, BF16), (rv, BF16), (MLA_HEADS * MLA_DV, BF16), (D, BF16), (D, BF16)],
                    [((1, D), F32), ((1, D), F32), ((1, rv), F32)],
                    tiled_outs=[_transposed_blocks(T, tm, MLA_DV, BF16), _transposed_blocks(T, tm, SUBLANES, F32)],
                    exchange=exchange)


def _proj_mla_bwd(dqt, dk, dv, lat, tabs, w_uq, w_uk, w_uv, qn_g, kvn_g, tm):
    T = dk.shape[0]
    H = MLA_HEADS
    lat_w = Q_LORA + KV_LORA

    def body(i, dk_ref, dv_ref, lat_ref, c_ref, s1_ref, s2_ref, dqt_ref, wuq_ref, wuk_ref, wuv_ref, qg_ref, kg_ref,
             dlat_ref, dkpe_ref, dqb_ref, dkn_ref, dqg_ref, dkg_ref):
        c, s1, s2 = c_ref[...], s1_ref[...], s2_ref[...]
        dkpe = jnp.zeros((tm, LANES), F32)
        for h in range(H):
            o = h * MLA_QK
            dqh = dqt_ref[h].T * ATTN_SCALE
            dqb_ref[:, o:o + MLA_NOPE] = _bf(dqh[:, :MLA_NOPE])
            dqb_ref[:, o + MLA_NOPE:o + MLA_QK] = _bf(_rope_pe_bwd(dqh[:, MLA_NOPE:], c, s1, s2))
            dkn_ref[:, h * MLA_NOPE:(h + 1) * MLA_NOPE] = _bf(dk_ref[:, o:o + MLA_NOPE])
            dkpe += dk_ref[:, o + MLA_NOPE:o + MLA_QK]
        dkn_ref[:, H * MLA_NOPE:] = dv_ref[...]
        dkpe_ref[...] = _bf(_rope_pe_bwd(dkpe, c, s1, s2))
        dcqn = _dot_nt(dqb_ref[...], wuq_ref[...])
        dckn = _dot_nt(dkn_ref[:, :H * MLA_NOPE], wuk_ref[...]) + _dot_nt(dv_ref[...], wuv_ref[...])
        for dn, x, g_ref, dg_ref, sl in ((dcqn, lat_ref[:, :Q_LORA], qg_ref, dqg_ref, slice(0, Q_LORA)),
                                         (dckn, lat_ref[:, Q_LORA:], kg_ref, dkg_ref, slice(Q_LORA, lat_w))):
            xn, r = _rms(x, None)
            _acc(i, dg_ref, jnp.sum(dn * xn, axis=0, keepdims=True))
            dxn = dn * g_ref[...]
            dlat_ref[:, sl] = _bf(r * (dxn - xn * _mean(dxn * xn)))

    dqt_shape, dqt_spec = _transposed_blocks(T, tm, MLA_QK, F32)
    assert dqt.shape == dqt_shape.shape
    return _rowcall("proj_mla_bwd", body, T, tm, [dk, dv, lat, *tabs], [w_uq, w_uk, w_uv, qn_g, kvn_g],
                    [(lat_w, BF16), (LANES, BF16), (H * MLA_QK, BF16), (H * (MLA_NOPE + MLA_DV), BF16)],
                    [((1, Q_LORA), F32), ((1, KV_LORA), F32)], tiled_ins=[(dqt, dqt_spec)])


def _proj_bwd(drq, drk, drv, drg, dz1, dlat, dkpe, dgates, cos_r, sin_r, w_r, w_c, w_kpe, w_g, tm):
    T, D = dz1.shape
    qk = RET_HEADS * RET_DK
    rv = RET_HEADS * RET_DV
    o_lat = 2 * qk + 2 * rv
    o_kpe = o_lat + dlat.shape[1]
    o_gate = o_kpe + LANES
    o_end = o_gate + dgates.shape[1]
    width = -(-o_end // WG_TILE_N) * WG_TILE_N

    def body(i, drq_ref, drk_ref, drv_ref, drg_ref, dz_ref, dlat_ref, dkpe_ref, dgt_ref, cos_ref, sin_ref,
             wr_ref, wc_ref, wk_ref, wg_ref, dh_ref, dpr_ref):
        cos, sin = cos_ref[...], sin_ref[...]
        for src, off, scale in ((drq_ref, 0, 1.0), (drk_ref, qk, RET_DK ** -0.5)):
            for h in range(RET_HEADS):
                d = src[:, h * RET_DK:(h + 1) * RET_DK]
                dpr_ref[:, off + h * RET_DK:off + (h + 1) * RET_DK] = _bf(
                    (d * cos + _roll(d * sin, RET_DK // 2)) * scale)
        dpr_ref[:, 2 * qk:2 * qk + rv] = drv_ref[...]
        dpr_ref[:, 2 * qk + rv:o_lat] = drg_ref[...]
        dpr_ref[:, o_lat:o_kpe] = dlat_ref[...]
        dpr_ref[:, o_kpe:o_gate] = dkpe_ref[...]
        dpr_ref[:, o_gate:o_end] = dgt_ref[...]
        if width > o_end:
            dpr_ref[:, o_end:] = jnp.zeros((tm, width - o_end), BF16)
        dh_ref[...] = (ALPHA * dz_ref[...] + _dot_nt(dpr_ref[:, :o_lat], wr_ref[...])
                       + _dot_nt(dlat_ref[...], wc_ref[...]) + _dot_nt(dkpe_ref[...], wk_ref[...])
                       + _dot_nt(dgt_ref[...], wg_ref[...]))

    return _rowcall("proj_bwd", body, T, tm, [drq, drk, drv, drg, dz1, dlat, dkpe, dgates, cos_r, sin_r],
                    [w_r, w_c, w_kpe, w_g], [(D, F32), (width, BF16)])


def _ple_loss(h3, p, target, w_gate, w_proj, ln_g, ln_b, tm):
    T, D = h3.shape

    def body(i, h_ref, p_ref, t_ref, wg_ref, wp_ref, g_ref, b_ref, dh_ref, dgp_ref, dpp_ref, loss_ref, dg_ref, db_ref):
        hv = h_ref[...]
        sg = _sigmoid(_dot(_bf(hv), wg_ref[...]))
        pp = _dot(_bf(p_ref[...]), wp_ref[...])
        xhat, rstd = _ln_stats(ALPHA * hv + sg * pp)
        err = xhat * g_ref[...] + b_ref[...] - t_ref[...]
        row_loss = 0.5 * _mean(err * err)
        _acc(i, loss_ref, jnp.broadcast_to(jnp.sum(row_loss, axis=0, keepdims=True), (1, LANES)))
        dz, dg, db = _ln_bwd(err * (1.0 / D), xhat, rstd, g_ref[...])
        _acc(i, dg_ref, dg)
        _acc(i, db_ref, db)
        dgp = _bf(dz * pp * sg * (1.0 - sg))
        dgp_ref[...] = dgp
        dpp_ref[...] = _bf(dz * sg)
        dh_ref[...] = ALPHA * dz + _dot_nt(dgp, wg_ref[...])

    return _rowcall("ple_loss", body, T, tm, [h3, p, target], [w_gate, w_proj, ln_g, ln_b],
                    [(D, F32), (D, BF16), (D, BF16)], [((1, LANES), F32), ((1, D), F32), ((1, D), F32)])


def _ewise(name, fn, ins, n_out, out_dtype=F32):
    r, c = ins[0].shape
    tr = _tile(r, max(8, (1 << 19) // c // 8 * 8), 8)

    def kern(*refs):
        outs = fn(*[x[...] for x in refs[:len(ins)]])
        for o_ref, o in zip(refs[len(ins):], outs):
            o_ref[...] = o.astype(out_dtype)

    spec = pl.BlockSpec((tr, c), lambda i: (i, 0))
    return pl.pallas_call(kern, grid=(r // tr,), in_specs=[spec] * len(ins), out_specs=[spec] * n_out,
                          out_shape=[jax.ShapeDtypeStruct((r, c), out_dtype)] * n_out, name=name,
                          compiler_params=_params(("arbitrary",)))(*ins)


def _adamw_math(w, g, m, v):
    m = ADAM_B1 * m + (1.0 - ADAM_B1) * g
    v = ADAM_B2 * v + (1.0 - ADAM_B2) * (g * g)
    m_hat = m / (1.0 - ADAM_B1 ** ADAM_STEP)
    v_hat = v / (1.0 - ADAM_B2 ** ADAM_STEP)
    return -ADAM_LR * (m_hat / (jnp.sqrt(v_hat) + ADAM_EPS) + ADAM_WD * w), m, v


def _adamw(name, w, g, m, v):
    shape = w.shape
    c = shape[-1]
    flat = [t.reshape(-1, c) for t in (w, g, m, v)]
    return [t.reshape(shape) for t in _ewise(name, _adamw_math, flat, 3)]


def _place():
    return lax.axis_index("x"), lax.axis_index("y"), lax.axis_index("c")


def _dma_sems(n):
    return [pltpu.SemaphoreType.DMA((n,)), pltpu.SemaphoreType.DMA((n,))]


N_PEER_CHIPS = N_CHIPS - 1


def _chips_exchange(name, srcs, broadcast):
    n = len(srcs)

    def kern(*refs):
        cps = _chip_copies(refs[:n], refs[n:2 * n], refs[2 * n], refs[2 * n + 1], broadcast)
        for send, _ in cps:
            send.start()
        _wait_copies(cps)

    return pl.pallas_call(
        kern, out_shape=_exchange_shapes(srcs), in_specs=[HBM_SPEC] * n, out_specs=[HBM_SPEC] * n,
        scratch_shapes=_dma_sems(n * N_PEER_CHIPS), name=name)(*srcs)


def _exchange_shapes(srcs):
    return [jax.ShapeDtypeStruct((N_CHIPS,) + s.shape[1:], s.dtype) for s in srcs]


def _chip_copies(src_refs, out_refs, send_sems, recv_sems, broadcast):
    x, y, c = _place()
    me = 2 * x + y
    peers = [(1 - x, y), (x, 1 - y), (1 - x, 1 - y)]
    cps = []
    for j, (px, py) in enumerate(peers):
        for a, (src_ref, out_ref) in enumerate(zip(src_refs, out_refs)):
            piece = src_ref.at[c] if broadcast else src_ref.at[2 * px + py]

            def copy(slot):
                return pltpu.make_async_remote_copy(
                    src_ref=piece, dst_ref=out_ref.at[slot], send_sem=send_sems.at[a * N_PEER_CHIPS + j],
                    recv_sem=recv_sems.at[a * N_PEER_CHIPS + j], device_id=(px, py, c), device_id_type=MESH)

            cps.append((copy(me), copy(2 * px + py)))
    return cps


def _wait_copies(cps):
    for _, landing in cps:
        landing.wait_recv()
    for send, _ in cps:
        send.wait_send()


def _sibling_swap(name, srcs, mode):
    n = len(srcs)
    per = N_PEER_CHIPS if mode == "others" else 1

    def kern(*refs):
        src_refs, out_refs = refs[:n], refs[n:2 * n]
        send_sems, recv_sems = refs[2 * n:]
        x, y, c = _place()
        slots = [2 * (1 - x) + y, 2 * x + 1 - y, 2 * (1 - x) + 1 - y]
        cps = []
        for a in range(n):
            if mode == "others":
                pieces = [(src_refs[a].at[k], out_refs[a].at[k]) for k in slots]
            else:
                pieces = [(src_refs[a].at[:, 1 - c] if mode == "halves" else src_refs[a], out_refs[a])]
            for j, (src, dst) in enumerate(pieces):
                cps.append(pltpu.make_async_remote_copy(
                    src_ref=src, dst_ref=dst, send_sem=send_sems.at[a * per + j], recv_sem=recv_sems.at[a * per + j],
                    device_id=(x, y, 1 - c), device_id_type=MESH))
        for cp in cps:
            cp.start()
        for cp in cps:
            cp.wait_recv()
        for cp in cps:
            cp.wait_send()

    def out_shape(s):
        return jax.ShapeDtypeStruct((s.shape[0],) + s.shape[2:] if mode == "halves" else s.shape, s.dtype)

    return pl.pallas_call(
        kern, out_shape=[out_shape(s) for s in srcs], in_specs=[HBM_SPEC] * n, out_specs=[HBM_SPEC] * n,
        scratch_shapes=_dma_sems(n * per), name=name)(*srcs)


def _all_devices(name, src, reduce):
    r, c = src.shape
    n_dev = 2 * N_CHIPS

    def kern(src_ref, out_ref, *scratch):
        if reduce:
            gat_ref, send_sems, recv_sems = scratch
        else:
            gat_ref = out_ref
            send_sems, recv_sems = scratch
        x, y, cc = _place()
        me = 4 * x + 2 * y + cc
        gat_ref[me] = src_ref[...]
        peers = []
        for j in range(1, n_dev):
            px = 1 - x if j & 4 else x
            py = 1 - y if j & 2 else y
            pc = 1 - cc if j & 1 else cc
            peers.append((px, py, pc))

        def copy(j, peer, slot):
            return pltpu.make_async_remote_copy(
                src_ref=src_ref, dst_ref=gat_ref.at[slot], send_sem=send_sems.at[j], recv_sem=recv_sems.at[j],
                device_id=peer, device_id_type=MESH)

        sends = [copy(j, peer, me) for j, peer in enumerate(peers)]
        for cp in sends:
            cp.start()
        for j, (px, py, pc) in enumerate(peers):
            copy(j, (px, py, pc), 4 * px + 2 * py + pc).wait_recv()
        for cp in sends:
            cp.wait_send()
        if reduce:
            total = gat_ref[0]
            for d in range(1, n_dev):
                total = total + gat_ref[d]
            out_ref[...] = total

    out_shape = jax.ShapeDtypeStruct((r, c) if reduce else (n_dev, r, c), src.dtype)
    scratch = ([pltpu.VMEM((n_dev, r, c), src.dtype)] if reduce else []) + _dma_sems(n_dev - 1)
    return pl.pallas_call(kern, out_shape=out_shape, in_specs=[VMEM_SPEC], out_specs=VMEM_SPEC,
                          scratch_shapes=scratch, name=name)(src)


def _halves(t, axis):
    return t.reshape(t.shape[:axis] + (2, t.shape[axis] // 2) + t.shape[axis + 1:])


def _by_core(mine, theirs, axis):
    c = lax.axis_index("c")
    both = jnp.where(c == 0, jnp.stack([mine, theirs], axis), jnp.stack([theirs, mine], axis))
    return both.reshape(both.shape[:axis] + (2 * both.shape[axis + 1],) + both.shape[axis + 2:])


def _with_own(own, others):
    me = 2 * lax.axis_index("x") + lax.axis_index("y")
    is_me = (jnp.arange(N_CHIPS, dtype=jnp.int32) == me)[:, None, None]
    return jnp.where(is_me, own[None], others)


def _join_shards(name, shards):
    _, r, c = shards.shape
    if name in COL_SHARDED:
        return shards.transpose(1, 0, 2).reshape(r, N_CHIPS * c)
    return shards.reshape(N_CHIPS * r, c)


def _split_shards(name, full):
    if full.ndim == 3:
        return full
    r, c = full.shape
    if name in COL_SHARDED:
        return jnp.stack([full[:, k * (c // N_CHIPS):(k + 1) * (c // N_CHIPS)] for k in range(N_CHIPS)])
    return full.reshape(N_CHIPS, r // N_CHIPS, c)


def _rope_tables(positions):
    pos = positions.reshape(-1).astype(F32)[:, None]
    half = RET_DK // 2
    ang = pos * (ROPE_BASE ** (-jnp.arange(half, dtype=F32) / half))
    cos_r = jnp.concatenate([jnp.cos(ang)] * 2, axis=1)
    sin_r = jnp.concatenate([-jnp.sin(ang), jnp.sin(ang)], axis=1)
    half = MLA_ROPE // 2
    ang = pos * (ROPE_BASE ** (-jnp.arange(half, dtype=F32) / half))
    zeros = jnp.zeros_like(ang)
    rest = LANES - MLA_ROPE
    c = jnp.concatenate([jnp.cos(ang)] * 2 + [jnp.ones((ang.shape[0], rest), F32)], axis=1)
    s1 = jnp.concatenate([-jnp.sin(ang), zeros, jnp.zeros((ang.shape[0], rest), F32)], axis=1)
    s2 = jnp.concatenate([zeros, jnp.sin(ang), jnp.zeros((ang.shape[0], rest), F32)], axis=1)
    return cos_r, sin_r, (c, s1, s2)


GATHER_GROUPS = (("ffn1_w_in", "ffn1_w_out"), ("w_in", "w_uq", "w_ukv"),
                 ("w_ret_o", "w_mla_o", "w_out", "ffn2_w_in", "ffn2_w_out", "ple_w_gate", "ple_w_proj"))
REDUCE_GROUPS = (("ple_w_gate", "ple_w_proj", "ffn2_w_in", "ffn2_w_out"),
                 ("w_out", "w_ret_o", "w_mla_o", "w_uq", "w_ukv", "w_in"), ("ffn1_w_in",), ("ffn1_w_out",))


def _gathered(tag, names, own, mine):
    theirs = _sibling_swap("gather_cores_" + tag, mine, "others")
    out = {}
    for n, m, t in zip(names, mine, theirs):
        full = _with_own(own[n], _by_core(m, t, 1))
        out[n] = full if n in ("ffn1_w_in", "ffn2_w_in") else _join_shards(n, full)
    return out


def _chip_sums(tag, names, grads):
    halves = [_halves(_split_shards(n, grads[n]), 1) for n in names]
    theirs = _sibling_swap("reduce_cores_" + tag, halves, "halves")

    def one(n, g, t):
        k, _, r, c = g.shape
        tr = _tile(r, max(8, (1 << 17) // c // 8 * 8), 8)

        def kern(g_ref, t_ref, o_ref):
            mine = jnp.where(lax.axis_index("c") == 0, g_ref[:, 0], g_ref[:, 1])
            o_ref[...] = _bf(mine.astype(F32) + t_ref[...].astype(F32))

        spec = pl.BlockSpec((k, tr, c), lambda i: (0, i, 0))
        return pl.pallas_call(kern, grid=(r // tr,),
                              in_specs=[pl.BlockSpec((k, 2, tr, c), lambda i: (0, 0, i, 0)), spec], out_specs=spec,
                              out_shape=jax.ShapeDtypeStruct((k, r, c), BF16), name="reduce_cores_add_" + n,
                              compiler_params=_params(("arbitrary",)))(g, t)

    return [one(n, g, t) for n, g, t in zip(names, halves, theirs)]


def _block_totals(names, sums, parts):
    def one(n, s, pt):
        _, r, c = s.shape
        tr = _tile(r, max(8, (1 << 17) // c // 8 * 8), 8)

        def kern(s_ref, p_ref, o_ref):
            me = 2 * lax.axis_index("x") + lax.axis_index("y")
            terms = [jnp.where(k == me, s_ref[k], p_ref[k]).astype(F32) for k in range(N_CHIPS)]
            o_ref[...] = ((terms[0] + terms[1]) + terms[2]) + terms[3]

        spec = pl.BlockSpec((N_CHIPS, tr, c), lambda i: (0, i, 0))
        return pl.pallas_call(kern, grid=(r // tr,), in_specs=[spec, spec],
                              out_specs=pl.BlockSpec((tr, c), lambda i: (i, 0)),
                              out_shape=jax.ShapeDtypeStruct((r, c), F32), name="reduce_chips_add_" + n,
                              compiler_params=_params(("arbitrary",)))(s, pt)

    return [one(n, s, pt) for n, s, pt in zip(names, sums, parts)]


def _local_step(x, p, positions, target, shards, ln_g, ln_b, gn_g, qn_g, kvn_g):
    T, D = x.shape
    tm = min(256, T)
    H = MLA_HEADS
    qk, rv = RET_HEADS * RET_DK, RET_HEADS * RET_DV
    cos_r, sin_r, tabs = _rope_tables(positions)
    lgam = jnp.broadcast_to(jnp.log(1.0 - 2.0 ** (-5.0 - jnp.arange(RET_HEADS, dtype=F32)))[:, None, None],
                            (RET_HEADS, 1, LANES))
    lng = [ln_g[k:k + 1] for k in range(N_LN)]
    lnb = [ln_b[k:k + 1] for k in range(N_LN)]
    own = {n: _bf(shards[n]) for n in BIG_WEIGHTS}
    to_send = [[_halves(own[n], 0) for n in names] for names in GATHER_GROUPS]

    w = _gathered("a", GATHER_GROUPS[0], own, _chips_exchange("gather_chips_a", to_send[0], True))
    h1, z0, a1, *arrived = _ffn_fwd("ffn1_fwd", x, w["ffn1_w_in"], w["ffn1_w_out"], lng[0], lnb[0], 2 * tm,
                                    exchange=(to_send[1], True))
    w.update(_gathered("b", GATHER_GROUPS[1], own, arrived))

    w_in = w["w_in"]
    o_lat, o_kpe, o_gate = 2 * qk + 2 * rv, 2 * qk + 2 * rv + Q_LORA + KV_LORA, 2 * qk + 2 * rv + Q_LORA + KV_LORA + MLA_ROPE
    w_r, w_c = w_in[:, :o_lat], w_in[:, o_lat:o_kpe]
    w_kpe = jnp.pad(w_in[:, o_kpe:o_gate], ((0, 0), (0, LANES - MLA_ROPE)))
    w_g = w_in[:, o_gate:]
    w_uq = jnp.pad(w["w_uq"].reshape(Q_LORA, H, MLA_NOPE + MLA_ROPE),
                   ((0, 0), (0, 0), (0, MLA_QK - MLA_NOPE - MLA_ROPE))).reshape(Q_LORA, H * MLA_QK)
    w_ukv = w["w_ukv"].reshape(KV_LORA, H, MLA_NOPE + MLA_DV)
    w_uk = w_ukv[:, :, :MLA_NOPE].reshape(KV_LORA, H * MLA_NOPE)
    w_uv = w_ukv[:, :, MLA_NOPE:].reshape(KV_LORA, H * MLA_DV)

    rq, rk, rvv, rg = _proj_ret(h1, w_r, cos_r, sin_r, 2 * tm)
    lat, gates, q, k, v, latn, qt, kt, vt = _proj_mla(h1, tabs, w_c, w_kpe, w_g, w_uq, w_uk, w_uv, qn_g, kvn_g, 2 * tm)
    y = _ret_fwd(rq, rk, rvv, lgam)
    o, lse_rows, *arrived = _attn_fwd(k, qt, vt, exchange=to_send[2])
    w.update(_gathered("c", GATHER_GROUPS[2], own, arrived))
    h2, z1, yret, ymla, yr, mix = _mix_fwd(y, rg, o, gates, h1, gn_g, w["w_ret_o"], w["w_mla_o"], w["w_out"],
                                           lng[1], lnb[1], 2 * tm)
    h3, z2, a2 = _ffn_fwd("ffn2_fwd", h2, w["ffn2_w_in"], w["ffn2_w_out"], lng[2], lnb[2], 2 * tm)

    dh3, dgp, dpp, loss, dg3, db3 = _ple_loss(h3, p, target, w["ple_w_gate"], w["ple_w_proj"], lng[3], lnb[3], 2 * tm)
    dh2, da2, s2, df2, dg2, db2 = _ffn_bwd("ffn2_bwd", dh3, z2, a2, w["ffn2_w_in"], w["ffn2_w_out"], lng[2], tm)
    grads = {"ple_w_gate": _mm_tn("wg_ple_gate", h3, dgp), "ple_w_proj": _mm_tn("wg_ple_proj", p, dpp),
             "ffn2_w_in": _mm_tn("wg_ffn2_in", h2, da2, n_split=N_CHIPS), "ffn2_w_out": _mm_tn("wg_ffn2_out", s2, df2)}
    sums1 = _chip_sums("1", REDUCE_GROUPS[0], grads)
    (dz1, dgates, drg, dy, do, dyret, dymla, dg1, db1, dgn, dot_, delta_rows, *parts1) = _mix_bwd(
        dh2, z1, gates, yret, ymla, y, rg, o, gn_g, w["w_ret_o"], w["w_mla_o"], w["w_out"], lng[1], tm,
        exchange=(sums1, False))
    drq = _ret_bwd_q(rq, rk, rvv, dy, lgam)
    drk, drv = _ret_bwd_kv(rq, rk, rvv, dy, lgam)
    dk, dv, dqt = _attn_bwd(q, k, v, do, qt, kt, dot_, lse_rows, delta_rows)
    dlat, dkpe, dqb, dkv, dqg, dkg = _proj_mla_bwd(dqt, dk, dv, lat, tabs, w_uq, w_uk, w_uv, qn_g, kvn_g, 2 * tm)
    dh1, dpr = _proj_bwd(drq, drk, drv, drg, dz1, dlat, dkpe, dgates, cos_r, sin_r, w_r, w_c, w_kpe, w_g, tm)
    g_uq = _mm_tn("wg_uq", latn[:, :Q_LORA], dqb).reshape(Q_LORA, H, MLA_QK)[:, :, :MLA_NOPE + MLA_ROPE]
    g_ukv = _mm_tn("wg_ukv", latn[:, Q_LORA:], dkv)
    g_uk = g_ukv[:, :H * MLA_NOPE].reshape(KV_LORA, H, MLA_NOPE)
    g_uv = g_ukv[:, H * MLA_NOPE:].reshape(KV_LORA, H, MLA_DV)
    g_in = _mm_tn("wg_in", h1, dpr)
    grads.update({
        "w_in": jnp.concatenate([g_in[:, :o_kpe + MLA_ROPE], g_in[:, o_kpe + LANES:o_kpe + LANES + 2 * D]], axis=1),
        "w_ret_o": _mm_tn("wg_ret_o", yr, dyret),
        "w_uq": g_uq.reshape(Q_LORA, H * (MLA_NOPE + MLA_ROPE)),
        "w_ukv": jnp.concatenate([g_uk, g_uv], axis=2).reshape(KV_LORA, H * (MLA_NOPE + MLA_DV)),
        "w_mla_o": _mm_tn("wg_mla_o", o, dymla),
        "w_out": _mm_tn("wg_out", mix, dz1)})
    sums2 = _chip_sums("2", REDUCE_GROUPS[1], grads)
    dx, da1, s1, df1, dg0, db0, *parts2 = _ffn_bwd("ffn1_bwd", dh1, z0, a1, w["ffn1_w_in"], w["ffn1_w_out"], lng[0], tm,
                                                   exchange=(sums2, False))
    grads["ffn1_w_in"] = _mm_tn("wg_ffn1_in", x, da1, n_split=N_CHIPS)
    sums3 = _chip_sums("3", REDUCE_GROUPS[2], grads)
    grads["ffn1_w_out"], *parts3 = _mm_tn("wg_ffn1_out", s1, df1, exchange=sums3)
    sums4 = _chip_sums("4", REDUCE_GROUPS[3], grads)
    parts4 = _chips_exchange("reduce_chips_4", sums4, False)

    names = [n for group in REDUCE_GROUPS for n in group]
    totals = _block_totals(names, sums1 + sums2 + sums3 + sums4,
                           list(parts1) + list(parts2) + list(parts3) + list(parts4))
    others = _sibling_swap("reduce_join", totals, "whole")
    reduced = {n: _by_core(t, o_, 0) for n, t, o_ in zip(names, totals, others)}
    small = {"ln_g": jnp.concatenate([dg0, dg1, dg2, dg3], axis=0), "ln_b": jnp.concatenate([db0, db1, db2, db3], axis=0),
             "ret_gn_g": dgn, "q_norm_g": dqg, "kv_norm_g": dkg}
    return loss[0, 0], dx, reduced, small


def kernel(x, p, positions, ln_g, ln_b, ffn1_w_in, ffn1_w_out, w_in, ret_gn_g, w_ret_o, q_norm_g, kv_norm_g, w_uq, w_ukv, w_mla_o, w_out, ffn2_w_in, ffn2_w_out, ple_w_gate, ple_w_proj, loss_target, m_ln_g, m_ln_b, m_ffn1_w_in, m_ffn1_w_out, m_w_in, m_ret_gn_g, m_w_ret_o, m_q_norm_g, m_kv_norm_g, m_w_uq, m_w_ukv, m_w_mla_o, m_w_out, m_ffn2_w_in, m_ffn2_w_out, m_ple_w_gate, m_ple_w_proj, v_ln_g, v_ln_b, v_ffn1_w_in, v_ffn1_w_out, v_w_in, v_ret_gn_g, v_w_ret_o, v_q_norm_g, v_kv_norm_g, v_w_uq, v_w_ukv, v_w_mla_o, v_w_out, v_ffn2_w_in, v_ffn2_w_out, v_ple_w_gate, v_ple_w_proj):
    names = ("ln_g", "ln_b", "ffn1_w_in", "ffn1_w_out", "w_in", "ret_gn_g", "w_ret_o", "q_norm_g", "kv_norm_g", "w_uq",
             "w_ukv", "w_mla_o", "w_out", "ffn2_w_in", "ffn2_w_out", "ple_w_gate", "ple_w_proj")
    weights = dict(zip(names, (ln_g, ln_b, ffn1_w_in, ffn1_w_out, w_in, ret_gn_g, w_ret_o, q_norm_g, kv_norm_g, w_uq,
                               w_ukv, w_mla_o, w_out, ffn2_w_in, ffn2_w_out, ple_w_gate, ple_w_proj)))
    m_in = dict(zip(names, (m_ln_g, m_ln_b, m_ffn1_w_in, m_ffn1_w_out, m_w_in, m_ret_gn_g, m_w_ret_o, m_q_norm_g,
                            m_kv_norm_g, m_w_uq, m_w_ukv, m_w_mla_o, m_w_out, m_ffn2_w_in, m_ffn2_w_out, m_ple_w_gate,
                            m_ple_w_proj)))
    v_in = dict(zip(names, (v_ln_g, v_ln_b, v_ffn1_w_in, v_ffn1_w_out, v_w_in, v_ret_gn_g, v_w_ret_o, v_q_norm_g,
                            v_kv_norm_g, v_w_uq, v_w_ukv, v_w_mla_o, v_w_out, v_ffn2_w_in, v_ffn2_w_out, v_ple_w_gate,
                            v_ple_w_proj)))
    chip = 2 * lax.axis_index("x") + lax.axis_index("y")
    D = x.shape[-1]
    dq = D // N_CHIPS

    shards = {n: weights[n][0] for n in BIG_WEIGHTS}
    ln_all = _all_devices("gather_ln", jnp.concatenate([ln_g[0], ln_b[0]], axis=0), False)
    ln_full = ln_all[::2].transpose(1, 0, 2).reshape(2 * N_LN, D)
    loss, dx, big, small = _local_step(x[0], p[0, 0], positions, loss_target[0], shards, ln_full[:N_LN],
                                       ln_full[N_LN:], ret_gn_g, q_norm_g, kv_norm_g)

    loss = lax.psum(loss, ("x", "y", "c"))
    small_names = ("ln_g", "ln_b", "ret_gn_g", "q_norm_g", "kv_norm_g")
    flat = jnp.concatenate([small[n].reshape(-1) for n in small_names])
    rows = -(-flat.shape[0] // LANES // 8) * 8
    flat = jnp.pad(flat, (0, rows * LANES - flat.shape[0])).reshape(rows, LANES)
    flat = _all_devices("reduce_small", flat, True).reshape(-1)
    off = 0
    for n in small_names:
        size = small[n].size
        small[n] = flat[off:off + size].reshape(small[n].shape)
        off += size
    g_out = dict(big)
    for n in ("ln_g", "ln_b"):
        g_out[n] = lax.dynamic_slice_in_dim(small[n], chip * dq, dq, axis=1)
    for n in ("ret_gn_g", "q_norm_g", "kv_norm_g"):
        g_out[n] = small[n]

    deltas, new_m, new_v = {}, {}, {}
    for n in names:
        g = g_out[n].reshape(weights[n].shape)
        g_out[n] = g
        deltas[n], new_m[n], new_v[n] = _adamw("adamw_" + n, weights[n], g, m_in[n], v_in[n])
    return (loss, dx[None], *[g_out[n] for n in names], *[deltas[n] for n in names], *[new_m[n] for n in names],
            *[new_v[n] for n in names])
```

```python
import functools

import jax
import jax.numpy as jnp
from jax import lax
from jax.experimental import pallas as pl
from jax.experimental.pallas import tpu as pltpu

CHUNK = 64
RET_HEADS = 8
RET_DK = 128
RET_DV = 256
MLA_HEADS = 8
MLA_NOPE = 128
MLA_ROPE = 64
MLA_DV = 128
MLA_QK = 256
Q_LORA = 256
KV_LORA = 256
ROPE_BASE = 10000.0
EPS = 1e-5
N_LN = 4
ALPHA = 2.0 ** 0.25
ADAM_LR = 0.001
ADAM_B1 = 0.9
ADAM_B2 = 0.999
ADAM_EPS = 1e-08
ADAM_WD = 0.01
ADAM_STEP = 10

LANES = 128
VMEM_LIMIT = 60 << 20
N_CHIPS = 4

F32 = jnp.float32
BF16 = jnp.bfloat16
MESH = pl.DeviceIdType.MESH
HBM_SPEC = pl.BlockSpec(memory_space=pltpu.HBM)
VMEM_SPEC = pl.BlockSpec(memory_space=pltpu.VMEM)

BIG_WEIGHTS = ("ffn1_w_in", "ffn1_w_out", "w_in", "w_ret_o", "w_uq", "w_ukv", "w_mla_o", "w_out",
               "ffn2_w_in", "ffn2_w_out", "ple_w_gate", "ple_w_proj")
COL_SHARDED = ("ffn1_w_in", "w_in", "w_uq", "w_ukv", "ffn2_w_in", "ple_w_proj")


def _dot(a, b):
    return jnp.dot(a, b, preferred_element_type=F32)


def _dot_nt(a, b):
    return lax.dot_general(a, b, (((1,), (1,)), ((), ())), preferred_element_type=F32)


def _dot_tn(a, b):
    return lax.dot_general(a, b, (((0,), (0,)), ((), ())), preferred_element_type=F32)


def _bf(x):
    return x.astype(BF16)


def _sigmoid(x):
    return 0.5 * jnp.tanh(0.5 * x) + 0.5


def _mean(x):
    return jnp.mean(x, axis=-1, keepdims=True)


def _ln_stats(z):
    zc = z - _mean(z)
    rstd = lax.rsqrt(_mean(zc * zc) + EPS)
    return zc * rstd, rstd


def _ln_bwd(dy, xhat, rstd, g):
    dxhat = dy * g
    dz = rstd * (dxhat - _mean(dxhat) - xhat * _mean(dxhat * xhat))
    return dz, jnp.sum(dy * xhat, axis=0, keepdims=True), jnp.sum(dy, axis=0, keepdims=True)


def _roll(x, shift):
    return pltpu.roll(x, shift, 1)


def _chunk_of(idx):
    return jnp.right_shift(idx, CHUNK.bit_length() - 1)


def _tile(n, cap, mult=LANES):
    if n <= cap:
        return n
    for t in range(cap - cap % mult, 0, -mult):
        if n % t == 0:
            return t
    return n


def _zero_map(nd, *_):
    return (0,) * nd


def _params(sem):
    return pltpu.CompilerParams(dimension_semantics=sem, vmem_limit_bytes=VMEM_LIMIT)


def _rowcall(name, body, n_rows, tm, row_ins, full_ins, row_outs, acc_outs=(), tiled_outs=(), tiled_ins=(),
             exchange=None):
    n_steps = n_rows // tm
    ex_srcs, broadcast = exchange if exchange else ((), False)
    n_ex = len(ex_srcs)
    n_in = len(row_ins) + len(tiled_ins) + len(full_ins)
    n_out = len(row_outs) + len(acc_outs) + len(tiled_outs)

    def kern(*refs):
        step = pl.program_id(0)
        ex_in, ex_out = refs[n_in:n_in + n_ex], refs[n_in + n_ex + n_out:n_in + 2 * n_ex + n_out]
        sems = refs[n_in + 2 * n_ex + n_out:]
        if n_ex:
            @pl.when(step == 0)
            def _():
                for send, _ in _chip_copies(ex_in, ex_out, *sems, broadcast):
                    send.start()

        body(step, *refs[:n_in], *refs[n_in + n_ex:n_in + n_ex + n_out])
        if n_ex:
            @pl.when(step == n_steps - 1)
            def _():
                _wait_copies(_chip_copies(ex_in, ex_out, *sems, broadcast))

    in_specs = [pl.BlockSpec((tm, a.shape[1]), lambda i: (i, 0)) for a in row_ins]
    in_specs += [spec for (_, spec) in tiled_ins]
    row_ins = list(row_ins) + [a for (a, _) in tiled_ins]
    in_specs += [pl.BlockSpec(a.shape, functools.partial(_zero_map, a.ndim), pipeline_mode=pl.Buffered(1))
                 for a in full_ins]
    in_specs += [HBM_SPEC] * n_ex
    out_specs = [pl.BlockSpec((tm, w), lambda i: (i, 0)) for (w, _) in row_outs]
    out_specs += [pl.BlockSpec(s, functools.partial(_zero_map, len(s))) for (s, _) in acc_outs]
    out_specs += [spec for (_, spec) in tiled_outs]
    out_specs += [HBM_SPEC] * n_ex
    out_shape = [jax.ShapeDtypeStruct((n_rows, w), dt) for (w, dt) in row_outs]
    out_shape += [jax.ShapeDtypeStruct(s, dt) for (s, dt) in acc_outs]
    out_shape += [shape for (shape, _) in tiled_outs]
    out_shape += _exchange_shapes(ex_srcs)
    return pl.pallas_call(kern, grid=(n_steps,), in_specs=in_specs, out_specs=out_specs, out_shape=out_shape,
                          scratch_shapes=_dma_sems(n_ex * N_PEER_CHIPS) if n_ex else [], name=name,
                          compiler_params=_params(("arbitrary",)))(*row_ins, *full_ins, *ex_srcs)


def _acc(step, ref, val):
    @pl.when(step == 0)
    def _():
        ref[...] = val

    @pl.when(step != 0)
    def _():
        ref[...] += val


def _ffn_fwd(name, x, w_in4, w_out, ln_g, ln_b, tm, exchange=None):
    T, D = x.shape
    fh = w_in4.shape[2]

    def body(i, x_ref, w4_ref, wo_ref, g_ref, b_ref, h_ref, z_ref, a_ref):
        xv = x_ref[...]
        xb = _bf(xv)
        f = jnp.zeros((tm, D), F32)
        for k in range(2):
            gk = _dot(xb, w4_ref[k])
            uk = _dot(xb, w4_ref[2 + k])
            a_ref[:, k * fh:(k + 1) * fh] = _bf(gk)
            a_ref[:, (2 + k) * fh:(3 + k) * fh] = _bf(uk)
            f += _dot(_bf(gk * _sigmoid(gk) * uk), wo_ref[k * fh:(k + 1) * fh, :])
        z = ALPHA * xv + 0.5 * f
        xhat, _ = _ln_stats(z)
        z_ref[...] = z
        h_ref[...] = xhat * g_ref[...] + b_ref[...]

    return _rowcall(name, body, T, tm, [x], [w_in4, w_out, ln_g, ln_b],
                    [(D, F32), (D, F32), (4 * fh, BF16)], exchange=exchange)


def _ffn_up(name, x, w_in4, tm, exchange=None):
    T, _ = x.shape
    fh = w_in4.shape[2]

    def body(i, x_ref, w4_ref, a_ref):
        xb = _bf(x_ref[...])
        for k in range(4):
            a_ref[:, k * fh:(k + 1) * fh] = _bf(_dot(xb, w4_ref[k]))

    return _rowcall(name, body, T, tm, [x], [w_in4], [(4 * fh, BF16)], exchange=exchange)


def _ffn_down(name, x, a, w_out, ln_g, ln_b, tm, exchange=None):
    T, D = x.shape
    fh = a.shape[1] // 4

    def body(i, x_ref, a_ref, wo_ref, g_ref, b_ref, h_ref, z_ref):
        f = jnp.zeros((tm, D), F32)
        for k in range(2):
            gk = a_ref[:, k * fh:(k + 1) * fh].astype(F32)
            uk = a_ref[:, (2 + k) * fh:(3 + k) * fh].astype(F32)
            f += _dot(_bf(gk * _sigmoid(gk) * uk), wo_ref[k * fh:(k + 1) * fh, :])
        z = ALPHA * x_ref[...] + 0.5 * f
        xhat, _ = _ln_stats(z)
        z_ref[...] = z
        h_ref[...] = xhat * g_ref[...] + b_ref[...]

    return _rowcall(name, body, T, tm, [x, a], [w_out, ln_g, ln_b], [(D, F32), (D, F32)], exchange=exchange)


def _ffn_bwd(name, dh, z, a, w_in4, w_out, ln_g, tm, exchange=None):
    T, D = dh.shape
    fh = w_in4.shape[2]

    def body(i, dh_ref, z_ref, a_ref, w4_ref, wo_ref, g_ref, dx_ref, da_ref, s_ref, df_ref, dg_ref, db_ref):
        xhat, rstd = _ln_stats(z_ref[...])
        dz, dg, db = _ln_bwd(dh_ref[...], xhat, rstd, g_ref[...])
        _acc(i, dg_ref, dg)
        _acc(i, db_ref, db)
        dfb = _bf(0.5 * dz)
        df_ref[...] = dfb
        dx = ALPHA * dz
        for k in range(2):
            gk = a_ref[:, k * fh:(k + 1) * fh].astype(F32)
            uk = a_ref[:, (2 + k) * fh:(3 + k) * fh].astype(F32)
            ds = _dot_nt(dfb, wo_ref[k * fh:(k + 1) * fh, :])
            sig = _sigmoid(gk)
            silu = gk * sig
            dgk = _bf(ds * uk * sig * (1.0 + gk * (1.0 - sig)))
            duk = _bf(ds * silu)
            s_ref[:, k * fh:(k + 1) * fh] = _bf(silu * uk)
            da_ref[:, k * fh:(k + 1) * fh] = dgk
            da_ref[:, (2 + k) * fh:(3 + k) * fh] = duk
            dx += _dot_nt(dgk, w4_ref[k]) + _dot_nt(duk, w4_ref[2 + k])
        dx_ref[...] = dx

    return _rowcall(name, body, T, tm, [dh, z, a], [w_in4, w_out, ln_g],
                    [(D, F32), (4 * fh, BF16), (2 * fh, BF16), (D, BF16)],
                    [((1, D), F32), ((1, D), F32)], exchange=exchange)


WG_TILE_N = 1536


def _mm_tn(name, a, b, out_dtype=BF16, n_split=1, exchange=()):
    T, M = a.shape
    N = b.shape[1]
    tk = _tile(T, 2048, 8)
    tm = _tile(M, 1408)
    tn = _tile(N // n_split, WG_TILE_N)
    per = N // n_split // tn
    nk = T // tk
    n_ex = len(exchange)
    grid = (M // tm, N // tn, nk)
    if n_split > 1:
        out_spec = pl.BlockSpec((None, tm, tn), lambda i, j, k: (j // per, i, j % per))
        out_shape = jax.ShapeDtypeStruct((n_split, M, N // n_split), out_dtype)
    else:
        out_spec = pl.BlockSpec((tm, tn), lambda i, j, k: (i, j))
        out_shape = jax.ShapeDtypeStruct((M, N), out_dtype)

    def kern(a_ref, b_ref, *refs):
        ex_in, o_ref, ex_out = refs[:n_ex], refs[n_ex], refs[n_ex + 1:2 * n_ex + 1]
        acc_ref, sems = refs[2 * n_ex + 1], refs[2 * n_ex + 2:]
        k = pl.program_id(2)
        at_end = [pl.program_id(d) == grid[d] - 1 for d in range(3)]
        if n_ex:
            @pl.when(jnp.logical_and(jnp.logical_and(pl.program_id(0) == 0, pl.program_id(1) == 0), k == 0))
            def _():
                for send, _ in _chip_copies(ex_in, ex_out, *sems, False):
                    send.start()

        part = _dot_tn(_bf(a_ref[...]), _bf(b_ref[...]))

        @pl.when(k == 0)
        def _():
            acc_ref[...] = part

        @pl.when(k != 0)
        def _():
            acc_ref[...] += part

        @pl.when(k == nk - 1)
        def _():
            o_ref[...] = acc_ref[...].astype(out_dtype)

        if n_ex:
            @pl.when(jnp.logical_and(jnp.logical_and(at_end[0], at_end[1]), at_end[2]))
            def _():
                _wait_copies(_chip_copies(ex_in, ex_out, *sems, False))

    outs = pl.pallas_call(
        kern, grid=grid,
        in_specs=[pl.BlockSpec((tk, tm), lambda i, j, k: (k, i)), pl.BlockSpec((tk, tn), lambda i, j, k: (k, j))]
        + [HBM_SPEC] * n_ex,
        out_specs=[out_spec] + [HBM_SPEC] * n_ex, out_shape=[out_shape] + _exchange_shapes(exchange),
        scratch_shapes=[pltpu.VMEM((tm, tn), F32)] + (_dma_sems(n_ex * N_PEER_CHIPS) if n_ex else []), name=name,
        compiler_params=_params(("arbitrary", "arbitrary", "arbitrary")))(a, b, *exchange)
    return outs if n_ex else outs[0]


def _proj_ret(h1, w_r, cos_r, sin_r, tm):
    T, D = h1.shape
    qk = RET_HEADS * RET_DK
    rv = RET_HEADS * RET_DV

    def body(i, h_ref, cos_ref, sin_ref, w_ref, q_ref, k_ref, v_ref, g_ref):
        hb = _bf(h_ref[...])
        cos, sin = cos_ref[...], sin_ref[...]
        for out_ref, off, scale in ((q_ref, 0, 1.0), (k_ref, qk, RET_DK ** -0.5)):
            pr = _dot(hb, w_ref[:, off:off + qk])
            for h in range(RET_HEADS):
                t = pr[:, h * RET_DK:(h + 1) * RET_DK]
                out_ref[:, h * RET_DK:(h + 1) * RET_DK] = _bf((t * cos + _roll(t, RET_DK // 2) * sin) * scale)
        v_ref[...] = _bf(_dot(hb, w_ref[:, 2 * qk:2 * qk + rv]))
        g_ref[...] = _bf(_dot(hb, w_ref[:, 2 * qk + rv:2 * qk + 2 * rv]))

    return _rowcall("proj_ret", body, T, tm, [h1, cos_r, sin_r], [w_r],
                    [(qk, BF16), (qk, BF16), (rv, BF16), (rv, BF16)])


def _rope_pe(t, c, s1, s2):
    return t * c + _roll(t, LANES - MLA_ROPE // 2) * s1 + _roll(t, MLA_ROPE // 2) * s2


def _rope_pe_bwd(dy, c, s1, s2):
    return dy * c + _roll(dy * s1, MLA_ROPE // 2) + _roll(dy * s2, LANES - MLA_ROPE // 2)


def _rms(x, g):
    r = lax.rsqrt(_mean(x * x) + EPS)
    return x * r, r


def _attn_block(T):
    return min(512, T)


def _transposed_blocks(T, tm, w, dtype):
    tb = _attn_block(T)
    per = tb // tm
    return (jax.ShapeDtypeStruct((T // tb, MLA_HEADS, w, tb), dtype),
            pl.BlockSpec((None, MLA_HEADS, w, tm), lambda i: (i // per, 0, 0, i % per)))


ATTN_SCALE = (MLA_NOPE + MLA_ROPE) ** -0.5
LOG2E = 1.4426950408889634
Q_PRESCALE = ATTN_SCALE * LOG2E
V_ONES = 16


def _proj_mla(h1, tabs, w_c, w_kpe, w_g, w_uq, w_uk, w_uv, qn_g, kvn_g, tm):
    T, D = h1.shape
    H = MLA_HEADS

    def body(i, h_ref, c_ref, s1_ref, s2_ref, wc_ref, wk_ref, wg_ref, wuq_ref, wuk_ref, wuv_ref, qg_ref, kg_ref,
             lat_ref, gt_ref, q_ref, k_ref, v_ref, ln_ref, qt_ref, kt_ref, vt_ref):
        hb = _bf(h_ref[...])
        c, s1, s2 = c_ref[...], s1_ref[...], s2_ref[...]
        lat = _dot(hb, wc_ref[...])
        lat_ref[...] = lat
        gt_ref[...] = _bf(_dot(hb, wg_ref[...]))
        cqn, _ = _rms(lat[:, :Q_LORA], None)
        ckn, _ = _rms(lat[:, Q_LORA:], None)
        cqn = _bf(cqn * qg_ref[...])
        ckn = _bf(ckn * kg_ref[...])
        ln_ref[:, :Q_LORA] = cqn
        ln_ref[:, Q_LORA:] = ckn
        q = _dot(cqn, wuq_ref[...])
        kn = _dot(ckn, wuk_ref[...])
        vv = _dot(ckn, wuv_ref[...])
        v_ref[...] = _bf(vv)
        kpe = _rope_pe(_dot(hb, wk_ref[...]), c, s1, s2)
        ones = jnp.ones((V_ONES, tm), BF16)
        for h in range(H):
            o = h * MLA_QK
            qh = jnp.concatenate([q[:, o:o + MLA_NOPE], _rope_pe(q[:, o + MLA_NOPE:o + MLA_QK], c, s1, s2)], axis=1)
            qh = qh * Q_PRESCALE
            kh = jnp.concatenate([kn[:, h * MLA_NOPE:(h + 1) * MLA_NOPE], kpe], axis=1)
            q_ref[:, o:o + MLA_QK] = _bf(qh)
            k_ref[:, o:o + MLA_QK] = _bf(kh)
            qt_ref[h] = _bf(qh.T)
            kt_ref[h] = _bf(kh.T)
            vt_ref[h] = jnp.concatenate([_bf(vv[:, h * MLA_DV:(h + 1) * MLA_DV].T), ones], axis=0)

    lat_w = Q_LORA + KV_LORA
    return _rowcall("proj_mla", body, T, tm, [h1, *tabs], [w_c, w_kpe, w_g, w_uq, w_uk, w_uv, qn_g, kvn_g],
                    [(lat_w, F32), (2 * D, BF16), (H * MLA_QK, BF16), (H * MLA_QK, BF16), (H * MLA_DV, BF16),
                     (lat_w, BF16)],
                    tiled_outs=[_transposed_blocks(T, tm, MLA_QK, BF16), _transposed_blocks(T, tm, MLA_QK, BF16),
                                _transposed_blocks(T, tm, MLA_DV + V_ONES, BF16)])


def _ret_block(T):
    return min(256, T)


RET_HEADS_PER_STEP = 8


def _ret_dmat(lg, bt):
    n = lax.broadcasted_iota(jnp.int32, (bt, bt), 0)
    m = lax.broadcasted_iota(jnp.int32, (bt, bt), 1)
    return jnp.where(_chunk_of(m) <= _chunk_of(n), jnp.exp(lg * jnp.abs(n - m).astype(F32)), 0.0)


def _ret_scan(name, per_head, lgam, ins, outs, rev):
    T = ins[0][0].shape[0]
    bt = _ret_block(T)
    nb = T // bt
    hps = min(RET_HEADS_PER_STEP, RET_HEADS)
    n_in, n_out = len(ins), len(outs)

    def kern(lg_ref, *refs):
        in_refs, out_refs = refs[:n_in], refs[n_in:n_in + n_out]
        state_ref, dmat_ref = refs[n_in + n_out:]

        @pl.when(pl.program_id(1) == 0)
        def _():
            state_ref[...] = jnp.zeros_like(state_ref)
            for hh in range(hps):
                dmat_ref[hh] = _ret_dmat(lg_ref[hh][:, :1], bt)

        pos = lax.broadcasted_iota(jnp.int32, (bt, 1), 0).astype(F32)
        for hh in range(hps):
            lg = lg_ref[hh][:, :1]
            xi, zeta, gb = jnp.exp(lg * (pos + 1.0)), jnp.exp(lg * (bt - 1.0 - pos)), jnp.exp(lg * bt)
            tiles = [r[:, hh * w:(hh + 1) * w] for r, (_, w) in zip(in_refs, ins)]
            res = per_head(dmat_ref[hh], xi, zeta, gb, state_ref.at[hh], *tiles)
            for o_ref, (w, _), val in zip(out_refs, outs, res):
                o_ref[:, hh * w:(hh + 1) * w] = val.astype(o_ref.dtype)

    def blk(w):
        if rev:
            return pl.BlockSpec((bt, hps * w), lambda g, b: (nb - 1 - b, g))
        return pl.BlockSpec((bt, hps * w), lambda g, b: (b, g))

    return pl.pallas_call(
        kern, grid=(RET_HEADS // hps, nb),
        in_specs=[pl.BlockSpec((hps, 1, LANES), lambda g, b: (g, 0, 0))] + [blk(w) for _, w in ins],
        out_specs=[blk(w) for w, _ in outs],
        out_shape=[jax.ShapeDtypeStruct((T, RET_HEADS * w), dt) for w, dt in outs],
        scratch_shapes=[pltpu.VMEM((hps, RET_DK, RET_DV), F32), pltpu.VMEM((hps, bt, bt), F32)], name=name,
        compiler_params=_params(("arbitrary", "arbitrary")))(lgam, *[a for a, _ in ins])


def _ret_fwd(rq, rk, rv, lgam):
    def per_head(dmat, xi, zeta, gb, s_ref, q, k, v):
        sc = _dot_nt(q, k) * dmat
        y = _dot(_bf(sc), v) + _dot(q, _bf(s_ref[...])) * xi
        s_ref[...] = s_ref[...] * gb + _dot_tn(_bf(k.astype(F32) * zeta), v)
        return (y,)

    return _ret_scan("ret_fwd", per_head, lgam, [(rq, RET_DK), (rk, RET_DK), (rv, RET_DV)], [(RET_DV, BF16)], False)[0]


def _ret_bwd_q(rq, rk, rv, dy, lgam):
    def per_head(dmat, xi, zeta, gb, s_ref, k, v, dy):
        dp = _dot_nt(dy, v) * dmat
        dq = _dot(_bf(dp), k) + _dot_nt(dy, _bf(s_ref[...])) * xi
        s_ref[...] = s_ref[...] * gb + _dot_tn(_bf(k.astype(F32) * zeta), v)
        return (dq,)

    return _ret_scan("ret_bwd_q", per_head, lgam, [(rk, RET_DK), (rv, RET_DV), (dy, RET_DV)], [(RET_DK, F32)], False)[0]


def _ret_bwd_kv(rq, rk, rv, dy, lgam):
    def per_head(dmat, xi, zeta, gb, g_ref, q, k, v, dy):
        gs = _bf(g_ref[...])
        p = _dot_nt(q, k) * dmat
        dp = _dot_nt(dy, v) * dmat
        dv = _dot_tn(_bf(p), dy) + _dot(k, gs) * zeta
        dk = _dot_tn(_bf(dp), q) + _dot_nt(v, gs) * zeta
        g_ref[...] = g_ref[...] * gb + _dot_tn(_bf(q.astype(F32) * xi), dy)
        return dk, dv

    return _ret_scan("ret_bwd_kv", per_head, lgam, [(rq, RET_DK), (rk, RET_DK), (rv, RET_DV), (dy, RET_DV)],
                     [(RET_DK, F32), (RET_DV, BF16)], True)


def _attn_mask_t(tb):
    key = lax.broadcasted_iota(jnp.int32, (tb, tb), 0)
    qry = lax.broadcasted_iota(jnp.int32, (tb, tb), 1)
    return _chunk_of(key) <= _chunk_of(qry)


MASKED = -1e30
SUBLANES = 8


def _head_blocks(nb, w, tb):
    return pl.BlockSpec((nb, None, w, tb), lambda h, i: (0, h, 0, 0))


def _one_block(w, tb):
    return pl.BlockSpec((None, None, w, tb), lambda h, i: (i, h, 0, 0))


def _attn_fwd(k, qt, vt, exchange=()):
    T = k.shape[0]
    tb = _attn_block(T)
    nb = T // tb

    n_ex = len(exchange)

    def kern(qt_ref, k_ref, vt_ref, *refs):
        ex_in, (o_ref, lser_ref), ex_out = refs[:n_ex], refs[n_ex:n_ex + 2], refs[n_ex + 2:2 * n_ex + 2]
        m_ref, acc_ref, sa_ref, sb_ref = refs[2 * n_ex + 2:2 * n_ex + 6]
        sems = refs[2 * n_ex + 6:]
        qb = pl.program_id(1)
        first = jnp.logical_and(pl.program_id(0) == 0, qb == 0)
        last = jnp.logical_and(pl.program_id(0) == MLA_HEADS - 1, qb == nb - 1)
        if n_ex:
            @pl.when(first)
            def _():
                for send, _ in _chip_copies(ex_in, ex_out, *sems, True):
                    send.start()

        qt = qt_ref[...]
        m_ref[...] = jnp.full_like(m_ref, MASKED)
        acc_ref[...] = jnp.zeros_like(acc_ref)

        def scores(kb):
            rows = pl.ds(pl.multiple_of(kb * tb, tb), tb)
            return _dot(k_ref[rows, :], qt)

        def update(s, kb):
            m_old = m_ref[...]
            m_new = jnp.maximum(m_old, jnp.max(s, axis=0, keepdims=True))
            p = jnp.exp2(s - m_new)
            acc_ref[...] = acc_ref[...] * jnp.exp2(m_old - m_new) + _dot(vt_ref[kb], _bf(p))
            m_ref[...] = m_new

        def masked(s):
            return jnp.where(_attn_mask_t(tb), s, MASKED)

        sa_ref[...] = scores(0)

        def pair_body(j, carry):
            sb_ref[...] = scores(2 * j + 1)
            update(sa_ref[...], 2 * j)
            sa_ref[...] = scores(2 * j + 2)
            update(sb_ref[...], 2 * j + 1)
            return carry

        lax.fori_loop(0, qb // 2, pair_body, 0)

        @pl.when(qb % 2 == 0)
        def _():
            update(masked(sa_ref[...]), qb)

        @pl.when(qb % 2 == 1)
        def _():
            sb_ref[...] = masked(scores(qb))
            update(sa_ref[...], qb - 1)
            update(sb_ref[...], qb)

        l = acc_ref[MLA_DV:MLA_DV + 1, :]
        o_ref[...] = _bf((acc_ref[:MLA_DV, :] / l).T)
        lser_ref[...] = jnp.broadcast_to(m_ref[...] + jnp.log2(l), (SUBLANES, tb))
        if n_ex:
            @pl.when(last)
            def _():
                _wait_copies(_chip_copies(ex_in, ex_out, *sems, True))

    return pl.pallas_call(
        kern, grid=(MLA_HEADS, nb),
        in_specs=[_one_block(MLA_QK, tb), pl.BlockSpec((T, MLA_QK), lambda h, i: (0, h)),
                  _head_blocks(nb, MLA_DV + V_ONES, tb)] + [HBM_SPEC] * n_ex,
        out_specs=[pl.BlockSpec((tb, MLA_DV), lambda h, i: (i, h)), _one_block(SUBLANES, tb)] + [HBM_SPEC] * n_ex,
        out_shape=[jax.ShapeDtypeStruct((T, MLA_HEADS * MLA_DV), BF16),
                   jax.ShapeDtypeStruct((nb, MLA_HEADS, SUBLANES, tb), F32)] + _exchange_shapes(exchange),
        scratch_shapes=[pltpu.VMEM((1, tb), F32), pltpu.VMEM((MLA_DV + V_ONES, tb), F32),
                        pltpu.VMEM((tb, tb), F32), pltpu.VMEM((tb, tb), F32)]
        + (_dma_sems(n_ex * N_PEER_CHIPS) if n_ex else []),
        name="attn_fwd", compiler_params=_params(("arbitrary", "arbitrary")))(qt, k, vt, *exchange)


def _attn_bwd(q, k, v, do, qt, kt, dot_, lse_rows, delta_rows):
    T = q.shape[0]
    tb = _attn_block(T)
    nb = T // tb

    def kern(q_ref, k_ref, v_ref, do_ref, qt_ref, kt_ref, dot_ref, lse_ref, dl_ref, dk_ref, dv_ref, dqt_ref, dv_acc,
             sa_ref, pa_ref, sb_ref, pb_ref):
        kb = pl.program_id(1)
        kv, vv, ktv = k_ref[...], v_ref[...], kt_ref[...]
        dk_ref[...] = jnp.zeros_like(dk_ref)
        dv_acc[...] = jnp.zeros_like(dv_acc)

        @pl.when(kb == 0)
        def _():
            dqt_ref[...] = jnp.zeros_like(dqt_ref)

        def products(qb, s_ref, dp_ref, diagonal=False):
            s = _dot(kv, qt_ref[qb])
            s_ref[...] = jnp.where(_attn_mask_t(tb), s, MASKED) if diagonal else s
            dp_ref[...] = _dot(vv, dot_ref[qb])

        def consume(qb, s_ref, dp_ref):
            rows = pl.ds(pl.multiple_of(qb * tb, tb), tb)
            p = jnp.exp2(s_ref[...] - lse_ref[qb][:1, :])
            dv_acc[...] += _dot(_bf(p), do_ref[rows, :])
            ds = _bf(p * (dp_ref[...] - dl_ref[qb][:1, :]))
            dk_ref[...] += _dot(ds, q_ref[rows, :])
            dqt_ref[qb] += _dot(ktv, ds)

        n_full = nb - 1 - kb
        products(kb, sa_ref, pa_ref, diagonal=True)

        def pair_body(j, carry):
            q1 = kb + 1 + 2 * j
            products(q1, sb_ref, pb_ref)
            consume(q1 - 1, sa_ref, pa_ref)
            products(q1 + 1, sa_ref, pa_ref)
            consume(q1, sb_ref, pb_ref)
            return carry

        lax.fori_loop(0, n_full // 2, pair_body, 0)

        @pl.when(n_full % 2 == 0)
        def _():
            consume(nb - 1, sa_ref, pa_ref)

        @pl.when(n_full % 2 == 1)
        def _():
            products(nb - 1, sb_ref, pb_ref)
            consume(nb - 2, sa_ref, pa_ref)
            consume(nb - 1, sb_ref, pb_ref)

        dk_ref[...] = dk_ref[...] * (ATTN_SCALE / Q_PRESCALE)
        dv_ref[...] = _bf(dv_acc[...])

    def blk(w):
        return pl.BlockSpec((tb, w), lambda h, i: (i, h))

    def full(w):
        return pl.BlockSpec((T, w), lambda h, i: (0, h))

    return pl.pallas_call(
        kern, grid=(MLA_HEADS, nb),
        in_specs=[full(MLA_QK), blk(MLA_QK), blk(MLA_DV), full(MLA_DV), _head_blocks(nb, MLA_QK, tb),
                  _one_block(MLA_QK, tb), _head_blocks(nb, MLA_DV, tb), _head_blocks(nb, SUBLANES, tb),
                  _head_blocks(nb, SUBLANES, tb)],
        out_specs=[blk(MLA_QK), blk(MLA_DV), _head_blocks(nb, MLA_QK, tb)],
        out_shape=[jax.ShapeDtypeStruct((T, MLA_HEADS * MLA_QK), F32),
                   jax.ShapeDtypeStruct((T, MLA_HEADS * MLA_DV), BF16),
                   jax.ShapeDtypeStruct((nb, MLA_HEADS, MLA_QK, tb), F32)],
        scratch_shapes=[pltpu.VMEM((tb, MLA_DV), F32)] + [pltpu.VMEM((tb, tb), F32)] * 4,
        name="attn_bwd", compiler_params=_params(("arbitrary", "arbitrary")))(
            q, k, v, do, qt, kt, dot_, lse_rows, delta_rows)


def _group_norm(y):
    yc = y - _mean(y)
    rstd = lax.rsqrt(_mean(yc * yc) + EPS)
    return yc * rstd, rstd


def _mix_fwd(y, rg, o, gates, h1, gn_g, w_ret_o, w_mla_o, w_out, ln_g, ln_b, tm):
    T, D = h1.shape

    def body(i, y_ref, rg_ref, o_ref, gt_ref, h_ref, gn_ref, wr_ref, wm_ref, wo_ref, g_ref, b_ref,
             h2_ref, z_ref, yret_ref, ymla_ref, yr_ref, mix_ref):
        for h in range(RET_HEADS):
            sl = slice(h * RET_DV, (h + 1) * RET_DV)
            yn, _ = _group_norm(y_ref[:, sl].astype(F32))
            r = rg_ref[:, sl].astype(F32)
            yr_ref[:, sl] = _bf(r * _sigmoid(r) * (yn * gn_ref[:, sl]))
        yret = _dot(yr_ref[...], wr_ref[...])
        ymla = _dot(_bf(o_ref[...]), wm_ref[...])
        yret_ref[...] = _bf(yret)
        ymla_ref[...] = _bf(ymla)
        mix = _bf(_sigmoid(gt_ref[:, :D].astype(F32)) * yret + _sigmoid(gt_ref[:, D:].astype(F32)) * ymla)
        mix_ref[...] = mix
        z = ALPHA * h_ref[...] + _dot(mix, wo_ref[...])
        xhat, _ = _ln_stats(z)
        z_ref[...] = z
        h2_ref[...] = xhat * g_ref[...] + b_ref[...]

    return _rowcall("mix_fwd", body, T, tm, [y, rg, o, gates, h1], [gn_g, w_ret_o, w_mla_o, w_out, ln_g, ln_b],
                    [(D, F32), (D, F32), (D, BF16), (D, BF16), (RET_HEADS * RET_DV, BF16), (D, BF16)])


def _mix_bwd(dh2, z1, gates, yret, ymla, y, rg, o, gn_g, w_ret_o, w_mla_o, w_out, ln_g, tm, exchange=None):
    T, D = dh2.shape
    rv = RET_HEADS * RET_DV

    def body(i, dh_ref, z_ref, gt_ref, yret_ref, ymla_ref, y_ref, rg_ref, o_ref, gn_ref, wr_ref, wm_ref, wo_ref, g_ref,
             dz_ref, dgt_ref, drg_ref, dy_ref, do_ref, dyret_ref, dymla_ref, dg_ref, db_ref, dgn_ref, dot_ref,
             dl_ref):
        xhat, rstd = _ln_stats(z_ref[...])
        dz, dg, db = _ln_bwd(dh_ref[...], xhat, rstd, g_ref[...])
        _acc(i, dg_ref, dg)
        _acc(i, db_ref, db)
        dz_ref[...] = dz
        dmix = _dot_nt(_bf(dz), wo_ref[...])
        sr = _sigmoid(gt_ref[:, :D].astype(F32))
        sm = _sigmoid(gt_ref[:, D:].astype(F32))
        dgt_ref[:, :D] = _bf(dmix * yret_ref[...].astype(F32) * sr * (1.0 - sr))
        dgt_ref[:, D:] = _bf(dmix * ymla_ref[...].astype(F32) * sm * (1.0 - sm))
        dyret = _bf(dmix * sr)
        dymla = _bf(dmix * sm)
        dyret_ref[...] = dyret
        dymla_ref[...] = dymla
        dov = _dot_nt(dymla, wm_ref[...])
        do_ref[...] = _bf(dov)
        for h in range(MLA_HEADS):
            sl = slice(h * MLA_DV, (h + 1) * MLA_DV)
            dot_ref[h] = _bf(dov[:, sl].T)
            delta = jnp.sum(dov[:, sl] * o_ref[:, sl].astype(F32), axis=-1, keepdims=True)
            dl_ref[h] = jnp.broadcast_to(delta, (tm, LANES)).T[:SUBLANES, :]
        dyr = _dot_nt(dyret, wr_ref[...])
        dgn = []
        for h in range(RET_HEADS):
            sl = slice(h * RET_DV, (h + 1) * RET_DV)
            yn, grstd = _group_norm(y_ref[:, sl].astype(F32))
            r = rg_ref[:, sl].astype(F32)
            sig = _sigmoid(r)
            d = dyr[:, sl]
            drg_ref[:, sl] = _bf(d * (yn * gn_ref[:, sl]) * sig * (1.0 + r * (1.0 - sig)))
            dt = d * (r * sig)
            dgn.append(jnp.sum(dt * yn, axis=0, keepdims=True))
            dyn = dt * gn_ref[:, sl]
            dy_ref[:, sl] = _bf(grstd * (dyn - _mean(dyn) - yn * _mean(dyn * yn)))
        _acc(i, dgn_ref, jnp.concatenate(dgn, axis=1))

    return _rowcall("mix_bwd", body, T, tm, [dh2, z1, gates, yret, ymla, y, rg, o],
                    [gn_g, w_ret_o, w_mla_o, w_out, ln_g],
                    [(D, F32), (2 * D, BF16), (rv, BF16), (rv, BF16), (MLA_HEADS * MLA_DV, BF16), (D, BF16), (D, BF16)],
                    [((1, D), F32), ((1, D), F32), ((1, rv), F32)],
                    tiled_outs=[_transposed_blocks(T, tm, MLA_DV, BF16), _transposed_blocks(T, tm, SUBLANES, F32)],
                    exchange=exchange)


def _proj_mla_bwd(dqt, dk, dv, lat, tabs, w_uq, w_uk, w_uv, qn_g, kvn_g, tm):
    T = dk.shape[0]
    H = MLA_HEADS
    lat_w = Q_LORA + KV_LORA

    def body(i, dk_ref, dv_ref, lat_ref, c_ref, s1_ref, s2_ref, dqt_ref, wuq_ref, wuk_ref, wuv_ref, qg_ref, kg_ref,
             dlat_ref, dkpe_ref, dqb_ref, dkn_ref, dqg_ref, dkg_ref):
        c, s1, s2 = c_ref[...], s1_ref[...], s2_ref[...]
        dkpe = jnp.zeros((tm, LANES), F32)
        for h in range(H):
            o = h * MLA_QK
            dqh = dqt_ref[h].T * ATTN_SCALE
            dqb_ref[:, o:o + MLA_NOPE] = _bf(dqh[:, :MLA_NOPE])
            dqb_ref[:, o + MLA_NOPE:o + MLA_QK] = _bf(_rope_pe_bwd(dqh[:, MLA_NOPE:], c, s1, s2))
            dkn_ref[:, h * MLA_NOPE:(h + 1) * MLA_NOPE] = _bf(dk_ref[:, o:o + MLA_NOPE])
            dkpe += dk_ref[:, o + MLA_NOPE:o + MLA_QK]
        dkn_ref[:, H * MLA_NOPE:] = dv_ref[...]
        dkpe_ref[...] = _bf(_rope_pe_bwd(dkpe, c, s1, s2))
        dcqn = _dot_nt(dqb_ref[...], wuq_ref[...])
        dckn = _dot_nt(dkn_ref[:, :H * MLA_NOPE], wuk_ref[...]) + _dot_nt(dv_ref[...], wuv_ref[...])
        for dn, x, g_ref, dg_ref, sl in ((dcqn, lat_ref[:, :Q_LORA], qg_ref, dqg_ref, slice(0, Q_LORA)),
                                         (dckn, lat_ref[:, Q_LORA:], kg_ref, dkg_ref, slice(Q_LORA, lat_w))):
            xn, r = _rms(x, None)
            _acc(i, dg_ref, jnp.sum(dn * xn, axis=0, keepdims=True))
            dxn = dn * g_ref[...]
            dlat_ref[:, sl] = _bf(r * (dxn - xn * _mean(dxn * xn)))

    dqt_shape, dqt_spec = _transposed_blocks(T, tm, MLA_QK, F32)
    assert dqt.shape == dqt_shape.shape
    return _rowcall("proj_mla_bwd", body, T, tm, [dk, dv, lat, *tabs], [w_uq, w_uk, w_uv, qn_g, kvn_g],
                    [(lat_w, BF16), (LANES, BF16), (H * MLA_QK, BF16), (H * (MLA_NOPE + MLA_DV), BF16)],
                    [((1, Q_LORA), F32), ((1, KV_LORA), F32)], tiled_ins=[(dqt, dqt_spec)])


def _proj_bwd(drq, drk, drv, drg, dz1, dlat, dkpe, dgates, cos_r, sin_r, w_r, w_c, w_kpe, w_g, tm):
    T, D = dz1.shape
    qk = RET_HEADS * RET_DK
    rv = RET_HEADS * RET_DV
    o_lat = 2 * qk + 2 * rv
    o_kpe = o_lat + dlat.shape[1]
    o_gate = o_kpe + LANES
    o_end = o_gate + dgates.shape[1]
    width = -(-o_end // WG_TILE_N) * WG_TILE_N

    def body(i, drq_ref, drk_ref, drv_ref, drg_ref, dz_ref, dlat_ref, dkpe_ref, dgt_ref, cos_ref, sin_ref,
             wr_ref, wc_ref, wk_ref, wg_ref, dh_ref, dpr_ref):
        cos, sin = cos_ref[...], sin_ref[...]
        for src, off, scale in ((drq_ref, 0, 1.0), (drk_ref, qk, RET_DK ** -0.5)):
            for h in range(RET_HEADS):
                d = src[:, h * RET_DK:(h + 1) * RET_DK]
                dpr_ref[:, off + h * RET_DK:off + (h + 1) * RET_DK] = _bf(
                    (d * cos + _roll(d * sin, RET_DK // 2)) * scale)
        dpr_ref[:, 2 * qk:2 * qk + rv] = drv_ref[...]
        dpr_ref[:, 2 * qk + rv:o_lat] = drg_ref[...]
        dpr_ref[:, o_lat:o_kpe] = dlat_ref[...]
        dpr_ref[:, o_kpe:o_gate] = dkpe_ref[...]
        dpr_ref[:, o_gate:o_end] = dgt_ref[...]
        if width > o_end:
            dpr_ref[:, o_end:] = jnp.zeros((tm, width - o_end), BF16)
        dh_ref[...] = (ALPHA * dz_ref[...] + _dot_nt(dpr_ref[:, :o_lat], wr_ref[...])
                       + _dot_nt(dlat_ref[...], wc_ref[...]) + _dot_nt(dkpe_ref[...], wk_ref[...])
                       + _dot_nt(dgt_ref[...], wg_ref[...]))

    return _rowcall("proj_bwd", body, T, tm, [drq, drk, drv, drg, dz1, dlat, dkpe, dgates, cos_r, sin_r],
                    [w_r, w_c, w_kpe, w_g], [(D, F32), (width, BF16)])


def _ple_loss(h3, p, target, w_gate, w_proj, ln_g, ln_b, tm):
    T, D = h3.shape

    def body(i, h_ref, p_ref, t_ref, wg_ref, wp_ref, g_ref, b_ref, dh_ref, dgp_ref, dpp_ref, loss_ref, dg_ref, db_ref):
        hv = h_ref[...]
        sg = _sigmoid(_dot(_bf(hv), wg_ref[...]))
        pp = _dot(_bf(p_ref[...]), wp_ref[...])
        xhat, rstd = _ln_stats(ALPHA * hv + sg * pp)
        err = xhat * g_ref[...] + b_ref[...] - t_ref[...]
        row_loss = 0.5 * _mean(err * err)
        _acc(i, loss_ref, jnp.broadcast_to(jnp.sum(row_loss, axis=0, keepdims=True), (1, LANES)))
        dz, dg, db = _ln_bwd(err * (1.0 / D), xhat, rstd, g_ref[...])
        _acc(i, dg_ref, dg)
        _acc(i, db_ref, db)
        dgp = _bf(dz * pp * sg * (1.0 - sg))
        dgp_ref[...] = dgp
        dpp_ref[...] = _bf(dz * sg)
        dh_ref[...] = ALPHA * dz + _dot_nt(dgp, wg_ref[...])

    return _rowcall("ple_loss", body, T, tm, [h3, p, target], [w_gate, w_proj, ln_g, ln_b],
                    [(D, F32), (D, BF16), (D, BF16)], [((1, LANES), F32), ((1, D), F32), ((1, D), F32)])


def _ewise(name, fn, ins, n_out, out_dtype=F32):
    r, c = ins[0].shape
    tr = _tile(r, max(8, (1 << 19) // c // 8 * 8), 8)

    def kern(*refs):
        outs = fn(*[x[...] for x in refs[:len(ins)]])
        for o_ref, o in zip(refs[len(ins):], outs):
            o_ref[...] = o.astype(out_dtype)

    spec = pl.BlockSpec((tr, c), lambda i: (i, 0))
    return pl.pallas_call(kern, grid=(r // tr,), in_specs=[spec] * len(ins), out_specs=[spec] * n_out,
                          out_shape=[jax.ShapeDtypeStruct((r, c), out_dtype)] * n_out, name=name,
                          compiler_params=_params(("arbitrary",)))(*ins)


def _adamw_math(w, g, m, v):
    m = ADAM_B1 * m + (1.0 - ADAM_B1) * g
    v = ADAM_B2 * v + (1.0 - ADAM_B2) * (g * g)
    m_hat = m / (1.0 - ADAM_B1 ** ADAM_STEP)
    v_hat = v / (1.0 - ADAM_B2 ** ADAM_STEP)
    return -ADAM_LR * (m_hat / (jnp.sqrt(v_hat) + ADAM_EPS) + ADAM_WD * w), m, v


def _adamw(name, w, g, m, v):
    shape = w.shape
    c = shape[-1]
    flat = [t.reshape(-1, c) for t in (w, g, m, v)]
    return [t.reshape(shape) for t in _ewise(name, _adamw_math, flat, 3)]


def _place():
    return lax.axis_index("x"), lax.axis_index("y"), lax.axis_index("c")


def _dma_sems(n):
    return [pltpu.SemaphoreType.DMA((n,)), pltpu.SemaphoreType.DMA((n,))]


N_PEER_CHIPS = N_CHIPS - 1


def _chips_exchange(name, srcs, broadcast):
    n = len(srcs)

    def kern(*refs):
        cps = _chip_copies(refs[:n], refs[n:2 * n], refs[2 * n], refs[2 * n + 1], broadcast)
        for send, _ in cps:
            send.start()
        _wait_copies(cps)

    return pl.pallas_call(
        kern, out_shape=_exchange_shapes(srcs), in_specs=[HBM_SPEC] * n, out_specs=[HBM_SPEC] * n,
        scratch_shapes=_dma_sems(n * N_PEER_CHIPS), name=name)(*srcs)


def _exchange_shapes(srcs):
    return [jax.ShapeDtypeStruct((N_CHIPS,) + s.shape[1:], s.dtype) for s in srcs]


def _chip_copies(src_refs, out_refs, send_sems, recv_sems, broadcast):
    x, y, c = _place()
    me = 2 * x + y
    peers = [(1 - x, y), (x, 1 - y), (1 - x, 1 - y)]
    cps = []
    for j, (px, py) in enumerate(peers):
        for a, (src_ref, out_ref) in enumerate(zip(src_refs, out_refs)):
            piece = src_ref.at[c] if broadcast else src_ref.at[2 * px + py]

            def copy(slot):
                return pltpu.make_async_remote_copy(
                    src_ref=piece, dst_ref=out_ref.at[slot], send_sem=send_sems.at[a * N_PEER_CHIPS + j],
                    recv_sem=recv_sems.at[a * N_PEER_CHIPS + j], device_id=(px, py, c), device_id_type=MESH)

            cps.append((copy(me), copy(2 * px + py)))
    return cps


def _wait_copies(cps):
    for _, landing in cps:
        landing.wait_recv()
    for send, _ in cps:
        send.wait_send()


def _sibling_swap(name, srcs, mode):
    n = len(srcs)
    per = N_PEER_CHIPS if mode == "others" else 1

    def kern(*refs):
        src_refs, out_refs = refs[:n], refs[n:2 * n]
        send_sems, recv_sems = refs[2 * n:]
        x, y, c = _place()
        slots = [2 * (1 - x) + y, 2 * x + 1 - y, 2 * (1 - x) + 1 - y]
        cps = []
        for a in range(n):
            if mode == "others":
                pieces = [(src_refs[a].at[k], out_refs[a].at[k]) for k in slots]
            else:
                pieces = [(src_refs[a].at[:, 1 - c] if mode == "halves" else src_refs[a], out_refs[a])]
            for j, (src, dst) in enumerate(pieces):
                cps.append(pltpu.make_async_remote_copy(
                    src_ref=src, dst_ref=dst, send_sem=send_sems.at[a * per + j], recv_sem=recv_sems.at[a * per + j],
                    device_id=(x, y, 1 - c), device_id_type=MESH))
        for cp in cps:
            cp.start()
        for cp in cps:
            cp.wait_recv()
        for cp in cps:
            cp.wait_send()

    def out_shape(s):
        return jax.ShapeDtypeStruct((s.shape[0],) + s.shape[2:] if mode == "halves" else s.shape, s.dtype)

    return pl.pallas_call(
        kern, out_shape=[out_shape(s) for s in srcs], in_specs=[HBM_SPEC] * n, out_specs=[HBM_SPEC] * n,
        scratch_shapes=_dma_sems(n * per), name=name)(*srcs)


def _all_devices(name, src, reduce):
    r, c = src.shape
    n_dev = 2 * N_CHIPS

    def kern(src_ref, out_ref, *scratch):
        if reduce:
            gat_ref, send_sems, recv_sems = scratch
        else:
            gat_ref = out_ref
            send_sems, recv_sems = scratch
        x, y, cc = _place()
        me = 4 * x + 2 * y + cc
        gat_ref[me] = src_ref[...]
        peers = []
        for j in range(1, n_dev):
            px = 1 - x if j & 4 else x
            py = 1 - y if j & 2 else y
            pc = 1 - cc if j & 1 else cc
            peers.append((px, py, pc))

        def copy(j, peer, slot):
            return pltpu.make_async_remote_copy(
                src_ref=src_ref, dst_ref=gat_ref.at[slot], send_sem=send_sems.at[j], recv_sem=recv_sems.at[j],
                device_id=peer, device_id_type=MESH)

        sends = [copy(j, peer, me) for j, peer in enumerate(peers)]
        for cp in sends:
            cp.start()
        for j, (px, py, pc) in enumerate(peers):
            copy(j, (px, py, pc), 4 * px + 2 * py + pc).wait_recv()
        for cp in sends:
            cp.wait_send()
        if reduce:
            total = gat_ref[0]
            for d in range(1, n_dev):
                total = total + gat_ref[d]
            out_ref[...] = total

    out_shape = jax.ShapeDtypeStruct((r, c) if reduce else (n_dev, r, c), src.dtype)
    scratch = ([pltpu.VMEM((n_dev, r, c), src.dtype)] if reduce else []) + _dma_sems(n_dev - 1)
    return pl.pallas_call(kern, out_shape=out_shape, in_specs=[VMEM_SPEC], out_specs=VMEM_SPEC,
                          scratch_shapes=scratch, name=name)(src)


def _halves(t, axis):
    return t.reshape(t.shape[:axis] + (2, t.shape[axis] // 2) + t.shape[axis + 1:])


def _by_core(mine, theirs, axis):
    c = lax.axis_index("c")
    both = jnp.where(c == 0, jnp.stack([mine, theirs], axis), jnp.stack([theirs, mine], axis))
    return both.reshape(both.shape[:axis] + (2 * both.shape[axis + 1],) + both.shape[axis + 2:])


def _with_own(own, others):
    me = 2 * lax.axis_index("x") + lax.axis_index("y")
    is_me = (jnp.arange(N_CHIPS, dtype=jnp.int32) == me)[:, None, None]
    return jnp.where(is_me, own[None], others)


def _join_shards(name, shards):
    _, r, c = shards.shape
    if name in COL_SHARDED:
        return shards.transpose(1, 0, 2).reshape(r, N_CHIPS * c)
    return shards.reshape(N_CHIPS * r, c)


def _split_shards(name, full):
    if full.ndim == 3:
        return full
    r, c = full.shape
    if name in COL_SHARDED:
        return jnp.stack([full[:, k * (c // N_CHIPS):(k + 1) * (c // N_CHIPS)] for k in range(N_CHIPS)])
    return full.reshape(N_CHIPS, r // N_CHIPS, c)


def _rope_tables(positions):
    pos = positions.reshape(-1).astype(F32)[:, None]
    half = RET_DK // 2
    ang = pos * (ROPE_BASE ** (-jnp.arange(half, dtype=F32) / half))
    cos_r = jnp.concatenate([jnp.cos(ang)] * 2, axis=1)
    sin_r = jnp.concatenate([-jnp.sin(ang), jnp.sin(ang)], axis=1)
    half = MLA_ROPE // 2
    ang = pos * (ROPE_BASE ** (-jnp.arange(half, dtype=F32) / half))
    zeros = jnp.zeros_like(ang)
    rest = LANES - MLA_ROPE
    c = jnp.concatenate([jnp.cos(ang)] * 2 + [jnp.ones((ang.shape[0], rest), F32)], axis=1)
    s1 = jnp.concatenate([-jnp.sin(ang), zeros, jnp.zeros((ang.shape[0], rest), F32)], axis=1)
    s2 = jnp.concatenate([zeros, jnp.sin(ang), jnp.zeros((ang.shape[0], rest), F32)], axis=1)
    return cos_r, sin_r, (c, s1, s2)


GATHER_GROUPS = (("ffn1_w_in",), ("ffn1_w_out", "w_in"), ("w_uq", "w_ukv"),
                 ("w_ret_o", "w_mla_o", "w_out", "ffn2_w_in", "ffn2_w_out", "ple_w_gate", "ple_w_proj"))
REDUCE_GROUPS = (("ple_w_gate", "ple_w_proj", "ffn2_w_in", "ffn2_w_out"),
                 ("w_out", "w_ret_o", "w_mla_o", "w_uq", "w_ukv", "w_in"), ("ffn1_w_in",), ("ffn1_w_out",))


def _gathered(tag, names, own, mine):
    theirs = _sibling_swap("gather_cores_" + tag, mine, "others")
    out = {}
    for n, m, t in zip(names, mine, theirs):
        full = _with_own(own[n], _by_core(m, t, 1))
        out[n] = full if n in ("ffn1_w_in", "ffn2_w_in") else _join_shards(n, full)
    return out


def _chip_sums(tag, names, grads):
    halves = [_halves(_split_shards(n, grads[n]), 1) for n in names]
    theirs = _sibling_swap("reduce_cores_" + tag, halves, "halves")

    def one(n, g, t):
        k, _, r, c = g.shape
        tr = _tile(r, max(8, (1 << 17) // c // 8 * 8), 8)

        def kern(g_ref, t_ref, o_ref):
            mine = jnp.where(lax.axis_index("c") == 0, g_ref[:, 0], g_ref[:, 1])
            o_ref[...] = _bf(mine.astype(F32) + t_ref[...].astype(F32))

        spec = pl.BlockSpec((k, tr, c), lambda i: (0, i, 0))
        return pl.pallas_call(kern, grid=(r // tr,),
                              in_specs=[pl.BlockSpec((k, 2, tr, c), lambda i: (0, 0, i, 0)), spec], out_specs=spec,
                              out_shape=jax.ShapeDtypeStruct((k, r, c), BF16), name="reduce_cores_add_" + n,
                              compiler_params=_params(("arbitrary",)))(g, t)

    return [one(n, g, t) for n, g, t in zip(names, halves, theirs)]


def _block_totals(names, sums, parts):
    def one(n, s, pt):
        _, r, c = s.shape
        tr = _tile(r, max(8, (1 << 17) // c // 8 * 8), 8)

        def kern(s_ref, p_ref, o_ref):
            me = 2 * lax.axis_index("x") + lax.axis_index("y")
            terms = [jnp.where(k == me, s_ref[k], p_ref[k]).astype(F32) for k in range(N_CHIPS)]
            o_ref[...] = ((terms[0] + terms[1]) + terms[2]) + terms[3]

        spec = pl.BlockSpec((N_CHIPS, tr, c), lambda i: (0, i, 0))
        return pl.pallas_call(kern, grid=(r // tr,), in_specs=[spec, spec],
                              out_specs=pl.BlockSpec((tr, c), lambda i: (i, 0)),
                              out_shape=jax.ShapeDtypeStruct((r, c), F32), name="reduce_chips_add_" + n,
                              compiler_params=_params(("arbitrary",)))(s, pt)

    return [one(n, s, pt) for n, s, pt in zip(names, sums, parts)]


def _local_step(x, p, positions, target, shards, ln_g, ln_b, gn_g, qn_g, kvn_g):
    T, D = x.shape
    tm = min(256, T)
    H = MLA_HEADS
    qk, rv = RET_HEADS * RET_DK, RET_HEADS * RET_DV
    cos_r, sin_r, tabs = _rope_tables(positions)
    lgam = jnp.broadcast_to(jnp.log(1.0 - 2.0 ** (-5.0 - jnp.arange(RET_HEADS, dtype=F32)))[:, None, None],
                            (RET_HEADS, 1, LANES))
    lng = [ln_g[k:k + 1] for k in range(N_LN)]
    lnb = [ln_b[k:k + 1] for k in range(N_LN)]
    own = {n: _bf(shards[n]) for n in BIG_WEIGHTS}
    to_send = [[_halves(own[n], 0) for n in names] for names in GATHER_GROUPS]

    w = _gathered("a", GATHER_GROUPS[0], own, _chips_exchange("gather_chips_a", to_send[0], True))
    a1, *arrived = _ffn_up("ffn1_up", x, w["ffn1_w_in"], 2 * tm, exchange=(to_send[1], True))
    w.update(_gathered("b", GATHER_GROUPS[1], own, arrived))
    h1, z0, *arrived = _ffn_down("ffn1_down", x, a1, w["ffn1_w_out"], lng[0], lnb[0], 2 * tm,
                                 exchange=(to_send[2], True))
    w.update(_gathered("c", GATHER_GROUPS[2], own, arrived))

    w_in = w["w_in"]
    o_lat, o_kpe, o_gate = 2 * qk + 2 * rv, 2 * qk + 2 * rv + Q_LORA + KV_LORA, 2 * qk + 2 * rv + Q_LORA + KV_LORA + MLA_ROPE
    w_r, w_c = w_in[:, :o_lat], w_in[:, o_lat:o_kpe]
    w_kpe = jnp.pad(w_in[:, o_kpe:o_gate], ((0, 0), (0, LANES - MLA_ROPE)))
    w_g = w_in[:, o_gate:]
    w_uq = jnp.pad(w["w_uq"].reshape(Q_LORA, H, MLA_NOPE + MLA_ROPE),
                   ((0, 0), (0, 0), (0, MLA_QK - MLA_NOPE - MLA_ROPE))).reshape(Q_LORA, H * MLA_QK)
    w_ukv = w["w_ukv"].reshape(KV_LORA, H, MLA_NOPE + MLA_DV)
    w_uk = w_ukv[:, :, :MLA_NOPE].reshape(KV_LORA, H * MLA_NOPE)
    w_uv = w_ukv[:, :, MLA_NOPE:].reshape(KV_LORA, H * MLA_DV)

    rq, rk, rvv, rg = _proj_ret(h1, w_r, cos_r, sin_r, 2 * tm)
    lat, gates, q, k, v, latn, qt, kt, vt = _proj_mla(h1, tabs, w_c, w_kpe, w_g, w_uq, w_uk, w_uv, qn_g, kvn_g, 2 * tm)
    y = _ret_fwd(rq, rk, rvv, lgam)
    o, lse_rows, *arrived = _attn_fwd(k, qt, vt, exchange=to_send[3])
    w.update(_gathered("d", GATHER_GROUPS[3], own, arrived))
    h2, z1, yret, ymla, yr, mix = _mix_fwd(y, rg, o, gates, h1, gn_g, w["w_ret_o"], w["w_mla_o"], w["w_out"],
                                           lng[1], lnb[1], 2 * tm)
    h3, z2, a2 = _ffn_fwd("ffn2_fwd", h2, w["ffn2_w_in"], w["ffn2_w_out"], lng[2], lnb[2], 2 * tm)

    dh3, dgp, dpp, loss, dg3, db3 = _ple_loss(h3, p, target, w["ple_w_gate"], w["ple_w_proj"], lng[3], lnb[3], 2 * tm)
    dh2, da2, s2, df2, dg2, db2 = _ffn_bwd("ffn2_bwd", dh3, z2, a2, w["ffn2_w_in"], w["ffn2_w_out"], lng[2], tm)
    grads = {"ple_w_gate": _mm_tn("wg_ple_gate", h3, dgp), "ple_w_proj": _mm_tn("wg_ple_proj", p, dpp),
             "ffn2_w_in": _mm_tn("wg_ffn2_in", h2, da2, n_split=N_CHIPS), "ffn2_w_out": _mm_tn("wg_ffn2_out", s2, df2)}
    sums1 = _chip_sums("1", REDUCE_GROUPS[0], grads)
    (dz1, dgates, drg, dy, do, dyret, dymla, dg1, db1, dgn, dot_, delta_rows, *parts1) = _mix_bwd(
        dh2, z1, gates, yret, ymla, y, rg, o, gn_g, w["w_ret_o"], w["w_mla_o"], w["w_out"], lng[1], tm,
        exchange=(sums1, False))
    drq = _ret_bwd_q(rq, rk, rvv, dy, lgam)
    drk, drv = _ret_bwd_kv(rq, rk, rvv, dy, lgam)
    dk, dv, dqt = _attn_bwd(q, k, v, do, qt, kt, dot_, lse_rows, delta_rows)
    dlat, dkpe, dqb, dkv, dqg, dkg = _proj_mla_bwd(dqt, dk, dv, lat, tabs, w_uq, w_uk, w_uv, qn_g, kvn_g, 2 * tm)
    dh1, dpr = _proj_bwd(drq, drk, drv, drg, dz1, dlat, dkpe, dgates, cos_r, sin_r, w_r, w_c, w_kpe, w_g, tm)
    g_uq = _mm_tn("wg_uq", latn[:, :Q_LORA], dqb).reshape(Q_LORA, H, MLA_QK)[:, :, :MLA_NOPE + MLA_ROPE]
    g_ukv = _mm_tn("wg_ukv", latn[:, Q_LORA:], dkv)
    g_uk = g_ukv[:, :H * MLA_NOPE].reshape(KV_LORA, H, MLA_NOPE)
    g_uv = g_ukv[:, H * MLA_NOPE:].reshape(KV_LORA, H, MLA_DV)
    g_in = _mm_tn("wg_in", h1, dpr)
    grads.update({
        "w_in": jnp.concatenate([g_in[:, :o_kpe + MLA_ROPE], g_in[:, o_kpe + LANES:o_kpe + LANES + 2 * D]], axis=1),
        "w_ret_o": _mm_tn("wg_ret_o", yr, dyret),
        "w_uq": g_uq.reshape(Q_LORA, H * (MLA_NOPE + MLA_ROPE)),
        "w_ukv": jnp.concatenate([g_uk, g_uv], axis=2).reshape(KV_LORA, H * (MLA_NOPE + MLA_DV)),
        "w_mla_o": _mm_tn("wg_mla_o", o, dymla),
        "w_out": _mm_tn("wg_out", mix, dz1)})
    sums2 = _chip_sums("2", REDUCE_GROUPS[1], grads)
    dx, da1, s1, df1, dg0, db0, *parts2 = _ffn_bwd("ffn1_bwd", dh1, z0, a1, w["ffn1_w_in"], w["ffn1_w_out"], lng[0], tm,
                                                   exchange=(sums2, False))
    grads["ffn1_w_in"] = _mm_tn("wg_ffn1_in", x, da1, n_split=N_CHIPS)
    sums3 = _chip_sums("3", REDUCE_GROUPS[2], grads)
    grads["ffn1_w_out"], *parts3 = _mm_tn("wg_ffn1_out", s1, df1, exchange=sums3)
    sums4 = _chip_sums("4", REDUCE_GROUPS[3], grads)
    parts4 = _chips_exchange("reduce_chips_4", sums4, False)

    names = [n for group in REDUCE_GROUPS for n in group]
    totals = _block_totals(names, sums1 + sums2 + sums3 + sums4,
                           list(parts1) + list(parts2) + list(parts3) + list(parts4))
    others = _sibling_swap("reduce_join", totals, "whole")
    reduced = {n: _by_core(t, o_, 0) for n, t, o_ in zip(names, totals, others)}
    small = {"ln_g": jnp.concatenate([dg0, dg1, dg2, dg3], axis=0), "ln_b": jnp.concatenate([db0, db1, db2, db3], axis=0),
             "ret_gn_g": dgn, "q_norm_g": dqg, "kv_norm_g": dkg}
    return loss[0, 0], dx, reduced, small


def kernel(x, p, positions, ln_g, ln_b, ffn1_w_in, ffn1_w_out, w_in, ret_gn_g, w_ret_o, q_norm_g, kv_norm_g, w_uq, w_ukv, w_mla_o, w_out, ffn2_w_in, ffn2_w_out, ple_w_gate, ple_w_proj, loss_target, m_ln_g, m_ln_b, m_ffn1_w_in, m_ffn1_w_out, m_w_in, m_ret_gn_g, m_w_ret_o, m_q_norm_g, m_kv_norm_g, m_w_uq, m_w_ukv, m_w_mla_o, m_w_out, m_ffn2_w_in, m_ffn2_w_out, m_ple_w_gate, m_ple_w_proj, v_ln_g, v_ln_b, v_ffn1_w_in, v_ffn1_w_out, v_w_in, v_ret_gn_g, v_w_ret_o, v_q_norm_g, v_kv_norm_g, v_w_uq, v_w_ukv, v_w_mla_o, v_w_out, v_ffn2_w_in, v_ffn2_w_out, v_ple_w_gate, v_ple_w_proj):
    names = ("ln_g", "ln_b", "ffn1_w_in", "ffn1_w_out", "w_in", "ret_gn_g", "w_ret_o", "q_norm_g", "kv_norm_g", "w_uq",
             "w_ukv", "w_mla_o", "w_out", "ffn2_w_in", "ffn2_w_out", "ple_w_gate", "ple_w_proj")
    weights = dict(zip(names, (ln_g, ln_b, ffn1_w_in, ffn1_w_out, w_in, ret_gn_g, w_ret_o, q_norm_g, kv_norm_g, w_uq,
                               w_ukv, w_mla_o, w_out, ffn2_w_in, ffn2_w_out, ple_w_gate, ple_w_proj)))
    m_in = dict(zip(names, (m_ln_g, m_ln_b, m_ffn1_w_in, m_ffn1_w_out, m_w_in, m_ret_gn_g, m_w_ret_o, m_q_norm_g,
                            m_kv_norm_g, m_w_uq, m_w_ukv, m_w_mla_o, m_w_out, m_ffn2_w_in, m_ffn2_w_out, m_ple_w_gate,
                            m_ple_w_proj)))
    v_in = dict(zip(names, (v_ln_g, v_ln_b, v_ffn1_w_in, v_ffn1_w_out, v_w_in, v_ret_gn_g, v_w_ret_o, v_q_norm_g,
                            v_kv_norm_g, v_w_uq, v_w_ukv, v_w_mla_o, v_w_out, v_ffn2_w_in, v_ffn2_w_out, v_ple_w_gate,
                            v_ple_w_proj)))
    chip = 2 * lax.axis_index("x") + lax.axis_index("y")
    D = x.shape[-1]
    dq = D // N_CHIPS

    shards = {n: weights[n][0] for n in BIG_WEIGHTS}
    ln_all = _all_devices("gather_ln", jnp.concatenate([ln_g[0], ln_b[0]], axis=0), False)
    ln_full = ln_all[::2].transpose(1, 0, 2).reshape(2 * N_LN, D)
    loss, dx, big, small = _local_step(x[0], p[0, 0], positions, loss_target[0], shards, ln_full[:N_LN],
                                       ln_full[N_LN:], ret_gn_g, q_norm_g, kv_norm_g)

    loss = lax.psum(loss, ("x", "y", "c"))
    small_names = ("ln_g", "ln_b", "ret_gn_g", "q_norm_g", "kv_norm_g")
    flat = jnp.concatenate([small[n].reshape(-1) for n in small_names])
    rows = -(-flat.shape[0] // LANES // 8) * 8
    flat = jnp.pad(flat, (0, rows * LANES - flat.shape[0])).reshape(rows, LANES)
    flat = _all_devices("reduce_small", flat, True).reshape(-1)
    off = 0
    for n in small_names:
        size = small[n].size
        small[n] = flat[off:off + size].reshape(small[n].shape)
        off += size
    g_out = dict(big)
    for n in ("ln_g", "ln_b"):
        g_out[n] = lax.dynamic_slice_in_dim(small[n], chip * dq, dq, axis=1)
    for n in ("ret_gn_g", "q_norm_g", "kv_norm_g"):
        g_out[n] = small[n]

    deltas, new_m, new_v = {}, {}, {}
    for n in names:
        g = g_out[n].reshape(weights[n].shape)
        g_out[n] = g
        deltas[n], new_m[n], new_v[n] = _adamw("adamw_" + n, weights[n], g, m_in[n], v_in[n])
    return (loss, dx[None], *[g_out[n] for n in names], *[deltas[n] for n in names], *[new_m[n] for n in names],
            *[new_v[n] for n in names])
```

```python
import functools

import jax
import jax.numpy as jnp
from jax import lax
from jax.experimental import pallas as pl
from jax.experimental.pallas import tpu as pltpu

CHUNK = 64
RET_HEADS = 8
RET_DK = 128
RET_DV = 256
MLA_HEADS = 8
MLA_NOPE = 128
MLA_ROPE = 64
MLA_DV = 128
MLA_QK = 256
Q_LORA = 256
KV_LORA = 256
ROPE_BASE = 10000.0
EPS = 1e-5
N_LN = 4
ALPHA = 2.0 ** 0.25
ADAM_LR = 0.001
ADAM_B1 = 0.9
ADAM_B2 = 0.999
ADAM_EPS = 1e-08
ADAM_WD = 0.01
ADAM_STEP = 10

LANES = 128
VMEM_LIMIT = 60 << 20
N_CHIPS = 4

F32 = jnp.float32
BF16 = jnp.bfloat16
MESH = pl.DeviceIdType.MESH
HBM_SPEC = pl.BlockSpec(memory_space=pltpu.HBM)
VMEM_SPEC = pl.BlockSpec(memory_space=pltpu.VMEM)

BIG_WEIGHTS = ("ffn1_w_in", "ffn1_w_out", "w_in", "w_ret_o", "w_uq", "w_ukv", "w_mla_o", "w_out",
               "ffn2_w_in", "ffn2_w_out", "ple_w_gate", "ple_w_proj")
COL_SHARDED = ("ffn1_w_in", "w_in", "w_uq", "w_ukv", "ffn2_w_in", "ple_w_proj")


def _dot(a, b):
    return jnp.dot(a, b, preferred_element_type=F32)


def _dot_nt(a, b):
    return lax.dot_general(a, b, (((1,), (1,)), ((), ())), preferred_element_type=F32)


def _dot_tn(a, b):
    return lax.dot_general(a, b, (((0,), (0,)), ((), ())), preferred_element_type=F32)


def _bf(x):
    return x.astype(BF16)


def _sigmoid(x):
    return 0.5 * jnp.tanh(0.5 * x) + 0.5


def _mean(x):
    return jnp.mean(x, axis=-1, keepdims=True)


def _ln_stats(z):
    zc = z - _mean(z)
    rstd = lax.rsqrt(_mean(zc * zc) + EPS)
    return zc * rstd, rstd


def _ln_bwd(dy, xhat, rstd, g):
    dxhat = dy * g
    dz = rstd * (dxhat - _mean(dxhat) - xhat * _mean(dxhat * xhat))
    return dz, jnp.sum(dy * xhat, axis=0, keepdims=True), jnp.sum(dy, axis=0, keepdims=True)


def _roll(x, shift):
    return pltpu.roll(x, shift, 1)


def _chunk_of(idx):
    return jnp.right_shift(idx, CHUNK.bit_length() - 1)


def _tile(n, cap, mult=LANES):
    if n <= cap:
        return n
    for t in range(cap - cap % mult, 0, -mult):
        if n % t == 0:
            return t
    return n


def _zero_map(nd, *_):
    return (0,) * nd


def _params(sem):
    return pltpu.CompilerParams(dimension_semantics=sem, vmem_limit_bytes=VMEM_LIMIT)


def _rowcall(name, body, n_rows, tm, row_ins, full_ins, row_outs, acc_outs=(), tiled_outs=(), tiled_ins=(),
             exchange=None):
    n_steps = n_rows // tm
    ex_srcs, broadcast = exchange if exchange else ((), False)
    n_ex = len(ex_srcs)
    n_in = len(row_ins) + len(tiled_ins) + len(full_ins)
    n_out = len(row_outs) + len(acc_outs) + len(tiled_outs)

    def kern(*refs):
        step = pl.program_id(0)
        ex_in, ex_out = refs[n_in:n_in + n_ex], refs[n_in + n_ex + n_out:n_in + 2 * n_ex + n_out]
        sems = refs[n_in + 2 * n_ex + n_out:]
        if n_ex:
            @pl.when(step == 0)
            def _():
                for send, _ in _chip_copies(ex_in, ex_out, *sems, broadcast):
                    send.start()

        body(step, *refs[:n_in], *refs[n_in + n_ex:n_in + n_ex + n_out])
        if n_ex:
            @pl.when(step == n_steps - 1)
            def _():
                _wait_copies(_chip_copies(ex_in, ex_out, *sems, broadcast))

    in_specs = [pl.BlockSpec((tm, a.shape[1]), lambda i: (i, 0)) for a in row_ins]
    in_specs += [spec for (_, spec) in tiled_ins]
    row_ins = list(row_ins) + [a for (a, _) in tiled_ins]
    in_specs += [pl.BlockSpec(a.shape, functools.partial(_zero_map, a.ndim), pipeline_mode=pl.Buffered(1))
                 for a in full_ins]
    in_specs += [HBM_SPEC] * n_ex
    out_specs = [pl.BlockSpec((tm, w), lambda i: (i, 0)) for (w, _) in row_outs]
    out_specs += [pl.BlockSpec(s, functools.partial(_zero_map, len(s))) for (s, _) in acc_outs]
    out_specs += [spec for (_, spec) in tiled_outs]
    out_specs += [HBM_SPEC] * n_ex
    out_shape = [jax.ShapeDtypeStruct((n_rows, w), dt) for (w, dt) in row_outs]
    out_shape += [jax.ShapeDtypeStruct(s, dt) for (s, dt) in acc_outs]
    out_shape += [shape for (shape, _) in tiled_outs]
    out_shape += _exchange_shapes(ex_srcs)
    return pl.pallas_call(kern, grid=(n_steps,), in_specs=in_specs, out_specs=out_specs, out_shape=out_shape,
                          scratch_shapes=_dma_sems(n_ex * N_PEER_CHIPS) if n_ex else [], name=name,
                          compiler_params=_params(("arbitrary",)))(*row_ins, *full_ins, *ex_srcs)


def _acc(step, ref, val):
    @pl.when(step == 0)
    def _():
        ref[...] = val

    @pl.when(step != 0)
    def _():
        ref[...] += val


def _ffn_fwd(name, x, w_in4, w_out, ln_g, ln_b, tm, exchange=None):
    T, D = x.shape
    fh = w_in4.shape[2]

    def body(i, x_ref, w4_ref, wo_ref, g_ref, b_ref, h_ref, z_ref, a_ref):
        xv = x_ref[...]
        xb = _bf(xv)
        f = jnp.zeros((tm, D), F32)
        for k in range(2):
            gk = _dot(xb, w4_ref[k])
            uk = _dot(xb, w4_ref[2 + k])
            a_ref[:, k * fh:(k + 1) * fh] = _bf(gk)
            a_ref[:, (2 + k) * fh:(3 + k) * fh] = _bf(uk)
            f += _dot(_bf(gk * _sigmoid(gk) * uk), wo_ref[k * fh:(k + 1) * fh, :])
        z = ALPHA * xv + 0.5 * f
        xhat, _ = _ln_stats(z)
        z_ref[...] = z
        h_ref[...] = xhat * g_ref[...] + b_ref[...]

    return _rowcall(name, body, T, tm, [x], [w_in4, w_out, ln_g, ln_b],
                    [(D, F32), (D, F32), (4 * fh, BF16)], exchange=exchange)


def _ffn_bwd(name, dh, z, a, w_in4, w_out, ln_g, tm, exchange=None):
    T, D = dh.shape
    fh = w_in4.shape[2]

    def body(i, dh_ref, z_ref, a_ref, w4_ref, wo_ref, g_ref, dx_ref, da_ref, s_ref, df_ref, dg_ref, db_ref):
        xhat, rstd = _ln_stats(z_ref[...])
        dz, dg, db = _ln_bwd(dh_ref[...], xhat, rstd, g_ref[...])
        _acc(i, dg_ref, dg)
        _acc(i, db_ref, db)
        dfb = _bf(0.5 * dz)
        df_ref[...] = dfb
        dx = ALPHA * dz
        for k in range(2):
            gk = a_ref[:, k * fh:(k + 1) * fh].astype(F32)
            uk = a_ref[:, (2 + k) * fh:(3 + k) * fh].astype(F32)
            ds = _dot_nt(dfb, wo_ref[k * fh:(k + 1) * fh, :])
            sig = _sigmoid(gk)
            silu = gk * sig
            dgk = _bf(ds * uk * sig * (1.0 + gk * (1.0 - sig)))
            duk = _bf(ds * silu)
            s_ref[:, k * fh:(k + 1) * fh] = _bf(silu * uk)
            da_ref[:, k * fh:(k + 1) * fh] = dgk
            da_ref[:, (2 + k) * fh:(3 + k) * fh] = duk
            dx += _dot_nt(dgk, w4_ref[k]) + _dot_nt(duk, w4_ref[2 + k])
        dx_ref[...] = dx

    return _rowcall(name, body, T, tm, [dh, z, a], [w_in4, w_out, ln_g],
                    [(D, F32), (4 * fh, BF16), (2 * fh, BF16), (D, BF16)],
                    [((1, D), F32), ((1, D), F32)], exchange=exchange)


WG_TILE_N = 1536


def _mm_tn(name, a, b, out_dtype=BF16, n_split=1, exchange=()):
    T, M = a.shape
    N = b.shape[1]
    tk = _tile(T, 2048, 8)
    tm = _tile(M, 1408)
    tn = _tile(N // n_split, WG_TILE_N)
    per = N // n_split // tn
    nk = T // tk
    n_ex = len(exchange)
    grid = (M // tm, N // tn, nk)
    if n_split > 1:
        out_spec = pl.BlockSpec((None, tm, tn), lambda i, j, k: (j // per, i, j % per))
        out_shape = jax.ShapeDtypeStruct((n_split, M, N // n_split), out_dtype)
    else:
        out_spec = pl.BlockSpec((tm, tn), lambda i, j, k: (i, j))
        out_shape = jax.ShapeDtypeStruct((M, N), out_dtype)

    def kern(a_ref, b_ref, *refs):
        ex_in, o_ref, ex_out = refs[:n_ex], refs[n_ex], refs[n_ex + 1:2 * n_ex + 1]
        acc_ref, sems = refs[2 * n_ex + 1], refs[2 * n_ex + 2:]
        k = pl.program_id(2)
        at_end = [pl.program_id(d) == grid[d] - 1 for d in range(3)]
        if n_ex:
            @pl.when(jnp.logical_and(jnp.logical_and(pl.program_id(0) == 0, pl.program_id(1) == 0), k == 0))
            def _():
                for send, _ in _chip_copies(ex_in, ex_out, *sems, False):
                    send.start()

        part = _dot_tn(_bf(a_ref[...]), _bf(b_ref[...]))

        @pl.when(k == 0)
        def _():
            acc_ref[...] = part

        @pl.when(k != 0)
        def _():
            acc_ref[...] += part

        @pl.when(k == nk - 1)
        def _():
            o_ref[...] = acc_ref[...].astype(out_dtype)

        if n_ex:
            @pl.when(jnp.logical_and(jnp.logical_and(at_end[0], at_end[1]), at_end[2]))
            def _():
                _wait_copies(_chip_copies(ex_in, ex_out, *sems, False))

    outs = pl.pallas_call(
        kern, grid=grid,
        in_specs=[pl.BlockSpec((tk, tm), lambda i, j, k: (k, i)), pl.BlockSpec((tk, tn), lambda i, j, k: (k, j))]
        + [HBM_SPEC] * n_ex,
        out_specs=[out_spec] + [HBM_SPEC] * n_ex, out_shape=[out_shape] + _exchange_shapes(exchange),
        scratch_shapes=[pltpu.VMEM((tm, tn), F32)] + (_dma_sems(n_ex * N_PEER_CHIPS) if n_ex else []), name=name,
        compiler_params=_params(("arbitrary", "arbitrary", "arbitrary")))(a, b, *exchange)
    return outs if n_ex else outs[0]


def _proj_ret(h1, w_r, cos_r, sin_r, tm):
    T, D = h1.shape
    qk = RET_HEADS * RET_DK
    rv = RET_HEADS * RET_DV

    def body(i, h_ref, cos_ref, sin_ref, w_ref, q_ref, k_ref, v_ref, g_ref):
        hb = _bf(h_ref[...])
        cos, sin = cos_ref[...], sin_ref[...]
        for out_ref, off, scale in ((q_ref, 0, 1.0), (k_ref, qk, RET_DK ** -0.5)):
            pr = _dot(hb, w_ref[:, off:off + qk])
            for h in range(RET_HEADS):
                t = pr[:, h * RET_DK:(h + 1) * RET_DK]
                out_ref[:, h * RET_DK:(h + 1) * RET_DK] = _bf((t * cos + _roll(t, RET_DK // 2) * sin) * scale)
        v_ref[...] = _bf(_dot(hb, w_ref[:, 2 * qk:2 * qk + rv]))
        g_ref[...] = _bf(_dot(hb, w_ref[:, 2 * qk + rv:2 * qk + 2 * rv]))

    return _rowcall("proj_ret", body, T, tm, [h1, cos_r, sin_r], [w_r],
                    [(qk, BF16), (qk, BF16), (rv, BF16), (rv, BF16)])


def _rope_pe(t, c, s1, s2):
    return t * c + _roll(t, LANES - MLA_ROPE // 2) * s1 + _roll(t, MLA_ROPE // 2) * s2


def _rope_pe_bwd(dy, c, s1, s2):
    return dy * c + _roll(dy * s1, MLA_ROPE // 2) + _roll(dy * s2, LANES - MLA_ROPE // 2)


def _rms(x, g):
    r = lax.rsqrt(_mean(x * x) + EPS)
    return x * r, r


def _attn_block(T):
    return min(512, T)


def _transposed_blocks(T, tm, w, dtype):
    tb = _attn_block(T)
    per = tb // tm
    return (jax.ShapeDtypeStruct((T // tb, MLA_HEADS, w, tb), dtype),
            pl.BlockSpec((None, MLA_HEADS, w, tm), lambda i: (i // per, 0, 0, i % per)))


ATTN_SCALE = (MLA_NOPE + MLA_ROPE) ** -0.5
LOG2E = 1.4426950408889634
Q_PRESCALE = ATTN_SCALE * LOG2E
V_ONES = 16


def _proj_mla(h1, tabs, w_c, w_kpe, w_g, w_uq, w_uk, w_uv, qn_g, kvn_g, tm):
    T, D = h1.shape
    H = MLA_HEADS

    def body(i, h_ref, c_ref, s1_ref, s2_ref, wc_ref, wk_ref, wg_ref, wuq_ref, wuk_ref, wuv_ref, qg_ref, kg_ref,
             lat_ref, gt_ref, q_ref, k_ref, v_ref, ln_ref, qt_ref, kt_ref, vt_ref):
        hb = _bf(h_ref[...])
        c, s1, s2 = c_ref[...], s1_ref[...], s2_ref[...]
        lat = _dot(hb, wc_ref[...])
        lat_ref[...] = lat
        gt_ref[...] = _bf(_dot(hb, wg_ref[...]))
        cqn, _ = _rms(lat[:, :Q_LORA], None)
        ckn, _ = _rms(lat[:, Q_LORA:], None)
        cqn = _bf(cqn * qg_ref[...])
        ckn = _bf(ckn * kg_ref[...])
        ln_ref[:, :Q_LORA] = cqn
        ln_ref[:, Q_LORA:] = ckn
        q = _dot(cqn, wuq_ref[...])
        kn = _dot(ckn, wuk_ref[...])
        vv = _dot(ckn, wuv_ref[...])
        v_ref[...] = _bf(vv)
        kpe = _rope_pe(_dot(hb, wk_ref[...]), c, s1, s2)
        ones = jnp.ones((V_ONES, tm), BF16)
        for h in range(H):
            o = h * MLA_QK
            qh = jnp.concatenate([q[:, o:o + MLA_NOPE], _rope_pe(q[:, o + MLA_NOPE:o + MLA_QK], c, s1, s2)], axis=1)
            qh = qh * Q_PRESCALE
            kh = jnp.concatenate([kn[:, h * MLA_NOPE:(h + 1) * MLA_NOPE], kpe], axis=1)
            q_ref[:, o:o + MLA_QK] = _bf(qh)
            k_ref[:, o:o + MLA_QK] = _bf(kh)
            qt_ref[h] = _bf(qh.T)
            kt_ref[h] = _bf(kh.T)
            vt_ref[h] = jnp.concatenate([_bf(vv[:, h * MLA_DV:(h + 1) * MLA_DV].T), ones], axis=0)

    lat_w = Q_LORA + KV_LORA
    return _rowcall("proj_mla", body, T, tm, [h1, *tabs], [w_c, w_kpe, w_g, w_uq, w_uk, w_uv, qn_g, kvn_g],
                    [(lat_w, F32), (2 * D, BF16), (H * MLA_QK, BF16), (H * MLA_QK, BF16), (H * MLA_DV, BF16),
                     (lat_w, BF16)],
                    tiled_outs=[_transposed_blocks(T, tm, MLA_QK, BF16), _transposed_blocks(T, tm, MLA_QK, BF16),
                                _transposed_blocks(T, tm, MLA_DV + V_ONES, BF16)])


def _ret_block(T):
    return min(256, T)


RET_HEADS_PER_STEP = 8


def _ret_dmat(lg, bt):
    n = lax.broadcasted_iota(jnp.int32, (bt, bt), 0)
    m = lax.broadcasted_iota(jnp.int32, (bt, bt), 1)
    return jnp.where(_chunk_of(m) <= _chunk_of(n), jnp.exp(lg * jnp.abs(n - m).astype(F32)), 0.0)


def _ret_scan(name, per_head, lgam, ins, outs, rev):
    T = ins[0][0].shape[0]
    bt = _ret_block(T)
    nb = T // bt
    hps = min(RET_HEADS_PER_STEP, RET_HEADS)
    n_in, n_out = len(ins), len(outs)

    def kern(lg_ref, *refs):
        in_refs, out_refs = refs[:n_in], refs[n_in:n_in + n_out]
        state_ref, dmat_ref = refs[n_in + n_out:]

        @pl.when(pl.program_id(1) == 0)
        def _():
            state_ref[...] = jnp.zeros_like(state_ref)
            for hh in range(hps):
                dmat_ref[hh] = _ret_dmat(lg_ref[hh][:, :1], bt)

        pos = lax.broadcasted_iota(jnp.int32, (bt, 1), 0).astype(F32)
        for hh in range(hps):
            lg = lg_ref[hh][:, :1]
            xi, zeta, gb = jnp.exp(lg * (pos + 1.0)), jnp.exp(lg * (bt - 1.0 - pos)), jnp.exp(lg * bt)
            tiles = [r[:, hh * w:(hh + 1) * w] for r, (_, w) in zip(in_refs, ins)]
            res = per_head(dmat_ref[hh], xi, zeta, gb, state_ref.at[hh], *tiles)
            for o_ref, (w, _), val in zip(out_refs, outs, res):
                o_ref[:, hh * w:(hh + 1) * w] = val.astype(o_ref.dtype)

    def blk(w):
        if rev:
            return pl.BlockSpec((bt, hps * w), lambda g, b: (nb - 1 - b, g))
        return pl.BlockSpec((bt, hps * w), lambda g, b: (b, g))

    return pl.pallas_call(
        kern, grid=(RET_HEADS // hps, nb),
        in_specs=[pl.BlockSpec((hps, 1, LANES), lambda g, b: (g, 0, 0))] + [blk(w) for _, w in ins],
        out_specs=[blk(w) for w, _ in outs],
        out_shape=[jax.ShapeDtypeStruct((T, RET_HEADS * w), dt) for w, dt in outs],
        scratch_shapes=[pltpu.VMEM((hps, RET_DK, RET_DV), F32), pltpu.VMEM((hps, bt, bt), F32)], name=name,
        compiler_params=_params(("arbitrary", "arbitrary")))(lgam, *[a for a, _ in ins])


def _ret_fwd(rq, rk, rv, lgam):
    def per_head(dmat, xi, zeta, gb, s_ref, q, k, v):
        sc = _dot_nt(q, k) * dmat
        y = _dot(_bf(sc), v) + _dot(q, _bf(s_ref[...])) * xi
        s_ref[...] = s_ref[...] * gb + _dot_tn(_bf(k.astype(F32) * zeta), v)
        return (y,)

    return _ret_scan("ret_fwd", per_head, lgam, [(rq, RET_DK), (rk, RET_DK), (rv, RET_DV)], [(RET_DV, BF16)], False)[0]


def _ret_bwd_q(rq, rk, rv, dy, lgam):
    def per_head(dmat, xi, zeta, gb, s_ref, k, v, dy):
        dp = _dot_nt(dy, v) * dmat
        dq = _dot(_bf(dp), k) + _dot_nt(dy, _bf(s_ref[...])) * xi
        s_ref[...] = s_ref[...] * gb + _dot_tn(_bf(k.astype(F32) * zeta), v)
        return (dq,)

    return _ret_scan("ret_bwd_q", per_head, lgam, [(rk, RET_DK), (rv, RET_DV), (dy, RET_DV)], [(RET_DK, F32)], False)[0]


def _ret_bwd_kv(rq, rk, rv, dy, lgam):
    def per_head(dmat, xi, zeta, gb, g_ref, q, k, v, dy):
        gs = _bf(g_ref[...])
        p = _dot_nt(q, k) * dmat
        dp = _dot_nt(dy, v) * dmat
        dv = _dot_tn(_bf(p), dy) + _dot(k, gs) * zeta
        dk = _dot_tn(_bf(dp), q) + _dot_nt(v, gs) * zeta
        g_ref[...] = g_ref[...] * gb + _dot_tn(_bf(q.astype(F32) * xi), dy)
        return dk, dv

    return _ret_scan("ret_bwd_kv", per_head, lgam, [(rq, RET_DK), (rk, RET_DK), (rv, RET_DV), (dy, RET_DV)],
                     [(RET_DK, F32), (RET_DV, BF16)], True)


def _attn_mask_t(tb):
    key = lax.broadcasted_iota(jnp.int32, (tb, tb), 0)
    qry = lax.broadcasted_iota(jnp.int32, (tb, tb), 1)
    return _chunk_of(key) <= _chunk_of(qry)


MASKED = -1e30
SUBLANES = 8


def _head_blocks(nb, w, tb):
    return pl.BlockSpec((nb, None, w, tb), lambda h, i: (0, h, 0, 0))


def _one_block(w, tb):
    return pl.BlockSpec((None, None, w, tb), lambda h, i: (i, h, 0, 0))


def _attn_fwd(k, qt, vt, exchange=()):
    T = k.shape[0]
    tb = _attn_block(T)
    nb = T // tb

    n_ex = len(exchange)

    def kern(qt_ref, k_ref, vt_ref, *refs):
        ex_in, (o_ref, lser_ref), ex_out = refs[:n_ex], refs[n_ex:n_ex + 2], refs[n_ex + 2:2 * n_ex + 2]
        m_ref, acc_ref, sa_ref, sb_ref = refs[2 * n_ex + 2:2 * n_ex + 6]
        sems = refs[2 * n_ex + 6:]
        qb = pl.program_id(1)
        first = jnp.logical_and(pl.program_id(0) == 0, qb == 0)
        last = jnp.logical_and(pl.program_id(0) == MLA_HEADS - 1, qb == nb - 1)
        if n_ex:
            @pl.when(first)
            def _():
                for send, _ in _chip_copies(ex_in, ex_out, *sems, True):
                    send.start()

        qt = qt_ref[...]
        m_ref[...] = jnp.full_like(m_ref, MASKED)
        acc_ref[...] = jnp.zeros_like(acc_ref)

        def scores(kb):
            rows = pl.ds(pl.multiple_of(kb * tb, tb), tb)
            return _dot(k_ref[rows, :], qt)

        def update(s, kb):
            m_old = m_ref[...]
            m_new = jnp.maximum(m_old, jnp.max(s, axis=0, keepdims=True))
            p = jnp.exp2(s - m_new)
            acc_ref[...] = acc_ref[...] * jnp.exp2(m_old - m_new) + _dot(vt_ref[kb], _bf(p))
            m_ref[...] = m_new

        def masked(s):
            return jnp.where(_attn_mask_t(tb), s, MASKED)

        sa_ref[...] = scores(0)

        def pair_body(j, carry):
            sb_ref[...] = scores(2 * j + 1)
            update(sa_ref[...], 2 * j)
            sa_ref[...] = scores(2 * j + 2)
            update(sb_ref[...], 2 * j + 1)
            return carry

        lax.fori_loop(0, qb // 2, pair_body, 0)

        @pl.when(qb % 2 == 0)
        def _():
            update(masked(sa_ref[...]), qb)

        @pl.when(qb % 2 == 1)
        def _():
            sb_ref[...] = masked(scores(qb))
            update(sa_ref[...], qb - 1)
            update(sb_ref[...], qb)

        l = acc_ref[MLA_DV:MLA_DV + 1, :]
        o_ref[...] = _bf((acc_ref[:MLA_DV, :] / l).T)
        lser_ref[...] = jnp.broadcast_to(m_ref[...] + jnp.log2(l), (SUBLANES, tb))
        if n_ex:
            @pl.when(last)
            def _():
                _wait_copies(_chip_copies(ex_in, ex_out, *sems, True))

    return pl.pallas_call(
        kern, grid=(MLA_HEADS, nb),
        in_specs=[_one_block(MLA_QK, tb), pl.BlockSpec((T, MLA_QK), lambda h, i: (0, h)),
                  _head_blocks(nb, MLA_DV + V_ONES, tb)] + [HBM_SPEC] * n_ex,
        out_specs=[pl.BlockSpec((tb, MLA_DV), lambda h, i: (i, h)), _one_block(SUBLANES, tb)] + [HBM_SPEC] * n_ex,
        out_shape=[jax.ShapeDtypeStruct((T, MLA_HEADS * MLA_DV), BF16),
                   jax.ShapeDtypeStruct((nb, MLA_HEADS, SUBLANES, tb), F32)] + _exchange_shapes(exchange),
        scratch_shapes=[pltpu.VMEM((1, tb), F32), pltpu.VMEM((MLA_DV + V_ONES, tb), F32),
                        pltpu.VMEM((tb, tb), F32), pltpu.VMEM((tb, tb), F32)]
        + (_dma_sems(n_ex * N_PEER_CHIPS) if n_ex else []),
        name="attn_fwd", compiler_params=_params(("arbitrary", "arbitrary")))(qt, k, vt, *exchange)


def _attn_bwd(q, k, v, do, qt, kt, dot_, lse_rows, delta_rows):
    T = q.shape[0]
    tb = _attn_block(T)
    nb = T // tb

    def kern(q_ref, k_ref, v_ref, do_ref, qt_ref, kt_ref, dot_ref, lse_ref, dl_ref, dk_ref, dv_ref, dqt_ref, dv_acc,
             sa_ref, pa_ref, sb_ref, pb_ref):
        kb = pl.program_id(1)
        kv, vv, ktv = k_ref[...], v_ref[...], kt_ref[...]
        dk_ref[...] = jnp.zeros_like(dk_ref)
        dv_acc[...] = jnp.zeros_like(dv_acc)

        @pl.when(kb == 0)
        def _():
            dqt_ref[...] = jnp.zeros_like(dqt_ref)

        def products(qb, s_ref, dp_ref, diagonal=False):
            s = _dot(kv, qt_ref[qb])
            s_ref[...] = jnp.where(_attn_mask_t(tb), s, MASKED) if diagonal else s
            dp_ref[...] = _dot(vv, dot_ref[qb])

        def consume(qb, s_ref, dp_ref):
            rows = pl.ds(pl.multiple_of(qb * tb, tb), tb)
            p = jnp.exp2(s_ref[...] - lse_ref[qb][:1, :])
            dv_acc[...] += _dot(_bf(p), do_ref[rows, :])
            ds = _bf(p * (dp_ref[...] - dl_ref[qb][:1, :]))
            dk_ref[...] += _dot(ds, q_ref[rows, :])
            dqt_ref[qb] += _dot(ktv, ds)

        n_full = nb - 1 - kb
        products(kb, sa_ref, pa_ref, diagonal=True)

        def pair_body(j, carry):
            q1 = kb + 1 + 2 * j
            products(q1, sb_ref, pb_ref)
            consume(q1 - 1, sa_ref, pa_ref)
            products(q1 + 1, sa_ref, pa_ref)
            consume(q1, sb_ref, pb_ref)
            return carry

        lax.fori_loop(0, n_full // 2, pair_body, 0)

        @pl.when(n_full % 2 == 0)
        def _():
            consume(nb - 1, sa_ref, pa_ref)

        @pl.when(n_full % 2 == 1)
        def _():
            products(nb - 1, sb_ref, pb_ref)
            consume(nb - 2, sa_ref, pa_ref)
            consume(nb - 1, sb_ref, pb_ref)

        dk_ref[...] = dk_ref[...] * (ATTN_SCALE / Q_PRESCALE)
        dv_ref[...] = _bf(dv_acc[...])

    def blk(w):
        return pl.BlockSpec((tb, w), lambda h, i: (i, h))

    def full(w):
        return pl.BlockSpec((T, w), lambda h, i: (0, h))

    return pl.pallas_call(
        kern, grid=(MLA_HEADS, nb),
        in_specs=[full(MLA_QK), blk(MLA_QK), blk(MLA_DV), full(MLA_DV), _head_blocks(nb, MLA_QK, tb),
                  _one_block(MLA_QK, tb), _head_blocks(nb, MLA_DV, tb), _head_blocks(nb, SUBLANES, tb),
                  _head_blocks(nb, SUBLANES, tb)],
        out_specs=[blk(MLA_QK), blk(MLA_DV), _head_blocks(nb, MLA_QK, tb)],
        out_shape=[jax.ShapeDtypeStruct((T, MLA_HEADS * MLA_QK), F32),
                   jax.ShapeDtypeStruct((T, MLA_HEADS * MLA_DV), BF16),
                   jax.ShapeDtypeStruct((nb, MLA_HEADS, MLA_QK, tb), F32)],
        scratch_shapes=[pltpu.VMEM((tb, MLA_DV), F32)] + [pltpu.VMEM((tb, tb), F32)] * 4,
        name="attn_bwd", compiler_params=_params(("arbitrary", "arbitrary")))(
            q, k, v, do, qt, kt, dot_, lse_rows, delta_rows)


def _group_norm(y):
    yc = y - _mean(y)
    rstd = lax.rsqrt(_mean(yc * yc) + EPS)
    return yc * rstd, rstd


def _mix_fwd(y, rg, o, gates, h1, gn_g, w_ret_o, w_mla_o, w_out, ln_g, ln_b, tm):
    T, D = h1.shape

    def body(i, y_ref, rg_ref, o_ref, gt_ref, h_ref, gn_ref, wr_ref, wm_ref, wo_ref, g_ref, b_ref,
             h2_ref, z_ref, yret_ref, ymla_ref, yr_ref, mix_ref):
        for h in range(RET_HEADS):
            sl = slice(h * RET_DV, (h + 1) * RET_DV)
            yn, _ = _group_norm(y_ref[:, sl].astype(F32))
            r = rg_ref[:, sl].astype(F32)
            yr_ref[:, sl] = _bf(r * _sigmoid(r) * (yn * gn_ref[:, sl]))
        yret = _dot(yr_ref[...], wr_ref[...])
        ymla = _dot(_bf(o_ref[...]), wm_ref[...])
        yret_ref[...] = _bf(yret)
        ymla_ref[...] = _bf(ymla)
        mix = _bf(_sigmoid(gt_ref[:, :D].astype(F32)) * yret + _sigmoid(gt_ref[:, D:].astype(F32)) * ymla)
        mix_ref[...] = mix
        z = ALPHA * h_ref[...] + _dot(mix, wo_ref[...])
        xhat, _ = _ln_stats(z)
        z_ref[...] = z
        h2_ref[...] = xhat * g_ref[...] + b_ref[...]

    return _rowcall("mix_fwd", body, T, tm, [y, rg, o, gates, h1], [gn_g, w_ret_o, w_mla_o, w_out, ln_g, ln_b],
                    [(D, F32), (D, F32), (D, BF16), (D, BF16), (RET_HEADS * RET_DV, BF16), (D, BF16)])


def _mix_bwd(dh2, z1, gates, yret, ymla, y, rg, o, gn_g, w_ret_o, w_mla_o, w_out, ln_g, tm, exchange=None):
    T, D = dh2.shape
    rv = RET_HEADS * RET_DV

    def body(i, dh_ref, z_ref, gt_ref, yret_ref, ymla_ref, y_ref, rg_ref, o_ref, gn_ref, wr_ref, wm_ref, wo_ref, g_ref,
             dz_ref, dgt_ref, drg_ref, dy_ref, do_ref, dyret_ref, dymla_ref, dg_ref, db_ref, dgn_ref, dot_ref,
             dl_ref):
        xhat, rstd = _ln_stats(z_ref[...])
        dz, dg, db = _ln_bwd(dh_ref[...], xhat, rstd, g_ref[...])
        _acc(i, dg_ref, dg)
        _acc(i, db_ref, db)
        dz_ref[...] = dz
        dmix = _dot_nt(_bf(dz), wo_ref[...])
        sr = _sigmoid(gt_ref[:, :D].astype(F32))
        sm = _sigmoid(gt_ref[:, D:].astype(F32))
        dgt_ref[:, :D] = _bf(dmix * yret_ref[...].astype(F32) * sr * (1.0 - sr))
        dgt_ref[:, D:] = _bf(dmix * ymla_ref[...].astype(F32) * sm * (1.0 - sm))
        dyret = _bf(dmix * sr)
        dymla = _bf(dmix * sm)
        dyret_ref[...] = dyret
        dymla_ref[...] = dymla
        dov = _dot_nt(dymla, wm_ref[...])
        do_ref[...] = _bf(dov)
        for h in range(MLA_HEADS):
            sl = slice(h * MLA_DV, (h + 1) * MLA_DV)
            dot_ref[h] = _bf(dov[:, sl].T)
            delta = jnp.sum(dov[:, sl] * o_ref[:, sl].astype(F32), axis=-1, keepdims=True)
            dl_ref[h] = jnp.broadcast_to(delta, (tm, LANES)).T[:SUBLANES, :]
        dyr = _dot_nt(dyret, wr_ref[...])
        dgn = []
        for h in range(RET_HEADS):
            sl = slice(h * RET_DV, (h + 1) * RET_DV)
            yn, grstd = _group_norm(y_ref[:, sl].astype(F32))
            r = rg_ref[:, sl].astype(F32)
            sig = _sigmoid(r)
            d = dyr[:, sl]
            drg_ref[:, sl] = _bf(d * (yn * gn_ref[:, sl]) * sig * (1.0 + r * (1.0 - sig)))
            dt = d * (r * sig)
            dgn.append(jnp.sum(dt * yn, axis=0, keepdims=True))
            dyn = dt * gn_ref[:, sl]
            dy_ref[:, sl] = _bf(grstd * (dyn - _mean(dyn) - yn * _mean(dyn * yn)))
        _acc(i, dgn_ref, jnp.concatenate(dgn, axis=1))

    return _rowcall("mix_bwd", body, T, tm, [dh2, z1, gates, yret, ymla, y, rg, o],
                    [gn_g, w_ret_o, w_mla_o, w_out, ln_g],
                    [(D, F32), (2 * D, BF16), (rv, BF16), (rv, BF16), (MLA_HEADS * MLA_DV, BF16), (D, BF16), (D, BF16)],
                    [((1, D), F32), ((1, D), F32), ((1, rv), F32)],
                    tiled_outs=[_transposed_blocks(T, tm, MLA_DV, BF16), _transposed_blocks(T, tm, SUBLANES, F32)],
                    exchange=exchange)


def _proj_mla_bwd(dqt, dk, dv, lat, tabs, w_uq, w_uk, w_uv, qn_g, kvn_g, tm):
    T = dk.shape[0]
    H = MLA_HEADS
    lat_w = Q_LORA + KV_LORA

    def body(i, dk_ref, dv_ref, lat_ref, c_ref, s1_ref, s2_ref, dqt_ref, wuq_ref, wuk_ref, wuv_ref, qg_ref, kg_ref,
             dlat_ref, dkpe_ref, dqb_ref, dkn_ref, dqg_ref, dkg_ref):
        c, s1, s2 = c_ref[...], s1_ref[...], s2_ref[...]
        dkpe = jnp.zeros((tm, LANES), F32)
        for h in range(H):
            o = h * MLA_QK
            dqh = dqt_ref[h].T * ATTN_SCALE
            dqb_ref[:, o:o + MLA_NOPE] = _bf(dqh[:, :MLA_NOPE])
            dqb_ref[:, o + MLA_NOPE:o + MLA_QK] = _bf(_rope_pe_bwd(dqh[:, MLA_NOPE:], c, s1, s2))
            dkn_ref[:, h * MLA_NOPE:(h + 1) * MLA_NOPE] = _bf(dk_ref[:, o:o + MLA_NOPE])
            dkpe += dk_ref[:, o + MLA_NOPE:o + MLA_QK]
        dkn_ref[:, H * MLA_NOPE:] = dv_ref[...]
        dkpe_ref[...] = _bf(_rope_pe_bwd(dkpe, c, s1, s2))
        dcqn = _dot_nt(dqb_ref[...], wuq_ref[...])
        dckn = _dot_nt(dkn_ref[:, :H * MLA_NOPE], wuk_ref[...]) + _dot_nt(dv_ref[...], wuv_ref[...])
        for dn, x, g_ref, dg_ref, sl in ((dcqn, lat_ref[:, :Q_LORA], qg_ref, dqg_ref, slice(0, Q_LORA)),
                                         (dckn, lat_ref[:, Q_LORA:], kg_ref, dkg_ref, slice(Q_LORA, lat_w))):
            xn, r = _rms(x, None)
            _acc(i, dg_ref, jnp.sum(dn * xn, axis=0, keepdims=True))
            dxn = dn * g_ref[...]
            dlat_ref[:, sl] = _bf(r * (dxn - xn * _mean(dxn * xn)))

    dqt_shape, dqt_spec = _transposed_blocks(T, tm, MLA_QK, F32)
    assert dqt.shape == dqt_shape.shape
    return _rowcall("proj_mla_bwd", body, T, tm, [dk, dv, lat, *tabs], [w_uq, w_uk, w_uv, qn_g, kvn_g],
                    [(lat_w, BF16), (LANES, BF16), (H * MLA_QK, BF16), (H * (MLA_NOPE + MLA_DV), BF16)],
                    [((1, Q_LORA), F32), ((1, KV_LORA), F32)], tiled_ins=[(dqt, dqt_spec)])


def _proj_bwd(drq, drk, drv, drg, dz1, dlat, dkpe, dgates, cos_r, sin_r, w_r, w_c, w_kpe, w_g, tm):
    T, D = dz1.shape
    qk = RET_HEADS * RET_DK
    rv = RET_HEADS * RET_DV
    o_lat = 2 * qk + 2 * rv
    o_kpe = o_lat + dlat.shape[1]
    o_gate = o_kpe + LANES
    o_end = o_gate + dgates.shape[1]
    width = -(-o_end // WG_TILE_N) * WG_TILE_N

    def body(i, drq_ref, drk_ref, drv_ref, drg_ref, dz_ref, dlat_ref, dkpe_ref, dgt_ref, cos_ref, sin_ref,
             wr_ref, wc_ref, wk_ref, wg_ref, dh_ref, dpr_ref):
        cos, sin = cos_ref[...], sin_ref[...]
        for src, off, scale in ((drq_ref, 0, 1.0), (drk_ref, qk, RET_DK ** -0.5)):
            for h in range(RET_HEADS):
                d = src[:, h * RET_DK:(h + 1) * RET_DK]
                dpr_ref[:, off + h * RET_DK:off + (h + 1) * RET_DK] = _bf(
                    (d * cos + _roll(d * sin, RET_DK // 2)) * scale)
        dpr_ref[:, 2 * qk:2 * qk + rv] = drv_ref[...]
        dpr_ref[:, 2 * qk + rv:o_lat] = drg_ref[...]
        dpr_ref[:, o_lat:o_kpe] = dlat_ref[...]
        dpr_ref[:, o_kpe:o_gate] = dkpe_ref[...]
        dpr_ref[:, o_gate:o_end] = dgt_ref[...]
        if width > o_end:
            dpr_ref[:, o_end:] = jnp.zeros((tm, width - o_end), BF16)
        dh_ref[...] = (ALPHA * dz_ref[...] + _dot_nt(dpr_ref[:, :o_lat], wr_ref[...])
                       + _dot_nt(dlat_ref[...], wc_ref[...]) + _dot_nt(dkpe_ref[...], wk_ref[...])
                       + _dot_nt(dgt_ref[...], wg_ref[...]))

    return _rowcall("proj_bwd", body, T, tm, [drq, drk, drv, drg, dz1, dlat, dkpe, dgates, cos_r, sin_r],
                    [w_r, w_c, w_kpe, w_g], [(D, F32), (width, BF16)])


def _ple_loss(h3, p, target, w_gate, w_proj, ln_g, ln_b, tm):
    T, D = h3.shape

    def body(i, h_ref, p_ref, t_ref, wg_ref, wp_ref, g_ref, b_ref, dh_ref, dgp_ref, dpp_ref, loss_ref, dg_ref, db_ref):
        hv = h_ref[...]
        sg = _sigmoid(_dot(_bf(hv), wg_ref[...]))
        pp = _dot(_bf(p_ref[...]), wp_ref[...])
        xhat, rstd = _ln_stats(ALPHA * hv + sg * pp)
        err = xhat * g_ref[...] + b_ref[...] - t_ref[...]
        row_loss = 0.5 * _mean(err * err)
        _acc(i, loss_ref, jnp.broadcast_to(jnp.sum(row_loss, axis=0, keepdims=True), (1, LANES)))
        dz, dg, db = _ln_bwd(err * (1.0 / D), xhat, rstd, g_ref[...])
        _acc(i, dg_ref, dg)
        _acc(i, db_ref, db)
        dgp = _bf(dz * pp * sg * (1.0 - sg))
        dgp_ref[...] = dgp
        dpp_ref[...] = _bf(dz * sg)
        dh_ref[...] = ALPHA * dz + _dot_nt(dgp, wg_ref[...])

    return _rowcall("ple_loss", body, T, tm, [h3, p, target], [w_gate, w_proj, ln_g, ln_b],
                    [(D, F32), (D, BF16), (D, BF16)], [((1, LANES), F32), ((1, D), F32), ((1, D), F32)])


def _ewise(name, fn, ins, n_out, out_dtype=F32):
    r, c = ins[0].shape
    tr = _tile(r, max(8, (1 << 19) // c // 8 * 8), 8)

    def kern(*refs):
        outs = fn(*[x[...] for x in refs[:len(ins)]])
        for o_ref, o in zip(refs[len(ins):], outs):
            o_ref[...] = o.astype(out_dtype)

    spec = pl.BlockSpec((tr, c), lambda i: (i, 0))
    return pl.pallas_call(kern, grid=(r // tr,), in_specs=[spec] * len(ins), out_specs=[spec] * n_out,
                          out_shape=[jax.ShapeDtypeStruct((r, c), out_dtype)] * n_out, name=name,
                          compiler_params=_params(("arbitrary",)))(*ins)


def _adamw_math(w, g, m, v):
    m = ADAM_B1 * m + (1.0 - ADAM_B1) * g
    v = ADAM_B2 * v + (1.0 - ADAM_B2) * (g * g)
    m_hat = m / (1.0 - ADAM_B1 ** ADAM_STEP)
    v_hat = v / (1.0 - ADAM_B2 ** ADAM_STEP)
    return -ADAM_LR * (m_hat / (jnp.sqrt(v_hat) + ADAM_EPS) + ADAM_WD * w), m, v


def _adamw(name, w, g, m, v):
    shape = w.shape
    c = shape[-1]
    flat = [t.reshape(-1, c) for t in (w, g, m, v)]
    return [t.reshape(shape) for t in _ewise(name, _adamw_math, flat, 3)]


def _place():
    return lax.axis_index("x"), lax.axis_index("y"), lax.axis_index("c")


def _dma_sems(n):
    return [pltpu.SemaphoreType.DMA((n,)), pltpu.SemaphoreType.DMA((n,))]


N_PEER_CHIPS = N_CHIPS - 1


def _chips_exchange(name, srcs, broadcast):
    n = len(srcs)

    def kern(*refs):
        cps = _chip_copies(refs[:n], refs[n:2 * n], refs[2 * n], refs[2 * n + 1], broadcast)
        for send, _ in cps:
            send.start()
        _wait_copies(cps)

    return pl.pallas_call(
        kern, out_shape=_exchange_shapes(srcs), in_specs=[HBM_SPEC] * n, out_specs=[HBM_SPEC] * n,
        scratch_shapes=_dma_sems(n * N_PEER_CHIPS), name=name)(*srcs)


def _exchange_shapes(srcs):
    return [jax.ShapeDtypeStruct((N_CHIPS,) + s.shape[1:], s.dtype) for s in srcs]


def _chip_copies(src_refs, out_refs, send_sems, recv_sems, broadcast):
    x, y, c = _place()
    me = 2 * x + y
    peers = [(1 - x, y), (x, 1 - y), (1 - x, 1 - y)]
    cps = []
    for j, (px, py) in enumerate(peers):
        for a, (src_ref, out_ref) in enumerate(zip(src_refs, out_refs)):
            piece = src_ref.at[c] if broadcast else src_ref.at[2 * px + py]

            def copy(slot):
                return pltpu.make_async_remote_copy(
                    src_ref=piece, dst_ref=out_ref.at[slot], send_sem=send_sems.at[a * N_PEER_CHIPS + j],
                    recv_sem=recv_sems.at[a * N_PEER_CHIPS + j], device_id=(px, py, c), device_id_type=MESH)

            cps.append((copy(me), copy(2 * px + py)))
    return cps


def _wait_copies(cps):
    for _, landing in cps:
        landing.wait_recv()
    for send, _ in cps:
        send.wait_send()


def _sibling_swap(name, srcs, mode):
    n = len(srcs)
    per = N_PEER_CHIPS if mode == "others" else 1

    def kern(*refs):
        src_refs, out_refs = refs[:n], refs[n:2 * n]
        send_sems, recv_sems = refs[2 * n:]
        x, y, c = _place()
        slots = [2 * (1 - x) + y, 2 * x + 1 - y, 2 * (1 - x) + 1 - y]
        cps = []
        for a in range(n):
            if mode == "others":
                pieces = [(src_refs[a].at[k], out_refs[a].at[k]) for k in slots]
            else:
                pieces = [(src_refs[a].at[:, 1 - c] if mode == "halves" else src_refs[a], out_refs[a])]
            for j, (src, dst) in enumerate(pieces):
                cps.append(pltpu.make_async_remote_copy(
                    src_ref=src, dst_ref=dst, send_sem=send_sems.at[a * per + j], recv_sem=recv_sems.at[a * per + j],
                    device_id=(x, y, 1 - c), device_id_type=MESH))
        for cp in cps:
            cp.start()
        for cp in cps:
            cp.wait_recv()
        for cp in cps:
            cp.wait_send()

    def out_shape(s):
        return jax.ShapeDtypeStruct((s.shape[0],) + s.shape[2:] if mode == "halves" else s.shape, s.dtype)

    return pl.pallas_call(
        kern, out_shape=[out_shape(s) for s in srcs], in_specs=[HBM_SPEC] * n, out_specs=[HBM_SPEC] * n,
        scratch_shapes=_dma_sems(n * per), name=name)(*srcs)


def _all_devices(name, src, reduce):
    r, c = src.shape
    n_dev = 2 * N_CHIPS

    def kern(src_ref, out_ref, *scratch):
        if reduce:
            gat_ref, send_sems, recv_sems = scratch
        else:
            gat_ref = out_ref
            send_sems, recv_sems = scratch
        x, y, cc = _place()
        me = 4 * x + 2 * y + cc
        gat_ref[me] = src_ref[...]
        peers = []
        for j in range(1, n_dev):
            px = 1 - x if j & 4 else x
            py = 1 - y if j & 2 else y
            pc = 1 - cc if j & 1 else cc
            peers.append((px, py, pc))

        def copy(j, peer, slot):
            return pltpu.make_async_remote_copy(
                src_ref=src_ref, dst_ref=gat_ref.at[slot], send_sem=send_sems.at[j], recv_sem=recv_sems.at[j],
                device_id=peer, device_id_type=MESH)

        sends = [copy(j, peer, me) for j, peer in enumerate(peers)]
        for cp in sends:
            cp.start()
        for j, (px, py, pc) in enumerate(peers):
            copy(j, (px, py, pc), 4 * px + 2 * py + pc).wait_recv()
        for cp in sends:
            cp.wait_send()
        if reduce:
            total = gat_ref[0]
            for d in range(1, n_dev):
                total = total + gat_ref[d]
            out_ref[...] = total

    out_shape = jax.ShapeDtypeStruct((r, c) if reduce else (n_dev, r, c), src.dtype)
    scratch = ([pltpu.VMEM((n_dev, r, c), src.dtype)] if reduce else []) + _dma_sems(n_dev - 1)
    return pl.pallas_call(kern, out_shape=out_shape, in_specs=[VMEM_SPEC], out_specs=VMEM_SPEC,
                          scratch_shapes=scratch, name=name)(src)


def _halves(t, axis):
    return t.reshape(t.shape[:axis] + (2, t.shape[axis] // 2) + t.shape[axis + 1:])


def _by_core(mine, theirs, axis):
    c = lax.axis_index("c")
    both = jnp.where(c == 0, jnp.stack([mine, theirs], axis), jnp.stack([theirs, mine], axis))
    return both.reshape(both.shape[:axis] + (2 * both.shape[axis + 1],) + both.shape[axis + 2:])


def _with_own(own, others):
    me = 2 * lax.axis_index("x") + lax.axis_index("y")
    is_me = (jnp.arange(N_CHIPS, dtype=jnp.int32) == me)[:, None, None]
    return jnp.where(is_me, own[None], others)


def _join_shards(name, shards):
    _, r, c = shards.shape
    if name in COL_SHARDED:
        return shards.transpose(1, 0, 2).reshape(r, N_CHIPS * c)
    return shards.reshape(N_CHIPS * r, c)


def _split_shards(name, full):
    if full.ndim == 3:
        return full
    r, c = full.shape
    if name in COL_SHARDED:
        return jnp.stack([full[:, k * (c // N_CHIPS):(k + 1) * (c // N_CHIPS)] for k in range(N_CHIPS)])
    return full.reshape(N_CHIPS, r // N_CHIPS, c)


def _rope_tables(positions):
    pos = positions.reshape(-1).astype(F32)[:, None]
    half = RET_DK // 2
    ang = pos * (ROPE_BASE ** (-jnp.arange(half, dtype=F32) / half))
    cos_r = jnp.concatenate([jnp.cos(ang)] * 2, axis=1)
    sin_r = jnp.concatenate([-jnp.sin(ang), jnp.sin(ang)], axis=1)
    half = MLA_ROPE // 2
    ang = pos * (ROPE_BASE ** (-jnp.arange(half, dtype=F32) / half))
    zeros = jnp.zeros_like(ang)
    rest = LANES - MLA_ROPE
    c = jnp.concatenate([jnp.cos(ang)] * 2 + [jnp.ones((ang.shape[0], rest), F32)], axis=1)
    s1 = jnp.concatenate([-jnp.sin(ang), zeros, jnp.zeros((ang.shape[0], rest), F32)], axis=1)
    s2 = jnp.concatenate([zeros, jnp.sin(ang), jnp.zeros((ang.shape[0], rest), F32)], axis=1)
    return cos_r, sin_r, (c, s1, s2)


GATHER_GROUPS = (("ffn1_w_in", "ffn1_w_out"), ("w_in", "w_uq", "w_ukv"),
                 ("w_ret_o", "w_mla_o", "w_out", "ffn2_w_in", "ffn2_w_out", "ple_w_gate", "ple_w_proj"))
REDUCE_GROUPS = (("ple_w_gate", "ple_w_proj", "ffn2_w_in", "ffn2_w_out"),
                 ("w_out", "w_ret_o", "w_mla_o", "w_uq", "w_ukv", "w_in"), ("ffn1_w_in",), ("ffn1_w_out",))


def _gathered(tag, names, own, mine):
    theirs = _sibling_swap("gather_cores_" + tag, mine, "others")
    out = {}
    for n, m, t in zip(names, mine, theirs):
        full = _with_own(own[n], _by_core(m, t, 1))
        out[n] = full if n in ("ffn1_w_in", "ffn2_w_in") else _join_shards(n, full)
    return out


def _chip_sums(tag, names, grads):
    halves = [_halves(_split_shards(n, grads[n]), 1) for n in names]
    theirs = _sibling_swap("reduce_cores_" + tag, halves, "halves")

    def one(n, g, t):
        k, _, r, c = g.shape
        tr = _tile(r, max(8, (1 << 17) // c // 8 * 8), 8)

        def kern(g_ref, t_ref, o_ref):
            mine = jnp.where(lax.axis_index("c") == 0, g_ref[:, 0], g_ref[:, 1])
            o_ref[...] = _bf(mine.astype(F32) + t_ref[...].astype(F32))

        spec = pl.BlockSpec((k, tr, c), lambda i: (0, i, 0))
        return pl.pallas_call(kern, grid=(r // tr,),
                              in_specs=[pl.BlockSpec((k, 2, tr, c), lambda i: (0, 0, i, 0)), spec], out_specs=spec,
                              out_shape=jax.ShapeDtypeStruct((k, r, c), BF16), name="reduce_cores_add_" + n,
                              compiler_params=_params(("arbitrary",)))(g, t)

    return [one(n, g, t) for n, g, t in zip(names, halves, theirs)]


def _block_totals(names, sums, parts):
    def one(n, s, pt):
        _, r, c = s.shape
        tr = _tile(r, max(8, (1 << 17) // c // 8 * 8), 8)

        def kern(s_ref, p_ref, o_ref):
            me = 2 * lax.axis_index("x") + lax.axis_index("y")
            terms = [jnp.where(k == me, s_ref[k], p_ref[k]).astype(F32) for k in range(N_CHIPS)]
            o_ref[...] = ((terms[0] + terms[1]) + terms[2]) + terms[3]

        spec = pl.BlockSpec((N_CHIPS, tr, c), lambda i: (0, i, 0))
        return pl.pallas_call(kern, grid=(r // tr,), in_specs=[spec, spec],
                              out_specs=pl.BlockSpec((tr, c), lambda i: (i, 0)),
                              out_shape=jax.ShapeDtypeStruct((r, c), F32), name="reduce_chips_add_" + n,
                              compiler_params=_params(("arbitrary",)))(s, pt)

    return [one(n, s, pt) for n, s, pt in zip(names, sums, parts)]


def _local_step(x, p, positions, target, shards, ln_blk, gn_g, qn_g, kvn_g):
    T, D = x.shape
    tm = min(256, T)
    H = MLA_HEADS
    qk, rv = RET_HEADS * RET_DK, RET_HEADS * RET_DV
    cos_r, sin_r, tabs = _rope_tables(positions)
    lgam = jnp.broadcast_to(jnp.log(1.0 - 2.0 ** (-5.0 - jnp.arange(RET_HEADS, dtype=F32)))[:, None, None],
                            (RET_HEADS, 1, LANES))
    own = {n: _bf(shards[n]) for n in BIG_WEIGHTS}
    to_send = [[_halves(own[n], 0) for n in names] for names in GATHER_GROUPS]

    *arrived, ln_others = _chips_exchange("gather_chips_a", to_send[0] + [jnp.stack([ln_blk, ln_blk])], True)
    ln_full = _with_own(ln_blk, ln_others).transpose(1, 0, 2).reshape(2 * N_LN, D)
    lng = [ln_full[k:k + 1] for k in range(N_LN)]
    lnb = [ln_full[N_LN + k:N_LN + k + 1] for k in range(N_LN)]

    w = _gathered("a", GATHER_GROUPS[0], own, arrived)
    h1, z0, a1, *arrived = _ffn_fwd("ffn1_fwd", x, w["ffn1_w_in"], w["ffn1_w_out"], lng[0], lnb[0], 2 * tm,
                                    exchange=(to_send[1], True))
    w.update(_gathered("b", GATHER_GROUPS[1], own, arrived))

    w_in = w["w_in"]
    o_lat, o_kpe, o_gate = 2 * qk + 2 * rv, 2 * qk + 2 * rv + Q_LORA + KV_LORA, 2 * qk + 2 * rv + Q_LORA + KV_LORA + MLA_ROPE
    w_r, w_c = w_in[:, :o_lat], w_in[:, o_lat:o_kpe]
    w_kpe = jnp.pad(w_in[:, o_kpe:o_gate], ((0, 0), (0, LANES - MLA_ROPE)))
    w_g = w_in[:, o_gate:]
    w_uq = jnp.pad(w["w_uq"].reshape(Q_LORA, H, MLA_NOPE + MLA_ROPE),
                   ((0, 0), (0, 0), (0, MLA_QK - MLA_NOPE - MLA_ROPE))).reshape(Q_LORA, H * MLA_QK)
    w_ukv = w["w_ukv"].reshape(KV_LORA, H, MLA_NOPE + MLA_DV)
    w_uk = w_ukv[:, :, :MLA_NOPE].reshape(KV_LORA, H * MLA_NOPE)
    w_uv = w_ukv[:, :, MLA_NOPE:].reshape(KV_LORA, H * MLA_DV)

    rq, rk, rvv, rg = _proj_ret(h1, w_r, cos_r, sin_r, 2 * tm)
    lat, gates, q, k, v, latn, qt, kt, vt = _proj_mla(h1, tabs, w_c, w_kpe, w_g, w_uq, w_uk, w_uv, qn_g, kvn_g, 2 * tm)
    y = _ret_fwd(rq, rk, rvv, lgam)
    o, lse_rows, *arrived = _attn_fwd(k, qt, vt, exchange=to_send[2])
    w.update(_gathered("c", GATHER_GROUPS[2], own, arrived))
    h2, z1, yret, ymla, yr, mix = _mix_fwd(y, rg, o, gates, h1, gn_g, w["w_ret_o"], w["w_mla_o"], w["w_out"],
                                           lng[1], lnb[1], 2 * tm)
    h3, z2, a2 = _ffn_fwd("ffn2_fwd", h2, w["ffn2_w_in"], w["ffn2_w_out"], lng[2], lnb[2], 2 * tm)

    dh3, dgp, dpp, loss, dg3, db3 = _ple_loss(h3, p, target, w["ple_w_gate"], w["ple_w_proj"], lng[3], lnb[3], 2 * tm)
    dh2, da2, s2, df2, dg2, db2 = _ffn_bwd("ffn2_bwd", dh3, z2, a2, w["ffn2_w_in"], w["ffn2_w_out"], lng[2], tm)
    grads = {"ple_w_gate": _mm_tn("wg_ple_gate", h3, dgp), "ple_w_proj": _mm_tn("wg_ple_proj", p, dpp),
             "ffn2_w_in": _mm_tn("wg_ffn2_in", h2, da2, n_split=N_CHIPS), "ffn2_w_out": _mm_tn("wg_ffn2_out", s2, df2)}
    sums1 = _chip_sums("1", REDUCE_GROUPS[0], grads)
    (dz1, dgates, drg, dy, do, dyret, dymla, dg1, db1, dgn, dot_, delta_rows, *parts1) = _mix_bwd(
        dh2, z1, gates, yret, ymla, y, rg, o, gn_g, w["w_ret_o"], w["w_mla_o"], w["w_out"], lng[1], tm,
        exchange=(sums1, False))
    drq = _ret_bwd_q(rq, rk, rvv, dy, lgam)
    drk, drv = _ret_bwd_kv(rq, rk, rvv, dy, lgam)
    dk, dv, dqt = _attn_bwd(q, k, v, do, qt, kt, dot_, lse_rows, delta_rows)
    dlat, dkpe, dqb, dkv, dqg, dkg = _proj_mla_bwd(dqt, dk, dv, lat, tabs, w_uq, w_uk, w_uv, qn_g, kvn_g, 2 * tm)
    dh1, dpr = _proj_bwd(drq, drk, drv, drg, dz1, dlat, dkpe, dgates, cos_r, sin_r, w_r, w_c, w_kpe, w_g, tm)
    g_uq = _mm_tn("wg_uq", latn[:, :Q_LORA], dqb).reshape(Q_LORA, H, MLA_QK)[:, :, :MLA_NOPE + MLA_ROPE]
    g_ukv = _mm_tn("wg_ukv", latn[:, Q_LORA:], dkv)
    g_uk = g_ukv[:, :H * MLA_NOPE].reshape(KV_LORA, H, MLA_NOPE)
    g_uv = g_ukv[:, H * MLA_NOPE:].reshape(KV_LORA, H, MLA_DV)
    g_in = _mm_tn("wg_in", h1, dpr)
    grads.update({
        "w_in": jnp.concatenate([g_in[:, :o_kpe + MLA_ROPE], g_in[:, o_kpe + LANES:o_kpe + LANES + 2 * D]], axis=1),
        "w_ret_o": _mm_tn("wg_ret_o", yr, dyret),
        "w_uq": g_uq.reshape(Q_LORA, H * (MLA_NOPE + MLA_ROPE)),
        "w_ukv": jnp.concatenate([g_uk, g_uv], axis=2).reshape(KV_LORA, H * (MLA_NOPE + MLA_DV)),
        "w_mla_o": _mm_tn("wg_mla_o", o, dymla),
        "w_out": _mm_tn("wg_out", mix, dz1)})
    sums2 = _chip_sums("2", REDUCE_GROUPS[1], grads)
    dx, da1, s1, df1, dg0, db0, *parts2 = _ffn_bwd("ffn1_bwd", dh1, z0, a1, w["ffn1_w_in"], w["ffn1_w_out"], lng[0], tm,
                                                   exchange=(sums2, False))
    grads["ffn1_w_in"] = _mm_tn("wg_ffn1_in", x, da1, n_split=N_CHIPS)
    sums3 = _chip_sums("3", REDUCE_GROUPS[2], grads)
    grads["ffn1_w_out"], *parts3 = _mm_tn("wg_ffn1_out", s1, df1, exchange=sums3)
    sums4 = _chip_sums("4", REDUCE_GROUPS[3], grads)
    parts4 = _chips_exchange("reduce_chips_4", sums4, False)

    names = [n for group in REDUCE_GROUPS for n in group]
    totals = _block_totals(names, sums1 + sums2 + sums3 + sums4,
                           list(parts1) + list(parts2) + list(parts3) + list(parts4))
    others = _sibling_swap("reduce_join", totals, "whole")
    reduced = {n: _by_core(t, o_, 0) for n, t, o_ in zip(names, totals, others)}
    small = {"ln_g": jnp.concatenate([dg0, dg1, dg2, dg3], axis=0), "ln_b": jnp.concatenate([db0, db1, db2, db3], axis=0),
             "ret_gn_g": dgn, "q_norm_g": dqg, "kv_norm_g": dkg}
    return loss[0, 0], dx, reduced, small


def kernel(x, p, positions, ln_g, ln_b, ffn1_w_in, ffn1_w_out, w_in, ret_gn_g, w_ret_o, q_norm_g, kv_norm_g, w_uq, w_ukv, w_mla_o, w_out, ffn2_w_in, ffn2_w_out, ple_w_gate, ple_w_proj, loss_target, m_ln_g, m_ln_b, m_ffn1_w_in, m_ffn1_w_out, m_w_in, m_ret_gn_g, m_w_ret_o, m_q_norm_g, m_kv_norm_g, m_w_uq, m_w_ukv, m_w_mla_o, m_w_out, m_ffn2_w_in, m_ffn2_w_out, m_ple_w_gate, m_ple_w_proj, v_ln_g, v_ln_b, v_ffn1_w_in, v_ffn1_w_out, v_w_in, v_ret_gn_g, v_w_ret_o, v_q_norm_g, v_kv_norm_g, v_w_uq, v_w_ukv, v_w_mla_o, v_w_out, v_ffn2_w_in, v_ffn2_w_out, v_ple_w_gate, v_ple_w_proj):
    names = ("ln_g", "ln_b", "ffn1_w_in", "ffn1_w_out", "w_in", "ret_gn_g", "w_ret_o", "q_norm_g", "kv_norm_g", "w_uq",
             "w_ukv", "w_mla_o", "w_out", "ffn2_w_in", "ffn2_w_out", "ple_w_gate", "ple_w_proj")
    weights = dict(zip(names, (ln_g, ln_b, ffn1_w_in, ffn1_w_out, w_in, ret_gn_g, w_ret_o, q_norm_g, kv_norm_g, w_uq,
                               w_ukv, w_mla_o, w_out, ffn2_w_in, ffn2_w_out, ple_w_gate, ple_w_proj)))
    m_in = dict(zip(names, (m_ln_g, m_ln_b, m_ffn1_w_in, m_ffn1_w_out, m_w_in, m_ret_gn_g, m_w_ret_o, m_q_norm_g,
                            m_kv_norm_g, m_w_uq, m_w_ukv, m_w_mla_o, m_w_out, m_ffn2_w_in, m_ffn2_w_out, m_ple_w_gate,
                            m_ple_w_proj)))
    v_in = dict(zip(names, (v_ln_g, v_ln_b, v_ffn1_w_in, v_ffn1_w_out, v_w_in, v_ret_gn_g, v_w_ret_o, v_q_norm_g,
                            v_kv_norm_g, v_w_uq, v_w_ukv, v_w_mla_o, v_w_out, v_ffn2_w_in, v_ffn2_w_out, v_ple_w_gate,
                            v_ple_w_proj)))
    chip = 2 * lax.axis_index("x") + lax.axis_index("y")
    D = x.shape[-1]
    dq = D // N_CHIPS

    shards = {n: weights[n][0] for n in BIG_WEIGHTS}
    loss, dx, big, small = _local_step(x[0], p[0, 0], positions, loss_target[0], shards,
                                       jnp.concatenate([ln_g[0], ln_b[0]], axis=0), ret_gn_g, q_norm_g, kv_norm_g)

    loss = lax.psum(loss, ("x", "y", "c"))
    small_names = ("ln_g", "ln_b", "ret_gn_g", "q_norm_g", "kv_norm_g")
    flat = jnp.concatenate([small[n].reshape(-1) for n in small_names])
    rows = -(-flat.shape[0] // LANES // 8) * 8
    flat = jnp.pad(flat, (0, rows * LANES - flat.shape[0])).reshape(rows, LANES)
    flat = _all_devices("reduce_small", flat, True).reshape(-1)
    off = 0
    for n in small_names:
        size = small[n].size
        small[n] = flat[off:off + size].reshape(small[n].shape)
        off += size
    g_out = dict(big)
    for n in ("ln_g", "ln_b"):
        g_out[n] = lax.dynamic_slice_in_dim(small[n], chip * dq, dq, axis=1)
    for n in ("ret_gn_g", "q_norm_g", "kv_norm_g"):
        g_out[n] = small[n]

    deltas, new_m, new_v = {}, {}, {}
    for n in names:
        g = g_out[n].reshape(weights[n].shape)
        g_out[n] = g
        deltas[n], new_m[n], new_v[n] = _adamw("adamw_" + n, weights[n], g, m_in[n], v_in[n])
    return (loss, dx[None], *[g_out[n] for n in names], *[deltas[n] for n in names], *[new_m[n] for n in names],
            *[new_v[n] for n in names])
```

```python
import functools

import jax
import jax.numpy as jnp
from jax import lax
from jax.experimental import pallas as pl
from jax.experimental.pallas import tpu as pltpu

CHUNK = 64
RET_HEADS = 8
RET_DK = 128
RET_DV = 256
MLA_HEADS = 8
MLA_NOPE = 128
MLA_ROPE = 64
MLA_DV = 128
MLA_QK = 256
Q_LORA = 256
KV_LORA = 256
ROPE_BASE = 10000.0
EPS = 1e-5
N_LN = 4
ALPHA = 2.0 ** 0.25
ADAM_LR = 0.001
ADAM_B1 = 0.9
ADAM_B2 = 0.999
ADAM_EPS = 1e-08
ADAM_WD = 0.01
ADAM_STEP = 10

LANES = 128
VMEM_LIMIT = 60 << 20
N_CHIPS = 4

F32 = jnp.float32
BF16 = jnp.bfloat16
MESH = pl.DeviceIdType.MESH
HBM_SPEC = pl.BlockSpec(memory_space=pltpu.HBM)
VMEM_SPEC = pl.BlockSpec(memory_space=pltpu.VMEM)

BIG_WEIGHTS = ("ffn1_w_in", "ffn1_w_out", "w_in", "w_ret_o", "w_uq", "w_ukv", "w_mla_o", "w_out",
               "ffn2_w_in", "ffn2_w_out", "ple_w_gate", "ple_w_proj")
COL_SHARDED = ("ffn1_w_in", "w_in", "w_uq", "w_ukv", "ffn2_w_in", "ple_w_proj")


def _dot(a, b):
    return jnp.dot(a, b, preferred_element_type=F32)


def _dot_nt(a, b):
    return lax.dot_general(a, b, (((1,), (1,)), ((), ())), preferred_element_type=F32)


def _dot_tn(a, b):
    return lax.dot_general(a, b, (((0,), (0,)), ((), ())), preferred_element_type=F32)


def _bf(x):
    return x.astype(BF16)


def _sigmoid(x):
    return 0.5 * jnp.tanh(0.5 * x) + 0.5


def _mean(x):
    return jnp.mean(x, axis=-1, keepdims=True)


def _ln_stats(z):
    zc = z - _mean(z)
    rstd = lax.rsqrt(_mean(zc * zc) + EPS)
    return zc * rstd, rstd


def _ln_bwd(dy, xhat, rstd, g):
    dxhat = dy * g
    dz = rstd * (dxhat - _mean(dxhat) - xhat * _mean(dxhat * xhat))
    return dz, jnp.sum(dy * xhat, axis=0, keepdims=True), jnp.sum(dy, axis=0, keepdims=True)


def _roll(x, shift):
    return pltpu.roll(x, shift, 1)


def _chunk_of(idx):
    return jnp.right_shift(idx, CHUNK.bit_length() - 1)


def _tile(n, cap, mult=LANES):
    if n <= cap:
        return n
    for t in range(cap - cap % mult, 0, -mult):
        if n % t == 0:
            return t
    return n


def _zero_map(nd, *_):
    return (0,) * nd


def _params(sem):
    return pltpu.CompilerParams(dimension_semantics=sem, vmem_limit_bytes=VMEM_LIMIT)


def _rowcall(name, body, n_rows, tm, row_ins, full_ins, row_outs, acc_outs=(), tiled_outs=(), tiled_ins=(),
             exchange=None):
    n_steps = n_rows // tm
    ex_srcs, broadcast = exchange if exchange else ((), False)
    n_ex = len(ex_srcs)
    n_in = len(row_ins) + len(tiled_ins) + len(full_ins)
    n_out = len(row_outs) + len(acc_outs) + len(tiled_outs)

    def kern(*refs):
        step = pl.program_id(0)
        ex_in, ex_out = refs[n_in:n_in + n_ex], refs[n_in + n_ex + n_out:n_in + 2 * n_ex + n_out]
        sems = refs[n_in + 2 * n_ex + n_out:]
        if n_ex:
            @pl.when(step == 0)
            def _():
                for send, _ in _chip_copies(ex_in, ex_out, *sems, broadcast):
                    send.start()

        body(step, *refs[:n_in], *refs[n_in + n_ex:n_in + n_ex + n_out])
        if n_ex:
            @pl.when(step == n_steps - 1)
            def _():
                _wait_copies(_chip_copies(ex_in, ex_out, *sems, broadcast))

    in_specs = [pl.BlockSpec((tm, a.shape[1]), lambda i: (i, 0)) for a in row_ins]
    in_specs += [spec for (_, spec) in tiled_ins]
    row_ins = list(row_ins) + [a for (a, _) in tiled_ins]
    in_specs += [pl.BlockSpec(a.shape, functools.partial(_zero_map, a.ndim), pipeline_mode=pl.Buffered(1))
                 for a in full_ins]
    in_specs += [HBM_SPEC] * n_ex
    out_specs = [pl.BlockSpec((tm, w), lambda i: (i, 0)) for (w, _) in row_outs]
    out_specs += [pl.BlockSpec(s, functools.partial(_zero_map, len(s))) for (s, _) in acc_outs]
    out_specs += [spec for (_, spec) in tiled_outs]
    out_specs += [HBM_SPEC] * n_ex
    out_shape = [jax.ShapeDtypeStruct((n_rows, w), dt) for (w, dt) in row_outs]
    out_shape += [jax.ShapeDtypeStruct(s, dt) for (s, dt) in acc_outs]
    out_shape += [shape for (shape, _) in tiled_outs]
    out_shape += _exchange_shapes(ex_srcs)
    return pl.pallas_call(kern, grid=(n_steps,), in_specs=in_specs, out_specs=out_specs, out_shape=out_shape,
                          scratch_shapes=_dma_sems(n_ex * N_PEER_CHIPS) if n_ex else [], name=name,
                          compiler_params=_params(("arbitrary",)))(*row_ins, *full_ins, *ex_srcs)


def _acc(step, ref, val):
    @pl.when(step == 0)
    def _():
        ref[...] = val

    @pl.when(step != 0)
    def _():
        ref[...] += val


def _ffn_fwd(name, x, w_in4, w_out, ln_g, ln_b, tm, exchange=None):
    T, D = x.shape
    fh = w_in4.shape[2]

    def body(i, x_ref, w4_ref, wo_ref, g_ref, b_ref, h_ref, z_ref, a_ref):
        xv = x_ref[...]
        xb = _bf(xv)
        f = jnp.zeros((tm, D), F32)
        for k in range(2):
            gk = _dot(xb, w4_ref[k])
            uk = _dot(xb, w4_ref[2 + k])
            a_ref[:, k * fh:(k + 1) * fh] = _bf(gk)
            a_ref[:, (2 + k) * fh:(3 + k) * fh] = _bf(uk)
            f += _dot(_bf(gk * _sigmoid(gk) * uk), wo_ref[k * fh:(k + 1) * fh, :])
        z = ALPHA * xv + 0.5 * f
        xhat, _ = _ln_stats(z)
        z_ref[...] = z
        h_ref[...] = xhat * g_ref[...] + b_ref[...]

    return _rowcall(name, body, T, tm, [x], [w_in4, w_out, ln_g, ln_b],
                    [(D, F32), (D, F32), (4 * fh, BF16)], exchange=exchange)


def _ffn_bwd(name, dh, z, a, w_in4, w_out, ln_g, tm, exchange=None):
    T, D = dh.shape
    fh = w_in4.shape[2]

    def body(i, dh_ref, z_ref, a_ref, w4_ref, wo_ref, g_ref, dx_ref, da_ref, s_ref, df_ref, dg_ref, db_ref):
        xhat, rstd = _ln_stats(z_ref[...])
        dz, dg, db = _ln_bwd(dh_ref[...], xhat, rstd, g_ref[...])
        _acc(i, dg_ref, dg)
        _acc(i, db_ref, db)
        dfb = _bf(0.5 * dz)
        df_ref[...] = dfb
        dx = ALPHA * dz
        for k in range(2):
            gk = a_ref[:, k * fh:(k + 1) * fh].astype(F32)
            uk = a_ref[:, (2 + k) * fh:(3 + k) * fh].astype(F32)
            ds = _dot_nt(dfb, wo_ref[k * fh:(k + 1) * fh, :])
            sig = _sigmoid(gk)
            silu = gk * sig
            dgk = _bf(ds * uk * sig * (1.0 + gk * (1.0 - sig)))
            duk = _bf(ds * silu)
            s_ref[:, k * fh:(k + 1) * fh] = _bf(silu * uk)
            da_ref[:, k * fh:(k + 1) * fh] = dgk
            da_ref[:, (2 + k) * fh:(3 + k) * fh] = duk
            dx += _dot_nt(dgk, w4_ref[k]) + _dot_nt(duk, w4_ref[2 + k])
        dx_ref[...] = dx

    return _rowcall(name, body, T, tm, [dh, z, a], [w_in4, w_out, ln_g],
                    [(D, F32), (4 * fh, BF16), (2 * fh, BF16), (D, BF16)],
                    [((1, D), F32), ((1, D), F32)], exchange=exchange)


WG_TILE_N = 1536


def _mm_tn(name, a, b, out_dtype=BF16, n_split=1, exchange=()):
    T, M = a.shape
    N = b.shape[1]
    tk = _tile(T, 2048, 8)
    tm = _tile(M, 1408)
    tn = _tile(N // n_split, WG_TILE_N)
    per = N // n_split // tn
    nk = T // tk
    n_ex = len(exchange)
    grid = (M // tm, N // tn, nk)
    if n_split > 1:
        out_spec = pl.BlockSpec((None, tm, tn), lambda i, j, k: (j // per, i, j % per))
        out_shape = jax.ShapeDtypeStruct((n_split, M, N // n_split), out_dtype)
    else:
        out_spec = pl.BlockSpec((tm, tn), lambda i, j, k: (i, j))
        out_shape = jax.ShapeDtypeStruct((M, N), out_dtype)

    def kern(a_ref, b_ref, *refs):
        ex_in, o_ref, ex_out = refs[:n_ex], refs[n_ex], refs[n_ex + 1:2 * n_ex + 1]
        acc_ref, sems = refs[2 * n_ex + 1], refs[2 * n_ex + 2:]
        k = pl.program_id(2)
        at_end = [pl.program_id(d) == grid[d] - 1 for d in range(3)]
        if n_ex:
            @pl.when(jnp.logical_and(jnp.logical_and(pl.program_id(0) == 0, pl.program_id(1) == 0), k == 0))
            def _():
                for send, _ in _chip_copies(ex_in, ex_out, *sems, False):
                    send.start()

        part = _dot_tn(_bf(a_ref[...]), _bf(b_ref[...]))

        @pl.when(k == 0)
        def _():
            acc_ref[...] = part

        @pl.when(k != 0)
        def _():
            acc_ref[...] += part

        @pl.when(k == nk - 1)
        def _():
            o_ref[...] = acc_ref[...].astype(out_dtype)

        if n_ex:
            @pl.when(jnp.logical_and(jnp.logical_and(at_end[0], at_end[1]), at_end[2]))
            def _():
                _wait_copies(_chip_copies(ex_in, ex_out, *sems, False))

    outs = pl.pallas_call(
        kern, grid=grid,
        in_specs=[pl.BlockSpec((tk, tm), lambda i, j, k: (k, i)), pl.BlockSpec((tk, tn), lambda i, j, k: (k, j))]
        + [HBM_SPEC] * n_ex,
        out_specs=[out_spec] + [HBM_SPEC] * n_ex, out_shape=[out_shape] + _exchange_shapes(exchange),
        scratch_shapes=[pltpu.VMEM((tm, tn), F32)] + (_dma_sems(n_ex * N_PEER_CHIPS) if n_ex else []), name=name,
        compiler_params=_params(("arbitrary", "arbitrary", "arbitrary")))(a, b, *exchange)
    return outs if n_ex else outs[0]


def _proj_ret(h1, w_r, cos_r, sin_r, tm):
    T, D = h1.shape
    qk = RET_HEADS * RET_DK
    rv = RET_HEADS * RET_DV

    def body(i, h_ref, cos_ref, sin_ref, w_ref, q_ref, k_ref, v_ref, g_ref):
        hb = _bf(h_ref[...])
        cos, sin = cos_ref[...], sin_ref[...]
        for out_ref, off, scale in ((q_ref, 0, 1.0), (k_ref, qk, RET_DK ** -0.5)):
            pr = _dot(hb, w_ref[:, off:off + qk])
            for h in range(RET_HEADS):
                t = pr[:, h * RET_DK:(h + 1) * RET_DK]
                out_ref[:, h * RET_DK:(h + 1) * RET_DK] = _bf((t * cos + _roll(t, RET_DK // 2) * sin) * scale)
        v_ref[...] = _bf(_dot(hb, w_ref[:, 2 * qk:2 * qk + rv]))
        g_ref[...] = _bf(_dot(hb, w_ref[:, 2 * qk + rv:2 * qk + 2 * rv]))

    return _rowcall("proj_ret", body, T, tm, [h1, cos_r, sin_r], [w_r],
                    [(qk, BF16), (qk, BF16), (rv, BF16), (rv, BF16)])


def _rope_pe(t, c, s1, s2):
    return t * c + _roll(t, LANES - MLA_ROPE // 2) * s1 + _roll(t, MLA_ROPE // 2) * s2


def _rope_pe_bwd(dy, c, s1, s2):
    return dy * c + _roll(dy * s1, MLA_ROPE // 2) + _roll(dy * s2, LANES - MLA_ROPE // 2)


def _rms(x, g):
    r = lax.rsqrt(_mean(x * x) + EPS)
    return x * r, r


def _attn_block(T):
    return min(512, T)


def _transposed_blocks(T, tm, w, dtype):
    tb = _attn_block(T)
    per = tb // tm
    return (jax.ShapeDtypeStruct((T // tb, MLA_HEADS, w, tb), dtype),
            pl.BlockSpec((None, MLA_HEADS, w, tm), lambda i: (i // per, 0, 0, i % per)))


ATTN_SCALE = (MLA_NOPE + MLA_ROPE) ** -0.5
LOG2E = 1.4426950408889634
Q_PRESCALE = ATTN_SCALE * LOG2E
V_ONES = 16


def _proj_mla(h1, tabs, w_c, w_kpe, w_g, w_uq, w_uk, w_uv, qn_g, kvn_g, tm):
    T, D = h1.shape
    H = MLA_HEADS

    def body(i, h_ref, c_ref, s1_ref, s2_ref, wc_ref, wk_ref, wg_ref, wuq_ref, wuk_ref, wuv_ref, qg_ref, kg_ref,
             lat_ref, gt_ref, q_ref, k_ref, v_ref, ln_ref, qt_ref, kt_ref, vt_ref):
        hb = _bf(h_ref[...])
        c, s1, s2 = c_ref[...], s1_ref[...], s2_ref[...]
        lat = _dot(hb, wc_ref[...])
        lat_ref[...] = lat
        gt_ref[...] = _bf(_dot(hb, wg_ref[...]))
        cqn, _ = _rms(lat[:, :Q_LORA], None)
        ckn, _ = _rms(lat[:, Q_LORA:], None)
        cqn = _bf(cqn * qg_ref[...])
        ckn = _bf(ckn * kg_ref[...])
        ln_ref[:, :Q_LORA] = cqn
        ln_ref[:, Q_LORA:] = ckn
        q = _dot(cqn, wuq_ref[...])
        kn = _dot(ckn, wuk_ref[...])
        vv = _dot(ckn, wuv_ref[...])
        v_ref[...] = _bf(vv)
        kpe = _rope_pe(_dot(hb, wk_ref[...]), c, s1, s2)
        ones = jnp.ones((V_ONES, tm), BF16)
        for h in range(H):
            o = h * MLA_QK
            qh = jnp.concatenate([q[:, o:o + MLA_NOPE], _rope_pe(q[:, o + MLA_NOPE:o + MLA_QK], c, s1, s2)], axis=1)
            qh = qh * Q_PRESCALE
            kh = jnp.concatenate([kn[:, h * MLA_NOPE:(h + 1) * MLA_NOPE], kpe], axis=1)
            q_ref[:, o:o + MLA_QK] = _bf(qh)
            k_ref[:, o:o + MLA_QK] = _bf(kh)
            qt_ref[h] = _bf(qh.T)
            kt_ref[h] = _bf(kh.T)
            vt_ref[h] = jnp.concatenate([_bf(vv[:, h * MLA_DV:(h + 1) * MLA_DV].T), ones], axis=0)

    lat_w = Q_LORA + KV_LORA
    return _rowcall("proj_mla", body, T, tm, [h1, *tabs], [w_c, w_kpe, w_g, w_uq, w_uk, w_uv, qn_g, kvn_g],
                    [(lat_w, F32), (2 * D, BF16), (H * MLA_QK, BF16), (H * MLA_QK, BF16), (H * MLA_DV, BF16),
                     (lat_w, BF16)],
                    tiled_outs=[_transposed_blocks(T, tm, MLA_QK, BF16), _transposed_blocks(T, tm, MLA_QK, BF16),
                                _transposed_blocks(T, tm, MLA_DV + V_ONES, BF16)])


def _ret_block(T):
    return min(256, T)


RET_HEADS_PER_STEP = 8


def _ret_dmat(lg, bt):
    n = lax.broadcasted_iota(jnp.int32, (bt, bt), 0)
    m = lax.broadcasted_iota(jnp.int32, (bt, bt), 1)
    return jnp.where(_chunk_of(m) <= _chunk_of(n), jnp.exp(lg * jnp.abs(n - m).astype(F32)), 0.0)


def _ret_scan(name, per_head, lgam, ins, outs, rev):
    T = ins[0][0].shape[0]
    bt = _ret_block(T)
    nb = T // bt
    hps = min(RET_HEADS_PER_STEP, RET_HEADS)
    n_in, n_out = len(ins), len(outs)

    def kern(lg_ref, *refs):
        in_refs, out_refs = refs[:n_in], refs[n_in:n_in + n_out]
        state_ref, dmat_ref = refs[n_in + n_out:]

        @pl.when(pl.program_id(1) == 0)
        def _():
            state_ref[...] = jnp.zeros_like(state_ref)
            for hh in range(hps):
                dmat_ref[hh] = _ret_dmat(lg_ref[hh][:, :1], bt)

        pos = lax.broadcasted_iota(jnp.int32, (bt, 1), 0).astype(F32)
        for hh in range(hps):
            lg = lg_ref[hh][:, :1]
            xi, zeta, gb = jnp.exp(lg * (pos + 1.0)), jnp.exp(lg * (bt - 1.0 - pos)), jnp.exp(lg * bt)
            tiles = [r[:, hh * w:(hh + 1) * w] for r, (_, w) in zip(in_refs, ins)]
            res = per_head(dmat_ref[hh], xi, zeta, gb, state_ref.at[hh], *tiles)
            for o_ref, (w, _), val in zip(out_refs, outs, res):
                o_ref[:, hh * w:(hh + 1) * w] = val.astype(o_ref.dtype)

    def blk(w):
        if rev:
            return pl.BlockSpec((bt, hps * w), lambda g, b: (nb - 1 - b, g))
        return pl.BlockSpec((bt, hps * w), lambda g, b: (b, g))

    return pl.pallas_call(
        kern, grid=(RET_HEADS // hps, nb),
        in_specs=[pl.BlockSpec((hps, 1, LANES), lambda g, b: (g, 0, 0))] + [blk(w) for _, w in ins],
        out_specs=[blk(w) for w, _ in outs],
        out_shape=[jax.ShapeDtypeStruct((T, RET_HEADS * w), dt) for w, dt in outs],
        scratch_shapes=[pltpu.VMEM((hps, RET_DK, RET_DV), F32), pltpu.VMEM((hps, bt, bt), F32)], name=name,
        compiler_params=_params(("arbitrary", "arbitrary")))(lgam, *[a for a, _ in ins])


def _ret_fwd(rq, rk, rv, lgam):
    def per_head(dmat, xi, zeta, gb, s_ref, q, k, v):
        sc = _dot_nt(q, k) * dmat
        y = _dot(_bf(sc), v) + _dot(q, _bf(s_ref[...])) * xi
        s_ref[...] = s_ref[...] * gb + _dot_tn(_bf(k.astype(F32) * zeta), v)
        return (y,)

    return _ret_scan("ret_fwd", per_head, lgam, [(rq, RET_DK), (rk, RET_DK), (rv, RET_DV)], [(RET_DV, BF16)], False)[0]


def _ret_bwd_q(rq, rk, rv, dy, lgam):
    def per_head(dmat, xi, zeta, gb, s_ref, k, v, dy):
        dp = _dot_nt(dy, v) * dmat
        dq = _dot(_bf(dp), k) + _dot_nt(dy, _bf(s_ref[...])) * xi
        s_ref[...] = s_ref[...] * gb + _dot_tn(_bf(k.astype(F32) * zeta), v)
        return (dq,)

    return _ret_scan("ret_bwd_q", per_head, lgam, [(rk, RET_DK), (rv, RET_DV), (dy, RET_DV)], [(RET_DK, F32)], False)[0]


def _ret_bwd_kv(rq, rk, rv, dy, lgam):
    def per_head(dmat, xi, zeta, gb, g_ref, q, k, v, dy):
        gs = _bf(g_ref[...])
        p = _dot_nt(q, k) * dmat
        dp = _dot_nt(dy, v) * dmat
        dv = _dot_tn(_bf(p), dy) + _dot(k, gs) * zeta
        dk = _dot_tn(_bf(dp), q) + _dot_nt(v, gs) * zeta
        g_ref[...] = g_ref[...] * gb + _dot_tn(_bf(q.astype(F32) * xi), dy)
        return dk, dv

    return _ret_scan("ret_bwd_kv", per_head, lgam, [(rq, RET_DK), (rk, RET_DK), (rv, RET_DV), (dy, RET_DV)],
                     [(RET_DK, F32), (RET_DV, BF16)], True)


def _attn_mask_t(tb):
    key = lax.broadcasted_iota(jnp.int32, (tb, tb), 0)
    qry = lax.broadcasted_iota(jnp.int32, (tb, tb), 1)
    return _chunk_of(key) <= _chunk_of(qry)


MASKED = -1e30
SUBLANES = 8


def _head_blocks(nb, w, tb):
    return pl.BlockSpec((nb, None, w, tb), lambda h, i: (0, h, 0, 0))


def _one_block(w, tb):
    return pl.BlockSpec((None, None, w, tb), lambda h, i: (i, h, 0, 0))


def _attn_fwd(k, qt, vt, exchange=()):
    T = k.shape[0]
    tb = _attn_block(T)
    nb = T // tb

    n_ex = len(exchange)

    def kern(qt_ref, k_ref, vt_ref, *refs):
        ex_in, (o_ref, lser_ref), ex_out = refs[:n_ex], refs[n_ex:n_ex + 2], refs[n_ex + 2:2 * n_ex + 2]
        m_ref, acc_ref, sa_ref, sb_ref = refs[2 * n_ex + 2:2 * n_ex + 6]
        sems = refs[2 * n_ex + 6:]
        qb = pl.program_id(1)
        first = jnp.logical_and(pl.program_id(0) == 0, qb == 0)
        last = jnp.logical_and(pl.program_id(0) == MLA_HEADS - 1, qb == nb - 1)
        if n_ex:
            @pl.when(first)
            def _():
                for send, _ in _chip_copies(ex_in, ex_out, *sems, True):
                    send.start()

        qt = qt_ref[...]
        m_ref[...] = jnp.full_like(m_ref, MASKED)
        acc_ref[...] = jnp.zeros_like(acc_ref)

        def scores(kb):
            rows = pl.ds(pl.multiple_of(kb * tb, tb), tb)
            return _dot(k_ref[rows, :], qt)

        def update(s, kb):
            m_old = m_ref[...]
            m_new = jnp.maximum(m_old, jnp.max(s, axis=0, keepdims=True))
            p = jnp.exp2(s - m_new)
            acc_ref[...] = acc_ref[...] * jnp.exp2(m_old - m_new) + _dot(vt_ref[kb], _bf(p))
            m_ref[...] = m_new

        def masked(s):
            return jnp.where(_attn_mask_t(tb), s, MASKED)

        sa_ref[...] = scores(0)

        def pair_body(j, carry):
            sb_ref[...] = scores(2 * j + 1)
            update(sa_ref[...], 2 * j)
            sa_ref[...] = scores(2 * j + 2)
            update(sb_ref[...], 2 * j + 1)
            return carry

        lax.fori_loop(0, qb // 2, pair_body, 0)

        @pl.when(qb % 2 == 0)
        def _():
            update(masked(sa_ref[...]), qb)

        @pl.when(qb % 2 == 1)
        def _():
            sb_ref[...] = masked(scores(qb))
            update(sa_ref[...], qb - 1)
            update(sb_ref[...], qb)

        l = acc_ref[MLA_DV:MLA_DV + 1, :]
        o_ref[...] = _bf((acc_ref[:MLA_DV, :] / l).T)
        lser_ref[...] = jnp.broadcast_to(m_ref[...] + jnp.log2(l), (SUBLANES, tb))
        if n_ex:
            @pl.when(last)
            def _():
                _wait_copies(_chip_copies(ex_in, ex_out, *sems, True))

    return pl.pallas_call(
        kern, grid=(MLA_HEADS, nb),
        in_specs=[_one_block(MLA_QK, tb), pl.BlockSpec((T, MLA_QK), lambda h, i: (0, h)),
                  _head_blocks(nb, MLA_DV + V_ONES, tb)] + [HBM_SPEC] * n_ex,
        out_specs=[pl.BlockSpec((tb, MLA_DV), lambda h, i: (i, h)), _one_block(SUBLANES, tb)] + [HBM_SPEC] * n_ex,
        out_shape=[jax.ShapeDtypeStruct((T, MLA_HEADS * MLA_DV), BF16),
                   jax.ShapeDtypeStruct((nb, MLA_HEADS, SUBLANES, tb), F32)] + _exchange_shapes(exchange),
        scratch_shapes=[pltpu.VMEM((1, tb), F32), pltpu.VMEM((MLA_DV + V_ONES, tb), F32),
                        pltpu.VMEM((tb, tb), F32), pltpu.VMEM((tb, tb), F32)]
        + (_dma_sems(n_ex * N_PEER_CHIPS) if n_ex else []),
        name="attn_fwd", compiler_params=_params(("arbitrary", "arbitrary")))(qt, k, vt, *exchange)


def _attn_bwd(q, k, v, do, qt, kt, dot_, lse_rows, delta_rows):
    T = q.shape[0]
    tb = _attn_block(T)
    nb = T // tb

    def kern(q_ref, k_ref, v_ref, do_ref, qt_ref, kt_ref, dot_ref, lse_ref, dl_ref, dk_ref, dv_ref, dqt_ref, dv_acc,
             sa_ref, pa_ref, sb_ref, pb_ref):
        kb = pl.program_id(1)
        kv, vv, ktv = k_ref[...], v_ref[...], kt_ref[...]
        dk_ref[...] = jnp.zeros_like(dk_ref)
        dv_acc[...] = jnp.zeros_like(dv_acc)

        @pl.when(kb == 0)
        def _():
            dqt_ref[...] = jnp.zeros_like(dqt_ref)

        def products(qb, s_ref, dp_ref, diagonal=False):
            s = _dot(kv, qt_ref[qb])
            s_ref[...] = jnp.where(_attn_mask_t(tb), s, MASKED) if diagonal else s
            dp_ref[...] = _dot(vv, dot_ref[qb])

        def consume(qb, s_ref, dp_ref):
            rows = pl.ds(pl.multiple_of(qb * tb, tb), tb)
            p = jnp.exp2(s_ref[...] - lse_ref[qb][:1, :])
            dv_acc[...] += _dot(_bf(p), do_ref[rows, :])
            ds = _bf(p * (dp_ref[...] - dl_ref[qb][:1, :]))
            dk_ref[...] += _dot(ds, q_ref[rows, :])
            dqt_ref[qb] += _dot(ktv, ds)

        n_full = nb - 1 - kb
        products(kb, sa_ref, pa_ref, diagonal=True)

        def pair_body(j, carry):
            q1 = kb + 1 + 2 * j
            products(q1, sb_ref, pb_ref)
            consume(q1 - 1, sa_ref, pa_ref)
            products(q1 + 1, sa_ref, pa_ref)
            consume(q1, sb_ref, pb_ref)
            return carry

        lax.fori_loop(0, n_full // 2, pair_body, 0)

        @pl.when(n_full % 2 == 0)
        def _():
            consume(nb - 1, sa_ref, pa_ref)

        @pl.when(n_full % 2 == 1)
        def _():
            products(nb - 1, sb_ref, pb_ref)
            consume(nb - 2, sa_ref, pa_ref)
            consume(nb - 1, sb_ref, pb_ref)

        dk_ref[...] = dk_ref[...] * (ATTN_SCALE / Q_PRESCALE)
        dv_ref[...] = _bf(dv_acc[...])

    def blk(w):
        return pl.BlockSpec((tb, w), lambda h, i: (i, h))

    def full(w):
        return pl.BlockSpec((T, w), lambda h, i: (0, h))

    return pl.pallas_call(
        kern, grid=(MLA_HEADS, nb),
        in_specs=[full(MLA_QK), blk(MLA_QK), blk(MLA_DV), full(MLA_DV), _head_blocks(nb, MLA_QK, tb),
                  _one_block(MLA_QK, tb), _head_blocks(nb, MLA_DV, tb), _head_blocks(nb, SUBLANES, tb),
                  _head_blocks(nb, SUBLANES, tb)],
        out_specs=[blk(MLA_QK), blk(MLA_DV), _head_blocks(nb, MLA_QK, tb)],
        out_shape=[jax.ShapeDtypeStruct((T, MLA_HEADS * MLA_QK), F32),
                   jax.ShapeDtypeStruct((T, MLA_HEADS * MLA_DV), BF16),
                   jax.ShapeDtypeStruct((nb, MLA_HEADS, MLA_QK, tb), F32)],
        scratch_shapes=[pltpu.VMEM((tb, MLA_DV), F32)] + [pltpu.VMEM((tb, tb), F32)] * 4,
        name="attn_bwd", compiler_params=_params(("arbitrary", "arbitrary")))(
            q, k, v, do, qt, kt, dot_, lse_rows, delta_rows)


def _group_norm(y):
    yc = y - _mean(y)
    rstd = lax.rsqrt(_mean(yc * yc) + EPS)
    return yc * rstd, rstd


def _mix_fwd(y, rg, o, gates, h1, gn_g, w_ret_o, w_mla_o, w_out, ln_g, ln_b, tm):
    T, D = h1.shape

    def body(i, y_ref, rg_ref, o_ref, gt_ref, h_ref, gn_ref, wr_ref, wm_ref, wo_ref, g_ref, b_ref,
             h2_ref, z_ref, yret_ref, ymla_ref, yr_ref, mix_ref):
        for h in range(RET_HEADS):
            sl = slice(h * RET_DV, (h + 1) * RET_DV)
            yn, _ = _group_norm(y_ref[:, sl].astype(F32))
            r = rg_ref[:, sl].astype(F32)
            yr_ref[:, sl] = _bf(r * _sigmoid(r) * (yn * gn_ref[:, sl]))
        yret = _dot(yr_ref[...], wr_ref[...])
        ymla = _dot(_bf(o_ref[...]), wm_ref[...])
        yret_ref[...] = _bf(yret)
        ymla_ref[...] = _bf(ymla)
        mix = _bf(_sigmoid(gt_ref[:, :D].astype(F32)) * yret + _sigmoid(gt_ref[:, D:].astype(F32)) * ymla)
        mix_ref[...] = mix
        z = ALPHA * h_ref[...] + _dot(mix, wo_ref[...])
        xhat, _ = _ln_stats(z)
        z_ref[...] = z
        h2_ref[...] = xhat * g_ref[...] + b_ref[...]

    return _rowcall("mix_fwd", body, T, tm, [y, rg, o, gates, h1], [gn_g, w_ret_o, w_mla_o, w_out, ln_g, ln_b],
                    [(D, F32), (D, F32), (D, BF16), (D, BF16), (RET_HEADS * RET_DV, BF16), (D, BF16)])


def _mix_bwd(dh2, z1, gates, yret, ymla, y, rg, o, gn_g, w_ret_o, w_mla_o, w_out, ln_g, tm, exchange=None):
    T, D = dh2.shape
    rv = RET_HEADS * RET_DV

    def body(i, dh_ref, z_ref, gt_ref, yret_ref, ymla_ref, y_ref, rg_ref, o_ref, gn_ref, wr_ref, wm_ref, wo_ref, g_ref,
             dz_ref, dgt_ref, drg_ref, dy_ref, do_ref, dyret_ref, dymla_ref, dg_ref, db_ref, dgn_ref, dot_ref,
             dl_ref):
        xhat, rstd = _ln_stats(z_ref[...])
        dz, dg, db = _ln_bwd(dh_ref[...], xhat, rstd, g_ref[...])
        _acc(i, dg_ref, dg)
        _acc(i, db_ref, db)
        dz_ref[...] = dz
        dmix = _dot_nt(_bf(dz), wo_ref[...])
        sr = _sigmoid(gt_ref[:, :D].astype(F32))
        sm = _sigmoid(gt_ref[:, D:].astype(F32))
        dgt_ref[:, :D] = _bf(dmix * yret_ref[...].astype(F32) * sr * (1.0 - sr))
        dgt_ref[:, D:] = _bf(dmix * ymla_ref[...].astype(F32) * sm * (1.0 - sm))
        dyret = _bf(dmix * sr)
        dymla = _bf(dmix * sm)
        dyret_ref[...] = dyret
        dymla_ref[...] = dymla
        dov = _dot_nt(dymla, wm_ref[...])
        do_ref[...] = _bf(dov)
        for h in range(MLA_HEADS):
            sl = slice(h * MLA_DV, (h + 1) * MLA_DV)
            dot_ref[h] = _bf(dov[:, sl].T)
            delta = jnp.sum(dov[:, sl] * o_ref[:, sl].astype(F32), axis=-1, keepdims=True)
            dl_ref[h] = jnp.broadcast_to(delta, (tm, LANES)).T[:SUBLANES, :]
        dyr = _dot_nt(dyret, wr_ref[...])
        dgn = []
        for h in range(RET_HEADS):
            sl = slice(h * RET_DV, (h + 1) * RET_DV)
            yn, grstd = _group_norm(y_ref[:, sl].astype(F32))
            r = rg_ref[:, sl].astype(F32)
            sig = _sigmoid(r)
            d = dyr[:, sl]
            drg_ref[:, sl] = _bf(d * (yn * gn_ref[:, sl]) * sig * (1.0 + r * (1.0 - sig)))
            dt = d * (r * sig)
            dgn.append(jnp.sum(dt * yn, axis=0, keepdims=True))
            dyn = dt * gn_ref[:, sl]
            dy_ref[:, sl] = _bf(grstd * (dyn - _mean(dyn) - yn * _mean(dyn * yn)))
        _acc(i, dgn_ref, jnp.concatenate(dgn, axis=1))

    return _rowcall("mix_bwd", body, T, tm, [dh2, z1, gates, yret, ymla, y, rg, o],
                    [gn_g, w_ret_o, w_mla_o, w_out, ln_g],
                    [(D, F32), (2 * D, BF16), (rv, BF16), (rv, BF16), (MLA_HEADS * MLA_DV, BF16), (D, BF16), (D, BF16)],
                    [((1, D), F32), ((1, D), F32), ((1, rv), F32)],
                    tiled_outs=[_transposed_blocks(T, tm, MLA_DV, BF16), _transposed_blocks(T, tm, SUBLANES, F32)],
                    exchange=exchange)


def _proj_mla_bwd(dqt, dk, dv, lat, tabs, w_uq, w_uk, w_uv, qn_g, kvn_g, tm):
    T = dk.shape[0]
    H = MLA_HEADS
    lat_w = Q_LORA + KV_LORA

    def body(i, dk_ref, dv_ref, lat_ref, c_ref, s1_ref, s2_ref, dqt_ref, wuq_ref, wuk_ref, wuv_ref, qg_ref, kg_ref,
             dlat_ref, dkpe_ref, dqb_ref, dkn_ref, dqg_ref, dkg_ref):
        c, s1, s2 = c_ref[...], s1_ref[...], s2_ref[...]
        dkpe = jnp.zeros((tm, LANES), F32)
        for h in range(H):
            o = h * MLA_QK
            dqh = dqt_ref[h].T * ATTN_SCALE
            dqb_ref[:, o:o + MLA_NOPE] = _bf(dqh[:, :MLA_NOPE])
            dqb_ref[:, o + MLA_NOPE:o + MLA_QK] = _bf(_rope_pe_bwd(dqh[:, MLA_NOPE:], c, s1, s2))
            dkn_ref[:, h * MLA_NOPE:(h + 1) * MLA_NOPE] = _bf(dk_ref[:, o:o + MLA_NOPE])
            dkpe += dk_ref[:, o + MLA_NOPE:o + MLA_QK]
        dkn_ref[:, H * MLA_NOPE:] = dv_ref[...]
        dkpe_ref[...] = _bf(_rope_pe_bwd(dkpe, c, s1, s2))
        dcqn = _dot_nt(dqb_ref[...], wuq_ref[...])
        dckn = _dot_nt(dkn_ref[:, :H * MLA_NOPE], wuk_ref[...]) + _dot_nt(dv_ref[...], wuv_ref[...])
        for dn, x, g_ref, dg_ref, sl in ((dcqn, lat_ref[:, :Q_LORA], qg_ref, dqg_ref, slice(0, Q_LORA)),
                                         (dckn, lat_ref[:, Q_LORA:], kg_ref, dkg_ref, slice(Q_LORA, lat_w))):
            xn, r = _rms(x, None)
            _acc(i, dg_ref, jnp.sum(dn * xn, axis=0, keepdims=True))
            dxn = dn * g_ref[...]
            dlat_ref[:, sl] = _bf(r * (dxn - xn * _mean(dxn * xn)))

    dqt_shape, dqt_spec = _transposed_blocks(T, tm, MLA_QK, F32)
    assert dqt.shape == dqt_shape.shape
    return _rowcall("proj_mla_bwd", body, T, tm, [dk, dv, lat, *tabs], [w_uq, w_uk, w_uv, qn_g, kvn_g],
                    [(lat_w, BF16), (LANES, BF16), (H * MLA_QK, BF16), (H * (MLA_NOPE + MLA_DV), BF16)],
                    [((1, Q_LORA), F32), ((1, KV_LORA), F32)], tiled_ins=[(dqt, dqt_spec)])


def _proj_bwd(drq, drk, drv, drg, dz1, dlat, dkpe, dgates, cos_r, sin_r, w_r, w_c, w_kpe, w_g, tm):
    T, D = dz1.shape
    qk = RET_HEADS * RET_DK
    rv = RET_HEADS * RET_DV
    o_lat = 2 * qk + 2 * rv
    o_kpe = o_lat + dlat.shape[1]
    o_gate = o_kpe + LANES
    o_end = o_gate + dgates.shape[1]
    width = -(-o_end // WG_TILE_N) * WG_TILE_N

    def body(i, drq_ref, drk_ref, drv_ref, drg_ref, dz_ref, dlat_ref, dkpe_ref, dgt_ref, cos_ref, sin_ref,
             wr_ref, wc_ref, wk_ref, wg_ref, dh_ref, dpr_ref):
        cos, sin = cos_ref[...], sin_ref[...]
        for src, off, scale in ((drq_ref, 0, 1.0), (drk_ref, qk, RET_DK ** -0.5)):
            for h in range(RET_HEADS):
                d = src[:, h * RET_DK:(h + 1) * RET_DK]
                dpr_ref[:, off + h * RET_DK:off + (h + 1) * RET_DK] = _bf(
                    (d * cos + _roll(d * sin, RET_DK // 2)) * scale)
        dpr_ref[:, 2 * qk:2 * qk + rv] = drv_ref[...]
        dpr_ref[:, 2 * qk + rv:o_lat] = drg_ref[...]
        dpr_ref[:, o_lat:o_kpe] = dlat_ref[...]
        dpr_ref[:, o_kpe:o_gate] = dkpe_ref[...]
        dpr_ref[:, o_gate:o_end] = dgt_ref[...]
        if width > o_end:
            dpr_ref[:, o_end:] = jnp.zeros((tm, width - o_end), BF16)
        dh_ref[...] = (ALPHA * dz_ref[...] + _dot_nt(dpr_ref[:, :o_lat], wr_ref[...])
                       + _dot_nt(dlat_ref[...], wc_ref[...]) + _dot_nt(dkpe_ref[...], wk_ref[...])
                       + _dot_nt(dgt_ref[...], wg_ref[...]))

    return _rowcall("proj_bwd", body, T, tm, [drq, drk, drv, drg, dz1, dlat, dkpe, dgates, cos_r, sin_r],
                    [w_r, w_c, w_kpe, w_g], [(D, F32), (width, BF16)])


def _ple_loss(h3, p, target, w_gate, w_proj, ln_g, ln_b, tm):
    T, D = h3.shape

    def body(i, h_ref, p_ref, t_ref, wg_ref, wp_ref, g_ref, b_ref, dh_ref, dgp_ref, dpp_ref, loss_ref, dg_ref, db_ref):
        hv = h_ref[...]
        sg = _sigmoid(_dot(_bf(hv), wg_ref[...]))
        pp = _dot(_bf(p_ref[...]), wp_ref[...])
        xhat, rstd = _ln_stats(ALPHA * hv + sg * pp)
        err = xhat * g_ref[...] + b_ref[...] - t_ref[...]
        row_loss = 0.5 * _mean(err * err)
        _acc(i, loss_ref, jnp.broadcast_to(jnp.sum(row_loss, axis=0, keepdims=True), (1, LANES)))
        dz, dg, db = _ln_bwd(err * (1.0 / D), xhat, rstd, g_ref[...])
        _acc(i, dg_ref, dg)
        _acc(i, db_ref, db)
        dgp = _bf(dz * pp * sg * (1.0 - sg))
        dgp_ref[...] = dgp
        dpp_ref[...] = _bf(dz * sg)
        dh_ref[...] = ALPHA * dz + _dot_nt(dgp, wg_ref[...])

    return _rowcall("ple_loss", body, T, tm, [h3, p, target], [w_gate, w_proj, ln_g, ln_b],
                    [(D, F32), (D, BF16), (D, BF16)], [((1, LANES), F32), ((1, D), F32), ((1, D), F32)])


def _ewise(name, fn, ins, n_out, out_dtype=F32):
    r, c = ins[0].shape
    tr = _tile(r, max(8, (1 << 19) // c // 8 * 8), 8)

    def kern(*refs):
        outs = fn(*[x[...] for x in refs[:len(ins)]])
        for o_ref, o in zip(refs[len(ins):], outs):
            o_ref[...] = o.astype(out_dtype)

    spec = pl.BlockSpec((tr, c), lambda i: (i, 0))
    return pl.pallas_call(kern, grid=(r // tr,), in_specs=[spec] * len(ins), out_specs=[spec] * n_out,
                          out_shape=[jax.ShapeDtypeStruct((r, c), out_dtype)] * n_out, name=name,
                          compiler_params=_params(("arbitrary",)))(*ins)


def _adamw_math(w, g, m, v):
    m = ADAM_B1 * m + (1.0 - ADAM_B1) * g
    v = ADAM_B2 * v + (1.0 - ADAM_B2) * (g * g)
    m_hat = m / (1.0 - ADAM_B1 ** ADAM_STEP)
    v_hat = v / (1.0 - ADAM_B2 ** ADAM_STEP)
    return -ADAM_LR * (m_hat / (jnp.sqrt(v_hat) + ADAM_EPS) + ADAM_WD * w), m, v


def _adamw(name, w, g, m, v):
    shape = w.shape
    c = shape[-1]
    flat = [t.reshape(-1, c) for t in (w, g, m, v)]
    return [t.reshape(shape) for t in _ewise(name, _adamw_math, flat, 3)]


def _place():
    return lax.axis_index("x"), lax.axis_index("y"), lax.axis_index("c")


def _dma_sems(n):
    return [pltpu.SemaphoreType.DMA((n,)), pltpu.SemaphoreType.DMA((n,))]


N_PEER_CHIPS = N_CHIPS - 1


def _chips_exchange(name, srcs, broadcast):
    n = len(srcs)

    def kern(*refs):
        cps = _chip_copies(refs[:n], refs[n:2 * n], refs[2 * n], refs[2 * n + 1], broadcast)
        for send, _ in cps:
            send.start()
        _wait_copies(cps)

    return pl.pallas_call(
        kern, out_shape=_exchange_shapes(srcs), in_specs=[HBM_SPEC] * n, out_specs=[HBM_SPEC] * n,
        scratch_shapes=_dma_sems(n * N_PEER_CHIPS), name=name)(*srcs)


def _exchange_shapes(srcs):
    return [jax.ShapeDtypeStruct((N_CHIPS,) + s.shape[1:], s.dtype) for s in srcs]


def _chip_copies(src_refs, out_refs, send_sems, recv_sems, broadcast):
    x, y, c = _place()
    me = 2 * x + y
    peers = [(1 - x, y), (x, 1 - y), (1 - x, 1 - y)]
    cps = []
    for j, (px, py) in enumerate(peers):
        for a, (src_ref, out_ref) in enumerate(zip(src_refs, out_refs)):
            piece = src_ref.at[c] if broadcast else src_ref.at[2 * px + py]

            def copy(slot):
                return pltpu.make_async_remote_copy(
                    src_ref=piece, dst_ref=out_ref.at[slot], send_sem=send_sems.at[a * N_PEER_CHIPS + j],
                    recv_sem=recv_sems.at[a * N_PEER_CHIPS + j], device_id=(px, py, c), device_id_type=MESH)

            cps.append((copy(me), copy(2 * px + py)))
    return cps


def _wait_copies(cps):
    for _, landing in cps:
        landing.wait_recv()
    for send, _ in cps:
        send.wait_send()


def _sibling_swap(name, srcs, mode):
    n = len(srcs)
    per = N_PEER_CHIPS if mode == "others" else 1

    def kern(*refs):
        src_refs, out_refs = refs[:n], refs[n:2 * n]
        send_sems, recv_sems = refs[2 * n:]
        x, y, c = _place()
        slots = [2 * (1 - x) + y, 2 * x + 1 - y, 2 * (1 - x) + 1 - y]
        cps = []
        for a in range(n):
            if mode == "others":
                pieces = [(src_refs[a].at[k], out_refs[a].at[k]) for k in slots]
            else:
                pieces = [(src_refs[a].at[:, 1 - c] if mode == "halves" else src_refs[a], out_refs[a])]
            for j, (src, dst) in enumerate(pieces):
                cps.append(pltpu.make_async_remote_copy(
                    src_ref=src, dst_ref=dst, send_sem=send_sems.at[a * per + j], recv_sem=recv_sems.at[a * per + j],
                    device_id=(x, y, 1 - c), device_id_type=MESH))
        for cp in cps:
            cp.start()
        for cp in cps:
            cp.wait_recv()
        for cp in cps:
            cp.wait_send()

    def out_shape(s):
        return jax.ShapeDtypeStruct((s.shape[0],) + s.shape[2:] if mode == "halves" else s.shape, s.dtype)

    return pl.pallas_call(
        kern, out_shape=[out_shape(s) for s in srcs], in_specs=[HBM_SPEC] * n, out_specs=[HBM_SPEC] * n,
        scratch_shapes=_dma_sems(n * per), name=name)(*srcs)


def _all_devices(name, src, reduce):
    r, c = src.shape
    n_dev = 2 * N_CHIPS

    def kern(src_ref, out_ref, *scratch):
        if reduce:
            gat_ref, send_sems, recv_sems = scratch
        else:
            gat_ref = out_ref
            send_sems, recv_sems = scratch
        x, y, cc = _place()
        me = 4 * x + 2 * y + cc
        gat_ref[me] = src_ref[...]
        peers = []
        for j in range(1, n_dev):
            px = 1 - x if j & 4 else x
            py = 1 - y if j & 2 else y
            pc = 1 - cc if j & 1 else cc
            peers.append((px, py, pc))

        def copy(j, peer, slot):
            return pltpu.make_async_remote_copy(
                src_ref=src_ref, dst_ref=gat_ref.at[slot], send_sem=send_sems.at[j], recv_sem=recv_sems.at[j],
                device_id=peer, device_id_type=MESH)

        sends = [copy(j, peer, me) for j, peer in enumerate(peers)]
        for cp in sends:
            cp.start()
        for j, (px, py, pc) in enumerate(peers):
            copy(j, (px, py, pc), 4 * px + 2 * py + pc).wait_recv()
        for cp in sends:
            cp.wait_send()
        if reduce:
            total = gat_ref[0]
            for d in range(1, n_dev):
                total = total + gat_ref[d]
            out_ref[...] = total

    out_shape = jax.ShapeDtypeStruct((r, c) if reduce else (n_dev, r, c), src.dtype)
    scratch = ([pltpu.VMEM((n_dev, r, c), src.dtype)] if reduce else []) + _dma_sems(n_dev - 1)
    return pl.pallas_call(kern, out_shape=out_shape, in_specs=[VMEM_SPEC], out_specs=VMEM_SPEC,
                          scratch_shapes=scratch, name=name)(src)


def _halves(t, axis):
    return t.reshape(t.shape[:axis] + (2, t.shape[axis] // 2) + t.shape[axis + 1:])


def _by_core(mine, theirs, axis):
    c = lax.axis_index("c")
    both = jnp.where(c == 0, jnp.stack([mine, theirs], axis), jnp.stack([theirs, mine], axis))
    return both.reshape(both.shape[:axis] + (2 * both.shape[axis + 1],) + both.shape[axis + 2:])


def _with_own(own, others):
    me = 2 * lax.axis_index("x") + lax.axis_index("y")
    is_me = (jnp.arange(N_CHIPS, dtype=jnp.int32) == me)[:, None, None]
    return jnp.where(is_me, own[None], others)


def _join_shards(name, shards):
    _, r, c = shards.shape
    if name in COL_SHARDED:
        return shards.transpose(1, 0, 2).reshape(r, N_CHIPS * c)
    return shards.reshape(N_CHIPS * r, c)


def _split_shards(name, full):
    if full.ndim == 3:
        return full
    r, c = full.shape
    if name in COL_SHARDED:
        return jnp.stack([full[:, k * (c // N_CHIPS):(k + 1) * (c // N_CHIPS)] for k in range(N_CHIPS)])
    return full.reshape(N_CHIPS, r // N_CHIPS, c)


def _rope_tables(positions):
    pos = positions.reshape(-1).astype(F32)[:, None]
    half = RET_DK // 2
    ang = pos * (ROPE_BASE ** (-jnp.arange(half, dtype=F32) / half))
    cos_r = jnp.concatenate([jnp.cos(ang)] * 2, axis=1)
    sin_r = jnp.concatenate([-jnp.sin(ang), jnp.sin(ang)], axis=1)
    half = MLA_ROPE // 2
    ang = pos * (ROPE_BASE ** (-jnp.arange(half, dtype=F32) / half))
    zeros = jnp.zeros_like(ang)
    rest = LANES - MLA_ROPE
    c = jnp.concatenate([jnp.cos(ang)] * 2 + [jnp.ones((ang.shape[0], rest), F32)], axis=1)
    s1 = jnp.concatenate([-jnp.sin(ang), zeros, jnp.zeros((ang.shape[0], rest), F32)], axis=1)
    s2 = jnp.concatenate([zeros, jnp.sin(ang), jnp.zeros((ang.shape[0], rest), F32)], axis=1)
    return cos_r, sin_r, (c, s1, s2)


GATHER_GROUPS = (("ffn1_w_in", "ffn1_w_out"), ("w_in", "w_uq", "w_ukv"),
                 ("w_ret_o", "w_mla_o", "w_out", "ffn2_w_in", "ffn2_w_out", "ple_w_gate", "ple_w_proj"))
REDUCE_GROUPS = (("ple_w_gate", "ple_w_proj", "ffn2_w_in", "ffn2_w_out"),
                 ("w_out", "w_ret_o", "w_mla_o", "w_uq", "w_ukv", "w_in"), ("ffn1_w_in",), ("ffn1_w_out",))


def _gathered(tag, names, own, mine):
    theirs = _sibling_swap("gather_cores_" + tag, mine, "others")
    out = {}
    for n, m, t in zip(names, mine, theirs):
        full = _with_own(own[n], _by_core(m, t, 1))
        out[n] = full if n in ("ffn1_w_in", "ffn2_w_in") else _join_shards(n, full)
    return out


def _chip_sums(tag, names, grads):
    halves = [_halves(_split_shards(n, grads[n]), 1) for n in names]
    theirs = _sibling_swap("reduce_cores_" + tag, halves, "halves")

    def one(n, g, t):
        k, _, r, c = g.shape
        tr = _tile(r, max(8, (1 << 17) // c // 8 * 8), 8)

        def kern(g_ref, t_ref, o_ref):
            mine = jnp.where(lax.axis_index("c") == 0, g_ref[:, 0], g_ref[:, 1])
            o_ref[...] = _bf(mine.astype(F32) + t_ref[...].astype(F32))

        spec = pl.BlockSpec((k, tr, c), lambda i: (0, i, 0))
        return pl.pallas_call(kern, grid=(r // tr,),
                              in_specs=[pl.BlockSpec((k, 2, tr, c), lambda i: (0, 0, i, 0)), spec], out_specs=spec,
                              out_shape=jax.ShapeDtypeStruct((k, r, c), BF16), name="reduce_cores_add_" + n,
                              compiler_params=_params(("arbitrary",)))(g, t)

    return [one(n, g, t) for n, g, t in zip(names, halves, theirs)]


def _block_totals(names, sums, parts):
    def one(n, s, pt):
        _, r, c = s.shape
        tr = _tile(r, max(8, (1 << 17) // c // 8 * 8), 8)

        def kern(s_ref, p_ref, o_ref):
            me = 2 * lax.axis_index("x") + lax.axis_index("y")
            terms = [jnp.where(k == me, s_ref[k], p_ref[k]).astype(F32) for k in range(N_CHIPS)]
            o_ref[...] = ((terms[0] + terms[1]) + terms[2]) + terms[3]

        spec = pl.BlockSpec((N_CHIPS, tr, c), lambda i: (0, i, 0))
        return pl.pallas_call(kern, grid=(r // tr,), in_specs=[spec, spec],
                              out_specs=pl.BlockSpec((tr, c), lambda i: (i, 0)),
                              out_shape=jax.ShapeDtypeStruct((r, c), F32), name="reduce_chips_add_" + n,
                              compiler_params=_params(("arbitrary",)))(s, pt)

    return [one(n, s, pt) for n, s, pt in zip(names, sums, parts)]


def _local_step(x, p, positions, target, shards, ln_blk, gn_g, qn_g, kvn_g):
    T, D = x.shape
    tm = min(256, T)
    H = MLA_HEADS
    qk, rv = RET_HEADS * RET_DK, RET_HEADS * RET_DV
    cos_r, sin_r, tabs = _rope_tables(positions)
    lgam = jnp.broadcast_to(jnp.log(1.0 - 2.0 ** (-5.0 - jnp.arange(RET_HEADS, dtype=F32)))[:, None, None],
                            (RET_HEADS, 1, LANES))
    own = {n: _bf(shards[n]) for n in BIG_WEIGHTS}
    to_send = [[_halves(own[n], 0) for n in names] for names in GATHER_GROUPS]

    *arrived, ln_others = _chips_exchange("gather_chips_a", to_send[0] + [jnp.stack([ln_blk, ln_blk])], True)
    ln_full = _with_own(ln_blk, ln_others).transpose(1, 0, 2).reshape(2 * N_LN, D)
    lng = [ln_full[k:k + 1] for k in range(N_LN)]
    lnb = [ln_full[N_LN + k:N_LN + k + 1] for k in range(N_LN)]

    w = _gathered("a", GATHER_GROUPS[0], own, arrived)
    h1, z0, a1, *arrived = _ffn_fwd("ffn1_fwd", x, w["ffn1_w_in"], w["ffn1_w_out"], lng[0], lnb[0], tm,
                                    exchange=(to_send[1], True))
    w.update(_gathered("b", GATHER_GROUPS[1], own, arrived))

    w_in = w["w_in"]
    o_lat, o_kpe, o_gate = 2 * qk + 2 * rv, 2 * qk + 2 * rv + Q_LORA + KV_LORA, 2 * qk + 2 * rv + Q_LORA + KV_LORA + MLA_ROPE
    w_r, w_c = w_in[:, :o_lat], w_in[:, o_lat:o_kpe]
    w_kpe = jnp.pad(w_in[:, o_kpe:o_gate], ((0, 0), (0, LANES - MLA_ROPE)))
    w_g = w_in[:, o_gate:]
    w_uq = jnp.pad(w["w_uq"].reshape(Q_LORA, H, MLA_NOPE + MLA_ROPE),
                   ((0, 0), (0, 0), (0, MLA_QK - MLA_NOPE - MLA_ROPE))).reshape(Q_LORA, H * MLA_QK)
    w_ukv = w["w_ukv"].reshape(KV_LORA, H, MLA_NOPE + MLA_DV)
    w_uk = w_ukv[:, :, :MLA_NOPE].reshape(KV_LORA, H * MLA_NOPE)
    w_uv = w_ukv[:, :, MLA_NOPE:].reshape(KV_LORA, H * MLA_DV)

    rq, rk, rvv, rg = _proj_ret(h1, w_r, cos_r, sin_r, 2 * tm)
    lat, gates, q, k, v, latn, qt, kt, vt = _proj_mla(h1, tabs, w_c, w_kpe, w_g, w_uq, w_uk, w_uv, qn_g, kvn_g, 2 * tm)
    y = _ret_fwd(rq, rk, rvv, lgam)
    o, lse_rows, *arrived = _attn_fwd(k, qt, vt, exchange=to_send[2])
    w.update(_gathered("c", GATHER_GROUPS[2], own, arrived))
    h2, z1, yret, ymla, yr, mix = _mix_fwd(y, rg, o, gates, h1, gn_g, w["w_ret_o"], w["w_mla_o"], w["w_out"],
                                           lng[1], lnb[1], 2 * tm)
    h3, z2, a2 = _ffn_fwd("ffn2_fwd", h2, w["ffn2_w_in"], w["ffn2_w_out"], lng[2], lnb[2], 2 * tm)

    dh3, dgp, dpp, loss, dg3, db3 = _ple_loss(h3, p, target, w["ple_w_gate"], w["ple_w_proj"], lng[3], lnb[3], 2 * tm)
    dh2, da2, s2, df2, dg2, db2 = _ffn_bwd("ffn2_bwd", dh3, z2, a2, w["ffn2_w_in"], w["ffn2_w_out"], lng[2], tm)
    grads = {"ple_w_gate": _mm_tn("wg_ple_gate", h3, dgp), "ple_w_proj": _mm_tn("wg_ple_proj", p, dpp),
             "ffn2_w_in": _mm_tn("wg_ffn2_in", h2, da2, n_split=N_CHIPS), "ffn2_w_out": _mm_tn("wg_ffn2_out", s2, df2)}
    sums1 = _chip_sums("1", REDUCE_GROUPS[0], grads)
    (dz1, dgates, drg, dy, do, dyret, dymla, dg1, db1, dgn, dot_, delta_rows, *parts1) = _mix_bwd(
        dh2, z1, gates, yret, ymla, y, rg, o, gn_g, w["w_ret_o"], w["w_mla_o"], w["w_out"], lng[1], tm,
        exchange=(sums1, False))
    drq = _ret_bwd_q(rq, rk, rvv, dy, lgam)
    drk, drv = _ret_bwd_kv(rq, rk, rvv, dy, lgam)
    dk, dv, dqt = _attn_bwd(q, k, v, do, qt, kt, dot_, lse_rows, delta_rows)
    dlat, dkpe, dqb, dkv, dqg, dkg = _proj_mla_bwd(dqt, dk, dv, lat, tabs, w_uq, w_uk, w_uv, qn_g, kvn_g, 2 * tm)
    dh1, dpr = _proj_bwd(drq, drk, drv, drg, dz1, dlat, dkpe, dgates, cos_r, sin_r, w_r, w_c, w_kpe, w_g, tm)
    g_uq = _mm_tn("wg_uq", latn[:, :Q_LORA], dqb).reshape(Q_LORA, H, MLA_QK)[:, :, :MLA_NOPE + MLA_ROPE]
    g_ukv = _mm_tn("wg_ukv", latn[:, Q_LORA:], dkv)
    g_uk = g_ukv[:, :H * MLA_NOPE].reshape(KV_LORA, H, MLA_NOPE)
    g_uv = g_ukv[:, H * MLA_NOPE:].reshape(KV_LORA, H, MLA_DV)
    g_in = _mm_tn("wg_in", h1, dpr)
    grads.update({
        "w_in": jnp.concatenate([g_in[:, :o_kpe + MLA_ROPE], g_in[:, o_kpe + LANES:o_kpe + LANES + 2 * D]], axis=1),
        "w_ret_o": _mm_tn("wg_ret_o", yr, dyret),
        "w_uq": g_uq.reshape(Q_LORA, H * (MLA_NOPE + MLA_ROPE)),
        "w_ukv": jnp.concatenate([g_uk, g_uv], axis=2).reshape(KV_LORA, H * (MLA_NOPE + MLA_DV)),
        "w_mla_o": _mm_tn("wg_mla_o", o, dymla),
        "w_out": _mm_tn("wg_out", mix, dz1)})
    sums2 = _chip_sums("2", REDUCE_GROUPS[1], grads)
    dx, da1, s1, df1, dg0, db0, *parts2 = _ffn_bwd("ffn1_bwd", dh1, z0, a1, w["ffn1_w_in"], w["ffn1_w_out"], lng[0], tm,
                                                   exchange=(sums2, False))
    grads["ffn1_w_in"] = _mm_tn("wg_ffn1_in", x, da1, n_split=N_CHIPS)
    sums3 = _chip_sums("3", REDUCE_GROUPS[2], grads)
    grads["ffn1_w_out"], *parts3 = _mm_tn("wg_ffn1_out", s1, df1, exchange=sums3)
    sums4 = _chip_sums("4", REDUCE_GROUPS[3], grads)
    parts4 = _chips_exchange("reduce_chips_4", sums4, False)

    names = [n for group in REDUCE_GROUPS for n in group]
    totals = _block_totals(names, sums1 + sums2 + sums3 + sums4,
                           list(parts1) + list(parts2) + list(parts3) + list(parts4))
    others = _sibling_swap("reduce_join", totals, "whole")
    reduced = {n: _by_core(t, o_, 0) for n, t, o_ in zip(names, totals, others)}
    small = {"ln_g": jnp.concatenate([dg0, dg1, dg2, dg3], axis=0), "ln_b": jnp.concatenate([db0, db1, db2, db3], axis=0),
             "ret_gn_g": dgn, "q_norm_g": dqg, "kv_norm_g": dkg}
    return loss[0, 0], dx, reduced, small


def kernel(x, p, positions, ln_g, ln_b, ffn1_w_in, ffn1_w_out, w_in, ret_gn_g, w_ret_o, q_norm_g, kv_norm_g, w_uq, w_ukv, w_mla_o, w_out, ffn2_w_in, ffn2_w_out, ple_w_gate, ple_w_proj, loss_target, m_ln_g, m_ln_b, m_ffn1_w_in, m_ffn1_w_out, m_w_in, m_ret_gn_g, m_w_ret_o, m_q_norm_g, m_kv_norm_g, m_w_uq, m_w_ukv, m_w_mla_o, m_w_out, m_ffn2_w_in, m_ffn2_w_out, m_ple_w_gate, m_ple_w_proj, v_ln_g, v_ln_b, v_ffn1_w_in, v_ffn1_w_out, v_w_in, v_ret_gn_g, v_w_ret_o, v_q_norm_g, v_kv_norm_g, v_w_uq, v_w_ukv, v_w_mla_o, v_w_out, v_ffn2_w_in, v_ffn2_w_out, v_ple_w_gate, v_ple_w_proj):
    names = ("ln_g", "ln_b", "ffn1_w_in", "ffn1_w_out", "w_in", "ret_gn_g", "w_ret_o", "q_norm_g", "kv_norm_g", "w_uq",
             "w_ukv", "w_mla_o", "w_out", "ffn2_w_in", "ffn2_w_out", "ple_w_gate", "ple_w_proj")
    weights = dict(zip(names, (ln_g, ln_b, ffn1_w_in, ffn1_w_out, w_in, ret_gn_g, w_ret_o, q_norm_g, kv_norm_g, w_uq,
                               w_ukv, w_mla_o, w_out, ffn2_w_in, ffn2_w_out, ple_w_gate, ple_w_proj)))
    m_in = dict(zip(names, (m_ln_g, m_ln_b, m_ffn1_w_in, m_ffn1_w_out, m_w_in, m_ret_gn_g, m_w_ret_o, m_q_norm_g,
                            m_kv_norm_g, m_w_uq, m_w_ukv, m_w_mla_o, m_w_out, m_ffn2_w_in, m_ffn2_w_out, m_ple_w_gate,
                            m_ple_w_proj)))
    v_in = dict(zip(names, (v_ln_g, v_ln_b, v_ffn1_w_in, v_ffn1_w_out, v_w_in, v_ret_gn_g, v_w_ret_o, v_q_norm_g,
                            v_kv_norm_g, v_w_uq, v_w_ukv, v_w_mla_o, v_w_out, v_ffn2_w_in, v_ffn2_w_out, v_ple_w_gate,
                            v_ple_w_proj)))
    chip = 2 * lax.axis_index("x") + lax.axis_index("y")
    D = x.shape[-1]
    dq = D // N_CHIPS

    shards = {n: weights[n][0] for n in BIG_WEIGHTS}
    loss, dx, big, small = _local_step(x[0], p[0, 0], positions, loss_target[0], shards,
                                       jnp.concatenate([ln_g[0], ln_b[0]], axis=0), ret_gn_g, q_norm_g, kv_norm_g)

    loss = lax.psum(loss, ("x", "y", "c"))
    small_names = ("ln_g", "ln_b", "ret_gn_g", "q_norm_g", "kv_norm_g")
    flat = jnp.concatenate([small[n].reshape(-1) for n in small_names])
    rows = -(-flat.shape[0] // LANES // 8) * 8
    flat = jnp.pad(flat, (0, rows * LANES - flat.shape[0])).reshape(rows, LANES)
    flat = _all_devices("reduce_small", flat, True).reshape(-1)
    off = 0
    for n in small_names:
        size = small[n].size
        small[n] = flat[off:off + size].reshape(small[n].shape)
        off += size
    g_out = dict(big)
    for n in ("ln_g", "ln_b"):
        g_out[n] = lax.dynamic_slice_in_dim(small[n], chip * dq, dq, axis=1)
    for n in ("ret_gn_g", "q_norm_g", "kv_norm_g"):
        g_out[n] = small[n]

    deltas, new_m, new_v = {}, {}, {}
    for n in names:
        g = g_out[n].reshape(weights[n].shape)
        g_out[n] = g
        deltas[n], new_m[n], new_v[n] = _adamw("adamw_" + n, weights[n], g, m_in[n], v_in[n])
    return (loss, dx[None], *[g_out[n] for n in names], *[deltas[n] for n in names], *[new_m[n] for n in names],
            *[new_v[n] for n in names])
```
